```python
import math
import jax, jax.numpy as jnp
from jax import lax
import numpy as np

D_MODEL = 2048
BATCH = 8
SEQ = 8192
DEPTH = 2

N_A = DEPTH // 2
N_B = DEPTH - N_A
CONV_K = 31
FFN_CONV_K = 3
D_FF = 5632
GROUPS = ((128, 1), (512, 4), (2048, 16))
N_GROUPS = len(GROUPS)
HEADS_PER_GROUP = 8
HEAD_DIM = 128
Q_WIDTH = N_GROUPS * HEADS_PER_GROUP * HEAD_DIM
O_WIDTH = HEADS_PER_GROUP * HEAD_DIM
ROT_DIM = HEAD_DIM // 4
ROPE_THETA = 500000.0
BLK = 128
EPS = 1e-6
NEG = -1e30

kernel_name = "yoco_conformer_dilated_hybrid"


def rms_norm(x, g):
    xf = x.astype(jnp.float32)
    y = xf * lax.rsqrt(jnp.mean(xf * xf, axis=-1, keepdims=True) + EPS)
    return (y * g.astype(jnp.float32)).astype(x.dtype)


def layer_norm(x, g, b):
    xf = x.astype(jnp.float32)
    mu = jnp.mean(xf, axis=-1, keepdims=True)
    var = jnp.mean(jnp.square(xf - mu), axis=-1, keepdims=True)
    y = (xf - mu) * lax.rsqrt(var + EPS)
    return (y * g.astype(jnp.float32) + b.astype(jnp.float32)).astype(x.dtype)


def modulate(x, g, shift, scale):
    return rms_norm(x, g) * (1 + scale[:, None, :]) + shift[:, None, :]


def causal_dwconv(x, w, b):
    k, ch = w.shape
    y = lax.conv_general_dilated(
        x, w[:, None, :].astype(x.dtype), window_strides=(1,), padding=[(k - 1, 0)],
        dimension_numbers=("NWC", "WIO", "NWC"), feature_group_count=ch)
    return y + b


def rope_partial(x, positions):
    inv_freq = ROPE_THETA ** (-jnp.arange(0, ROT_DIM, 2, dtype=jnp.float32) / ROT_DIM)
    ang = positions.astype(jnp.float32)[..., None] * inv_freq
    ang = ang.reshape(ang.shape[:2] + (1,) * (x.ndim - 3) + ang.shape[-1:])
    cos, sin = jnp.cos(ang), jnp.sin(ang)
    xf = x.astype(jnp.float32)
    x1 = xf[..., : ROT_DIM // 2]
    x2 = xf[..., ROT_DIM // 2: ROT_DIM]
    out = jnp.concatenate([x1 * cos - x2 * sin, x2 * cos + x1 * sin, xf[..., ROT_DIM:]], axis=-1)
    return out.astype(x.dtype)


def conformer_conv(h, pw1_w, pw1_b, dw_w, dw_b, ln_g, ln_b, pw2_w, pw2_b):
    u = h @ pw1_w + pw1_b
    a, gt = jnp.split(u, 2, axis=-1)
    u = a * jax.nn.sigmoid(gt)
    u = causal_dwconv(u, dw_w, dw_b)
    u = jax.nn.silu(layer_norm(u, ln_g, ln_b))
    return u @ pw2_w + pw2_b


def conv_ffn(h, up_w, dw_w, dw_b, down_w):
    u = h @ up_w
    gt, val = jnp.split(u, 2, axis=-1)
    gt = causal_dwconv(gt, dw_w, dw_b)
    return (jax.nn.silu(gt) * val) @ down_w


def shared_kv(x, c, kv_mod_w, kv_mod_b, kv_norm_g, w_kv, k_norm_g, positions):
    b, s, _ = x.shape
    m = jax.nn.silu(c) @ kv_mod_w + kv_mod_b
    shift, scale = jnp.split(m, 2, axis=-1)
    h = modulate(x, kv_norm_g, shift, scale)
    kv = (h @ w_kv).reshape(b, s, 2, N_GROUPS, HEADS_PER_GROUP, HEAD_DIM)
    k = rope_partial(rms_norm(kv[:, :, 0], k_norm_g), positions)
    v = kv[:, :, 1]
    return k, v


def dilated_band_attn(q, k, v, span, r):
    b, s, h, dh = q.shape
    chunk = r * BLK
    s_pad = -(-s // chunk) * chunk
    nb = s_pad // chunk
    padw = ((0, 0), (0, s_pad - s), (0, 0), (0, 0))
    split = lambda t: jnp.pad(t, padw).reshape(b, nb, BLK, r, h, dh)
    qb, kb, vb = split(q), split(k), split(v)
    shift_prev = lambda t: jnp.concatenate([jnp.zeros_like(t[:, :1]), t[:, :-1]], axis=1)
    kcat = jnp.concatenate([shift_prev(kb), kb], axis=2)
    vcat = jnp.concatenate([shift_prev(vb), vb], axis=2)
    sc = jnp.einsum("bnqrhd,bnkrhd->bnrhqk", qb, kcat,
                    preferred_element_type=jnp.float32) * (1.0 / math.sqrt(dh))
    qi = jnp.arange(BLK)[:, None]
    kj = jnp.arange(2 * BLK)[None, :]
    dist = qi + BLK - kj
    band = (dist >= 0) & (dist <= span)
    kpos = jnp.arange(nb)[:, None, None] * BLK + kj[None] - BLK
    mask = band[None] & (kpos >= 0)
    sc = jnp.where(mask[None, :, None, None], sc, NEG)
    lse = jax.nn.logsumexp(sc, axis=-1)
    p = jnp.exp(sc - lse[..., None])
    o = jnp.einsum("bnrhqk,bnkrhd->bnqrhd", p.astype(v.dtype), vcat)
    o = o.reshape(b, s_pad, h, dh)[:, :s]
    lse = lse.transpose(0, 1, 4, 2, 3).reshape(b, s_pad, h)[:, :s]
    return o, lse


def dilated_mixture_attn(h, k, v, w_q, q_norm_g, w_o, positions):
    b, s, _ = h.shape
    q = (h @ w_q).reshape(b, s, N_GROUPS, HEADS_PER_GROUP, HEAD_DIM)
    q = rope_partial(rms_norm(q, q_norm_g), positions)
    outs, lses = [], []
    for g, (window, dil) in enumerate(GROUPS):
        o_g, lse_g = dilated_band_attn(q[:, :, g], k[:, :, g], v[:, :, g], window // dil, dil)
        outs.append(o_g)
        lses.append(lse_g)
    alpha = jax.nn.softmax(jnp.stack(lses, axis=0), axis=0)
    o = jnp.einsum("gbsh,gbshd->bshd", alpha, jnp.stack(outs, 0).astype(jnp.float32))
    return o.astype(h.dtype).reshape(b, s, O_WIDTH) @ w_o


def _fwd_setup_inputs(seed: int = 0) -> dict:
    key = jax.random.key(seed)
    ks = iter(jax.random.split(key, 40))
    D = D_MODEL
    nrm = lambda shape, scale: jax.random.normal(next(ks), shape, jnp.float32) * scale
    gain = lambda shape: 1.0 + nrm(shape, 0.02)
    return {
        "x": nrm((BATCH, SEQ, D), 1.0),
        "c": nrm((BATCH, D), 1.0),
        "positions": jnp.broadcast_to(jnp.arange(SEQ, dtype=jnp.int32)[None], (BATCH, SEQ)),
        "mod_w": nrm((DEPTH, D, 6 * D), 0.5 * D ** -0.5),
        "mod_b": nrm((DEPTH, 6 * D), 0.01),
        "norm_mix_g": gain((DEPTH, D)),
        "norm_ffn_g": gain((DEPTH, D)),
        "conv_pw1_w": nrm((N_A, D, 2 * D), D ** -0.5),
        "conv_pw1_b": nrm((N_A, 2 * D), 0.01),
        "conv_dw_w": nrm((N_A, CONV_K, D), CONV_K ** -0.5),
        "conv_dw_b": nrm((N_A, D), 0.01),
        "conv_ln_g": gain((N_A, D)),
        "conv_ln_b": nrm((N_A, D), 0.01),
        "conv_pw2_w": nrm((N_A, D, D), D ** -0.5),
        "conv_pw2_b": nrm((N_A, D), 0.01),
        "kv_mod_w": nrm((D, 2 * D), 0.5 * D ** -0.5),
        "kv_mod_b": nrm((2 * D,), 0.01),
        "kv_norm_g": gain((D,)),
        "w_kv": nrm((D, 2 * Q_WIDTH), D ** -0.5),
        "k_norm_g": gain((HEAD_DIM,)),
        "w_q": nrm((N_B, D, Q_WIDTH), D ** -0.5),
        "q_norm_g": gain((N_B, HEAD_DIM)),
        "w_o": nrm((N_B, O_WIDTH, D), O_WIDTH ** -0.5),
        "ffn_up_w": nrm((DEPTH, D, 2 * D_FF), D ** -0.5),
        "ffn_dw_w": nrm((DEPTH, FFN_CONV_K, D_FF), FFN_CONV_K ** -0.5),
        "ffn_dw_b": nrm((DEPTH, D_FF), 0.01),
        "ffn_down_w": nrm((DEPTH, D_FF, D), D_FF ** -0.5),
    }


def _fwd_reference(x, c, positions, mod_w, mod_b, norm_mix_g, norm_ffn_g,
              conv_pw1_w, conv_pw1_b, conv_dw_w, conv_dw_b, conv_ln_g, conv_ln_b,
              conv_pw2_w, conv_pw2_b, kv_mod_w, kv_mod_b, kv_norm_g, w_kv, k_norm_g,
              w_q, q_norm_g, w_o, ffn_up_w, ffn_dw_w, ffn_dw_b, ffn_down_w):
    k_sh = v_sh = None
    for l in range(DEPTH):
        m = jax.nn.silu(c) @ mod_w[l] + mod_b[l]
        sh_m, sc_m, g_m, sh_f, sc_f, g_f = jnp.split(m, 6, axis=-1)
        if l < N_A:
            h = modulate(x, norm_mix_g[l], sh_m, sc_m)
            y = conformer_conv(h, conv_pw1_w[l], conv_pw1_b[l], conv_dw_w[l], conv_dw_b[l],
                               conv_ln_g[l], conv_ln_b[l], conv_pw2_w[l], conv_pw2_b[l])
        else:
            if l == N_A:
                k_sh, v_sh = shared_kv(x, c, kv_mod_w, kv_mod_b, kv_norm_g, w_kv,
                                       k_norm_g, positions)
            j = l - N_A
            h = modulate(x, norm_mix_g[l], sh_m, sc_m)
            y = dilated_mixture_attn(h, k_sh, v_sh, w_q[j], q_norm_g[j], w_o[j], positions)
        x = x + g_m[:, None, :] * y
        h = modulate(x, norm_ffn_g[l], sh_f, sc_f)
        x = x + g_f[:, None, :] * conv_ffn(h, ffn_up_w[l], ffn_dw_w[l], ffn_dw_b[l], ffn_down_w[l])
    return x


import jax as _jax
import jax.numpy as _jnp

TWIN_FORMAT = 'train_step'
FWD_PARAMS = ['x', 'c', 'positions', 'mod_w', 'mod_b', 'norm_mix_g', 'norm_ffn_g', 'conv_pw1_w', 'conv_pw1_b', 'conv_dw_w', 'conv_dw_b', 'conv_ln_g', 'conv_ln_b', 'conv_pw2_w', 'conv_pw2_b', 'kv_mod_w', 'kv_mod_b', 'kv_norm_g', 'w_kv', 'k_norm_g', 'w_q', 'q_norm_g', 'w_o', 'ffn_up_w', 'ffn_dw_w', 'ffn_dw_b', 'ffn_down_w']
TWIN_WEIGHTS = ['mod_w', 'mod_b', 'norm_mix_g', 'norm_ffn_g', 'conv_pw1_w', 'conv_pw1_b', 'conv_dw_w', 'conv_dw_b', 'conv_ln_g', 'conv_ln_b', 'conv_pw2_w', 'conv_pw2_b', 'kv_mod_w', 'kv_mod_b', 'kv_norm_g', 'w_kv', 'k_norm_g', 'w_q', 'q_norm_g', 'w_o', 'ffn_up_w', 'ffn_dw_w', 'ffn_dw_b', 'ffn_down_w']
TWIN_DIFF_INPUT = 'x'
TWIN_INPUTS = ['x', 'c', 'positions', 'mod_w', 'mod_b', 'norm_mix_g', 'norm_ffn_g', 'conv_pw1_w', 'conv_pw1_b', 'conv_dw_w', 'conv_dw_b', 'conv_ln_g', 'conv_ln_b', 'conv_pw2_w', 'conv_pw2_b', 'kv_mod_w', 'kv_mod_b', 'kv_norm_g', 'w_kv', 'k_norm_g', 'w_q', 'q_norm_g', 'w_o', 'ffn_up_w', 'ffn_dw_w', 'ffn_dw_b', 'ffn_down_w', 'loss_target', 'm_mod_w', 'm_mod_b', 'm_norm_mix_g', 'm_norm_ffn_g', 'm_conv_pw1_w', 'm_conv_pw1_b', 'm_conv_dw_w', 'm_conv_dw_b', 'm_conv_ln_g', 'm_conv_ln_b', 'm_conv_pw2_w', 'm_conv_pw2_b', 'm_kv_mod_w', 'm_kv_mod_b', 'm_kv_norm_g', 'm_w_kv', 'm_k_norm_g', 'm_w_q', 'm_q_norm_g', 'm_w_o', 'm_ffn_up_w', 'm_ffn_dw_w', 'm_ffn_dw_b', 'm_ffn_down_w', 'v_mod_w', 'v_mod_b', 'v_norm_mix_g', 'v_norm_ffn_g', 'v_conv_pw1_w', 'v_conv_pw1_b', 'v_conv_dw_w', 'v_conv_dw_b', 'v_conv_ln_g', 'v_conv_ln_b', 'v_conv_pw2_w', 'v_conv_pw2_b', 'v_kv_mod_w', 'v_kv_mod_b', 'v_kv_norm_g', 'v_w_kv', 'v_k_norm_g', 'v_w_q', 'v_q_norm_g', 'v_w_o', 'v_ffn_up_w', 'v_ffn_dw_w', 'v_ffn_dw_b', 'v_ffn_down_w']
TWIN_OUTPUTS = ['loss', 'grad_x', 'grad_mod_w', 'grad_mod_b', 'grad_norm_mix_g', 'grad_norm_ffn_g', 'grad_conv_pw1_w', 'grad_conv_pw1_b', 'grad_conv_dw_w', 'grad_conv_dw_b', 'grad_conv_ln_g', 'grad_conv_ln_b', 'grad_conv_pw2_w', 'grad_conv_pw2_b', 'grad_kv_mod_w', 'grad_kv_mod_b', 'grad_kv_norm_g', 'grad_w_kv', 'grad_k_norm_g', 'grad_w_q', 'grad_q_norm_g', 'grad_w_o', 'grad_ffn_up_w', 'grad_ffn_dw_w', 'grad_ffn_dw_b', 'grad_ffn_down_w', 'delta_mod_w', 'delta_mod_b', 'delta_norm_mix_g', 'delta_norm_ffn_g', 'delta_conv_pw1_w', 'delta_conv_pw1_b', 'delta_conv_dw_w', 'delta_conv_dw_b', 'delta_conv_ln_g', 'delta_conv_ln_b', 'delta_conv_pw2_w', 'delta_conv_pw2_b', 'delta_kv_mod_w', 'delta_kv_mod_b', 'delta_kv_norm_g', 'delta_w_kv', 'delta_k_norm_g', 'delta_w_q', 'delta_q_norm_g', 'delta_w_o', 'delta_ffn_up_w', 'delta_ffn_dw_w', 'delta_ffn_dw_b', 'delta_ffn_down_w', 'new_m_mod_w', 'new_m_mod_b', 'new_m_norm_mix_g', 'new_m_norm_ffn_g', 'new_m_conv_pw1_w', 'new_m_conv_pw1_b', 'new_m_conv_dw_w', 'new_m_conv_dw_b', 'new_m_conv_ln_g', 'new_m_conv_ln_b', 'new_m_conv_pw2_w', 'new_m_conv_pw2_b', 'new_m_kv_mod_w', 'new_m_kv_mod_b', 'new_m_kv_norm_g', 'new_m_w_kv', 'new_m_k_norm_g', 'new_m_w_q', 'new_m_q_norm_g', 'new_m_w_o', 'new_m_ffn_up_w', 'new_m_ffn_dw_w', 'new_m_ffn_dw_b', 'new_m_ffn_down_w', 'new_v_mod_w', 'new_v_mod_b', 'new_v_norm_mix_g', 'new_v_norm_ffn_g', 'new_v_conv_pw1_w', 'new_v_conv_pw1_b', 'new_v_conv_dw_w', 'new_v_conv_dw_b', 'new_v_conv_ln_g', 'new_v_conv_ln_b', 'new_v_conv_pw2_w', 'new_v_conv_pw2_b', 'new_v_kv_mod_w', 'new_v_kv_mod_b', 'new_v_kv_norm_g', 'new_v_w_kv', 'new_v_k_norm_g', 'new_v_w_q', 'new_v_q_norm_g', 'new_v_w_o', 'new_v_ffn_up_w', 'new_v_ffn_dw_w', 'new_v_ffn_dw_b', 'new_v_ffn_down_w']
TWIN_LEAF_KINDS = {'loss': 'loss', 'grad_x': 'grad_x', 'grad_mod_w': 'grad_w', 'grad_mod_b': 'grad_w', 'grad_norm_mix_g': 'grad_w', 'grad_norm_ffn_g': 'grad_w', 'grad_conv_pw1_w': 'grad_w', 'grad_conv_pw1_b': 'grad_w', 'grad_conv_dw_w': 'grad_w', 'grad_conv_dw_b': 'grad_w', 'grad_conv_ln_g': 'grad_w', 'grad_conv_ln_b': 'grad_w', 'grad_conv_pw2_w': 'grad_w', 'grad_conv_pw2_b': 'grad_w', 'grad_kv_mod_w': 'grad_w', 'grad_kv_mod_b': 'grad_w', 'grad_kv_norm_g': 'grad_w', 'grad_w_kv': 'grad_w', 'grad_k_norm_g': 'grad_w', 'grad_w_q': 'grad_w', 'grad_q_norm_g': 'grad_w', 'grad_w_o': 'grad_w', 'grad_ffn_up_w': 'grad_w', 'grad_ffn_dw_w': 'grad_w', 'grad_ffn_dw_b': 'grad_w', 'grad_ffn_down_w': 'grad_w', 'delta_mod_w': 'delta_w', 'delta_mod_b': 'delta_w', 'delta_norm_mix_g': 'delta_w', 'delta_norm_ffn_g': 'delta_w', 'delta_conv_pw1_w': 'delta_w', 'delta_conv_pw1_b': 'delta_w', 'delta_conv_dw_w': 'delta_w', 'delta_conv_dw_b': 'delta_w', 'delta_conv_ln_g': 'delta_w', 'delta_conv_ln_b': 'delta_w', 'delta_conv_pw2_w': 'delta_w', 'delta_conv_pw2_b': 'delta_w', 'delta_kv_mod_w': 'delta_w', 'delta_kv_mod_b': 'delta_w', 'delta_kv_norm_g': 'delta_w', 'delta_w_kv': 'delta_w', 'delta_k_norm_g': 'delta_w', 'delta_w_q': 'delta_w', 'delta_q_norm_g': 'delta_w', 'delta_w_o': 'delta_w', 'delta_ffn_up_w': 'delta_w', 'delta_ffn_dw_w': 'delta_w', 'delta_ffn_dw_b': 'delta_w', 'delta_ffn_down_w': 'delta_w', 'new_m_mod_w': 'new_m', 'new_m_mod_b': 'new_m', 'new_m_norm_mix_g': 'new_m', 'new_m_norm_ffn_g': 'new_m', 'new_m_conv_pw1_w': 'new_m', 'new_m_conv_pw1_b': 'new_m', 'new_m_conv_dw_w': 'new_m', 'new_m_conv_dw_b': 'new_m', 'new_m_conv_ln_g': 'new_m', 'new_m_conv_ln_b': 'new_m', 'new_m_conv_pw2_w': 'new_m', 'new_m_conv_pw2_b': 'new_m', 'new_m_kv_mod_w': 'new_m', 'new_m_kv_mod_b': 'new_m', 'new_m_kv_norm_g': 'new_m', 'new_m_w_kv': 'new_m', 'new_m_k_norm_g': 'new_m', 'new_m_w_q': 'new_m', 'new_m_q_norm_g': 'new_m', 'new_m_w_o': 'new_m', 'new_m_ffn_up_w': 'new_m', 'new_m_ffn_dw_w': 'new_m', 'new_m_ffn_dw_b': 'new_m', 'new_m_ffn_down_w': 'new_m', 'new_v_mod_w': 'new_v', 'new_v_mod_b': 'new_v', 'new_v_norm_mix_g': 'new_v', 'new_v_norm_ffn_g': 'new_v', 'new_v_conv_pw1_w': 'new_v', 'new_v_conv_pw1_b': 'new_v', 'new_v_conv_dw_w': 'new_v', 'new_v_conv_dw_b': 'new_v', 'new_v_conv_ln_g': 'new_v', 'new_v_conv_ln_b': 'new_v', 'new_v_conv_pw2_w': 'new_v', 'new_v_conv_pw2_b': 'new_v', 'new_v_kv_mod_w': 'new_v', 'new_v_kv_mod_b': 'new_v', 'new_v_kv_norm_g': 'new_v', 'new_v_w_kv': 'new_v', 'new_v_k_norm_g': 'new_v', 'new_v_w_q': 'new_v', 'new_v_q_norm_g': 'new_v', 'new_v_w_o': 'new_v', 'new_v_ffn_up_w': 'new_v', 'new_v_ffn_dw_w': 'new_v', 'new_v_ffn_dw_b': 'new_v', 'new_v_ffn_down_w': 'new_v'}


def _forward(args):
    return _fwd_reference(*[args[k] for k in FWD_PARAMS])


def _output_shape():
    def fwd():
        inp = _fwd_setup_inputs(0)
        return _fwd_reference(*[inp[k] for k in FWD_PARAMS])
    out = _jax.eval_shape(fwd)
    return out.shape, out.dtype

N_MICROBATCH = 1
ADAM_LR = 0.001
ADAM_B1 = 0.9
ADAM_B2 = 0.999
ADAM_EPS = 1e-08
ADAM_WD = 0.01
ADAM_STEP = 10
PER_EXAMPLE_BATCH_AXIS = {'x': 0, 'c': 0, 'positions': 0, 'loss_target': 0}
SHARED_INPUTS = []
_WEIGHT_DTYPES = {'mod_w': _jnp.float32, 'mod_b': _jnp.float32, 'norm_mix_g': _jnp.float32, 'norm_ffn_g': _jnp.float32, 'conv_pw1_w': _jnp.float32, 'conv_pw1_b': _jnp.float32, 'conv_dw_w': _jnp.float32, 'conv_dw_b': _jnp.float32, 'conv_ln_g': _jnp.float32, 'conv_ln_b': _jnp.float32, 'conv_pw2_w': _jnp.float32, 'conv_pw2_b': _jnp.float32, 'kv_mod_w': _jnp.float32, 'kv_mod_b': _jnp.float32, 'kv_norm_g': _jnp.float32, 'w_kv': _jnp.float32, 'k_norm_g': _jnp.float32, 'w_q': _jnp.float32, 'q_norm_g': _jnp.float32, 'w_o': _jnp.float32, 'ffn_up_w': _jnp.float32, 'ffn_dw_w': _jnp.float32, 'ffn_dw_b': _jnp.float32, 'ffn_down_w': _jnp.float32}
MOMENT_SCALE = {'mod_w': 6.135174e-01, 'mod_b': 1.661471e+00, 'norm_mix_g': 2.950189e-02, 'norm_ffn_g': 3.216289e+00, 'conv_pw1_w': 4.971136e-02, 'conv_pw1_b': 3.022598e-01, 'conv_dw_w': 7.663871e-02, 'conv_dw_b': 6.574742e-01, 'conv_ln_g': 1.349124e+00, 'conv_ln_b': 9.192088e-01, 'conv_pw2_w': 1.530053e-01, 'conv_pw2_b': 7.897011e-01, 'kv_mod_w': 1.120075e-01, 'kv_mod_b': 2.206591e-01, 'kv_norm_g': 3.013242e-02, 'w_kv': 3.868426e-02, 'k_norm_g': 2.911872e-01, 'w_q': 9.263861e-03, 'q_norm_g': 2.904974e-01, 'w_o': 5.699913e-02, 'ffn_up_w': 5.814987e-02, 'ffn_dw_w': 3.773078e-01, 'ffn_dw_b': 4.192446e-01, 'ffn_down_w': 5.480056e-02}


def _to_microbatches(a, axis):
    t = _jnp.moveaxis(a, axis, 0)
    t = t.reshape((N_MICROBATCH, t.shape[0] // N_MICROBATCH) + t.shape[1:])
    return _jnp.moveaxis(t, 1, axis + 1)


def setup_inputs(seed: int = 0) -> dict:
    inp = _fwd_setup_inputs(seed)
    key = _jax.random.fold_in(_jax.random.key(seed), 7919)
    shape, _ = _output_shape()
    out = dict(inp)
    out["loss_target"] = _jax.random.normal(_jax.random.fold_in(key, 0), shape, _jnp.float32)
    for i, name in enumerate(TWIN_WEIGHTS):
        w = inp[name].astype(_jnp.float32)
        if MOMENT_SCALE is None:
            s = _jnp.sqrt(_jnp.mean(_jnp.square(w)) + 1e-30)
        else:
            s = MOMENT_SCALE[name]
        km, kv = _jax.random.split(_jax.random.fold_in(key, i + 1))
        out[name] = w
        out["m_" + name] = s * _jax.random.normal(km, w.shape, _jnp.float32)
        out["v_" + name] = (s * s) * _jax.random.uniform(kv, w.shape, _jnp.float32, 0.5, 1.5)
    if N_MICROBATCH > 1:
        for name, axis in PER_EXAMPLE_BATCH_AXIS.items():
            out[name] = _to_microbatches(out[name], axis)
    return {'x': out['x'], 'c': out['c'], 'positions': out['positions'], 'mod_w': out['mod_w'], 'mod_b': out['mod_b'], 'norm_mix_g': out['norm_mix_g'], 'norm_ffn_g': out['norm_ffn_g'], 'conv_pw1_w': out['conv_pw1_w'], 'conv_pw1_b': out['conv_pw1_b'], 'conv_dw_w': out['conv_dw_w'], 'conv_dw_b': out['conv_dw_b'], 'conv_ln_g': out['conv_ln_g'], 'conv_ln_b': out['conv_ln_b'], 'conv_pw2_w': out['conv_pw2_w'], 'conv_pw2_b': out['conv_pw2_b'], 'kv_mod_w': out['kv_mod_w'], 'kv_mod_b': out['kv_mod_b'], 'kv_norm_g': out['kv_norm_g'], 'w_kv': out['w_kv'], 'k_norm_g': out['k_norm_g'], 'w_q': out['w_q'], 'q_norm_g': out['q_norm_g'], 'w_o': out['w_o'], 'ffn_up_w': out['ffn_up_w'], 'ffn_dw_w': out['ffn_dw_w'], 'ffn_dw_b': out['ffn_dw_b'], 'ffn_down_w': out['ffn_down_w'], 'loss_target': out['loss_target'], 'm_mod_w': out['m_mod_w'], 'm_mod_b': out['m_mod_b'], 'm_norm_mix_g': out['m_norm_mix_g'], 'm_norm_ffn_g': out['m_norm_ffn_g'], 'm_conv_pw1_w': out['m_conv_pw1_w'], 'm_conv_pw1_b': out['m_conv_pw1_b'], 'm_conv_dw_w': out['m_conv_dw_w'], 'm_conv_dw_b': out['m_conv_dw_b'], 'm_conv_ln_g': out['m_conv_ln_g'], 'm_conv_ln_b': out['m_conv_ln_b'], 'm_conv_pw2_w': out['m_conv_pw2_w'], 'm_conv_pw2_b': out['m_conv_pw2_b'], 'm_kv_mod_w': out['m_kv_mod_w'], 'm_kv_mod_b': out['m_kv_mod_b'], 'm_kv_norm_g': out['m_kv_norm_g'], 'm_w_kv': out['m_w_kv'], 'm_k_norm_g': out['m_k_norm_g'], 'm_w_q': out['m_w_q'], 'm_q_norm_g': out['m_q_norm_g'], 'm_w_o': out['m_w_o'], 'm_ffn_up_w': out['m_ffn_up_w'], 'm_ffn_dw_w': out['m_ffn_dw_w'], 'm_ffn_dw_b': out['m_ffn_dw_b'], 'm_ffn_down_w': out['m_ffn_down_w'], 'v_mod_w': out['v_mod_w'], 'v_mod_b': out['v_mod_b'], 'v_norm_mix_g': out['v_norm_mix_g'], 'v_norm_ffn_g': out['v_norm_ffn_g'], 'v_conv_pw1_w': out['v_conv_pw1_w'], 'v_conv_pw1_b': out['v_conv_pw1_b'], 'v_conv_dw_w': out['v_conv_dw_w'], 'v_conv_dw_b': out['v_conv_dw_b'], 'v_conv_ln_g': out['v_conv_ln_g'], 'v_conv_ln_b': out['v_conv_ln_b'], 'v_conv_pw2_w': out['v_conv_pw2_w'], 'v_conv_pw2_b': out['v_conv_pw2_b'], 'v_kv_mod_w': out['v_kv_mod_w'], 'v_kv_mod_b': out['v_kv_mod_b'], 'v_kv_norm_g': out['v_kv_norm_g'], 'v_w_kv': out['v_w_kv'], 'v_k_norm_g': out['v_k_norm_g'], 'v_w_q': out['v_w_q'], 'v_q_norm_g': out['v_q_norm_g'], 'v_w_o': out['v_w_o'], 'v_ffn_up_w': out['v_ffn_up_w'], 'v_ffn_dw_w': out['v_ffn_dw_w'], 'v_ffn_dw_b': out['v_ffn_dw_b'], 'v_ffn_down_w': out['v_ffn_down_w']}


def _loss(weights, diff, rest, loss_target):
    with _jax.named_scope("forward"):
        args = {**rest, TWIN_DIFF_INPUT: diff, **{k: w.astype(_WEIGHT_DTYPES[k]) for k, w in weights.items()}}
        y = _forward(args)
    with _jax.named_scope("loss_head"):
        err = _jnp.square(y.astype(_jnp.float32) - loss_target)
        return 0.5 * _jnp.sum(_jnp.mean(err, axis=-1)) if err.ndim else 0.5 * err


def _adamw(w, g, m, v):
    m = ADAM_B1 * m + (1.0 - ADAM_B1) * g
    v = ADAM_B2 * v + (1.0 - ADAM_B2) * _jnp.square(g)
    m_hat = m / (1.0 - ADAM_B1 ** ADAM_STEP)
    v_hat = v / (1.0 - ADAM_B2 ** ADAM_STEP)
    delta = -ADAM_LR * (m_hat / (_jnp.sqrt(v_hat) + ADAM_EPS) + ADAM_WD * w)
    return delta, m, v


def reference(x, c, positions, mod_w, mod_b, norm_mix_g, norm_ffn_g, conv_pw1_w, conv_pw1_b, conv_dw_w, conv_dw_b, conv_ln_g, conv_ln_b, conv_pw2_w, conv_pw2_b, kv_mod_w, kv_mod_b, kv_norm_g, w_kv, k_norm_g, w_q, q_norm_g, w_o, ffn_up_w, ffn_dw_w, ffn_dw_b, ffn_down_w, loss_target, m_mod_w, m_mod_b, m_norm_mix_g, m_norm_ffn_g, m_conv_pw1_w, m_conv_pw1_b, m_conv_dw_w, m_conv_dw_b, m_conv_ln_g, m_conv_ln_b, m_conv_pw2_w, m_conv_pw2_b, m_kv_mod_w, m_kv_mod_b, m_kv_norm_g, m_w_kv, m_k_norm_g, m_w_q, m_q_norm_g, m_w_o, m_ffn_up_w, m_ffn_dw_w, m_ffn_dw_b, m_ffn_down_w, v_mod_w, v_mod_b, v_norm_mix_g, v_norm_ffn_g, v_conv_pw1_w, v_conv_pw1_b, v_conv_dw_w, v_conv_dw_b, v_conv_ln_g, v_conv_ln_b, v_conv_pw2_w, v_conv_pw2_b, v_kv_mod_w, v_kv_mod_b, v_kv_norm_g, v_w_kv, v_k_norm_g, v_w_q, v_q_norm_g, v_w_o, v_ffn_up_w, v_ffn_dw_w, v_ffn_dw_b, v_ffn_down_w):
    given = dict(x=x, c=c, positions=positions, mod_w=mod_w, mod_b=mod_b, norm_mix_g=norm_mix_g, norm_ffn_g=norm_ffn_g, conv_pw1_w=conv_pw1_w, conv_pw1_b=conv_pw1_b, conv_dw_w=conv_dw_w, conv_dw_b=conv_dw_b, conv_ln_g=conv_ln_g, conv_ln_b=conv_ln_b, conv_pw2_w=conv_pw2_w, conv_pw2_b=conv_pw2_b, kv_mod_w=kv_mod_w, kv_mod_b=kv_mod_b, kv_norm_g=kv_norm_g, w_kv=w_kv, k_norm_g=k_norm_g, w_q=w_q, q_norm_g=q_norm_g, w_o=w_o, ffn_up_w=ffn_up_w, ffn_dw_w=ffn_dw_w, ffn_dw_b=ffn_dw_b, ffn_down_w=ffn_down_w, loss_target=loss_target, m_mod_w=m_mod_w, m_mod_b=m_mod_b, m_norm_mix_g=m_norm_mix_g, m_norm_ffn_g=m_norm_ffn_g, m_conv_pw1_w=m_conv_pw1_w, m_conv_pw1_b=m_conv_pw1_b, m_conv_dw_w=m_conv_dw_w, m_conv_dw_b=m_conv_dw_b, m_conv_ln_g=m_conv_ln_g, m_conv_ln_b=m_conv_ln_b, m_conv_pw2_w=m_conv_pw2_w, m_conv_pw2_b=m_conv_pw2_b, m_kv_mod_w=m_kv_mod_w, m_kv_mod_b=m_kv_mod_b, m_kv_norm_g=m_kv_norm_g, m_w_kv=m_w_kv, m_k_norm_g=m_k_norm_g, m_w_q=m_w_q, m_q_norm_g=m_q_norm_g, m_w_o=m_w_o, m_ffn_up_w=m_ffn_up_w, m_ffn_dw_w=m_ffn_dw_w, m_ffn_dw_b=m_ffn_dw_b, m_ffn_down_w=m_ffn_down_w, v_mod_w=v_mod_w, v_mod_b=v_mod_b, v_norm_mix_g=v_norm_mix_g, v_norm_ffn_g=v_norm_ffn_g, v_conv_pw1_w=v_conv_pw1_w, v_conv_pw1_b=v_conv_pw1_b, v_conv_dw_w=v_conv_dw_w, v_conv_dw_b=v_conv_dw_b, v_conv_ln_g=v_conv_ln_g, v_conv_ln_b=v_conv_ln_b, v_conv_pw2_w=v_conv_pw2_w, v_conv_pw2_b=v_conv_pw2_b, v_kv_mod_w=v_kv_mod_w, v_kv_mod_b=v_kv_mod_b, v_kv_norm_g=v_kv_norm_g, v_w_kv=v_w_kv, v_k_norm_g=v_k_norm_g, v_w_q=v_w_q, v_q_norm_g=v_q_norm_g, v_w_o=v_w_o, v_ffn_up_w=v_ffn_up_w, v_ffn_dw_w=v_ffn_dw_w, v_ffn_dw_b=v_ffn_dw_b, v_ffn_down_w=v_ffn_down_w)
    weights = {n: given[n] for n in TWIN_WEIGHTS}
    shared = {n: given[n] for n in SHARED_INPUTS}
    per_example = {n: given[n] for n in ['x', 'c', 'positions']}
    grad_fn = _jax.value_and_grad(_loss, argnums=(0, 1))

    def one_microbatch(ex, loss_target):
        ex = dict(ex)
        diff = ex.pop(TWIN_DIFF_INPUT)
        return grad_fn(weights, diff, {**shared, **ex}, loss_target)

    if N_MICROBATCH == 1:
        loss, (grad_w, grad_x) = one_microbatch(per_example, given["loss_target"])
    else:
        def body(carry, xs):
            loss_sum, grad_sum = carry
            l_k, (gw_k, gx_k) = one_microbatch(xs[0], xs[1])
            with _jax.named_scope("update"):
                return (loss_sum + l_k, _jax.tree.map(_jnp.add, grad_sum, gw_k)), gx_k

        init = (_jnp.zeros((), _jnp.float32), _jax.tree.map(_jnp.zeros_like, weights))
        (loss, grad_w), grad_x = _jax.lax.scan(body, init, (per_example, given["loss_target"]))
    with _jax.named_scope("update"):
        delta_w, new_m, new_v = {}, {}, {}
        for n in TWIN_WEIGHTS:
            delta_w[n], new_m[n], new_v[n] = _adamw(weights[n], grad_w[n], given["m_" + n], given["v_" + n])
    return (loss, grad_x, *[grad_w[n] for n in TWIN_WEIGHTS], *[delta_w[n] for n in TWIN_WEIGHTS],
            *[new_m[n] for n in TWIN_WEIGHTS], *[new_v[n] for n in TWIN_WEIGHTS])
```

```python
import functools
import math

import numpy as np
import jax
import jax.numpy as jnp
from jax import lax
from jax.experimental import pallas as pl
from jax.experimental.pallas import tpu as pltpu

f32 = jnp.float32
bf16 = jnp.bfloat16

D = 2048
SEQ = 8192
FF = 5632
CONV_K = 31
FFN_K = 3
HPG = 8
DH = 128
NG = 3
DILS = (1, 4, 16)
BLK = 128
ROT = 32
THETA = 500000.0
EPS = 1e-6
NEG = -1e30
NDEV = 8
HALO = 32
FHALO = 16

LR, B1, B2, AEPS, WD, STEP = 0.001, 0.9, 0.999, 1e-08, 0.01, 10

VMEM_BIG = 56 * 1024 * 1024

ARB = "arbitrary"
PAR = "parallel"
MESH = pl.DeviceIdType.MESH


def _cp(sem, vmem=None):
    return pltpu.CompilerParams(dimension_semantics=sem, vmem_limit_bytes=vmem)


def _tile(n, pref, mult=128):
    if n <= pref:
        return n
    t = (pref // mult) * mult
    while t >= mult:
        if n % t == 0:
            return t
        t -= mult
    return n


def _sigmoid(x):
    return 1.0 / (1.0 + jnp.exp(-x))


def _me():
    return lax.axis_index("x"), lax.axis_index("y"), lax.axis_index("c")


def _ag_small(x, name):
    r, c = x.shape

    def body(x_ref, out_ref, send_sems, recv_sems):
        mx, my, mc = _me()
        mine = 4 * mx + 2 * my + mc
        out_ref[mine] = x_ref[...]
        copies = []
        for k in range(1, NDEV):
            px = 1 - mx if (k >> 2) & 1 else mx
            py = 1 - my if (k >> 1) & 1 else my
            pc = 1 - mc if k & 1 else mc
            cp = pltpu.make_async_remote_copy(
                src_ref=x_ref, dst_ref=out_ref.at[mine], send_sem=send_sems.at[k - 1], recv_sem=recv_sems.at[k - 1],
                device_id=(px, py, pc), device_id_type=MESH)
            cp.start()
            copies.append((cp, 4 * px + 2 * py + pc))
        for k, (cp, peer) in enumerate(copies):
            pltpu.make_async_remote_copy(
                src_ref=x_ref, dst_ref=out_ref.at[peer], send_sem=send_sems.at[k], recv_sem=recv_sems.at[k],
                device_id=(mx, my, mc), device_id_type=MESH).wait_recv()
        for cp, _ in copies:
            cp.wait_send()

    return pl.pallas_call(
        body, name=name,
        out_shape=jax.ShapeDtypeStruct((NDEV, r, c), x.dtype),
        in_specs=[pl.BlockSpec(memory_space=pltpu.VMEM)],
        out_specs=pl.BlockSpec(memory_space=pltpu.VMEM),
        scratch_shapes=[pltpu.SemaphoreType.DMA((NDEV - 1,)), pltpu.SemaphoreType.DMA((NDEV - 1,))],
    )(x)


def _ag_big(w, axis, name):
    n = w.shape[axis]
    out_shape = list(w.shape)
    out_shape[axis] = NDEV * n

    def body(x_ref, out_ref, send_sems, recv_sems, local_sem):
        mx, my, mc = _me()
        me, sibling = (mx, my, mc), (mx, my, 1 - mc)
        chips = [(1 - mx, my), (mx, 1 - my), (1 - mx, 1 - my)]

        def blk(px, py, pc):
            start = pl.multiple_of((4 * px + 2 * py + pc) * n, n)
            if axis == 1:
                return out_ref.at[:, pl.ds(start, n), :]
            return out_ref.at[:, :, pl.ds(start, n)]

        def copy(k, block, to, src=None):
            return pltpu.make_async_remote_copy(
                src_ref=blk(*block) if src is None else src, dst_ref=blk(*block),
                send_sem=send_sems.at[k], recv_sem=recv_sems.at[k], device_id=to, device_id_type=MESH)

        mine = pltpu.make_async_copy(x_ref, blk(*me), local_sem)
        mine.start()
        first = [copy(0, me, sibling, src=x_ref)]
        first += [copy(1 + j, me, (*chip, mc), src=x_ref) for j, chip in enumerate(chips)]
        for cp in first:
            cp.start()
        passed = [copy(4 + j, (*chip, mc), sibling) for j, chip in enumerate(chips)]
        for j, chip in enumerate(chips):
            copy(1 + j, (*chip, mc), me).wait_recv()
            passed[j].start()
        copy(0, sibling, me).wait_recv()
        for j, chip in enumerate(chips):
            copy(4 + j, (*chip, 1 - mc), me).wait_recv()
        for cp in first + passed:
            cp.wait_send()
        mine.wait()

    return pl.pallas_call(
        body, name=name,
        out_shape=jax.ShapeDtypeStruct(tuple(out_shape), w.dtype),
        in_specs=[pl.BlockSpec(memory_space=pl.ANY)],
        out_specs=pl.BlockSpec(memory_space=pl.ANY),
        scratch_shapes=[pltpu.SemaphoreType.DMA((7,)), pltpu.SemaphoreType.DMA((7,)), pltpu.SemaphoreType.DMA],
    )(w)


def _rs_sibling(dwb, name):
    shp = dwb.shape[1:]

    def body(src_ref, out_ref, send_sems, recv_sems):
        mx, my, mc = _me()
        copies = []
        for p in range(4):
            cp = pltpu.make_async_remote_copy(
                src_ref=src_ref.at[2 * p + (1 - mc)], dst_ref=out_ref.at[p],
                send_sem=send_sems.at[p], recv_sem=recv_sems.at[p], device_id=(mx, my, 1 - mc), device_id_type=MESH)
            cp.start()
            copies.append(cp)
        for cp in copies:
            cp.wait_recv()
        for cp in copies:
            cp.wait_send()

    return pl.pallas_call(
        body, name=name,
        out_shape=jax.ShapeDtypeStruct((4,) + shp, dwb.dtype),
        in_specs=[pl.BlockSpec(memory_space=pl.ANY)],
        out_specs=pl.BlockSpec(memory_space=pl.ANY),
        scratch_shapes=[pltpu.SemaphoreType.DMA((4,)), pltpu.SemaphoreType.DMA((4,))],
    )(dwb)


def _rs_chips(part, name):
    shp = part.shape[1:]

    def body(src_ref, out_ref, send_sems, recv_sems):
        mx, my, mc = _me()
        chips = [(1 - mx, my), (mx, 1 - my), (1 - mx, 1 - my)]
        copies = []
        for k, (px, py) in enumerate(chips):
            cp = pltpu.make_async_remote_copy(
                src_ref=src_ref.at[2 * px + py], dst_ref=out_ref.at[k],
                send_sem=send_sems.at[k], recv_sem=recv_sems.at[k], device_id=(px, py, mc), device_id_type=MESH)
            cp.start()
            copies.append(cp)
        for cp in copies:
            cp.wait_recv()
        for cp in copies:
            cp.wait_send()

    return pl.pallas_call(
        body, name=name,
        out_shape=jax.ShapeDtypeStruct((3,) + shp, part.dtype),
        in_specs=[pl.BlockSpec(memory_space=pl.ANY)],
        out_specs=pl.BlockSpec(memory_space=pl.ANY),
        scratch_shapes=[pltpu.SemaphoreType.DMA((3,)), pltpu.SemaphoreType.DMA((3,))],
    )(part)


def _chip_partial(dwb, r1, core, name):
    _, L, A, B = dwb.shape
    ta = _tile(A, 512, 16)

    def body(c_ref, a_ref, b_ref, o_ref):
        o_ref[...] = (a_ref[...].astype(f32) + b_ref[...].astype(f32)).astype(o_ref.dtype)

    grid_spec = pltpu.PrefetchScalarGridSpec(
        num_scalar_prefetch=1, grid=(4, L, A // ta),
        in_specs=[pl.BlockSpec((None, None, ta, B), lambda p, l, i, c: (2 * p + c[0], l, i, 0)),
                  pl.BlockSpec((None, None, ta, B), lambda p, l, i, c: (p, l, i, 0))],
        out_specs=pl.BlockSpec((None, None, ta, B), lambda p, l, i, c: (p, l, i, 0)))
    return pl.pallas_call(body, name=name, grid_spec=grid_spec,
                          out_shape=jax.ShapeDtypeStruct((4, L, A, B), dwb.dtype),
                          compiler_params=_cp((PAR, PAR, PAR)))(core, dwb, r1)


def _adam_math(w, g, m, v):
    m2 = B1 * m + (1.0 - B1) * g
    v2 = B2 * v + (1.0 - B2) * (g * g)
    m_hat = m2 / (1.0 - B1 ** STEP)
    v_hat = v2 / (1.0 - B2 ** STEP)
    delta = -LR * (m_hat / (jnp.sqrt(v_hat) + AEPS) + WD * w)
    return delta, m2, v2


def _adamw_reduced(w, m, v, part, r2, chip, name):
    L, A, B = w.shape
    ta = _tile(A, 256, 8)

    def body(c_ref, w_ref, m_ref, v_ref, p_ref, r_ref, g_out, d_out, m_out, v_out):
        g = ((p_ref[...].astype(f32) + r_ref[0].astype(f32)) + r_ref[1].astype(f32)) + r_ref[2].astype(f32)
        d, m2, v2 = _adam_math(w_ref[...], g, m_ref[...], v_ref[...])
        g_out[...] = g
        d_out[...] = d
        m_out[...] = m2
        v_out[...] = v2

    wspec = pl.BlockSpec((None, ta, B), lambda l, i, c: (l, i, 0))
    grid_spec = pltpu.PrefetchScalarGridSpec(
        num_scalar_prefetch=1, grid=(L, A // ta),
        in_specs=[wspec, wspec, wspec,
                  pl.BlockSpec((None, None, ta, B), lambda l, i, c: (c[0], l, i, 0)),
                  pl.BlockSpec((3, None, ta, B), lambda l, i, c: (0, l, i, 0))],
        out_specs=[wspec, wspec, wspec, wspec])
    shp = jax.ShapeDtypeStruct((L, A, B), f32)
    return pl.pallas_call(body, name=name, grid_spec=grid_spec, out_shape=[shp, shp, shp, shp],
                          compiler_params=_cp((PAR, PAR)))(chip, w, m, v, part, r2)


def _adamw_plain(w, m, v, g, name):
    A, B = w.shape
    ta = _tile(A, 256, 8)

    def body(w_ref, m_ref, v_ref, g_ref, d_out, m_out, v_out):
        d, m2, v2 = _adam_math(w_ref[...], g_ref[...], m_ref[...], v_ref[...])
        d_out[...] = d
        m_out[...] = m2
        v_out[...] = v2

    spec = pl.BlockSpec((ta, B), lambda i: (i, 0))
    shp = jax.ShapeDtypeStruct((A, B), f32)
    return pl.pallas_call(body, name=name, grid=(A // ta,), in_specs=[spec] * 4, out_specs=[spec] * 3,
                          out_shape=[shp, shp, shp], compiler_params=_cp((PAR,)))(w, m, v, g)


def _sum8(g, name):
    _, R, C = g.shape

    def body(g_ref, o_ref):
        acc = g_ref[0]
        for j in range(1, NDEV):
            acc = acc + g_ref[j]
        o_ref[...] = acc

    return pl.pallas_call(body, name=name, out_shape=jax.ShapeDtypeStruct((R, C), f32))(g)


def _modproj(c_all, w, bias, name):
    K, N = w.shape
    tn = _tile(N, 512)

    def body(c_ref, w_ref, b_ref, o_ref):
        cc = c_ref[...]
        sc = (cc * _sigmoid(cc)).astype(bf16)
        o_ref[...] = jnp.dot(sc, w_ref[...].astype(bf16), preferred_element_type=f32) + b_ref[...]

    return pl.pallas_call(
        body, name=name, grid=(N // tn,),
        in_specs=[pl.BlockSpec((NDEV, K), lambda j: (0, 0)), pl.BlockSpec((K, tn), lambda j: (0, j)),
                  pl.BlockSpec((1, tn), lambda j: (0, j))],
        out_specs=pl.BlockSpec((NDEV, tn), lambda j: (0, j)),
        out_shape=jax.ShapeDtypeStruct((NDEV, N), f32), compiler_params=_cp((PAR,)))(c_all, w, bias)


def _modgrad(c_all_t, dm, name):
    K = c_all_t.shape[0]
    N = dm.shape[1]
    tn = _tile(N, 512)

    def body(c_ref, d_ref, o_ref):
        cc = c_ref[...]
        sc = cc * _sigmoid(cc)
        dmv = d_ref[...]
        acc = sc[:, 0:1] * dmv[0:1, :]
        for b in range(1, NDEV):
            acc = acc + sc[:, b:b + 1] * dmv[b:b + 1, :]
        o_ref[...] = acc

    return pl.pallas_call(
        body, name=name, grid=(N // tn,),
        in_specs=[pl.BlockSpec((K, NDEV), lambda j: (0, 0)), pl.BlockSpec((NDEV, tn), lambda j: (0, j))],
        out_specs=pl.BlockSpec((K, tn), lambda j: (0, j)),
        out_shape=jax.ShapeDtypeStruct((K, N), f32), compiler_params=_cp((PAR,)))(c_all_t, dm)


def _mm_nn(a, w, l, *, name, out_dtype=bf16, bias=None, res=None, gate=None, tm=1024, tn=1024, tk=2048):
    M, K = a.shape
    N = w.shape[2]
    tm, tn, tk = _tile(M, tm, 8), _tile(N, tn), _tile(K, tk)
    nk = K // tk
    epi = res is not None

    def body(*refs):
        it = iter(refs)
        a_ref, w_ref = next(it), next(it)
        b_ref = next(it) if bias is not None else None
        r_ref = next(it) if epi else None
        g_ref = next(it) if epi else None
        o_ref = next(it)
        f_ref = next(it) if epi else None
        acc = next(it)
        k = pl.program_id(2)

        @pl.when(k == 0)
        def _():
            acc[...] = jnp.zeros_like(acc)

        acc[...] += jnp.dot(a_ref[...], w_ref[...], preferred_element_type=f32)

        @pl.when(k == nk - 1)
        def _():
            y = acc[...]
            if b_ref is not None:
                y = y + b_ref[...]
            if epi:
                f_ref[...] = y.astype(f_ref.dtype)
                o_ref[...] = r_ref[...] + g_ref[...] * y
            else:
                o_ref[...] = y.astype(o_ref.dtype)

    in_specs = [pl.BlockSpec((tm, tk), lambda i, j, k: (i, k)), pl.BlockSpec((None, tk, tn), lambda i, j, k: (l, k, j))]
    args = [a, w]
    if bias is not None:
        in_specs.append(pl.BlockSpec((1, tn), lambda i, j, k: (0, j)))
        args.append(bias)
    ospec = pl.BlockSpec((tm, tn), lambda i, j, k: (i, j))
    if epi:
        in_specs += [ospec, pl.BlockSpec((1, tn), lambda i, j, k: (0, j))]
        args += [res, gate]
        out_shape = [jax.ShapeDtypeStruct((M, N), f32), jax.ShapeDtypeStruct((M, N), bf16)]
        out_specs = [ospec, ospec]
    else:
        out_shape = jax.ShapeDtypeStruct((M, N), out_dtype)
        out_specs = ospec
    return pl.pallas_call(
        body, name=name, grid=(M // tm, N // tn, nk), in_specs=in_specs, out_specs=out_specs, out_shape=out_shape,
        scratch_shapes=[pltpu.VMEM((tm, tn), f32)], compiler_params=_cp((PAR, PAR, ARB), VMEM_BIG))(*args)


def _mm_nt(a, w, l, *, name, out_dtype, tm=1024, tko=2048, tn=1024):
    planes = a.ndim == 3
    M = a.shape[-2]
    K, N = w.shape[1], w.shape[2]
    npl = a.shape[-1]
    tm, tko = _tile(M, tm, 8), _tile(K, tko)
    tn = _tile(npl, tn)
    nn = N // tn
    per_plane = npl // tn

    def body(a_ref, w_ref, o_ref, acc):
        k = pl.program_id(2)

        @pl.when(k == 0)
        def _():
            acc[...] = jnp.zeros_like(acc)

        acc[...] += lax.dot_general(a_ref[...], w_ref[...], (((1,), (1,)), ((), ())), preferred_element_type=f32)

        @pl.when(k == nn - 1)
        def _():
            o_ref[...] = acc[...].astype(o_ref.dtype)

    if planes:
        a_spec = pl.BlockSpec((None, tm, tn), lambda i, j, k: (k // per_plane, i, k % per_plane))
    else:
        a_spec = pl.BlockSpec((tm, tn), lambda i, j, k: (i, k))
    return pl.pallas_call(
        body, name=name, grid=(M // tm, K // tko, nn),
        in_specs=[a_spec, pl.BlockSpec((None, tko, tn), lambda i, j, k: (l, j, k))],
        out_specs=pl.BlockSpec((tm, tko), lambda i, j, k: (i, j)),
        out_shape=jax.ShapeDtypeStruct((M, K), out_dtype),
        scratch_shapes=[pltpu.VMEM((tm, tko), f32)], compiler_params=_cp((PAR, PAR, ARB), VMEM_BIG))(a, w)


def _mm_tn(a, b, *, name, col_sharded, tk=2048, ts=1024):
    planes = b.ndim == 3
    S, K = a.shape
    N = b.shape[-1] * (2 if planes else 1)
    ts = _tile(S, ts, 16)
    ns_steps = S // ts
    if col_sharded:
        tn = N // NDEV
        tk = _tile(K, tk)
    else:
        tn = N
        tk = _tile(K, 1408)
    per_plane = (b.shape[-1] // tn) if planes else 0

    def body(a_ref, b_ref, o_ref, acc):
        s = pl.program_id(2)

        @pl.when(s == 0)
        def _():
            acc[...] = jnp.zeros_like(acc)

        acc[...] += lax.dot_general(a_ref[...], b_ref[...], (((0,), (0,)), ((), ())), preferred_element_type=f32)

        @pl.when(s == ns_steps - 1)
        def _():
            o_ref[...] = acc[...].astype(o_ref.dtype)

    if planes:
        b_spec = pl.BlockSpec((None, ts, tn), lambda k, n, s: (n // per_plane, s, n % per_plane))
    else:
        b_spec = pl.BlockSpec((ts, tn), lambda k, n, s: (s, n))
    if col_sharded:
        out_shape = jax.ShapeDtypeStruct((NDEV, K, tn), bf16)
        out_spec = pl.BlockSpec((None, tk, tn), lambda k, n, s: (n, k, 0))
    else:
        out_shape = jax.ShapeDtypeStruct((K, N), bf16)
        out_spec = pl.BlockSpec((tk, tn), lambda k, n, s: (k, n))
    out = pl.pallas_call(
        body, name=name, grid=(K // tk, N // tn, ns_steps),
        in_specs=[pl.BlockSpec((ts, tk), lambda k, n, s: (s, k)), b_spec],
        out_specs=out_spec, out_shape=out_shape,
        scratch_shapes=[pltpu.VMEM((tk, tn), f32)], compiler_params=_cp((PAR, PAR, ARB), VMEM_BIG))(a, b)
    if not col_sharded:
        out = out.reshape(NDEV, K // NDEV, N)
    return out


def _acc_spec(w, rows=1):
    return pl.BlockSpec((rows, w), lambda i: (0, 0))


def _mod_fwd(x, g, sh, sc, name):
    S, W = x.shape
    tm = _tile(S, 256, 8)

    def body(x_ref, g_ref, sh_ref, sc_ref, h_ref):
        xv = x_ref[...]
        r = lax.rsqrt(jnp.mean(xv * xv, axis=-1, keepdims=True) + EPS)
        h_ref[...] = ((xv * r) * g_ref[...] * (1.0 + sc_ref[...]) + sh_ref[...]).astype(h_ref.dtype)

    row = pl.BlockSpec((tm, W), lambda i: (i, 0))
    return pl.pallas_call(body, name=name, grid=(S // tm,), in_specs=[row, _acc_spec(W), _acc_spec(W), _acc_spec(W)],
                          out_specs=row, out_shape=jax.ShapeDtypeStruct((S, W), bf16), compiler_params=_cp((PAR,)))(x, g, sh, sc)


def _mod_bwd(dh, x, dx_in, g, sc, name):
    S, W = x.shape
    tm = _tile(S, 256, 8)
    nt = S // tm

    def body(dh_ref, x_ref, dxi_ref, g_ref, sc_ref, dx_ref, dsh_ref, dsc_ref, dg_ref):
        i = pl.program_id(0)

        @pl.when(i == 0)
        def _():
            dsh_ref[...] = jnp.zeros_like(dsh_ref)
            dsc_ref[...] = jnp.zeros_like(dsc_ref)

        xv = x_ref[...]
        dh = dh_ref[...].astype(f32)
        r = lax.rsqrt(jnp.mean(xv * xv, axis=-1, keepdims=True) + EPS)
        n = xv * r
        dn = dh * (g_ref[...] * (1.0 + sc_ref[...]))
        dx = r * (dn - n * jnp.mean(dn * n, axis=-1, keepdims=True))
        dx_ref[...] = dxi_ref[...] + dx
        dsh_ref[...] += jnp.sum(dh, axis=0, keepdims=True)
        dsc_ref[...] += jnp.sum(dh * n, axis=0, keepdims=True)

        @pl.when(i == nt - 1)
        def _():
            a2 = dsc_ref[...]
            dg_ref[...] = a2 * (1.0 + sc_ref[...])
            dsc_ref[...] = a2 * g_ref[...]

    row = pl.BlockSpec((tm, W), lambda i: (i, 0))
    vec = jax.ShapeDtypeStruct((1, W), f32)
    return pl.pallas_call(
        body, name=name, grid=(nt,), in_specs=[row, row, row, _acc_spec(W), _acc_spec(W)],
        out_specs=[row, _acc_spec(W), _acc_spec(W), _acc_spec(W)],
        out_shape=[jax.ShapeDtypeStruct((S, W), f32), vec, vec, vec], compiler_params=_cp((ARB,)))(dh, x, dx_in, g, sc)


def _gate_bwd(dx, f, gate, name):
    S, W = dx.shape
    tm = _tile(S, 256, 16)

    def body(dx_ref, f_ref, g_ref, df_ref, dg_ref, sdf_ref):
        i = pl.program_id(0)

        @pl.when(i == 0)
        def _():
            dg_ref[...] = jnp.zeros_like(dg_ref)
            sdf_ref[...] = jnp.zeros_like(sdf_ref)

        d = dx_ref[...]
        df = g_ref[...] * d
        df_ref[...] = df.astype(df_ref.dtype)
        dg_ref[...] += jnp.sum(d * f_ref[...].astype(f32), axis=0, keepdims=True)
        sdf_ref[...] += jnp.sum(df, axis=0, keepdims=True)

    row = pl.BlockSpec((tm, W), lambda i: (i, 0))
    vec = jax.ShapeDtypeStruct((1, W), f32)
    return pl.pallas_call(
        body, name=name, grid=(S // tm,), in_specs=[row, row, _acc_spec(W)], out_specs=[row, _acc_spec(W), _acc_spec(W)],
        out_shape=[jax.ShapeDtypeStruct((S, W), bf16), vec, vec], compiler_params=_cp((ARB,)))(dx, f, gate)


def _loss_grad(y, target, name):
    S, W = y.shape
    tm = _tile(S, 256, 8)

    def body(y_ref, t_ref, dy_ref, l_ref):
        i = pl.program_id(0)

        @pl.when(i == 0)
        def _():
            l_ref[...] = jnp.zeros_like(l_ref)

        e = y_ref[...] - t_ref[...]
        dy_ref[...] = e * (1.0 / W)
        l_ref[...] += 0.5 * jnp.sum(jnp.mean(e * e, axis=-1, keepdims=True))

    row = pl.BlockSpec((tm, W), lambda i: (i, 0))
    return pl.pallas_call(
        body, name=name, grid=(S // tm,), in_specs=[row, row], out_specs=[row, pl.BlockSpec((8, 128), lambda i: (0, 0))],
        out_shape=[jax.ShapeDtypeStruct((S, W), f32), jax.ShapeDtypeStruct((8, 128), f32)],
        compiler_params=_cp((ARB,)))(y, target)


def _conv_core(u_ref, uh_ref, w_ref, b_ref, lg_ref, lb_ref, gbuf, tm, first):
    C = u_ref.shape[1] // 2
    u = u_ref[...].astype(f32)
    uh = uh_ref[...].astype(f32)
    gbuf[pl.ds(HALO, tm), :] = u[:, :C] * _sigmoid(u[:, C:])
    halo = uh[:, :C] * _sigmoid(uh[:, C:])
    gbuf[pl.ds(0, HALO), :] = jnp.where(first, 0.0, halo)
    w = w_ref[...]
    cv = jnp.zeros((tm, C), f32) + b_ref[...]
    for k in range(CONV_K):
        cv = cv + w[k:k + 1, :] * gbuf[pl.ds(HALO - (CONV_K - 1) + k, tm), :]
    mu = jnp.mean(cv, axis=-1, keepdims=True)
    xc = cv - mu
    rstd = lax.rsqrt(jnp.mean(xc * xc, axis=-1, keepdims=True) + EPS)
    z = xc * rstd
    ln = z * lg_ref[...] + lb_ref[...]
    return z, rstd, ln


def _halo_prev(tm, hb, w):
    return pl.BlockSpec((hb, w), lambda i: (jnp.maximum(i * (tm // hb) - 1, 0), 0))


def _conv_fwd(u, w, b, lg, lb, name):
    S, C2 = u.shape
    C = C2 // 2
    tm = _tile(S, 256, HALO)

    def body(u_ref, uh_ref, w_ref, b_ref, lg_ref, lb_ref, s_ref, gbuf):
        first = pl.program_id(0) == 0
        _, _, ln = _conv_core(u_ref, uh_ref, w_ref, b_ref, lg_ref, lb_ref, gbuf, tm, first)
        s_ref[...] = (ln * _sigmoid(ln)).astype(s_ref.dtype)

    return pl.pallas_call(
        body, name=name, grid=(S // tm,),
        in_specs=[pl.BlockSpec((tm, C2), lambda i: (i, 0)), _halo_prev(tm, HALO, C2), _acc_spec(C, 32),
                  _acc_spec(C), _acc_spec(C), _acc_spec(C)],
        out_specs=pl.BlockSpec((tm, C), lambda i: (i, 0)), out_shape=jax.ShapeDtypeStruct((S, C), bf16),
        scratch_shapes=[pltpu.VMEM((tm + HALO, C), f32)], compiler_params=_cp((PAR,), VMEM_BIG))(u, u, w, b, lg, lb)


def _conv_bwd1(u, ds, w, b, lg, lb, name):
    S, C2 = u.shape
    C = C2 // 2
    tm = _tile(S, 256, HALO)

    def body(u_ref, uh_ref, ds_ref, w_ref, b_ref, lg_ref, lb_ref, dcv_ref, dlg_ref, dlb_ref, ddb_ref, ddw_ref, gbuf):
        i = pl.program_id(0)

        @pl.when(i == 0)
        def _():
            dlg_ref[...] = jnp.zeros_like(dlg_ref)
            dlb_ref[...] = jnp.zeros_like(dlb_ref)
            ddb_ref[...] = jnp.zeros_like(ddb_ref)
            ddw_ref[...] = jnp.zeros_like(ddw_ref)

        z, rstd, ln = _conv_core(u_ref, uh_ref, w_ref, b_ref, lg_ref, lb_ref, gbuf, tm, i == 0)
        sg = _sigmoid(ln)
        dln = ds_ref[...].astype(f32) * (sg * (1.0 + ln * (1.0 - sg)))
        dlg_ref[...] += jnp.sum(dln * z, axis=0, keepdims=True)
        dlb_ref[...] += jnp.sum(dln, axis=0, keepdims=True)
        dz = dln * lg_ref[...]
        dcv = rstd * (dz - jnp.mean(dz, axis=-1, keepdims=True) - z * jnp.mean(dz * z, axis=-1, keepdims=True))
        dcv_ref[...] = dcv
        ddb_ref[...] += jnp.sum(dcv, axis=0, keepdims=True)
        for k in range(CONV_K):
            ddw_ref[pl.ds(k, 1), :] += jnp.sum(dcv * gbuf[pl.ds(HALO - (CONV_K - 1) + k, tm), :], axis=0, keepdims=True)

    vec = jax.ShapeDtypeStruct((1, C), f32)
    return pl.pallas_call(
        body, name=name, grid=(S // tm,),
        in_specs=[pl.BlockSpec((tm, C2), lambda i: (i, 0)), _halo_prev(tm, HALO, C2), pl.BlockSpec((tm, C), lambda i: (i, 0)),
                  _acc_spec(C, 32), _acc_spec(C), _acc_spec(C), _acc_spec(C)],
        out_specs=[pl.BlockSpec((tm, C), lambda i: (i, 0)), _acc_spec(C), _acc_spec(C), _acc_spec(C), _acc_spec(C, 32)],
        out_shape=[jax.ShapeDtypeStruct((S, C), f32), vec, vec, vec, jax.ShapeDtypeStruct((32, C), f32)],
        scratch_shapes=[pltpu.VMEM((tm + HALO, C), f32)], compiler_params=_cp((ARB,), VMEM_BIG))(u, u, ds, w, b, lg, lb)


def _conv_bwd2(dcv, u, w, name):
    S, C2 = u.shape
    C = C2 // 2
    tm = _tile(S, 256, HALO)
    nt = S // tm
    nhb = S // HALO

    def body(d_ref, dn_ref, u_ref, w_ref, du_ref, db_ref, dbuf):
        i = pl.program_id(0)

        @pl.when(i == 0)
        def _():
            db_ref[...] = jnp.zeros_like(db_ref)

        dbuf[pl.ds(0, tm), :] = d_ref[...]
        dbuf[pl.ds(tm, HALO), :] = jnp.where(i == nt - 1, 0.0, dn_ref[...])
        w = w_ref[...]
        dglu = jnp.zeros((tm, C), f32)
        for k in range(CONV_K):
            dglu = dglu + w[k:k + 1, :] * dbuf[pl.ds(CONV_K - 1 - k, tm), :]
        u = u_ref[...].astype(f32)
        a, gt = u[:, :C], u[:, C:]
        sg = _sigmoid(gt)
        da = dglu * sg
        dgt = dglu * a * sg * (1.0 - sg)
        du_ref[:, :C] = da.astype(du_ref.dtype)
        du_ref[:, C:] = dgt.astype(du_ref.dtype)
        db_ref[:, :C] += jnp.sum(da, axis=0, keepdims=True)
        db_ref[:, C:] += jnp.sum(dgt, axis=0, keepdims=True)

    return pl.pallas_call(
        body, name=name, grid=(nt,),
        in_specs=[pl.BlockSpec((tm, C), lambda i: (i, 0)),
                  pl.BlockSpec((HALO, C), lambda i: (jnp.minimum((i + 1) * (tm // HALO), nhb - 1), 0)),
                  pl.BlockSpec((tm, C2), lambda i: (i, 0)), _acc_spec(C, 32)],
        out_specs=[pl.BlockSpec((tm, C2), lambda i: (i, 0)), _acc_spec(C2)],
        out_shape=[jax.ShapeDtypeStruct((S, C2), bf16), jax.ShapeDtypeStruct((1, C2), f32)],
        scratch_shapes=[pltpu.VMEM((tm + HALO, C), f32)], compiler_params=_cp((ARB,), VMEM_BIG))(dcv, dcv, u, w)


def _ffn_gate_fwd(u2, w, b, name):
    S, F2 = u2.shape
    F = F2 // 2
    cw = _tile(F, 1408)
    ncw = F // cw
    tm = _tile(S, 256, FHALO)

    def body(g_ref, gh_ref, v_ref, w_ref, b_ref, a_ref, gbuf):
        first = pl.program_id(0) == 0
        gbuf[pl.ds(FHALO, tm), :] = g_ref[...].astype(f32)
        gbuf[pl.ds(0, FHALO), :] = jnp.where(first, 0.0, gh_ref[...].astype(f32))
        w = w_ref[...]
        gc = jnp.zeros((tm, cw), f32) + b_ref[...]
        for k in range(FFN_K):
            gc = gc + w[k:k + 1, :] * gbuf[pl.ds(FHALO - (FFN_K - 1) + k, tm), :]
        a_ref[...] = (gc * _sigmoid(gc) * v_ref[...].astype(f32)).astype(a_ref.dtype)

    return pl.pallas_call(
        body, name=name, grid=(S // tm, ncw),
        in_specs=[pl.BlockSpec((tm, cw), lambda i, j: (i, j)),
                  pl.BlockSpec((FHALO, cw), lambda i, j: (jnp.maximum(i * (tm // FHALO) - 1, 0), j)),
                  pl.BlockSpec((tm, cw), lambda i, j: (i, ncw + j)),
                  pl.BlockSpec((8, cw), lambda i, j: (0, j)), pl.BlockSpec((1, cw), lambda i, j: (0, j))],
        out_specs=pl.BlockSpec((tm, cw), lambda i, j: (i, j)), out_shape=jax.ShapeDtypeStruct((S, F), bf16),
        scratch_shapes=[pltpu.VMEM((tm + FHALO, cw), f32)], compiler_params=_cp((PAR, PAR)))(u2, u2, u2, w, b)


def _ffn_gate_bwd(u2, dact, w, b, name):
    S, F2 = u2.shape
    F = F2 // 2
    cw = _tile(F, 1408)
    ncw = F // cw
    tm = _tile(S, 256, FHALO)
    nt = S // tm
    nhb = S // FHALO
    R = tm + 2 * FHALO

    def body(g_ref, gp_ref, gn_ref, v_ref, vn_ref, d_ref, dn_ref, w_ref, b_ref, du_ref, dw_ref, db_ref, gbuf, dbuf):
        i = pl.program_id(1)
        first, last = i == 0, i == nt - 1

        @pl.when(i == 0)
        def _():
            dw_ref[...] = jnp.zeros_like(dw_ref)
            db_ref[...] = jnp.zeros_like(db_ref)

        gbuf[pl.ds(0, FHALO), :] = jnp.where(first, 0.0, gp_ref[...].astype(f32))
        gbuf[pl.ds(FHALO, tm), :] = g_ref[...].astype(f32)
        gbuf[pl.ds(FHALO + tm, FHALO), :] = gn_ref[...].astype(f32)
        w = w_ref[...]
        n_ext = tm + FHALO
        gc = jnp.zeros((n_ext, cw), f32) + b_ref[...]
        for k in range(FFN_K):
            gc = gc + w[k:k + 1, :] * gbuf[pl.ds(FHALO - (FFN_K - 1) + k, n_ext), :]
        sg = _sigmoid(gc)
        val = jnp.concatenate([v_ref[...].astype(f32), vn_ref[...].astype(f32)], axis=0)
        dact_ext = jnp.concatenate([d_ref[...].astype(f32), jnp.where(last, 0.0, dn_ref[...].astype(f32))], axis=0)
        dgc = dact_ext * val * (sg * (1.0 + gc * (1.0 - sg)))
        dbuf[...] = dgc
        dval = dact_ext[:tm] * (gc[:tm] * sg[:tm])
        dgt = jnp.zeros((tm, cw), f32)
        for k in range(FFN_K):
            dgt = dgt + w[k:k + 1, :] * dbuf[pl.ds(FFN_K - 1 - k, tm), :]
        du_ref[0] = dgt.astype(du_ref.dtype)
        du_ref[1] = dval.astype(du_ref.dtype)
        dgc_t = dgc[:tm]
        db_ref[...] += jnp.sum(dgc_t, axis=0, keepdims=True)
        for k in range(FFN_K):
            dw_ref[pl.ds(k, 1), :] += jnp.sum(dgc_t * gbuf[pl.ds(FHALO - (FFN_K - 1) + k, tm), :], axis=0, keepdims=True)

    hb = tm // FHALO
    prev = lambda j, i: (jnp.maximum(i * hb - 1, 0), j)
    nxt = lambda j, i: (jnp.minimum((i + 1) * hb, nhb - 1), j)
    nxt_v = lambda j, i: (jnp.minimum((i + 1) * hb, nhb - 1), ncw + j)
    return pl.pallas_call(
        body, name=name, grid=(ncw, nt),
        in_specs=[pl.BlockSpec((tm, cw), lambda j, i: (i, j)), pl.BlockSpec((FHALO, cw), prev), pl.BlockSpec((FHALO, cw), nxt),
                  pl.BlockSpec((tm, cw), lambda j, i: (i, ncw + j)), pl.BlockSpec((FHALO, cw), nxt_v),
                  pl.BlockSpec((tm, cw), lambda j, i: (i, j)), pl.BlockSpec((FHALO, cw), nxt),
                  pl.BlockSpec((8, cw), lambda j, i: (0, j)), pl.BlockSpec((1, cw), lambda j, i: (0, j))],
        out_specs=[pl.BlockSpec((2, tm, cw), lambda j, i: (0, i, j)), pl.BlockSpec((8, cw), lambda j, i: (0, j)),
                   pl.BlockSpec((1, cw), lambda j, i: (0, j))],
        out_shape=[jax.ShapeDtypeStruct((2, S, F), bf16), jax.ShapeDtypeStruct((8, F), f32), jax.ShapeDtypeStruct((1, F), f32)],
        scratch_shapes=[pltpu.VMEM((R, cw), f32), pltpu.VMEM((tm + FHALO, cw), f32)],
        compiler_params=_cp((PAR, ARB), VMEM_BIG))(u2, u2, u2, u2, u2, dact, dact, w, b)


def _rope_tables(pos_col, name):
    S = pos_col.shape[0]
    tm = _tile(S, 512, 8)
    half = ROT // 2
    inv = THETA ** (-np.arange(0, ROT, 2, dtype=np.float32) / ROT)
    lane_freq = np.zeros((1, DH), np.float32)
    lane_freq[0, :half] = inv
    lane_freq[0, half:ROT] = inv
    lane_freq = jnp.asarray(lane_freq)

    def body(p_ref, fr_ref, c_ref, sa_ref, sb_ref):
        ang = p_ref[...].astype(f32) * fr_ref[...]
        lane = lax.broadcasted_iota(jnp.int32, (tm, DH), 1)
        cs, sn = jnp.cos(ang), jnp.sin(ang)
        c_ref[...] = jnp.where(lane < ROT, cs, 1.0)
        sa_ref[...] = jnp.where(lane < half, -sn, 0.0)
        sb_ref[...] = jnp.where((lane >= half) & (lane < ROT), sn, 0.0)

    row = pl.BlockSpec((tm, DH), lambda i: (i, 0))
    shp = jax.ShapeDtypeStruct((S, DH), f32)
    return pl.pallas_call(body, name=name, grid=(S // tm,),
                          in_specs=[pl.BlockSpec((tm, 1), lambda i: (i, 0)), pl.BlockSpec((1, DH), lambda i: (0, 0))],
                          out_specs=[row, row, row], out_shape=[shp, shp, shp], compiler_params=_cp((PAR,)))(pos_col, lane_freq)


def _rope(n, c, sa, sb):
    return n * c + pltpu.roll(n, DH - ROT // 2, 1) * sa + pltpu.roll(n, ROT // 2, 1) * sb


def _rope_t(d, c, sa, sb):
    return d * c + pltpu.roll(d * sa, ROT // 2, 1) + pltpu.roll(d * sb, DH - ROT // 2, 1)


def _qk_fwd(raw, g, tabs, width, name):
    S = raw.shape[0]
    nh = width // DH
    tm = _tile(S, 256, 16)

    def body(x_ref, g_ref, c_ref, sa_ref, sb_ref, o_ref):
        c, sa, sb = c_ref[...], sa_ref[...], sb_ref[...]
        for h in range(nh):
            xv = x_ref[:, h * DH:(h + 1) * DH].astype(f32)
            r = lax.rsqrt(jnp.mean(xv * xv, axis=-1, keepdims=True) + EPS)
            o_ref[:, h * DH:(h + 1) * DH] = _rope(xv * r * g_ref[...], c, sa, sb).astype(o_ref.dtype)

    row = pl.BlockSpec((tm, width), lambda i: (i, 0))
    tab = pl.BlockSpec((tm, DH), lambda i: (i, 0))
    return pl.pallas_call(body, name=name, grid=(S // tm,), in_specs=[row, _acc_spec(DH), tab, tab, tab], out_specs=row,
                          out_shape=jax.ShapeDtypeStruct((S, width), bf16), compiler_params=_cp((PAR,)))(raw, g, *tabs)


def _qk_bwd(dparts, raw, g, tabs, width, extra, name):
    S = raw.shape[0]
    nh = width // DH
    ow = dparts[0].shape[1]
    hpg = ow // DH
    tm = _tile(S, 256, 16)
    wout = width + (len(extra) * ow if extra else 0)

    def body(*refs):
        d_refs = refs[:NG]
        x_ref, g_ref, c_ref, sa_ref, sb_ref = refs[NG:NG + 5]
        e_refs = refs[NG + 5:NG + 5 + len(extra)]
        o_ref, dg_ref = refs[NG + 5 + len(extra):]
        i = pl.program_id(0)

        @pl.when(i == 0)
        def _():
            dg_ref[...] = jnp.zeros_like(dg_ref)

        c, sa, sb = c_ref[...], sa_ref[...], sb_ref[...]
        gv = g_ref[...]
        dg = jnp.zeros((1, DH), f32)
        for h in range(nh):
            dout = d_refs[h // hpg][:, (h % hpg) * DH:(h % hpg + 1) * DH].astype(f32)
            xv = x_ref[:, h * DH:(h + 1) * DH].astype(f32)
            r = lax.rsqrt(jnp.mean(xv * xv, axis=-1, keepdims=True) + EPS)
            xh = xv * r
            dn = _rope_t(dout, c, sa, sb)
            dg = dg + jnp.sum(dn * xh, axis=0, keepdims=True)
            dxn = dn * gv
            dx = r * (dxn - xh * jnp.mean(dxn * xh, axis=-1, keepdims=True))
            o_ref[:, h * DH:(h + 1) * DH] = dx.astype(o_ref.dtype)
        for e, e_ref in enumerate(e_refs):
            o_ref[:, width + e * ow:width + (e + 1) * ow] = e_ref[...]
        dg_ref[...] += dg

    part = pl.BlockSpec((tm, ow), lambda i: (i, 0))
    tab = pl.BlockSpec((tm, DH), lambda i: (i, 0))
    return pl.pallas_call(
        body, name=name, grid=(S // tm,),
        in_specs=[part] * NG + [pl.BlockSpec((tm, width), lambda i: (i, 0)), _acc_spec(DH), tab, tab, tab] + [part] * len(extra),
        out_specs=[pl.BlockSpec((tm, wout), lambda i: (i, 0)), _acc_spec(DH)],
        out_shape=[jax.ShapeDtypeStruct((S, wout), bf16), jax.ShapeDtypeStruct((1, DH), f32)],
        compiler_params=_cp((ARB,)))(*dparts, raw, g, *tabs, *extra)


def _dot_nt(a, b):
    return lax.dot_general(a, b, (((1,), (1,)), ((), ())), preferred_element_type=f32)


def _dot_tn(a, b):
    return lax.dot_general(a, b, (((0,), (0,)), ((), ())), preferred_element_type=f32)


def _band_masks():
    qi = lax.broadcasted_iota(jnp.int32, (BLK, BLK), 0)
    ki = lax.broadcasted_iota(jnp.int32, (BLK, BLK), 1)
    return ki <= qi, ki >= qi


def _attn_fwd(q, k, kv, g, r, name):
    S = q.shape[0]
    ow = q.shape[1] // NG
    hpg = ow // DH
    sr = S // r
    nb = sr // BLK
    scale = 1.0 / math.sqrt(DH)

    def body(q_ref, kc_ref, kp_ref, vc_ref, vp_ref, o_ref, l_ref):
        n = pl.program_id(1)
        m_cur, m_prev = _band_masks()
        m_prev = m_prev & (n > 0)
        for h in range(hpg):
            hs = slice(h * DH, (h + 1) * DH)
            qh = q_ref[:, hs]
            s_c = jnp.where(m_cur, _dot_nt(qh, kc_ref[:, hs]) * scale, NEG)
            s_p = jnp.where(m_prev, _dot_nt(qh, kp_ref[:, hs]) * scale, NEG)
            mx = jnp.maximum(jnp.max(s_c, axis=-1, keepdims=True), jnp.max(s_p, axis=-1, keepdims=True))
            p_c = jnp.exp(s_c - mx)
            p_p = jnp.exp(s_p - mx)
            den = jnp.sum(p_c, axis=-1, keepdims=True) + jnp.sum(p_p, axis=-1, keepdims=True)
            o = jnp.dot(p_c.astype(bf16), vc_ref[:, hs], preferred_element_type=f32)
            o = o + jnp.dot(p_p.astype(bf16), vp_ref[:, hs], preferred_element_type=f32)
            o_ref[:, hs] = (o / den).astype(o_ref.dtype)
            l_ref[:, hs] = jnp.broadcast_to(mx + jnp.log(den), (BLK, DH))

    qv = q.reshape(sr, r * NG * ow)
    kview = k.reshape(sr, r * NG * ow)
    vview = kv.reshape(sr, r * 2 * NG * ow)
    cur = lambda j, n: (n, j * NG + g)
    prev = lambda j, n: (jnp.maximum(n - 1, 0), j * NG + g)
    vcur = lambda j, n: (n, j * 2 * NG + NG + g)
    vprev = lambda j, n: (jnp.maximum(n - 1, 0), j * 2 * NG + NG + g)
    blk = lambda f: pl.BlockSpec((BLK, ow), f)
    o, lse = pl.pallas_call(
        body, name=name, grid=(r, nb), in_specs=[blk(cur), blk(cur), blk(prev), blk(vcur), blk(vprev)],
        out_specs=[blk(lambda j, n: (n, j)), blk(lambda j, n: (n, j))],
        out_shape=[jax.ShapeDtypeStruct((sr, r * ow), bf16), jax.ShapeDtypeStruct((sr, r * ow), f32)],
        compiler_params=_cp((PAR, PAR)))(qv, kview, kview, vview, vview)
    return o.reshape(S, ow), lse.reshape(S, ow)


def _attn_bwd_q(q, k, kv, do_g, lse, corr, g, r, name):
    S = q.shape[0]
    ow = q.shape[1] // NG
    hpg = ow // DH
    sr = S // r
    nb = sr // BLK
    scale = 1.0 / math.sqrt(DH)

    def body(q_ref, kc_ref, kp_ref, vc_ref, vp_ref, do_ref, l_ref, c_ref, dq_ref):
        n = pl.program_id(1)
        m_cur, m_prev = _band_masks()
        m_prev = m_prev & (n > 0)
        for h in range(hpg):
            hs = slice(h * DH, (h + 1) * DH)
            qh, doh = q_ref[:, hs], do_ref[:, hs]
            ls = slice(h * DH, h * DH + BLK)
            lh, ch = l_ref[:, ls], c_ref[:, ls]
            dq = jnp.zeros((BLK, DH), f32)
            for k_ref, v_ref, msk in ((kc_ref, vc_ref, m_cur), (kp_ref, vp_ref, m_prev)):
                kh = k_ref[:, hs]
                s = jnp.where(msk, _dot_nt(qh, kh) * scale, NEG)
                p = jnp.exp(s - lh)
                dsc = p * (_dot_nt(doh, v_ref[:, hs]) + ch)
                dq = dq + jnp.dot(dsc.astype(bf16), kh, preferred_element_type=f32)
            dq_ref[:, hs] = (dq * scale).astype(dq_ref.dtype)

    qv = q.reshape(sr, r * NG * ow)
    kview = k.reshape(sr, r * NG * ow)
    vview = kv.reshape(sr, r * 2 * NG * ow)
    cur = lambda j, n: (n, j * NG + g)
    prev = lambda j, n: (jnp.maximum(n - 1, 0), j * NG + g)
    vcur = lambda j, n: (n, j * 2 * NG + NG + g)
    vprev = lambda j, n: (jnp.maximum(n - 1, 0), j * 2 * NG + NG + g)
    own = lambda j, n: (n, j)
    blk = lambda f: pl.BlockSpec((BLK, ow), f)
    dq = pl.pallas_call(
        body, name=name, grid=(r, nb),
        in_specs=[blk(cur), blk(cur), blk(prev), blk(vcur), blk(vprev), blk(own), blk(own), blk(own)],
        out_specs=blk(own), out_shape=jax.ShapeDtypeStruct((sr, r * ow), bf16),
        compiler_params=_cp((PAR, PAR)))(qv, kview, kview, vview, vview, do_g.reshape(sr, r * ow), lse.reshape(sr, r * ow),
                                         corr.reshape(sr, r * ow))
    return dq.reshape(S, ow)


def _attn_bwd_kv(q, k, kv, do_g, lse, corr, g, r, name):
    S = q.shape[0]
    ow = q.shape[1] // NG
    hpg = ow // DH
    sr = S // r
    nb = sr // BLK
    scale = 1.0 / math.sqrt(DH)

    def body(k_ref, v_ref, qc_ref, qn_ref, doc_ref, don_ref, lc_ref, ln_ref, cc_ref, cn_ref, dk_ref, dv_ref):
        n = pl.program_id(1)
        m_cur, m_prev = _band_masks()
        m_next = m_prev & (n < nb - 1)
        for h in range(hpg):
            hs = slice(h * DH, (h + 1) * DH)
            ls = slice(h * DH, h * DH + BLK)
            kh, vh = k_ref[:, hs], v_ref[:, hs]
            dk = jnp.zeros((BLK, DH), f32)
            dv = jnp.zeros((BLK, DH), f32)
            for q_ref, do_ref, l_ref, c_ref, msk in ((qc_ref, doc_ref, lc_ref, cc_ref, m_cur),
                                                     (qn_ref, don_ref, ln_ref, cn_ref, m_next)):
                qh, doh = q_ref[:, hs], do_ref[:, hs]
                s = jnp.where(msk, _dot_nt(qh, kh) * scale, NEG)
                p = jnp.exp(s - l_ref[:, ls])
                dv = dv + _dot_tn(p.astype(bf16), doh)
                dsc = p * (_dot_nt(doh, vh) + c_ref[:, ls])
                dk = dk + _dot_tn(dsc.astype(bf16), qh)
            dk_ref[:, hs] = (dk * scale).astype(dk_ref.dtype)
            dv_ref[:, hs] = dv.astype(dv_ref.dtype)

    qv = q.reshape(sr, r * NG * ow)
    kview = k.reshape(sr, r * NG * ow)
    vview = kv.reshape(sr, r * 2 * NG * ow)
    cur = lambda j, n: (n, j * NG + g)
    nxt = lambda j, n: (jnp.minimum(n + 1, nb - 1), j * NG + g)
    vcur = lambda j, n: (n, j * 2 * NG + NG + g)
    own = lambda j, n: (n, j)
    ownn = lambda j, n: (jnp.minimum(n + 1, nb - 1), j)
    blk = lambda f: pl.BlockSpec((BLK, ow), f)
    dov, lv, cv = do_g.reshape(sr, r * ow), lse.reshape(sr, r * ow), corr.reshape(sr, r * ow)
    shp = jax.ShapeDtypeStruct((sr, r * ow), bf16)
    dk, dv = pl.pallas_call(
        body, name=name, grid=(r, nb),
        in_specs=[blk(cur), blk(vcur), blk(cur), blk(nxt), blk(own), blk(ownn), blk(own), blk(ownn), blk(own), blk(ownn)],
        out_specs=[blk(own), blk(own)], out_shape=[shp, shp],
        compiler_params=_cp((PAR, PAR)))(kview, vview, qv, qv, dov, dov, lv, lv, cv, cv)
    return dk.reshape(S, ow), dv.reshape(S, ow)


def _mix_weights(l_refs):
    ls = [l[...] for l in l_refs]
    mx = functools.reduce(jnp.maximum, ls)
    es = [jnp.exp(l - mx) for l in ls]
    den = functools.reduce(lambda a, b: a + b, es)
    return [e / den for e in es]


def _combine_fwd(os_, lses, name):
    S, ow = os_[0].shape
    tm = _tile(S, 256, 16)

    def body(*refs):
        o_refs, l_refs, out_ref = refs[:NG], refs[NG:2 * NG], refs[2 * NG]
        al = _mix_weights(l_refs)
        acc = al[0] * o_refs[0][...].astype(f32)
        for gi in range(1, NG):
            acc = acc + al[gi] * o_refs[gi][...].astype(f32)
        out_ref[...] = acc.astype(out_ref.dtype)

    row = pl.BlockSpec((tm, ow), lambda i: (i, 0))
    return pl.pallas_call(body, name=name, grid=(S // tm,), in_specs=[row] * (2 * NG), out_specs=row,
                          out_shape=jax.ShapeDtypeStruct((S, ow), bf16), compiler_params=_cp((PAR,)))(*os_, *lses)


def _combine_bwd(do, os_, lses, name):
    S, ow = do.shape
    hpg = ow // DH
    tm = _tile(S, 256, 16)

    def body(*refs):
        do_ref = refs[0]
        o_refs, l_refs = refs[1:1 + NG], refs[1 + NG:1 + 2 * NG]
        dog_refs, c_refs = refs[1 + 2 * NG:1 + 3 * NG], refs[1 + 3 * NG:]
        al = _mix_weights(l_refs)
        dov = do_ref[...]
        o = al[0] * o_refs[0][...].astype(f32)
        for gi in range(1, NG):
            o = o + al[gi] * o_refs[gi][...].astype(f32)
        prod = dov * o
        t = jnp.concatenate(
            [jnp.broadcast_to(jnp.sum(prod[:, h * DH:(h + 1) * DH], axis=-1, keepdims=True), (tm, DH)) for h in range(hpg)],
            axis=1)
        for gi in range(NG):
            dog_refs[gi][...] = (al[gi] * dov).astype(dog_refs[gi].dtype)
            c_refs[gi][...] = -(al[gi] * t)

    row = pl.BlockSpec((tm, ow), lambda i: (i, 0))
    return pl.pallas_call(
        body, name=name, grid=(S // tm,), in_specs=[row] * (1 + 2 * NG), out_specs=[row] * (2 * NG),
        out_shape=[jax.ShapeDtypeStruct((S, ow), bf16)] * NG + [jax.ShapeDtypeStruct((S, ow), f32)] * NG,
        compiler_params=_cp((PAR,)))(do, *os_, *lses)


def _pad_rows(w, rows):
    return jnp.concatenate([w, jnp.zeros((rows - w.shape[0], w.shape[1]), w.dtype)], axis=0)


def _reduce_and_update(dwb, w, m, v, core, chip, tag):
    r1 = _rs_sibling(dwb, f"rs_d2d_{tag}")
    part = _chip_partial(dwb, r1, core, f"rs_add_{tag}")
    r2 = _rs_chips(part, f"rs_ici_{tag}")
    return _adamw_reduced(w, m, v, part, r2, chip, f"adamw_{tag}")


def kernel(x, c, positions, mod_w, mod_b, norm_mix_g, norm_ffn_g, conv_pw1_w, conv_pw1_b, conv_dw_w, conv_dw_b, conv_ln_g, conv_ln_b, conv_pw2_w, conv_pw2_b, kv_mod_w, kv_mod_b, kv_norm_g, w_kv, k_norm_g, w_q, q_norm_g, w_o, ffn_up_w, ffn_dw_w, ffn_dw_b, ffn_down_w, loss_target, m_mod_w, m_mod_b, m_norm_mix_g, m_norm_ffn_g, m_conv_pw1_w, m_conv_pw1_b, m_conv_dw_w, m_conv_dw_b, m_conv_ln_g, m_conv_ln_b, m_conv_pw2_w, m_conv_pw2_b, m_kv_mod_w, m_kv_mod_b, m_kv_norm_g, m_w_kv, m_k_norm_g, m_w_q, m_q_norm_g, m_w_o, m_ffn_up_w, m_ffn_dw_w, m_ffn_dw_b, m_ffn_down_w, v_mod_w, v_mod_b, v_norm_mix_g, v_norm_ffn_g, v_conv_pw1_w, v_conv_pw1_b, v_conv_dw_w, v_conv_dw_b, v_conv_ln_g, v_conv_ln_b, v_conv_pw2_w, v_conv_pw2_b, v_kv_mod_w, v_kv_mod_b, v_kv_norm_g, v_w_kv, v_k_norm_g, v_w_q, v_q_norm_g, v_w_o, v_ffn_up_w, v_ffn_dw_w, v_ffn_dw_b, v_ffn_down_w):
    S, Dm = x.shape[1], x.shape[2]
    F = ffn_dw_b.shape[1]
    QW = NG * HPG * DH
    OW = HPG * DH
    mx, my, mc = _me()
    me = 4 * mx + 2 * my + mc
    core = jnp.reshape(mc, (1,)).astype(jnp.int32)
    chip = jnp.reshape(2 * mx + my, (1,)).astype(jnp.int32)
    x0 = x.reshape(S, Dm)
    target = loss_target.reshape(S, Dm)

    c_all = _ag_small(c, "ag_c").reshape(NDEV, Dm)
    n_mod = mod_w.shape[2]
    n_kvm = kv_mod_w.shape[1]
    b0 = lax.dynamic_slice(mod_b, (0, me * n_mod), (1, n_mod))
    b1 = lax.dynamic_slice(mod_b, (1, me * n_mod), (1, n_mod))
    bk = lax.dynamic_slice(kv_mod_b.reshape(1, -1), (0, me * n_kvm), (1, n_kvm))
    m_part = jnp.concatenate([_modproj(c_all, mod_w[0], b0, "modproj0"), _modproj(c_all, mod_w[1], b1, "modproj1"),
                              _modproj(c_all, kv_mod_w, bk, "modproj_kv")], axis=1)
    m_all = _ag_small(m_part, "ag_mod")
    m_mine = lax.dynamic_index_in_dim(m_all, me, axis=1, keepdims=False)
    mod0 = m_mine[:, :n_mod].reshape(6, Dm)
    mod1 = m_mine[:, n_mod:2 * n_mod].reshape(6, Dm)
    modkv = m_mine[:, 2 * n_mod:].reshape(2, Dm)
    row = lambda a, i: a[i:i + 1]

    as3 = lambda w: w if w.ndim == 3 else w[None]
    gw = lambda w, axis, tag: _ag_big(as3(w).astype(bf16), axis, f"ag_{tag}")
    W_pw1 = gw(conv_pw1_w, 2, "pw1")
    W_pw2 = gw(conv_pw2_w, 1, "pw2")
    W_up = gw(ffn_up_w, 2, "up")
    W_down = gw(ffn_down_w, 1, "down")
    W_kv = gw(w_kv, 2, "wkv")
    W_q = gw(w_q, 2, "wq")
    W_o = gw(w_o, 2, "wo")

    sp_flat = jnp.concatenate([conv_pw1_b.reshape(-1), conv_dw_b.reshape(-1), conv_ln_g.reshape(-1), conv_ln_b.reshape(-1),
                               conv_pw2_b.reshape(-1), conv_dw_w.reshape(-1), ffn_dw_w.reshape(-1)])
    sp_rows = -(-sp_flat.shape[0] // 1024) * 8
    sp_flat = jnp.concatenate([sp_flat, jnp.zeros((sp_rows * 128 - sp_flat.shape[0],), f32)]).reshape(sp_rows, 128)
    n1, nd = conv_pw1_b.shape[1], conv_dw_b.shape[1]
    nfw = ffn_dw_w.shape[2]
    sp = _ag_small(sp_flat, "ag_small_params").reshape(NDEV, -1)
    off = 0
    pw1_b = sp[:, off:off + n1].reshape(1, -1); off += n1
    dw_b = sp[:, off:off + nd].reshape(1, -1); off += nd
    ln_g = sp[:, off:off + nd].reshape(1, -1); off += nd
    ln_b = sp[:, off:off + nd].reshape(1, -1); off += nd
    pw2_b = sp[:, off:off + nd].reshape(1, -1); off += nd
    dw_w = jnp.transpose(sp[:, off:off + CONV_K * nd].reshape(NDEV, CONV_K, nd), (1, 0, 2)).reshape(CONV_K, -1); off += CONV_K * nd
    fdw_w = jnp.transpose(sp[:, off:off + 2 * FFN_K * nfw].reshape(NDEV, 2, FFN_K, nfw), (1, 2, 0, 3)).reshape(2, FFN_K, -1)
    dw_w32 = _pad_rows(dw_w, 32)

    tabs = _rope_tables(positions.reshape(S, 1), "rope_tables")

    def ffn_forward(xin, l, modv):
        h2 = _mod_fwd(xin, row(norm_ffn_g, l), row(modv, 3), row(modv, 4), f"ffn{l}_mod")
        u2 = _mm_nn(h2, W_up, l, name=f"ffn{l}_up")
        fw8 = _pad_rows(fdw_w[l], 8)
        act = _ffn_gate_fwd(u2, fw8, row(ffn_dw_b, l), f"ffn{l}_gate")
        xout, f = _mm_nn(act, W_down, l, name=f"ffn{l}_down", res=xin, gate=row(modv, 5), tk=1408)
        return xout, (h2, u2, act, f, fw8)

    def ffn_backward(dx, xin, l, modv, saved):
        h2, u2, act, f, fw8 = saved
        df, dgate, _ = _gate_bwd(dx, f, row(modv, 5), f"ffn{l}_gate_bwd")
        dact = _mm_nt(df, W_down, l, name=f"ffn{l}_dact", out_dtype=bf16, tko=512)
        d_down = _mm_tn(act, df, name=f"ffn{l}_ddown", col_sharded=False)
        du2, d_fw, d_fb = _ffn_gate_bwd(u2, dact, fw8, row(ffn_dw_b, l), f"ffn{l}_gatebwd")
        dh2 = _mm_nt(du2, W_up, l, name=f"ffn{l}_dh", out_dtype=f32)
        d_up = _mm_tn(h2, du2, name=f"ffn{l}_dup", col_sharded=True)
        dxin, dsh, dsc, dg = _mod_bwd(dh2, xin, dx, row(norm_ffn_g, l), row(modv, 4), f"ffn{l}_mod_bwd")
        return dxin, dict(d_up=d_up, d_down=d_down, d_fw=d_fw[:FFN_K], d_fb=d_fb, dsh=dsh, dsc=dsc, dgate=dgate, dg=dg)

    h0 = _mod_fwd(x0, row(norm_mix_g, 0), row(mod0, 0), row(mod0, 1), "l0_mod")
    u0 = _mm_nn(h0, W_pw1, 0, name="l0_pw1", bias=pw1_b)
    s0 = _conv_fwd(u0, dw_w32, dw_b, ln_g, ln_b, "l0_conv")
    x1, f0 = _mm_nn(s0, W_pw2, 0, name="l0_pw2", bias=pw2_b, res=x0, gate=row(mod0, 2))
    x2, ffn0_saved = ffn_forward(x1, 0, mod0)

    hkv = _mod_fwd(x2, kv_norm_g.reshape(1, -1), row(modkv, 0), row(modkv, 1), "kv_mod")
    kvraw = _mm_nn(hkv, W_kv, 0, name="kv_proj")
    kg = k_norm_g.reshape(1, -1)
    kk = _qk_fwd(kvraw, kg, tabs, QW, "k_norm_rope")
    h1 = _mod_fwd(x2, row(norm_mix_g, 1), row(mod1, 0), row(mod1, 1), "l1_mod")
    qraw = _mm_nn(h1, W_q, 0, name="q_proj")
    qg = q_norm_g.reshape(1, -1)
    qq = _qk_fwd(qraw, qg, tabs, QW, "q_norm_rope")
    o_gs, lses = [], []
    for gi, r in enumerate(DILS):
        o_g, lse_g = _attn_fwd(qq, kk, kvraw, gi, r, f"attn_fwd{gi}")
        o_gs.append(o_g)
        lses.append(lse_g)
    o_mix = _combine_fwd(o_gs, lses, "attn_mix")
    x3, f1 = _mm_nn(o_mix, W_o, 0, name="o_proj", res=x2, gate=row(mod1, 2))
    x4, ffn1_saved = ffn_forward(x3, 1, mod1)

    dx4, loss_blk = _loss_grad(x4, target, "loss")
    loss = lax.psum(loss_blk[0, 0], ("x", "y", "c"))

    dx3, gf1 = ffn_backward(dx4, x3, 1, mod1, ffn1_saved)
    dy1, dgate_m1, _ = _gate_bwd(dx3, f1, row(mod1, 2), "l1_gate_bwd")
    do = _mm_nt(dy1, W_o, 0, name="o_proj_dx", out_dtype=f32, tko=1024)
    d_wo = _mm_tn(o_mix, dy1, name="o_proj_dw", col_sharded=True)
    outs = _combine_bwd(do, o_gs, lses, "attn_mix_bwd")
    do_gs, corrs = outs[:NG], outs[NG:]
    dq_gs, dk_gs, dv_gs = [], [], []
    for gi, r in enumerate(DILS):
        dq_gs.append(_attn_bwd_q(qq, kk, kvraw, do_gs[gi], lses[gi], corrs[gi], gi, r, f"attn_bwd_q{gi}"))
        dk_g, dv_g = _attn_bwd_kv(qq, kk, kvraw, do_gs[gi], lses[gi], corrs[gi], gi, r, f"attn_bwd_kv{gi}")
        dk_gs.append(dk_g)
        dv_gs.append(dv_g)
    dqraw, d_qg = _qk_bwd(dq_gs, qraw, qg, tabs, QW, (), "q_norm_rope_bwd")
    dkvraw, d_kg = _qk_bwd(dk_gs, kvraw, kg, tabs, QW, tuple(dv_gs), "k_norm_rope_bwd")
    dh1 = _mm_nt(dqraw, W_q, 0, name="q_proj_dx", out_dtype=f32)
    d_wq = _mm_tn(h1, dqraw, name="q_proj_dw", col_sharded=True)
    dhkv = _mm_nt(dkvraw, W_kv, 0, name="kv_proj_dx", out_dtype=f32)
    d_wkv = _mm_tn(hkv, dkvraw, name="kv_proj_dw", col_sharded=True)
    dx2a, dsh_m1, dsc_m1, dg_mix1 = _mod_bwd(dh1, x2, dx3, row(norm_mix_g, 1), row(mod1, 1), "l1_mod_bwd")
    dx2, dsh_kv, dsc_kv, dg_kvn = _mod_bwd(dhkv, x2, dx2a, kv_norm_g.reshape(1, -1), row(modkv, 1), "kv_mod_bwd")

    dx1, gf0 = ffn_backward(dx2, x1, 0, mod0, ffn0_saved)
    dy0, dgate_m0, d_pw2b = _gate_bwd(dx1, f0, row(mod0, 2), "l0_gate_bwd")
    ds0 = _mm_nt(dy0, W_pw2, 0, name="l0_pw2_dx", out_dtype=bf16, tko=1024)
    d_pw2 = _mm_tn(s0, dy0, name="l0_pw2_dw", col_sharded=False)
    dcv, d_lng, d_lnb, d_dwb, d_dww = _conv_bwd1(u0, ds0, dw_w32, dw_b, ln_g, ln_b, "l0_conv_bwd1")
    du0, d_pw1b = _conv_bwd2(dcv, u0, dw_w32, "l0_conv_bwd2")
    dh0 = _mm_nt(du0, W_pw1, 0, name="l0_pw1_dx", out_dtype=f32)
    d_pw1 = _mm_tn(h0, du0, name="l0_pw1_dw", col_sharded=True)
    grad_x, dsh_m0, dsc_m0, dg_mix0 = _mod_bwd(dh0, x0, dx1, row(norm_mix_g, 0), row(mod0, 1), "l0_mod_bwd")

    dm0 = [dsh_m0, dsc_m0, dgate_m0, gf0["dsh"], gf0["dsc"], gf0["dgate"]]
    dm1 = [dsh_m1, dsc_m1, dgate_m1, gf1["dsh"], gf1["dsc"], gf1["dgate"]]
    pieces = dm0 + dm1 + [dsh_kv, dsc_kv,
                          dg_mix0, dg_mix1, gf0["dg"], gf1["dg"], dg_kvn, d_kg, d_qg, gf0["d_fb"], gf1["d_fb"],
                          d_pw1b, d_dww[:CONV_K], d_dwb, d_lng, d_lnb, d_pw2b, gf0["d_fw"], gf1["d_fw"]]
    flat = jnp.concatenate([p.reshape(-1) for p in pieces])
    n_flat = flat.shape[0]
    n_rows = -(-n_flat // 1024) * 8
    flat = jnp.concatenate([flat, jnp.zeros((n_rows * 128 - n_flat,), f32)]).reshape(n_rows, 128)
    g_all = _ag_small(flat, "ag_small_grads")
    g_sum = _sum8(g_all, "sum_small_grads").reshape(-1)
    n_dm = 2 * 6 * Dm + 2 * Dm
    dm_all = g_all.reshape(NDEV, -1)[:, :n_dm]

    take_pos = [0]

    def take(shape):
        n = int(np.prod(shape))
        out = g_sum[take_pos[0]:take_pos[0] + n].reshape(shape)
        take_pos[0] += n
        return out

    g_mod_b = take((2, 6 * Dm))
    g_kv_mod_b = take((2 * Dm,))
    g_norm_mix0, g_norm_mix1 = take((Dm,)), take((Dm,))
    g_norm_ffn0, g_norm_ffn1 = take((Dm,)), take((Dm,))
    g_kv_norm = take((Dm,))
    g_k_norm = take((DH,))
    g_q_norm = take((1, DH))
    g_ffn_dw_b = take((2, F))
    shard = lambda full, n, axis: lax.dynamic_slice_in_dim(full, me * n, n, axis)
    g_pw1_b = shard(take((1, 2 * Dm)), n1, 1)
    g_dw_w = shard(take((1, CONV_K, Dm)), nd, 2)
    g_dw_b = shard(take((1, Dm)), nd, 1)
    g_ln_g = shard(take((1, Dm)), nd, 1)
    g_ln_b = shard(take((1, Dm)), nd, 1)
    g_pw2_b = shard(take((1, Dm)), nd, 1)
    g_ffn_dw_w = shard(jnp.stack([take((FFN_K, F)), take((FFN_K, F))]), nfw, 2)
    g_norm_mix = jnp.stack([g_norm_mix0, g_norm_mix1])
    g_norm_ffn = jnp.stack([g_norm_ffn0, g_norm_ffn1])

    small = [("mod_b", mod_b, m_mod_b, v_mod_b, g_mod_b), ("norm_mix_g", norm_mix_g, m_norm_mix_g, v_norm_mix_g, g_norm_mix),
             ("norm_ffn_g", norm_ffn_g, m_norm_ffn_g, v_norm_ffn_g, g_norm_ffn),
             ("conv_pw1_b", conv_pw1_b, m_conv_pw1_b, v_conv_pw1_b, g_pw1_b),
             ("conv_dw_w", conv_dw_w, m_conv_dw_w, v_conv_dw_w, g_dw_w), ("conv_dw_b", conv_dw_b, m_conv_dw_b, v_conv_dw_b, g_dw_b),
             ("conv_ln_g", conv_ln_g, m_conv_ln_g, v_conv_ln_g, g_ln_g), ("conv_ln_b", conv_ln_b, m_conv_ln_b, v_conv_ln_b, g_ln_b),
             ("conv_pw2_b", conv_pw2_b, m_conv_pw2_b, v_conv_pw2_b, g_pw2_b),
             ("kv_mod_b", kv_mod_b, m_kv_mod_b, v_kv_mod_b, g_kv_mod_b), ("kv_norm_g", kv_norm_g, m_kv_norm_g, v_kv_norm_g, g_kv_norm),
             ("k_norm_g", k_norm_g, m_k_norm_g, v_k_norm_g, g_k_norm), ("q_norm_g", q_norm_g, m_q_norm_g, v_q_norm_g, g_q_norm),
             ("ffn_dw_w", ffn_dw_w, m_ffn_dw_w, v_ffn_dw_w, g_ffn_dw_w), ("ffn_dw_b", ffn_dw_b, m_ffn_dw_b, v_ffn_dw_b, g_ffn_dw_b)]
    n_small = sum(int(np.prod(s[1].shape)) for s in small)
    rows_small = -(-n_small // 1024) * 8

    def pack(idx):
        fl = jnp.concatenate([s[idx].reshape(-1) for s in small])
        return jnp.concatenate([fl, jnp.ones((rows_small * 128 - n_small,), f32)]).reshape(rows_small, 128)

    sd, sm, sv = _adamw_plain(pack(1), pack(2), pack(3), pack(4), "adamw_small")
    res = {}
    pos = 0
    for name, w, _, _, g in small:
        n = int(np.prod(w.shape))
        cut = lambda a: a.reshape(-1)[pos:pos + n].reshape(w.shape)
        res[name] = (g.reshape(w.shape), cut(sd), cut(sm), cut(sv))
        pos += n

    c_all_t = jnp.transpose(c_all)

    def mod_update(w2d, m2d, v2d, dm_cols, tag):
        g = _modgrad(c_all_t, dm_cols, f"modgrad_{tag}")
        d, m2, v2 = _adamw_plain(w2d, m2d, v2d, g, f"adamw_{tag}")
        return g, d, m2, v2

    mw = []
    for l in range(2):
        cols = lax.dynamic_slice_in_dim(dm_all[:, l * 6 * Dm:(l + 1) * 6 * Dm], me * n_mod, n_mod, 1)
        mw.append(mod_update(mod_w[l], m_mod_w[l], v_mod_w[l], cols, f"mod_w{l}"))
    res["mod_w"] = tuple(jnp.stack([mw[0][i], mw[1][i]]) for i in range(4))
    cols = lax.dynamic_slice_in_dim(dm_all[:, 12 * Dm:], me * n_kvm, n_kvm, 1)
    res["kv_mod_w"] = mod_update(kv_mod_w, m_kv_mod_w, v_kv_mod_w, cols, "kv_mod_w")

    def big(dwbs, w, m, v, tag):
        dwb = jnp.stack(dwbs, axis=1)
        outs = _reduce_and_update(dwb, as3(w), as3(m), as3(v), core, chip, tag)
        return tuple(o.reshape(w.shape) for o in outs)

    res["conv_pw1_w"] = big([d_pw1], conv_pw1_w, m_conv_pw1_w, v_conv_pw1_w, "pw1")
    res["conv_pw2_w"] = big([d_pw2], conv_pw2_w, m_conv_pw2_w, v_conv_pw2_w, "pw2")
    res["w_kv"] = big([d_wkv], w_kv, m_w_kv, v_w_kv, "wkv")
    res["w_q"] = big([d_wq], w_q, m_w_q, v_w_q, "wq")
    res["w_o"] = big([d_wo], w_o, m_w_o, v_w_o, "wo")
    res["ffn_up_w"] = big([gf0["d_up"], gf1["d_up"]], ffn_up_w, m_ffn_up_w, v_ffn_up_w, "up")
    res["ffn_down_w"] = big([gf0["d_down"], gf1["d_down"]], ffn_down_w, m_ffn_down_w, v_ffn_down_w, "down")

    order = ["mod_w", "mod_b", "norm_mix_g", "norm_ffn_g", "conv_pw1_w", "conv_pw1_b", "conv_dw_w", "conv_dw_b", "conv_ln_g",
             "conv_ln_b", "conv_pw2_w", "conv_pw2_b", "kv_mod_w", "kv_mod_b", "kv_norm_g", "w_kv", "k_norm_g", "w_q", "q_norm_g",
             "w_o", "ffn_up_w", "ffn_dw_w", "ffn_dw_b", "ffn_down_w"]
    out = [loss, grad_x.reshape(x.shape)]
    for i in range(4):
        out += [res[n][i] for n in order]
    return tuple(out)
```

```python
import functools
import math

import numpy as np
import jax
import jax.numpy as jnp
from jax import lax
from jax.experimental import pallas as pl
from jax.experimental.pallas import tpu as pltpu

f32 = jnp.float32
bf16 = jnp.bfloat16

D = 2048
SEQ = 8192
FF = 5632
CONV_K = 31
FFN_K = 3
HPG = 8
DH = 128
NG = 3
DILS = (1, 4, 16)
BLK = 128
ROT = 32
THETA = 500000.0
EPS = 1e-6
NEG = -1e30
NDEV = 8
HALO = 32
FHALO = 16

LR, B1, B2, AEPS, WD, STEP = 0.001, 0.9, 0.999, 1e-08, 0.01, 10

VMEM_BIG = 56 * 1024 * 1024

ARB = "arbitrary"
PAR = "parallel"
MESH = pl.DeviceIdType.MESH


def _cp(sem, vmem=None):
    return pltpu.CompilerParams(dimension_semantics=sem, vmem_limit_bytes=vmem)


def _tile(n, pref, mult=128):
    if n <= pref:
        return n
    t = (pref // mult) * mult
    while t >= mult:
        if n % t == 0:
            return t
        t -= mult
    return n


def _sigmoid(x):
    return 1.0 / (1.0 + jnp.exp(-x))


def _me():
    return lax.axis_index("x"), lax.axis_index("y"), lax.axis_index("c")


class _Comm:
    def __init__(self, arrays, out_shapes, sems, start, finish):
        self.arrays, self.out_shapes, self.sems, self.start, self.finish = arrays, out_shapes, sems, start, finish


def _pcall(body, *, name, grid, in_specs, out_specs, out_shape, args, scratch_shapes=(), sem=None, vmem=None, comms=(),
           aliases=None):
    aliases = aliases or {}
    if not comms:
        return pl.pallas_call(body, name=name, grid=grid, in_specs=in_specs, out_specs=out_specs, out_shape=out_shape,
                              scratch_shapes=list(scratch_shapes), input_output_aliases=aliases,
                              compiler_params=_cp(sem, vmem))(*args)
    single = not isinstance(out_shape, (list, tuple))
    outs_shape = [out_shape] if single else list(out_shape)
    outs_spec = [out_specs] if single else list(out_specs)
    n_in, n_out, n_scr = len(args), len(outs_shape), len(scratch_shapes)
    c_in = [a for cm in comms for a in cm.arrays]
    c_out = [s for cm in comms for s in cm.out_shapes]
    c_scr = [s for cm in comms for s in cm.sems]

    def split(refs, counts):
        out, pos = [], 0
        for n in counts:
            out.append(refs[pos:pos + n])
            pos += n
        return out

    def wrapped(*refs):
        ins, cins, outs, couts, scr, cscr = split(refs, [n_in, len(c_in), n_out, len(c_out), n_scr, len(c_scr)])
        ids = [pl.program_id(a) for a in range(len(grid))]
        first = functools.reduce(jnp.logical_and, [i == 0 for i in ids])
        last = functools.reduce(jnp.logical_and, [i == g - 1 for i, g in zip(ids, grid)])
        per_in = split(cins, [len(cm.arrays) for cm in comms])
        per_out = split(couts, [len(cm.out_shapes) for cm in comms])
        per_sem = split(cscr, [len(cm.sems) for cm in comms])

        @pl.when(first)
        def _():
            for cm, a, b, s in zip(comms, per_in, per_out, per_sem):
                cm.start(a, b, s)

        body(*ins, *outs, *scr)

        @pl.when(last)
        def _():
            for cm, a, b, s in zip(comms, per_in, per_out, per_sem):
                cm.finish(a, b, s)

    hbm = pl.BlockSpec(memory_space=pl.ANY)
    res = pl.pallas_call(
        wrapped, name=name, grid=grid, in_specs=list(in_specs) + [hbm] * len(c_in),
        out_specs=outs_spec + [hbm] * len(c_out), out_shape=outs_shape + c_out,
        scratch_shapes=list(scratch_shapes) + c_scr, input_output_aliases=aliases,
        compiler_params=_cp((ARB,) * len(grid), vmem))(*args, *c_in)
    main = res[0] if single else list(res[:n_out])
    return main, split(list(res[n_out:]), [len(cm.out_shapes) for cm in comms])


def _comm_allgather(w, axis):
    n = w.shape[axis]
    out_shape = list(w.shape)
    out_shape[axis] = NDEV * n

    def parts(ins, outs, sems):
        x_ref, out_ref = ins[0], outs[0]
        send_sems, recv_sems, local_sem = sems
        mx, my, mc = _me()
        chips = [(1 - mx, my), (mx, 1 - my), (1 - mx, 1 - my)]

        def blk(px, py, pc):
            start = pl.multiple_of((4 * px + 2 * py + pc) * n, n)
            if axis == 1:
                return out_ref.at[:, pl.ds(start, n), :]
            return out_ref.at[:, :, pl.ds(start, n)]

        def copy(k, block, to, src=None):
            return pltpu.make_async_remote_copy(
                src_ref=blk(*block) if src is None else src, dst_ref=blk(*block),
                send_sem=send_sems.at[k], recv_sem=recv_sems.at[k], device_id=to, device_id_type=MESH)

        me, sibling = (mx, my, mc), (mx, my, 1 - mc)
        mine = pltpu.make_async_copy(x_ref, blk(*me), local_sem)
        first = [copy(0, me, sibling, src=x_ref)] + [copy(1 + j, me, (*chip, mc), src=x_ref) for j, chip in enumerate(chips)]
        passed = [copy(4 + j, (*chip, mc), sibling) for j, chip in enumerate(chips)]
        return me, sibling, chips, mc, copy, mine, first, passed

    def start(ins, outs, sems):
        *_, mine, first, _ = parts(ins, outs, sems)
        mine.start()
        for cp in first:
            cp.start()

    def finish(ins, outs, sems):
        me, sibling, chips, mc, copy, mine, first, passed = parts(ins, outs, sems)
        for j, chip in enumerate(chips):
            copy(1 + j, (*chip, mc), me).wait_recv()
            passed[j].start()
        copy(0, sibling, me).wait_recv()
        for j, chip in enumerate(chips):
            copy(4 + j, (*chip, 1 - mc), me).wait_recv()
        for cp in first + passed:
            cp.wait_send()
        mine.wait()

    return _Comm([w], [jax.ShapeDtypeStruct(tuple(out_shape), w.dtype)],
                 [pltpu.SemaphoreType.DMA((7,)), pltpu.SemaphoreType.DMA((7,)), pltpu.SemaphoreType.DMA], start, finish)


def _comm_rs_sibling(dwb):
    def copies(ins, outs, sems):
        mx, my, mc = _me()
        return [pltpu.make_async_remote_copy(
            src_ref=ins[0].at[2 * p + (1 - mc)], dst_ref=outs[0].at[p], send_sem=sems[0].at[p], recv_sem=sems[1].at[p],
            device_id=(mx, my, 1 - mc), device_id_type=MESH) for p in range(4)]

    def start(ins, outs, sems):
        for cp in copies(ins, outs, sems):
            cp.start()

    def finish(ins, outs, sems):
        cps = copies(ins, outs, sems)
        for cp in cps:
            cp.wait_recv()
        for cp in cps:
            cp.wait_send()

    return _Comm([dwb], [jax.ShapeDtypeStruct((4,) + dwb.shape[1:], dwb.dtype)],
                 [pltpu.SemaphoreType.DMA((4,)), pltpu.SemaphoreType.DMA((4,))], start, finish)


def _comm_rs_chips(part):
    def copies(ins, outs, sems):
        mx, my, mc = _me()
        chips = [(1 - mx, my), (mx, 1 - my), (1 - mx, 1 - my)]
        return [pltpu.make_async_remote_copy(
            src_ref=ins[0].at[2 * px + py], dst_ref=outs[0].at[k], send_sem=sems[0].at[k], recv_sem=sems[1].at[k],
            device_id=(px, py, mc), device_id_type=MESH) for k, (px, py) in enumerate(chips)]

    def start(ins, outs, sems):
        for cp in copies(ins, outs, sems):
            cp.start()

    def finish(ins, outs, sems):
        cps = copies(ins, outs, sems)
        for cp in cps:
            cp.wait_recv()
        for cp in cps:
            cp.wait_send()

    return _Comm([part], [jax.ShapeDtypeStruct((3,) + part.shape[1:], part.dtype)],
                 [pltpu.SemaphoreType.DMA((3,)), pltpu.SemaphoreType.DMA((3,))], start, finish)


def _ag_small(x, name):
    r, c = x.shape

    def body(x_ref, out_ref, send_sems, recv_sems):
        mx, my, mc = _me()
        mine = 4 * mx + 2 * my + mc
        out_ref[mine] = x_ref[...]
        copies = []
        for k in range(1, NDEV):
            px = 1 - mx if (k >> 2) & 1 else mx
            py = 1 - my if (k >> 1) & 1 else my
            pc = 1 - mc if k & 1 else mc
            cp = pltpu.make_async_remote_copy(
                src_ref=x_ref, dst_ref=out_ref.at[mine], send_sem=send_sems.at[k - 1], recv_sem=recv_sems.at[k - 1],
                device_id=(px, py, pc), device_id_type=MESH)
            cp.start()
            copies.append((cp, 4 * px + 2 * py + pc))
        for k, (cp, peer) in enumerate(copies):
            pltpu.make_async_remote_copy(
                src_ref=x_ref, dst_ref=out_ref.at[peer], send_sem=send_sems.at[k], recv_sem=recv_sems.at[k],
                device_id=(mx, my, mc), device_id_type=MESH).wait_recv()
        for cp, _ in copies:
            cp.wait_send()

    return pl.pallas_call(
        body, name=name,
        out_shape=jax.ShapeDtypeStruct((NDEV, r, c), x.dtype),
        in_specs=[pl.BlockSpec(memory_space=pltpu.VMEM)],
        out_specs=pl.BlockSpec(memory_space=pltpu.VMEM),
        scratch_shapes=[pltpu.SemaphoreType.DMA((NDEV - 1,)), pltpu.SemaphoreType.DMA((NDEV - 1,))],
    )(x)


def _ag_big(w, axis, name):
    n = w.shape[axis]
    out_shape = list(w.shape)
    out_shape[axis] = NDEV * n

    def body(x_ref, out_ref, send_sems, recv_sems, local_sem):
        mx, my, mc = _me()
        me, sibling = (mx, my, mc), (mx, my, 1 - mc)
        chips = [(1 - mx, my), (mx, 1 - my), (1 - mx, 1 - my)]

        def blk(px, py, pc):
            start = pl.multiple_of((4 * px + 2 * py + pc) * n, n)
            if axis == 1:
                return out_ref.at[:, pl.ds(start, n), :]
            return out_ref.at[:, :, pl.ds(start, n)]

        def copy(k, block, to, src=None):
            return pltpu.make_async_remote_copy(
                src_ref=blk(*block) if src is None else src, dst_ref=blk(*block),
                send_sem=send_sems.at[k], recv_sem=recv_sems.at[k], device_id=to, device_id_type=MESH)

        mine = pltpu.make_async_copy(x_ref, blk(*me), local_sem)
        mine.start()
        first = [copy(0, me, sibling, src=x_ref)]
        first += [copy(1 + j, me, (*chip, mc), src=x_ref) for j, chip in enumerate(chips)]
        for cp in first:
            cp.start()
        passed = [copy(4 + j, (*chip, mc), sibling) for j, chip in enumerate(chips)]
        for j, chip in enumerate(chips):
            copy(1 + j, (*chip, mc), me).wait_recv()
            passed[j].start()
        copy(0, sibling, me).wait_recv()
        for j, chip in enumerate(chips):
            copy(4 + j, (*chip, 1 - mc), me).wait_recv()
        for cp in first + passed:
            cp.wait_send()
        mine.wait()

    return pl.pallas_call(
        body, name=name,
        out_shape=jax.ShapeDtypeStruct(tuple(out_shape), w.dtype),
        in_specs=[pl.BlockSpec(memory_space=pl.ANY)],
        out_specs=pl.BlockSpec(memory_space=pl.ANY),
        scratch_shapes=[pltpu.SemaphoreType.DMA((7,)), pltpu.SemaphoreType.DMA((7,)), pltpu.SemaphoreType.DMA],
    )(w)


def _chip_partial(dwb, r1, core, name):
    _, A, B = dwb.shape
    ta = _tile(A, 512, 16)

    def body(c_ref, a_ref, b_ref, o_ref):
        o_ref[...] = (a_ref[...].astype(f32) + b_ref[...].astype(f32)).astype(o_ref.dtype)

    grid_spec = pltpu.PrefetchScalarGridSpec(
        num_scalar_prefetch=1, grid=(4, A // ta),
        in_specs=[pl.BlockSpec((None, ta, B), lambda p, i, c: (2 * p + c[0], i, 0)),
                  pl.BlockSpec((None, ta, B), lambda p, i, c: (p, i, 0))],
        out_specs=pl.BlockSpec((None, ta, B), lambda p, i, c: (p, i, 0)))
    return pl.pallas_call(body, name=name, grid_spec=grid_spec,
                          out_shape=jax.ShapeDtypeStruct((4, A, B), dwb.dtype),
                          compiler_params=_cp((PAR, PAR)))(core, dwb, r1)


def _adam_math(w, g, m, v):
    m2 = B1 * m + (1.0 - B1) * g
    v2 = B2 * v + (1.0 - B2) * (g * g)
    m_hat = m2 / (1.0 - B1 ** STEP)
    v_hat = v2 / (1.0 - B2 ** STEP)
    delta = -LR * (m_hat / (jnp.sqrt(v_hat) + AEPS) + WD * w)
    return delta, m2, v2


def _adamw_reduced(w, m, v, mine, r2, l, prev, name, comms=()):
    L, A, B = w.shape
    ta = _tile(A, 256, 8)

    def body(w_ref, m_ref, v_ref, p_ref, r_ref, *rest):
        g_out, d_out, m_out, v_out = rest[-4:]
        g = ((p_ref[...].astype(f32) + r_ref[0].astype(f32)) + r_ref[1].astype(f32)) + r_ref[2].astype(f32)
        d, m2, v2 = _adam_math(w_ref[...], g, m_ref[...], v_ref[...])
        g_out[...] = g
        d_out[...] = d
        m_out[...] = m2
        v_out[...] = v2

    wspec = pl.BlockSpec((None, ta, B), lambda i: (l, i, 0))
    in_specs = [wspec, wspec, wspec, pl.BlockSpec((ta, B), lambda i: (i, 0)), pl.BlockSpec((3, ta, B), lambda i: (0, i, 0))]
    args = [w, m, v, mine, r2]
    aliases = {}
    if prev is not None:
        in_specs += [pl.BlockSpec(memory_space=pl.ANY)] * 4
        args += list(prev)
        aliases = {5 + i: i for i in range(4)}
    shp = jax.ShapeDtypeStruct((L, A, B), f32)
    return _pcall(body, name=name, grid=(A // ta,), in_specs=in_specs, out_specs=[wspec] * 4, out_shape=[shp] * 4,
                  args=args, sem=(PAR,), comms=comms, aliases=aliases)


def _adamw_plain(w, m, v, g, name):
    A, B = w.shape
    ta = _tile(A, 256, 8)

    def body(w_ref, m_ref, v_ref, g_ref, d_out, m_out, v_out):
        d, m2, v2 = _adam_math(w_ref[...], g_ref[...], m_ref[...], v_ref[...])
        d_out[...] = d
        m_out[...] = m2
        v_out[...] = v2

    spec = pl.BlockSpec((ta, B), lambda i: (i, 0))
    shp = jax.ShapeDtypeStruct((A, B), f32)
    return pl.pallas_call(body, name=name, grid=(A // ta,), in_specs=[spec] * 4, out_specs=[spec] * 3,
                          out_shape=[shp, shp, shp], compiler_params=_cp((PAR,)))(w, m, v, g)


def _sum8(g, name):
    _, R, C = g.shape

    def body(g_ref, o_ref):
        acc = g_ref[0]
        for j in range(1, NDEV):
            acc = acc + g_ref[j]
        o_ref[...] = acc

    return pl.pallas_call(body, name=name, out_shape=jax.ShapeDtypeStruct((R, C), f32))(g)


def _modproj(c_all, w, bias, name):
    K, N = w.shape
    tn = _tile(N, 512)

    def body(c_ref, w_ref, b_ref, o_ref):
        cc = c_ref[...]
        sc = (cc * _sigmoid(cc)).astype(bf16)
        o_ref[...] = jnp.dot(sc, w_ref[...].astype(bf16), preferred_element_type=f32) + b_ref[...]

    return pl.pallas_call(
        body, name=name, grid=(N // tn,),
        in_specs=[pl.BlockSpec((NDEV, K), lambda j: (0, 0)), pl.BlockSpec((K, tn), lambda j: (0, j)),
                  pl.BlockSpec((1, tn), lambda j: (0, j))],
        out_specs=pl.BlockSpec((NDEV, tn), lambda j: (0, j)),
        out_shape=jax.ShapeDtypeStruct((NDEV, N), f32), compiler_params=_cp((PAR,)))(c_all, w, bias)


def _modgrad(c_all_t, dm, name):
    K = c_all_t.shape[0]
    N = dm.shape[1]
    tn = _tile(N, 512)

    def body(c_ref, d_ref, o_ref):
        cc = c_ref[...]
        sc = cc * _sigmoid(cc)
        dmv = d_ref[...]
        acc = sc[:, 0:1] * dmv[0:1, :]
        for b in range(1, NDEV):
            acc = acc + sc[:, b:b + 1] * dmv[b:b + 1, :]
        o_ref[...] = acc

    return pl.pallas_call(
        body, name=name, grid=(N // tn,),
        in_specs=[pl.BlockSpec((K, NDEV), lambda j: (0, 0)), pl.BlockSpec((NDEV, tn), lambda j: (0, j))],
        out_specs=pl.BlockSpec((K, tn), lambda j: (0, j)),
        out_shape=jax.ShapeDtypeStruct((K, N), f32), compiler_params=_cp((PAR,)))(c_all_t, dm)


def _mm_nn(a, w, l, *, name, out_dtype=bf16, bias=None, res=None, gate=None, tm=1024, tn=1024, tk=2048, comms=()):
    M, K = a.shape
    N = w.shape[2]
    tm, tn, tk = _tile(M, tm, 8), _tile(N, tn), _tile(K, tk)
    nk = K // tk
    epi = res is not None

    def body(*refs):
        it = iter(refs)
        a_ref, w_ref = next(it), next(it)
        b_ref = next(it) if bias is not None else None
        r_ref = next(it) if epi else None
        g_ref = next(it) if epi else None
        o_ref = next(it)
        f_ref = next(it) if epi else None
        acc = next(it)
        k = pl.program_id(2)

        @pl.when(k == 0)
        def _():
            acc[...] = jnp.zeros_like(acc)

        acc[...] += jnp.dot(a_ref[...], w_ref[...], preferred_element_type=f32)

        @pl.when(k == nk - 1)
        def _():
            y = acc[...]
            if b_ref is not None:
                y = y + b_ref[...]
            if epi:
                f_ref[...] = y.astype(f_ref.dtype)
                o_ref[...] = r_ref[...] + g_ref[...] * y
            else:
                o_ref[...] = y.astype(o_ref.dtype)

    in_specs = [pl.BlockSpec((tm, tk), lambda i, j, k: (i, k)), pl.BlockSpec((None, tk, tn), lambda i, j, k: (l, k, j))]
    args = [a, w]
    if bias is not None:
        in_specs.append(pl.BlockSpec((1, tn), lambda i, j, k: (0, j)))
        args.append(bias)
    ospec = pl.BlockSpec((tm, tn), lambda i, j, k: (i, j))
    if epi:
        in_specs += [ospec, pl.BlockSpec((1, tn), lambda i, j, k: (0, j))]
        args += [res, gate]
        out_shape = [jax.ShapeDtypeStruct((M, N), f32), jax.ShapeDtypeStruct((M, N), bf16)]
        out_specs = [ospec, ospec]
    else:
        out_shape = jax.ShapeDtypeStruct((M, N), out_dtype)
        out_specs = ospec
    return _pcall(body, name=name, grid=(M // tm, N // tn, nk), in_specs=in_specs, out_specs=out_specs, out_shape=out_shape,
                  args=args, scratch_shapes=[pltpu.VMEM((tm, tn), f32)], sem=(PAR, PAR, ARB), vmem=VMEM_BIG, comms=comms)


def _mm_nt(a, w, l, *, name, out_dtype, tm=1024, tko=2048, tn=1024, comms=()):
    planes = a.ndim == 3
    M = a.shape[-2]
    K, N = w.shape[1], w.shape[2]
    npl = a.shape[-1]
    tm, tko = _tile(M, tm, 8), _tile(K, tko)
    tn = _tile(npl, tn)
    nn = N // tn
    per_plane = npl // tn

    def body(a_ref, w_ref, o_ref, acc):
        k = pl.program_id(2)

        @pl.when(k == 0)
        def _():
            acc[...] = jnp.zeros_like(acc)

        acc[...] += lax.dot_general(a_ref[...], w_ref[...], (((1,), (1,)), ((), ())), preferred_element_type=f32)

        @pl.when(k == nn - 1)
        def _():
            o_ref[...] = acc[...].astype(o_ref.dtype)

    if planes:
        a_spec = pl.BlockSpec((None, tm, tn), lambda i, j, k: (k // per_plane, i, k % per_plane))
    else:
        a_spec = pl.BlockSpec((tm, tn), lambda i, j, k: (i, k))
    return _pcall(body, name=name, grid=(M // tm, K // tko, nn),
                  in_specs=[a_spec, pl.BlockSpec((None, tko, tn), lambda i, j, k: (l, j, k))],
                  out_specs=pl.BlockSpec((tm, tko), lambda i, j, k: (i, j)),
                  out_shape=jax.ShapeDtypeStruct((M, K), out_dtype), args=[a, w],
                  scratch_shapes=[pltpu.VMEM((tm, tko), f32)], sem=(PAR, PAR, ARB), vmem=VMEM_BIG, comms=comms)


def _mm_tn(a, b, *, name, col_sharded, tk=2048, ts=1024, comms=()):
    planes = b.ndim == 3
    S, K = a.shape
    N = b.shape[-1] * (2 if planes else 1)
    ts = _tile(S, ts, 16)
    ns_steps = S // ts
    if col_sharded:
        tn = N // NDEV
        tk = _tile(K, tk)
    else:
        tn = N
        tk = _tile(K, 1408)
    per_plane = (b.shape[-1] // tn) if planes else 0

    def body(a_ref, b_ref, o_ref, acc):
        s = pl.program_id(2)

        @pl.when(s == 0)
        def _():
            acc[...] = jnp.zeros_like(acc)

        acc[...] += lax.dot_general(a_ref[...], b_ref[...], (((0,), (0,)), ((), ())), preferred_element_type=f32)

        @pl.when(s == ns_steps - 1)
        def _():
            o_ref[...] = acc[...].astype(o_ref.dtype)

    if planes:
        b_spec = pl.BlockSpec((None, ts, tn), lambda k, n, s: (n // per_plane, s, n % per_plane))
    else:
        b_spec = pl.BlockSpec((ts, tn), lambda k, n, s: (s, n))
    if col_sharded:
        out_shape = jax.ShapeDtypeStruct((NDEV, K, tn), bf16)
        out_spec = pl.BlockSpec((None, tk, tn), lambda k, n, s: (n, k, 0))
    else:
        out_shape = jax.ShapeDtypeStruct((K, N), bf16)
        out_spec = pl.BlockSpec((tk, tn), lambda k, n, s: (k, n))
    res = _pcall(body, name=name, grid=(K // tk, N // tn, ns_steps),
                 in_specs=[pl.BlockSpec((ts, tk), lambda k, n, s: (s, k)), b_spec],
                 out_specs=out_spec, out_shape=out_shape, args=[a, b],
                 scratch_shapes=[pltpu.VMEM((tk, tn), f32)], sem=(PAR, PAR, ARB), vmem=VMEM_BIG, comms=comms)
    out, couts = res if comms else (res, None)
    if not col_sharded:
        out = out.reshape(NDEV, K // NDEV, N)
    return (out, couts) if comms else out


def _acc_spec(w, rows=1):
    return pl.BlockSpec((rows, w), lambda i: (0, 0))


def _mod_fwd(x, g, sh, sc, name):
    S, W = x.shape
    tm = _tile(S, 256, 8)

    def body(x_ref, g_ref, sh_ref, sc_ref, h_ref):
        xv = x_ref[...]
        r = lax.rsqrt(jnp.mean(xv * xv, axis=-1, keepdims=True) + EPS)
        h_ref[...] = ((xv * r) * g_ref[...] * (1.0 + sc_ref[...]) + sh_ref[...]).astype(h_ref.dtype)

    row = pl.BlockSpec((tm, W), lambda i: (i, 0))
    return pl.pallas_call(body, name=name, grid=(S // tm,), in_specs=[row, _acc_spec(W), _acc_spec(W), _acc_spec(W)],
                          out_specs=row, out_shape=jax.ShapeDtypeStruct((S, W), bf16), compiler_params=_cp((PAR,)))(x, g, sh, sc)


def _mod_bwd(dh, x, dx_in, g, sc, name, comms=()):
    S, W = x.shape
    tm = _tile(S, 256, 8)
    nt = S // tm

    def body(dh_ref, x_ref, dxi_ref, g_ref, sc_ref, dx_ref, dsh_ref, dsc_ref, dg_ref):
        i = pl.program_id(0)

        @pl.when(i == 0)
        def _():
            dsh_ref[...] = jnp.zeros_like(dsh_ref)
            dsc_ref[...] = jnp.zeros_like(dsc_ref)

        xv = x_ref[...]
        dh = dh_ref[...].astype(f32)
        r = lax.rsqrt(jnp.mean(xv * xv, axis=-1, keepdims=True) + EPS)
        n = xv * r
        dn = dh * (g_ref[...] * (1.0 + sc_ref[...]))
        dx = r * (dn - n * jnp.mean(dn * n, axis=-1, keepdims=True))
        dx_ref[...] = dxi_ref[...] + dx
        dsh_ref[...] += jnp.sum(dh, axis=0, keepdims=True)
        dsc_ref[...] += jnp.sum(dh * n, axis=0, keepdims=True)

        @pl.when(i == nt - 1)
        def _():
            a2 = dsc_ref[...]
            dg_ref[...] = a2 * (1.0 + sc_ref[...])
            dsc_ref[...] = a2 * g_ref[...]

    row = pl.BlockSpec((tm, W), lambda i: (i, 0))
    vec = jax.ShapeDtypeStruct((1, W), f32)
    return _pcall(body, name=name, grid=(nt,), in_specs=[row, row, row, _acc_spec(W), _acc_spec(W)],
                  out_specs=[row, _acc_spec(W), _acc_spec(W), _acc_spec(W)],
                  out_shape=[jax.ShapeDtypeStruct((S, W), f32), vec, vec, vec], args=[dh, x, dx_in, g, sc], sem=(ARB,),
                  comms=comms)


def _gate_bwd(dx, f, gate, name):
    S, W = dx.shape
    tm = _tile(S, 256, 16)

    def body(dx_ref, f_ref, g_ref, df_ref, dg_ref, sdf_ref):
        i = pl.program_id(0)

        @pl.when(i == 0)
        def _():
            dg_ref[...] = jnp.zeros_like(dg_ref)
            sdf_ref[...] = jnp.zeros_like(sdf_ref)

        d = dx_ref[...]
        df = g_ref[...] * d
        df_ref[...] = df.astype(df_ref.dtype)
        dg_ref[...] += jnp.sum(d * f_ref[...].astype(f32), axis=0, keepdims=True)
        sdf_ref[...] += jnp.sum(df, axis=0, keepdims=True)

    row = pl.BlockSpec((tm, W), lambda i: (i, 0))
    vec = jax.ShapeDtypeStruct((1, W), f32)
    return pl.pallas_call(
        body, name=name, grid=(S // tm,), in_specs=[row, row, _acc_spec(W)], out_specs=[row, _acc_spec(W), _acc_spec(W)],
        out_shape=[jax.ShapeDtypeStruct((S, W), bf16), vec, vec], compiler_params=_cp((ARB,)))(dx, f, gate)


def _loss_grad(y, target, name):
    S, W = y.shape
    tm = _tile(S, 256, 8)

    def body(y_ref, t_ref, dy_ref, l_ref):
        i = pl.program_id(0)

        @pl.when(i == 0)
        def _():
            l_ref[...] = jnp.zeros_like(l_ref)

        e = y_ref[...] - t_ref[...]
        dy_ref[...] = e * (1.0 / W)
        l_ref[...] += 0.5 * jnp.sum(jnp.mean(e * e, axis=-1, keepdims=True))

    row = pl.BlockSpec((tm, W), lambda i: (i, 0))
    return pl.pallas_call(
        body, name=name, grid=(S // tm,), in_specs=[row, row], out_specs=[row, pl.BlockSpec((8, 128), lambda i: (0, 0))],
        out_shape=[jax.ShapeDtypeStruct((S, W), f32), jax.ShapeDtypeStruct((8, 128), f32)],
        compiler_params=_cp((ARB,)))(y, target)


def _conv_core(u_ref, uh_ref, w_ref, b_ref, lg_ref, lb_ref, gbuf, tm, first):
    C = u_ref.shape[1] // 2
    u = u_ref[...].astype(f32)
    uh = uh_ref[...].astype(f32)
    gbuf[pl.ds(HALO, tm), :] = u[:, :C] * _sigmoid(u[:, C:])
    halo = uh[:, :C] * _sigmoid(uh[:, C:])
    gbuf[pl.ds(0, HALO), :] = jnp.where(first, 0.0, halo)
    w = w_ref[...]
    cv = jnp.zeros((tm, C), f32) + b_ref[...]
    for k in range(CONV_K):
        cv = cv + w[k:k + 1, :] * gbuf[pl.ds(HALO - (CONV_K - 1) + k, tm), :]
    mu = jnp.mean(cv, axis=-1, keepdims=True)
    xc = cv - mu
    rstd = lax.rsqrt(jnp.mean(xc * xc, axis=-1, keepdims=True) + EPS)
    z = xc * rstd
    ln = z * lg_ref[...] + lb_ref[...]
    return z, rstd, ln


def _halo_prev(tm, hb, w):
    return pl.BlockSpec((hb, w), lambda i: (jnp.maximum(i * (tm // hb) - 1, 0), 0))


def _conv_fwd(u, w, b, lg, lb, name, comms=()):
    S, C2 = u.shape
    C = C2 // 2
    tm = _tile(S, 256, HALO)

    def body(u_ref, uh_ref, w_ref, b_ref, lg_ref, lb_ref, s_ref, gbuf):
        first = pl.program_id(0) == 0
        _, _, ln = _conv_core(u_ref, uh_ref, w_ref, b_ref, lg_ref, lb_ref, gbuf, tm, first)
        s_ref[...] = (ln * _sigmoid(ln)).astype(s_ref.dtype)

    return _pcall(body, name=name, grid=(S // tm,),
                  in_specs=[pl.BlockSpec((tm, C2), lambda i: (i, 0)), _halo_prev(tm, HALO, C2), _acc_spec(C, 32),
                            _acc_spec(C), _acc_spec(C), _acc_spec(C)],
                  out_specs=pl.BlockSpec((tm, C), lambda i: (i, 0)), out_shape=jax.ShapeDtypeStruct((S, C), bf16),
                  args=[u, u, w, b, lg, lb], scratch_shapes=[pltpu.VMEM((tm + HALO, C), f32)], sem=(PAR,), vmem=VMEM_BIG,
                  comms=comms)


def _conv_bwd1(u, ds, w, b, lg, lb, name, comms=()):
    S, C2 = u.shape
    C = C2 // 2
    tm = _tile(S, 256, HALO)

    def body(u_ref, uh_ref, ds_ref, w_ref, b_ref, lg_ref, lb_ref, dcv_ref, dlg_ref, dlb_ref, ddb_ref, ddw_ref, gbuf):
        i = pl.program_id(0)

        @pl.when(i == 0)
        def _():
            dlg_ref[...] = jnp.zeros_like(dlg_ref)
            dlb_ref[...] = jnp.zeros_like(dlb_ref)
            ddb_ref[...] = jnp.zeros_like(ddb_ref)
            ddw_ref[...] = jnp.zeros_like(ddw_ref)

        z, rstd, ln = _conv_core(u_ref, uh_ref, w_ref, b_ref, lg_ref, lb_ref, gbuf, tm, i == 0)
        sg = _sigmoid(ln)
        dln = ds_ref[...].astype(f32) * (sg * (1.0 + ln * (1.0 - sg)))
        dlg_ref[...] += jnp.sum(dln * z, axis=0, keepdims=True)
        dlb_ref[...] += jnp.sum(dln, axis=0, keepdims=True)
        dz = dln * lg_ref[...]
        dcv = rstd * (dz - jnp.mean(dz, axis=-1, keepdims=True) - z * jnp.mean(dz * z, axis=-1, keepdims=True))
        dcv_ref[...] = dcv
        ddb_ref[...] += jnp.sum(dcv, axis=0, keepdims=True)
        for k in range(CONV_K):
            ddw_ref[pl.ds(k, 1), :] += jnp.sum(dcv * gbuf[pl.ds(HALO - (CONV_K - 1) + k, tm), :], axis=0, keepdims=True)

    vec = jax.ShapeDtypeStruct((1, C), f32)
    return _pcall(
        body, name=name, grid=(S // tm,),
        in_specs=[pl.BlockSpec((tm, C2), lambda i: (i, 0)), _halo_prev(tm, HALO, C2), pl.BlockSpec((tm, C), lambda i: (i, 0)),
                  _acc_spec(C, 32), _acc_spec(C), _acc_spec(C), _acc_spec(C)],
        out_specs=[pl.BlockSpec((tm, C), lambda i: (i, 0)), _acc_spec(C), _acc_spec(C), _acc_spec(C), _acc_spec(C, 32)],
        out_shape=[jax.ShapeDtypeStruct((S, C), f32), vec, vec, vec, jax.ShapeDtypeStruct((32, C), f32)],
        args=[u, u, ds, w, b, lg, lb], scratch_shapes=[pltpu.VMEM((tm + HALO, C), f32)], sem=(ARB,), vmem=VMEM_BIG,
        comms=comms)


def _conv_bwd2(dcv, u, w, name, comms=()):
    S, C2 = u.shape
    C = C2 // 2
    tm = _tile(S, 256, HALO)
    nt = S // tm
    nhb = S // HALO

    def body(d_ref, dn_ref, u_ref, w_ref, du_ref, db_ref, dbuf):
        i = pl.program_id(0)

        @pl.when(i == 0)
        def _():
            db_ref[...] = jnp.zeros_like(db_ref)

        dbuf[pl.ds(0, tm), :] = d_ref[...]
        dbuf[pl.ds(tm, HALO), :] = jnp.where(i == nt - 1, 0.0, dn_ref[...])
        w = w_ref[...]
        dglu = jnp.zeros((tm, C), f32)
        for k in range(CONV_K):
            dglu = dglu + w[k:k + 1, :] * dbuf[pl.ds(CONV_K - 1 - k, tm), :]
        u = u_ref[...].astype(f32)
        a, gt = u[:, :C], u[:, C:]
        sg = _sigmoid(gt)
        da = dglu * sg
        dgt = dglu * a * sg * (1.0 - sg)
        du_ref[:, :C] = da.astype(du_ref.dtype)
        du_ref[:, C:] = dgt.astype(du_ref.dtype)
        db_ref[:, :C] += jnp.sum(da, axis=0, keepdims=True)
        db_ref[:, C:] += jnp.sum(dgt, axis=0, keepdims=True)

    return _pcall(
        body, name=name, grid=(nt,),
        in_specs=[pl.BlockSpec((tm, C), lambda i: (i, 0)),
                  pl.BlockSpec((HALO, C), lambda i: (jnp.minimum((i + 1) * (tm // HALO), nhb - 1), 0)),
                  pl.BlockSpec((tm, C2), lambda i: (i, 0)), _acc_spec(C, 32)],
        out_specs=[pl.BlockSpec((tm, C2), lambda i: (i, 0)), _acc_spec(C2)],
        out_shape=[jax.ShapeDtypeStruct((S, C2), bf16), jax.ShapeDtypeStruct((1, C2), f32)],
        args=[dcv, dcv, u, w], scratch_shapes=[pltpu.VMEM((tm + HALO, C), f32)], sem=(ARB,), vmem=VMEM_BIG, comms=comms)


def _ffn_gate_fwd(u2, w, b, name, comms=()):
    S, F2 = u2.shape
    F = F2 // 2
    cw = _tile(F, 1408)
    ncw = F // cw
    tm = _tile(S, 256, FHALO)

    def body(g_ref, gh_ref, v_ref, w_ref, b_ref, a_ref, gbuf):
        first = pl.program_id(0) == 0
        gbuf[pl.ds(FHALO, tm), :] = g_ref[...].astype(f32)
        gbuf[pl.ds(0, FHALO), :] = jnp.where(first, 0.0, gh_ref[...].astype(f32))
        w = w_ref[...]
        gc = jnp.zeros((tm, cw), f32) + b_ref[...]
        for k in range(FFN_K):
            gc = gc + w[k:k + 1, :] * gbuf[pl.ds(FHALO - (FFN_K - 1) + k, tm), :]
        a_ref[...] = (gc * _sigmoid(gc) * v_ref[...].astype(f32)).astype(a_ref.dtype)

    return _pcall(
        body, name=name, grid=(S // tm, ncw),
        in_specs=[pl.BlockSpec((tm, cw), lambda i, j: (i, j)),
                  pl.BlockSpec((FHALO, cw), lambda i, j: (jnp.maximum(i * (tm // FHALO) - 1, 0), j)),
                  pl.BlockSpec((tm, cw), lambda i, j: (i, ncw + j)),
                  pl.BlockSpec((8, cw), lambda i, j: (0, j)), pl.BlockSpec((1, cw), lambda i, j: (0, j))],
        out_specs=pl.BlockSpec((tm, cw), lambda i, j: (i, j)), out_shape=jax.ShapeDtypeStruct((S, F), bf16),
        args=[u2, u2, u2, w, b], scratch_shapes=[pltpu.VMEM((tm + FHALO, cw), f32)], sem=(PAR, PAR), comms=comms)


def _ffn_gate_bwd(u2, dact, w, b, name, comms=()):
    S, F2 = u2.shape
    F = F2 // 2
    cw = _tile(F, 1408)
    ncw = F // cw
    tm = _tile(S, 256, FHALO)
    nt = S // tm
    nhb = S // FHALO
    R = tm + 2 * FHALO

    def body(g_ref, gp_ref, gn_ref, v_ref, vn_ref, d_ref, dn_ref, w_ref, b_ref, du_ref, dw_ref, db_ref, gbuf, dbuf):
        i = pl.program_id(1)
        first, last = i == 0, i == nt - 1

        @pl.when(i == 0)
        def _():
            dw_ref[...] = jnp.zeros_like(dw_ref)
            db_ref[...] = jnp.zeros_like(db_ref)

        gbuf[pl.ds(0, FHALO), :] = jnp.where(first, 0.0, gp_ref[...].astype(f32))
        gbuf[pl.ds(FHALO, tm), :] = g_ref[...].astype(f32)
        gbuf[pl.ds(FHALO + tm, FHALO), :] = gn_ref[...].astype(f32)
        w = w_ref[...]
        n_ext = tm + FHALO
        gc = jnp.zeros((n_ext, cw), f32) + b_ref[...]
        for k in range(FFN_K):
            gc = gc + w[k:k + 1, :] * gbuf[pl.ds(FHALO - (FFN_K - 1) + k, n_ext), :]
        sg = _sigmoid(gc)
        val = jnp.concatenate([v_ref[...].astype(f32), vn_ref[...].astype(f32)], axis=0)
        dact_ext = jnp.concatenate([d_ref[...].astype(f32), jnp.where(last, 0.0, dn_ref[...].astype(f32))], axis=0)
        dgc = dact_ext * val * (sg * (1.0 + gc * (1.0 - sg)))
        dbuf[...] = dgc
        dval = dact_ext[:tm] * (gc[:tm] * sg[:tm])
        dgt = jnp.zeros((tm, cw), f32)
        for k in range(FFN_K):
            dgt = dgt + w[k:k + 1, :] * dbuf[pl.ds(FFN_K - 1 - k, tm), :]
        du_ref[0] = dgt.astype(du_ref.dtype)
        du_ref[1] = dval.astype(du_ref.dtype)
        dgc_t = dgc[:tm]
        db_ref[...] += jnp.sum(dgc_t, axis=0, keepdims=True)
        for k in range(FFN_K):
            dw_ref[pl.ds(k, 1), :] += jnp.sum(dgc_t * gbuf[pl.ds(FHALO - (FFN_K - 1) + k, tm), :], axis=0, keepdims=True)

    hb = tm // FHALO
    prev = lambda j, i: (jnp.maximum(i * hb - 1, 0), j)
    nxt = lambda j, i: (jnp.minimum((i + 1) * hb, nhb - 1), j)
    nxt_v = lambda j, i: (jnp.minimum((i + 1) * hb, nhb - 1), ncw + j)
    return _pcall(
        body, name=name, grid=(ncw, nt), comms=comms, sem=(PAR, ARB), vmem=VMEM_BIG,
        args=[u2, u2, u2, u2, u2, dact, dact, w, b],
        in_specs=[pl.BlockSpec((tm, cw), lambda j, i: (i, j)), pl.BlockSpec((FHALO, cw), prev), pl.BlockSpec((FHALO, cw), nxt),
                  pl.BlockSpec((tm, cw), lambda j, i: (i, ncw + j)), pl.BlockSpec((FHALO, cw), nxt_v),
                  pl.BlockSpec((tm, cw), lambda j, i: (i, j)), pl.BlockSpec((FHALO, cw), nxt),
                  pl.BlockSpec((8, cw), lambda j, i: (0, j)), pl.BlockSpec((1, cw), lambda j, i: (0, j))],
        out_specs=[pl.BlockSpec((2, tm, cw), lambda j, i: (0, i, j)), pl.BlockSpec((8, cw), lambda j, i: (0, j)),
                   pl.BlockSpec((1, cw), lambda j, i: (0, j))],
        out_shape=[jax.ShapeDtypeStruct((2, S, F), bf16), jax.ShapeDtypeStruct((8, F), f32), jax.ShapeDtypeStruct((1, F), f32)],
        scratch_shapes=[pltpu.VMEM((R, cw), f32), pltpu.VMEM((tm + FHALO, cw), f32)])


def _rope_tables(pos_col, name):
    S = pos_col.shape[0]
    tm = _tile(S, 512, 8)
    half = ROT // 2
    inv = THETA ** (-np.arange(0, ROT, 2, dtype=np.float32) / ROT)
    lane_freq = np.zeros((1, DH), np.float32)
    lane_freq[0, :half] = inv
    lane_freq[0, half:ROT] = inv
    lane_freq = jnp.asarray(lane_freq)

    def body(p_ref, fr_ref, c_ref, sa_ref, sb_ref):
        ang = p_ref[...].astype(f32) * fr_ref[...]
        lane = lax.broadcasted_iota(jnp.int32, (tm, DH), 1)
        cs, sn = jnp.cos(ang), jnp.sin(ang)
        c_ref[...] = jnp.where(lane < ROT, cs, 1.0)
        sa_ref[...] = jnp.where(lane < half, -sn, 0.0)
        sb_ref[...] = jnp.where((lane >= half) & (lane < ROT), sn, 0.0)

    row = pl.BlockSpec((tm, DH), lambda i: (i, 0))
    shp = jax.ShapeDtypeStruct((S, DH), f32)
    return pl.pallas_call(body, name=name, grid=(S // tm,),
                          in_specs=[pl.BlockSpec((tm, 1), lambda i: (i, 0)), pl.BlockSpec((1, DH), lambda i: (0, 0))],
                          out_specs=[row, row, row], out_shape=[shp, shp, shp], compiler_params=_cp((PAR,)))(pos_col, lane_freq)


def _rope(n, c, sa, sb):
    return n * c + pltpu.roll(n, DH - ROT // 2, 1) * sa + pltpu.roll(n, ROT // 2, 1) * sb


def _rope_t(d, c, sa, sb):
    return d * c + pltpu.roll(d * sa, ROT // 2, 1) + pltpu.roll(d * sb, DH - ROT // 2, 1)


def _qk_fwd(raw, g, tabs, width, name):
    S = raw.shape[0]
    nh = width // DH
    tm = _tile(S, 256, 16)

    def body(x_ref, g_ref, c_ref, sa_ref, sb_ref, o_ref):
        c, sa, sb = c_ref[...], sa_ref[...], sb_ref[...]
        for h in range(nh):
            xv = x_ref[:, h * DH:(h + 1) * DH].astype(f32)
            r = lax.rsqrt(jnp.mean(xv * xv, axis=-1, keepdims=True) + EPS)
            o_ref[:, h * DH:(h + 1) * DH] = _rope(xv * r * g_ref[...], c, sa, sb).astype(o_ref.dtype)

    row = pl.BlockSpec((tm, width), lambda i: (i, 0))
    tab = pl.BlockSpec((tm, DH), lambda i: (i, 0))
    return pl.pallas_call(body, name=name, grid=(S // tm,), in_specs=[row, _acc_spec(DH), tab, tab, tab], out_specs=row,
                          out_shape=jax.ShapeDtypeStruct((S, width), bf16), compiler_params=_cp((PAR,)))(raw, g, *tabs)


def _qk_bwd(dparts, raw, g, tabs, width, extra, name):
    S = raw.shape[0]
    nh = width // DH
    ow = dparts[0].shape[1]
    hpg = ow // DH
    tm = _tile(S, 256, 16)
    wout = width + (len(extra) * ow if extra else 0)

    def body(*refs):
        d_refs = refs[:NG]
        x_ref, g_ref, c_ref, sa_ref, sb_ref = refs[NG:NG + 5]
        e_refs = refs[NG + 5:NG + 5 + len(extra)]
        o_ref, dg_ref = refs[NG + 5 + len(extra):]
        i = pl.program_id(0)

        @pl.when(i == 0)
        def _():
            dg_ref[...] = jnp.zeros_like(dg_ref)

        c, sa, sb = c_ref[...], sa_ref[...], sb_ref[...]
        gv = g_ref[...]
        dg = jnp.zeros((1, DH), f32)
        for h in range(nh):
            dout = d_refs[h // hpg][:, (h % hpg) * DH:(h % hpg + 1) * DH].astype(f32)
            xv = x_ref[:, h * DH:(h + 1) * DH].astype(f32)
            r = lax.rsqrt(jnp.mean(xv * xv, axis=-1, keepdims=True) + EPS)
            xh = xv * r
            dn = _rope_t(dout, c, sa, sb)
            dg = dg + jnp.sum(dn * xh, axis=0, keepdims=True)
            dxn = dn * gv
            dx = r * (dxn - xh * jnp.mean(dxn * xh, axis=-1, keepdims=True))
            o_ref[:, h * DH:(h + 1) * DH] = dx.astype(o_ref.dtype)
        for e, e_ref in enumerate(e_refs):
            o_ref[:, width + e * ow:width + (e + 1) * ow] = e_ref[...]
        dg_ref[...] += dg

    part = pl.BlockSpec((tm, ow), lambda i: (i, 0))
    tab = pl.BlockSpec((tm, DH), lambda i: (i, 0))
    return pl.pallas_call(
        body, name=name, grid=(S // tm,),
        in_specs=[part] * NG + [pl.BlockSpec((tm, width), lambda i: (i, 0)), _acc_spec(DH), tab, tab, tab] + [part] * len(extra),
        out_specs=[pl.BlockSpec((tm, wout), lambda i: (i, 0)), _acc_spec(DH)],
        out_shape=[jax.ShapeDtypeStruct((S, wout), bf16), jax.ShapeDtypeStruct((1, DH), f32)],
        compiler_params=_cp((ARB,)))(*dparts, raw, g, *tabs, *extra)


def _dot_nt(a, b):
    return lax.dot_general(a, b, (((1,), (1,)), ((), ())), preferred_element_type=f32)


def _dot_tn(a, b):
    return lax.dot_general(a, b, (((0,), (0,)), ((), ())), preferred_element_type=f32)


def _band_masks():
    qi = lax.broadcasted_iota(jnp.int32, (BLK, BLK), 0)
    ki = lax.broadcasted_iota(jnp.int32, (BLK, BLK), 1)
    return ki <= qi, ki >= qi


def _attn_fwd(q, k, kv, g, r, name):
    S = q.shape[0]
    ow = q.shape[1] // NG
    hpg = ow // DH
    sr = S // r
    nb = sr // BLK
    scale = 1.0 / math.sqrt(DH)

    def body(q_ref, kc_ref, kp_ref, vc_ref, vp_ref, o_ref, l_ref):
        n = pl.program_id(1)
        m_cur, m_prev = _band_masks()
        m_prev = m_prev & (n > 0)
        for h in range(hpg):
            hs = slice(h * DH, (h + 1) * DH)
            qh = q_ref[:, hs]
            s_c = jnp.where(m_cur, _dot_nt(qh, kc_ref[:, hs]) * scale, NEG)
            s_p = jnp.where(m_prev, _dot_nt(qh, kp_ref[:, hs]) * scale, NEG)
            mx = jnp.maximum(jnp.max(s_c, axis=-1, keepdims=True), jnp.max(s_p, axis=-1, keepdims=True))
            p_c = jnp.exp(s_c - mx)
            p_p = jnp.exp(s_p - mx)
            den = jnp.sum(p_c, axis=-1, keepdims=True) + jnp.sum(p_p, axis=-1, keepdims=True)
            o = jnp.dot(p_c.astype(bf16), vc_ref[:, hs], preferred_element_type=f32)
            o = o + jnp.dot(p_p.astype(bf16), vp_ref[:, hs], preferred_element_type=f32)
            o_ref[:, hs] = (o / den).astype(o_ref.dtype)
            l_ref[:, hs] = jnp.broadcast_to(mx + jnp.log(den), (BLK, DH))

    qv = q.reshape(sr, r * NG * ow)
    kview = k.reshape(sr, r * NG * ow)
    vview = kv.reshape(sr, r * 2 * NG * ow)
    cur = lambda j, n: (n, j * NG + g)
    prev = lambda j, n: (jnp.maximum(n - 1, 0), j * NG + g)
    vcur = lambda j, n: (n, j * 2 * NG + NG + g)
    vprev = lambda j, n: (jnp.maximum(n - 1, 0), j * 2 * NG + NG + g)
    blk = lambda f: pl.BlockSpec((BLK, ow), f)
    o, lse = pl.pallas_call(
        body, name=name, grid=(r, nb), in_specs=[blk(cur), blk(cur), blk(prev), blk(vcur), blk(vprev)],
        out_specs=[blk(lambda j, n: (n, j)), blk(lambda j, n: (n, j))],
        out_shape=[jax.ShapeDtypeStruct((sr, r * ow), bf16), jax.ShapeDtypeStruct((sr, r * ow), f32)],
        compiler_params=_cp((PAR, PAR)))(qv, kview, kview, vview, vview)
    return o.reshape(S, ow), lse.reshape(S, ow)


def _attn_bwd_q(q, k, kv, do_g, lse, corr, g, r, name, comms=()):
    S = q.shape[0]
    ow = q.shape[1] // NG
    hpg = ow // DH
    sr = S // r
    nb = sr // BLK
    scale = 1.0 / math.sqrt(DH)

    def body(q_ref, kc_ref, kp_ref, vc_ref, vp_ref, do_ref, l_ref, c_ref, dq_ref):
        n = pl.program_id(1)
        m_cur, m_prev = _band_masks()
        m_prev = m_prev & (n > 0)
        for h in range(hpg):
            hs = slice(h * DH, (h + 1) * DH)
            qh, doh = q_ref[:, hs], do_ref[:, hs]
            ls = slice(h * DH, h * DH + BLK)
            lh, ch = l_ref[:, ls], c_ref[:, ls]
            dq = jnp.zeros((BLK, DH), f32)
            for k_ref, v_ref, msk in ((kc_ref, vc_ref, m_cur), (kp_ref, vp_ref, m_prev)):
                kh = k_ref[:, hs]
                s = jnp.where(msk, _dot_nt(qh, kh) * scale, NEG)
                p = jnp.exp(s - lh)
                dsc = p * (_dot_nt(doh, v_ref[:, hs]) + ch)
                dq = dq + jnp.dot(dsc.astype(bf16), kh, preferred_element_type=f32)
            dq_ref[:, hs] = (dq * scale).astype(dq_ref.dtype)

    qv = q.reshape(sr, r * NG * ow)
    kview = k.reshape(sr, r * NG * ow)
    vview = kv.reshape(sr, r * 2 * NG * ow)
    cur = lambda j, n: (n, j * NG + g)
    prev = lambda j, n: (jnp.maximum(n - 1, 0), j * NG + g)
    vcur = lambda j, n: (n, j * 2 * NG + NG + g)
    vprev = lambda j, n: (jnp.maximum(n - 1, 0), j * 2 * NG + NG + g)
    own = lambda j, n: (n, j)
    blk = lambda f: pl.BlockSpec((BLK, ow), f)
    res = _pcall(
        body, name=name, grid=(r, nb),
        in_specs=[blk(cur), blk(cur), blk(prev), blk(vcur), blk(vprev), blk(own), blk(own), blk(own)],
        out_specs=blk(own), out_shape=jax.ShapeDtypeStruct((sr, r * ow), bf16), sem=(PAR, PAR), comms=comms,
        args=[qv, kview, kview, vview, vview, do_g.reshape(sr, r * ow), lse.reshape(sr, r * ow), corr.reshape(sr, r * ow)])
    if comms:
        return res[0].reshape(S, ow), res[1]
    return res.reshape(S, ow)


def _attn_bwd_kv(q, k, kv, do_g, lse, corr, g, r, name):
    S = q.shape[0]
    ow = q.shape[1] // NG
    hpg = ow // DH
    sr = S // r
    nb = sr // BLK
    scale = 1.0 / math.sqrt(DH)

    def body(k_ref, v_ref, qc_ref, qn_ref, doc_ref, don_ref, lc_ref, ln_ref, cc_ref, cn_ref, dk_ref, dv_ref):
        n = pl.program_id(1)
        m_cur, m_prev = _band_masks()
        m_next = m_prev & (n < nb - 1)
        for h in range(hpg):
            hs = slice(h * DH, (h + 1) * DH)
            ls = slice(h * DH, h * DH + BLK)
            kh, vh = k_ref[:, hs], v_ref[:, hs]
            dk = jnp.zeros((BLK, DH), f32)
            dv = jnp.zeros((BLK, DH), f32)
            for q_ref, do_ref, l_ref, c_ref, msk in ((qc_ref, doc_ref, lc_ref, cc_ref, m_cur),
                                                     (qn_ref, don_ref, ln_ref, cn_ref, m_next)):
                qh, doh = q_ref[:, hs], do_ref[:, hs]
                s = jnp.where(msk, _dot_nt(qh, kh) * scale, NEG)
                p = jnp.exp(s - l_ref[:, ls])
                dv = dv + _dot_tn(p.astype(bf16), doh)
                dsc = p * (_dot_nt(doh, vh) + c_ref[:, ls])
                dk = dk + _dot_tn(dsc.astype(bf16), qh)
            dk_ref[:, hs] = (dk * scale).astype(dk_ref.dtype)
            dv_ref[:, hs] = dv.astype(dv_ref.dtype)

    qv = q.reshape(sr, r * NG * ow)
    kview = k.reshape(sr, r * NG * ow)
    vview = kv.reshape(sr, r * 2 * NG * ow)
    cur = lambda j, n: (n, j * NG + g)
    nxt = lambda j, n: (jnp.minimum(n + 1, nb - 1), j * NG + g)
    vcur = lambda j, n: (n, j * 2 * NG + NG + g)
    own = lambda j, n: (n, j)
    ownn = lambda j, n: (jnp.minimum(n + 1, nb - 1), j)
    blk = lambda f: pl.BlockSpec((BLK, ow), f)
    dov, lv, cv = do_g.reshape(sr, r * ow), lse.reshape(sr, r * ow), corr.reshape(sr, r * ow)
    shp = jax.ShapeDtypeStruct((sr, r * ow), bf16)
    dk, dv = pl.pallas_call(
        body, name=name, grid=(r, nb),
        in_specs=[blk(cur), blk(vcur), blk(cur), blk(nxt), blk(own), blk(ownn), blk(own), blk(ownn), blk(own), blk(ownn)],
        out_specs=[blk(own), blk(own)], out_shape=[shp, shp],
        compiler_params=_cp((PAR, PAR)))(kview, vview, qv, qv, dov, dov, lv, lv, cv, cv)
    return dk.reshape(S, ow), dv.reshape(S, ow)


def _mix_weights(l_refs):
    ls = [l[...] for l in l_refs]
    mx = functools.reduce(jnp.maximum, ls)
    es = [jnp.exp(l - mx) for l in ls]
    den = functools.reduce(lambda a, b: a + b, es)
    return [e / den for e in es]


def _combine_fwd(os_, lses, name):
    S, ow = os_[0].shape
    tm = _tile(S, 256, 16)

    def body(*refs):
        o_refs, l_refs, out_ref = refs[:NG], refs[NG:2 * NG], refs[2 * NG]
        al = _mix_weights(l_refs)
        acc = al[0] * o_refs[0][...].astype(f32)
        for gi in range(1, NG):
            acc = acc + al[gi] * o_refs[gi][...].astype(f32)
        out_ref[...] = acc.astype(out_ref.dtype)

    row = pl.BlockSpec((tm, ow), lambda i: (i, 0))
    return pl.pallas_call(body, name=name, grid=(S // tm,), in_specs=[row] * (2 * NG), out_specs=row,
                          out_shape=jax.ShapeDtypeStruct((S, ow), bf16), compiler_params=_cp((PAR,)))(*os_, *lses)


def _combine_bwd(do, os_, lses, name, comms=()):
    S, ow = do.shape
    hpg = ow // DH
    tm = _tile(S, 256, 16)

    def body(*refs):
        do_ref = refs[0]
        o_refs, l_refs = refs[1:1 + NG], refs[1 + NG:1 + 2 * NG]
        dog_refs, c_refs = refs[1 + 2 * NG:1 + 3 * NG], refs[1 + 3 * NG:]
        al = _mix_weights(l_refs)
        dov = do_ref[...]
        o = al[0] * o_refs[0][...].astype(f32)
        for gi in range(1, NG):
            o = o + al[gi] * o_refs[gi][...].astype(f32)
        prod = dov * o
        t = jnp.concatenate(
            [jnp.broadcast_to(jnp.sum(prod[:, h * DH:(h + 1) * DH], axis=-1, keepdims=True), (tm, DH)) for h in range(hpg)],
            axis=1)
        for gi in range(NG):
            dog_refs[gi][...] = (al[gi] * dov).astype(dog_refs[gi].dtype)
            c_refs[gi][...] = -(al[gi] * t)

    row = pl.BlockSpec((tm, ow), lambda i: (i, 0))
    return _pcall(
        body, name=name, grid=(S // tm,), in_specs=[row] * (1 + 2 * NG), out_specs=[row] * (2 * NG),
        out_shape=[jax.ShapeDtypeStruct((S, ow), bf16)] * NG + [jax.ShapeDtypeStruct((S, ow), f32)] * NG,
        args=[do, *os_, *lses], sem=(PAR,), comms=comms)


def _pad_rows(w, rows):
    return jnp.concatenate([w, jnp.zeros((rows - w.shape[0], w.shape[1]), w.dtype)], axis=0)


def kernel(x, c, positions, mod_w, mod_b, norm_mix_g, norm_ffn_g, conv_pw1_w, conv_pw1_b, conv_dw_w, conv_dw_b, conv_ln_g, conv_ln_b, conv_pw2_w, conv_pw2_b, kv_mod_w, kv_mod_b, kv_norm_g, w_kv, k_norm_g, w_q, q_norm_g, w_o, ffn_up_w, ffn_dw_w, ffn_dw_b, ffn_down_w, loss_target, m_mod_w, m_mod_b, m_norm_mix_g, m_norm_ffn_g, m_conv_pw1_w, m_conv_pw1_b, m_conv_dw_w, m_conv_dw_b, m_conv_ln_g, m_conv_ln_b, m_conv_pw2_w, m_conv_pw2_b, m_kv_mod_w, m_kv_mod_b, m_kv_norm_g, m_w_kv, m_k_norm_g, m_w_q, m_q_norm_g, m_w_o, m_ffn_up_w, m_ffn_dw_w, m_ffn_dw_b, m_ffn_down_w, v_mod_w, v_mod_b, v_norm_mix_g, v_norm_ffn_g, v_conv_pw1_w, v_conv_pw1_b, v_conv_dw_w, v_conv_dw_b, v_conv_ln_g, v_conv_ln_b, v_conv_pw2_w, v_conv_pw2_b, v_kv_mod_w, v_kv_mod_b, v_kv_norm_g, v_w_kv, v_k_norm_g, v_w_q, v_q_norm_g, v_w_o, v_ffn_up_w, v_ffn_dw_w, v_ffn_dw_b, v_ffn_down_w):
    S, Dm = x.shape[1], x.shape[2]
    F = ffn_dw_b.shape[1]
    QW = NG * HPG * DH
    OW = HPG * DH
    mx, my, mc = _me()
    me = 4 * mx + 2 * my + mc
    core = jnp.reshape(mc, (1,)).astype(jnp.int32)
    chip = jnp.reshape(2 * mx + my, (1,)).astype(jnp.int32)
    x0 = x.reshape(S, Dm)
    target = loss_target.reshape(S, Dm)

    c_all = _ag_small(c, "ag_c").reshape(NDEV, Dm)
    n_mod = mod_w.shape[2]
    n_kvm = kv_mod_w.shape[1]
    b0 = lax.dynamic_slice(mod_b, (0, me * n_mod), (1, n_mod))
    b1 = lax.dynamic_slice(mod_b, (1, me * n_mod), (1, n_mod))
    bk = lax.dynamic_slice(kv_mod_b.reshape(1, -1), (0, me * n_kvm), (1, n_kvm))
    m_part = jnp.concatenate([_modproj(c_all, mod_w[0], b0, "modproj0"), _modproj(c_all, mod_w[1], b1, "modproj1"),
                              _modproj(c_all, kv_mod_w, bk, "modproj_kv")], axis=1)
    m_all = _ag_small(m_part, "ag_mod")
    m_mine = lax.dynamic_index_in_dim(m_all, me, axis=1, keepdims=False)
    mod0 = m_mine[:, :n_mod].reshape(6, Dm)
    mod1 = m_mine[:, n_mod:2 * n_mod].reshape(6, Dm)
    modkv = m_mine[:, 2 * n_mod:].reshape(2, Dm)
    row = lambda a, i: a[i:i + 1]

    as3 = lambda w: w if w.ndim == 3 else w[None]
    sh16 = lambda w: as3(w).astype(bf16)
    W_pw1 = _ag_big(sh16(conv_pw1_w), 2, "ag_pw1")
    ag_pw2 = _comm_allgather(sh16(conv_pw2_w), 1)
    ag_up = [_comm_allgather(sh16(ffn_up_w[l]), 2) for l in range(2)]
    ag_down = [_comm_allgather(sh16(ffn_down_w[l]), 1) for l in range(2)]
    ag_kv = _comm_allgather(sh16(w_kv), 2)
    ag_q = _comm_allgather(sh16(w_q), 2)
    ag_o = _comm_allgather(sh16(w_o), 2)

    sp_flat = jnp.concatenate([conv_pw1_b.reshape(-1), conv_dw_b.reshape(-1), conv_ln_g.reshape(-1), conv_ln_b.reshape(-1),
                               conv_pw2_b.reshape(-1), conv_dw_w.reshape(-1), ffn_dw_w.reshape(-1)])
    sp_rows = -(-sp_flat.shape[0] // 1024) * 8
    sp_flat = jnp.concatenate([sp_flat, jnp.zeros((sp_rows * 128 - sp_flat.shape[0],), f32)]).reshape(sp_rows, 128)
    n1, nd = conv_pw1_b.shape[1], conv_dw_b.shape[1]
    nfw = ffn_dw_w.shape[2]
    sp = _ag_small(sp_flat, "ag_small_params").reshape(NDEV, -1)
    off = 0
    pw1_b = sp[:, off:off + n1].reshape(1, -1); off += n1
    dw_b = sp[:, off:off + nd].reshape(1, -1); off += nd
    ln_g = sp[:, off:off + nd].reshape(1, -1); off += nd
    ln_b = sp[:, off:off + nd].reshape(1, -1); off += nd
    pw2_b = sp[:, off:off + nd].reshape(1, -1); off += nd
    dw_w = jnp.transpose(sp[:, off:off + CONV_K * nd].reshape(NDEV, CONV_K, nd), (1, 0, 2)).reshape(CONV_K, -1); off += CONV_K * nd
    fdw_w = jnp.transpose(sp[:, off:off + 2 * FFN_K * nfw].reshape(NDEV, 2, FFN_K, nfw), (1, 2, 0, 3)).reshape(2, FFN_K, -1)
    dw_w32 = _pad_rows(dw_w, 32)

    tabs = _rope_tables(positions.reshape(S, 1), "rope_tables")

    def with_comms(res, comms):
        return res if comms else (res, [])

    def rs_d2d(dwb):
        return [_comm_rs_sibling(dwb)]

    def rs_add(dwb, couts, tag):
        return _chip_partial(dwb, couts[0][0], core, f"rs_add_{tag}")

    def rs_ici(part):
        return [_comm_rs_chips(part)]

    def ffn_forward(xin, l, modv, w_up, w_down, up_comms, gate_comms, down_comms):
        h2 = _mod_fwd(xin, row(norm_ffn_g, l), row(modv, 3), row(modv, 4), f"ffn{l}_mod")
        u2, c_up = with_comms(_mm_nn(h2, w_up, 0, name=f"ffn{l}_up", comms=up_comms), up_comms)
        if w_down is None:
            w_down, c_up = c_up[0][0], c_up[1:]
        fw8 = _pad_rows(fdw_w[l], 8)
        act, c_gate = with_comms(_ffn_gate_fwd(u2, fw8, row(ffn_dw_b, l), f"ffn{l}_gate", comms=gate_comms), gate_comms)
        (xout, f), c_down = with_comms(
            _mm_nn(act, w_down, 0, name=f"ffn{l}_down", res=xin, gate=row(modv, 5), tk=1408, comms=down_comms), down_comms)
        return xout, (h2, u2, act, f, fw8, w_up, w_down), c_up, c_gate, c_down

    def ffn_backward(dx, xin, l, modv, saved, dact_comms):
        h2, u2, act, f, fw8, w_up, w_down = saved
        df, dgate, _ = _gate_bwd(dx, f, row(modv, 5), f"ffn{l}_gate_bwd")
        dact, c_dact = with_comms(_mm_nt(df, w_down, 0, name=f"ffn{l}_dact", out_dtype=bf16, tko=512, comms=dact_comms), dact_comms)
        d_down = _mm_tn(act, df, name=f"ffn{l}_ddown", col_sharded=False)
        (du2, d_fw, d_fb), c1 = _ffn_gate_bwd(u2, dact, fw8, row(ffn_dw_b, l), f"ffn{l}_gatebwd", comms=rs_d2d(d_down))
        part_down = rs_add(d_down, c1, f"down{l}")
        dh2, c2 = _mm_nt(du2, w_up, 0, name=f"ffn{l}_dh", out_dtype=f32, comms=rs_ici(part_down))
        d_up = _mm_tn(h2, du2, name=f"ffn{l}_dup", col_sharded=True)
        (dxin, dsh, dsc, dg), c3 = _mod_bwd(dh2, xin, dx, row(norm_ffn_g, l), row(modv, 4), f"ffn{l}_mod_bwd", comms=rs_d2d(d_up))
        part_up = rs_add(d_up, c3, f"up{l}")
        grads = dict(d_fw=d_fw[:FFN_K], d_fb=d_fb, dsh=dsh, dsc=dsc, dgate=dgate, dg=dg,
                     down=(part_down, c2[0][0]), part_up=part_up)
        return dxin, grads, c_dact

    h0 = _mod_fwd(x0, row(norm_mix_g, 0), row(mod0, 0), row(mod0, 1), "l0_mod")
    u0, c = _mm_nn(h0, W_pw1, 0, name="l0_pw1", bias=pw1_b, comms=[ag_pw2])
    W_pw2 = c[0][0]
    s0, c = _conv_fwd(u0, dw_w32, dw_b, ln_g, ln_b, "l0_conv", comms=[ag_up[0]])
    W_up0 = c[0][0]
    x1, f0 = _mm_nn(s0, W_pw2, 0, name="l0_pw2", bias=pw2_b, res=x0, gate=row(mod0, 2))
    x2, ffn0_saved, _, c_gate, c_down = ffn_forward(x1, 0, mod0, W_up0, None, [ag_down[0]], [ag_kv], [ag_q, ag_o])
    W_kv, W_q, W_o = c_gate[0][0], c_down[0][0], c_down[1][0]

    hkv = _mod_fwd(x2, kv_norm_g.reshape(1, -1), row(modkv, 0), row(modkv, 1), "kv_mod")
    kvraw, c = _mm_nn(hkv, W_kv, 0, name="kv_proj", comms=[ag_up[1]])
    W_up1 = c[0][0]
    kg = k_norm_g.reshape(1, -1)
    kk = _qk_fwd(kvraw, kg, tabs, QW, "k_norm_rope")
    h1 = _mod_fwd(x2, row(norm_mix_g, 1), row(mod1, 0), row(mod1, 1), "l1_mod")
    qraw, c_q = _mm_nn(h1, W_q, 0, name="q_proj", comms=[ag_down[1]])
    qg = q_norm_g.reshape(1, -1)
    qq = _qk_fwd(qraw, qg, tabs, QW, "q_norm_rope")
    o_gs, lses = [], []
    for gi, r in enumerate(DILS):
        o_g, lse_g = _attn_fwd(qq, kk, kvraw, gi, r, f"attn_fwd{gi}")
        o_gs.append(o_g)
        lses.append(lse_g)
    o_mix = _combine_fwd(o_gs, lses, "attn_mix")
    x3, f1 = _mm_nn(o_mix, W_o, 0, name="o_proj", res=x2, gate=row(mod1, 2))
    x4, ffn1_saved, _, _, _ = ffn_forward(x3, 1, mod1, W_up1, c_q[0][0], (), (), ())

    dx4, loss_blk = _loss_grad(x4, target, "loss")
    loss = lax.psum(loss_blk[0, 0], ("x", "y", "c"))

    red = {}
    dx3, gf1, _ = ffn_backward(dx4, x3, 1, mod1, ffn1_saved, ())
    dy1, dgate_m1, _ = _gate_bwd(dx3, f1, row(mod1, 2), "l1_gate_bwd")
    do = _mm_nt(dy1, W_o, 0, name="o_proj_dx", out_dtype=f32, tko=1024)
    d_wo = _mm_tn(o_mix, dy1, name="o_proj_dw", col_sharded=True)
    outs, c = _combine_bwd(do, o_gs, lses, "attn_mix_bwd", comms=rs_d2d(d_wo))
    part_wo = rs_add(d_wo, c, "wo")
    do_gs, corrs = outs[:NG], outs[NG:]
    dq_gs, dk_gs, dv_gs = [], [], []
    for gi, r in enumerate(DILS):
        cm = rs_ici(part_wo) if gi == 0 else ()
        dq_g, c = with_comms(_attn_bwd_q(qq, kk, kvraw, do_gs[gi], lses[gi], corrs[gi], gi, r, f"attn_bwd_q{gi}", comms=cm), cm)
        if gi == 0:
            red["w_o"] = (part_wo, c[0][0])
        dq_gs.append(dq_g)
        dk_g, dv_g = _attn_bwd_kv(qq, kk, kvraw, do_gs[gi], lses[gi], corrs[gi], gi, r, f"attn_bwd_kv{gi}")
        dk_gs.append(dk_g)
        dv_gs.append(dv_g)
    dqraw, d_qg = _qk_bwd(dq_gs, qraw, qg, tabs, QW, (), "q_norm_rope_bwd")
    dkvraw, d_kg = _qk_bwd(dk_gs, kvraw, kg, tabs, QW, tuple(dv_gs), "k_norm_rope_bwd")
    dh1 = _mm_nt(dqraw, W_q, 0, name="q_proj_dx", out_dtype=f32)
    d_wq = _mm_tn(h1, dqraw, name="q_proj_dw", col_sharded=True)
    dhkv, c = _mm_nt(dkvraw, W_kv, 0, name="kv_proj_dx", out_dtype=f32, comms=rs_d2d(d_wq))
    part_wq = rs_add(d_wq, c, "wq")
    d_wkv, c = _mm_tn(hkv, dkvraw, name="kv_proj_dw", col_sharded=True, comms=rs_ici(gf1["part_up"]))
    red["ffn_up_w1"] = (gf1["part_up"], c[0][0])
    (dx2a, dsh_m1, dsc_m1, dg_mix1), c = _mod_bwd(dh1, x2, dx3, row(norm_mix_g, 1), row(mod1, 1), "l1_mod_bwd",
                                                  comms=rs_ici(part_wq))
    red["w_q"] = (part_wq, c[0][0])
    (dx2, dsh_kv, dsc_kv, dg_kvn), c = _mod_bwd(dhkv, x2, dx2a, kv_norm_g.reshape(1, -1), row(modkv, 1), "kv_mod_bwd",
                                                comms=rs_d2d(d_wkv))
    part_wkv = rs_add(d_wkv, c, "wkv")

    dx1, gf0, c = ffn_backward(dx2, x1, 0, mod0, ffn0_saved, rs_ici(part_wkv))
    red["w_kv"] = (part_wkv, c[0][0])
    dy0, dgate_m0, d_pw2b = _gate_bwd(dx1, f0, row(mod0, 2), "l0_gate_bwd")
    ds0 = _mm_nt(dy0, W_pw2, 0, name="l0_pw2_dx", out_dtype=bf16, tko=1024)
    d_pw2 = _mm_tn(s0, dy0, name="l0_pw2_dw", col_sharded=False)
    (dcv, d_lng, d_lnb, d_dwb, d_dww), c = _conv_bwd1(u0, ds0, dw_w32, dw_b, ln_g, ln_b, "l0_conv_bwd1",
                                                      comms=rs_ici(gf0["part_up"]))
    red["ffn_up_w0"] = (gf0["part_up"], c[0][0])
    (du0, d_pw1b), c = _conv_bwd2(dcv, u0, dw_w32, "l0_conv_bwd2", comms=rs_d2d(d_pw2))
    part_pw2 = rs_add(d_pw2, c, "pw2")
    dh0, c = _mm_nt(du0, W_pw1, 0, name="l0_pw1_dx", out_dtype=f32, comms=rs_ici(part_pw2))
    red["conv_pw2_w"] = (part_pw2, c[0][0])
    d_pw1 = _mm_tn(h0, du0, name="l0_pw1_dw", col_sharded=True)
    (grad_x, dsh_m0, dsc_m0, dg_mix0), c = _mod_bwd(dh0, x0, dx1, row(norm_mix_g, 0), row(mod0, 1), "l0_mod_bwd",
                                                    comms=rs_d2d(d_pw1))
    part_pw1 = rs_add(d_pw1, c, "pw1")
    red["ffn_down_w0"], red["ffn_down_w1"] = gf0["down"], gf1["down"]

    dm0 = [dsh_m0, dsc_m0, dgate_m0, gf0["dsh"], gf0["dsc"], gf0["dgate"]]
    dm1 = [dsh_m1, dsc_m1, dgate_m1, gf1["dsh"], gf1["dsc"], gf1["dgate"]]
    pieces = dm0 + dm1 + [dsh_kv, dsc_kv,
                          dg_mix0, dg_mix1, gf0["dg"], gf1["dg"], dg_kvn, d_kg, d_qg, gf0["d_fb"], gf1["d_fb"],
                          d_pw1b, d_dww[:CONV_K], d_dwb, d_lng, d_lnb, d_pw2b, gf0["d_fw"], gf1["d_fw"]]
    flat = jnp.concatenate([p.reshape(-1) for p in pieces])
    n_flat = flat.shape[0]
    n_rows = -(-n_flat // 1024) * 8
    flat = jnp.concatenate([flat, jnp.zeros((n_rows * 128 - n_flat,), f32)]).reshape(n_rows, 128)
    g_all = _ag_small(flat, "ag_small_grads")
    g_sum = _sum8(g_all, "sum_small_grads").reshape(-1)
    n_dm = 2 * 6 * Dm + 2 * Dm
    dm_all = g_all.reshape(NDEV, -1)[:, :n_dm]

    take_pos = [0]

    def take(shape):
        n = int(np.prod(shape))
        out = g_sum[take_pos[0]:take_pos[0] + n].reshape(shape)
        take_pos[0] += n
        return out

    g_mod_b = take((2, 6 * Dm))
    g_kv_mod_b = take((2 * Dm,))
    g_norm_mix0, g_norm_mix1 = take((Dm,)), take((Dm,))
    g_norm_ffn0, g_norm_ffn1 = take((Dm,)), take((Dm,))
    g_kv_norm = take((Dm,))
    g_k_norm = take((DH,))
    g_q_norm = take((1, DH))
    g_ffn_dw_b = take((2, F))
    shard = lambda full, n, axis: lax.dynamic_slice_in_dim(full, me * n, n, axis)
    g_pw1_b = shard(take((1, 2 * Dm)), n1, 1)
    g_dw_w = shard(take((1, CONV_K, Dm)), nd, 2)
    g_dw_b = shard(take((1, Dm)), nd, 1)
    g_ln_g = shard(take((1, Dm)), nd, 1)
    g_ln_b = shard(take((1, Dm)), nd, 1)
    g_pw2_b = shard(take((1, Dm)), nd, 1)
    g_ffn_dw_w = shard(jnp.stack([take((FFN_K, F)), take((FFN_K, F))]), nfw, 2)
    g_norm_mix = jnp.stack([g_norm_mix0, g_norm_mix1])
    g_norm_ffn = jnp.stack([g_norm_ffn0, g_norm_ffn1])

    small = [("mod_b", mod_b, m_mod_b, v_mod_b, g_mod_b), ("norm_mix_g", norm_mix_g, m_norm_mix_g, v_norm_mix_g, g_norm_mix),
             ("norm_ffn_g", norm_ffn_g, m_norm_ffn_g, v_norm_ffn_g, g_norm_ffn),
             ("conv_pw1_b", conv_pw1_b, m_conv_pw1_b, v_conv_pw1_b, g_pw1_b),
             ("conv_dw_w", conv_dw_w, m_conv_dw_w, v_conv_dw_w, g_dw_w), ("conv_dw_b", conv_dw_b, m_conv_dw_b, v_conv_dw_b, g_dw_b),
             ("conv_ln_g", conv_ln_g, m_conv_ln_g, v_conv_ln_g, g_ln_g), ("conv_ln_b", conv_ln_b, m_conv_ln_b, v_conv_ln_b, g_ln_b),
             ("conv_pw2_b", conv_pw2_b, m_conv_pw2_b, v_conv_pw2_b, g_pw2_b),
             ("kv_mod_b", kv_mod_b, m_kv_mod_b, v_kv_mod_b, g_kv_mod_b), ("kv_norm_g", kv_norm_g, m_kv_norm_g, v_kv_norm_g, g_kv_norm),
             ("k_norm_g", k_norm_g, m_k_norm_g, v_k_norm_g, g_k_norm), ("q_norm_g", q_norm_g, m_q_norm_g, v_q_norm_g, g_q_norm),
             ("ffn_dw_w", ffn_dw_w, m_ffn_dw_w, v_ffn_dw_w, g_ffn_dw_w), ("ffn_dw_b", ffn_dw_b, m_ffn_dw_b, v_ffn_dw_b, g_ffn_dw_b)]
    n_small = sum(int(np.prod(s[1].shape)) for s in small)
    rows_small = -(-n_small // 1024) * 8

    def pack(idx):
        fl = jnp.concatenate([s[idx].reshape(-1) for s in small])
        return jnp.concatenate([fl, jnp.ones((rows_small * 128 - n_small,), f32)]).reshape(rows_small, 128)

    sd, sm, sv = _adamw_plain(pack(1), pack(2), pack(3), pack(4), "adamw_small")
    res = {}
    pos = 0
    for name, w, _, _, g in small:
        n = int(np.prod(w.shape))
        cut = lambda a: a.reshape(-1)[pos:pos + n].reshape(w.shape)
        res[name] = (g.reshape(w.shape), cut(sd), cut(sm), cut(sv))
        pos += n

    c_all_t = jnp.transpose(c_all)

    def mod_update(w2d, m2d, v2d, dm_cols, tag):
        g = _modgrad(c_all_t, dm_cols, f"modgrad_{tag}")
        d, m2, v2 = _adamw_plain(w2d, m2d, v2d, g, f"adamw_{tag}")
        return g, d, m2, v2

    mw = []
    for l in range(2):
        cols = lax.dynamic_slice_in_dim(dm_all[:, l * 6 * Dm:(l + 1) * 6 * Dm], me * n_mod, n_mod, 1)
        mw.append(mod_update(mod_w[l], m_mod_w[l], v_mod_w[l], cols, f"mod_w{l}"))
    res["mod_w"] = tuple(jnp.stack([mw[0][i], mw[1][i]]) for i in range(4))
    cols = lax.dynamic_slice_in_dim(dm_all[:, 12 * Dm:], me * n_kvm, n_kvm, 1)
    res["kv_mod_w"] = mod_update(kv_mod_w, m_kv_mod_w, v_kv_mod_w, cols, "kv_mod_w")

    def mine(part):
        return lax.dynamic_index_in_dim(part, chip[0], 0, keepdims=False)

    def big(key, w, m, v, l, prev, tag, comms=()):
        part, r2 = red[key]
        return _adamw_reduced(as3(w), as3(m), as3(v), mine(part), r2, l, prev, f"adamw_{tag}", comms=comms)

    up1, c = big("ffn_up_w1", ffn_up_w, m_ffn_up_w, v_ffn_up_w, 1, None, "up1", comms=rs_ici(part_pw1))
    red["conv_pw1_w"] = (part_pw1, c[0][0])
    res["ffn_up_w"] = tuple(big("ffn_up_w0", ffn_up_w, m_ffn_up_w, v_ffn_up_w, 0, up1, "up0"))
    down1 = big("ffn_down_w1", ffn_down_w, m_ffn_down_w, v_ffn_down_w, 1, None, "down1")
    res["ffn_down_w"] = tuple(big("ffn_down_w0", ffn_down_w, m_ffn_down_w, v_ffn_down_w, 0, down1, "down0"))
    for key, w, m, v in (("conv_pw1_w", conv_pw1_w, m_conv_pw1_w, v_conv_pw1_w), ("conv_pw2_w", conv_pw2_w, m_conv_pw2_w, v_conv_pw2_w),
                         ("w_kv", w_kv, m_w_kv, v_w_kv), ("w_q", w_q, m_w_q, v_w_q), ("w_o", w_o, m_w_o, v_w_o)):
        res[key] = tuple(o.reshape(w.shape) for o in big(key, w, m, v, 0, None, key))

    order = ["mod_w", "mod_b", "norm_mix_g", "norm_ffn_g", "conv_pw1_w", "conv_pw1_b", "conv_dw_w", "conv_dw_b", "conv_ln_g",
             "conv_ln_b", "conv_pw2_w", "conv_pw2_b", "kv_mod_w", "kv_mod_b", "kv_norm_g", "w_kv", "k_norm_g", "w_q", "q_norm_g",
             "w_o", "ffn_up_w", "ffn_dw_w", "ffn_dw_b", "ffn_down_w"]
    out = [loss, grad_x.reshape(x.shape)]
    for i in range(4):
        out += [res[n][i] for n in order]
    return tuple(out)
```

```python
import functools
import math

import numpy as np
import jax
import jax.numpy as jnp
from jax import lax
from jax.experimental import pallas as pl
from jax.experimental.pallas import tpu as pltpu

f32 = jnp.float32
bf16 = jnp.bfloat16

D = 2048
SEQ = 8192
FF = 5632
CONV_K = 31
FFN_K = 3
HPG = 8
DH = 128
NG = 3
DILS = (1, 4, 16)
BLK = 128
ROT = 32
THETA = 500000.0
EPS = 1e-6
NEG = -1e30
NDEV = 8
HALO = 32
FHALO = 16

LR, B1, B2, AEPS, WD, STEP = 0.001, 0.9, 0.999, 1e-08, 0.01, 10

VMEM_BIG = 56 * 1024 * 1024

ARB = "arbitrary"
PAR = "parallel"
MESH = pl.DeviceIdType.MESH


def _cp(sem, vmem=None):
    return pltpu.CompilerParams(dimension_semantics=sem, vmem_limit_bytes=vmem)


def _tile(n, pref, mult=128):
    if n <= pref:
        return n
    t = (pref // mult) * mult
    while t >= mult:
        if n % t == 0:
            return t
        t -= mult
    return n


def _sigmoid(x):
    return 1.0 / (1.0 + jnp.exp(-x))


def _me():
    return lax.axis_index("x"), lax.axis_index("y"), lax.axis_index("c")


class _Comm:
    def __init__(self, arrays, out_shapes, sems, start, finish):
        self.arrays, self.out_shapes, self.sems, self.start, self.finish = arrays, out_shapes, sems, start, finish


def _pcall(body, *, name, grid, in_specs, out_specs, out_shape, args, scratch_shapes=(), sem=None, vmem=None, comms=(),
           aliases=None):
    aliases = aliases or {}
    if not comms:
        return pl.pallas_call(body, name=name, grid=grid, in_specs=in_specs, out_specs=out_specs, out_shape=out_shape,
                              scratch_shapes=list(scratch_shapes), input_output_aliases=aliases,
                              compiler_params=_cp(sem, vmem))(*args)
    single = not isinstance(out_shape, (list, tuple))
    outs_shape = [out_shape] if single else list(out_shape)
    outs_spec = [out_specs] if single else list(out_specs)
    n_in, n_out, n_scr = len(args), len(outs_shape), len(scratch_shapes)
    c_in = [a for cm in comms for a in cm.arrays]
    c_out = [s for cm in comms for s in cm.out_shapes]
    c_scr = [s for cm in comms for s in cm.sems]

    def split(refs, counts):
        out, pos = [], 0
        for n in counts:
            out.append(refs[pos:pos + n])
            pos += n
        return out

    def wrapped(*refs):
        ins, cins, outs, couts, scr, cscr = split(refs, [n_in, len(c_in), n_out, len(c_out), n_scr, len(c_scr)])
        ids = [pl.program_id(a) for a in range(len(grid))]
        first = functools.reduce(jnp.logical_and, [i == 0 for i in ids])
        last = functools.reduce(jnp.logical_and, [i == g - 1 for i, g in zip(ids, grid)])
        per_in = split(cins, [len(cm.arrays) for cm in comms])
        per_out = split(couts, [len(cm.out_shapes) for cm in comms])
        per_sem = split(cscr, [len(cm.sems) for cm in comms])

        @pl.when(first)
        def _():
            for cm, a, b, s in zip(comms, per_in, per_out, per_sem):
                cm.start(a, b, s)

        body(*ins, *outs, *scr)

        @pl.when(last)
        def _():
            for cm, a, b, s in zip(comms, per_in, per_out, per_sem):
                cm.finish(a, b, s)

    hbm = pl.BlockSpec(memory_space=pl.ANY)
    res = pl.pallas_call(
        wrapped, name=name, grid=grid, in_specs=list(in_specs) + [hbm] * len(c_in),
        out_specs=outs_spec + [hbm] * len(c_out), out_shape=outs_shape + c_out,
        scratch_shapes=list(scratch_shapes) + c_scr, input_output_aliases=aliases,
        compiler_params=_cp((ARB,) * len(grid), vmem))(*args, *c_in)
    main = res[0] if single else list(res[:n_out])
    return main, split(list(res[n_out:]), [len(cm.out_shapes) for cm in comms])


def _comm_allgather(w, axis):
    n = w.shape[axis]
    out_shape = list(w.shape)
    out_shape[axis] = NDEV * n

    def parts(ins, outs, sems):
        x_ref, out_ref = ins[0], outs[0]
        send_sems, recv_sems, local_sem = sems
        mx, my, mc = _me()
        chips = [(1 - mx, my), (mx, 1 - my), (1 - mx, 1 - my)]

        def blk(px, py, pc):
            start = pl.multiple_of((4 * px + 2 * py + pc) * n, n)
            if axis == 1:
                return out_ref.at[:, pl.ds(start, n), :]
            return out_ref.at[:, :, pl.ds(start, n)]

        def copy(k, block, to, src=None):
            return pltpu.make_async_remote_copy(
                src_ref=blk(*block) if src is None else src, dst_ref=blk(*block),
                send_sem=send_sems.at[k], recv_sem=recv_sems.at[k], device_id=to, device_id_type=MESH)

        me, sibling = (mx, my, mc), (mx, my, 1 - mc)
        mine = pltpu.make_async_copy(x_ref, blk(*me), local_sem)
        first = [copy(0, me, sibling, src=x_ref)] + [copy(1 + j, me, (*chip, mc), src=x_ref) for j, chip in enumerate(chips)]
        passed = [copy(4 + j, (*chip, mc), sibling) for j, chip in enumerate(chips)]
        return me, sibling, chips, mc, copy, mine, first, passed

    def start(ins, outs, sems):
        *_, mine, first, _ = parts(ins, outs, sems)
        mine.start()
        for cp in first:
            cp.start()

    def finish(ins, outs, sems):
        me, sibling, chips, mc, copy, mine, first, passed = parts(ins, outs, sems)
        for j, chip in enumerate(chips):
            copy(1 + j, (*chip, mc), me).wait_recv()
            passed[j].start()
        copy(0, sibling, me).wait_recv()
        for j, chip in enumerate(chips):
            copy(4 + j, (*chip, 1 - mc), me).wait_recv()
        for cp in first + passed:
            cp.wait_send()
        mine.wait()

    return _Comm([w], [jax.ShapeDtypeStruct(tuple(out_shape), w.dtype)],
                 [pltpu.SemaphoreType.DMA((7,)), pltpu.SemaphoreType.DMA((7,)), pltpu.SemaphoreType.DMA], start, finish)


def _comm_rs_sibling(dwb):
    def copies(ins, outs, sems):
        mx, my, mc = _me()
        return [pltpu.make_async_remote_copy(
            src_ref=ins[0].at[2 * p + (1 - mc)], dst_ref=outs[0].at[p], send_sem=sems[0].at[p], recv_sem=sems[1].at[p],
            device_id=(mx, my, 1 - mc), device_id_type=MESH) for p in range(4)]

    def start(ins, outs, sems):
        for cp in copies(ins, outs, sems):
            cp.start()

    def finish(ins, outs, sems):
        cps = copies(ins, outs, sems)
        for cp in cps:
            cp.wait_recv()
        for cp in cps:
            cp.wait_send()

    return _Comm([dwb], [jax.ShapeDtypeStruct((4,) + dwb.shape[1:], dwb.dtype)],
                 [pltpu.SemaphoreType.DMA((4,)), pltpu.SemaphoreType.DMA((4,))], start, finish)


def _comm_rs_chips(part):
    def copies(ins, outs, sems):
        mx, my, mc = _me()
        chips = [(1 - mx, my), (mx, 1 - my), (1 - mx, 1 - my)]
        return [pltpu.make_async_remote_copy(
            src_ref=ins[0].at[2 * px + py], dst_ref=outs[0].at[k], send_sem=sems[0].at[k], recv_sem=sems[1].at[k],
            device_id=(px, py, mc), device_id_type=MESH) for k, (px, py) in enumerate(chips)]

    def start(ins, outs, sems):
        for cp in copies(ins, outs, sems):
            cp.start()

    def finish(ins, outs, sems):
        cps = copies(ins, outs, sems)
        for cp in cps:
            cp.wait_recv()
        for cp in cps:
            cp.wait_send()

    return _Comm([part], [jax.ShapeDtypeStruct((3,) + part.shape[1:], part.dtype)],
                 [pltpu.SemaphoreType.DMA((3,)), pltpu.SemaphoreType.DMA((3,))], start, finish)


def _ag_small(x, name):
    r, c = x.shape

    def body(x_ref, out_ref, send_sems, recv_sems):
        mx, my, mc = _me()
        mine = 4 * mx + 2 * my + mc
        out_ref[mine] = x_ref[...]
        copies = []
        for k in range(1, NDEV):
            px = 1 - mx if (k >> 2) & 1 else mx
            py = 1 - my if (k >> 1) & 1 else my
            pc = 1 - mc if k & 1 else mc
            cp = pltpu.make_async_remote_copy(
                src_ref=x_ref, dst_ref=out_ref.at[mine], send_sem=send_sems.at[k - 1], recv_sem=recv_sems.at[k - 1],
                device_id=(px, py, pc), device_id_type=MESH)
            cp.start()
            copies.append((cp, 4 * px + 2 * py + pc))
        for k, (cp, peer) in enumerate(copies):
            pltpu.make_async_remote_copy(
                src_ref=x_ref, dst_ref=out_ref.at[peer], send_sem=send_sems.at[k], recv_sem=recv_sems.at[k],
                device_id=(mx, my, mc), device_id_type=MESH).wait_recv()
        for cp, _ in copies:
            cp.wait_send()

    return pl.pallas_call(
        body, name=name,
        out_shape=jax.ShapeDtypeStruct((NDEV, r, c), x.dtype),
        in_specs=[pl.BlockSpec(memory_space=pltpu.VMEM)],
        out_specs=pl.BlockSpec(memory_space=pltpu.VMEM),
        scratch_shapes=[pltpu.SemaphoreType.DMA((NDEV - 1,)), pltpu.SemaphoreType.DMA((NDEV - 1,))],
    )(x)


def _ag_big(w, axis, name):
    n = w.shape[axis]
    out_shape = list(w.shape)
    out_shape[axis] = NDEV * n

    def body(x_ref, out_ref, send_sems, recv_sems, local_sem):
        mx, my, mc = _me()
        me, sibling = (mx, my, mc), (mx, my, 1 - mc)
        chips = [(1 - mx, my), (mx, 1 - my), (1 - mx, 1 - my)]

        def blk(px, py, pc):
            start = pl.multiple_of((4 * px + 2 * py + pc) * n, n)
            if axis == 1:
                return out_ref.at[:, pl.ds(start, n), :]
            return out_ref.at[:, :, pl.ds(start, n)]

        def copy(k, block, to, src=None):
            return pltpu.make_async_remote_copy(
                src_ref=blk(*block) if src is None else src, dst_ref=blk(*block),
                send_sem=send_sems.at[k], recv_sem=recv_sems.at[k], device_id=to, device_id_type=MESH)

        mine = pltpu.make_async_copy(x_ref, blk(*me), local_sem)
        mine.start()
        first = [copy(0, me, sibling, src=x_ref)]
        first += [copy(1 + j, me, (*chip, mc), src=x_ref) for j, chip in enumerate(chips)]
        for cp in first:
            cp.start()
        passed = [copy(4 + j, (*chip, mc), sibling) for j, chip in enumerate(chips)]
        for j, chip in enumerate(chips):
            copy(1 + j, (*chip, mc), me).wait_recv()
            passed[j].start()
        copy(0, sibling, me).wait_recv()
        for j, chip in enumerate(chips):
            copy(4 + j, (*chip, 1 - mc), me).wait_recv()
        for cp in first + passed:
            cp.wait_send()
        mine.wait()

    return pl.pallas_call(
        body, name=name,
        out_shape=jax.ShapeDtypeStruct(tuple(out_shape), w.dtype),
        in_specs=[pl.BlockSpec(memory_space=pl.ANY)],
        out_specs=pl.BlockSpec(memory_space=pl.ANY),
        scratch_shapes=[pltpu.SemaphoreType.DMA((7,)), pltpu.SemaphoreType.DMA((7,)), pltpu.SemaphoreType.DMA],
    )(w)


def _chip_partial(dwb, r1, core, name):
    _, A, B = dwb.shape
    ta = _tile(A, 512, 16)

    def body(c_ref, a_ref, b_ref, o_ref):
        o_ref[...] = (a_ref[...].astype(f32) + b_ref[...].astype(f32)).astype(o_ref.dtype)

    grid_spec = pltpu.PrefetchScalarGridSpec(
        num_scalar_prefetch=1, grid=(4, A // ta),
        in_specs=[pl.BlockSpec((None, ta, B), lambda p, i, c: (2 * p + c[0], i, 0)),
                  pl.BlockSpec((None, ta, B), lambda p, i, c: (p, i, 0))],
        out_specs=pl.BlockSpec((None, ta, B), lambda p, i, c: (p, i, 0)))
    return pl.pallas_call(body, name=name, grid_spec=grid_spec,
                          out_shape=jax.ShapeDtypeStruct((4, A, B), dwb.dtype),
                          compiler_params=_cp((PAR, PAR)))(core, dwb, r1)


def _adam_math(w, g, m, v):
    m2 = B1 * m + (1.0 - B1) * g
    v2 = B2 * v + (1.0 - B2) * (g * g)
    m_hat = m2 / (1.0 - B1 ** STEP)
    v_hat = v2 / (1.0 - B2 ** STEP)
    delta = -LR * (m_hat / (jnp.sqrt(v_hat) + AEPS) + WD * w)
    return delta, m2, v2


def _adamw_reduced(w, m, v, mine, r2, l, prev, name, comms=()):
    L, A, B = w.shape
    ta = _tile(A, 256, 8)

    def body(w_ref, m_ref, v_ref, p_ref, r_ref, *rest):
        g_out, d_out, m_out, v_out = rest[-4:]
        g = ((p_ref[...].astype(f32) + r_ref[0].astype(f32)) + r_ref[1].astype(f32)) + r_ref[2].astype(f32)
        d, m2, v2 = _adam_math(w_ref[...], g, m_ref[...], v_ref[...])
        g_out[...] = g
        d_out[...] = d
        m_out[...] = m2
        v_out[...] = v2

    wspec = pl.BlockSpec((None, ta, B), lambda i: (l, i, 0))
    in_specs = [wspec, wspec, wspec, pl.BlockSpec((ta, B), lambda i: (i, 0)), pl.BlockSpec((3, ta, B), lambda i: (0, i, 0))]
    args = [w, m, v, mine, r2]
    aliases = {}
    if prev is not None:
        in_specs += [pl.BlockSpec(memory_space=pl.ANY)] * 4
        args += list(prev)
        aliases = {5 + i: i for i in range(4)}
    shp = jax.ShapeDtypeStruct((L, A, B), f32)
    return _pcall(body, name=name, grid=(A // ta,), in_specs=in_specs, out_specs=[wspec] * 4, out_shape=[shp] * 4,
                  args=args, sem=(PAR,), comms=comms, aliases=aliases)


def _adamw_plain(w, m, v, g, name):
    A, B = w.shape
    ta = _tile(A, 256, 8)

    def body(w_ref, m_ref, v_ref, g_ref, d_out, m_out, v_out):
        d, m2, v2 = _adam_math(w_ref[...], g_ref[...], m_ref[...], v_ref[...])
        d_out[...] = d
        m_out[...] = m2
        v_out[...] = v2

    spec = pl.BlockSpec((ta, B), lambda i: (i, 0))
    shp = jax.ShapeDtypeStruct((A, B), f32)
    return pl.pallas_call(body, name=name, grid=(A // ta,), in_specs=[spec] * 4, out_specs=[spec] * 3,
                          out_shape=[shp, shp, shp], compiler_params=_cp((PAR,)))(w, m, v, g)


def _sum8(g, name):
    _, R, C = g.shape

    def body(g_ref, o_ref):
        acc = g_ref[0]
        for j in range(1, NDEV):
            acc = acc + g_ref[j]
        o_ref[...] = acc

    return pl.pallas_call(body, name=name, out_shape=jax.ShapeDtypeStruct((R, C), f32))(g)


def _modproj(c_all, w, bias, name):
    K, N = w.shape
    tn = _tile(N, 512)

    def body(c_ref, w_ref, b_ref, o_ref):
        cc = c_ref[...]
        sc = (cc * _sigmoid(cc)).astype(bf16)
        o_ref[...] = jnp.dot(sc, w_ref[...].astype(bf16), preferred_element_type=f32) + b_ref[...]

    return pl.pallas_call(
        body, name=name, grid=(N // tn,),
        in_specs=[pl.BlockSpec((NDEV, K), lambda j: (0, 0)), pl.BlockSpec((K, tn), lambda j: (0, j)),
                  pl.BlockSpec((1, tn), lambda j: (0, j))],
        out_specs=pl.BlockSpec((NDEV, tn), lambda j: (0, j)),
        out_shape=jax.ShapeDtypeStruct((NDEV, N), f32), compiler_params=_cp((PAR,)))(c_all, w, bias)


def _modgrad(c_all_t, dm, name):
    K = c_all_t.shape[0]
    N = dm.shape[1]
    tn = _tile(N, 512)

    def body(c_ref, d_ref, o_ref):
        cc = c_ref[...]
        sc = cc * _sigmoid(cc)
        dmv = d_ref[...]
        acc = sc[:, 0:1] * dmv[0:1, :]
        for b in range(1, NDEV):
            acc = acc + sc[:, b:b + 1] * dmv[b:b + 1, :]
        o_ref[...] = acc

    return pl.pallas_call(
        body, name=name, grid=(N // tn,),
        in_specs=[pl.BlockSpec((K, NDEV), lambda j: (0, 0)), pl.BlockSpec((NDEV, tn), lambda j: (0, j))],
        out_specs=pl.BlockSpec((K, tn), lambda j: (0, j)),
        out_shape=jax.ShapeDtypeStruct((K, N), f32), compiler_params=_cp((PAR,)))(c_all_t, dm)


def _mm_nn(a, w, l, *, name, out_dtype=bf16, bias=None, res=None, gate=None, tm=1024, tn=1024, tk=2048, comms=()):
    M, K = a.shape
    N = w.shape[2]
    tm, tn, tk = _tile(M, tm, 8), _tile(N, tn), _tile(K, tk)
    nk = K // tk
    epi = res is not None

    def body(*refs):
        it = iter(refs)
        a_ref, w_ref = next(it), next(it)
        b_ref = next(it) if bias is not None else None
        r_ref = next(it) if epi else None
        g_ref = next(it) if epi else None
        o_ref = next(it)
        f_ref = next(it) if epi else None
        acc = next(it)
        k = pl.program_id(2)

        @pl.when(k == 0)
        def _():
            acc[...] = jnp.zeros_like(acc)

        acc[...] += jnp.dot(a_ref[...], w_ref[...], preferred_element_type=f32)

        @pl.when(k == nk - 1)
        def _():
            y = acc[...]
            if b_ref is not None:
                y = y + b_ref[...]
            if epi:
                f_ref[...] = y.astype(f_ref.dtype)
                o_ref[...] = r_ref[...] + g_ref[...] * y
            else:
                o_ref[...] = y.astype(o_ref.dtype)

    in_specs = [pl.BlockSpec((tm, tk), lambda i, j, k: (i, k)), pl.BlockSpec((None, tk, tn), lambda i, j, k: (l, k, j))]
    args = [a, w]
    if bias is not None:
        in_specs.append(pl.BlockSpec((1, tn), lambda i, j, k: (0, j)))
        args.append(bias)
    ospec = pl.BlockSpec((tm, tn), lambda i, j, k: (i, j))
    if epi:
        in_specs += [ospec, pl.BlockSpec((1, tn), lambda i, j, k: (0, j))]
        args += [res, gate]
        out_shape = [jax.ShapeDtypeStruct((M, N), f32), jax.ShapeDtypeStruct((M, N), bf16)]
        out_specs = [ospec, ospec]
    else:
        out_shape = jax.ShapeDtypeStruct((M, N), out_dtype)
        out_specs = ospec
    return _pcall(body, name=name, grid=(M // tm, N // tn, nk), in_specs=in_specs, out_specs=out_specs, out_shape=out_shape,
                  args=args, scratch_shapes=[pltpu.VMEM((tm, tn), f32)], sem=(PAR, PAR, ARB), vmem=VMEM_BIG, comms=comms)


def _mm_nt(a, w, l, *, name, out_dtype, tm=1024, tko=2048, tn=1024, comms=()):
    planes = a.ndim == 3
    M = a.shape[-2]
    K, N = w.shape[1], w.shape[2]
    npl = a.shape[-1]
    tm, tko = _tile(M, tm, 8), _tile(K, tko)
    tn = _tile(npl, tn)
    nn = N // tn
    per_plane = npl // tn

    def body(a_ref, w_ref, o_ref, acc):
        k = pl.program_id(2)

        @pl.when(k == 0)
        def _():
            acc[...] = jnp.zeros_like(acc)

        acc[...] += lax.dot_general(a_ref[...], w_ref[...], (((1,), (1,)), ((), ())), preferred_element_type=f32)

        @pl.when(k == nn - 1)
        def _():
            o_ref[...] = acc[...].astype(o_ref.dtype)

    if planes:
        a_spec = pl.BlockSpec((None, tm, tn), lambda i, j, k: (k // per_plane, i, k % per_plane))
    else:
        a_spec = pl.BlockSpec((tm, tn), lambda i, j, k: (i, k))
    return _pcall(body, name=name, grid=(M // tm, K // tko, nn),
                  in_specs=[a_spec, pl.BlockSpec((None, tko, tn), lambda i, j, k: (l, j, k))],
                  out_specs=pl.BlockSpec((tm, tko), lambda i, j, k: (i, j)),
                  out_shape=jax.ShapeDtypeStruct((M, K), out_dtype), args=[a, w],
                  scratch_shapes=[pltpu.VMEM((tm, tko), f32)], sem=(PAR, PAR, ARB), vmem=VMEM_BIG, comms=comms)


def _mm_tn(a, b, *, name, col_sharded, tk=2048, ts=1024, comms=()):
    planes = b.ndim == 3
    S, K = a.shape
    N = b.shape[-1] * (2 if planes else 1)
    ts = _tile(S, ts, 16)
    ns_steps = S // ts
    if col_sharded:
        tn = N // NDEV
        tk = _tile(K, tk)
    else:
        tn = N
        tk = _tile(K, 1408)
    per_plane = (b.shape[-1] // tn) if planes else 0

    def body(a_ref, b_ref, o_ref, acc):
        s = pl.program_id(2)

        @pl.when(s == 0)
        def _():
            acc[...] = jnp.zeros_like(acc)

        acc[...] += lax.dot_general(a_ref[...], b_ref[...], (((0,), (0,)), ((), ())), preferred_element_type=f32)

        @pl.when(s == ns_steps - 1)
        def _():
            o_ref[...] = acc[...].astype(o_ref.dtype)

    if planes:
        b_spec = pl.BlockSpec((None, ts, tn), lambda k, n, s: (n // per_plane, s, n % per_plane))
    else:
        b_spec = pl.BlockSpec((ts, tn), lambda k, n, s: (s, n))
    if col_sharded:
        out_shape = jax.ShapeDtypeStruct((NDEV, K, tn), bf16)
        out_spec = pl.BlockSpec((None, tk, tn), lambda k, n, s: (n, k, 0))
    else:
        out_shape = jax.ShapeDtypeStruct((K, N), bf16)
        out_spec = pl.BlockSpec((tk, tn), lambda k, n, s: (k, n))
    res = _pcall(body, name=name, grid=(K // tk, N // tn, ns_steps),
                 in_specs=[pl.BlockSpec((ts, tk), lambda k, n, s: (s, k)), b_spec],
                 out_specs=out_spec, out_shape=out_shape, args=[a, b],
                 scratch_shapes=[pltpu.VMEM((tk, tn), f32)], sem=(PAR, PAR, ARB), vmem=VMEM_BIG, comms=comms)
    out, couts = res if comms else (res, None)
    if not col_sharded:
        out = out.reshape(NDEV, K // NDEV, N)
    return (out, couts) if comms else out


def _acc_spec(w, rows=1):
    return pl.BlockSpec((rows, w), lambda i: (0, 0))


def _mod_fwd(x, g, sh, sc, name):
    S, W = x.shape
    tm = _tile(S, 256, 8)

    def body(x_ref, g_ref, sh_ref, sc_ref, h_ref):
        xv = x_ref[...]
        r = lax.rsqrt(jnp.mean(xv * xv, axis=-1, keepdims=True) + EPS)
        h_ref[...] = ((xv * r) * g_ref[...] * (1.0 + sc_ref[...]) + sh_ref[...]).astype(h_ref.dtype)

    row = pl.BlockSpec((tm, W), lambda i: (i, 0))
    return pl.pallas_call(body, name=name, grid=(S // tm,), in_specs=[row, _acc_spec(W), _acc_spec(W), _acc_spec(W)],
                          out_specs=row, out_shape=jax.ShapeDtypeStruct((S, W), bf16), compiler_params=_cp((PAR,)))(x, g, sh, sc)


def _mod_bwd(dh, x, dx_in, g, sc, name, comms=()):
    S, W = x.shape
    tm = _tile(S, 256, 8)
    nt = S // tm

    def body(dh_ref, x_ref, dxi_ref, g_ref, sc_ref, dx_ref, dsh_ref, dsc_ref, dg_ref):
        i = pl.program_id(0)

        @pl.when(i == 0)
        def _():
            dsh_ref[...] = jnp.zeros_like(dsh_ref)
            dsc_ref[...] = jnp.zeros_like(dsc_ref)

        xv = x_ref[...]
        dh = dh_ref[...].astype(f32)
        r = lax.rsqrt(jnp.mean(xv * xv, axis=-1, keepdims=True) + EPS)
        n = xv * r
        dn = dh * (g_ref[...] * (1.0 + sc_ref[...]))
        dx = r * (dn - n * jnp.mean(dn * n, axis=-1, keepdims=True))
        dx_ref[...] = dxi_ref[...] + dx
        dsh_ref[...] += jnp.sum(dh, axis=0, keepdims=True)
        dsc_ref[...] += jnp.sum(dh * n, axis=0, keepdims=True)

        @pl.when(i == nt - 1)
        def _():
            a2 = dsc_ref[...]
            dg_ref[...] = a2 * (1.0 + sc_ref[...])
            dsc_ref[...] = a2 * g_ref[...]

    row = pl.BlockSpec((tm, W), lambda i: (i, 0))
    vec = jax.ShapeDtypeStruct((1, W), f32)
    return _pcall(body, name=name, grid=(nt,), in_specs=[row, row, row, _acc_spec(W), _acc_spec(W)],
                  out_specs=[row, _acc_spec(W), _acc_spec(W), _acc_spec(W)],
                  out_shape=[jax.ShapeDtypeStruct((S, W), f32), vec, vec, vec], args=[dh, x, dx_in, g, sc], sem=(ARB,),
                  comms=comms)


def _gate_bwd(dx, f, gate, name):
    S, W = dx.shape
    tm = _tile(S, 256, 16)

    def body(dx_ref, f_ref, g_ref, df_ref, dg_ref, sdf_ref):
        i = pl.program_id(0)

        @pl.when(i == 0)
        def _():
            dg_ref[...] = jnp.zeros_like(dg_ref)
            sdf_ref[...] = jnp.zeros_like(sdf_ref)

        d = dx_ref[...]
        df = g_ref[...] * d
        df_ref[...] = df.astype(df_ref.dtype)
        dg_ref[...] += jnp.sum(d * f_ref[...].astype(f32), axis=0, keepdims=True)
        sdf_ref[...] += jnp.sum(df, axis=0, keepdims=True)

    row = pl.BlockSpec((tm, W), lambda i: (i, 0))
    vec = jax.ShapeDtypeStruct((1, W), f32)
    return pl.pallas_call(
        body, name=name, grid=(S // tm,), in_specs=[row, row, _acc_spec(W)], out_specs=[row, _acc_spec(W), _acc_spec(W)],
        out_shape=[jax.ShapeDtypeStruct((S, W), bf16), vec, vec], compiler_params=_cp((ARB,)))(dx, f, gate)


def _loss_grad(y, target, name):
    S, W = y.shape
    tm = _tile(S, 256, 8)

    def body(y_ref, t_ref, dy_ref, l_ref):
        i = pl.program_id(0)

        @pl.when(i == 0)
        def _():
            l_ref[...] = jnp.zeros_like(l_ref)

        e = y_ref[...] - t_ref[...]
        dy_ref[...] = e * (1.0 / W)
        l_ref[...] += 0.5 * jnp.sum(jnp.mean(e * e, axis=-1, keepdims=True))

    row = pl.BlockSpec((tm, W), lambda i: (i, 0))
    return pl.pallas_call(
        body, name=name, grid=(S // tm,), in_specs=[row, row], out_specs=[row, pl.BlockSpec((8, 128), lambda i: (0, 0))],
        out_shape=[jax.ShapeDtypeStruct((S, W), f32), jax.ShapeDtypeStruct((8, 128), f32)],
        compiler_params=_cp((ARB,)))(y, target)


def _tap_groups(offsets):
    groups = {}
    for k, o in enumerate(offsets):
        groups.setdefault(o % 8, []).append((k, o - o % 8))
    return sorted(groups.items())


def _tap_sum(buf, w, offsets, tm):
    out = None
    for b, taps in _tap_groups(offsets):
        n = tm + 8 if b else tm
        y = None
        for k, base in taps:
            term = w[k:k + 1, :] * buf[pl.ds(base, n), :]
            y = term if y is None else y + term
        part = y[b:b + tm] if b else y
        out = part if out is None else out + part
    return out


def _tap_wgrad(d, buf, dsh, acc_ref, offsets, tm):
    for b, taps in _tap_groups(offsets):
        if b:
            dsh[pl.ds(0, 8), :] = jnp.zeros((8, dsh.shape[1]), f32)
            dsh[pl.ds(tm, 8), :] = jnp.zeros((8, dsh.shape[1]), f32)
            dsh[pl.ds(b, tm), :] = d
            dd, n = dsh[...], tm + 8
        else:
            dd, n = d, tm
        for k, base in taps:
            acc_ref[pl.ds(k, 1), :] += jnp.sum(dd * buf[pl.ds(base, n), :], axis=0, keepdims=True)


_CONV_OFFSETS = [HALO - (CONV_K - 1) + k for k in range(CONV_K)]
_CONV_OFFSETS_T = [CONV_K - 1 - k for k in range(CONV_K)]


def _conv_core(u_ref, uh_ref, w_ref, b_ref, lg_ref, lb_ref, gbuf, tm, first):
    C = u_ref.shape[1] // 2
    u = u_ref[...].astype(f32)
    uh = uh_ref[...].astype(f32)
    gbuf[pl.ds(HALO, tm), :] = u[:, :C] * _sigmoid(u[:, C:])
    halo = uh[:, :C] * _sigmoid(uh[:, C:])
    gbuf[pl.ds(0, HALO), :] = jnp.where(first, 0.0, halo)
    cv = _tap_sum(gbuf, w_ref[...], _CONV_OFFSETS, tm) + b_ref[...]
    mu = jnp.mean(cv, axis=-1, keepdims=True)
    xc = cv - mu
    rstd = lax.rsqrt(jnp.mean(xc * xc, axis=-1, keepdims=True) + EPS)
    z = xc * rstd
    ln = z * lg_ref[...] + lb_ref[...]
    return z, rstd, ln


def _halo_prev(tm, hb, w):
    return pl.BlockSpec((hb, w), lambda i: (jnp.maximum(i * (tm // hb) - 1, 0), 0))


def _conv_fwd(u, w, b, lg, lb, name, comms=()):
    S, C2 = u.shape
    C = C2 // 2
    tm = _tile(S, 256, HALO)

    def body(u_ref, uh_ref, w_ref, b_ref, lg_ref, lb_ref, s_ref, gbuf):
        first = pl.program_id(0) == 0
        _, _, ln = _conv_core(u_ref, uh_ref, w_ref, b_ref, lg_ref, lb_ref, gbuf, tm, first)
        s_ref[...] = (ln * _sigmoid(ln)).astype(s_ref.dtype)

    return _pcall(body, name=name, grid=(S // tm,),
                  in_specs=[pl.BlockSpec((tm, C2), lambda i: (i, 0)), _halo_prev(tm, HALO, C2), _acc_spec(C, 32),
                            _acc_spec(C), _acc_spec(C), _acc_spec(C)],
                  out_specs=pl.BlockSpec((tm, C), lambda i: (i, 0)), out_shape=jax.ShapeDtypeStruct((S, C), bf16),
                  args=[u, u, w, b, lg, lb], scratch_shapes=[pltpu.VMEM((tm + HALO, C), f32)], sem=(PAR,), vmem=VMEM_BIG,
                  comms=comms)


def _conv_bwd1(u, ds, w, b, lg, lb, name, comms=()):
    S, C2 = u.shape
    C = C2 // 2
    tm = _tile(S, 256, HALO)

    def body(u_ref, uh_ref, ds_ref, w_ref, b_ref, lg_ref, lb_ref, dcv_ref, dlg_ref, dlb_ref, ddb_ref, ddw_ref, gbuf, dsh):
        i = pl.program_id(0)

        @pl.when(i == 0)
        def _():
            dlg_ref[...] = jnp.zeros_like(dlg_ref)
            dlb_ref[...] = jnp.zeros_like(dlb_ref)
            ddb_ref[...] = jnp.zeros_like(ddb_ref)
            ddw_ref[...] = jnp.zeros_like(ddw_ref)

        z, rstd, ln = _conv_core(u_ref, uh_ref, w_ref, b_ref, lg_ref, lb_ref, gbuf, tm, i == 0)
        sg = _sigmoid(ln)
        dln = ds_ref[...].astype(f32) * (sg * (1.0 + ln * (1.0 - sg)))
        dlg_ref[...] += jnp.sum(dln * z, axis=0, keepdims=True)
        dlb_ref[...] += jnp.sum(dln, axis=0, keepdims=True)
        dz = dln * lg_ref[...]
        dcv = rstd * (dz - jnp.mean(dz, axis=-1, keepdims=True) - z * jnp.mean(dz * z, axis=-1, keepdims=True))
        dcv_ref[...] = dcv
        ddb_ref[...] += jnp.sum(dcv, axis=0, keepdims=True)
        _tap_wgrad(dcv, gbuf, dsh, ddw_ref, _CONV_OFFSETS, tm)

    vec = jax.ShapeDtypeStruct((1, C), f32)
    return _pcall(
        body, name=name, grid=(S // tm,),
        in_specs=[pl.BlockSpec((tm, C2), lambda i: (i, 0)), _halo_prev(tm, HALO, C2), pl.BlockSpec((tm, C), lambda i: (i, 0)),
                  _acc_spec(C, 32), _acc_spec(C), _acc_spec(C), _acc_spec(C)],
        out_specs=[pl.BlockSpec((tm, C), lambda i: (i, 0)), _acc_spec(C), _acc_spec(C), _acc_spec(C), _acc_spec(C, 32)],
        out_shape=[jax.ShapeDtypeStruct((S, C), f32), vec, vec, vec, jax.ShapeDtypeStruct((32, C), f32)],
        args=[u, u, ds, w, b, lg, lb], scratch_shapes=[pltpu.VMEM((tm + HALO, C), f32), pltpu.VMEM((tm + 8, C), f32)],
        sem=(ARB,), vmem=VMEM_BIG, comms=comms)


def _conv_bwd2(dcv, u, w, name, comms=()):
    S, C2 = u.shape
    C = C2 // 2
    tm = _tile(S, 256, HALO)
    nt = S // tm
    nhb = S // HALO

    def body(d_ref, dn_ref, u_ref, w_ref, du_ref, db_ref, dbuf):
        i = pl.program_id(0)

        @pl.when(i == 0)
        def _():
            db_ref[...] = jnp.zeros_like(db_ref)

        dbuf[pl.ds(0, tm), :] = d_ref[...]
        dbuf[pl.ds(tm, HALO), :] = jnp.where(i == nt - 1, 0.0, dn_ref[...])
        dglu = _tap_sum(dbuf, w_ref[...], _CONV_OFFSETS_T, tm)
        u = u_ref[...].astype(f32)
        a, gt = u[:, :C], u[:, C:]
        sg = _sigmoid(gt)
        da = dglu * sg
        dgt = dglu * a * sg * (1.0 - sg)
        du_ref[:, :C] = da.astype(du_ref.dtype)
        du_ref[:, C:] = dgt.astype(du_ref.dtype)
        db_ref[:, :C] += jnp.sum(da, axis=0, keepdims=True)
        db_ref[:, C:] += jnp.sum(dgt, axis=0, keepdims=True)

    return _pcall(
        body, name=name, grid=(nt,),
        in_specs=[pl.BlockSpec((tm, C), lambda i: (i, 0)),
                  pl.BlockSpec((HALO, C), lambda i: (jnp.minimum((i + 1) * (tm // HALO), nhb - 1), 0)),
                  pl.BlockSpec((tm, C2), lambda i: (i, 0)), _acc_spec(C, 32)],
        out_specs=[pl.BlockSpec((tm, C2), lambda i: (i, 0)), _acc_spec(C2)],
        out_shape=[jax.ShapeDtypeStruct((S, C2), bf16), jax.ShapeDtypeStruct((1, C2), f32)],
        args=[dcv, dcv, u, w], scratch_shapes=[pltpu.VMEM((tm + HALO, C), f32)], sem=(ARB,), vmem=VMEM_BIG, comms=comms)


def _ffn_gate_fwd(u2, w, b, name, comms=()):
    S, F2 = u2.shape
    F = F2 // 2
    cw = _tile(F, 1408)
    ncw = F // cw
    tm = _tile(S, 256, FHALO)

    def body(g_ref, gh_ref, v_ref, w_ref, b_ref, a_ref, gbuf):
        first = pl.program_id(0) == 0
        gbuf[pl.ds(FHALO, tm), :] = g_ref[...].astype(f32)
        gbuf[pl.ds(0, FHALO), :] = jnp.where(first, 0.0, gh_ref[...].astype(f32))
        w = w_ref[...]
        gc = jnp.zeros((tm, cw), f32) + b_ref[...]
        for k in range(FFN_K):
            gc = gc + w[k:k + 1, :] * gbuf[pl.ds(FHALO - (FFN_K - 1) + k, tm), :]
        a_ref[...] = (gc * _sigmoid(gc) * v_ref[...].astype(f32)).astype(a_ref.dtype)

    return _pcall(
        body, name=name, grid=(S // tm, ncw),
        in_specs=[pl.BlockSpec((tm, cw), lambda i, j: (i, j)),
                  pl.BlockSpec((FHALO, cw), lambda i, j: (jnp.maximum(i * (tm // FHALO) - 1, 0), j)),
                  pl.BlockSpec((tm, cw), lambda i, j: (i, ncw + j)),
                  pl.BlockSpec((8, cw), lambda i, j: (0, j)), pl.BlockSpec((1, cw), lambda i, j: (0, j))],
        out_specs=pl.BlockSpec((tm, cw), lambda i, j: (i, j)), out_shape=jax.ShapeDtypeStruct((S, F), bf16),
        args=[u2, u2, u2, w, b], scratch_shapes=[pltpu.VMEM((tm + FHALO, cw), f32)], sem=(PAR, PAR), comms=comms)


def _ffn_gate_bwd(u2, dact, w, b, name, comms=()):
    S, F2 = u2.shape
    F = F2 // 2
    cw = _tile(F, 1408)
    ncw = F // cw
    tm = _tile(S, 256, FHALO)
    nt = S // tm
    nhb = S // FHALO
    R = tm + 2 * FHALO

    def body(g_ref, gp_ref, gn_ref, v_ref, vn_ref, d_ref, dn_ref, w_ref, b_ref, du_ref, dw_ref, db_ref, gbuf, dbuf):
        i = pl.program_id(1)
        first, last = i == 0, i == nt - 1

        @pl.when(i == 0)
        def _():
            dw_ref[...] = jnp.zeros_like(dw_ref)
            db_ref[...] = jnp.zeros_like(db_ref)

        gbuf[pl.ds(0, FHALO), :] = jnp.where(first, 0.0, gp_ref[...].astype(f32))
        gbuf[pl.ds(FHALO, tm), :] = g_ref[...].astype(f32)
        gbuf[pl.ds(FHALO + tm, FHALO), :] = gn_ref[...].astype(f32)
        w = w_ref[...]
        n_ext = tm + FHALO
        gc = jnp.zeros((n_ext, cw), f32) + b_ref[...]
        for k in range(FFN_K):
            gc = gc + w[k:k + 1, :] * gbuf[pl.ds(FHALO - (FFN_K - 1) + k, n_ext), :]
        sg = _sigmoid(gc)
        val = jnp.concatenate([v_ref[...].astype(f32), vn_ref[...].astype(f32)], axis=0)
        dact_ext = jnp.concatenate([d_ref[...].astype(f32), jnp.where(last, 0.0, dn_ref[...].astype(f32))], axis=0)
        dgc = dact_ext * val * (sg * (1.0 + gc * (1.0 - sg)))
        dbuf[...] = dgc
        dval = dact_ext[:tm] * (gc[:tm] * sg[:tm])
        dgt = jnp.zeros((tm, cw), f32)
        for k in range(FFN_K):
            dgt = dgt + w[k:k + 1, :] * dbuf[pl.ds(FFN_K - 1 - k, tm), :]
        du_ref[0] = dgt.astype(du_ref.dtype)
        du_ref[1] = dval.astype(du_ref.dtype)
        dgc_t = dgc[:tm]
        db_ref[...] += jnp.sum(dgc_t, axis=0, keepdims=True)
        for k in range(FFN_K):
            dw_ref[pl.ds(k, 1), :] += jnp.sum(dgc_t * gbuf[pl.ds(FHALO - (FFN_K - 1) + k, tm), :], axis=0, keepdims=True)

    hb = tm // FHALO
    prev = lambda j, i: (jnp.maximum(i * hb - 1, 0), j)
    nxt = lambda j, i: (jnp.minimum((i + 1) * hb, nhb - 1), j)
    nxt_v = lambda j, i: (jnp.minimum((i + 1) * hb, nhb - 1), ncw + j)
    return _pcall(
        body, name=name, grid=(ncw, nt), comms=comms, sem=(PAR, ARB), vmem=VMEM_BIG,
        args=[u2, u2, u2, u2, u2, dact, dact, w, b],
        in_specs=[pl.BlockSpec((tm, cw), lambda j, i: (i, j)), pl.BlockSpec((FHALO, cw), prev), pl.BlockSpec((FHALO, cw), nxt),
                  pl.BlockSpec((tm, cw), lambda j, i: (i, ncw + j)), pl.BlockSpec((FHALO, cw), nxt_v),
                  pl.BlockSpec((tm, cw), lambda j, i: (i, j)), pl.BlockSpec((FHALO, cw), nxt),
                  pl.BlockSpec((8, cw), lambda j, i: (0, j)), pl.BlockSpec((1, cw), lambda j, i: (0, j))],
        out_specs=[pl.BlockSpec((2, tm, cw), lambda j, i: (0, i, j)), pl.BlockSpec((8, cw), lambda j, i: (0, j)),
                   pl.BlockSpec((1, cw), lambda j, i: (0, j))],
        out_shape=[jax.ShapeDtypeStruct((2, S, F), bf16), jax.ShapeDtypeStruct((8, F), f32), jax.ShapeDtypeStruct((1, F), f32)],
        scratch_shapes=[pltpu.VMEM((R, cw), f32), pltpu.VMEM((tm + FHALO, cw), f32)])


def _rope_tables(pos_col, name):
    S = pos_col.shape[0]
    tm = _tile(S, 512, 8)
    half = ROT // 2
    inv = THETA ** (-np.arange(0, ROT, 2, dtype=np.float32) / ROT)
    lane_freq = np.zeros((1, DH), np.float32)
    lane_freq[0, :half] = inv
    lane_freq[0, half:ROT] = inv
    lane_freq = jnp.asarray(lane_freq)

    def body(p_ref, fr_ref, c_ref, sa_ref, sb_ref):
        ang = p_ref[...].astype(f32) * fr_ref[...]
        lane = lax.broadcasted_iota(jnp.int32, (tm, DH), 1)
        cs, sn = jnp.cos(ang), jnp.sin(ang)
        c_ref[...] = jnp.where(lane < ROT, cs, 1.0)
        sa_ref[...] = jnp.where(lane < half, -sn, 0.0)
        sb_ref[...] = jnp.where((lane >= half) & (lane < ROT), sn, 0.0)

    row = pl.BlockSpec((tm, DH), lambda i: (i, 0))
    shp = jax.ShapeDtypeStruct((S, DH), f32)
    return pl.pallas_call(body, name=name, grid=(S // tm,),
                          in_specs=[pl.BlockSpec((tm, 1), lambda i: (i, 0)), pl.BlockSpec((1, DH), lambda i: (0, 0))],
                          out_specs=[row, row, row], out_shape=[shp, shp, shp], compiler_params=_cp((PAR,)))(pos_col, lane_freq)


def _rope(n, c, sa, sb):
    return n * c + pltpu.roll(n, DH - ROT // 2, 1) * sa + pltpu.roll(n, ROT // 2, 1) * sb


def _rope_t(d, c, sa, sb):
    return d * c + pltpu.roll(d * sa, ROT // 2, 1) + pltpu.roll(d * sb, DH - ROT // 2, 1)


def _qk_fwd(raw, g, tabs, width, with_values, name):
    S = raw.shape[0]
    nh = width // DH
    ow = width // NG
    hpg = ow // DH
    tm = _tile(S, 256, 16 * max(DILS))
    vgroups = [gi for gi, r in enumerate(DILS) if r > 1] if with_values else []

    def body(x_ref, g_ref, c_ref, sa_ref, sb_ref, *rest):
        o_refs = rest[:NG]
        v_refs = rest[NG:NG + len(vgroups)]
        scr, vscr = rest[NG + len(vgroups):]
        c, sa, sb = c_ref[...], sa_ref[...], sb_ref[...]
        for h in range(nh):
            gi, hh = h // hpg, h % hpg
            r = DILS[gi]
            xv = x_ref[:, h * DH:(h + 1) * DH].astype(f32)
            rs = lax.rsqrt(jnp.mean(xv * xv, axis=-1, keepdims=True) + EPS)
            y = _rope(xv * rs * g_ref[...], c, sa, sb)
            if r == 1:
                o_refs[gi][:, hh * DH:(hh + 1) * DH] = y.astype(bf16)
            else:
                scr[...] = y
                for j in range(r):
                    o_refs[gi][:, j * ow + hh * DH:j * ow + (hh + 1) * DH] = scr[pl.ds(j, tm // r, stride=r), :].astype(bf16)
        for vi, gi in enumerate(vgroups):
            _to_view(x_ref[:, width + gi * ow:width + (gi + 1) * ow].astype(f32), v_refs[vi], vscr, DILS[gi], ow, tm)

    win = raw.shape[1] if with_values else width
    row = pl.BlockSpec((tm, win), lambda i: (i, 0))
    tab = pl.BlockSpec((tm, DH), lambda i: (i, 0))
    view = lambda r: pl.BlockSpec((tm // r, r * ow), lambda i: (i, 0))
    vshape = lambda r: jax.ShapeDtypeStruct((S // r, r * ow), bf16)
    outs = pl.pallas_call(
        body, name=name, grid=(S // tm,), in_specs=[row, _acc_spec(DH), tab, tab, tab],
        out_specs=[view(r) for r in DILS] + [view(DILS[gi]) for gi in vgroups],
        out_shape=[vshape(r) for r in DILS] + [vshape(DILS[gi]) for gi in vgroups],
        scratch_shapes=[pltpu.VMEM((tm, DH), f32), pltpu.VMEM((ow // DH, tm, DH), f32)],
        compiler_params=_cp((PAR,)))(raw, g, *tabs)
    return outs[:NG], outs[NG:]


def _qk_bwd(dparts, raw, g, tabs, width, extra, name):
    S = raw.shape[0]
    nh = width // DH
    ow = width // NG
    hpg = ow // DH
    tm = _tile(S, 256, 16 * max(DILS))
    wout = width + len(extra) * ow

    def body(*refs):
        d_refs = refs[:NG]
        x_ref, g_ref, c_ref, sa_ref, sb_ref = refs[NG:NG + 5]
        e_refs = refs[NG + 5:NG + 5 + len(extra)]
        o_ref, dg_ref, scr, vscr = refs[NG + 5 + len(extra):]
        i = pl.program_id(0)

        @pl.when(i == 0)
        def _():
            dg_ref[...] = jnp.zeros_like(dg_ref)

        c, sa, sb = c_ref[...], sa_ref[...], sb_ref[...]
        gv = g_ref[...]
        dg = jnp.zeros((1, DH), f32)
        for h in range(nh):
            gi, hh = h // hpg, h % hpg
            r = DILS[gi]
            if r == 1:
                dout = d_refs[gi][:, hh * DH:(hh + 1) * DH].astype(f32)
            else:
                for j in range(r):
                    scr[pl.ds(j, tm // r, stride=r), :] = d_refs[gi][:, j * ow + hh * DH:j * ow + (hh + 1) * DH].astype(f32)
                dout = scr[...]
            xv = x_ref[:, h * DH:(h + 1) * DH].astype(f32)
            rs = lax.rsqrt(jnp.mean(xv * xv, axis=-1, keepdims=True) + EPS)
            xh = xv * rs
            dn = _rope_t(dout, c, sa, sb)
            dg = dg + jnp.sum(dn * xh, axis=0, keepdims=True)
            dxn = dn * gv
            dx = rs * (dxn - xh * jnp.mean(dxn * xh, axis=-1, keepdims=True))
            o_ref[:, h * DH:(h + 1) * DH] = dx.astype(o_ref.dtype)
        for gi, e_ref in enumerate(e_refs):
            o_ref[:, width + gi * ow:width + (gi + 1) * ow] = _from_view(e_ref, vscr, DILS[gi], ow, tm).astype(o_ref.dtype)
        dg_ref[...] += dg

    views = [pl.BlockSpec((tm // r, r * ow), lambda i: (i, 0)) for r in DILS]
    tab = pl.BlockSpec((tm, DH), lambda i: (i, 0))
    return pl.pallas_call(
        body, name=name, grid=(S // tm,),
        in_specs=views + [pl.BlockSpec((tm, width), lambda i: (i, 0)), _acc_spec(DH), tab, tab, tab] + (views if extra else []),
        out_specs=[pl.BlockSpec((tm, wout), lambda i: (i, 0)), _acc_spec(DH)],
        out_shape=[jax.ShapeDtypeStruct((S, wout), bf16), jax.ShapeDtypeStruct((1, DH), f32)],
        scratch_shapes=[pltpu.VMEM((tm, DH), f32), pltpu.VMEM((ow // DH, tm, DH), f32)],
        compiler_params=_cp((ARB,)))(*dparts, raw, g, *tabs, *extra)


def _dot_nt(a, b):
    return lax.dot_general(a, b, (((1,), (1,)), ((), ())), preferred_element_type=f32)


def _dot_tn(a, b):
    return lax.dot_general(a, b, (((0,), (0,)), ((), ())), preferred_element_type=f32)


def _band_masks():
    qi = lax.broadcasted_iota(jnp.int32, (BLK, BLK), 0)
    ki = lax.broadcasted_iota(jnp.int32, (BLK, BLK), 1)
    return ki <= qi, ki >= qi


def _attn_fwd(qv, kview, vview, vbase, r, name):
    sr = qv.shape[0]
    ow = qv.shape[1] // r
    hpg = ow // DH
    nb = sr // BLK
    scale = 1.0 / math.sqrt(DH)

    def body(q_ref, kc_ref, kp_ref, vc_ref, vp_ref, o_ref, l_ref):
        n = pl.program_id(1)
        m_cur, m_prev = _band_masks()
        m_prev = m_prev & (n > 0)
        for h in range(hpg):
            hs = slice(h * DH, (h + 1) * DH)
            qh = q_ref[:, hs]
            s_c = jnp.where(m_cur, _dot_nt(qh, kc_ref[:, hs]) * scale, NEG)
            s_p = jnp.where(m_prev, _dot_nt(qh, kp_ref[:, hs]) * scale, NEG)
            mx = jnp.maximum(jnp.max(s_c, axis=-1, keepdims=True), jnp.max(s_p, axis=-1, keepdims=True))
            p_c = jnp.exp(s_c - mx)
            p_p = jnp.exp(s_p - mx)
            den = jnp.sum(p_c, axis=-1, keepdims=True) + jnp.sum(p_p, axis=-1, keepdims=True)
            o = jnp.dot(p_c.astype(bf16), vc_ref[:, hs], preferred_element_type=f32)
            o = o + jnp.dot(p_p.astype(bf16), vp_ref[:, hs], preferred_element_type=f32)
            o_ref[:, hs] = (o / den).astype(o_ref.dtype)
            l_ref[:, hs] = jnp.broadcast_to(mx + jnp.log(den), (BLK, DH))

    cur = lambda j, n: (n, j)
    prev = lambda j, n: (jnp.maximum(n - 1, 0), j)
    vcur = lambda j, n: (n, vbase + j)
    vprev = lambda j, n: (jnp.maximum(n - 1, 0), vbase + j)
    blk = lambda f: pl.BlockSpec((BLK, ow), f)
    return pl.pallas_call(
        body, name=name, grid=(r, nb), in_specs=[blk(cur), blk(cur), blk(prev), blk(vcur), blk(vprev)],
        out_specs=[blk(cur), blk(cur)],
        out_shape=[jax.ShapeDtypeStruct((sr, r * ow), bf16), jax.ShapeDtypeStruct((sr, r * ow), f32)],
        compiler_params=_cp((PAR, PAR)))(qv, kview, kview, vview, vview)


def _attn_bwd_q(qv, kview, vview, vbase, do_g, lse, corr, r, name, comms=()):
    sr = qv.shape[0]
    ow = qv.shape[1] // r
    hpg = ow // DH
    nb = sr // BLK
    scale = 1.0 / math.sqrt(DH)

    def body(q_ref, kc_ref, kp_ref, vc_ref, vp_ref, do_ref, l_ref, c_ref, dq_ref):
        n = pl.program_id(1)
        m_cur, m_prev = _band_masks()
        m_prev = m_prev & (n > 0)
        for h in range(hpg):
            hs = slice(h * DH, (h + 1) * DH)
            qh, doh = q_ref[:, hs], do_ref[:, hs]
            ls = slice(h * DH, h * DH + BLK)
            lh, ch = l_ref[:, ls], c_ref[:, ls]
            dq = jnp.zeros((BLK, DH), f32)
            for k_ref, v_ref, msk in ((kc_ref, vc_ref, m_cur), (kp_ref, vp_ref, m_prev)):
                kh = k_ref[:, hs]
                s = jnp.where(msk, _dot_nt(qh, kh) * scale, NEG)
                p = jnp.exp(s - lh)
                dsc = p * (_dot_nt(doh, v_ref[:, hs]) + ch)
                dq = dq + jnp.dot(dsc.astype(bf16), kh, preferred_element_type=f32)
            dq_ref[:, hs] = (dq * scale).astype(dq_ref.dtype)

    cur = lambda j, n: (n, j)
    prev = lambda j, n: (jnp.maximum(n - 1, 0), j)
    vcur = lambda j, n: (n, vbase + j)
    vprev = lambda j, n: (jnp.maximum(n - 1, 0), vbase + j)
    blk = lambda f: pl.BlockSpec((BLK, ow), f)
    return _pcall(
        body, name=name, grid=(r, nb),
        in_specs=[blk(cur), blk(cur), blk(prev), blk(vcur), blk(vprev), blk(cur), blk(cur), blk(cur)],
        out_specs=blk(cur), out_shape=jax.ShapeDtypeStruct((sr, r * ow), bf16), sem=(PAR, PAR), comms=comms,
        args=[qv, kview, kview, vview, vview, do_g, lse, corr])


def _attn_bwd_kv(qv, kview, vview, vbase, do_g, lse, corr, r, name):
    sr = qv.shape[0]
    ow = qv.shape[1] // r
    hpg = ow // DH
    nb = sr // BLK
    scale = 1.0 / math.sqrt(DH)

    def body(k_ref, v_ref, qc_ref, qn_ref, doc_ref, don_ref, lc_ref, ln_ref, cc_ref, cn_ref, dk_ref, dv_ref):
        n = pl.program_id(1)
        m_cur, m_prev = _band_masks()
        m_next = m_prev & (n < nb - 1)
        for h in range(hpg):
            hs = slice(h * DH, (h + 1) * DH)
            ls = slice(h * DH, h * DH + BLK)
            kh, vh = k_ref[:, hs], v_ref[:, hs]
            dk = jnp.zeros((BLK, DH), f32)
            dv = jnp.zeros((BLK, DH), f32)
            for q_ref, do_ref, l_ref, c_ref, msk in ((qc_ref, doc_ref, lc_ref, cc_ref, m_cur),
                                                     (qn_ref, don_ref, ln_ref, cn_ref, m_next)):
                qh, doh = q_ref[:, hs], do_ref[:, hs]
                s = jnp.where(msk, _dot_nt(qh, kh) * scale, NEG)
                p = jnp.exp(s - l_ref[:, ls])
                dv = dv + _dot_tn(p.astype(bf16), doh)
                dsc = p * (_dot_nt(doh, vh) + c_ref[:, ls])
                dk = dk + _dot_tn(dsc.astype(bf16), qh)
            dk_ref[:, hs] = (dk * scale).astype(dk_ref.dtype)
            dv_ref[:, hs] = dv.astype(dv_ref.dtype)

    cur = lambda j, n: (n, j)
    nxt = lambda j, n: (jnp.minimum(n + 1, nb - 1), j)
    vcur = lambda j, n: (n, vbase + j)
    blk = lambda f: pl.BlockSpec((BLK, ow), f)
    shp = jax.ShapeDtypeStruct((sr, r * ow), bf16)
    return pl.pallas_call(
        body, name=name, grid=(r, nb),
        in_specs=[blk(cur), blk(vcur), blk(cur), blk(nxt), blk(cur), blk(nxt), blk(cur), blk(nxt), blk(cur), blk(nxt)],
        out_specs=[blk(cur), blk(cur)], out_shape=[shp, shp],
        compiler_params=_cp((PAR, PAR)))(kview, vview, qv, qv, do_g, do_g, lse, lse, corr, corr)


def _mix_weights(l_refs):
    ls = [l[...] for l in l_refs]
    mx = functools.reduce(jnp.maximum, ls)
    es = [jnp.exp(l - mx) for l in ls]
    den = functools.reduce(lambda a, b: a + b, es)
    return [e / den for e in es]


def _from_view(ref, scr, r, ow, tm):
    if r == 1:
        return ref[...].astype(f32)
    for c in range(ow // DH):
        for j in range(r):
            scr[c, pl.ds(j, tm // r, stride=r), :] = ref[:, j * ow + c * DH:j * ow + (c + 1) * DH].astype(f32)
    return jnp.concatenate([scr[c] for c in range(ow // DH)], axis=1)


def _to_view(val, ref, scr, r, ow, tm):
    if r == 1:
        ref[...] = val.astype(ref.dtype)
        return
    for c in range(ow // DH):
        scr[c] = val[:, c * DH:(c + 1) * DH]
        for j in range(r):
            ref[:, j * ow + c * DH:j * ow + (c + 1) * DH] = scr[c, pl.ds(j, tm // r, stride=r), :].astype(ref.dtype)


def _view_specs(tm, ow):
    return [pl.BlockSpec((tm // r, r * ow), lambda i: (i, 0)) for r in DILS]


def _combine_fwd(os_, lses, name):
    ow = os_[0].shape[1] // DILS[0]
    S = os_[0].shape[0] * DILS[0]
    tm = _tile(S, 256, 16 * max(DILS))

    def body(*refs):
        o_refs, l_refs, out_ref = refs[:NG], refs[NG:2 * NG], refs[2 * NG]
        scr = refs[2 * NG + 1:]
        ov = [_from_view(o_refs[gi], scr[2 * gi], DILS[gi], ow, tm) for gi in range(NG)]
        lv = [_from_view(l_refs[gi], scr[2 * gi + 1], DILS[gi], ow, tm) for gi in range(NG)]
        al = _mix_weights(lv)
        acc = al[0] * ov[0]
        for gi in range(1, NG):
            acc = acc + al[gi] * ov[gi]
        out_ref[...] = acc.astype(out_ref.dtype)

    views = _view_specs(tm, ow)
    return pl.pallas_call(body, name=name, grid=(S // tm,), in_specs=views + views,
                          out_specs=pl.BlockSpec((tm, ow), lambda i: (i, 0)), out_shape=jax.ShapeDtypeStruct((S, ow), bf16),
                          scratch_shapes=[pltpu.VMEM((ow // DH, tm, DH), f32)] * (2 * NG),
                          compiler_params=_cp((PAR,), VMEM_BIG))(*os_, *lses)


def _combine_bwd(do, os_, lses, name, comms=()):
    S, ow = do.shape
    hpg = ow // DH
    tm = _tile(S, 256, 16 * max(DILS))

    def body(*refs):
        do_ref = refs[0]
        o_refs, l_refs = refs[1:1 + NG], refs[1 + NG:1 + 2 * NG]
        dog_refs, c_refs = refs[1 + 2 * NG:1 + 3 * NG], refs[1 + 3 * NG:1 + 4 * NG]
        scr = refs[1 + 4 * NG:]
        ov = [_from_view(o_refs[gi], scr[2 * gi], DILS[gi], ow, tm) for gi in range(NG)]
        lv = [_from_view(l_refs[gi], scr[2 * gi + 1], DILS[gi], ow, tm) for gi in range(NG)]
        al = _mix_weights(lv)
        dov = do_ref[...]
        o = al[0] * ov[0]
        for gi in range(1, NG):
            o = o + al[gi] * ov[gi]
        prod = dov * o
        t = jnp.concatenate(
            [jnp.broadcast_to(jnp.sum(prod[:, h * DH:(h + 1) * DH], axis=-1, keepdims=True), (tm, DH)) for h in range(hpg)],
            axis=1)
        for gi in range(NG):
            _to_view(al[gi] * dov, dog_refs[gi], scr[2 * NG], DILS[gi], ow, tm)
            _to_view(-(al[gi] * t), c_refs[gi], scr[2 * NG], DILS[gi], ow, tm)

    views = _view_specs(tm, ow)
    vshape = lambda dt: [jax.ShapeDtypeStruct((S // r, r * ow), dt) for r in DILS]
    return _pcall(
        body, name=name, grid=(S // tm,), in_specs=[pl.BlockSpec((tm, ow), lambda i: (i, 0))] + views + views,
        out_specs=views + views, out_shape=vshape(bf16) + vshape(f32),
        args=[do, *os_, *lses], scratch_shapes=[pltpu.VMEM((ow // DH, tm, DH), f32)] * (2 * NG + 1), sem=(PAR,), vmem=VMEM_BIG,
        comms=comms)


def _pad_rows(w, rows):
    return jnp.concatenate([w, jnp.zeros((rows - w.shape[0], w.shape[1]), w.dtype)], axis=0)


def kernel(x, c, positions, mod_w, mod_b, norm_mix_g, norm_ffn_g, conv_pw1_w, conv_pw1_b, conv_dw_w, conv_dw_b, conv_ln_g, conv_ln_b, conv_pw2_w, conv_pw2_b, kv_mod_w, kv_mod_b, kv_norm_g, w_kv, k_norm_g, w_q, q_norm_g, w_o, ffn_up_w, ffn_dw_w, ffn_dw_b, ffn_down_w, loss_target, m_mod_w, m_mod_b, m_norm_mix_g, m_norm_ffn_g, m_conv_pw1_w, m_conv_pw1_b, m_conv_dw_w, m_conv_dw_b, m_conv_ln_g, m_conv_ln_b, m_conv_pw2_w, m_conv_pw2_b, m_kv_mod_w, m_kv_mod_b, m_kv_norm_g, m_w_kv, m_k_norm_g, m_w_q, m_q_norm_g, m_w_o, m_ffn_up_w, m_ffn_dw_w, m_ffn_dw_b, m_ffn_down_w, v_mod_w, v_mod_b, v_norm_mix_g, v_norm_ffn_g, v_conv_pw1_w, v_conv_pw1_b, v_conv_dw_w, v_conv_dw_b, v_conv_ln_g, v_conv_ln_b, v_conv_pw2_w, v_conv_pw2_b, v_kv_mod_w, v_kv_mod_b, v_kv_norm_g, v_w_kv, v_k_norm_g, v_w_q, v_q_norm_g, v_w_o, v_ffn_up_w, v_ffn_dw_w, v_ffn_dw_b, v_ffn_down_w):
    S, Dm = x.shape[1], x.shape[2]
    F = ffn_dw_b.shape[1]
    QW = NG * HPG * DH
    OW = HPG * DH
    mx, my, mc = _me()
    me = 4 * mx + 2 * my + mc
    core = jnp.reshape(mc, (1,)).astype(jnp.int32)
    chip = jnp.reshape(2 * mx + my, (1,)).astype(jnp.int32)
    x0 = x.reshape(S, Dm)
    target = loss_target.reshape(S, Dm)

    c_all = _ag_small(c, "ag_c").reshape(NDEV, Dm)
    n_mod = mod_w.shape[2]
    n_kvm = kv_mod_w.shape[1]
    b0 = lax.dynamic_slice(mod_b, (0, me * n_mod), (1, n_mod))
    b1 = lax.dynamic_slice(mod_b, (1, me * n_mod), (1, n_mod))
    bk = lax.dynamic_slice(kv_mod_b.reshape(1, -1), (0, me * n_kvm), (1, n_kvm))
    m_part = jnp.concatenate([_modproj(c_all, mod_w[0], b0, "modproj0"), _modproj(c_all, mod_w[1], b1, "modproj1"),
                              _modproj(c_all, kv_mod_w, bk, "modproj_kv")], axis=1)
    m_all = _ag_small(m_part, "ag_mod")
    m_mine = lax.dynamic_index_in_dim(m_all, me, axis=1, keepdims=False)
    mod0 = m_mine[:, :n_mod].reshape(6, Dm)
    mod1 = m_mine[:, n_mod:2 * n_mod].reshape(6, Dm)
    modkv = m_mine[:, 2 * n_mod:].reshape(2, Dm)
    row = lambda a, i: a[i:i + 1]

    as3 = lambda w: w if w.ndim == 3 else w[None]
    sh16 = lambda w: as3(w).astype(bf16)
    W_pw1 = _ag_big(sh16(conv_pw1_w), 2, "ag_pw1")
    ag_pw2 = _comm_allgather(sh16(conv_pw2_w), 1)
    ag_up = [_comm_allgather(sh16(ffn_up_w[l]), 2) for l in range(2)]
    ag_down = [_comm_allgather(sh16(ffn_down_w[l]), 1) for l in range(2)]
    ag_kv = _comm_allgather(sh16(w_kv), 2)
    ag_q = _comm_allgather(sh16(w_q), 2)
    ag_o = _comm_allgather(sh16(w_o), 2)

    sp_flat = jnp.concatenate([conv_pw1_b.reshape(-1), conv_dw_b.reshape(-1), conv_ln_g.reshape(-1), conv_ln_b.reshape(-1),
                               conv_pw2_b.reshape(-1), conv_dw_w.reshape(-1), ffn_dw_w.reshape(-1)])
    sp_rows = -(-sp_flat.shape[0] // 1024) * 8
    sp_flat = jnp.concatenate([sp_flat, jnp.zeros((sp_rows * 128 - sp_flat.shape[0],), f32)]).reshape(sp_rows, 128)
    n1, nd = conv_pw1_b.shape[1], conv_dw_b.shape[1]
    nfw = ffn_dw_w.shape[2]
    sp = _ag_small(sp_flat, "ag_small_params").reshape(NDEV, -1)
    off = 0
    pw1_b = sp[:, off:off + n1].reshape(1, -1); off += n1
    dw_b = sp[:, off:off + nd].reshape(1, -1); off += nd
    ln_g = sp[:, off:off + nd].reshape(1, -1); off += nd
    ln_b = sp[:, off:off + nd].reshape(1, -1); off += nd
    pw2_b = sp[:, off:off + nd].reshape(1, -1); off += nd
    dw_w = jnp.transpose(sp[:, off:off + CONV_K * nd].reshape(NDEV, CONV_K, nd), (1, 0, 2)).reshape(CONV_K, -1); off += CONV_K * nd
    fdw_w = jnp.transpose(sp[:, off:off + 2 * FFN_K * nfw].reshape(NDEV, 2, FFN_K, nfw), (1, 2, 0, 3)).reshape(2, FFN_K, -1)
    dw_w32 = _pad_rows(dw_w, 32)

    tabs = _rope_tables(positions.reshape(S, 1), "rope_tables")

    def with_comms(res, comms):
        return res if comms else (res, [])

    def rs_d2d(dwb):
        return [_comm_rs_sibling(dwb)]

    def rs_add(dwb, couts, tag):
        return _chip_partial(dwb, couts[0][0], core, f"rs_add_{tag}")

    def rs_ici(part):
        return [_comm_rs_chips(part)]

    def ffn_forward(xin, l, modv, w_up, w_down, up_comms, gate_comms, down_comms):
        h2 = _mod_fwd(xin, row(norm_ffn_g, l), row(modv, 3), row(modv, 4), f"ffn{l}_mod")
        u2, c_up = with_comms(_mm_nn(h2, w_up, 0, name=f"ffn{l}_up", comms=up_comms), up_comms)
        if w_down is None:
            w_down, c_up = c_up[0][0], c_up[1:]
        fw8 = _pad_rows(fdw_w[l], 8)
        act, c_gate = with_comms(_ffn_gate_fwd(u2, fw8, row(ffn_dw_b, l), f"ffn{l}_gate", comms=gate_comms), gate_comms)
        (xout, f), c_down = with_comms(
            _mm_nn(act, w_down, 0, name=f"ffn{l}_down", res=xin, gate=row(modv, 5), tk=1408, comms=down_comms), down_comms)
        return xout, (h2, u2, act, f, fw8, w_up, w_down), c_up, c_gate, c_down

    def ffn_backward(dx, xin, l, modv, saved, dact_comms):
        h2, u2, act, f, fw8, w_up, w_down = saved
        df, dgate, _ = _gate_bwd(dx, f, row(modv, 5), f"ffn{l}_gate_bwd")
        dact, c_dact = with_comms(_mm_nt(df, w_down, 0, name=f"ffn{l}_dact", out_dtype=bf16, tko=512, comms=dact_comms), dact_comms)
        d_down = _mm_tn(act, df, name=f"ffn{l}_ddown", col_sharded=False)
        (du2, d_fw, d_fb), c1 = _ffn_gate_bwd(u2, dact, fw8, row(ffn_dw_b, l), f"ffn{l}_gatebwd", comms=rs_d2d(d_down))
        part_down = rs_add(d_down, c1, f"down{l}")
        dh2, c2 = _mm_nt(du2, w_up, 0, name=f"ffn{l}_dh", out_dtype=f32, comms=rs_ici(part_down))
        d_up = _mm_tn(h2, du2, name=f"ffn{l}_dup", col_sharded=True)
        (dxin, dsh, dsc, dg), c3 = _mod_bwd(dh2, xin, dx, row(norm_ffn_g, l), row(modv, 4), f"ffn{l}_mod_bwd", comms=rs_d2d(d_up))
        part_up = rs_add(d_up, c3, f"up{l}")
        grads = dict(d_fw=d_fw[:FFN_K], d_fb=d_fb, dsh=dsh, dsc=dsc, dgate=dgate, dg=dg,
                     down=(part_down, c2[0][0]), part_up=part_up)
        return dxin, grads, c_dact

    h0 = _mod_fwd(x0, row(norm_mix_g, 0), row(mod0, 0), row(mod0, 1), "l0_mod")
    u0, c = _mm_nn(h0, W_pw1, 0, name="l0_pw1", bias=pw1_b, comms=[ag_pw2])
    W_pw2 = c[0][0]
    s0, c = _conv_fwd(u0, dw_w32, dw_b, ln_g, ln_b, "l0_conv", comms=[ag_up[0]])
    W_up0 = c[0][0]
    x1, f0 = _mm_nn(s0, W_pw2, 0, name="l0_pw2", bias=pw2_b, res=x0, gate=row(mod0, 2))
    x2, ffn0_saved, _, c_gate, c_down = ffn_forward(x1, 0, mod0, W_up0, None, [ag_down[0]], [ag_kv], [ag_q, ag_o])
    W_kv, W_q, W_o = c_gate[0][0], c_down[0][0], c_down[1][0]

    hkv = _mod_fwd(x2, kv_norm_g.reshape(1, -1), row(modkv, 0), row(modkv, 1), "kv_mod")
    kvraw, c = _mm_nn(hkv, W_kv, 0, name="kv_proj", comms=[ag_up[1]])
    W_up1 = c[0][0]
    kg = k_norm_g.reshape(1, -1)
    k_gv, v_dil = _qk_fwd(kvraw, kg, tabs, QW, True, "k_norm_rope")
    dilated = [gi for gi, r in enumerate(DILS) if r > 1]
    v_of = {gi: (kvraw, NG + gi) for gi, r in enumerate(DILS) if r == 1}
    v_of.update({gi: (v_dil[i], 0) for i, gi in enumerate(dilated)})
    h1 = _mod_fwd(x2, row(norm_mix_g, 1), row(mod1, 0), row(mod1, 1), "l1_mod")
    qraw, c_q = _mm_nn(h1, W_q, 0, name="q_proj", comms=[ag_down[1]])
    qg = q_norm_g.reshape(1, -1)
    q_gv, _ = _qk_fwd(qraw, qg, tabs, QW, False, "q_norm_rope")
    o_gs, lses = [], []
    for gi, r in enumerate(DILS):
        o_g, lse_g = _attn_fwd(q_gv[gi], k_gv[gi], *v_of[gi], r, f"attn_fwd{gi}")
        o_gs.append(o_g)
        lses.append(lse_g)
    o_mix = _combine_fwd(o_gs, lses, "attn_mix")
    x3, f1 = _mm_nn(o_mix, W_o, 0, name="o_proj", res=x2, gate=row(mod1, 2))
    x4, ffn1_saved, _, _, _ = ffn_forward(x3, 1, mod1, W_up1, c_q[0][0], (), (), ())

    dx4, loss_blk = _loss_grad(x4, target, "loss")
    loss = lax.psum(loss_blk[0, 0], ("x", "y", "c"))

    red = {}
    dx3, gf1, _ = ffn_backward(dx4, x3, 1, mod1, ffn1_saved, ())
    dy1, dgate_m1, _ = _gate_bwd(dx3, f1, row(mod1, 2), "l1_gate_bwd")
    do = _mm_nt(dy1, W_o, 0, name="o_proj_dx", out_dtype=f32, tko=1024)
    d_wo = _mm_tn(o_mix, dy1, name="o_proj_dw", col_sharded=True)
    outs, c = _combine_bwd(do, o_gs, lses, "attn_mix_bwd", comms=rs_d2d(d_wo))
    part_wo = rs_add(d_wo, c, "wo")
    do_gs, corrs = outs[:NG], outs[NG:]
    dq_gs, dk_gs, dv_gs = [], [], []
    for gi, r in enumerate(DILS):
        cm = rs_ici(part_wo) if gi == 0 else ()
        dq_g, c = with_comms(_attn_bwd_q(q_gv[gi], k_gv[gi], *v_of[gi], do_gs[gi], lses[gi], corrs[gi], r, f"attn_bwd_q{gi}",
                                         comms=cm), cm)
        if gi == 0:
            red["w_o"] = (part_wo, c[0][0])
        dq_gs.append(dq_g)
        dk_g, dv_g = _attn_bwd_kv(q_gv[gi], k_gv[gi], *v_of[gi], do_gs[gi], lses[gi], corrs[gi], r, f"attn_bwd_kv{gi}")
        dk_gs.append(dk_g)
        dv_gs.append(dv_g)
    dqraw, d_qg = _qk_bwd(dq_gs, qraw, qg, tabs, QW, (), "q_norm_rope_bwd")
    dkvraw, d_kg = _qk_bwd(dk_gs, kvraw, kg, tabs, QW, tuple(dv_gs), "k_norm_rope_bwd")
    dh1 = _mm_nt(dqraw, W_q, 0, name="q_proj_dx", out_dtype=f32)
    d_wq = _mm_tn(h1, dqraw, name="q_proj_dw", col_sharded=True)
    dhkv, c = _mm_nt(dkvraw, W_kv, 0, name="kv_proj_dx", out_dtype=f32, comms=rs_d2d(d_wq))
    part_wq = rs_add(d_wq, c, "wq")
    d_wkv, c = _mm_tn(hkv, dkvraw, name="kv_proj_dw", col_sharded=True, comms=rs_ici(gf1["part_up"]))
    red["ffn_up_w1"] = (gf1["part_up"], c[0][0])
    (dx2a, dsh_m1, dsc_m1, dg_mix1), c = _mod_bwd(dh1, x2, dx3, row(norm_mix_g, 1), row(mod1, 1), "l1_mod_bwd",
                                                  comms=rs_ici(part_wq))
    red["w_q"] = (part_wq, c[0][0])
    (dx2, dsh_kv, dsc_kv, dg_kvn), c = _mod_bwd(dhkv, x2, dx2a, kv_norm_g.reshape(1, -1), row(modkv, 1), "kv_mod_bwd",
                                                comms=rs_d2d(d_wkv))
    part_wkv = rs_add(d_wkv, c, "wkv")

    dx1, gf0, c = ffn_backward(dx2, x1, 0, mod0, ffn0_saved, rs_ici(part_wkv))
    red["w_kv"] = (part_wkv, c[0][0])
    dy0, dgate_m0, d_pw2b = _gate_bwd(dx1, f0, row(mod0, 2), "l0_gate_bwd")
    ds0 = _mm_nt(dy0, W_pw2, 0, name="l0_pw2_dx", out_dtype=bf16, tko=1024)
    d_pw2 = _mm_tn(s0, dy0, name="l0_pw2_dw", col_sharded=False)
    (dcv, d_lng, d_lnb, d_dwb, d_dww), c = _conv_bwd1(u0, ds0, dw_w32, dw_b, ln_g, ln_b, "l0_conv_bwd1",
                                                      comms=rs_ici(gf0["part_up"]))
    red["ffn_up_w0"] = (gf0["part_up"], c[0][0])
    (du0, d_pw1b), c = _conv_bwd2(dcv, u0, dw_w32, "l0_conv_bwd2", comms=rs_d2d(d_pw2))
    part_pw2 = rs_add(d_pw2, c, "pw2")
    d_pw1 = _mm_tn(h0, du0, name="l0_pw1_dw", col_sharded=True)
    dh0, c = _mm_nt(du0, W_pw1, 0, name="l0_pw1_dx", out_dtype=f32, comms=rs_ici(part_pw2) + rs_d2d(d_pw1))
    red["conv_pw2_w"] = (part_pw2, c[0][0])
    part_pw1 = rs_add(d_pw1, c[1:], "pw1")
    (grad_x, dsh_m0, dsc_m0, dg_mix0), c = _mod_bwd(dh0, x0, dx1, row(norm_mix_g, 0), row(mod0, 1), "l0_mod_bwd",
                                                    comms=rs_ici(part_pw1))
    red["conv_pw1_w"] = (part_pw1, c[0][0])
    red["ffn_down_w0"], red["ffn_down_w1"] = gf0["down"], gf1["down"]

    dm0 = [dsh_m0, dsc_m0, dgate_m0, gf0["dsh"], gf0["dsc"], gf0["dgate"]]
    dm1 = [dsh_m1, dsc_m1, dgate_m1, gf1["dsh"], gf1["dsc"], gf1["dgate"]]
    pieces = dm0 + dm1 + [dsh_kv, dsc_kv,
                          dg_mix0, dg_mix1, gf0["dg"], gf1["dg"], dg_kvn, d_kg, d_qg, gf0["d_fb"], gf1["d_fb"],
                          d_pw1b, d_dww[:CONV_K], d_dwb, d_lng, d_lnb, d_pw2b, gf0["d_fw"], gf1["d_fw"]]
    flat = jnp.concatenate([p.reshape(-1) for p in pieces])
    n_flat = flat.shape[0]
    n_rows = -(-n_flat // 1024) * 8
    flat = jnp.concatenate([flat, jnp.zeros((n_rows * 128 - n_flat,), f32)]).reshape(n_rows, 128)
    g_all = _ag_small(flat, "ag_small_grads")
    g_sum = _sum8(g_all, "sum_small_grads").reshape(-1)
    n_dm = 2 * 6 * Dm + 2 * Dm
    dm_all = g_all.reshape(NDEV, -1)[:, :n_dm]

    take_pos = [0]

    def take(shape):
        n = int(np.prod(shape))
        out = g_sum[take_pos[0]:take_pos[0] + n].reshape(shape)
        take_pos[0] += n
        return out

    g_mod_b = take((2, 6 * Dm))
    g_kv_mod_b = take((2 * Dm,))
    g_norm_mix0, g_norm_mix1 = take((Dm,)), take((Dm,))
    g_norm_ffn0, g_norm_ffn1 = take((Dm,)), take((Dm,))
    g_kv_norm = take((Dm,))
    g_k_norm = take((DH,))
    g_q_norm = take((1, DH))
    g_ffn_dw_b = take((2, F))
    shard = lambda full, n, axis: lax.dynamic_slice_in_dim(full, me * n, n, axis)
    g_pw1_b = shard(take((1, 2 * Dm)), n1, 1)
    g_dw_w = shard(take((1, CONV_K, Dm)), nd, 2)
    g_dw_b = shard(take((1, Dm)), nd, 1)
    g_ln_g = shard(take((1, Dm)), nd, 1)
    g_ln_b = shard(take((1, Dm)), nd, 1)
    g_pw2_b = shard(take((1, Dm)), nd, 1)
    g_ffn_dw_w = shard(jnp.stack([take((FFN_K, F)), take((FFN_K, F))]), nfw, 2)
    g_norm_mix = jnp.stack([g_norm_mix0, g_norm_mix1])
    g_norm_ffn = jnp.stack([g_norm_ffn0, g_norm_ffn1])

    small = [("mod_b", mod_b, m_mod_b, v_mod_b, g_mod_b), ("norm_mix_g", norm_mix_g, m_norm_mix_g, v_norm_mix_g, g_norm_mix),
             ("norm_ffn_g", norm_ffn_g, m_norm_ffn_g, v_norm_ffn_g, g_norm_ffn),
             ("conv_pw1_b", conv_pw1_b, m_conv_pw1_b, v_conv_pw1_b, g_pw1_b),
             ("conv_dw_w", conv_dw_w, m_conv_dw_w, v_conv_dw_w, g_dw_w), ("conv_dw_b", conv_dw_b, m_conv_dw_b, v_conv_dw_b, g_dw_b),
             ("conv_ln_g", conv_ln_g, m_conv_ln_g, v_conv_ln_g, g_ln_g), ("conv_ln_b", conv_ln_b, m_conv_ln_b, v_conv_ln_b, g_ln_b),
             ("conv_pw2_b", conv_pw2_b, m_conv_pw2_b, v_conv_pw2_b, g_pw2_b),
             ("kv_mod_b", kv_mod_b, m_kv_mod_b, v_kv_mod_b, g_kv_mod_b), ("kv_norm_g", kv_norm_g, m_kv_norm_g, v_kv_norm_g, g_kv_norm),
             ("k_norm_g", k_norm_g, m_k_norm_g, v_k_norm_g, g_k_norm), ("q_norm_g", q_norm_g, m_q_norm_g, v_q_norm_g, g_q_norm),
             ("ffn_dw_w", ffn_dw_w, m_ffn_dw_w, v_ffn_dw_w, g_ffn_dw_w), ("ffn_dw_b", ffn_dw_b, m_ffn_dw_b, v_ffn_dw_b, g_ffn_dw_b)]
    n_small = sum(int(np.prod(s[1].shape)) for s in small)
    rows_small = -(-n_small // 1024) * 8

    def pack(idx):
        fl = jnp.concatenate([s[idx].reshape(-1) for s in small])
        return jnp.concatenate([fl, jnp.ones((rows_small * 128 - n_small,), f32)]).reshape(rows_small, 128)

    sd, sm, sv = _adamw_plain(pack(1), pack(2), pack(3), pack(4), "adamw_small")
    res = {}
    pos = 0
    for name, w, _, _, g in small:
        n = int(np.prod(w.shape))
        cut = lambda a: a.reshape(-1)[pos:pos + n].reshape(w.shape)
        res[name] = (g.reshape(w.shape), cut(sd), cut(sm), cut(sv))
        pos += n

    c_all_t = jnp.transpose(c_all)

    def mod_update(w2d, m2d, v2d, dm_cols, tag):
        g = _modgrad(c_all_t, dm_cols, f"modgrad_{tag}")
        d, m2, v2 = _adamw_plain(w2d, m2d, v2d, g, f"adamw_{tag}")
        return g, d, m2, v2

    mw = []
    for l in range(2):
        cols = lax.dynamic_slice_in_dim(dm_all[:, l * 6 * Dm:(l + 1) * 6 * Dm], me * n_mod, n_mod, 1)
        mw.append(mod_update(mod_w[l], m_mod_w[l], v_mod_w[l], cols, f"mod_w{l}"))
    res["mod_w"] = tuple(jnp.stack([mw[0][i], mw[1][i]]) for i in range(4))
    cols = lax.dynamic_slice_in_dim(dm_all[:, 12 * Dm:], me * n_kvm, n_kvm, 1)
    res["kv_mod_w"] = mod_update(kv_mod_w, m_kv_mod_w, v_kv_mod_w, cols, "kv_mod_w")

    def mine(part):
        return lax.dynamic_index_in_dim(part, chip[0], 0, keepdims=False)

    def big(key, w, m, v, l, prev, tag, comms=()):
        part, r2 = red[key]
        return _adamw_reduced(as3(w), as3(m), as3(v), mine(part), r2, l, prev, f"adamw_{tag}", comms=comms)

    up1 = big("ffn_up_w1", ffn_up_w, m_ffn_up_w, v_ffn_up_w, 1, None, "up1")
    res["ffn_up_w"] = tuple(big("ffn_up_w0", ffn_up_w, m_ffn_up_w, v_ffn_up_w, 0, up1, "up0"))
    down1 = big("ffn_down_w1", ffn_down_w, m_ffn_down_w, v_ffn_down_w, 1, None, "down1")
    res["ffn_down_w"] = tuple(big("ffn_down_w0", ffn_down_w, m_ffn_down_w, v_ffn_down_w, 0, down1, "down0"))
    for key, w, m, v in (("conv_pw1_w", conv_pw1_w, m_conv_pw1_w, v_conv_pw1_w), ("conv_pw2_w", conv_pw2_w, m_conv_pw2_w, v_conv_pw2_w),
                         ("w_kv", w_kv, m_w_kv, v_w_kv), ("w_q", w_q, m_w_q, v_w_q), ("w_o", w_o, m_w_o, v_w_o)):
        res[key] = tuple(o.reshape(w.shape) for o in big(key, w, m, v, 0, None, key))

    order = ["mod_w", "mod_b", "norm_mix_g", "norm_ffn_g", "conv_pw1_w", "conv_pw1_b", "conv_dw_w", "conv_dw_b", "conv_ln_g",
             "conv_ln_b", "conv_pw2_w", "conv_pw2_b", "kv_mod_w", "kv_mod_b", "kv_norm_g", "w_kv", "k_norm_g", "w_q", "q_norm_g",
             "w_o", "ffn_up_w", "ffn_dw_w", "ffn_dw_b", "ffn_down_w"]
    out = [loss, grad_x.reshape(x.shape)]
    for i in range(4):
        out += [res[n][i] for n in order]
    return tuple(out)
```

```python
import functools
import math

import numpy as np
import jax
import jax.numpy as jnp
from jax import lax
from jax.experimental import pallas as pl
from jax.experimental.pallas import tpu as pltpu

f32 = jnp.float32
bf16 = jnp.bfloat16

D = 2048
SEQ = 8192
FF = 5632
CONV_K = 31
FFN_K = 3
HPG = 8
DH = 128
NG = 3
DILS = (1, 4, 16)
BLK = 128
ROT = 32
THETA = 500000.0
EPS = 1e-6
NEG = -1e30
NDEV = 8
HALO = 32
FHALO = 16

LR, B1, B2, AEPS, WD, STEP = 0.001, 0.9, 0.999, 1e-08, 0.01, 10

VMEM_BIG = 56 * 1024 * 1024

ARB = "arbitrary"
PAR = "parallel"
MESH = pl.DeviceIdType.MESH


def _cp(sem, vmem=None):
    return pltpu.CompilerParams(dimension_semantics=sem, vmem_limit_bytes=vmem)


def _tile(n, pref, mult=128):
    if n <= pref:
        return n
    t = (pref // mult) * mult
    while t >= mult:
        if n % t == 0:
            return t
        t -= mult
    return n


def _sigmoid(x):
    return 1.0 / (1.0 + jnp.exp(-x))


def _me():
    return lax.axis_index("x"), lax.axis_index("y"), lax.axis_index("c")


class _Comm:
    def __init__(self, arrays, out_shapes, sems, start, finish):
        self.arrays, self.out_shapes, self.sems, self.start, self.finish = arrays, out_shapes, sems, start, finish


def _pcall(body, *, name, grid, in_specs, out_specs, out_shape, args, scratch_shapes=(), sem=None, vmem=None, comms=(),
           aliases=None):
    aliases = aliases or {}
    if not comms:
        return pl.pallas_call(body, name=name, grid=grid, in_specs=in_specs, out_specs=out_specs, out_shape=out_shape,
                              scratch_shapes=list(scratch_shapes), input_output_aliases=aliases,
                              compiler_params=_cp(sem, vmem))(*args)
    single = not isinstance(out_shape, (list, tuple))
    outs_shape = [out_shape] if single else list(out_shape)
    outs_spec = [out_specs] if single else list(out_specs)
    n_in, n_out, n_scr = len(args), len(outs_shape), len(scratch_shapes)
    c_in = [a for cm in comms for a in cm.arrays]
    c_out = [s for cm in comms for s in cm.out_shapes]
    c_scr = [s for cm in comms for s in cm.sems]

    def split(refs, counts):
        out, pos = [], 0
        for n in counts:
            out.append(refs[pos:pos + n])
            pos += n
        return out

    def wrapped(*refs):
        ins, cins, outs, couts, scr, cscr = split(refs, [n_in, len(c_in), n_out, len(c_out), n_scr, len(c_scr)])
        ids = [pl.program_id(a) for a in range(len(grid))]
        first = functools.reduce(jnp.logical_and, [i == 0 for i in ids])
        last = functools.reduce(jnp.logical_and, [i == g - 1 for i, g in zip(ids, grid)])
        per_in = split(cins, [len(cm.arrays) for cm in comms])
        per_out = split(couts, [len(cm.out_shapes) for cm in comms])
        per_sem = split(cscr, [len(cm.sems) for cm in comms])

        @pl.when(first)
        def _():
            for cm, a, b, s in zip(comms, per_in, per_out, per_sem):
                cm.start(a, b, s)

        body(*ins, *outs, *scr)

        @pl.when(last)
        def _():
            for cm, a, b, s in zip(comms, per_in, per_out, per_sem):
                cm.finish(a, b, s)

    hbm = pl.BlockSpec(memory_space=pl.ANY)
    res = pl.pallas_call(
        wrapped, name=name, grid=grid, in_specs=list(in_specs) + [hbm] * len(c_in),
        out_specs=outs_spec + [hbm] * len(c_out), out_shape=outs_shape + c_out,
        scratch_shapes=list(scratch_shapes) + c_scr, input_output_aliases=aliases,
        compiler_params=_cp((ARB,) * len(grid), vmem))(*args, *c_in)
    main = res[0] if single else list(res[:n_out])
    return main, split(list(res[n_out:]), [len(cm.out_shapes) for cm in comms])


def _comm_allgather(w, axis):
    n = w.shape[axis]
    out_shape = list(w.shape)
    out_shape[axis] = NDEV * n

    def parts(ins, outs, sems):
        x_ref, out_ref = ins[0], outs[0]
        send_sems, recv_sems, local_sem = sems
        mx, my, mc = _me()
        chips = [(1 - mx, my), (mx, 1 - my), (1 - mx, 1 - my)]

        def blk(px, py, pc):
            start = pl.multiple_of((4 * px + 2 * py + pc) * n, n)
            if axis == 1:
                return out_ref.at[:, pl.ds(start, n), :]
            return out_ref.at[:, :, pl.ds(start, n)]

        def copy(k, block, to, src=None):
            return pltpu.make_async_remote_copy(
                src_ref=blk(*block) if src is None else src, dst_ref=blk(*block),
                send_sem=send_sems.at[k], recv_sem=recv_sems.at[k], device_id=to, device_id_type=MESH)

        me, sibling = (mx, my, mc), (mx, my, 1 - mc)
        mine = pltpu.make_async_copy(x_ref, blk(*me), local_sem)
        first = [copy(0, me, sibling, src=x_ref)] + [copy(1 + j, me, (*chip, mc), src=x_ref) for j, chip in enumerate(chips)]
        passed = [copy(4 + j, (*chip, mc), sibling) for j, chip in enumerate(chips)]
        return me, sibling, chips, mc, copy, mine, first, passed

    def start(ins, outs, sems):
        *_, mine, first, _ = parts(ins, outs, sems)
        mine.start()
        for cp in first:
            cp.start()

    def finish(ins, outs, sems):
        me, sibling, chips, mc, copy, mine, first, passed = parts(ins, outs, sems)
        for j, chip in enumerate(chips):
            copy(1 + j, (*chip, mc), me).wait_recv()
            passed[j].start()
        copy(0, sibling, me).wait_recv()
        for j, chip in enumerate(chips):
            copy(4 + j, (*chip, 1 - mc), me).wait_recv()
        for cp in first + passed:
            cp.wait_send()
        mine.wait()

    return _Comm([w], [jax.ShapeDtypeStruct(tuple(out_shape), w.dtype)],
                 [pltpu.SemaphoreType.DMA((7,)), pltpu.SemaphoreType.DMA((7,)), pltpu.SemaphoreType.DMA], start, finish)


def _comm_rs_sibling(dwb):
    def copies(ins, outs, sems):
        mx, my, mc = _me()
        return [pltpu.make_async_remote_copy(
            src_ref=ins[0].at[2 * p + (1 - mc)], dst_ref=outs[0].at[p], send_sem=sems[0].at[p], recv_sem=sems[1].at[p],
            device_id=(mx, my, 1 - mc), device_id_type=MESH) for p in range(4)]

    def start(ins, outs, sems):
        for cp in copies(ins, outs, sems):
            cp.start()

    def finish(ins, outs, sems):
        cps = copies(ins, outs, sems)
        for cp in cps:
            cp.wait_recv()
        for cp in cps:
            cp.wait_send()

    return _Comm([dwb], [jax.ShapeDtypeStruct((4,) + dwb.shape[1:], dwb.dtype)],
                 [pltpu.SemaphoreType.DMA((4,)), pltpu.SemaphoreType.DMA((4,))], start, finish)


def _comm_rs_chips(part):
    def copies(ins, outs, sems):
        mx, my, mc = _me()
        chips = [(1 - mx, my), (mx, 1 - my), (1 - mx, 1 - my)]
        return [pltpu.make_async_remote_copy(
            src_ref=ins[0].at[2 * px + py], dst_ref=outs[0].at[k], send_sem=sems[0].at[k], recv_sem=sems[1].at[k],
            device_id=(px, py, mc), device_id_type=MESH) for k, (px, py) in enumerate(chips)]

    def start(ins, outs, sems):
        for cp in copies(ins, outs, sems):
            cp.start()

    def finish(ins, outs, sems):
        cps = copies(ins, outs, sems)
        for cp in cps:
            cp.wait_recv()
        for cp in cps:
            cp.wait_send()

    return _Comm([part], [jax.ShapeDtypeStruct((3,) + part.shape[1:], part.dtype)],
                 [pltpu.SemaphoreType.DMA((3,)), pltpu.SemaphoreType.DMA((3,))], start, finish)


def _ag_small(x, name):
    r, c = x.shape

    def body(x_ref, out_ref, send_sems, recv_sems):
        mx, my, mc = _me()
        mine = 4 * mx + 2 * my + mc
        out_ref[mine] = x_ref[...]
        copies = []
        for k in range(1, NDEV):
            px = 1 - mx if (k >> 2) & 1 else mx
            py = 1 - my if (k >> 1) & 1 else my
            pc = 1 - mc if k & 1 else mc
            cp = pltpu.make_async_remote_copy(
                src_ref=x_ref, dst_ref=out_ref.at[mine], send_sem=send_sems.at[k - 1], recv_sem=recv_sems.at[k - 1],
                device_id=(px, py, pc), device_id_type=MESH)
            cp.start()
            copies.append((cp, 4 * px + 2 * py + pc))
        for k, (cp, peer) in enumerate(copies):
            pltpu.make_async_remote_copy(
                src_ref=x_ref, dst_ref=out_ref.at[peer], send_sem=send_sems.at[k], recv_sem=recv_sems.at[k],
                device_id=(mx, my, mc), device_id_type=MESH).wait_recv()
        for cp, _ in copies:
            cp.wait_send()

    return pl.pallas_call(
        body, name=name,
        out_shape=jax.ShapeDtypeStruct((NDEV, r, c), x.dtype),
        in_specs=[pl.BlockSpec(memory_space=pltpu.VMEM)],
        out_specs=pl.BlockSpec(memory_space=pltpu.VMEM),
        scratch_shapes=[pltpu.SemaphoreType.DMA((NDEV - 1,)), pltpu.SemaphoreType.DMA((NDEV - 1,))],
    )(x)


def _ag_big(w, axis, name):
    n = w.shape[axis]
    out_shape = list(w.shape)
    out_shape[axis] = NDEV * n

    def body(x_ref, out_ref, send_sems, recv_sems, local_sem):
        mx, my, mc = _me()
        me, sibling = (mx, my, mc), (mx, my, 1 - mc)
        chips = [(1 - mx, my), (mx, 1 - my), (1 - mx, 1 - my)]

        def blk(px, py, pc):
            start = pl.multiple_of((4 * px + 2 * py + pc) * n, n)
            if axis == 1:
                return out_ref.at[:, pl.ds(start, n), :]
            return out_ref.at[:, :, pl.ds(start, n)]

        def copy(k, block, to, src=None):
            return pltpu.make_async_remote_copy(
                src_ref=blk(*block) if src is None else src, dst_ref=blk(*block),
                send_sem=send_sems.at[k], recv_sem=recv_sems.at[k], device_id=to, device_id_type=MESH)

        mine = pltpu.make_async_copy(x_ref, blk(*me), local_sem)
        mine.start()
        first = [copy(0, me, sibling, src=x_ref)]
        first += [copy(1 + j, me, (*chip, mc), src=x_ref) for j, chip in enumerate(chips)]
        for cp in first:
            cp.start()
        passed = [copy(4 + j, (*chip, mc), sibling) for j, chip in enumerate(chips)]
        for j, chip in enumerate(chips):
            copy(1 + j, (*chip, mc), me).wait_recv()
            passed[j].start()
        copy(0, sibling, me).wait_recv()
        for j, chip in enumerate(chips):
            copy(4 + j, (*chip, 1 - mc), me).wait_recv()
        for cp in first + passed:
            cp.wait_send()
        mine.wait()

    return pl.pallas_call(
        body, name=name,
        out_shape=jax.ShapeDtypeStruct(tuple(out_shape), w.dtype),
        in_specs=[pl.BlockSpec(memory_space=pl.ANY)],
        out_specs=pl.BlockSpec(memory_space=pl.ANY),
        scratch_shapes=[pltpu.SemaphoreType.DMA((7,)), pltpu.SemaphoreType.DMA((7,)), pltpu.SemaphoreType.DMA],
    )(w)


def _chip_partial(dwb, r1, core, name):
    _, A, B = dwb.shape
    ta = _tile(A, 512, 16)

    def body(c_ref, a_ref, b_ref, o_ref):
        o_ref[...] = (a_ref[...].astype(f32) + b_ref[...].astype(f32)).astype(o_ref.dtype)

    grid_spec = pltpu.PrefetchScalarGridSpec(
        num_scalar_prefetch=1, grid=(4, A // ta),
        in_specs=[pl.BlockSpec((None, ta, B), lambda p, i, c: (2 * p + c[0], i, 0)),
                  pl.BlockSpec((None, ta, B), lambda p, i, c: (p, i, 0))],
        out_specs=pl.BlockSpec((None, ta, B), lambda p, i, c: (p, i, 0)))
    return pl.pallas_call(body, name=name, grid_spec=grid_spec,
                          out_shape=jax.ShapeDtypeStruct((4, A, B), dwb.dtype),
                          compiler_params=_cp((PAR, PAR)))(core, dwb, r1)


def _adam_math(w, g, m, v):
    m2 = B1 * m + (1.0 - B1) * g
    v2 = B2 * v + (1.0 - B2) * (g * g)
    m_hat = m2 / (1.0 - B1 ** STEP)
    v_hat = v2 / (1.0 - B2 ** STEP)
    delta = -LR * (m_hat / (jnp.sqrt(v_hat) + AEPS) + WD * w)
    return delta, m2, v2


def _adamw_reduced(w, m, v, mine, r2, l, prev, name, comms=()):
    L, A, B = w.shape
    ta = _tile(A, 256, 8)

    def body(w_ref, m_ref, v_ref, p_ref, r_ref, *rest):
        g_out, d_out, m_out, v_out = rest[-4:]
        g = ((p_ref[...].astype(f32) + r_ref[0].astype(f32)) + r_ref[1].astype(f32)) + r_ref[2].astype(f32)
        d, m2, v2 = _adam_math(w_ref[...], g, m_ref[...], v_ref[...])
        g_out[...] = g
        d_out[...] = d
        m_out[...] = m2
        v_out[...] = v2

    wspec = pl.BlockSpec((None, ta, B), lambda i: (l, i, 0))
    in_specs = [wspec, wspec, wspec, pl.BlockSpec((ta, B), lambda i: (i, 0)), pl.BlockSpec((3, ta, B), lambda i: (0, i, 0))]
    args = [w, m, v, mine, r2]
    aliases = {}
    if prev is not None:
        in_specs += [pl.BlockSpec(memory_space=pl.ANY)] * 4
        args += list(prev)
        aliases = {5 + i: i for i in range(4)}
    shp = jax.ShapeDtypeStruct((L, A, B), f32)
    return _pcall(body, name=name, grid=(A // ta,), in_specs=in_specs, out_specs=[wspec] * 4, out_shape=[shp] * 4,
                  args=args, sem=(PAR,), comms=comms, aliases=aliases)


def _adamw_plain(w, m, v, g, name):
    A, B = w.shape
    ta = _tile(A, 256, 8)

    def body(w_ref, m_ref, v_ref, g_ref, d_out, m_out, v_out):
        d, m2, v2 = _adam_math(w_ref[...], g_ref[...], m_ref[...], v_ref[...])
        d_out[...] = d
        m_out[...] = m2
        v_out[...] = v2

    spec = pl.BlockSpec((ta, B), lambda i: (i, 0))
    shp = jax.ShapeDtypeStruct((A, B), f32)
    return pl.pallas_call(body, name=name, grid=(A // ta,), in_specs=[spec] * 4, out_specs=[spec] * 3,
                          out_shape=[shp, shp, shp], compiler_params=_cp((PAR,)))(w, m, v, g)


def _sum8(g, name):
    _, R, C = g.shape

    def body(g_ref, o_ref):
        acc = g_ref[0]
        for j in range(1, NDEV):
            acc = acc + g_ref[j]
        o_ref[...] = acc

    return pl.pallas_call(body, name=name, out_shape=jax.ShapeDtypeStruct((R, C), f32))(g)


def _modproj(c_all, w, bias, name):
    K, N = w.shape
    tn = _tile(N, 512)

    def body(c_ref, w_ref, b_ref, o_ref):
        cc = c_ref[...]
        sc = (cc * _sigmoid(cc)).astype(bf16)
        o_ref[...] = jnp.dot(sc, w_ref[...].astype(bf16), preferred_element_type=f32) + b_ref[...]

    return pl.pallas_call(
        body, name=name, grid=(N // tn,),
        in_specs=[pl.BlockSpec((NDEV, K), lambda j: (0, 0)), pl.BlockSpec((K, tn), lambda j: (0, j)),
                  pl.BlockSpec((1, tn), lambda j: (0, j))],
        out_specs=pl.BlockSpec((NDEV, tn), lambda j: (0, j)),
        out_shape=jax.ShapeDtypeStruct((NDEV, N), f32), compiler_params=_cp((PAR,)))(c_all, w, bias)


def _modgrad(c_all_t, dm, name):
    K = c_all_t.shape[0]
    N = dm.shape[1]
    tn = _tile(N, 512)

    def body(c_ref, d_ref, o_ref):
        cc = c_ref[...]
        sc = cc * _sigmoid(cc)
        dmv = d_ref[...]
        acc = sc[:, 0:1] * dmv[0:1, :]
        for b in range(1, NDEV):
            acc = acc + sc[:, b:b + 1] * dmv[b:b + 1, :]
        o_ref[...] = acc

    return pl.pallas_call(
        body, name=name, grid=(N // tn,),
        in_specs=[pl.BlockSpec((K, NDEV), lambda j: (0, 0)), pl.BlockSpec((NDEV, tn), lambda j: (0, j))],
        out_specs=pl.BlockSpec((K, tn), lambda j: (0, j)),
        out_shape=jax.ShapeDtypeStruct((K, N), f32), compiler_params=_cp((PAR,)))(c_all_t, dm)


def _mm_nn(a, w, l, *, name, out_dtype=bf16, bias=None, res=None, gate=None, tm=1024, tn=1024, tk=2048, comms=()):
    M, K = a.shape
    N = w.shape[2]
    tm, tn, tk = _tile(M, tm, 8), _tile(N, tn), _tile(K, tk)
    nk = K // tk
    epi = res is not None

    def body(*refs):
        it = iter(refs)
        a_ref, w_ref = next(it), next(it)
        b_ref = next(it) if bias is not None else None
        r_ref = next(it) if epi else None
        g_ref = next(it) if epi else None
        o_ref = next(it)
        f_ref = next(it) if epi else None

        def finish(y):
            if b_ref is not None:
                y = y + b_ref[...]
            if epi:
                f_ref[...] = y.astype(f_ref.dtype)
                o_ref[...] = r_ref[...] + g_ref[...] * y
            else:
                o_ref[...] = y.astype(o_ref.dtype)

        if nk == 1:
            finish(jnp.dot(a_ref[...], w_ref[...], preferred_element_type=f32))
            return
        acc = next(it)
        k = pl.program_id(2)

        @pl.when(k == 0)
        def _():
            acc[...] = jnp.zeros_like(acc)

        acc[...] += jnp.dot(a_ref[...], w_ref[...], preferred_element_type=f32)

        @pl.when(k == nk - 1)
        def _():
            finish(acc[...])

    in_specs = [pl.BlockSpec((tm, tk), lambda i, j, k: (i, k)), pl.BlockSpec((None, tk, tn), lambda i, j, k: (l, k, j))]
    args = [a, w]
    if bias is not None:
        in_specs.append(pl.BlockSpec((1, tn), lambda i, j, k: (0, j)))
        args.append(bias)
    ospec = pl.BlockSpec((tm, tn), lambda i, j, k: (i, j))
    if epi:
        in_specs += [ospec, pl.BlockSpec((1, tn), lambda i, j, k: (0, j))]
        args += [res, gate]
        out_shape = [jax.ShapeDtypeStruct((M, N), f32), jax.ShapeDtypeStruct((M, N), bf16)]
        out_specs = [ospec, ospec]
    else:
        out_shape = jax.ShapeDtypeStruct((M, N), out_dtype)
        out_specs = ospec
    return _pcall(body, name=name, grid=(M // tm, N // tn, nk), in_specs=in_specs, out_specs=out_specs, out_shape=out_shape,
                  args=args, scratch_shapes=[pltpu.VMEM((tm, tn), f32)] if nk > 1 else [], sem=(PAR, PAR, ARB), vmem=VMEM_BIG,
                  comms=comms)


def _mm_nt(a, w, l, *, name, out_dtype, tm=1024, tko=2048, tn=1024, comms=()):
    planes = a.ndim == 3
    M = a.shape[-2]
    K, N = w.shape[1], w.shape[2]
    npl = a.shape[-1]
    tm, tko = _tile(M, tm, 8), _tile(K, tko)
    tn = _tile(npl, tn)
    nn = N // tn
    per_plane = npl // tn

    def body(a_ref, w_ref, o_ref, *scratch):
        if nn == 1:
            o_ref[...] = _dot_nt(a_ref[...], w_ref[...]).astype(o_ref.dtype)
            return
        acc = scratch[0]
        k = pl.program_id(2)

        @pl.when(k == 0)
        def _():
            acc[...] = jnp.zeros_like(acc)

        acc[...] += _dot_nt(a_ref[...], w_ref[...])

        @pl.when(k == nn - 1)
        def _():
            o_ref[...] = acc[...].astype(o_ref.dtype)

    if planes:
        a_spec = pl.BlockSpec((None, tm, tn), lambda i, j, k: (k // per_plane, i, k % per_plane))
    else:
        a_spec = pl.BlockSpec((tm, tn), lambda i, j, k: (i, k))
    return _pcall(body, name=name, grid=(M // tm, K // tko, nn),
                  in_specs=[a_spec, pl.BlockSpec((None, tko, tn), lambda i, j, k: (l, j, k))],
                  out_specs=pl.BlockSpec((tm, tko), lambda i, j, k: (i, j)),
                  out_shape=jax.ShapeDtypeStruct((M, K), out_dtype), args=[a, w],
                  scratch_shapes=[pltpu.VMEM((tm, tko), f32)] if nn > 1 else [], sem=(PAR, PAR, ARB), vmem=VMEM_BIG,
                  comms=comms)


def _mm_tn(a, b, *, name, col_sharded, tk=2048, ts=1024, comms=()):
    planes = b.ndim == 3
    S, K = a.shape
    N = b.shape[-1] * (2 if planes else 1)
    ts = _tile(S, ts, 16)
    ns_steps = S // ts
    if col_sharded:
        tn = N // NDEV
        tk = _tile(K, tk)
    else:
        tn = N
        tk = _tile(K, 1408)
    per_plane = (b.shape[-1] // tn) if planes else 0

    def body(a_ref, b_ref, o_ref, acc):
        s = pl.program_id(2)

        @pl.when(s == 0)
        def _():
            acc[...] = jnp.zeros_like(acc)

        acc[...] += lax.dot_general(a_ref[...], b_ref[...], (((0,), (0,)), ((), ())), preferred_element_type=f32)

        @pl.when(s == ns_steps - 1)
        def _():
            o_ref[...] = acc[...].astype(o_ref.dtype)

    if planes:
        b_spec = pl.BlockSpec((None, ts, tn), lambda k, n, s: (n // per_plane, s, n % per_plane))
    else:
        b_spec = pl.BlockSpec((ts, tn), lambda k, n, s: (s, n))
    if col_sharded:
        out_shape = jax.ShapeDtypeStruct((NDEV, K, tn), bf16)
        out_spec = pl.BlockSpec((None, tk, tn), lambda k, n, s: (n, k, 0))
    else:
        out_shape = jax.ShapeDtypeStruct((K, N), bf16)
        out_spec = pl.BlockSpec((tk, tn), lambda k, n, s: (k, n))
    res = _pcall(body, name=name, grid=(K // tk, N // tn, ns_steps),
                 in_specs=[pl.BlockSpec((ts, tk), lambda k, n, s: (s, k)), b_spec],
                 out_specs=out_spec, out_shape=out_shape, args=[a, b],
                 scratch_shapes=[pltpu.VMEM((tk, tn), f32)], sem=(PAR, PAR, ARB), vmem=VMEM_BIG, comms=comms)
    out, couts = res if comms else (res, None)
    if not col_sharded:
        out = out.reshape(NDEV, K // NDEV, N)
    return (out, couts) if comms else out


def _acc_spec(w, rows=1):
    return pl.BlockSpec((rows, w), lambda i: (0, 0))


def _mod_fwd(x, g, sh, sc, name):
    S, W = x.shape
    tm = _tile(S, 256, 8)

    def body(x_ref, g_ref, sh_ref, sc_ref, h_ref):
        xv = x_ref[...]
        r = lax.rsqrt(jnp.mean(xv * xv, axis=-1, keepdims=True) + EPS)
        h_ref[...] = ((xv * r) * g_ref[...] * (1.0 + sc_ref[...]) + sh_ref[...]).astype(h_ref.dtype)

    row = pl.BlockSpec((tm, W), lambda i: (i, 0))
    return pl.pallas_call(body, name=name, grid=(S // tm,), in_specs=[row, _acc_spec(W), _acc_spec(W), _acc_spec(W)],
                          out_specs=row, out_shape=jax.ShapeDtypeStruct((S, W), bf16), compiler_params=_cp((PAR,)))(x, g, sh, sc)


def _mod_bwd(dh, x, dx_in, g, sc, name, comms=()):
    S, W = x.shape
    tm = _tile(S, 256, 8)
    nt = S // tm

    def body(dh_ref, x_ref, dxi_ref, g_ref, sc_ref, dx_ref, dsh_ref, dsc_ref, dg_ref):
        i = pl.program_id(0)

        @pl.when(i == 0)
        def _():
            dsh_ref[...] = jnp.zeros_like(dsh_ref)
            dsc_ref[...] = jnp.zeros_like(dsc_ref)

        xv = x_ref[...]
        dh = dh_ref[...].astype(f32)
        r = lax.rsqrt(jnp.mean(xv * xv, axis=-1, keepdims=True) + EPS)
        n = xv * r
        dn = dh * (g_ref[...] * (1.0 + sc_ref[...]))
        dx = r * (dn - n * jnp.mean(dn * n, axis=-1, keepdims=True))
        dx_ref[...] = dxi_ref[...] + dx
        dsh_ref[...] += jnp.sum(dh, axis=0, keepdims=True)
        dsc_ref[...] += jnp.sum(dh * n, axis=0, keepdims=True)

        @pl.when(i == nt - 1)
        def _():
            a2 = dsc_ref[...]
            dg_ref[...] = a2 * (1.0 + sc_ref[...])
            dsc_ref[...] = a2 * g_ref[...]

    row = pl.BlockSpec((tm, W), lambda i: (i, 0))
    vec = jax.ShapeDtypeStruct((1, W), f32)
    return _pcall(body, name=name, grid=(nt,), in_specs=[row, row, row, _acc_spec(W), _acc_spec(W)],
                  out_specs=[row, _acc_spec(W), _acc_spec(W), _acc_spec(W)],
                  out_shape=[jax.ShapeDtypeStruct((S, W), f32), vec, vec, vec], args=[dh, x, dx_in, g, sc], sem=(ARB,),
                  comms=comms)


def _gate_bwd(dx, f, gate, name):
    S, W = dx.shape
    tm = _tile(S, 256, 16)

    def body(dx_ref, f_ref, g_ref, df_ref, dg_ref, sdf_ref):
        i = pl.program_id(0)

        @pl.when(i == 0)
        def _():
            dg_ref[...] = jnp.zeros_like(dg_ref)
            sdf_ref[...] = jnp.zeros_like(sdf_ref)

        d = dx_ref[...]
        df = g_ref[...] * d
        df_ref[...] = df.astype(df_ref.dtype)
        dg_ref[...] += jnp.sum(d * f_ref[...].astype(f32), axis=0, keepdims=True)
        sdf_ref[...] += jnp.sum(df, axis=0, keepdims=True)

    row = pl.BlockSpec((tm, W), lambda i: (i, 0))
    vec = jax.ShapeDtypeStruct((1, W), f32)
    return pl.pallas_call(
        body, name=name, grid=(S // tm,), in_specs=[row, row, _acc_spec(W)], out_specs=[row, _acc_spec(W), _acc_spec(W)],
        out_shape=[jax.ShapeDtypeStruct((S, W), bf16), vec, vec], compiler_params=_cp((ARB,)))(dx, f, gate)


def _loss_grad(y, target, name):
    S, W = y.shape
    tm = _tile(S, 256, 8)

    def body(y_ref, t_ref, dy_ref, l_ref):
        i = pl.program_id(0)

        @pl.when(i == 0)
        def _():
            l_ref[...] = jnp.zeros_like(l_ref)

        e = y_ref[...] - t_ref[...]
        dy_ref[...] = e * (1.0 / W)
        l_ref[...] += 0.5 * jnp.sum(jnp.mean(e * e, axis=-1, keepdims=True))

    row = pl.BlockSpec((tm, W), lambda i: (i, 0))
    return pl.pallas_call(
        body, name=name, grid=(S // tm,), in_specs=[row, row], out_specs=[row, pl.BlockSpec((8, 128), lambda i: (0, 0))],
        out_shape=[jax.ShapeDtypeStruct((S, W), f32), jax.ShapeDtypeStruct((8, 128), f32)],
        compiler_params=_cp((ARB,)))(y, target)


def _tap_groups(offsets):
    groups = {}
    for k, o in enumerate(offsets):
        groups.setdefault(o % 8, []).append((k, o - o % 8))
    return sorted(groups.items())


def _tap_sum(buf, w, offsets, tm):
    out = None
    for b, taps in _tap_groups(offsets):
        n = tm + 8 if b else tm
        y = None
        for k, base in taps:
            term = w[k:k + 1, :] * buf[pl.ds(base, n), :]
            y = term if y is None else y + term
        part = y[b:b + tm] if b else y
        out = part if out is None else out + part
    return out


def _tap_wgrad(d, buf, dsh, acc_ref, offsets, tm):
    for b, taps in _tap_groups(offsets):
        if b:
            dsh[pl.ds(0, 8), :] = jnp.zeros((8, dsh.shape[1]), f32)
            dsh[pl.ds(tm, 8), :] = jnp.zeros((8, dsh.shape[1]), f32)
            dsh[pl.ds(b, tm), :] = d
            dd, n = dsh[...], tm + 8
        else:
            dd, n = d, tm
        for k, base in taps:
            acc_ref[pl.ds(k, 1), :] += jnp.sum(dd * buf[pl.ds(base, n), :], axis=0, keepdims=True)


_CONV_OFFSETS = [HALO - (CONV_K - 1) + k for k in range(CONV_K)]
_CONV_OFFSETS_T = [CONV_K - 1 - k for k in range(CONV_K)]


def _conv_core(u_ref, uh_ref, w_ref, b_ref, lg_ref, lb_ref, gbuf, tm, first):
    C = u_ref.shape[1] // 2
    u = u_ref[...].astype(f32)
    uh = uh_ref[...].astype(f32)
    gbuf[pl.ds(HALO, tm), :] = u[:, :C] * _sigmoid(u[:, C:])
    halo = uh[:, :C] * _sigmoid(uh[:, C:])
    gbuf[pl.ds(0, HALO), :] = jnp.where(first, 0.0, halo)
    cv = _tap_sum(gbuf, w_ref[...], _CONV_OFFSETS, tm) + b_ref[...]
    mu = jnp.mean(cv, axis=-1, keepdims=True)
    xc = cv - mu
    rstd = lax.rsqrt(jnp.mean(xc * xc, axis=-1, keepdims=True) + EPS)
    z = xc * rstd
    ln = z * lg_ref[...] + lb_ref[...]
    return z, rstd, ln


def _halo_prev(tm, hb, w):
    return pl.BlockSpec((hb, w), lambda i: (jnp.maximum(i * (tm // hb) - 1, 0), 0))


def _conv_fwd(u, w, b, lg, lb, name, comms=()):
    S, C2 = u.shape
    C = C2 // 2
    tm = _tile(S, 256, HALO)

    def body(u_ref, uh_ref, w_ref, b_ref, lg_ref, lb_ref, s_ref, gbuf):
        first = pl.program_id(0) == 0
        _, _, ln = _conv_core(u_ref, uh_ref, w_ref, b_ref, lg_ref, lb_ref, gbuf, tm, first)
        s_ref[...] = (ln * _sigmoid(ln)).astype(s_ref.dtype)

    return _pcall(body, name=name, grid=(S // tm,),
                  in_specs=[pl.BlockSpec((tm, C2), lambda i: (i, 0)), _halo_prev(tm, HALO, C2), _acc_spec(C, 32),
                            _acc_spec(C), _acc_spec(C), _acc_spec(C)],
                  out_specs=pl.BlockSpec((tm, C), lambda i: (i, 0)), out_shape=jax.ShapeDtypeStruct((S, C), bf16),
                  args=[u, u, w, b, lg, lb], scratch_shapes=[pltpu.VMEM((tm + HALO, C), f32)], sem=(PAR,), vmem=VMEM_BIG,
                  comms=comms)


def _conv_bwd1(u, ds, w, b, lg, lb, name, comms=()):
    S, C2 = u.shape
    C = C2 // 2
    tm = _tile(S, 256, HALO)

    def body(u_ref, uh_ref, ds_ref, w_ref, b_ref, lg_ref, lb_ref, dcv_ref, dlg_ref, dlb_ref, ddb_ref, ddw_ref, gbuf, dsh):
        i = pl.program_id(0)

        @pl.when(i == 0)
        def _():
            dlg_ref[...] = jnp.zeros_like(dlg_ref)
            dlb_ref[...] = jnp.zeros_like(dlb_ref)
            ddb_ref[...] = jnp.zeros_like(ddb_ref)
            ddw_ref[...] = jnp.zeros_like(ddw_ref)

        z, rstd, ln = _conv_core(u_ref, uh_ref, w_ref, b_ref, lg_ref, lb_ref, gbuf, tm, i == 0)
        sg = _sigmoid(ln)
        dln = ds_ref[...].astype(f32) * (sg * (1.0 + ln * (1.0 - sg)))
        dlg_ref[...] += jnp.sum(dln * z, axis=0, keepdims=True)
        dlb_ref[...] += jnp.sum(dln, axis=0, keepdims=True)
        dz = dln * lg_ref[...]
        dcv = rstd * (dz - jnp.mean(dz, axis=-1, keepdims=True) - z * jnp.mean(dz * z, axis=-1, keepdims=True))
        dcv_ref[...] = dcv
        ddb_ref[...] += jnp.sum(dcv, axis=0, keepdims=True)
        _tap_wgrad(dcv, gbuf, dsh, ddw_ref, _CONV_OFFSETS, tm)

    vec = jax.ShapeDtypeStruct((1, C), f32)
    return _pcall(
        body, name=name, grid=(S // tm,),
        in_specs=[pl.BlockSpec((tm, C2), lambda i: (i, 0)), _halo_prev(tm, HALO, C2), pl.BlockSpec((tm, C), lambda i: (i, 0)),
                  _acc_spec(C, 32), _acc_spec(C), _acc_spec(C), _acc_spec(C)],
        out_specs=[pl.BlockSpec((tm, C), lambda i: (i, 0)), _acc_spec(C), _acc_spec(C), _acc_spec(C), _acc_spec(C, 32)],
        out_shape=[jax.ShapeDtypeStruct((S, C), f32), vec, vec, vec, jax.ShapeDtypeStruct((32, C), f32)],
        args=[u, u, ds, w, b, lg, lb], scratch_shapes=[pltpu.VMEM((tm + HALO, C), f32), pltpu.VMEM((tm + 8, C), f32)],
        sem=(ARB,), vmem=VMEM_BIG, comms=comms)


def _conv_bwd2(dcv, u, w, name, comms=()):
    S, C2 = u.shape
    C = C2 // 2
    tm = _tile(S, 256, HALO)
    nt = S // tm
    nhb = S // HALO

    def body(d_ref, dn_ref, u_ref, w_ref, du_ref, db_ref, dbuf):
        i = pl.program_id(0)

        @pl.when(i == 0)
        def _():
            db_ref[...] = jnp.zeros_like(db_ref)

        dbuf[pl.ds(0, tm), :] = d_ref[...]
        dbuf[pl.ds(tm, HALO), :] = jnp.where(i == nt - 1, 0.0, dn_ref[...])
        dglu = _tap_sum(dbuf, w_ref[...], _CONV_OFFSETS_T, tm)
        u = u_ref[...].astype(f32)
        a, gt = u[:, :C], u[:, C:]
        sg = _sigmoid(gt)
        da = dglu * sg
        dgt = dglu * a * sg * (1.0 - sg)
        du_ref[:, :C] = da.astype(du_ref.dtype)
        du_ref[:, C:] = dgt.astype(du_ref.dtype)
        db_ref[:, :C] += jnp.sum(da, axis=0, keepdims=True)
        db_ref[:, C:] += jnp.sum(dgt, axis=0, keepdims=True)

    return _pcall(
        body, name=name, grid=(nt,),
        in_specs=[pl.BlockSpec((tm, C), lambda i: (i, 0)),
                  pl.BlockSpec((HALO, C), lambda i: (jnp.minimum((i + 1) * (tm // HALO), nhb - 1), 0)),
                  pl.BlockSpec((tm, C2), lambda i: (i, 0)), _acc_spec(C, 32)],
        out_specs=[pl.BlockSpec((tm, C2), lambda i: (i, 0)), _acc_spec(C2)],
        out_shape=[jax.ShapeDtypeStruct((S, C2), bf16), jax.ShapeDtypeStruct((1, C2), f32)],
        args=[dcv, dcv, u, w], scratch_shapes=[pltpu.VMEM((tm + HALO, C), f32)], sem=(ARB,), vmem=VMEM_BIG, comms=comms)


def _ffn_gate_fwd(u2, w, b, name, comms=()):
    S, F2 = u2.shape
    F = F2 // 2
    cw = _tile(F, 1408)
    ncw = F // cw
    tm = _tile(S, 256, FHALO)

    def body(g_ref, gh_ref, v_ref, w_ref, b_ref, a_ref, gbuf):
        first = pl.program_id(0) == 0
        gbuf[pl.ds(FHALO, tm), :] = g_ref[...].astype(f32)
        gbuf[pl.ds(0, FHALO), :] = jnp.where(first, 0.0, gh_ref[...].astype(f32))
        w = w_ref[...]
        gc = jnp.zeros((tm, cw), f32) + b_ref[...]
        for k in range(FFN_K):
            gc = gc + w[k:k + 1, :] * gbuf[pl.ds(FHALO - (FFN_K - 1) + k, tm), :]
        a_ref[...] = (gc * _sigmoid(gc) * v_ref[...].astype(f32)).astype(a_ref.dtype)

    return _pcall(
        body, name=name, grid=(S // tm, ncw),
        in_specs=[pl.BlockSpec((tm, cw), lambda i, j: (i, j)),
                  pl.BlockSpec((FHALO, cw), lambda i, j: (jnp.maximum(i * (tm // FHALO) - 1, 0), j)),
                  pl.BlockSpec((tm, cw), lambda i, j: (i, ncw + j)),
                  pl.BlockSpec((8, cw), lambda i, j: (0, j)), pl.BlockSpec((1, cw), lambda i, j: (0, j))],
        out_specs=pl.BlockSpec((tm, cw), lambda i, j: (i, j)), out_shape=jax.ShapeDtypeStruct((S, F), bf16),
        args=[u2, u2, u2, w, b], scratch_shapes=[pltpu.VMEM((tm + FHALO, cw), f32)], sem=(PAR, PAR), comms=comms)


def _ffn_gate_bwd(u2, dact, w, b, name, comms=()):
    S, F2 = u2.shape
    F = F2 // 2
    cw = _tile(F, 1408)
    ncw = F // cw
    tm = _tile(S, 256, FHALO)
    nt = S // tm
    nhb = S // FHALO
    R = tm + 2 * FHALO

    def body(g_ref, gp_ref, gn_ref, v_ref, vn_ref, d_ref, dn_ref, w_ref, b_ref, du_ref, dw_ref, db_ref, gbuf, dbuf):
        i = pl.program_id(1)
        first, last = i == 0, i == nt - 1

        @pl.when(i == 0)
        def _():
            dw_ref[...] = jnp.zeros_like(dw_ref)
            db_ref[...] = jnp.zeros_like(db_ref)

        gbuf[pl.ds(0, FHALO), :] = jnp.where(first, 0.0, gp_ref[...].astype(f32))
        gbuf[pl.ds(FHALO, tm), :] = g_ref[...].astype(f32)
        gbuf[pl.ds(FHALO + tm, FHALO), :] = gn_ref[...].astype(f32)
        w = w_ref[...]
        n_ext = tm + FHALO
        gc = jnp.zeros((n_ext, cw), f32) + b_ref[...]
        for k in range(FFN_K):
            gc = gc + w[k:k + 1, :] * gbuf[pl.ds(FHALO - (FFN_K - 1) + k, n_ext), :]
        sg = _sigmoid(gc)
        val = jnp.concatenate([v_ref[...].astype(f32), vn_ref[...].astype(f32)], axis=0)
        dact_ext = jnp.concatenate([d_ref[...].astype(f32), jnp.where(last, 0.0, dn_ref[...].astype(f32))], axis=0)
        dgc = dact_ext * val * (sg * (1.0 + gc * (1.0 - sg)))
        dbuf[...] = dgc
        dval = dact_ext[:tm] * (gc[:tm] * sg[:tm])
        dgt = jnp.zeros((tm, cw), f32)
        for k in range(FFN_K):
            dgt = dgt + w[k:k + 1, :] * dbuf[pl.ds(FFN_K - 1 - k, tm), :]
        du_ref[0] = dgt.astype(du_ref.dtype)
        du_ref[1] = dval.astype(du_ref.dtype)
        dgc_t = dgc[:tm]
        db_ref[...] += jnp.sum(dgc_t, axis=0, keepdims=True)
        for k in range(FFN_K):
            dw_ref[pl.ds(k, 1), :] += jnp.sum(dgc_t * gbuf[pl.ds(FHALO - (FFN_K - 1) + k, tm), :], axis=0, keepdims=True)

    hb = tm // FHALO
    prev = lambda j, i: (jnp.maximum(i * hb - 1, 0), j)
    nxt = lambda j, i: (jnp.minimum((i + 1) * hb, nhb - 1), j)
    nxt_v = lambda j, i: (jnp.minimum((i + 1) * hb, nhb - 1), ncw + j)
    return _pcall(
        body, name=name, grid=(ncw, nt), comms=comms, sem=(PAR, ARB), vmem=VMEM_BIG,
        args=[u2, u2, u2, u2, u2, dact, dact, w, b],
        in_specs=[pl.BlockSpec((tm, cw), lambda j, i: (i, j)), pl.BlockSpec((FHALO, cw), prev), pl.BlockSpec((FHALO, cw), nxt),
                  pl.BlockSpec((tm, cw), lambda j, i: (i, ncw + j)), pl.BlockSpec((FHALO, cw), nxt_v),
                  pl.BlockSpec((tm, cw), lambda j, i: (i, j)), pl.BlockSpec((FHALO, cw), nxt),
                  pl.BlockSpec((8, cw), lambda j, i: (0, j)), pl.BlockSpec((1, cw), lambda j, i: (0, j))],
        out_specs=[pl.BlockSpec((2, tm, cw), lambda j, i: (0, i, j)), pl.BlockSpec((8, cw), lambda j, i: (0, j)),
                   pl.BlockSpec((1, cw), lambda j, i: (0, j))],
        out_shape=[jax.ShapeDtypeStruct((2, S, F), bf16), jax.ShapeDtypeStruct((8, F), f32), jax.ShapeDtypeStruct((1, F), f32)],
        scratch_shapes=[pltpu.VMEM((R, cw), f32), pltpu.VMEM((tm + FHALO, cw), f32)])


def _rope_tables(pos_col, name):
    S = pos_col.shape[0]
    tm = _tile(S, 512, 8)
    half = ROT // 2
    inv = THETA ** (-np.arange(0, ROT, 2, dtype=np.float32) / ROT)
    lane_freq = np.zeros((1, DH), np.float32)
    lane_freq[0, :half] = inv
    lane_freq[0, half:ROT] = inv
    lane_freq = jnp.asarray(lane_freq)

    def body(p_ref, fr_ref, c_ref, sa_ref, sb_ref):
        ang = p_ref[...].astype(f32) * fr_ref[...]
        lane = lax.broadcasted_iota(jnp.int32, (tm, DH), 1)
        cs, sn = jnp.cos(ang), jnp.sin(ang)
        c_ref[...] = jnp.where(lane < ROT, cs, 1.0)
        sa_ref[...] = jnp.where(lane < half, -sn, 0.0)
        sb_ref[...] = jnp.where((lane >= half) & (lane < ROT), sn, 0.0)

    row = pl.BlockSpec((tm, DH), lambda i: (i, 0))
    shp = jax.ShapeDtypeStruct((S, DH), f32)
    return pl.pallas_call(body, name=name, grid=(S // tm,),
                          in_specs=[pl.BlockSpec((tm, 1), lambda i: (i, 0)), pl.BlockSpec((1, DH), lambda i: (0, 0))],
                          out_specs=[row, row, row], out_shape=[shp, shp, shp], compiler_params=_cp((PAR,)))(pos_col, lane_freq)


def _rope(n, c, sa, sb):
    return n * c + pltpu.roll(n, DH - ROT // 2, 1) * sa + pltpu.roll(n, ROT // 2, 1) * sb


def _rope_t(d, c, sa, sb):
    return d * c + pltpu.roll(d * sa, ROT // 2, 1) + pltpu.roll(d * sb, DH - ROT // 2, 1)


def _qk_fwd(raw, g, tabs, width, with_values, name):
    S = raw.shape[0]
    nh = width // DH
    ow = width // NG
    hpg = ow // DH
    tm = _tile(S, 256, 16 * max(DILS))
    vgroups = [gi for gi, r in enumerate(DILS) if r > 1] if with_values else []

    def body(x_ref, g_ref, c_ref, sa_ref, sb_ref, *rest):
        o_refs = rest[:NG]
        v_refs = rest[NG:NG + len(vgroups)]
        scr, vscr = rest[NG + len(vgroups):]
        c, sa, sb = c_ref[...], sa_ref[...], sb_ref[...]
        for gi, r in enumerate(DILS):
            heads = range(gi * hpg, (gi + 1) * hpg)
            xs = [x_ref[:, h * DH:(h + 1) * DH].astype(f32) for h in heads]
            rs = [lax.rsqrt(jnp.mean(xv * xv, axis=-1, keepdims=True) + EPS) for xv in xs]
            ys = [_rope(xv * rv * g_ref[...], c, sa, sb) for xv, rv in zip(xs, rs)]
            for hh, y in enumerate(ys):
                if r == 1:
                    o_refs[gi][:, hh * DH:(hh + 1) * DH] = y.astype(bf16)
                else:
                    scr[hh] = y
            if r > 1:
                for hh in range(hpg):
                    for j in range(r):
                        o_refs[gi][:, j * ow + hh * DH:j * ow + (hh + 1) * DH] = scr[hh, pl.ds(j, tm // r, stride=r), :].astype(bf16)
        for vi, gi in enumerate(vgroups):
            _to_view(x_ref[:, width + gi * ow:width + (gi + 1) * ow].astype(f32), v_refs[vi], vscr, DILS[gi], ow, tm)

    win = raw.shape[1] if with_values else width
    row = pl.BlockSpec((tm, win), lambda i: (i, 0))
    tab = pl.BlockSpec((tm, DH), lambda i: (i, 0))
    view = lambda r: pl.BlockSpec((tm // r, r * ow), lambda i: (i, 0))
    vshape = lambda r: jax.ShapeDtypeStruct((S // r, r * ow), bf16)
    outs = pl.pallas_call(
        body, name=name, grid=(S // tm,), in_specs=[row, _acc_spec(DH), tab, tab, tab],
        out_specs=[view(r) for r in DILS] + [view(DILS[gi]) for gi in vgroups],
        out_shape=[vshape(r) for r in DILS] + [vshape(DILS[gi]) for gi in vgroups],
        scratch_shapes=[pltpu.VMEM((hpg, tm, DH), f32), pltpu.VMEM((ow // DH, tm, DH), f32)],
        compiler_params=_cp((PAR,)))(raw, g, *tabs)
    return outs[:NG], outs[NG:]


def _qk_bwd(dparts, raw, g, tabs, width, extra, name):
    S = raw.shape[0]
    nh = width // DH
    ow = width // NG
    hpg = ow // DH
    tm = _tile(S, 256, 16 * max(DILS))
    wout = width + len(extra) * ow

    def body(*refs):
        d_refs = refs[:NG]
        x_ref, g_ref, c_ref, sa_ref, sb_ref = refs[NG:NG + 5]
        e_refs = refs[NG + 5:NG + 5 + len(extra)]
        o_ref, dg_ref, scr, vscr = refs[NG + 5 + len(extra):]
        i = pl.program_id(0)

        @pl.when(i == 0)
        def _():
            dg_ref[...] = jnp.zeros_like(dg_ref)

        c, sa, sb = c_ref[...], sa_ref[...], sb_ref[...]
        gv = g_ref[...]
        dg = jnp.zeros((1, DH), f32)
        for gi, r in enumerate(DILS):
            heads = list(range(gi * hpg, (gi + 1) * hpg))
            if r == 1:
                douts = [d_refs[gi][:, hh * DH:(hh + 1) * DH].astype(f32) for hh in range(hpg)]
            else:
                for hh in range(hpg):
                    for j in range(r):
                        scr[hh, pl.ds(j, tm // r, stride=r), :] = d_refs[gi][:, j * ow + hh * DH:j * ow + (hh + 1) * DH].astype(f32)
                douts = [scr[hh] for hh in range(hpg)]
            xs = [x_ref[:, h * DH:(h + 1) * DH].astype(f32) for h in heads]
            rs = [lax.rsqrt(jnp.mean(xv * xv, axis=-1, keepdims=True) + EPS) for xv in xs]
            xhs = [xv * rv for xv, rv in zip(xs, rs)]
            dns = [_rope_t(d, c, sa, sb) for d in douts]
            for dn, xh in zip(dns, xhs):
                dg = dg + jnp.sum(dn * xh, axis=0, keepdims=True)
            dxns = [dn * gv for dn in dns]
            dxs = [rv * (dxn - xh * jnp.mean(dxn * xh, axis=-1, keepdims=True)) for rv, dxn, xh in zip(rs, dxns, xhs)]
            for h, dx in zip(heads, dxs):
                o_ref[:, h * DH:(h + 1) * DH] = dx.astype(o_ref.dtype)
        for gi, e_ref in enumerate(e_refs):
            o_ref[:, width + gi * ow:width + (gi + 1) * ow] = _from_view(e_ref, vscr, DILS[gi], ow, tm).astype(o_ref.dtype)
        dg_ref[...] += dg

    views = [pl.BlockSpec((tm // r, r * ow), lambda i: (i, 0)) for r in DILS]
    tab = pl.BlockSpec((tm, DH), lambda i: (i, 0))
    return pl.pallas_call(
        body, name=name, grid=(S // tm,),
        in_specs=views + [pl.BlockSpec((tm, width), lambda i: (i, 0)), _acc_spec(DH), tab, tab, tab] + (views if extra else []),
        out_specs=[pl.BlockSpec((tm, wout), lambda i: (i, 0)), _acc_spec(DH)],
        out_shape=[jax.ShapeDtypeStruct((S, wout), bf16), jax.ShapeDtypeStruct((1, DH), f32)],
        scratch_shapes=[pltpu.VMEM((hpg, tm, DH), f32), pltpu.VMEM((ow // DH, tm, DH), f32)],
        compiler_params=_cp((ARB,)))(*dparts, raw, g, *tabs, *extra)


def _dot_nt(a, b):
    return lax.dot_general(a, b, (((1,), (1,)), ((), ())), preferred_element_type=f32)


def _dot_tn(a, b):
    return lax.dot_general(a, b, (((0,), (0,)), ((), ())), preferred_element_type=f32)


def _band_masks():
    qi = lax.broadcasted_iota(jnp.int32, (BLK, BLK), 0)
    ki = lax.broadcasted_iota(jnp.int32, (BLK, BLK), 1)
    return ki <= qi, ki >= qi


def _attn_fwd(qv, kview, vview, vbase, r, name):
    sr = qv.shape[0]
    ow = qv.shape[1] // r
    hpg = ow // DH
    nb = sr // BLK
    scale = 1.0 / math.sqrt(DH)

    def body(q_ref, kc_ref, kp_ref, vc_ref, vp_ref, o_ref, l_ref):
        n = pl.program_id(1)
        m_cur, m_prev = _band_masks()
        m_prev = m_prev & (n > 0)
        hs = [slice(h * DH, (h + 1) * DH) for h in range(hpg)]
        s_c = [jnp.where(m_cur, _dot_nt(q_ref[:, s], kc_ref[:, s]) * scale, NEG) for s in hs]
        s_p = [jnp.where(m_prev, _dot_nt(q_ref[:, s], kp_ref[:, s]) * scale, NEG) for s in hs]
        mx = [jnp.maximum(jnp.max(a, axis=-1, keepdims=True), jnp.max(b, axis=-1, keepdims=True)) for a, b in zip(s_c, s_p)]
        p_c = [jnp.exp(a - m) for a, m in zip(s_c, mx)]
        p_p = [jnp.exp(a - m) for a, m in zip(s_p, mx)]
        den = [jnp.sum(a, axis=-1, keepdims=True) + jnp.sum(b, axis=-1, keepdims=True) for a, b in zip(p_c, p_p)]
        for h, s in enumerate(hs):
            o = jnp.dot(p_c[h].astype(bf16), vc_ref[:, s], preferred_element_type=f32)
            o = o + jnp.dot(p_p[h].astype(bf16), vp_ref[:, s], preferred_element_type=f32)
            o_ref[:, s] = (o / den[h]).astype(o_ref.dtype)
            l_ref[:, s] = jnp.broadcast_to(mx[h] + jnp.log(den[h]), (BLK, DH))

    cur = lambda j, n: (n, j)
    prev = lambda j, n: (jnp.maximum(n - 1, 0), j)
    vcur = lambda j, n: (n, vbase + j)
    vprev = lambda j, n: (jnp.maximum(n - 1, 0), vbase + j)
    blk = lambda f: pl.BlockSpec((BLK, ow), f)
    return pl.pallas_call(
        body, name=name, grid=(r, nb), in_specs=[blk(cur), blk(cur), blk(prev), blk(vcur), blk(vprev)],
        out_specs=[blk(cur), blk(cur)],
        out_shape=[jax.ShapeDtypeStruct((sr, r * ow), bf16), jax.ShapeDtypeStruct((sr, r * ow), f32)],
        compiler_params=_cp((PAR, PAR)))(qv, kview, kview, vview, vview)


def _attn_bwd_q(qv, kview, vview, vbase, do_g, lse, corr, r, name, comms=()):
    sr = qv.shape[0]
    ow = qv.shape[1] // r
    hpg = ow // DH
    nb = sr // BLK
    scale = 1.0 / math.sqrt(DH)

    def body(q_ref, kc_ref, kp_ref, vc_ref, vp_ref, do_ref, l_ref, c_ref, dq_ref):
        n = pl.program_id(1)
        m_cur, m_prev = _band_masks()
        m_prev = m_prev & (n > 0)
        hs = [slice(h * DH, (h + 1) * DH) for h in range(hpg)]
        ls = [slice(h * DH, h * DH + BLK) for h in range(hpg)]
        sides = ((kc_ref, vc_ref, m_cur), (kp_ref, vp_ref, m_prev))
        sc = [[jnp.where(msk, _dot_nt(q_ref[:, s], k_ref[:, s]) * scale, NEG) for s in hs] for k_ref, _, msk in sides]
        dp = [[_dot_nt(do_ref[:, s], v_ref[:, s]) for s in hs] for _, v_ref, _ in sides]
        ds = [[(jnp.exp(sc[i][h] - l_ref[:, ls[h]]) * (dp[i][h] + c_ref[:, ls[h]])).astype(bf16) for h in range(hpg)]
              for i in range(2)]
        for h, s in enumerate(hs):
            dq = jnp.dot(ds[0][h], kc_ref[:, s], preferred_element_type=f32)
            dq = dq + jnp.dot(ds[1][h], kp_ref[:, s], preferred_element_type=f32)
            dq_ref[:, s] = (dq * scale).astype(dq_ref.dtype)

    cur = lambda j, n: (n, j)
    prev = lambda j, n: (jnp.maximum(n - 1, 0), j)
    vcur = lambda j, n: (n, vbase + j)
    vprev = lambda j, n: (jnp.maximum(n - 1, 0), vbase + j)
    blk = lambda f: pl.BlockSpec((BLK, ow), f)
    return _pcall(
        body, name=name, grid=(r, nb),
        in_specs=[blk(cur), blk(cur), blk(prev), blk(vcur), blk(vprev), blk(cur), blk(cur), blk(cur)],
        out_specs=blk(cur), out_shape=jax.ShapeDtypeStruct((sr, r * ow), bf16), sem=(PAR, PAR), comms=comms,
        args=[qv, kview, kview, vview, vview, do_g, lse, corr])


def _attn_bwd_kv(qv, kview, vview, vbase, do_g, lse, corr, r, name):
    sr = qv.shape[0]
    ow = qv.shape[1] // r
    hpg = ow // DH
    nb = sr // BLK
    scale = 1.0 / math.sqrt(DH)

    def body(k_ref, v_ref, qc_ref, qn_ref, doc_ref, don_ref, lc_ref, ln_ref, cc_ref, cn_ref, dk_ref, dv_ref):
        n = pl.program_id(1)
        m_cur, m_prev = _band_masks()
        m_next = m_prev & (n < nb - 1)
        hs = [slice(h * DH, (h + 1) * DH) for h in range(hpg)]
        ls = [slice(h * DH, h * DH + BLK) for h in range(hpg)]
        sides = ((qc_ref, doc_ref, lc_ref, cc_ref, m_cur), (qn_ref, don_ref, ln_ref, cn_ref, m_next))
        sc = [[jnp.where(msk, _dot_nt(q_ref[:, s], k_ref[:, s]) * scale, NEG) for s in hs] for q_ref, _, _, _, msk in sides]
        dp = [[_dot_nt(do_ref[:, s], v_ref[:, s]) for s in hs] for _, do_ref, _, _, _ in sides]
        p = [[jnp.exp(sc[i][h] - sides[i][2][:, ls[h]]) for h in range(hpg)] for i in range(2)]
        ds = [[(p[i][h] * (dp[i][h] + sides[i][3][:, ls[h]])).astype(bf16) for h in range(hpg)] for i in range(2)]
        for h, s in enumerate(hs):
            dv = _dot_tn(p[0][h].astype(bf16), doc_ref[:, s]) + _dot_tn(p[1][h].astype(bf16), don_ref[:, s])
            dk = _dot_tn(ds[0][h], qc_ref[:, s]) + _dot_tn(ds[1][h], qn_ref[:, s])
            dk_ref[:, s] = (dk * scale).astype(dk_ref.dtype)
            dv_ref[:, s] = dv.astype(dv_ref.dtype)

    cur = lambda j, n: (n, j)
    nxt = lambda j, n: (jnp.minimum(n + 1, nb - 1), j)
    vcur = lambda j, n: (n, vbase + j)
    blk = lambda f: pl.BlockSpec((BLK, ow), f)
    shp = jax.ShapeDtypeStruct((sr, r * ow), bf16)
    return pl.pallas_call(
        body, name=name, grid=(r, nb),
        in_specs=[blk(cur), blk(vcur), blk(cur), blk(nxt), blk(cur), blk(nxt), blk(cur), blk(nxt), blk(cur), blk(nxt)],
        out_specs=[blk(cur), blk(cur)], out_shape=[shp, shp],
        compiler_params=_cp((PAR, PAR)))(kview, vview, qv, qv, do_g, do_g, lse, lse, corr, corr)


def _mix_weights(l_refs):
    ls = [l[...] for l in l_refs]
    mx = functools.reduce(jnp.maximum, ls)
    es = [jnp.exp(l - mx) for l in ls]
    den = functools.reduce(lambda a, b: a + b, es)
    return [e / den for e in es]


def _from_view(ref, scr, r, ow, tm):
    if r == 1:
        return ref[...].astype(f32)
    for c in range(ow // DH):
        for j in range(r):
            scr[c, pl.ds(j, tm // r, stride=r), :] = ref[:, j * ow + c * DH:j * ow + (c + 1) * DH].astype(f32)
    return jnp.concatenate([scr[c] for c in range(ow // DH)], axis=1)


def _to_view(val, ref, scr, r, ow, tm):
    if r == 1:
        ref[...] = val.astype(ref.dtype)
        return
    for c in range(ow // DH):
        scr[c] = val[:, c * DH:(c + 1) * DH]
        for j in range(r):
            ref[:, j * ow + c * DH:j * ow + (c + 1) * DH] = scr[c, pl.ds(j, tm // r, stride=r), :].astype(ref.dtype)


def _view_specs(tm, ow):
    return [pl.BlockSpec((tm // r, r * ow), lambda i: (i, 0)) for r in DILS]


def _combine_fwd(os_, lses, name):
    ow = os_[0].shape[1] // DILS[0]
    S = os_[0].shape[0] * DILS[0]
    tm = _tile(S, 256, 16 * max(DILS))

    def body(*refs):
        o_refs, l_refs, out_ref = refs[:NG], refs[NG:2 * NG], refs[2 * NG]
        scr = refs[2 * NG + 1:]
        ov = [_from_view(o_refs[gi], scr[2 * gi], DILS[gi], ow, tm) for gi in range(NG)]
        lv = [_from_view(l_refs[gi], scr[2 * gi + 1], DILS[gi], ow, tm) for gi in range(NG)]
        al = _mix_weights(lv)
        acc = al[0] * ov[0]
        for gi in range(1, NG):
            acc = acc + al[gi] * ov[gi]
        out_ref[...] = acc.astype(out_ref.dtype)

    views = _view_specs(tm, ow)
    return pl.pallas_call(body, name=name, grid=(S // tm,), in_specs=views + views,
                          out_specs=pl.BlockSpec((tm, ow), lambda i: (i, 0)), out_shape=jax.ShapeDtypeStruct((S, ow), bf16),
                          scratch_shapes=[pltpu.VMEM((ow // DH, tm, DH), f32)] * (2 * NG),
                          compiler_params=_cp((PAR,), VMEM_BIG))(*os_, *lses)


def _combine_bwd(do, os_, lses, name, comms=()):
    S, ow = do.shape
    hpg = ow // DH
    tm = _tile(S, 256, 16 * max(DILS))

    def body(*refs):
        do_ref = refs[0]
        o_refs, l_refs = refs[1:1 + NG], refs[1 + NG:1 + 2 * NG]
        dog_refs, c_refs = refs[1 + 2 * NG:1 + 3 * NG], refs[1 + 3 * NG:1 + 4 * NG]
        scr = refs[1 + 4 * NG:]
        ov = [_from_view(o_refs[gi], scr[2 * gi], DILS[gi], ow, tm) for gi in range(NG)]
        lv = [_from_view(l_refs[gi], scr[2 * gi + 1], DILS[gi], ow, tm) for gi in range(NG)]
        al = _mix_weights(lv)
        dov = do_ref[...]
        o = al[0] * ov[0]
        for gi in range(1, NG):
            o = o + al[gi] * ov[gi]
        prod = dov * o
        t = jnp.concatenate(
            [jnp.broadcast_to(jnp.sum(prod[:, h * DH:(h + 1) * DH], axis=-1, keepdims=True), (tm, DH)) for h in range(hpg)],
            axis=1)
        for gi in range(NG):
            _to_view(al[gi] * dov, dog_refs[gi], scr[2 * NG], DILS[gi], ow, tm)
            _to_view(-(al[gi] * t), c_refs[gi], scr[2 * NG], DILS[gi], ow, tm)

    views = _view_specs(tm, ow)
    vshape = lambda dt: [jax.ShapeDtypeStruct((S // r, r * ow), dt) for r in DILS]
    return _pcall(
        body, name=name, grid=(S // tm,), in_specs=[pl.BlockSpec((tm, ow), lambda i: (i, 0))] + views + views,
        out_specs=views + views, out_shape=vshape(bf16) + vshape(f32),
        args=[do, *os_, *lses], scratch_shapes=[pltpu.VMEM((ow // DH, tm, DH), f32)] * (2 * NG + 1), sem=(PAR,), vmem=VMEM_BIG,
        comms=comms)


def _pad_rows(w, rows):
    return jnp.concatenate([w, jnp.zeros((rows - w.shape[0], w.shape[1]), w.dtype)], axis=0)


def kernel(x, c, positions, mod_w, mod_b, norm_mix_g, norm_ffn_g, conv_pw1_w, conv_pw1_b, conv_dw_w, conv_dw_b, conv_ln_g, conv_ln_b, conv_pw2_w, conv_pw2_b, kv_mod_w, kv_mod_b, kv_norm_g, w_kv, k_norm_g, w_q, q_norm_g, w_o, ffn_up_w, ffn_dw_w, ffn_dw_b, ffn_down_w, loss_target, m_mod_w, m_mod_b, m_norm_mix_g, m_norm_ffn_g, m_conv_pw1_w, m_conv_pw1_b, m_conv_dw_w, m_conv_dw_b, m_conv_ln_g, m_conv_ln_b, m_conv_pw2_w, m_conv_pw2_b, m_kv_mod_w, m_kv_mod_b, m_kv_norm_g, m_w_kv, m_k_norm_g, m_w_q, m_q_norm_g, m_w_o, m_ffn_up_w, m_ffn_dw_w, m_ffn_dw_b, m_ffn_down_w, v_mod_w, v_mod_b, v_norm_mix_g, v_norm_ffn_g, v_conv_pw1_w, v_conv_pw1_b, v_conv_dw_w, v_conv_dw_b, v_conv_ln_g, v_conv_ln_b, v_conv_pw2_w, v_conv_pw2_b, v_kv_mod_w, v_kv_mod_b, v_kv_norm_g, v_w_kv, v_k_norm_g, v_w_q, v_q_norm_g, v_w_o, v_ffn_up_w, v_ffn_dw_w, v_ffn_dw_b, v_ffn_down_w):
    S, Dm = x.shape[1], x.shape[2]
    F = ffn_dw_b.shape[1]
    QW = NG * HPG * DH
    OW = HPG * DH
    mx, my, mc = _me()
    me = 4 * mx + 2 * my + mc
    core = jnp.reshape(mc, (1,)).astype(jnp.int32)
    chip = jnp.reshape(2 * mx + my, (1,)).astype(jnp.int32)
    x0 = x.reshape(S, Dm)
    target = loss_target.reshape(S, Dm)

    c_all = _ag_small(c, "ag_c").reshape(NDEV, Dm)
    n_mod = mod_w.shape[2]
    n_kvm = kv_mod_w.shape[1]
    b0 = lax.dynamic_slice(mod_b, (0, me * n_mod), (1, n_mod))
    b1 = lax.dynamic_slice(mod_b, (1, me * n_mod), (1, n_mod))
    bk = lax.dynamic_slice(kv_mod_b.reshape(1, -1), (0, me * n_kvm), (1, n_kvm))
    m_part = jnp.concatenate([_modproj(c_all, mod_w[0], b0, "modproj0"), _modproj(c_all, mod_w[1], b1, "modproj1"),
                              _modproj(c_all, kv_mod_w, bk, "modproj_kv")], axis=1)
    m_all = _ag_small(m_part, "ag_mod")
    m_mine = lax.dynamic_index_in_dim(m_all, me, axis=1, keepdims=False)
    mod0 = m_mine[:, :n_mod].reshape(6, Dm)
    mod1 = m_mine[:, n_mod:2 * n_mod].reshape(6, Dm)
    modkv = m_mine[:, 2 * n_mod:].reshape(2, Dm)
    row = lambda a, i: a[i:i + 1]

    as3 = lambda w: w if w.ndim == 3 else w[None]
    sh16 = lambda w: as3(w).astype(bf16)
    W_pw1 = _ag_big(sh16(conv_pw1_w), 2, "ag_pw1")
    ag_pw2 = _comm_allgather(sh16(conv_pw2_w), 1)
    ag_up = [_comm_allgather(sh16(ffn_up_w[l]), 2) for l in range(2)]
    ag_down = [_comm_allgather(sh16(ffn_down_w[l]), 1) for l in range(2)]
    ag_kv = _comm_allgather(sh16(w_kv), 2)
    ag_q = _comm_allgather(sh16(w_q), 2)
    ag_o = _comm_allgather(sh16(w_o), 2)

    sp_flat = jnp.concatenate([conv_pw1_b.reshape(-1), conv_dw_b.reshape(-1), conv_ln_g.reshape(-1), conv_ln_b.reshape(-1),
                               conv_pw2_b.reshape(-1), conv_dw_w.reshape(-1), ffn_dw_w.reshape(-1)])
    sp_rows = -(-sp_flat.shape[0] // 1024) * 8
    sp_flat = jnp.concatenate([sp_flat, jnp.zeros((sp_rows * 128 - sp_flat.shape[0],), f32)]).reshape(sp_rows, 128)
    n1, nd = conv_pw1_b.shape[1], conv_dw_b.shape[1]
    nfw = ffn_dw_w.shape[2]
    sp = _ag_small(sp_flat, "ag_small_params").reshape(NDEV, -1)
    off = 0
    pw1_b = sp[:, off:off + n1].reshape(1, -1); off += n1
    dw_b = sp[:, off:off + nd].reshape(1, -1); off += nd
    ln_g = sp[:, off:off + nd].reshape(1, -1); off += nd
    ln_b = sp[:, off:off + nd].reshape(1, -1); off += nd
    pw2_b = sp[:, off:off + nd].reshape(1, -1); off += nd
    dw_w = jnp.transpose(sp[:, off:off + CONV_K * nd].reshape(NDEV, CONV_K, nd), (1, 0, 2)).reshape(CONV_K, -1); off += CONV_K * nd
    fdw_w = jnp.transpose(sp[:, off:off + 2 * FFN_K * nfw].reshape(NDEV, 2, FFN_K, nfw), (1, 2, 0, 3)).reshape(2, FFN_K, -1)
    dw_w32 = _pad_rows(dw_w, 32)

    tabs = _rope_tables(positions.reshape(S, 1), "rope_tables")

    def with_comms(res, comms):
        return res if comms else (res, [])

    def rs_d2d(dwb):
        return [_comm_rs_sibling(dwb)]

    def rs_add(dwb, couts, tag):
        return _chip_partial(dwb, couts[0][0], core, f"rs_add_{tag}")

    def rs_ici(part):
        return [_comm_rs_chips(part)]

    def ffn_forward(xin, l, modv, w_up, w_down, up_comms, gate_comms, down_comms):
        h2 = _mod_fwd(xin, row(norm_ffn_g, l), row(modv, 3), row(modv, 4), f"ffn{l}_mod")
        u2, c_up = with_comms(_mm_nn(h2, w_up, 0, name=f"ffn{l}_up", comms=up_comms), up_comms)
        if w_down is None:
            w_down, c_up = c_up[0][0], c_up[1:]
        fw8 = _pad_rows(fdw_w[l], 8)
        act, c_gate = with_comms(_ffn_gate_fwd(u2, fw8, row(ffn_dw_b, l), f"ffn{l}_gate", comms=gate_comms), gate_comms)
        (xout, f), c_down = with_comms(
            _mm_nn(act, w_down, 0, name=f"ffn{l}_down", res=xin, gate=row(modv, 5), tk=F, tn=512, comms=down_comms), down_comms)
        return xout, (h2, u2, act, f, fw8, w_up, w_down), c_up, c_gate, c_down

    def ffn_backward(dx, xin, l, modv, saved, dact_comms):
        h2, u2, act, f, fw8, w_up, w_down = saved
        df, dgate, _ = _gate_bwd(dx, f, row(modv, 5), f"ffn{l}_gate_bwd")
        dact, c_dact = with_comms(_mm_nt(df, w_down, 0, name=f"ffn{l}_dact", out_dtype=bf16, tko=512, tn=Dm, comms=dact_comms), dact_comms)
        d_down = _mm_tn(act, df, name=f"ffn{l}_ddown", col_sharded=False)
        (du2, d_fw, d_fb), c1 = _ffn_gate_bwd(u2, dact, fw8, row(ffn_dw_b, l), f"ffn{l}_gatebwd", comms=rs_d2d(d_down))
        part_down = rs_add(d_down, c1, f"down{l}")
        dh2, c2 = _mm_nt(du2, w_up, 0, name=f"ffn{l}_dh", out_dtype=f32, tko=1024, tn=F // 2, comms=rs_ici(part_down))
        d_up = _mm_tn(h2, du2, name=f"ffn{l}_dup", col_sharded=True)
        (dxin, dsh, dsc, dg), c3 = _mod_bwd(dh2, xin, dx, row(norm_ffn_g, l), row(modv, 4), f"ffn{l}_mod_bwd", comms=rs_d2d(d_up))
        part_up = rs_add(d_up, c3, f"up{l}")
        grads = dict(d_fw=d_fw[:FFN_K], d_fb=d_fb, dsh=dsh, dsc=dsc, dgate=dgate, dg=dg,
                     down=(part_down, c2[0][0]), part_up=part_up)
        return dxin, grads, c_dact

    h0 = _mod_fwd(x0, row(norm_mix_g, 0), row(mod0, 0), row(mod0, 1), "l0_mod")
    u0, c = _mm_nn(h0, W_pw1, 0, name="l0_pw1", bias=pw1_b, comms=[ag_pw2])
    W_pw2 = c[0][0]
    s0, c = _conv_fwd(u0, dw_w32, dw_b, ln_g, ln_b, "l0_conv", comms=[ag_up[0]])
    W_up0 = c[0][0]
    x1, f0 = _mm_nn(s0, W_pw2, 0, name="l0_pw2", bias=pw2_b, res=x0, gate=row(mod0, 2))
    x2, ffn0_saved, _, c_gate, c_down = ffn_forward(x1, 0, mod0, W_up0, None, [ag_down[0]], [ag_kv], [ag_q, ag_o])
    W_kv, W_q, W_o = c_gate[0][0], c_down[0][0], c_down[1][0]

    hkv = _mod_fwd(x2, kv_norm_g.reshape(1, -1), row(modkv, 0), row(modkv, 1), "kv_mod")
    kvraw, c = _mm_nn(hkv, W_kv, 0, name="kv_proj", comms=[ag_up[1]])
    W_up1 = c[0][0]
    kg = k_norm_g.reshape(1, -1)
    k_gv, v_dil = _qk_fwd(kvraw, kg, tabs, QW, True, "k_norm_rope")
    dilated = [gi for gi, r in enumerate(DILS) if r > 1]
    v_of = {gi: (kvraw, NG + gi) for gi, r in enumerate(DILS) if r == 1}
    v_of.update({gi: (v_dil[i], 0) for i, gi in enumerate(dilated)})
    h1 = _mod_fwd(x2, row(norm_mix_g, 1), row(mod1, 0), row(mod1, 1), "l1_mod")
    qraw, c_q = _mm_nn(h1, W_q, 0, name="q_proj", comms=[ag_down[1]])
    qg = q_norm_g.reshape(1, -1)
    q_gv, _ = _qk_fwd(qraw, qg, tabs, QW, False, "q_norm_rope")
    o_gs, lses = [], []
    for gi, r in enumerate(DILS):
        o_g, lse_g = _attn_fwd(q_gv[gi], k_gv[gi], *v_of[gi], r, f"attn_fwd{gi}")
        o_gs.append(o_g)
        lses.append(lse_g)
    o_mix = _combine_fwd(o_gs, lses, "attn_mix")
    x3, f1 = _mm_nn(o_mix, W_o, 0, name="o_proj", res=x2, gate=row(mod1, 2))
    x4, ffn1_saved, _, _, _ = ffn_forward(x3, 1, mod1, W_up1, c_q[0][0], (), (), ())

    dx4, loss_blk = _loss_grad(x4, target, "loss")
    loss = lax.psum(loss_blk[0, 0], ("x", "y", "c"))

    red = {}
    dx3, gf1, _ = ffn_backward(dx4, x3, 1, mod1, ffn1_saved, ())
    dy1, dgate_m1, _ = _gate_bwd(dx3, f1, row(mod1, 2), "l1_gate_bwd")
    do = _mm_nt(dy1, W_o, 0, name="o_proj_dx", out_dtype=f32, tko=1024, tn=Dm)
    d_wo = _mm_tn(o_mix, dy1, name="o_proj_dw", col_sharded=True)
    outs, c = _combine_bwd(do, o_gs, lses, "attn_mix_bwd", comms=rs_d2d(d_wo))
    part_wo = rs_add(d_wo, c, "wo")
    do_gs, corrs = outs[:NG], outs[NG:]
    dq_gs, dk_gs, dv_gs = [], [], []
    for gi, r in enumerate(DILS):
        cm = rs_ici(part_wo) if gi == 0 else ()
        dq_g, c = with_comms(_attn_bwd_q(q_gv[gi], k_gv[gi], *v_of[gi], do_gs[gi], lses[gi], corrs[gi], r, f"attn_bwd_q{gi}",
                                         comms=cm), cm)
        if gi == 0:
            red["w_o"] = (part_wo, c[0][0])
        dq_gs.append(dq_g)
        dk_g, dv_g = _attn_bwd_kv(q_gv[gi], k_gv[gi], *v_of[gi], do_gs[gi], lses[gi], corrs[gi], r, f"attn_bwd_kv{gi}")
        dk_gs.append(dk_g)
        dv_gs.append(dv_g)
    dqraw, d_qg = _qk_bwd(dq_gs, qraw, qg, tabs, QW, (), "q_norm_rope_bwd")
    dkvraw, d_kg = _qk_bwd(dk_gs, kvraw, kg, tabs, QW, tuple(dv_gs), "k_norm_rope_bwd")
    dh1 = _mm_nt(dqraw, W_q, 0, name="q_proj_dx", out_dtype=f32, tko=1024, tn=QW)
    d_wq = _mm_tn(h1, dqraw, name="q_proj_dw", col_sharded=True)
    dhkv, c = _mm_nt(dkvraw, W_kv, 0, name="kv_proj_dx", out_dtype=f32, tko=512, tn=2 * QW, comms=rs_d2d(d_wq))
    part_wq = rs_add(d_wq, c, "wq")
    d_wkv, c = _mm_tn(hkv, dkvraw, name="kv_proj_dw", col_sharded=True, comms=rs_ici(gf1["part_up"]))
    red["ffn_up_w1"] = (gf1["part_up"], c[0][0])
    (dx2a, dsh_m1, dsc_m1, dg_mix1), c = _mod_bwd(dh1, x2, dx3, row(norm_mix_g, 1), row(mod1, 1), "l1_mod_bwd",
                                                  comms=rs_ici(part_wq))
    red["w_q"] = (part_wq, c[0][0])
    (dx2, dsh_kv, dsc_kv, dg_kvn), c = _mod_bwd(dhkv, x2, dx2a, kv_norm_g.reshape(1, -1), row(modkv, 1), "kv_mod_bwd",
                                                comms=rs_d2d(d_wkv))
    part_wkv = rs_add(d_wkv, c, "wkv")

    dx1, gf0, c = ffn_backward(dx2, x1, 0, mod0, ffn0_saved, rs_ici(part_wkv))
    red["w_kv"] = (part_wkv, c[0][0])
    dy0, dgate_m0, d_pw2b = _gate_bwd(dx1, f0, row(mod0, 2), "l0_gate_bwd")
    ds0 = _mm_nt(dy0, W_pw2, 0, name="l0_pw2_dx", out_dtype=bf16, tko=1024, tn=Dm)
    d_pw2 = _mm_tn(s0, dy0, name="l0_pw2_dw", col_sharded=False)
    (dcv, d_lng, d_lnb, d_dwb, d_dww), c = _conv_bwd1(u0, ds0, dw_w32, dw_b, ln_g, ln_b, "l0_conv_bwd1",
                                                      comms=rs_ici(gf0["part_up"]))
    red["ffn_up_w0"] = (gf0["part_up"], c[0][0])
    (du0, d_pw1b), c = _conv_bwd2(dcv, u0, dw_w32, "l0_conv_bwd2", comms=rs_d2d(d_pw2))
    part_pw2 = rs_add(d_pw2, c, "pw2")
    d_pw1 = _mm_tn(h0, du0, name="l0_pw1_dw", col_sharded=True)
    dh0, c = _mm_nt(du0, W_pw1, 0, name="l0_pw1_dx", out_dtype=f32, tko=1024, tn=2 * Dm, comms=rs_ici(part_pw2) + rs_d2d(d_pw1))
    red["conv_pw2_w"] = (part_pw2, c[0][0])
    part_pw1 = rs_add(d_pw1, c[1:], "pw1")
    (grad_x, dsh_m0, dsc_m0, dg_mix0), c = _mod_bwd(dh0, x0, dx1, row(norm_mix_g, 0), row(mod0, 1), "l0_mod_bwd",
                                                    comms=rs_ici(part_pw1))
    red["conv_pw1_w"] = (part_pw1, c[0][0])
    red["ffn_down_w0"], red["ffn_down_w1"] = gf0["down"], gf1["down"]

    dm0 = [dsh_m0, dsc_m0, dgate_m0, gf0["dsh"], gf0["dsc"], gf0["dgate"]]
    dm1 = [dsh_m1, dsc_m1, dgate_m1, gf1["dsh"], gf1["dsc"], gf1["dgate"]]
    pieces = dm0 + dm1 + [dsh_kv, dsc_kv,
                          dg_mix0, dg_mix1, gf0["dg"], gf1["dg"], dg_kvn, d_kg, d_qg, gf0["d_fb"], gf1["d_fb"],
                          d_pw1b, d_dww[:CONV_K], d_dwb, d_lng, d_lnb, d_pw2b, gf0["d_fw"], gf1["d_fw"]]
    flat = jnp.concatenate([p.reshape(-1) for p in pieces])
    n_flat = flat.shape[0]
    n_rows = -(-n_flat // 1024) * 8
    flat = jnp.concatenate([flat, jnp.zeros((n_rows * 128 - n_flat,), f32)]).reshape(n_rows, 128)
    g_all = _ag_small(flat, "ag_small_grads")
    g_sum = _sum8(g_all, "sum_small_grads").reshape(-1)
    n_dm = 2 * 6 * Dm + 2 * Dm
    dm_all = g_all.reshape(NDEV, -1)[:, :n_dm]

    take_pos = [0]

    def take(shape):
        n = int(np.prod(shape))
        out = g_sum[take_pos[0]:take_pos[0] + n].reshape(shape)
        take_pos[0] += n
        return out

    g_mod_b = take((2, 6 * Dm))
    g_kv_mod_b = take((2 * Dm,))
    g_norm_mix0, g_norm_mix1 = take((Dm,)), take((Dm,))
    g_norm_ffn0, g_norm_ffn1 = take((Dm,)), take((Dm,))
    g_kv_norm = take((Dm,))
    g_k_norm = take((DH,))
    g_q_norm = take((1, DH))
    g_ffn_dw_b = take((2, F))
    shard = lambda full, n, axis: lax.dynamic_slice_in_dim(full, me * n, n, axis)
    g_pw1_b = shard(take((1, 2 * Dm)), n1, 1)
    g_dw_w = shard(take((1, CONV_K, Dm)), nd, 2)
    g_dw_b = shard(take((1, Dm)), nd, 1)
    g_ln_g = shard(take((1, Dm)), nd, 1)
    g_ln_b = shard(take((1, Dm)), nd, 1)
    g_pw2_b = shard(take((1, Dm)), nd, 1)
    g_ffn_dw_w = shard(jnp.stack([take((FFN_K, F)), take((FFN_K, F))]), nfw, 2)
    g_norm_mix = jnp.stack([g_norm_mix0, g_norm_mix1])
    g_norm_ffn = jnp.stack([g_norm_ffn0, g_norm_ffn1])

    small = [("mod_b", mod_b, m_mod_b, v_mod_b, g_mod_b), ("norm_mix_g", norm_mix_g, m_norm_mix_g, v_norm_mix_g, g_norm_mix),
             ("norm_ffn_g", norm_ffn_g, m_norm_ffn_g, v_norm_ffn_g, g_norm_ffn),
             ("conv_pw1_b", conv_pw1_b, m_conv_pw1_b, v_conv_pw1_b, g_pw1_b),
             ("conv_dw_w", conv_dw_w, m_conv_dw_w, v_conv_dw_w, g_dw_w), ("conv_dw_b", conv_dw_b, m_conv_dw_b, v_conv_dw_b, g_dw_b),
             ("conv_ln_g", conv_ln_g, m_conv_ln_g, v_conv_ln_g, g_ln_g), ("conv_ln_b", conv_ln_b, m_conv_ln_b, v_conv_ln_b, g_ln_b),
             ("conv_pw2_b", conv_pw2_b, m_conv_pw2_b, v_conv_pw2_b, g_pw2_b),
             ("kv_mod_b", kv_mod_b, m_kv_mod_b, v_kv_mod_b, g_kv_mod_b), ("kv_norm_g", kv_norm_g, m_kv_norm_g, v_kv_norm_g, g_kv_norm),
             ("k_norm_g", k_norm_g, m_k_norm_g, v_k_norm_g, g_k_norm), ("q_norm_g", q_norm_g, m_q_norm_g, v_q_norm_g, g_q_norm),
             ("ffn_dw_w", ffn_dw_w, m_ffn_dw_w, v_ffn_dw_w, g_ffn_dw_w), ("ffn_dw_b", ffn_dw_b, m_ffn_dw_b, v_ffn_dw_b, g_ffn_dw_b)]
    n_small = sum(int(np.prod(s[1].shape)) for s in small)
    rows_small = -(-n_small // 1024) * 8

    def pack(idx):
        fl = jnp.concatenate([s[idx].reshape(-1) for s in small])
        return jnp.concatenate([fl, jnp.ones((rows_small * 128 - n_small,), f32)]).reshape(rows_small, 128)

    sd, sm, sv = _adamw_plain(pack(1), pack(2), pack(3), pack(4), "adamw_small")
    res = {}
    pos = 0
    for name, w, _, _, g in small:
        n = int(np.prod(w.shape))
        cut = lambda a: a.reshape(-1)[pos:pos + n].reshape(w.shape)
        res[name] = (g.reshape(w.shape), cut(sd), cut(sm), cut(sv))
        pos += n

    c_all_t = jnp.transpose(c_all)

    def mod_update(w2d, m2d, v2d, dm_cols, tag):
        g = _modgrad(c_all_t, dm_cols, f"modgrad_{tag}")
        d, m2, v2 = _adamw_plain(w2d, m2d, v2d, g, f"adamw_{tag}")
        return g, d, m2, v2

    mw = []
    for l in range(2):
        cols = lax.dynamic_slice_in_dim(dm_all[:, l * 6 * Dm:(l + 1) * 6 * Dm], me * n_mod, n_mod, 1)
        mw.append(mod_update(mod_w[l], m_mod_w[l], v_mod_w[l], cols, f"mod_w{l}"))
    res["mod_w"] = tuple(jnp.stack([mw[0][i], mw[1][i]]) for i in range(4))
    cols = lax.dynamic_slice_in_dim(dm_all[:, 12 * Dm:], me * n_kvm, n_kvm, 1)
    res["kv_mod_w"] = mod_update(kv_mod_w, m_kv_mod_w, v_kv_mod_w, cols, "kv_mod_w")

    def mine(part):
        return lax.dynamic_index_in_dim(part, chip[0], 0, keepdims=False)

    def big(key, w, m, v, l, prev, tag, comms=()):
        part, r2 = red[key]
        return _adamw_reduced(as3(w), as3(m), as3(v), mine(part), r2, l, prev, f"adamw_{tag}", comms=comms)

    up1 = big("ffn_up_w1", ffn_up_w, m_ffn_up_w, v_ffn_up_w, 1, None, "up1")
    res["ffn_up_w"] = tuple(big("ffn_up_w0", ffn_up_w, m_ffn_up_w, v_ffn_up_w, 0, up1, "up0"))
    down1 = big("ffn_down_w1", ffn_down_w, m_ffn_down_w, v_ffn_down_w, 1, None, "down1")
    res["ffn_down_w"] = tuple(big("ffn_down_w0", ffn_down_w, m_ffn_down_w, v_ffn_down_w, 0, down1, "down0"))
    for key, w, m, v in (("conv_pw1_w", conv_pw1_w, m_conv_pw1_w, v_conv_pw1_w), ("conv_pw2_w", conv_pw2_w, m_conv_pw2_w, v_conv_pw2_w),
                         ("w_kv", w_kv, m_w_kv, v_w_kv), ("w_q", w_q, m_w_q, v_w_q), ("w_o", w_o, m_w_o, v_w_o)):
        res[key] = tuple(o.reshape(w.shape) for o in big(key, w, m, v, 0, None, key))

    order = ["mod_w", "mod_b", "norm_mix_g", "norm_ffn_g", "conv_pw1_w", "conv_pw1_b", "conv_dw_w", "conv_dw_b", "conv_ln_g",
             "conv_ln_b", "conv_pw2_w", "conv_pw2_b", "kv_mod_w", "kv_mod_b", "kv_norm_g", "w_kv", "k_norm_g", "w_q", "q_norm_g",
             "w_o", "ffn_up_w", "ffn_dw_w", "ffn_dw_b", "ffn_down_w"]
    out = [loss, grad_x.reshape(x.shape)]
    for i in range(4):
        out += [res[n][i] for n in order]
    return tuple(out)
```

```python
import functools
import math

import numpy as np
import jax
import jax.numpy as jnp
from jax import lax
from jax.experimental import pallas as pl
from jax.experimental.pallas import tpu as pltpu

f32 = jnp.float32
bf16 = jnp.bfloat16

D = 2048
SEQ = 8192
FF = 5632
CONV_K = 31
FFN_K = 3
HPG = 8
DH = 128
NG = 3
DILS = (1, 4, 16)
BLK = 128
ROT = 32
THETA = 500000.0
EPS = 1e-6
NEG = -1e30
NDEV = 8
HALO = 32
FHALO = 16

LR, B1, B2, AEPS, WD, STEP = 0.001, 0.9, 0.999, 1e-08, 0.01, 10

VMEM_BIG = 56 * 1024 * 1024

ARB = "arbitrary"
PAR = "parallel"
MESH = pl.DeviceIdType.MESH


def _cp(sem, vmem=None):
    return pltpu.CompilerParams(dimension_semantics=sem, vmem_limit_bytes=vmem)


def _tile(n, pref, mult=128):
    if n <= pref:
        return n
    t = (pref // mult) * mult
    while t >= mult:
        if n % t == 0:
            return t
        t -= mult
    return n


def _sigmoid(x):
    return 1.0 / (1.0 + jnp.exp(-x))


def _me():
    return lax.axis_index("x"), lax.axis_index("y"), lax.axis_index("c")


class _Comm:
    def __init__(self, arrays, out_shapes, sems, start, finish):
        self.arrays, self.out_shapes, self.sems, self.start, self.finish = arrays, out_shapes, sems, start, finish


def _pcall(body, *, name, grid, in_specs, out_specs, out_shape, args, scratch_shapes=(), sem=None, vmem=None, comms=(),
           aliases=None):
    aliases = aliases or {}
    if not comms:
        return pl.pallas_call(body, name=name, grid=grid, in_specs=in_specs, out_specs=out_specs, out_shape=out_shape,
                              scratch_shapes=list(scratch_shapes), input_output_aliases=aliases,
                              compiler_params=_cp(sem, vmem))(*args)
    single = not isinstance(out_shape, (list, tuple))
    outs_shape = [out_shape] if single else list(out_shape)
    outs_spec = [out_specs] if single else list(out_specs)
    n_in, n_out, n_scr = len(args), len(outs_shape), len(scratch_shapes)
    c_in = [a for cm in comms for a in cm.arrays]
    c_out = [s for cm in comms for s in cm.out_shapes]
    c_scr = [s for cm in comms for s in cm.sems]

    def split(refs, counts):
        out, pos = [], 0
        for n in counts:
            out.append(refs[pos:pos + n])
            pos += n
        return out

    def wrapped(*refs):
        ins, cins, outs, couts, scr, cscr = split(refs, [n_in, len(c_in), n_out, len(c_out), n_scr, len(c_scr)])
        ids = [pl.program_id(a) for a in range(len(grid))]
        first = functools.reduce(jnp.logical_and, [i == 0 for i in ids])
        last = functools.reduce(jnp.logical_and, [i == g - 1 for i, g in zip(ids, grid)])
        per_in = split(cins, [len(cm.arrays) for cm in comms])
        per_out = split(couts, [len(cm.out_shapes) for cm in comms])
        per_sem = split(cscr, [len(cm.sems) for cm in comms])

        @pl.when(first)
        def _():
            for cm, a, b, s in zip(comms, per_in, per_out, per_sem):
                cm.start(a, b, s)

        body(*ins, *outs, *scr)

        @pl.when(last)
        def _():
            for cm, a, b, s in zip(comms, per_in, per_out, per_sem):
                cm.finish(a, b, s)

    hbm = pl.BlockSpec(memory_space=pl.ANY)
    res = pl.pallas_call(
        wrapped, name=name, grid=grid, in_specs=list(in_specs) + [hbm] * len(c_in),
        out_specs=outs_spec + [hbm] * len(c_out), out_shape=outs_shape + c_out,
        scratch_shapes=list(scratch_shapes) + c_scr, input_output_aliases=aliases,
        compiler_params=_cp((ARB,) * len(grid), vmem))(*args, *c_in)
    main = res[0] if single else list(res[:n_out])
    return main, split(list(res[n_out:]), [len(cm.out_shapes) for cm in comms])


def _comm_allgather(w, axis):
    n = w.shape[axis]
    out_shape = list(w.shape)
    out_shape[axis] = NDEV * n

    def parts(ins, outs, sems):
        x_ref, out_ref = ins[0], outs[0]
        send_sems, recv_sems, local_sem = sems
        mx, my, mc = _me()
        chips = [(1 - mx, my), (mx, 1 - my), (1 - mx, 1 - my)]

        def blk(px, py, pc):
            start = pl.multiple_of((4 * px + 2 * py + pc) * n, n)
            if axis == 1:
                return out_ref.at[:, pl.ds(start, n), :]
            return out_ref.at[:, :, pl.ds(start, n)]

        def copy(k, block, to, src=None):
            return pltpu.make_async_remote_copy(
                src_ref=blk(*block) if src is None else src, dst_ref=blk(*block),
                send_sem=send_sems.at[k], recv_sem=recv_sems.at[k], device_id=to, device_id_type=MESH)

        me, sibling = (mx, my, mc), (mx, my, 1 - mc)
        mine = pltpu.make_async_copy(x_ref, blk(*me), local_sem)
        first = [copy(0, me, sibling, src=x_ref)] + [copy(1 + j, me, (*chip, mc), src=x_ref) for j, chip in enumerate(chips)]
        passed = [copy(4 + j, (*chip, mc), sibling) for j, chip in enumerate(chips)]
        return me, sibling, chips, mc, copy, mine, first, passed

    def start(ins, outs, sems):
        *_, mine, first, _ = parts(ins, outs, sems)
        mine.start()
        for cp in first:
            cp.start()

    def finish(ins, outs, sems):
        me, sibling, chips, mc, copy, mine, first, passed = parts(ins, outs, sems)
        for j, chip in enumerate(chips):
            copy(1 + j, (*chip, mc), me).wait_recv()
            passed[j].start()
        copy(0, sibling, me).wait_recv()
        for j, chip in enumerate(chips):
            copy(4 + j, (*chip, 1 - mc), me).wait_recv()
        for cp in first + passed:
            cp.wait_send()
        mine.wait()

    return _Comm([w], [jax.ShapeDtypeStruct(tuple(out_shape), w.dtype)],
                 [pltpu.SemaphoreType.DMA((7,)), pltpu.SemaphoreType.DMA((7,)), pltpu.SemaphoreType.DMA], start, finish)


def _comm_rs_sibling(dwb):
    def copies(ins, outs, sems):
        mx, my, mc = _me()
        return [pltpu.make_async_remote_copy(
            src_ref=ins[0].at[2 * p + (1 - mc)], dst_ref=outs[0].at[p], send_sem=sems[0].at[p], recv_sem=sems[1].at[p],
            device_id=(mx, my, 1 - mc), device_id_type=MESH) for p in range(4)]

    def start(ins, outs, sems):
        for cp in copies(ins, outs, sems):
            cp.start()

    def finish(ins, outs, sems):
        cps = copies(ins, outs, sems)
        for cp in cps:
            cp.wait_recv()
        for cp in cps:
            cp.wait_send()

    return _Comm([dwb], [jax.ShapeDtypeStruct((4,) + dwb.shape[1:], dwb.dtype)],
                 [pltpu.SemaphoreType.DMA((4,)), pltpu.SemaphoreType.DMA((4,))], start, finish)


def _comm_rs_chips(part):
    def copies(ins, outs, sems):
        mx, my, mc = _me()
        chips = [(1 - mx, my), (mx, 1 - my), (1 - mx, 1 - my)]
        return [pltpu.make_async_remote_copy(
            src_ref=ins[0].at[2 * px + py], dst_ref=outs[0].at[k], send_sem=sems[0].at[k], recv_sem=sems[1].at[k],
            device_id=(px, py, mc), device_id_type=MESH) for k, (px, py) in enumerate(chips)]

    def start(ins, outs, sems):
        for cp in copies(ins, outs, sems):
            cp.start()

    def finish(ins, outs, sems):
        cps = copies(ins, outs, sems)
        for cp in cps:
            cp.wait_recv()
        for cp in cps:
            cp.wait_send()

    return _Comm([part], [jax.ShapeDtypeStruct((3,) + part.shape[1:], part.dtype)],
                 [pltpu.SemaphoreType.DMA((3,)), pltpu.SemaphoreType.DMA((3,))], start, finish)


def _ag_small(x, name):
    r, c = x.shape

    def body(x_ref, out_ref, send_sems, recv_sems):
        mx, my, mc = _me()
        mine = 4 * mx + 2 * my + mc
        out_ref[mine] = x_ref[...]
        copies = []
        for k in range(1, NDEV):
            px = 1 - mx if (k >> 2) & 1 else mx
            py = 1 - my if (k >> 1) & 1 else my
            pc = 1 - mc if k & 1 else mc
            cp = pltpu.make_async_remote_copy(
                src_ref=x_ref, dst_ref=out_ref.at[mine], send_sem=send_sems.at[k - 1], recv_sem=recv_sems.at[k - 1],
                device_id=(px, py, pc), device_id_type=MESH)
            cp.start()
            copies.append((cp, 4 * px + 2 * py + pc))
        for k, (cp, peer) in enumerate(copies):
            pltpu.make_async_remote_copy(
                src_ref=x_ref, dst_ref=out_ref.at[peer], send_sem=send_sems.at[k], recv_sem=recv_sems.at[k],
                device_id=(mx, my, mc), device_id_type=MESH).wait_recv()
        for cp, _ in copies:
            cp.wait_send()

    return pl.pallas_call(
        body, name=name,
        out_shape=jax.ShapeDtypeStruct((NDEV, r, c), x.dtype),
        in_specs=[pl.BlockSpec(memory_space=pltpu.VMEM)],
        out_specs=pl.BlockSpec(memory_space=pltpu.VMEM),
        scratch_shapes=[pltpu.SemaphoreType.DMA((NDEV - 1,)), pltpu.SemaphoreType.DMA((NDEV - 1,))],
    )(x)


def _ag_big(w, axis, name):
    n = w.shape[axis]
    out_shape = list(w.shape)
    out_shape[axis] = NDEV * n

    def body(x_ref, out_ref, send_sems, recv_sems, local_sem):
        mx, my, mc = _me()
        me, sibling = (mx, my, mc), (mx, my, 1 - mc)
        chips = [(1 - mx, my), (mx, 1 - my), (1 - mx, 1 - my)]

        def blk(px, py, pc):
            start = pl.multiple_of((4 * px + 2 * py + pc) * n, n)
            if axis == 1:
                return out_ref.at[:, pl.ds(start, n), :]
            return out_ref.at[:, :, pl.ds(start, n)]

        def copy(k, block, to, src=None):
            return pltpu.make_async_remote_copy(
                src_ref=blk(*block) if src is None else src, dst_ref=blk(*block),
                send_sem=send_sems.at[k], recv_sem=recv_sems.at[k], device_id=to, device_id_type=MESH)

        mine = pltpu.make_async_copy(x_ref, blk(*me), local_sem)
        mine.start()
        first = [copy(0, me, sibling, src=x_ref)]
        first += [copy(1 + j, me, (*chip, mc), src=x_ref) for j, chip in enumerate(chips)]
        for cp in first:
            cp.start()
        passed = [copy(4 + j, (*chip, mc), sibling) for j, chip in enumerate(chips)]
        for j, chip in enumerate(chips):
            copy(1 + j, (*chip, mc), me).wait_recv()
            passed[j].start()
        copy(0, sibling, me).wait_recv()
        for j, chip in enumerate(chips):
            copy(4 + j, (*chip, 1 - mc), me).wait_recv()
        for cp in first + passed:
            cp.wait_send()
        mine.wait()

    return pl.pallas_call(
        body, name=name,
        out_shape=jax.ShapeDtypeStruct(tuple(out_shape), w.dtype),
        in_specs=[pl.BlockSpec(memory_space=pl.ANY)],
        out_specs=pl.BlockSpec(memory_space=pl.ANY),
        scratch_shapes=[pltpu.SemaphoreType.DMA((7,)), pltpu.SemaphoreType.DMA((7,)), pltpu.SemaphoreType.DMA],
    )(w)


def _chip_partial(dwb, r1, core, name):
    _, A, B = dwb.shape
    ta = _tile(A, 512, 16)

    def body(c_ref, a_ref, b_ref, o_ref):
        o_ref[...] = (a_ref[...].astype(f32) + b_ref[...].astype(f32)).astype(o_ref.dtype)

    grid_spec = pltpu.PrefetchScalarGridSpec(
        num_scalar_prefetch=1, grid=(4, A // ta),
        in_specs=[pl.BlockSpec((None, ta, B), lambda p, i, c: (2 * p + c[0], i, 0)),
                  pl.BlockSpec((None, ta, B), lambda p, i, c: (p, i, 0))],
        out_specs=pl.BlockSpec((None, ta, B), lambda p, i, c: (p, i, 0)))
    return pl.pallas_call(body, name=name, grid_spec=grid_spec,
                          out_shape=jax.ShapeDtypeStruct((4, A, B), dwb.dtype),
                          compiler_params=_cp((PAR, PAR)))(core, dwb, r1)


def _adam_math(w, g, m, v):
    m2 = B1 * m + (1.0 - B1) * g
    v2 = B2 * v + (1.0 - B2) * (g * g)
    m_hat = m2 / (1.0 - B1 ** STEP)
    v_hat = v2 / (1.0 - B2 ** STEP)
    delta = -LR * (m_hat / (jnp.sqrt(v_hat) + AEPS) + WD * w)
    return delta, m2, v2


def _adamw_reduced(w, m, v, mine, r2, l, prev, name, comms=()):
    L, A, B = w.shape
    ta = _tile(A, 256, 8)

    def body(w_ref, m_ref, v_ref, p_ref, r_ref, *rest):
        g_out, d_out, m_out, v_out = rest[-4:]
        g = ((p_ref[...].astype(f32) + r_ref[0].astype(f32)) + r_ref[1].astype(f32)) + r_ref[2].astype(f32)
        d, m2, v2 = _adam_math(w_ref[...], g, m_ref[...], v_ref[...])
        g_out[...] = g
        d_out[...] = d
        m_out[...] = m2
        v_out[...] = v2

    wspec = pl.BlockSpec((None, ta, B), lambda i: (l, i, 0))
    in_specs = [wspec, wspec, wspec, pl.BlockSpec((ta, B), lambda i: (i, 0)), pl.BlockSpec((3, ta, B), lambda i: (0, i, 0))]
    args = [w, m, v, mine, r2]
    aliases = {}
    if prev is not None:
        in_specs += [pl.BlockSpec(memory_space=pl.ANY)] * 4
        args += list(prev)
        aliases = {5 + i: i for i in range(4)}
    shp = jax.ShapeDtypeStruct((L, A, B), f32)
    return _pcall(body, name=name, grid=(A // ta,), in_specs=in_specs, out_specs=[wspec] * 4, out_shape=[shp] * 4,
                  args=args, sem=(PAR,), comms=comms, aliases=aliases)


def _adamw_plain(w, m, v, g, name):
    A, B = w.shape
    ta = _tile(A, 256, 8)

    def body(w_ref, m_ref, v_ref, g_ref, d_out, m_out, v_out):
        d, m2, v2 = _adam_math(w_ref[...], g_ref[...], m_ref[...], v_ref[...])
        d_out[...] = d
        m_out[...] = m2
        v_out[...] = v2

    spec = pl.BlockSpec((ta, B), lambda i: (i, 0))
    shp = jax.ShapeDtypeStruct((A, B), f32)
    return pl.pallas_call(body, name=name, grid=(A // ta,), in_specs=[spec] * 4, out_specs=[spec] * 3,
                          out_shape=[shp, shp, shp], compiler_params=_cp((PAR,)))(w, m, v, g)


def _sum8(g, name):
    _, R, C = g.shape

    def body(g_ref, o_ref):
        acc = g_ref[0]
        for j in range(1, NDEV):
            acc = acc + g_ref[j]
        o_ref[...] = acc

    return pl.pallas_call(body, name=name, out_shape=jax.ShapeDtypeStruct((R, C), f32))(g)


def _modproj(c_all, w, bias, name):
    K, N = w.shape
    tn = _tile(N, 512)

    def body(c_ref, w_ref, b_ref, o_ref):
        cc = c_ref[...]
        sc = (cc * _sigmoid(cc)).astype(bf16)
        o_ref[...] = jnp.dot(sc, w_ref[...].astype(bf16), preferred_element_type=f32) + b_ref[...]

    return pl.pallas_call(
        body, name=name, grid=(N // tn,),
        in_specs=[pl.BlockSpec((NDEV, K), lambda j: (0, 0)), pl.BlockSpec((K, tn), lambda j: (0, j)),
                  pl.BlockSpec((1, tn), lambda j: (0, j))],
        out_specs=pl.BlockSpec((NDEV, tn), lambda j: (0, j)),
        out_shape=jax.ShapeDtypeStruct((NDEV, N), f32), compiler_params=_cp((PAR,)))(c_all, w, bias)


def _modgrad(c_all_t, dm, name):
    K = c_all_t.shape[0]
    N = dm.shape[1]
    tn = _tile(N, 512)

    def body(c_ref, d_ref, o_ref):
        cc = c_ref[...]
        sc = cc * _sigmoid(cc)
        dmv = d_ref[...]
        acc = sc[:, 0:1] * dmv[0:1, :]
        for b in range(1, NDEV):
            acc = acc + sc[:, b:b + 1] * dmv[b:b + 1, :]
        o_ref[...] = acc

    return pl.pallas_call(
        body, name=name, grid=(N // tn,),
        in_specs=[pl.BlockSpec((K, NDEV), lambda j: (0, 0)), pl.BlockSpec((NDEV, tn), lambda j: (0, j))],
        out_specs=pl.BlockSpec((K, tn), lambda j: (0, j)),
        out_shape=jax.ShapeDtypeStruct((K, N), f32), compiler_params=_cp((PAR,)))(c_all_t, dm)


def _mm_nn(a, w, l, *, name, out_dtype=bf16, bias=None, res=None, gate=None, tm=1024, tn=1024, tk=2048, comms=()):
    M, K = a.shape
    N = w.shape[2]
    tm, tn, tk = _tile(M, tm, 8), _tile(N, tn), _tile(K, tk)
    nk = K // tk
    epi = res is not None

    def body(*refs):
        it = iter(refs)
        a_ref, w_ref = next(it), next(it)
        b_ref = next(it) if bias is not None else None
        r_ref = next(it) if epi else None
        g_ref = next(it) if epi else None
        o_ref = next(it)
        f_ref = next(it) if epi else None

        def finish(y):
            if b_ref is not None:
                y = y + b_ref[...]
            if epi:
                f_ref[...] = y.astype(f_ref.dtype)
                o_ref[...] = r_ref[...] + g_ref[...] * y
            else:
                o_ref[...] = y.astype(o_ref.dtype)

        if nk == 1:
            finish(jnp.dot(a_ref[...], w_ref[...], preferred_element_type=f32))
            return
        acc = next(it)
        k = pl.program_id(2)

        @pl.when(k == 0)
        def _():
            acc[...] = jnp.zeros_like(acc)

        acc[...] += jnp.dot(a_ref[...], w_ref[...], preferred_element_type=f32)

        @pl.when(k == nk - 1)
        def _():
            finish(acc[...])

    in_specs = [pl.BlockSpec((tm, tk), lambda i, j, k: (i, k)), pl.BlockSpec((None, tk, tn), lambda i, j, k: (l, k, j))]
    args = [a, w]
    if bias is not None:
        in_specs.append(pl.BlockSpec((1, tn), lambda i, j, k: (0, j)))
        args.append(bias)
    ospec = pl.BlockSpec((tm, tn), lambda i, j, k: (i, j))
    if epi:
        in_specs += [ospec, pl.BlockSpec((1, tn), lambda i, j, k: (0, j))]
        args += [res, gate]
        out_shape = [jax.ShapeDtypeStruct((M, N), f32), jax.ShapeDtypeStruct((M, N), bf16)]
        out_specs = [ospec, ospec]
    else:
        out_shape = jax.ShapeDtypeStruct((M, N), out_dtype)
        out_specs = ospec
    return _pcall(body, name=name, grid=(M // tm, N // tn, nk), in_specs=in_specs, out_specs=out_specs, out_shape=out_shape,
                  args=args, scratch_shapes=[pltpu.VMEM((tm, tn), f32)] if nk > 1 else [], sem=(PAR, PAR, ARB), vmem=VMEM_BIG,
                  comms=comms)


def _mm_nt(a, w, l, *, name, out_dtype, tm=1024, tko=2048, tn=1024, comms=()):
    planes = a.ndim == 3
    M = a.shape[-2]
    K, N = w.shape[1], w.shape[2]
    npl = a.shape[-1]
    tm, tko = _tile(M, tm, 8), _tile(K, tko)
    tn = _tile(npl, tn)
    nn = N // tn
    per_plane = npl // tn

    def body(a_ref, w_ref, o_ref, *scratch):
        if nn == 1:
            o_ref[...] = _dot_nt(a_ref[...], w_ref[...]).astype(o_ref.dtype)
            return
        acc = scratch[0]
        k = pl.program_id(2)

        @pl.when(k == 0)
        def _():
            acc[...] = jnp.zeros_like(acc)

        acc[...] += _dot_nt(a_ref[...], w_ref[...])

        @pl.when(k == nn - 1)
        def _():
            o_ref[...] = acc[...].astype(o_ref.dtype)

    if planes:
        a_spec = pl.BlockSpec((None, tm, tn), lambda i, j, k: (k // per_plane, i, k % per_plane))
    else:
        a_spec = pl.BlockSpec((tm, tn), lambda i, j, k: (i, k))
    return _pcall(body, name=name, grid=(M // tm, K // tko, nn),
                  in_specs=[a_spec, pl.BlockSpec((None, tko, tn), lambda i, j, k: (l, j, k))],
                  out_specs=pl.BlockSpec((tm, tko), lambda i, j, k: (i, j)),
                  out_shape=jax.ShapeDtypeStruct((M, K), out_dtype), args=[a, w],
                  scratch_shapes=[pltpu.VMEM((tm, tko), f32)] if nn > 1 else [], sem=(PAR, PAR, ARB), vmem=VMEM_BIG,
                  comms=comms)


def _mm_tn(a, b, *, name, col_sharded, comms=()):
    planes = b.ndim == 3
    S, K = a.shape
    N = b.shape[-1] * (2 if planes else 1)
    if col_sharded:
        tn, tk, ts = N // NDEV, _tile(K, 1024), _tile(S, 2048, 16)
    else:
        tn, tk, ts = N, _tile(K, 1408), _tile(S, 1024, 16)
    ns_steps = S // ts
    per_plane = (b.shape[-1] // tn) if planes else 0

    def body(a_ref, b_ref, o_ref, acc):
        s = pl.program_id(2)

        @pl.when(s == 0)
        def _():
            acc[...] = jnp.zeros_like(acc)

        acc[...] += lax.dot_general(a_ref[...], b_ref[...], (((0,), (0,)), ((), ())), preferred_element_type=f32)

        @pl.when(s == ns_steps - 1)
        def _():
            o_ref[...] = acc[...].astype(o_ref.dtype)

    if planes:
        b_spec = pl.BlockSpec((None, ts, tn), lambda k, n, s: (n // per_plane, s, n % per_plane))
    else:
        b_spec = pl.BlockSpec((ts, tn), lambda k, n, s: (s, n))
    if col_sharded:
        out_shape = jax.ShapeDtypeStruct((NDEV, K, tn), bf16)
        out_spec = pl.BlockSpec((None, tk, tn), lambda k, n, s: (n, k, 0))
    else:
        out_shape = jax.ShapeDtypeStruct((K, N), bf16)
        out_spec = pl.BlockSpec((tk, tn), lambda k, n, s: (k, n))
    res = _pcall(body, name=name, grid=(K // tk, N // tn, ns_steps),
                 in_specs=[pl.BlockSpec((ts, tk), lambda k, n, s: (s, k)), b_spec],
                 out_specs=out_spec, out_shape=out_shape, args=[a, b],
                 scratch_shapes=[pltpu.VMEM((tk, tn), f32)], sem=(PAR, PAR, ARB), vmem=VMEM_BIG, comms=comms)
    out, couts = res if comms else (res, None)
    if not col_sharded:
        out = out.reshape(NDEV, K // NDEV, N)
    return (out, couts) if comms else out


def _acc_spec(w, rows=1):
    return pl.BlockSpec((rows, w), lambda i: (0, 0))


def _mod_fwd(x, g, sh, sc, name, comms=()):
    S, W = x.shape
    tm = _tile(S, 256, 8)

    def body(x_ref, g_ref, sh_ref, sc_ref, h_ref):
        xv = x_ref[...]
        r = lax.rsqrt(jnp.mean(xv * xv, axis=-1, keepdims=True) + EPS)
        h_ref[...] = ((xv * r) * g_ref[...] * (1.0 + sc_ref[...]) + sh_ref[...]).astype(h_ref.dtype)

    row = pl.BlockSpec((tm, W), lambda i: (i, 0))
    return _pcall(body, name=name, grid=(S // tm,), in_specs=[row, _acc_spec(W), _acc_spec(W), _acc_spec(W)],
                  out_specs=row, out_shape=jax.ShapeDtypeStruct((S, W), bf16), args=[x, g, sh, sc], sem=(PAR,), comms=comms)


def _mod_bwd(dh, x, dx_in, g, sc, name, comms=()):
    S, W = x.shape
    tm = _tile(S, 256, 8)
    nt = S // tm

    def body(dh_ref, x_ref, dxi_ref, g_ref, sc_ref, dx_ref, dsh_ref, dsc_ref, dg_ref):
        i = pl.program_id(0)

        @pl.when(i == 0)
        def _():
            dsh_ref[...] = jnp.zeros_like(dsh_ref)
            dsc_ref[...] = jnp.zeros_like(dsc_ref)

        xv = x_ref[...]
        dh = dh_ref[...].astype(f32)
        r = lax.rsqrt(jnp.mean(xv * xv, axis=-1, keepdims=True) + EPS)
        n = xv * r
        dn = dh * (g_ref[...] * (1.0 + sc_ref[...]))
        dx = r * (dn - n * jnp.mean(dn * n, axis=-1, keepdims=True))
        dx_ref[...] = dxi_ref[...] + dx
        dsh_ref[...] += jnp.sum(dh, axis=0, keepdims=True)
        dsc_ref[...] += jnp.sum(dh * n, axis=0, keepdims=True)

        @pl.when(i == nt - 1)
        def _():
            a2 = dsc_ref[...]
            dg_ref[...] = a2 * (1.0 + sc_ref[...])
            dsc_ref[...] = a2 * g_ref[...]

    row = pl.BlockSpec((tm, W), lambda i: (i, 0))
    vec = jax.ShapeDtypeStruct((1, W), f32)
    return _pcall(body, name=name, grid=(nt,), in_specs=[row, row, row, _acc_spec(W), _acc_spec(W)],
                  out_specs=[row, _acc_spec(W), _acc_spec(W), _acc_spec(W)],
                  out_shape=[jax.ShapeDtypeStruct((S, W), f32), vec, vec, vec], args=[dh, x, dx_in, g, sc], sem=(ARB,),
                  comms=comms)


def _gate_bwd(dx, f, gate, name):
    S, W = dx.shape
    tm = _tile(S, 256, 16)

    def body(dx_ref, f_ref, g_ref, df_ref, dg_ref, sdf_ref):
        i = pl.program_id(0)

        @pl.when(i == 0)
        def _():
            dg_ref[...] = jnp.zeros_like(dg_ref)
            sdf_ref[...] = jnp.zeros_like(sdf_ref)

        d = dx_ref[...]
        df = g_ref[...] * d
        df_ref[...] = df.astype(df_ref.dtype)
        dg_ref[...] += jnp.sum(d * f_ref[...].astype(f32), axis=0, keepdims=True)
        sdf_ref[...] += jnp.sum(df, axis=0, keepdims=True)

    row = pl.BlockSpec((tm, W), lambda i: (i, 0))
    vec = jax.ShapeDtypeStruct((1, W), f32)
    return pl.pallas_call(
        body, name=name, grid=(S // tm,), in_specs=[row, row, _acc_spec(W)], out_specs=[row, _acc_spec(W), _acc_spec(W)],
        out_shape=[jax.ShapeDtypeStruct((S, W), bf16), vec, vec], compiler_params=_cp((ARB,)))(dx, f, gate)


def _loss_grad(y, target, name):
    S, W = y.shape
    tm = _tile(S, 256, 8)

    def body(y_ref, t_ref, dy_ref, l_ref):
        i = pl.program_id(0)

        @pl.when(i == 0)
        def _():
            l_ref[...] = jnp.zeros_like(l_ref)

        e = y_ref[...] - t_ref[...]
        dy_ref[...] = e * (1.0 / W)
        l_ref[...] += 0.5 * jnp.sum(jnp.mean(e * e, axis=-1, keepdims=True))

    row = pl.BlockSpec((tm, W), lambda i: (i, 0))
    return pl.pallas_call(
        body, name=name, grid=(S // tm,), in_specs=[row, row], out_specs=[row, pl.BlockSpec((8, 128), lambda i: (0, 0))],
        out_shape=[jax.ShapeDtypeStruct((S, W), f32), jax.ShapeDtypeStruct((8, 128), f32)],
        compiler_params=_cp((ARB,)))(y, target)


def _tap_groups(offsets):
    groups = {}
    for k, o in enumerate(offsets):
        groups.setdefault(o % 8, []).append((k, o - o % 8))
    return sorted(groups.items())


def _tap_sum(buf, w, offsets, tm):
    out = None
    for b, taps in _tap_groups(offsets):
        n = tm + 8 if b else tm
        y = None
        for k, base in taps:
            term = w[k:k + 1, :] * buf[pl.ds(base, n), :]
            y = term if y is None else y + term
        part = y[b:b + tm] if b else y
        out = part if out is None else out + part
    return out


def _tap_wgrad(d, buf, dsh, acc_ref, offsets, tm):
    for b, taps in _tap_groups(offsets):
        if b:
            dsh[pl.ds(0, 8), :] = jnp.zeros((8, dsh.shape[1]), f32)
            dsh[pl.ds(tm, 8), :] = jnp.zeros((8, dsh.shape[1]), f32)
            dsh[pl.ds(b, tm), :] = d
            dd, n = dsh[...], tm + 8
        else:
            dd, n = d, tm
        for k, base in taps:
            acc_ref[pl.ds(k, 1), :] += jnp.sum(dd * buf[pl.ds(base, n), :], axis=0, keepdims=True)


_CONV_OFFSETS = [HALO - (CONV_K - 1) + k for k in range(CONV_K)]
_CONV_OFFSETS_T = [CONV_K - 1 - k for k in range(CONV_K)]


def _conv_core(u_ref, uh_ref, w_ref, b_ref, lg_ref, lb_ref, gbuf, tm, first, cv_ref=None):
    C = u_ref.shape[1] // 2
    u = u_ref[...].astype(f32)
    uh = uh_ref[...].astype(f32)
    gbuf[pl.ds(HALO, tm), :] = u[:, :C] * _sigmoid(u[:, C:])
    halo = uh[:, :C] * _sigmoid(uh[:, C:])
    gbuf[pl.ds(0, HALO), :] = jnp.where(first, 0.0, halo)
    cv = _tap_sum(gbuf, w_ref[...], _CONV_OFFSETS, tm) + b_ref[...] if cv_ref is None else cv_ref[...]
    mu = jnp.mean(cv, axis=-1, keepdims=True)
    xc = cv - mu
    rstd = lax.rsqrt(jnp.mean(xc * xc, axis=-1, keepdims=True) + EPS)
    z = xc * rstd
    ln = z * lg_ref[...] + lb_ref[...]
    return cv, z, rstd, ln


def _halo_prev(tm, hb, w):
    return pl.BlockSpec((hb, w), lambda i: (jnp.maximum(i * (tm // hb) - 1, 0), 0))


def _conv_fwd(u, w, b, lg, lb, name, comms=()):
    S, C2 = u.shape
    C = C2 // 2
    tm = _tile(S, 256, HALO)

    def body(u_ref, uh_ref, w_ref, b_ref, lg_ref, lb_ref, s_ref, cv_ref, gbuf):
        first = pl.program_id(0) == 0
        cv, _, _, ln = _conv_core(u_ref, uh_ref, w_ref, b_ref, lg_ref, lb_ref, gbuf, tm, first)
        s_ref[...] = (ln * _sigmoid(ln)).astype(s_ref.dtype)
        cv_ref[...] = cv

    return _pcall(body, name=name, grid=(S // tm,),
                  in_specs=[pl.BlockSpec((tm, C2), lambda i: (i, 0)), _halo_prev(tm, HALO, C2), _acc_spec(C, 32),
                            _acc_spec(C), _acc_spec(C), _acc_spec(C)],
                  out_specs=[pl.BlockSpec((tm, C), lambda i: (i, 0))] * 2,
                  out_shape=[jax.ShapeDtypeStruct((S, C), bf16), jax.ShapeDtypeStruct((S, C), f32)],
                  args=[u, u, w, b, lg, lb], scratch_shapes=[pltpu.VMEM((tm + HALO, C), f32)], sem=(PAR,), vmem=VMEM_BIG,
                  comms=comms)


def _conv_bwd1(u, cv, ds, w, b, lg, lb, name, comms=()):
    S, C2 = u.shape
    C = C2 // 2
    tm = _tile(S, 256, HALO)

    def body(u_ref, uh_ref, cv_ref, ds_ref, w_ref, b_ref, lg_ref, lb_ref, dcv_ref, dlg_ref, dlb_ref, ddb_ref, ddw_ref, gbuf, dsh):
        i = pl.program_id(0)

        @pl.when(i == 0)
        def _():
            dlg_ref[...] = jnp.zeros_like(dlg_ref)
            dlb_ref[...] = jnp.zeros_like(dlb_ref)
            ddb_ref[...] = jnp.zeros_like(ddb_ref)
            ddw_ref[...] = jnp.zeros_like(ddw_ref)

        _, z, rstd, ln = _conv_core(u_ref, uh_ref, w_ref, b_ref, lg_ref, lb_ref, gbuf, tm, i == 0, cv_ref)
        sg = _sigmoid(ln)
        dln = ds_ref[...].astype(f32) * (sg * (1.0 + ln * (1.0 - sg)))
        dlg_ref[...] += jnp.sum(dln * z, axis=0, keepdims=True)
        dlb_ref[...] += jnp.sum(dln, axis=0, keepdims=True)
        dz = dln * lg_ref[...]
        dcv = rstd * (dz - jnp.mean(dz, axis=-1, keepdims=True) - z * jnp.mean(dz * z, axis=-1, keepdims=True))
        dcv_ref[...] = dcv
        ddb_ref[...] += jnp.sum(dcv, axis=0, keepdims=True)
        _tap_wgrad(dcv, gbuf, dsh, ddw_ref, _CONV_OFFSETS, tm)

    vec = jax.ShapeDtypeStruct((1, C), f32)
    return _pcall(
        body, name=name, grid=(S // tm,),
        in_specs=[pl.BlockSpec((tm, C2), lambda i: (i, 0)), _halo_prev(tm, HALO, C2), pl.BlockSpec((tm, C), lambda i: (i, 0)),
                  pl.BlockSpec((tm, C), lambda i: (i, 0)), _acc_spec(C, 32), _acc_spec(C), _acc_spec(C), _acc_spec(C)],
        out_specs=[pl.BlockSpec((tm, C), lambda i: (i, 0)), _acc_spec(C), _acc_spec(C), _acc_spec(C), _acc_spec(C, 32)],
        out_shape=[jax.ShapeDtypeStruct((S, C), f32), vec, vec, vec, jax.ShapeDtypeStruct((32, C), f32)],
        args=[u, u, cv, ds, w, b, lg, lb], scratch_shapes=[pltpu.VMEM((tm + HALO, C), f32), pltpu.VMEM((tm + 8, C), f32)],
        sem=(ARB,), vmem=VMEM_BIG, comms=comms)


def _conv_bwd2(dcv, u, w, name, comms=()):
    S, C2 = u.shape
    C = C2 // 2
    tm = _tile(S, 256, HALO)
    nt = S // tm
    nhb = S // HALO

    def body(d_ref, dn_ref, u_ref, w_ref, du_ref, db_ref, dbuf):
        i = pl.program_id(0)

        @pl.when(i == 0)
        def _():
            db_ref[...] = jnp.zeros_like(db_ref)

        dbuf[pl.ds(0, tm), :] = d_ref[...]
        dbuf[pl.ds(tm, HALO), :] = jnp.where(i == nt - 1, 0.0, dn_ref[...])
        dglu = _tap_sum(dbuf, w_ref[...], _CONV_OFFSETS_T, tm)
        u = u_ref[...].astype(f32)
        a, gt = u[:, :C], u[:, C:]
        sg = _sigmoid(gt)
        da = dglu * sg
        dgt = dglu * a * sg * (1.0 - sg)
        du_ref[:, :C] = da.astype(du_ref.dtype)
        du_ref[:, C:] = dgt.astype(du_ref.dtype)
        db_ref[:, :C] += jnp.sum(da, axis=0, keepdims=True)
        db_ref[:, C:] += jnp.sum(dgt, axis=0, keepdims=True)

    return _pcall(
        body, name=name, grid=(nt,),
        in_specs=[pl.BlockSpec((tm, C), lambda i: (i, 0)),
                  pl.BlockSpec((HALO, C), lambda i: (jnp.minimum((i + 1) * (tm // HALO), nhb - 1), 0)),
                  pl.BlockSpec((tm, C2), lambda i: (i, 0)), _acc_spec(C, 32)],
        out_specs=[pl.BlockSpec((tm, C2), lambda i: (i, 0)), _acc_spec(C2)],
        out_shape=[jax.ShapeDtypeStruct((S, C2), bf16), jax.ShapeDtypeStruct((1, C2), f32)],
        args=[dcv, dcv, u, w], scratch_shapes=[pltpu.VMEM((tm + HALO, C), f32)], sem=(ARB,), vmem=VMEM_BIG, comms=comms)


def _ffn_gate_fwd(u2, w, b, name, comms=()):
    S, F2 = u2.shape
    F = F2 // 2
    cw = _tile(F, 1408)
    ncw = F // cw
    tm = _tile(S, 256, FHALO)

    def body(g_ref, gh_ref, v_ref, w_ref, b_ref, a_ref, gbuf):
        first = pl.program_id(0) == 0
        gbuf[pl.ds(FHALO, tm), :] = g_ref[...].astype(f32)
        gbuf[pl.ds(0, FHALO), :] = jnp.where(first, 0.0, gh_ref[...].astype(f32))
        w = w_ref[...]
        gc = jnp.zeros((tm, cw), f32) + b_ref[...]
        for k in range(FFN_K):
            gc = gc + w[k:k + 1, :] * gbuf[pl.ds(FHALO - (FFN_K - 1) + k, tm), :]
        a_ref[...] = (gc * _sigmoid(gc) * v_ref[...].astype(f32)).astype(a_ref.dtype)

    return _pcall(
        body, name=name, grid=(S // tm, ncw),
        in_specs=[pl.BlockSpec((tm, cw), lambda i, j: (i, j)),
                  pl.BlockSpec((FHALO, cw), lambda i, j: (jnp.maximum(i * (tm // FHALO) - 1, 0), j)),
                  pl.BlockSpec((tm, cw), lambda i, j: (i, ncw + j)),
                  pl.BlockSpec((8, cw), lambda i, j: (0, j)), pl.BlockSpec((1, cw), lambda i, j: (0, j))],
        out_specs=pl.BlockSpec((tm, cw), lambda i, j: (i, j)), out_shape=jax.ShapeDtypeStruct((S, F), bf16),
        args=[u2, u2, u2, w, b], scratch_shapes=[pltpu.VMEM((tm + FHALO, cw), f32)], sem=(PAR, PAR), comms=comms)


def _ffn_gate_bwd(u2, dact, w, b, name, comms=()):
    S, F2 = u2.shape
    F = F2 // 2
    cw = _tile(F, 1408)
    ncw = F // cw
    tm = _tile(S, 256, FHALO)
    nt = S // tm
    nhb = S // FHALO
    R = tm + 2 * FHALO

    def body(g_ref, gp_ref, gn_ref, v_ref, vn_ref, d_ref, dn_ref, w_ref, b_ref, du_ref, dw_ref, db_ref, gbuf, dbuf):
        i = pl.program_id(1)
        first, last = i == 0, i == nt - 1

        @pl.when(i == 0)
        def _():
            dw_ref[...] = jnp.zeros_like(dw_ref)
            db_ref[...] = jnp.zeros_like(db_ref)

        gbuf[pl.ds(0, FHALO), :] = jnp.where(first, 0.0, gp_ref[...].astype(f32))
        gbuf[pl.ds(FHALO, tm), :] = g_ref[...].astype(f32)
        gbuf[pl.ds(FHALO + tm, FHALO), :] = gn_ref[...].astype(f32)
        w = w_ref[...]
        n_ext = tm + FHALO
        gc = jnp.zeros((n_ext, cw), f32) + b_ref[...]
        for k in range(FFN_K):
            gc = gc + w[k:k + 1, :] * gbuf[pl.ds(FHALO - (FFN_K - 1) + k, n_ext), :]
        sg = _sigmoid(gc)
        val = jnp.concatenate([v_ref[...].astype(f32), vn_ref[...].astype(f32)], axis=0)
        dact_ext = jnp.concatenate([d_ref[...].astype(f32), jnp.where(last, 0.0, dn_ref[...].astype(f32))], axis=0)
        dgc = dact_ext * val * (sg * (1.0 + gc * (1.0 - sg)))
        dbuf[...] = dgc
        dval = dact_ext[:tm] * (gc[:tm] * sg[:tm])
        dgt = jnp.zeros((tm, cw), f32)
        for k in range(FFN_K):
            dgt = dgt + w[k:k + 1, :] * dbuf[pl.ds(FFN_K - 1 - k, tm), :]
        du_ref[0] = dgt.astype(du_ref.dtype)
        du_ref[1] = dval.astype(du_ref.dtype)
        dgc_t = dgc[:tm]
        db_ref[...] += jnp.sum(dgc_t, axis=0, keepdims=True)
        for k in range(FFN_K):
            dw_ref[pl.ds(k, 1), :] += jnp.sum(dgc_t * gbuf[pl.ds(FHALO - (FFN_K - 1) + k, tm), :], axis=0, keepdims=True)

    hb = tm // FHALO
    prev = lambda j, i: (jnp.maximum(i * hb - 1, 0), j)
    nxt = lambda j, i: (jnp.minimum((i + 1) * hb, nhb - 1), j)
    nxt_v = lambda j, i: (jnp.minimum((i + 1) * hb, nhb - 1), ncw + j)
    return _pcall(
        body, name=name, grid=(ncw, nt), comms=comms, sem=(PAR, ARB), vmem=VMEM_BIG,
        args=[u2, u2, u2, u2, u2, dact, dact, w, b],
        in_specs=[pl.BlockSpec((tm, cw), lambda j, i: (i, j)), pl.BlockSpec((FHALO, cw), prev), pl.BlockSpec((FHALO, cw), nxt),
                  pl.BlockSpec((tm, cw), lambda j, i: (i, ncw + j)), pl.BlockSpec((FHALO, cw), nxt_v),
                  pl.BlockSpec((tm, cw), lambda j, i: (i, j)), pl.BlockSpec((FHALO, cw), nxt),
                  pl.BlockSpec((8, cw), lambda j, i: (0, j)), pl.BlockSpec((1, cw), lambda j, i: (0, j))],
        out_specs=[pl.BlockSpec((2, tm, cw), lambda j, i: (0, i, j)), pl.BlockSpec((8, cw), lambda j, i: (0, j)),
                   pl.BlockSpec((1, cw), lambda j, i: (0, j))],
        out_shape=[jax.ShapeDtypeStruct((2, S, F), bf16), jax.ShapeDtypeStruct((8, F), f32), jax.ShapeDtypeStruct((1, F), f32)],
        scratch_shapes=[pltpu.VMEM((R, cw), f32), pltpu.VMEM((tm + FHALO, cw), f32)])


def _rope_tables(pos_col, name):
    S = pos_col.shape[0]
    tm = _tile(S, 512, 8)
    half = ROT // 2
    inv = THETA ** (-np.arange(0, ROT, 2, dtype=np.float32) / ROT)
    lane_freq = np.zeros((1, DH), np.float32)
    lane_freq[0, :half] = inv
    lane_freq[0, half:ROT] = inv
    lane_freq = jnp.asarray(lane_freq)

    def body(p_ref, fr_ref, c_ref, sa_ref, sb_ref):
        ang = p_ref[...].astype(f32) * fr_ref[...]
        lane = lax.broadcasted_iota(jnp.int32, (tm, DH), 1)
        cs, sn = jnp.cos(ang), jnp.sin(ang)
        c_ref[...] = jnp.where(lane < ROT, cs, 1.0)
        sa_ref[...] = jnp.where(lane < half, -sn, 0.0)
        sb_ref[...] = jnp.where((lane >= half) & (lane < ROT), sn, 0.0)

    row = pl.BlockSpec((tm, DH), lambda i: (i, 0))
    shp = jax.ShapeDtypeStruct((S, DH), f32)
    return pl.pallas_call(body, name=name, grid=(S // tm,),
                          in_specs=[pl.BlockSpec((tm, 1), lambda i: (i, 0)), pl.BlockSpec((1, DH), lambda i: (0, 0))],
                          out_specs=[row, row, row], out_shape=[shp, shp, shp], compiler_params=_cp((PAR,)))(pos_col, lane_freq)


def _swap_matrix():
    k = lax.broadcasted_iota(jnp.int32, (DH, DH), 0)
    i = lax.broadcasted_iota(jnp.int32, (DH, DH), 1)
    half = ROT // 2
    hit = ((i < half) & (k == i + half)) | ((i >= half) & (i < ROT) & (k == i - half))
    return jnp.where(hit, 1.0, 0.0).astype(bf16)


def _head_mean(x):
    return jnp.dot(x.astype(bf16), jnp.ones((DH, DH), bf16), preferred_element_type=f32) * (1.0 / DH)


def _rope(n, c, t, swap):
    return n * c + jnp.dot(n.astype(bf16), swap, preferred_element_type=f32) * t


def _rope_t(d, c, t, swap):
    return d * c + jnp.dot((d * t).astype(bf16), swap, preferred_element_type=f32)


def _qk_fwd(raw, g, tabs, width, with_values, name):
    S = raw.shape[0]
    nh = width // DH
    ow = width // NG
    hpg = ow // DH
    tm = _tile(S, 256, 16 * max(DILS))
    vgroups = [gi for gi, r in enumerate(DILS) if r > 1] if with_values else []

    def body(x_ref, g_ref, c_ref, sa_ref, sb_ref, *rest):
        o_refs = rest[:NG]
        v_refs = rest[NG:NG + len(vgroups)]
        scr, vscr = rest[NG + len(vgroups):]
        c, t, swap = c_ref[...], sa_ref[...] + sb_ref[...], _swap_matrix()
        for gi, r in enumerate(DILS):
            heads = range(gi * hpg, (gi + 1) * hpg)
            xs = [x_ref[:, h * DH:(h + 1) * DH].astype(f32) for h in heads]
            rs = [lax.rsqrt(_head_mean(xv * xv) + EPS) for xv in xs]
            ys = [_rope(xv * rv * g_ref[...], c, t, swap) for xv, rv in zip(xs, rs)]
            for hh, y in enumerate(ys):
                if r == 1:
                    o_refs[gi][:, hh * DH:(hh + 1) * DH] = y.astype(bf16)
                else:
                    scr[hh] = y
            if r > 1:
                for hh in range(hpg):
                    for j in range(r):
                        o_refs[gi][:, j * ow + hh * DH:j * ow + (hh + 1) * DH] = scr[hh, pl.ds(j, tm // r, stride=r), :].astype(bf16)
        for vi, gi in enumerate(vgroups):
            _to_view(x_ref[:, width + gi * ow:width + (gi + 1) * ow].astype(f32), v_refs[vi], vscr, DILS[gi], ow, tm)

    win = raw.shape[1] if with_values else width
    row = pl.BlockSpec((tm, win), lambda i: (i, 0))
    tab = pl.BlockSpec((tm, DH), lambda i: (i, 0))
    view = lambda r: pl.BlockSpec((tm // r, r * ow), lambda i: (i, 0))
    vshape = lambda r: jax.ShapeDtypeStruct((S // r, r * ow), bf16)
    outs = pl.pallas_call(
        body, name=name, grid=(S // tm,), in_specs=[row, _acc_spec(DH), tab, tab, tab],
        out_specs=[view(r) for r in DILS] + [view(DILS[gi]) for gi in vgroups],
        out_shape=[vshape(r) for r in DILS] + [vshape(DILS[gi]) for gi in vgroups],
        scratch_shapes=[pltpu.VMEM((hpg, tm, DH), f32), pltpu.VMEM((ow // DH, tm, DH), f32)],
        compiler_params=_cp((PAR,)))(raw, g, *tabs)
    return outs[:NG], outs[NG:]


def _qk_bwd(dparts, raw, g, tabs, width, extra, name):
    S = raw.shape[0]
    nh = width // DH
    ow = width // NG
    hpg = ow // DH
    tm = _tile(S, 256, 16 * max(DILS))
    wout = width + len(extra) * ow

    def body(*refs):
        d_refs = refs[:NG]
        x_ref, g_ref, c_ref, sa_ref, sb_ref = refs[NG:NG + 5]
        e_refs = refs[NG + 5:NG + 5 + len(extra)]
        o_ref, dg_ref, scr, vscr = refs[NG + 5 + len(extra):]
        i = pl.program_id(0)

        @pl.when(i == 0)
        def _():
            dg_ref[...] = jnp.zeros_like(dg_ref)

        c, t, swap = c_ref[...], sa_ref[...] + sb_ref[...], _swap_matrix()
        gv = g_ref[...]
        dg = jnp.zeros((1, DH), f32)
        for gi, r in enumerate(DILS):
            heads = list(range(gi * hpg, (gi + 1) * hpg))
            if r == 1:
                douts = [d_refs[gi][:, hh * DH:(hh + 1) * DH].astype(f32) for hh in range(hpg)]
            else:
                for hh in range(hpg):
                    for j in range(r):
                        scr[hh, pl.ds(j, tm // r, stride=r), :] = d_refs[gi][:, j * ow + hh * DH:j * ow + (hh + 1) * DH].astype(f32)
                douts = [scr[hh] for hh in range(hpg)]
            xs = [x_ref[:, h * DH:(h + 1) * DH].astype(f32) for h in heads]
            rs = [lax.rsqrt(_head_mean(xv * xv) + EPS) for xv in xs]
            xhs = [xv * rv for xv, rv in zip(xs, rs)]
            dns = [_rope_t(d, c, t, swap) for d in douts]
            for dn, xh in zip(dns, xhs):
                dg = dg + jnp.sum(dn * xh, axis=0, keepdims=True)
            dxns = [dn * gv for dn in dns]
            dxs = [rv * (dxn - xh * _head_mean(dxn * xh)) for rv, dxn, xh in zip(rs, dxns, xhs)]
            for h, dx in zip(heads, dxs):
                o_ref[:, h * DH:(h + 1) * DH] = dx.astype(o_ref.dtype)
        for gi, e_ref in enumerate(e_refs):
            o_ref[:, width + gi * ow:width + (gi + 1) * ow] = _from_view(e_ref, vscr, DILS[gi], ow, tm).astype(o_ref.dtype)
        dg_ref[...] += dg

    views = [pl.BlockSpec((tm // r, r * ow), lambda i: (i, 0)) for r in DILS]
    tab = pl.BlockSpec((tm, DH), lambda i: (i, 0))
    return pl.pallas_call(
        body, name=name, grid=(S // tm,),
        in_specs=views + [pl.BlockSpec((tm, width), lambda i: (i, 0)), _acc_spec(DH), tab, tab, tab] + (views if extra else []),
        out_specs=[pl.BlockSpec((tm, wout), lambda i: (i, 0)), _acc_spec(DH)],
        out_shape=[jax.ShapeDtypeStruct((S, wout), bf16), jax.ShapeDtypeStruct((1, DH), f32)],
        scratch_shapes=[pltpu.VMEM((hpg, tm, DH), f32), pltpu.VMEM((ow // DH, tm, DH), f32)],
        compiler_params=_cp((ARB,)))(*dparts, raw, g, *tabs, *extra)


def _dot_nt(a, b):
    return lax.dot_general(a, b, (((1,), (1,)), ((), ())), preferred_element_type=f32)


def _dot_tn(a, b):
    return lax.dot_general(a, b, (((0,), (0,)), ((), ())), preferred_element_type=f32)


def _band_masks():
    qi = lax.broadcasted_iota(jnp.int32, (BLK, BLK), 0)
    ki = lax.broadcasted_iota(jnp.int32, (BLK, BLK), 1)
    return ki <= qi, ki >= qi


def _attn_fwd(qv, kview, vview, vbase, r, name):
    sr = qv.shape[0]
    ow = qv.shape[1] // r
    hpg = ow // DH
    nb = sr // BLK
    scale = 1.0 / math.sqrt(DH)

    def body(q_ref, kc_ref, kp_ref, vc_ref, vp_ref, o_ref, l_ref):
        n = pl.program_id(1)
        m_cur, m_prev = _band_masks()
        m_prev = m_prev & (n > 0)
        hs = [slice(h * DH, (h + 1) * DH) for h in range(hpg)]
        s_c = [jnp.where(m_cur, _dot_nt(q_ref[:, s], kc_ref[:, s]) * scale, NEG) for s in hs]
        s_p = [jnp.where(m_prev, _dot_nt(q_ref[:, s], kp_ref[:, s]) * scale, NEG) for s in hs]
        mx = [jnp.maximum(jnp.max(a, axis=-1, keepdims=True), jnp.max(b, axis=-1, keepdims=True)) for a, b in zip(s_c, s_p)]
        p_c = [jnp.exp(a - m) for a, m in zip(s_c, mx)]
        p_p = [jnp.exp(a - m) for a, m in zip(s_p, mx)]
        den = [jnp.sum(a, axis=-1, keepdims=True) + jnp.sum(b, axis=-1, keepdims=True) for a, b in zip(p_c, p_p)]
        for h, s in enumerate(hs):
            o = jnp.dot(p_c[h].astype(bf16), vc_ref[:, s], preferred_element_type=f32)
            o = o + jnp.dot(p_p[h].astype(bf16), vp_ref[:, s], preferred_element_type=f32)
            o_ref[:, s] = (o / den[h]).astype(o_ref.dtype)
            l_ref[:, s] = jnp.broadcast_to(mx[h] + jnp.log(den[h]), (BLK, DH))

    cur = lambda j, n: (n, j)
    prev = lambda j, n: (jnp.maximum(n - 1, 0), j)
    vcur = lambda j, n: (n, vbase + j)
    vprev = lambda j, n: (jnp.maximum(n - 1, 0), vbase + j)
    blk = lambda f: pl.BlockSpec((BLK, ow), f)
    return pl.pallas_call(
        body, name=name, grid=(r, nb), in_specs=[blk(cur), blk(cur), blk(prev), blk(vcur), blk(vprev)],
        out_specs=[blk(cur), blk(cur)],
        out_shape=[jax.ShapeDtypeStruct((sr, r * ow), bf16), jax.ShapeDtypeStruct((sr, r * ow), f32)],
        compiler_params=_cp((PAR, PAR)))(qv, kview, kview, vview, vview)


def _attn_bwd_q(qv, kview, vview, vbase, do_g, lse, corr, r, name, comms=()):
    sr = qv.shape[0]
    ow = qv.shape[1] // r
    hpg = ow // DH
    nb = sr // BLK
    scale = 1.0 / math.sqrt(DH)

    def body(q_ref, kc_ref, kp_ref, vc_ref, vp_ref, do_ref, l_ref, c_ref, dq_ref):
        n = pl.program_id(1)
        m_cur, m_prev = _band_masks()
        m_prev = m_prev & (n > 0)
        hs = [slice(h * DH, (h + 1) * DH) for h in range(hpg)]
        ls = [slice(h * DH, h * DH + BLK) for h in range(hpg)]
        sides = ((kc_ref, vc_ref, m_cur), (kp_ref, vp_ref, m_prev))
        sc = [[jnp.where(msk, _dot_nt(q_ref[:, s], k_ref[:, s]) * scale, NEG) for s in hs] for k_ref, _, msk in sides]
        dp = [[_dot_nt(do_ref[:, s], v_ref[:, s]) for s in hs] for _, v_ref, _ in sides]
        ds = [[(jnp.exp(sc[i][h] - l_ref[:, ls[h]]) * (dp[i][h] + c_ref[:, ls[h]])).astype(bf16) for h in range(hpg)]
              for i in range(2)]
        for h, s in enumerate(hs):
            dq = jnp.dot(ds[0][h], kc_ref[:, s], preferred_element_type=f32)
            dq = dq + jnp.dot(ds[1][h], kp_ref[:, s], preferred_element_type=f32)
            dq_ref[:, s] = (dq * scale).astype(dq_ref.dtype)

    cur = lambda j, n: (n, j)
    prev = lambda j, n: (jnp.maximum(n - 1, 0), j)
    vcur = lambda j, n: (n, vbase + j)
    vprev = lambda j, n: (jnp.maximum(n - 1, 0), vbase + j)
    blk = lambda f: pl.BlockSpec((BLK, ow), f)
    return _pcall(
        body, name=name, grid=(r, nb),
        in_specs=[blk(cur), blk(cur), blk(prev), blk(vcur), blk(vprev), blk(cur), blk(cur), blk(cur)],
        out_specs=blk(cur), out_shape=jax.ShapeDtypeStruct((sr, r * ow), bf16), sem=(PAR, PAR), comms=comms,
        args=[qv, kview, kview, vview, vview, do_g, lse, corr])


def _attn_bwd_kv(qv, kview, vview, vbase, do_g, lse, corr, r, name):
    sr = qv.shape[0]
    ow = qv.shape[1] // r
    hpg = ow // DH
    nb = sr // BLK
    scale = 1.0 / math.sqrt(DH)

    def body(k_ref, v_ref, qc_ref, qn_ref, doc_ref, don_ref, lc_ref, ln_ref, cc_ref, cn_ref, dk_ref, dv_ref):
        n = pl.program_id(1)
        m_cur, m_prev = _band_masks()
        m_next = m_prev & (n < nb - 1)
        hs = [slice(h * DH, (h + 1) * DH) for h in range(hpg)]
        ls = [slice(h * DH, h * DH + BLK) for h in range(hpg)]
        sides = ((qc_ref, doc_ref, lc_ref, cc_ref, m_cur), (qn_ref, don_ref, ln_ref, cn_ref, m_next))
        sc = [[jnp.where(msk, _dot_nt(q_ref[:, s], k_ref[:, s]) * scale, NEG) for s in hs] for q_ref, _, _, _, msk in sides]
        dp = [[_dot_nt(do_ref[:, s], v_ref[:, s]) for s in hs] for _, do_ref, _, _, _ in sides]
        p = [[jnp.exp(sc[i][h] - sides[i][2][:, ls[h]]) for h in range(hpg)] for i in range(2)]
        ds = [[(p[i][h] * (dp[i][h] + sides[i][3][:, ls[h]])).astype(bf16) for h in range(hpg)] for i in range(2)]
        for h, s in enumerate(hs):
            dv = _dot_tn(p[0][h].astype(bf16), doc_ref[:, s]) + _dot_tn(p[1][h].astype(bf16), don_ref[:, s])
            dk = _dot_tn(ds[0][h], qc_ref[:, s]) + _dot_tn(ds[1][h], qn_ref[:, s])
            dk_ref[:, s] = (dk * scale).astype(dk_ref.dtype)
            dv_ref[:, s] = dv.astype(dv_ref.dtype)

    cur = lambda j, n: (n, j)
    nxt = lambda j, n: (jnp.minimum(n + 1, nb - 1), j)
    vcur = lambda j, n: (n, vbase + j)
    blk = lambda f: pl.BlockSpec((BLK, ow), f)
    shp = jax.ShapeDtypeStruct((sr, r * ow), bf16)
    return pl.pallas_call(
        body, name=name, grid=(r, nb),
        in_specs=[blk(cur), blk(vcur), blk(cur), blk(nxt), blk(cur), blk(nxt), blk(cur), blk(nxt), blk(cur), blk(nxt)],
        out_specs=[blk(cur), blk(cur)], out_shape=[shp, shp],
        compiler_params=_cp((PAR, PAR)))(kview, vview, qv, qv, do_g, do_g, lse, lse, corr, corr)


def _mix_weights(l_refs):
    ls = [l[...] for l in l_refs]
    mx = functools.reduce(jnp.maximum, ls)
    es = [jnp.exp(l - mx) for l in ls]
    den = functools.reduce(lambda a, b: a + b, es)
    return [e / den for e in es]


def _from_view(ref, scr, r, ow, tm):
    if r == 1:
        return ref[...].astype(f32)
    for c in range(ow // DH):
        for j in range(r):
            scr[c, pl.ds(j, tm // r, stride=r), :] = ref[:, j * ow + c * DH:j * ow + (c + 1) * DH].astype(f32)
    return jnp.concatenate([scr[c] for c in range(ow // DH)], axis=1)


def _to_view(val, ref, scr, r, ow, tm):
    if r == 1:
        ref[...] = val.astype(ref.dtype)
        return
    for c in range(ow // DH):
        scr[c] = val[:, c * DH:(c + 1) * DH]
        for j in range(r):
            ref[:, j * ow + c * DH:j * ow + (c + 1) * DH] = scr[c, pl.ds(j, tm // r, stride=r), :].astype(ref.dtype)


def _view_specs(tm, ow):
    return [pl.BlockSpec((tm // r, r * ow), lambda i: (i, 0)) for r in DILS]


def _combine_fwd(os_, lses, name):
    ow = os_[0].shape[1] // DILS[0]
    S = os_[0].shape[0] * DILS[0]
    tm = _tile(S, 256, 16 * max(DILS))

    def body(*refs):
        o_refs, l_refs, out_ref = refs[:NG], refs[NG:2 * NG], refs[2 * NG]
        scr = refs[2 * NG + 1:]
        ov = [_from_view(o_refs[gi], scr[2 * gi], DILS[gi], ow, tm) for gi in range(NG)]
        lv = [_from_view(l_refs[gi], scr[2 * gi + 1], DILS[gi], ow, tm) for gi in range(NG)]
        al = _mix_weights(lv)
        acc = al[0] * ov[0]
        for gi in range(1, NG):
            acc = acc + al[gi] * ov[gi]
        out_ref[...] = acc.astype(out_ref.dtype)

    views = _view_specs(tm, ow)
    return pl.pallas_call(body, name=name, grid=(S // tm,), in_specs=views + views,
                          out_specs=pl.BlockSpec((tm, ow), lambda i: (i, 0)), out_shape=jax.ShapeDtypeStruct((S, ow), bf16),
                          scratch_shapes=[pltpu.VMEM((ow // DH, tm, DH), f32)] * (2 * NG),
                          compiler_params=_cp((PAR,), VMEM_BIG))(*os_, *lses)


def _combine_bwd(do, os_, lses, name, comms=()):
    S, ow = do.shape
    hpg = ow // DH
    tm = _tile(S, 256, 16 * max(DILS))

    def body(*refs):
        do_ref = refs[0]
        o_refs, l_refs = refs[1:1 + NG], refs[1 + NG:1 + 2 * NG]
        dog_refs, c_refs = refs[1 + 2 * NG:1 + 3 * NG], refs[1 + 3 * NG:1 + 4 * NG]
        scr = refs[1 + 4 * NG:]
        ov = [_from_view(o_refs[gi], scr[2 * gi], DILS[gi], ow, tm) for gi in range(NG)]
        lv = [_from_view(l_refs[gi], scr[2 * gi + 1], DILS[gi], ow, tm) for gi in range(NG)]
        al = _mix_weights(lv)
        dov = do_ref[...]
        o = al[0] * ov[0]
        for gi in range(1, NG):
            o = o + al[gi] * ov[gi]
        prod = dov * o
        t = jnp.concatenate(
            [jnp.broadcast_to(jnp.sum(prod[:, h * DH:(h + 1) * DH], axis=-1, keepdims=True), (tm, DH)) for h in range(hpg)],
            axis=1)
        for gi in range(NG):
            _to_view(al[gi] * dov, dog_refs[gi], scr[2 * NG], DILS[gi], ow, tm)
            _to_view(-(al[gi] * t), c_refs[gi], scr[2 * NG], DILS[gi], ow, tm)

    views = _view_specs(tm, ow)
    vshape = lambda dt: [jax.ShapeDtypeStruct((S // r, r * ow), dt) for r in DILS]
    return _pcall(
        body, name=name, grid=(S // tm,), in_specs=[pl.BlockSpec((tm, ow), lambda i: (i, 0))] + views + views,
        out_specs=views + views, out_shape=vshape(bf16) + vshape(f32),
        args=[do, *os_, *lses], scratch_shapes=[pltpu.VMEM((ow // DH, tm, DH), f32)] * (2 * NG + 1), sem=(PAR,), vmem=VMEM_BIG,
        comms=comms)


def _pad_rows(w, rows):
    return jnp.concatenate([w, jnp.zeros((rows - w.shape[0], w.shape[1]), w.dtype)], axis=0)


def kernel(x, c, positions, mod_w, mod_b, norm_mix_g, norm_ffn_g, conv_pw1_w, conv_pw1_b, conv_dw_w, conv_dw_b, conv_ln_g, conv_ln_b, conv_pw2_w, conv_pw2_b, kv_mod_w, kv_mod_b, kv_norm_g, w_kv, k_norm_g, w_q, q_norm_g, w_o, ffn_up_w, ffn_dw_w, ffn_dw_b, ffn_down_w, loss_target, m_mod_w, m_mod_b, m_norm_mix_g, m_norm_ffn_g, m_conv_pw1_w, m_conv_pw1_b, m_conv_dw_w, m_conv_dw_b, m_conv_ln_g, m_conv_ln_b, m_conv_pw2_w, m_conv_pw2_b, m_kv_mod_w, m_kv_mod_b, m_kv_norm_g, m_w_kv, m_k_norm_g, m_w_q, m_q_norm_g, m_w_o, m_ffn_up_w, m_ffn_dw_w, m_ffn_dw_b, m_ffn_down_w, v_mod_w, v_mod_b, v_norm_mix_g, v_norm_ffn_g, v_conv_pw1_w, v_conv_pw1_b, v_conv_dw_w, v_conv_dw_b, v_conv_ln_g, v_conv_ln_b, v_conv_pw2_w, v_conv_pw2_b, v_kv_mod_w, v_kv_mod_b, v_kv_norm_g, v_w_kv, v_k_norm_g, v_w_q, v_q_norm_g, v_w_o, v_ffn_up_w, v_ffn_dw_w, v_ffn_dw_b, v_ffn_down_w):
    S, Dm = x.shape[1], x.shape[2]
    F = ffn_dw_b.shape[1]
    QW = NG * HPG * DH
    OW = HPG * DH
    mx, my, mc = _me()
    me = 4 * mx + 2 * my + mc
    core = jnp.reshape(mc, (1,)).astype(jnp.int32)
    chip = jnp.reshape(2 * mx + my, (1,)).astype(jnp.int32)
    x0 = x.reshape(S, Dm)
    target = loss_target.reshape(S, Dm)

    c_all = _ag_small(c, "ag_c").reshape(NDEV, Dm)
    n_mod = mod_w.shape[2]
    n_kvm = kv_mod_w.shape[1]
    b0 = lax.dynamic_slice(mod_b, (0, me * n_mod), (1, n_mod))
    b1 = lax.dynamic_slice(mod_b, (1, me * n_mod), (1, n_mod))
    bk = lax.dynamic_slice(kv_mod_b.reshape(1, -1), (0, me * n_kvm), (1, n_kvm))
    m_part = jnp.concatenate([_modproj(c_all, mod_w[0], b0, "modproj0"), _modproj(c_all, mod_w[1], b1, "modproj1"),
                              _modproj(c_all, kv_mod_w, bk, "modproj_kv")], axis=1)
    m_all = _ag_small(m_part, "ag_mod")
    m_mine = lax.dynamic_index_in_dim(m_all, me, axis=1, keepdims=False)
    mod0 = m_mine[:, :n_mod].reshape(6, Dm)
    mod1 = m_mine[:, n_mod:2 * n_mod].reshape(6, Dm)
    modkv = m_mine[:, 2 * n_mod:].reshape(2, Dm)
    row = lambda a, i: a[i:i + 1]

    as3 = lambda w: w if w.ndim == 3 else w[None]
    sh16 = lambda w: as3(w).astype(bf16)
    ag_pw1 = _comm_allgather(sh16(conv_pw1_w), 2)
    ag_pw2 = _comm_allgather(sh16(conv_pw2_w), 1)
    ag_up = [_comm_allgather(sh16(ffn_up_w[l]), 2) for l in range(2)]
    ag_down = [_comm_allgather(sh16(ffn_down_w[l]), 1) for l in range(2)]
    ag_kv = _comm_allgather(sh16(w_kv), 2)
    ag_q = _comm_allgather(sh16(w_q), 2)
    ag_o = _comm_allgather(sh16(w_o), 2)

    sp_flat = jnp.concatenate([conv_pw1_b.reshape(-1), conv_dw_b.reshape(-1), conv_ln_g.reshape(-1), conv_ln_b.reshape(-1),
                               conv_pw2_b.reshape(-1), conv_dw_w.reshape(-1), ffn_dw_w.reshape(-1)])
    sp_rows = -(-sp_flat.shape[0] // 1024) * 8
    sp_flat = jnp.concatenate([sp_flat, jnp.zeros((sp_rows * 128 - sp_flat.shape[0],), f32)]).reshape(sp_rows, 128)
    n1, nd = conv_pw1_b.shape[1], conv_dw_b.shape[1]
    nfw = ffn_dw_w.shape[2]
    sp = _ag_small(sp_flat, "ag_small_params").reshape(NDEV, -1)
    off = 0
    pw1_b = sp[:, off:off + n1].reshape(1, -1); off += n1
    dw_b = sp[:, off:off + nd].reshape(1, -1); off += nd
    ln_g = sp[:, off:off + nd].reshape(1, -1); off += nd
    ln_b = sp[:, off:off + nd].reshape(1, -1); off += nd
    pw2_b = sp[:, off:off + nd].reshape(1, -1); off += nd
    dw_w = jnp.transpose(sp[:, off:off + CONV_K * nd].reshape(NDEV, CONV_K, nd), (1, 0, 2)).reshape(CONV_K, -1); off += CONV_K * nd
    fdw_w = jnp.transpose(sp[:, off:off + 2 * FFN_K * nfw].reshape(NDEV, 2, FFN_K, nfw), (1, 2, 0, 3)).reshape(2, FFN_K, -1)
    dw_w32 = _pad_rows(dw_w, 32)

    tabs = _rope_tables(positions.reshape(S, 1), "rope_tables")

    def with_comms(res, comms):
        return res if comms else (res, [])

    def rs_d2d(dwb):
        return [_comm_rs_sibling(dwb)]

    def rs_add(dwb, couts, tag):
        return _chip_partial(dwb, couts[0][0], core, f"rs_add_{tag}")

    def rs_ici(part):
        return [_comm_rs_chips(part)]

    def ffn_forward(xin, l, modv, w_up, w_down, up_comms, gate_comms, down_comms):
        h2 = _mod_fwd(xin, row(norm_ffn_g, l), row(modv, 3), row(modv, 4), f"ffn{l}_mod")
        u2, c_up = with_comms(_mm_nn(h2, w_up, 0, name=f"ffn{l}_up", comms=up_comms), up_comms)
        if w_down is None:
            w_down, c_up = c_up[0][0], c_up[1:]
        fw8 = _pad_rows(fdw_w[l], 8)
        act, c_gate = with_comms(_ffn_gate_fwd(u2, fw8, row(ffn_dw_b, l), f"ffn{l}_gate", comms=gate_comms), gate_comms)
        (xout, f), c_down = with_comms(
            _mm_nn(act, w_down, 0, name=f"ffn{l}_down", res=xin, gate=row(modv, 5), tk=F, tn=512, comms=down_comms), down_comms)
        return xout, (h2, u2, act, f, fw8, w_up, w_down), c_up, c_gate, c_down

    def ffn_backward(dx, xin, l, modv, saved, dact_comms):
        h2, u2, act, f, fw8, w_up, w_down = saved
        df, dgate, _ = _gate_bwd(dx, f, row(modv, 5), f"ffn{l}_gate_bwd")
        dact, c_dact = with_comms(_mm_nt(df, w_down, 0, name=f"ffn{l}_dact", out_dtype=bf16, tko=512, tn=Dm, comms=dact_comms), dact_comms)
        d_down = _mm_tn(act, df, name=f"ffn{l}_ddown", col_sharded=False)
        (du2, d_fw, d_fb), c1 = _ffn_gate_bwd(u2, dact, fw8, row(ffn_dw_b, l), f"ffn{l}_gatebwd", comms=rs_d2d(d_down))
        part_down = rs_add(d_down, c1, f"down{l}")
        dh2, c2 = _mm_nt(du2, w_up, 0, name=f"ffn{l}_dh", out_dtype=f32, tko=1024, tn=F // 2, comms=rs_ici(part_down))
        d_up = _mm_tn(h2, du2, name=f"ffn{l}_dup", col_sharded=True)
        (dxin, dsh, dsc, dg), c3 = _mod_bwd(dh2, xin, dx, row(norm_ffn_g, l), row(modv, 4), f"ffn{l}_mod_bwd", comms=rs_d2d(d_up))
        part_up = rs_add(d_up, c3, f"up{l}")
        grads = dict(d_fw=d_fw[:FFN_K], d_fb=d_fb, dsh=dsh, dsc=dsc, dgate=dgate, dg=dg,
                     down=(part_down, c2[0][0]), part_up=part_up)
        return dxin, grads, c_dact

    h0, c = _mod_fwd(x0, row(norm_mix_g, 0), row(mod0, 0), row(mod0, 1), "l0_mod", comms=[ag_pw1])
    W_pw1 = c[0][0]
    u0, c = _mm_nn(h0, W_pw1, 0, name="l0_pw1", bias=pw1_b, comms=[ag_pw2])
    W_pw2 = c[0][0]
    (s0, cv0), c = _conv_fwd(u0, dw_w32, dw_b, ln_g, ln_b, "l0_conv", comms=[ag_up[0]])
    W_up0 = c[0][0]
    x1, f0 = _mm_nn(s0, W_pw2, 0, name="l0_pw2", bias=pw2_b, res=x0, gate=row(mod0, 2))
    x2, ffn0_saved, c_up, c_gate, c_down = ffn_forward(x1, 0, mod0, W_up0, None, [ag_down[0], ag_kv], [ag_q, ag_o, ag_down[1]],
                                                       [ag_up[1]])
    W_kv, W_q, W_o, W_down1, W_up1 = c_up[0][0], c_gate[0][0], c_gate[1][0], c_gate[2][0], c_down[0][0]

    hkv = _mod_fwd(x2, kv_norm_g.reshape(1, -1), row(modkv, 0), row(modkv, 1), "kv_mod")
    kvraw = _mm_nn(hkv, W_kv, 0, name="kv_proj")
    kg = k_norm_g.reshape(1, -1)
    k_gv, v_dil = _qk_fwd(kvraw, kg, tabs, QW, True, "k_norm_rope")
    dilated = [gi for gi, r in enumerate(DILS) if r > 1]
    v_of = {gi: (kvraw, NG + gi) for gi, r in enumerate(DILS) if r == 1}
    v_of.update({gi: (v_dil[i], 0) for i, gi in enumerate(dilated)})
    h1 = _mod_fwd(x2, row(norm_mix_g, 1), row(mod1, 0), row(mod1, 1), "l1_mod")
    qraw = _mm_nn(h1, W_q, 0, name="q_proj")
    qg = q_norm_g.reshape(1, -1)
    q_gv, _ = _qk_fwd(qraw, qg, tabs, QW, False, "q_norm_rope")
    o_gs, lses = [], []
    for gi, r in enumerate(DILS):
        o_g, lse_g = _attn_fwd(q_gv[gi], k_gv[gi], *v_of[gi], r, f"attn_fwd{gi}")
        o_gs.append(o_g)
        lses.append(lse_g)
    o_mix = _combine_fwd(o_gs, lses, "attn_mix")
    x3, f1 = _mm_nn(o_mix, W_o, 0, name="o_proj", res=x2, gate=row(mod1, 2))
    x4, ffn1_saved, _, _, _ = ffn_forward(x3, 1, mod1, W_up1, W_down1, (), (), ())

    dx4, loss_blk = _loss_grad(x4, target, "loss")
    loss = lax.psum(loss_blk[0, 0], ("x", "y", "c"))

    red = {}
    dx3, gf1, _ = ffn_backward(dx4, x3, 1, mod1, ffn1_saved, ())
    dy1, dgate_m1, _ = _gate_bwd(dx3, f1, row(mod1, 2), "l1_gate_bwd")
    do = _mm_nt(dy1, W_o, 0, name="o_proj_dx", out_dtype=f32, tko=1024, tn=Dm)
    d_wo = _mm_tn(o_mix, dy1, name="o_proj_dw", col_sharded=True)
    outs, c = _combine_bwd(do, o_gs, lses, "attn_mix_bwd", comms=rs_d2d(d_wo))
    part_wo = rs_add(d_wo, c, "wo")
    do_gs, corrs = outs[:NG], outs[NG:]
    dq_gs, dk_gs, dv_gs = [], [], []
    for gi, r in enumerate(DILS):
        cm = rs_ici(part_wo) if gi == 0 else ()
        dq_g, c = with_comms(_attn_bwd_q(q_gv[gi], k_gv[gi], *v_of[gi], do_gs[gi], lses[gi], corrs[gi], r, f"attn_bwd_q{gi}",
                                         comms=cm), cm)
        if gi == 0:
            red["w_o"] = (part_wo, c[0][0])
        dq_gs.append(dq_g)
        dk_g, dv_g = _attn_bwd_kv(q_gv[gi], k_gv[gi], *v_of[gi], do_gs[gi], lses[gi], corrs[gi], r, f"attn_bwd_kv{gi}")
        dk_gs.append(dk_g)
        dv_gs.append(dv_g)
    dqraw, d_qg = _qk_bwd(dq_gs, qraw, qg, tabs, QW, (), "q_norm_rope_bwd")
    dkvraw, d_kg = _qk_bwd(dk_gs, kvraw, kg, tabs, QW, tuple(dv_gs), "k_norm_rope_bwd")
    dh1 = _mm_nt(dqraw, W_q, 0, name="q_proj_dx", out_dtype=f32, tko=1024, tn=QW)
    d_wq = _mm_tn(h1, dqraw, name="q_proj_dw", col_sharded=True)
    dhkv, c = _mm_nt(dkvraw, W_kv, 0, name="kv_proj_dx", out_dtype=f32, tko=512, tn=2 * QW, comms=rs_d2d(d_wq))
    part_wq = rs_add(d_wq, c, "wq")
    d_wkv, c = _mm_tn(hkv, dkvraw, name="kv_proj_dw", col_sharded=True, comms=rs_ici(gf1["part_up"]))
    red["ffn_up_w1"] = (gf1["part_up"], c[0][0])
    (dx2a, dsh_m1, dsc_m1, dg_mix1), c = _mod_bwd(dh1, x2, dx3, row(norm_mix_g, 1), row(mod1, 1), "l1_mod_bwd",
                                                  comms=rs_ici(part_wq))
    red["w_q"] = (part_wq, c[0][0])
    (dx2, dsh_kv, dsc_kv, dg_kvn), c = _mod_bwd(dhkv, x2, dx2a, kv_norm_g.reshape(1, -1), row(modkv, 1), "kv_mod_bwd",
                                                comms=rs_d2d(d_wkv))
    part_wkv = rs_add(d_wkv, c, "wkv")

    dx1, gf0, c = ffn_backward(dx2, x1, 0, mod0, ffn0_saved, rs_ici(part_wkv))
    red["w_kv"] = (part_wkv, c[0][0])
    dy0, dgate_m0, d_pw2b = _gate_bwd(dx1, f0, row(mod0, 2), "l0_gate_bwd")
    ds0 = _mm_nt(dy0, W_pw2, 0, name="l0_pw2_dx", out_dtype=bf16, tko=1024, tn=Dm)
    d_pw2 = _mm_tn(s0, dy0, name="l0_pw2_dw", col_sharded=False)
    (dcv, d_lng, d_lnb, d_dwb, d_dww), c = _conv_bwd1(u0, cv0, ds0, dw_w32, dw_b, ln_g, ln_b, "l0_conv_bwd1",
                                                      comms=rs_ici(gf0["part_up"]))
    red["ffn_up_w0"] = (gf0["part_up"], c[0][0])
    (du0, d_pw1b), c = _conv_bwd2(dcv, u0, dw_w32, "l0_conv_bwd2", comms=rs_d2d(d_pw2))
    part_pw2 = rs_add(d_pw2, c, "pw2")
    d_pw1 = _mm_tn(h0, du0, name="l0_pw1_dw", col_sharded=True)
    dh0, c = _mm_nt(du0, W_pw1, 0, name="l0_pw1_dx", out_dtype=f32, tko=1024, tn=2 * Dm, comms=rs_ici(part_pw2) + rs_d2d(d_pw1))
    red["conv_pw2_w"] = (part_pw2, c[0][0])
    part_pw1 = rs_add(d_pw1, c[1:], "pw1")
    (grad_x, dsh_m0, dsc_m0, dg_mix0), c = _mod_bwd(dh0, x0, dx1, row(norm_mix_g, 0), row(mod0, 1), "l0_mod_bwd",
                                                    comms=rs_ici(part_pw1))
    red["conv_pw1_w"] = (part_pw1, c[0][0])
    red["ffn_down_w0"], red["ffn_down_w1"] = gf0["down"], gf1["down"]

    dm0 = [dsh_m0, dsc_m0, dgate_m0, gf0["dsh"], gf0["dsc"], gf0["dgate"]]
    dm1 = [dsh_m1, dsc_m1, dgate_m1, gf1["dsh"], gf1["dsc"], gf1["dgate"]]
    pieces = dm0 + dm1 + [dsh_kv, dsc_kv,
                          dg_mix0, dg_mix1, gf0["dg"], gf1["dg"], dg_kvn, d_kg, d_qg, gf0["d_fb"], gf1["d_fb"],
                          d_pw1b, d_dww[:CONV_K], d_dwb, d_lng, d_lnb, d_pw2b, gf0["d_fw"], gf1["d_fw"]]
    flat = jnp.concatenate([p.reshape(-1) for p in pieces])
    n_flat = flat.shape[0]
    n_rows = -(-n_flat // 1024) * 8
    flat = jnp.concatenate([flat, jnp.zeros((n_rows * 128 - n_flat,), f32)]).reshape(n_rows, 128)
    g_all = _ag_small(flat, "ag_small_grads")
    g_sum = _sum8(g_all, "sum_small_grads").reshape(-1)
    n_dm = 2 * 6 * Dm + 2 * Dm
    dm_all = g_all.reshape(NDEV, -1)[:, :n_dm]

    take_pos = [0]

    def take(shape):
        n = int(np.prod(shape))
        out = g_sum[take_pos[0]:take_pos[0] + n].reshape(shape)
        take_pos[0] += n
        return out

    g_mod_b = take((2, 6 * Dm))
    g_kv_mod_b = take((2 * Dm,))
    g_norm_mix0, g_norm_mix1 = take((Dm,)), take((Dm,))
    g_norm_ffn0, g_norm_ffn1 = take((Dm,)), take((Dm,))
    g_kv_norm = take((Dm,))
    g_k_norm = take((DH,))
    g_q_norm = take((1, DH))
    g_ffn_dw_b = take((2, F))
    shard = lambda full, n, axis: lax.dynamic_slice_in_dim(full, me * n, n, axis)
    g_pw1_b = shard(take((1, 2 * Dm)), n1, 1)
    g_dw_w = shard(take((1, CONV_K, Dm)), nd, 2)
    g_dw_b = shard(take((1, Dm)), nd, 1)
    g_ln_g = shard(take((1, Dm)), nd, 1)
    g_ln_b = shard(take((1, Dm)), nd, 1)
    g_pw2_b = shard(take((1, Dm)), nd, 1)
    g_ffn_dw_w = shard(jnp.stack([take((FFN_K, F)), take((FFN_K, F))]), nfw, 2)
    g_norm_mix = jnp.stack([g_norm_mix0, g_norm_mix1])
    g_norm_ffn = jnp.stack([g_norm_ffn0, g_norm_ffn1])

    small = [("mod_b", mod_b, m_mod_b, v_mod_b, g_mod_b), ("norm_mix_g", norm_mix_g, m_norm_mix_g, v_norm_mix_g, g_norm_mix),
             ("norm_ffn_g", norm_ffn_g, m_norm_ffn_g, v_norm_ffn_g, g_norm_ffn),
             ("conv_pw1_b", conv_pw1_b, m_conv_pw1_b, v_conv_pw1_b, g_pw1_b),
             ("conv_dw_w", conv_dw_w, m_conv_dw_w, v_conv_dw_w, g_dw_w), ("conv_dw_b", conv_dw_b, m_conv_dw_b, v_conv_dw_b, g_dw_b),
             ("conv_ln_g", conv_ln_g, m_conv_ln_g, v_conv_ln_g, g_ln_g), ("conv_ln_b", conv_ln_b, m_conv_ln_b, v_conv_ln_b, g_ln_b),
             ("conv_pw2_b", conv_pw2_b, m_conv_pw2_b, v_conv_pw2_b, g_pw2_b),
             ("kv_mod_b", kv_mod_b, m_kv_mod_b, v_kv_mod_b, g_kv_mod_b), ("kv_norm_g", kv_norm_g, m_kv_norm_g, v_kv_norm_g, g_kv_norm),
             ("k_norm_g", k_norm_g, m_k_norm_g, v_k_norm_g, g_k_norm), ("q_norm_g", q_norm_g, m_q_norm_g, v_q_norm_g, g_q_norm),
             ("ffn_dw_w", ffn_dw_w, m_ffn_dw_w, v_ffn_dw_w, g_ffn_dw_w), ("ffn_dw_b", ffn_dw_b, m_ffn_dw_b, v_ffn_dw_b, g_ffn_dw_b)]
    n_small = sum(int(np.prod(s[1].shape)) for s in small)
    rows_small = -(-n_small // 1024) * 8

    def pack(idx):
        fl = jnp.concatenate([s[idx].reshape(-1) for s in small])
        return jnp.concatenate([fl, jnp.ones((rows_small * 128 - n_small,), f32)]).reshape(rows_small, 128)

    sd, sm, sv = _adamw_plain(pack(1), pack(2), pack(3), pack(4), "adamw_small")
    res = {}
    pos = 0
    for name, w, _, _, g in small:
        n = int(np.prod(w.shape))
        cut = lambda a: a.reshape(-1)[pos:pos + n].reshape(w.shape)
        res[name] = (g.reshape(w.shape), cut(sd), cut(sm), cut(sv))
        pos += n

    c_all_t = jnp.transpose(c_all)

    def mod_update(w2d, m2d, v2d, dm_cols, tag):
        g = _modgrad(c_all_t, dm_cols, f"modgrad_{tag}")
        d, m2, v2 = _adamw_plain(w2d, m2d, v2d, g, f"adamw_{tag}")
        return g, d, m2, v2

    mw = []
    for l in range(2):
        cols = lax.dynamic_slice_in_dim(dm_all[:, l * 6 * Dm:(l + 1) * 6 * Dm], me * n_mod, n_mod, 1)
        mw.append(mod_update(mod_w[l], m_mod_w[l], v_mod_w[l], cols, f"mod_w{l}"))
    res["mod_w"] = tuple(jnp.stack([mw[0][i], mw[1][i]]) for i in range(4))
    cols = lax.dynamic_slice_in_dim(dm_all[:, 12 * Dm:], me * n_kvm, n_kvm, 1)
    res["kv_mod_w"] = mod_update(kv_mod_w, m_kv_mod_w, v_kv_mod_w, cols, "kv_mod_w")

    def mine(part):
        return lax.dynamic_index_in_dim(part, chip[0], 0, keepdims=False)

    def big(key, w, m, v, l, prev, tag, comms=()):
        part, r2 = red[key]
        return _adamw_reduced(as3(w), as3(m), as3(v), mine(part), r2, l, prev, f"adamw_{tag}", comms=comms)

    up1 = big("ffn_up_w1", ffn_up_w, m_ffn_up_w, v_ffn_up_w, 1, None, "up1")
    res["ffn_up_w"] = tuple(big("ffn_up_w0", ffn_up_w, m_ffn_up_w, v_ffn_up_w, 0, up1, "up0"))
    down1 = big("ffn_down_w1", ffn_down_w, m_ffn_down_w, v_ffn_down_w, 1, None, "down1")
    res["ffn_down_w"] = tuple(big("ffn_down_w0", ffn_down_w, m_ffn_down_w, v_ffn_down_w, 0, down1, "down0"))
    for key, w, m, v in (("conv_pw1_w", conv_pw1_w, m_conv_pw1_w, v_conv_pw1_w), ("conv_pw2_w", conv_pw2_w, m_conv_pw2_w, v_conv_pw2_w),
                         ("w_kv", w_kv, m_w_kv, v_w_kv), ("w_q", w_q, m_w_q, v_w_q), ("w_o", w_o, m_w_o, v_w_o)):
        res[key] = tuple(o.reshape(w.shape) for o in big(key, w, m, v, 0, None, key))

    order = ["mod_w", "mod_b", "norm_mix_g", "norm_ffn_g", "conv_pw1_w", "conv_pw1_b", "conv_dw_w", "conv_dw_b", "conv_ln_g",
             "conv_ln_b", "conv_pw2_w", "conv_pw2_b", "kv_mod_w", "kv_mod_b", "kv_norm_g", "w_kv", "k_norm_g", "w_q", "q_norm_g",
             "w_o", "ffn_up_w", "ffn_dw_w", "ffn_dw_b", "ffn_down_w"]
    out = [loss, grad_x.reshape(x.shape)]
    for i in range(4):
        out += [res[n][i] for n in order]
    return tuple(out)
```

```python
import functools
import math

import numpy as np
import jax
import jax.numpy as jnp
from jax import lax
from jax.experimental import pallas as pl
from jax.experimental.pallas import tpu as pltpu

f32 = jnp.float32
bf16 = jnp.bfloat16

D = 2048
SEQ = 8192
FF = 5632
CONV_K = 31
FFN_K = 3
HPG = 8
DH = 128
NG = 3
DILS = (1, 4, 16)
BLK = 128
ROT = 32
THETA = 500000.0
EPS = 1e-6
NEG = -1e30
NDEV = 8
HALO = 32
FHALO = 16

LR, B1, B2, AEPS, WD, STEP = 0.001, 0.9, 0.999, 1e-08, 0.01, 10

VMEM_BIG = 56 * 1024 * 1024

ARB = "arbitrary"
PAR = "parallel"
MESH = pl.DeviceIdType.MESH


def _cp(sem, vmem=None):
    return pltpu.CompilerParams(dimension_semantics=sem, vmem_limit_bytes=vmem)


def _tile(n, pref, mult=128):
    if n <= pref:
        return n
    t = (pref // mult) * mult
    while t >= mult:
        if n % t == 0:
            return t
        t -= mult
    return n


def _sigmoid(x):
    return 1.0 / (1.0 + jnp.exp(-x))


def _me():
    return lax.axis_index("x"), lax.axis_index("y"), lax.axis_index("c")


class _Comm:
    def __init__(self, arrays, out_shapes, sems, start, finish, mid=None):
        self.arrays, self.out_shapes, self.sems, self.start, self.finish = arrays, out_shapes, sems, start, finish
        self.mid = mid


def _pcall(body, *, name, grid, in_specs, out_specs, out_shape, args, scratch_shapes=(), sem=None, vmem=None, comms=(),
           aliases=None):
    aliases = aliases or {}
    if not comms:
        return pl.pallas_call(body, name=name, grid=grid, in_specs=in_specs, out_specs=out_specs, out_shape=out_shape,
                              scratch_shapes=list(scratch_shapes), input_output_aliases=aliases,
                              compiler_params=_cp(sem, vmem))(*args)
    single = not isinstance(out_shape, (list, tuple))
    outs_shape = [out_shape] if single else list(out_shape)
    outs_spec = [out_specs] if single else list(out_specs)
    n_in, n_out, n_scr = len(args), len(outs_shape), len(scratch_shapes)
    c_in = [a for cm in comms for a in cm.arrays]
    c_out = [s for cm in comms for s in cm.out_shapes]
    c_scr = [s for cm in comms for s in cm.sems]
    total = int(np.prod(grid))
    late = total - 1 - max(1, total // 8) if total >= 8 else None

    def split(refs, counts):
        out, pos = [], 0
        for n in counts:
            out.append(refs[pos:pos + n])
            pos += n
        return out

    def wrapped(*refs):
        ins, cins, outs, couts, scr, cscr = split(refs, [n_in, len(c_in), n_out, len(c_out), n_scr, len(c_scr)])
        ids = [pl.program_id(a) for a in range(len(grid))]
        first = functools.reduce(jnp.logical_and, [i == 0 for i in ids])
        last = functools.reduce(jnp.logical_and, [i == g - 1 for i, g in zip(ids, grid)])
        per_in = split(cins, [len(cm.arrays) for cm in comms])
        per_out = split(couts, [len(cm.out_shapes) for cm in comms])
        per_sem = split(cscr, [len(cm.sems) for cm in comms])

        @pl.when(first)
        def _():
            for cm, a, b, s in zip(comms, per_in, per_out, per_sem):
                cm.start(a, b, s)

        if late is not None:
            step = functools.reduce(lambda acc, ig: acc * ig[1] + ig[0], zip(ids, grid), 0)

            @pl.when(step == late)
            def _():
                for cm, a, b, s in zip(comms, per_in, per_out, per_sem):
                    if cm.mid is not None:
                        cm.mid(a, b, s)

        body(*ins, *outs, *scr)

        @pl.when(last)
        def _():
            for cm, a, b, s in zip(comms, per_in, per_out, per_sem):
                if late is None and cm.mid is not None:
                    cm.mid(a, b, s)
                cm.finish(a, b, s)

    hbm = pl.BlockSpec(memory_space=pl.ANY)
    res = pl.pallas_call(
        wrapped, name=name, grid=grid, in_specs=list(in_specs) + [hbm] * len(c_in),
        out_specs=outs_spec + [hbm] * len(c_out), out_shape=outs_shape + c_out,
        scratch_shapes=list(scratch_shapes) + c_scr, input_output_aliases=aliases,
        compiler_params=_cp((ARB,) * len(grid), vmem))(*args, *c_in)
    main = res[0] if single else list(res[:n_out])
    return main, split(list(res[n_out:]), [len(cm.out_shapes) for cm in comms])


def _comm_allgather(w, axis):
    n = w.shape[axis]
    out_shape = list(w.shape)
    out_shape[axis] = NDEV * n

    def parts(ins, outs, sems):
        x_ref, out_ref = ins[0], outs[0]
        send_sems, recv_sems, local_sem = sems
        mx, my, mc = _me()
        chips = [(1 - mx, my), (mx, 1 - my), (1 - mx, 1 - my)]

        def blk(px, py, pc):
            start = pl.multiple_of((4 * px + 2 * py + pc) * n, n)
            if axis == 1:
                return out_ref.at[:, pl.ds(start, n), :]
            return out_ref.at[:, :, pl.ds(start, n)]

        def copy(k, block, to, src=None):
            return pltpu.make_async_remote_copy(
                src_ref=blk(*block) if src is None else src, dst_ref=blk(*block),
                send_sem=send_sems.at[k], recv_sem=recv_sems.at[k], device_id=to, device_id_type=MESH)

        me, sibling = (mx, my, mc), (mx, my, 1 - mc)
        mine = pltpu.make_async_copy(x_ref, blk(*me), local_sem)
        first = [copy(0, me, sibling, src=x_ref)] + [copy(1 + j, me, (*chip, mc), src=x_ref) for j, chip in enumerate(chips)]
        passed = [copy(4 + j, (*chip, mc), sibling) for j, chip in enumerate(chips)]
        return me, sibling, chips, mc, copy, mine, first, passed

    def start(ins, outs, sems):
        *_, mine, first, _ = parts(ins, outs, sems)
        mine.start()
        for cp in first:
            cp.start()

    def mid(ins, outs, sems):
        me, sibling, chips, mc, copy, mine, first, passed = parts(ins, outs, sems)
        for j, chip in enumerate(chips):
            copy(1 + j, (*chip, mc), me).wait_recv()
            passed[j].start()

    def finish(ins, outs, sems):
        me, sibling, chips, mc, copy, mine, first, passed = parts(ins, outs, sems)
        copy(0, sibling, me).wait_recv()
        for j, chip in enumerate(chips):
            copy(4 + j, (*chip, 1 - mc), me).wait_recv()
        for cp in first + passed:
            cp.wait_send()
        mine.wait()

    return _Comm([w], [jax.ShapeDtypeStruct(tuple(out_shape), w.dtype)],
                 [pltpu.SemaphoreType.DMA((7,)), pltpu.SemaphoreType.DMA((7,)), pltpu.SemaphoreType.DMA], start, finish, mid)


def _comm_rs_sibling(dwb):
    def copies(ins, outs, sems):
        mx, my, mc = _me()
        return [pltpu.make_async_remote_copy(
            src_ref=ins[0].at[2 * p + (1 - mc)], dst_ref=outs[0].at[p], send_sem=sems[0].at[p], recv_sem=sems[1].at[p],
            device_id=(mx, my, 1 - mc), device_id_type=MESH) for p in range(4)]

    def start(ins, outs, sems):
        for cp in copies(ins, outs, sems):
            cp.start()

    def finish(ins, outs, sems):
        cps = copies(ins, outs, sems)
        for cp in cps:
            cp.wait_recv()
        for cp in cps:
            cp.wait_send()

    return _Comm([dwb], [jax.ShapeDtypeStruct((4,) + dwb.shape[1:], dwb.dtype)],
                 [pltpu.SemaphoreType.DMA((4,)), pltpu.SemaphoreType.DMA((4,))], start, finish)


def _comm_rs_chips(part):
    def copies(ins, outs, sems):
        mx, my, mc = _me()
        chips = [(1 - mx, my), (mx, 1 - my), (1 - mx, 1 - my)]
        return [pltpu.make_async_remote_copy(
            src_ref=ins[0].at[2 * px + py], dst_ref=outs[0].at[k], send_sem=sems[0].at[k], recv_sem=sems[1].at[k],
            device_id=(px, py, mc), device_id_type=MESH) for k, (px, py) in enumerate(chips)]

    def start(ins, outs, sems):
        for cp in copies(ins, outs, sems):
            cp.start()

    def finish(ins, outs, sems):
        cps = copies(ins, outs, sems)
        for cp in cps:
            cp.wait_recv()
        for cp in cps:
            cp.wait_send()

    return _Comm([part], [jax.ShapeDtypeStruct((3,) + part.shape[1:], part.dtype)],
                 [pltpu.SemaphoreType.DMA((3,)), pltpu.SemaphoreType.DMA((3,))], start, finish)


def _ag_small(x, name):
    r, c = x.shape

    def body(x_ref, out_ref, send_sems, recv_sems):
        mx, my, mc = _me()
        mine = 4 * mx + 2 * my + mc
        out_ref[mine] = x_ref[...]
        copies = []
        for k in range(1, NDEV):
            px = 1 - mx if (k >> 2) & 1 else mx
            py = 1 - my if (k >> 1) & 1 else my
            pc = 1 - mc if k & 1 else mc
            cp = pltpu.make_async_remote_copy(
                src_ref=x_ref, dst_ref=out_ref.at[mine], send_sem=send_sems.at[k - 1], recv_sem=recv_sems.at[k - 1],
                device_id=(px, py, pc), device_id_type=MESH)
            cp.start()
            copies.append((cp, 4 * px + 2 * py + pc))
        for k, (cp, peer) in enumerate(copies):
            pltpu.make_async_remote_copy(
                src_ref=x_ref, dst_ref=out_ref.at[peer], send_sem=send_sems.at[k], recv_sem=recv_sems.at[k],
                device_id=(mx, my, mc), device_id_type=MESH).wait_recv()
        for cp, _ in copies:
            cp.wait_send()

    return pl.pallas_call(
        body, name=name,
        out_shape=jax.ShapeDtypeStruct((NDEV, r, c), x.dtype),
        in_specs=[pl.BlockSpec(memory_space=pltpu.VMEM)],
        out_specs=pl.BlockSpec(memory_space=pltpu.VMEM),
        scratch_shapes=[pltpu.SemaphoreType.DMA((NDEV - 1,)), pltpu.SemaphoreType.DMA((NDEV - 1,))],
    )(x)


def _ag_big(w, axis, name):
    n = w.shape[axis]
    out_shape = list(w.shape)
    out_shape[axis] = NDEV * n

    def body(x_ref, out_ref, send_sems, recv_sems, local_sem):
        mx, my, mc = _me()
        me, sibling = (mx, my, mc), (mx, my, 1 - mc)
        chips = [(1 - mx, my), (mx, 1 - my), (1 - mx, 1 - my)]

        def blk(px, py, pc):
            start = pl.multiple_of((4 * px + 2 * py + pc) * n, n)
            if axis == 1:
                return out_ref.at[:, pl.ds(start, n), :]
            return out_ref.at[:, :, pl.ds(start, n)]

        def copy(k, block, to, src=None):
            return pltpu.make_async_remote_copy(
                src_ref=blk(*block) if src is None else src, dst_ref=blk(*block),
                send_sem=send_sems.at[k], recv_sem=recv_sems.at[k], device_id=to, device_id_type=MESH)

        mine = pltpu.make_async_copy(x_ref, blk(*me), local_sem)
        mine.start()
        first = [copy(0, me, sibling, src=x_ref)]
        first += [copy(1 + j, me, (*chip, mc), src=x_ref) for j, chip in enumerate(chips)]
        for cp in first:
            cp.start()
        passed = [copy(4 + j, (*chip, mc), sibling) for j, chip in enumerate(chips)]
        for j, chip in enumerate(chips):
            copy(1 + j, (*chip, mc), me).wait_recv()
            passed[j].start()
        copy(0, sibling, me).wait_recv()
        for j, chip in enumerate(chips):
            copy(4 + j, (*chip, 1 - mc), me).wait_recv()
        for cp in first + passed:
            cp.wait_send()
        mine.wait()

    return pl.pallas_call(
        body, name=name,
        out_shape=jax.ShapeDtypeStruct(tuple(out_shape), w.dtype),
        in_specs=[pl.BlockSpec(memory_space=pl.ANY)],
        out_specs=pl.BlockSpec(memory_space=pl.ANY),
        scratch_shapes=[pltpu.SemaphoreType.DMA((7,)), pltpu.SemaphoreType.DMA((7,)), pltpu.SemaphoreType.DMA],
    )(w)


def _chip_partial(dwb, r1, core, name):
    _, A, B = dwb.shape
    ta = _tile(A, 512, 16)

    def body(c_ref, a_ref, b_ref, o_ref):
        o_ref[...] = (a_ref[...].astype(f32) + b_ref[...].astype(f32)).astype(o_ref.dtype)

    grid_spec = pltpu.PrefetchScalarGridSpec(
        num_scalar_prefetch=1, grid=(4, A // ta),
        in_specs=[pl.BlockSpec((None, ta, B), lambda p, i, c: (2 * p + c[0], i, 0)),
                  pl.BlockSpec((None, ta, B), lambda p, i, c: (p, i, 0))],
        out_specs=pl.BlockSpec((None, ta, B), lambda p, i, c: (p, i, 0)))
    return pl.pallas_call(body, name=name, grid_spec=grid_spec,
                          out_shape=jax.ShapeDtypeStruct((4, A, B), dwb.dtype),
                          compiler_params=_cp((PAR, PAR)))(core, dwb, r1)


def _adam_math(w, g, m, v):
    m2 = B1 * m + (1.0 - B1) * g
    v2 = B2 * v + (1.0 - B2) * (g * g)
    m_hat = m2 / (1.0 - B1 ** STEP)
    v_hat = v2 / (1.0 - B2 ** STEP)
    delta = -LR * (m_hat / (jnp.sqrt(v_hat) + AEPS) + WD * w)
    return delta, m2, v2


def _adamw_reduced(w, m, v, mine, r2, l, prev, name, comms=()):
    L, A, B = w.shape
    ta = _tile(A, 256, 8)

    def body(w_ref, m_ref, v_ref, p_ref, r_ref, *rest):
        g_out, d_out, m_out, v_out = rest[-4:]
        g = ((p_ref[...].astype(f32) + r_ref[0].astype(f32)) + r_ref[1].astype(f32)) + r_ref[2].astype(f32)
        d, m2, v2 = _adam_math(w_ref[...], g, m_ref[...], v_ref[...])
        g_out[...] = g
        d_out[...] = d
        m_out[...] = m2
        v_out[...] = v2

    wspec = pl.BlockSpec((None, ta, B), lambda i: (l, i, 0))
    in_specs = [wspec, wspec, wspec, pl.BlockSpec((ta, B), lambda i: (i, 0)), pl.BlockSpec((3, ta, B), lambda i: (0, i, 0))]
    args = [w, m, v, mine, r2]
    aliases = {}
    if prev is not None:
        in_specs += [pl.BlockSpec(memory_space=pl.ANY)] * 4
        args += list(prev)
        aliases = {5 + i: i for i in range(4)}
    shp = jax.ShapeDtypeStruct((L, A, B), f32)
    return _pcall(body, name=name, grid=(A // ta,), in_specs=in_specs, out_specs=[wspec] * 4, out_shape=[shp] * 4,
                  args=args, sem=(PAR,), comms=comms, aliases=aliases)


def _adamw_plain(w, m, v, g, name):
    A, B = w.shape
    ta = _tile(A, 256, 8)

    def body(w_ref, m_ref, v_ref, g_ref, d_out, m_out, v_out):
        d, m2, v2 = _adam_math(w_ref[...], g_ref[...], m_ref[...], v_ref[...])
        d_out[...] = d
        m_out[...] = m2
        v_out[...] = v2

    spec = pl.BlockSpec((ta, B), lambda i: (i, 0))
    shp = jax.ShapeDtypeStruct((A, B), f32)
    return pl.pallas_call(body, name=name, grid=(A // ta,), in_specs=[spec] * 4, out_specs=[spec] * 3,
                          out_shape=[shp, shp, shp], compiler_params=_cp((PAR,)))(w, m, v, g)


def _sum8(g, name):
    _, R, C = g.shape

    def body(g_ref, o_ref):
        acc = g_ref[0]
        for j in range(1, NDEV):
            acc = acc + g_ref[j]
        o_ref[...] = acc

    return pl.pallas_call(body, name=name, out_shape=jax.ShapeDtypeStruct((R, C), f32))(g)


def _modproj(c_all, w, bias, name):
    K, N = w.shape
    tn = _tile(N, 512)

    def body(c_ref, w_ref, b_ref, o_ref):
        cc = c_ref[...]
        sc = (cc * _sigmoid(cc)).astype(bf16)
        o_ref[...] = jnp.dot(sc, w_ref[...].astype(bf16), preferred_element_type=f32) + b_ref[...]

    return pl.pallas_call(
        body, name=name, grid=(N // tn,),
        in_specs=[pl.BlockSpec((NDEV, K), lambda j: (0, 0)), pl.BlockSpec((K, tn), lambda j: (0, j)),
                  pl.BlockSpec((1, tn), lambda j: (0, j))],
        out_specs=pl.BlockSpec((NDEV, tn), lambda j: (0, j)),
        out_shape=jax.ShapeDtypeStruct((NDEV, N), f32), compiler_params=_cp((PAR,)))(c_all, w, bias)


def _modgrad(c_all_t, dm, name):
    K = c_all_t.shape[0]
    N = dm.shape[1]
    tn = _tile(N, 512)

    def body(c_ref, d_ref, o_ref):
        cc = c_ref[...]
        sc = cc * _sigmoid(cc)
        dmv = d_ref[...]
        acc = sc[:, 0:1] * dmv[0:1, :]
        for b in range(1, NDEV):
            acc = acc + sc[:, b:b + 1] * dmv[b:b + 1, :]
        o_ref[...] = acc

    return pl.pallas_call(
        body, name=name, grid=(N // tn,),
        in_specs=[pl.BlockSpec((K, NDEV), lambda j: (0, 0)), pl.BlockSpec((NDEV, tn), lambda j: (0, j))],
        out_specs=pl.BlockSpec((K, tn), lambda j: (0, j)),
        out_shape=jax.ShapeDtypeStruct((K, N), f32), compiler_params=_cp((PAR,)))(c_all_t, dm)


def _mm_nn(a, w, l, *, name, out_dtype=bf16, bias=None, res=None, gate=None, tm=1024, tn=1024, tk=2048, comms=()):
    M, K = a.shape
    N = w.shape[2]
    tm, tn, tk = _tile(M, tm, 8), _tile(N, tn), _tile(K, tk)
    nk = K // tk
    epi = res is not None

    def body(*refs):
        it = iter(refs)
        a_ref, w_ref = next(it), next(it)
        b_ref = next(it) if bias is not None else None
        r_ref = next(it) if epi else None
        g_ref = next(it) if epi else None
        o_ref = next(it)
        f_ref = next(it) if epi else None

        def finish(y):
            if b_ref is not None:
                y = y + b_ref[...]
            if epi:
                f_ref[...] = y.astype(f_ref.dtype)
                o_ref[...] = r_ref[...] + g_ref[...] * y
            else:
                o_ref[...] = y.astype(o_ref.dtype)

        if nk == 1:
            finish(jnp.dot(a_ref[...], w_ref[...], preferred_element_type=f32))
            return
        acc = next(it)
        k = pl.program_id(2)

        @pl.when(k == 0)
        def _():
            acc[...] = jnp.zeros_like(acc)

        acc[...] += jnp.dot(a_ref[...], w_ref[...], preferred_element_type=f32)

        @pl.when(k == nk - 1)
        def _():
            finish(acc[...])

    in_specs = [pl.BlockSpec((tm, tk), lambda i, j, k: (i, k)), pl.BlockSpec((None, tk, tn), lambda i, j, k: (l, k, j))]
    args = [a, w]
    if bias is not None:
        in_specs.append(pl.BlockSpec((1, tn), lambda i, j, k: (0, j)))
        args.append(bias)
    ospec = pl.BlockSpec((tm, tn), lambda i, j, k: (i, j))
    if epi:
        in_specs += [ospec, pl.BlockSpec((1, tn), lambda i, j, k: (0, j))]
        args += [res, gate]
        out_shape = [jax.ShapeDtypeStruct((M, N), f32), jax.ShapeDtypeStruct((M, N), bf16)]
        out_specs = [ospec, ospec]
    else:
        out_shape = jax.ShapeDtypeStruct((M, N), out_dtype)
        out_specs = ospec
    return _pcall(body, name=name, grid=(M // tm, N // tn, nk), in_specs=in_specs, out_specs=out_specs, out_shape=out_shape,
                  args=args, scratch_shapes=[pltpu.VMEM((tm, tn), f32)] if nk > 1 else [], sem=(PAR, PAR, ARB), vmem=VMEM_BIG,
                  comms=comms)


def _mm_nt(a, w, l, *, name, out_dtype, tm=1024, tko=2048, tn=1024, comms=()):
    planes = a.ndim == 3
    M = a.shape[-2]
    K, N = w.shape[1], w.shape[2]
    npl = a.shape[-1]
    tm, tko = _tile(M, tm, 8), _tile(K, tko)
    tn = _tile(npl, tn)
    nn = N // tn
    per_plane = npl // tn

    def body(a_ref, w_ref, o_ref, *scratch):
        if nn == 1:
            o_ref[...] = _dot_nt(a_ref[...], w_ref[...]).astype(o_ref.dtype)
            return
        acc = scratch[0]
        k = pl.program_id(2)

        @pl.when(k == 0)
        def _():
            acc[...] = jnp.zeros_like(acc)

        acc[...] += _dot_nt(a_ref[...], w_ref[...])

        @pl.when(k == nn - 1)
        def _():
            o_ref[...] = acc[...].astype(o_ref.dtype)

    if planes:
        a_spec = pl.BlockSpec((None, tm, tn), lambda i, j, k: (k // per_plane, i, k % per_plane))
    else:
        a_spec = pl.BlockSpec((tm, tn), lambda i, j, k: (i, k))
    return _pcall(body, name=name, grid=(M // tm, K // tko, nn),
                  in_specs=[a_spec, pl.BlockSpec((None, tko, tn), lambda i, j, k: (l, j, k))],
                  out_specs=pl.BlockSpec((tm, tko), lambda i, j, k: (i, j)),
                  out_shape=jax.ShapeDtypeStruct((M, K), out_dtype), args=[a, w],
                  scratch_shapes=[pltpu.VMEM((tm, tko), f32)] if nn > 1 else [], sem=(PAR, PAR, ARB), vmem=VMEM_BIG,
                  comms=comms)


def _mm_tn(a, b, *, name, col_sharded, comms=()):
    planes = b.ndim == 3
    S, K = a.shape
    N = b.shape[-1] * (2 if planes else 1)
    if col_sharded:
        tn, tk, ts = N // NDEV, _tile(K, 1024), _tile(S, 2048, 16)
    else:
        tn, tk, ts = N, _tile(K, 1408), _tile(S, 1024, 16)
    ns_steps = S // ts
    per_plane = (b.shape[-1] // tn) if planes else 0

    def body(a_ref, b_ref, o_ref, acc):
        s = pl.program_id(2)

        @pl.when(s == 0)
        def _():
            acc[...] = jnp.zeros_like(acc)

        acc[...] += lax.dot_general(a_ref[...], b_ref[...], (((0,), (0,)), ((), ())), preferred_element_type=f32)

        @pl.when(s == ns_steps - 1)
        def _():
            o_ref[...] = acc[...].astype(o_ref.dtype)

    if planes:
        b_spec = pl.BlockSpec((None, ts, tn), lambda k, n, s: (n // per_plane, s, n % per_plane))
    else:
        b_spec = pl.BlockSpec((ts, tn), lambda k, n, s: (s, n))
    if col_sharded:
        out_shape = jax.ShapeDtypeStruct((NDEV, K, tn), bf16)
        out_spec = pl.BlockSpec((None, tk, tn), lambda k, n, s: (n, k, 0))
    else:
        out_shape = jax.ShapeDtypeStruct((K, N), bf16)
        out_spec = pl.BlockSpec((tk, tn), lambda k, n, s: (k, n))
    res = _pcall(body, name=name, grid=(K // tk, N // tn, ns_steps),
                 in_specs=[pl.BlockSpec((ts, tk), lambda k, n, s: (s, k)), b_spec],
                 out_specs=out_spec, out_shape=out_shape, args=[a, b],
                 scratch_shapes=[pltpu.VMEM((tk, tn), f32)], sem=(PAR, PAR, ARB), vmem=VMEM_BIG, comms=comms)
    out, couts = res if comms else (res, None)
    if not col_sharded:
        out = out.reshape(NDEV, K // NDEV, N)
    return (out, couts) if comms else out


def _acc_spec(w, rows=1):
    return pl.BlockSpec((rows, w), lambda i: (0, 0))


def _mod_fwd(x, g, sh, sc, name, comms=()):
    S, W = x.shape
    tm = _tile(S, 256, 8)

    def body(x_ref, g_ref, sh_ref, sc_ref, h_ref):
        xv = x_ref[...]
        r = lax.rsqrt(jnp.mean(xv * xv, axis=-1, keepdims=True) + EPS)
        h_ref[...] = ((xv * r) * g_ref[...] * (1.0 + sc_ref[...]) + sh_ref[...]).astype(h_ref.dtype)

    row = pl.BlockSpec((tm, W), lambda i: (i, 0))
    return _pcall(body, name=name, grid=(S // tm,), in_specs=[row, _acc_spec(W), _acc_spec(W), _acc_spec(W)],
                  out_specs=row, out_shape=jax.ShapeDtypeStruct((S, W), bf16), args=[x, g, sh, sc], sem=(PAR,), comms=comms)


def _mod_bwd(dh, x, dx_in, g, sc, name, comms=()):
    S, W = x.shape
    tm = _tile(S, 256, 8)
    nt = S // tm

    def body(dh_ref, x_ref, dxi_ref, g_ref, sc_ref, dx_ref, dsh_ref, dsc_ref, dg_ref):
        i = pl.program_id(0)

        @pl.when(i == 0)
        def _():
            dsh_ref[...] = jnp.zeros_like(dsh_ref)
            dsc_ref[...] = jnp.zeros_like(dsc_ref)

        xv = x_ref[...]
        dh = dh_ref[...].astype(f32)
        r = lax.rsqrt(jnp.mean(xv * xv, axis=-1, keepdims=True) + EPS)
        n = xv * r
        dn = dh * (g_ref[...] * (1.0 + sc_ref[...]))
        dx = r * (dn - n * jnp.mean(dn * n, axis=-1, keepdims=True))
        dx_ref[...] = dxi_ref[...] + dx
        dsh_ref[...] += jnp.sum(dh, axis=0, keepdims=True)
        dsc_ref[...] += jnp.sum(dh * n, axis=0, keepdims=True)

        @pl.when(i == nt - 1)
        def _():
            a2 = dsc_ref[...]
            dg_ref[...] = a2 * (1.0 + sc_ref[...])
            dsc_ref[...] = a2 * g_ref[...]

    row = pl.BlockSpec((tm, W), lambda i: (i, 0))
    vec = jax.ShapeDtypeStruct((1, W), f32)
    return _pcall(body, name=name, grid=(nt,), in_specs=[row, row, row, _acc_spec(W), _acc_spec(W)],
                  out_specs=[row, _acc_spec(W), _acc_spec(W), _acc_spec(W)],
                  out_shape=[jax.ShapeDtypeStruct((S, W), f32), vec, vec, vec], args=[dh, x, dx_in, g, sc], sem=(ARB,),
                  comms=comms)


def _gate_bwd(dx, f, gate, name):
    S, W = dx.shape
    tm = _tile(S, 256, 16)

    def body(dx_ref, f_ref, g_ref, df_ref, dg_ref, sdf_ref):
        i = pl.program_id(0)

        @pl.when(i == 0)
        def _():
            dg_ref[...] = jnp.zeros_like(dg_ref)
            sdf_ref[...] = jnp.zeros_like(sdf_ref)

        d = dx_ref[...]
        df = g_ref[...] * d
        df_ref[...] = df.astype(df_ref.dtype)
        dg_ref[...] += jnp.sum(d * f_ref[...].astype(f32), axis=0, keepdims=True)
        sdf_ref[...] += jnp.sum(df, axis=0, keepdims=True)

    row = pl.BlockSpec((tm, W), lambda i: (i, 0))
    vec = jax.ShapeDtypeStruct((1, W), f32)
    return pl.pallas_call(
        body, name=name, grid=(S // tm,), in_specs=[row, row, _acc_spec(W)], out_specs=[row, _acc_spec(W), _acc_spec(W)],
        out_shape=[jax.ShapeDtypeStruct((S, W), bf16), vec, vec], compiler_params=_cp((ARB,)))(dx, f, gate)


def _loss_grad(y, target, name):
    S, W = y.shape
    tm = _tile(S, 256, 8)

    def body(y_ref, t_ref, dy_ref, l_ref):
        i = pl.program_id(0)

        @pl.when(i == 0)
        def _():
            l_ref[...] = jnp.zeros_like(l_ref)

        e = y_ref[...] - t_ref[...]
        dy_ref[...] = e * (1.0 / W)
        l_ref[...] += 0.5 * jnp.sum(jnp.mean(e * e, axis=-1, keepdims=True))

    row = pl.BlockSpec((tm, W), lambda i: (i, 0))
    return pl.pallas_call(
        body, name=name, grid=(S // tm,), in_specs=[row, row], out_specs=[row, pl.BlockSpec((8, 128), lambda i: (0, 0))],
        out_shape=[jax.ShapeDtypeStruct((S, W), f32), jax.ShapeDtypeStruct((8, 128), f32)],
        compiler_params=_cp((ARB,)))(y, target)


def _tap_groups(offsets):
    groups = {}
    for k, o in enumerate(offsets):
        groups.setdefault(o % 8, []).append((k, o - o % 8))
    return sorted(groups.items())


def _tap_sum(buf, w, offsets, tm):
    out = None
    for b, taps in _tap_groups(offsets):
        n = tm + 8 if b else tm
        y = None
        for k, base in taps:
            term = w[k:k + 1, :] * buf[pl.ds(base, n), :]
            y = term if y is None else y + term
        part = y[b:b + tm] if b else y
        out = part if out is None else out + part
    return out


def _tap_wgrad(d, buf, dsh, acc_ref, offsets, tm):
    for b, taps in _tap_groups(offsets):
        if b:
            dsh[pl.ds(0, 8), :] = jnp.zeros((8, dsh.shape[1]), f32)
            dsh[pl.ds(tm, 8), :] = jnp.zeros((8, dsh.shape[1]), f32)
            dsh[pl.ds(b, tm), :] = d
            dd, n = dsh[...], tm + 8
        else:
            dd, n = d, tm
        for k, base in taps:
            acc_ref[pl.ds(k, 1), :] += jnp.sum(dd * buf[pl.ds(base, n), :], axis=0, keepdims=True)


_CONV_OFFSETS = [HALO - (CONV_K - 1) + k for k in range(CONV_K)]
_CONV_OFFSETS_T = [CONV_K - 1 - k for k in range(CONV_K)]


def _conv_core(u_ref, uh_ref, w_ref, b_ref, lg_ref, lb_ref, gbuf, tm, first, cv_ref=None):
    C = u_ref.shape[1] // 2
    u = u_ref[...].astype(f32)
    uh = uh_ref[...].astype(f32)
    gbuf[pl.ds(HALO, tm), :] = u[:, :C] * _sigmoid(u[:, C:])
    halo = uh[:, :C] * _sigmoid(uh[:, C:])
    gbuf[pl.ds(0, HALO), :] = jnp.where(first, 0.0, halo)
    cv = _tap_sum(gbuf, w_ref[...], _CONV_OFFSETS, tm) + b_ref[...] if cv_ref is None else cv_ref[...]
    mu = jnp.mean(cv, axis=-1, keepdims=True)
    xc = cv - mu
    rstd = lax.rsqrt(jnp.mean(xc * xc, axis=-1, keepdims=True) + EPS)
    z = xc * rstd
    ln = z * lg_ref[...] + lb_ref[...]
    return cv, z, rstd, ln


def _halo_prev(tm, hb, w):
    return pl.BlockSpec((hb, w), lambda i: (jnp.maximum(i * (tm // hb) - 1, 0), 0))


def _conv_fwd(u, w, b, lg, lb, name, comms=()):
    S, C2 = u.shape
    C = C2 // 2
    tm = _tile(S, 256, HALO)

    def body(u_ref, uh_ref, w_ref, b_ref, lg_ref, lb_ref, s_ref, cv_ref, gbuf):
        first = pl.program_id(0) == 0
        cv, _, _, ln = _conv_core(u_ref, uh_ref, w_ref, b_ref, lg_ref, lb_ref, gbuf, tm, first)
        s_ref[...] = (ln * _sigmoid(ln)).astype(s_ref.dtype)
        cv_ref[...] = cv

    return _pcall(body, name=name, grid=(S // tm,),
                  in_specs=[pl.BlockSpec((tm, C2), lambda i: (i, 0)), _halo_prev(tm, HALO, C2), _acc_spec(C, 32),
                            _acc_spec(C), _acc_spec(C), _acc_spec(C)],
                  out_specs=[pl.BlockSpec((tm, C), lambda i: (i, 0))] * 2,
                  out_shape=[jax.ShapeDtypeStruct((S, C), bf16), jax.ShapeDtypeStruct((S, C), f32)],
                  args=[u, u, w, b, lg, lb], scratch_shapes=[pltpu.VMEM((tm + HALO, C), f32)], sem=(PAR,), vmem=VMEM_BIG,
                  comms=comms)


def _conv_bwd1(u, cv, ds, w, b, lg, lb, name, comms=()):
    S, C2 = u.shape
    C = C2 // 2
    tm = _tile(S, 256, HALO)

    def body(u_ref, uh_ref, cv_ref, ds_ref, w_ref, b_ref, lg_ref, lb_ref, dcv_ref, dlg_ref, dlb_ref, ddb_ref, ddw_ref, gbuf, dsh):
        i = pl.program_id(0)

        @pl.when(i == 0)
        def _():
            dlg_ref[...] = jnp.zeros_like(dlg_ref)
            dlb_ref[...] = jnp.zeros_like(dlb_ref)
            ddb_ref[...] = jnp.zeros_like(ddb_ref)
            ddw_ref[...] = jnp.zeros_like(ddw_ref)

        _, z, rstd, ln = _conv_core(u_ref, uh_ref, w_ref, b_ref, lg_ref, lb_ref, gbuf, tm, i == 0, cv_ref)
        sg = _sigmoid(ln)
        dln = ds_ref[...].astype(f32) * (sg * (1.0 + ln * (1.0 - sg)))
        dlg_ref[...] += jnp.sum(dln * z, axis=0, keepdims=True)
        dlb_ref[...] += jnp.sum(dln, axis=0, keepdims=True)
        dz = dln * lg_ref[...]
        dcv = rstd * (dz - jnp.mean(dz, axis=-1, keepdims=True) - z * jnp.mean(dz * z, axis=-1, keepdims=True))
        dcv_ref[...] = dcv
        ddb_ref[...] += jnp.sum(dcv, axis=0, keepdims=True)
        _tap_wgrad(dcv, gbuf, dsh, ddw_ref, _CONV_OFFSETS, tm)

    vec = jax.ShapeDtypeStruct((1, C), f32)
    return _pcall(
        body, name=name, grid=(S // tm,),
        in_specs=[pl.BlockSpec((tm, C2), lambda i: (i, 0)), _halo_prev(tm, HALO, C2), pl.BlockSpec((tm, C), lambda i: (i, 0)),
                  pl.BlockSpec((tm, C), lambda i: (i, 0)), _acc_spec(C, 32), _acc_spec(C), _acc_spec(C), _acc_spec(C)],
        out_specs=[pl.BlockSpec((tm, C), lambda i: (i, 0)), _acc_spec(C), _acc_spec(C), _acc_spec(C), _acc_spec(C, 32)],
        out_shape=[jax.ShapeDtypeStruct((S, C), f32), vec, vec, vec, jax.ShapeDtypeStruct((32, C), f32)],
        args=[u, u, cv, ds, w, b, lg, lb], scratch_shapes=[pltpu.VMEM((tm + HALO, C), f32), pltpu.VMEM((tm + 8, C), f32)],
        sem=(ARB,), vmem=VMEM_BIG, comms=comms)


def _conv_bwd2(dcv, u, w, name, comms=()):
    S, C2 = u.shape
    C = C2 // 2
    tm = _tile(S, 256, HALO)
    nt = S // tm
    nhb = S // HALO

    def body(d_ref, dn_ref, u_ref, w_ref, du_ref, db_ref, dbuf):
        i = pl.program_id(0)

        @pl.when(i == 0)
        def _():
            db_ref[...] = jnp.zeros_like(db_ref)

        dbuf[pl.ds(0, tm), :] = d_ref[...]
        dbuf[pl.ds(tm, HALO), :] = jnp.where(i == nt - 1, 0.0, dn_ref[...])
        dglu = _tap_sum(dbuf, w_ref[...], _CONV_OFFSETS_T, tm)
        u = u_ref[...].astype(f32)
        a, gt = u[:, :C], u[:, C:]
        sg = _sigmoid(gt)
        da = dglu * sg
        dgt = dglu * a * sg * (1.0 - sg)
        du_ref[:, :C] = da.astype(du_ref.dtype)
        du_ref[:, C:] = dgt.astype(du_ref.dtype)
        db_ref[:, :C] += jnp.sum(da, axis=0, keepdims=True)
        db_ref[:, C:] += jnp.sum(dgt, axis=0, keepdims=True)

    return _pcall(
        body, name=name, grid=(nt,),
        in_specs=[pl.BlockSpec((tm, C), lambda i: (i, 0)),
                  pl.BlockSpec((HALO, C), lambda i: (jnp.minimum((i + 1) * (tm // HALO), nhb - 1), 0)),
                  pl.BlockSpec((tm, C2), lambda i: (i, 0)), _acc_spec(C, 32)],
        out_specs=[pl.BlockSpec((tm, C2), lambda i: (i, 0)), _acc_spec(C2)],
        out_shape=[jax.ShapeDtypeStruct((S, C2), bf16), jax.ShapeDtypeStruct((1, C2), f32)],
        args=[dcv, dcv, u, w], scratch_shapes=[pltpu.VMEM((tm + HALO, C), f32)], sem=(ARB,), vmem=VMEM_BIG, comms=comms)


def _up_gate(h, w_up, w, b, name, comms=()):
    S, K = h.shape
    F = w.shape[1]
    tm = _tile(S, 1024, 16)
    tn = _tile(F, 512)
    nf = F // tn
    ch = _tile(tn, 256)

    def body(h_ref, wg_ref, wv_ref, w_ref, b_ref, u_ref, a_ref, tail):
        i, j = pl.program_id(0), pl.program_id(1)

        @pl.when(i == 0)
        def _():
            tail[j] = jnp.zeros((8, tn), f32)

        hv = h_ref[...]
        for c in range(tn // ch):
            cs = slice(c * ch, (c + 1) * ch)
            g16 = jnp.dot(hv, wg_ref[:, cs], preferred_element_type=f32).astype(bf16)
            v16 = jnp.dot(hv, wv_ref[:, cs], preferred_element_type=f32).astype(bf16)
            u_ref[0, :, cs] = g16
            u_ref[1, :, cs] = v16
            g = g16.astype(f32)
            ext = jnp.concatenate([tail[j, :, cs], g], axis=0)
            gc = b_ref[:, cs] + w_ref[0:1, cs] * ext[6:6 + tm] + w_ref[1:2, cs] * ext[7:7 + tm] + w_ref[2:3, cs] * g
            a_ref[:, cs] = (gc * _sigmoid(gc) * v16.astype(f32)).astype(a_ref.dtype)
            tail[j, :, cs] = g[tm - 8:tm]

    return _pcall(
        body, name=name, grid=(S // tm, nf),
        in_specs=[pl.BlockSpec((tm, K), lambda i, j: (i, 0)),
                  pl.BlockSpec((None, K, tn), lambda i, j: (0, 0, j)), pl.BlockSpec((None, K, tn), lambda i, j: (0, 0, nf + j)),
                  pl.BlockSpec((8, tn), lambda i, j: (0, j)), pl.BlockSpec((1, tn), lambda i, j: (0, j))],
        out_specs=[pl.BlockSpec((2, tm, tn), lambda i, j: (0, i, j)), pl.BlockSpec((tm, tn), lambda i, j: (i, j))],
        out_shape=[jax.ShapeDtypeStruct((2, S, F), bf16), jax.ShapeDtypeStruct((S, F), bf16)],
        args=[h, w_up, w_up, w, b], scratch_shapes=[pltpu.VMEM((nf, 8, tn), f32)], sem=(ARB, ARB), vmem=VMEM_BIG, comms=comms)


def _ffn_gate_bwd(u2, dact, w, b, name, comms=()):
    _, S, F = u2.shape
    cw = _tile(F, 1408)
    ncw = F // cw
    tm = _tile(S, 256, FHALO)
    nt = S // tm
    nhb = S // FHALO
    R = tm + 2 * FHALO

    def body(g_ref, gp_ref, gn_ref, v_ref, vn_ref, d_ref, dn_ref, w_ref, b_ref, du_ref, dw_ref, db_ref, gbuf, dbuf):
        i = pl.program_id(1)
        first, last = i == 0, i == nt - 1

        @pl.when(i == 0)
        def _():
            dw_ref[...] = jnp.zeros_like(dw_ref)
            db_ref[...] = jnp.zeros_like(db_ref)

        gbuf[pl.ds(0, FHALO), :] = jnp.where(first, 0.0, gp_ref[...].astype(f32))
        gbuf[pl.ds(FHALO, tm), :] = g_ref[...].astype(f32)
        gbuf[pl.ds(FHALO + tm, FHALO), :] = gn_ref[...].astype(f32)
        w = w_ref[...]
        n_ext = tm + FHALO
        gc = jnp.zeros((n_ext, cw), f32) + b_ref[...]
        for k in range(FFN_K):
            gc = gc + w[k:k + 1, :] * gbuf[pl.ds(FHALO - (FFN_K - 1) + k, n_ext), :]
        sg = _sigmoid(gc)
        val = jnp.concatenate([v_ref[...].astype(f32), vn_ref[...].astype(f32)], axis=0)
        dact_ext = jnp.concatenate([d_ref[...].astype(f32), jnp.where(last, 0.0, dn_ref[...].astype(f32))], axis=0)
        dgc = dact_ext * val * (sg * (1.0 + gc * (1.0 - sg)))
        dbuf[...] = dgc
        dval = dact_ext[:tm] * (gc[:tm] * sg[:tm])
        dgt = jnp.zeros((tm, cw), f32)
        for k in range(FFN_K):
            dgt = dgt + w[k:k + 1, :] * dbuf[pl.ds(FFN_K - 1 - k, tm), :]
        du_ref[0] = dgt.astype(du_ref.dtype)
        du_ref[1] = dval.astype(du_ref.dtype)
        dgc_t = dgc[:tm]
        db_ref[...] += jnp.sum(dgc_t, axis=0, keepdims=True)
        for k in range(FFN_K):
            dw_ref[pl.ds(k, 1), :] += jnp.sum(dgc_t * gbuf[pl.ds(FHALO - (FFN_K - 1) + k, tm), :], axis=0, keepdims=True)

    hb = tm // FHALO
    prev = lambda j, i: (jnp.maximum(i * hb - 1, 0), j)
    nxt = lambda j, i: (jnp.minimum((i + 1) * hb, nhb - 1), j)
    plane = lambda p, f: (lambda j, i: (p,) + f(j, i))
    return _pcall(
        body, name=name, grid=(ncw, nt), comms=comms, sem=(PAR, ARB), vmem=VMEM_BIG,
        args=[u2, u2, u2, u2, u2, dact, dact, w, b],
        in_specs=[pl.BlockSpec((None, tm, cw), lambda j, i: (0, i, j)), pl.BlockSpec((None, FHALO, cw), plane(0, prev)),
                  pl.BlockSpec((None, FHALO, cw), plane(0, nxt)),
                  pl.BlockSpec((None, tm, cw), lambda j, i: (1, i, j)), pl.BlockSpec((None, FHALO, cw), plane(1, nxt)),
                  pl.BlockSpec((tm, cw), lambda j, i: (i, j)), pl.BlockSpec((FHALO, cw), nxt),
                  pl.BlockSpec((8, cw), lambda j, i: (0, j)), pl.BlockSpec((1, cw), lambda j, i: (0, j))],
        out_specs=[pl.BlockSpec((2, tm, cw), lambda j, i: (0, i, j)), pl.BlockSpec((8, cw), lambda j, i: (0, j)),
                   pl.BlockSpec((1, cw), lambda j, i: (0, j))],
        out_shape=[jax.ShapeDtypeStruct((2, S, F), bf16), jax.ShapeDtypeStruct((8, F), f32), jax.ShapeDtypeStruct((1, F), f32)],
        scratch_shapes=[pltpu.VMEM((R, cw), f32), pltpu.VMEM((tm + FHALO, cw), f32)])


def _rope_tables(pos_col, name):
    S = pos_col.shape[0]
    tm = _tile(S, 512, 8)
    half = ROT // 2
    inv = THETA ** (-np.arange(0, ROT, 2, dtype=np.float32) / ROT)
    lane_freq = np.zeros((1, DH), np.float32)
    lane_freq[0, :half] = inv
    lane_freq[0, half:ROT] = inv
    lane_freq = jnp.asarray(lane_freq)

    def body(p_ref, fr_ref, c_ref, sa_ref, sb_ref):
        ang = p_ref[...].astype(f32) * fr_ref[...]
        lane = lax.broadcasted_iota(jnp.int32, (tm, DH), 1)
        cs, sn = jnp.cos(ang), jnp.sin(ang)
        c_ref[...] = jnp.where(lane < ROT, cs, 1.0)
        sa_ref[...] = jnp.where(lane < half, -sn, 0.0)
        sb_ref[...] = jnp.where((lane >= half) & (lane < ROT), sn, 0.0)

    row = pl.BlockSpec((tm, DH), lambda i: (i, 0))
    shp = jax.ShapeDtypeStruct((S, DH), f32)
    return pl.pallas_call(body, name=name, grid=(S // tm,),
                          in_specs=[pl.BlockSpec((tm, 1), lambda i: (i, 0)), pl.BlockSpec((1, DH), lambda i: (0, 0))],
                          out_specs=[row, row, row], out_shape=[shp, shp, shp], compiler_params=_cp((PAR,)))(pos_col, lane_freq)


def _swap_matrix():
    k = lax.broadcasted_iota(jnp.int32, (DH, DH), 0)
    i = lax.broadcasted_iota(jnp.int32, (DH, DH), 1)
    half = ROT // 2
    hit = ((i < half) & (k == i + half)) | ((i >= half) & (i < ROT) & (k == i - half))
    return jnp.where(hit, 1.0, 0.0).astype(bf16)


def _head_mean(x):
    return jnp.dot(x.astype(bf16), jnp.ones((DH, DH), bf16), preferred_element_type=f32) * (1.0 / DH)


def _rope(n, c, t, swap):
    return n * c + jnp.dot(n.astype(bf16), swap, preferred_element_type=f32) * t


def _rope_t(d, c, t, swap):
    return d * c + jnp.dot((d * t).astype(bf16), swap, preferred_element_type=f32)


def _qk_fwd(raw, g, tabs, width, with_values, name):
    S = raw.shape[0]
    nh = width // DH
    ow = width // NG
    hpg = ow // DH
    tm = _tile(S, 256, 16 * max(DILS))
    vgroups = [gi for gi, r in enumerate(DILS) if r > 1] if with_values else []

    def body(x_ref, g_ref, c_ref, sa_ref, sb_ref, *rest):
        o_refs = rest[:NG]
        v_refs = rest[NG:NG + len(vgroups)]
        scr, vscr = rest[NG + len(vgroups):]
        c, t, swap = c_ref[...], sa_ref[...] + sb_ref[...], _swap_matrix()
        for gi, r in enumerate(DILS):
            heads = range(gi * hpg, (gi + 1) * hpg)
            xs = [x_ref[:, h * DH:(h + 1) * DH].astype(f32) for h in heads]
            rs = [lax.rsqrt(_head_mean(xv * xv) + EPS) for xv in xs]
            ys = [_rope(xv * rv * g_ref[...], c, t, swap) for xv, rv in zip(xs, rs)]
            for hh, y in enumerate(ys):
                if r == 1:
                    o_refs[gi][:, hh * DH:(hh + 1) * DH] = y.astype(bf16)
                else:
                    scr[hh] = y
            if r > 1:
                for hh in range(hpg):
                    for j in range(r):
                        o_refs[gi][:, j * ow + hh * DH:j * ow + (hh + 1) * DH] = scr[hh, pl.ds(j, tm // r, stride=r), :].astype(bf16)
        for vi, gi in enumerate(vgroups):
            _to_view(x_ref[:, width + gi * ow:width + (gi + 1) * ow].astype(f32), v_refs[vi], vscr, DILS[gi], ow, tm)

    win = raw.shape[1] if with_values else width
    row = pl.BlockSpec((tm, win), lambda i: (i, 0))
    tab = pl.BlockSpec((tm, DH), lambda i: (i, 0))
    view = lambda r: pl.BlockSpec((tm // r, r * ow), lambda i: (i, 0))
    vshape = lambda r: jax.ShapeDtypeStruct((S // r, r * ow), bf16)
    outs = pl.pallas_call(
        body, name=name, grid=(S // tm,), in_specs=[row, _acc_spec(DH), tab, tab, tab],
        out_specs=[view(r) for r in DILS] + [view(DILS[gi]) for gi in vgroups],
        out_shape=[vshape(r) for r in DILS] + [vshape(DILS[gi]) for gi in vgroups],
        scratch_shapes=[pltpu.VMEM((hpg, tm, DH), f32), pltpu.VMEM((ow // DH, tm, DH), f32)],
        compiler_params=_cp((PAR,)))(raw, g, *tabs)
    return outs[:NG], outs[NG:]


def _qk_bwd(dparts, raw, g, tabs, width, extra, name):
    S = raw.shape[0]
    nh = width // DH
    ow = width // NG
    hpg = ow // DH
    tm = _tile(S, 256, 16 * max(DILS))
    wout = width + len(extra) * ow

    def body(*refs):
        d_refs = refs[:NG]
        x_ref, g_ref, c_ref, sa_ref, sb_ref = refs[NG:NG + 5]
        e_refs = refs[NG + 5:NG + 5 + len(extra)]
        o_ref, dg_ref, scr, vscr = refs[NG + 5 + len(extra):]
        i = pl.program_id(0)

        @pl.when(i == 0)
        def _():
            dg_ref[...] = jnp.zeros_like(dg_ref)

        c, t, swap = c_ref[...], sa_ref[...] + sb_ref[...], _swap_matrix()
        gv = g_ref[...]
        dg = jnp.zeros((1, DH), f32)
        for gi, r in enumerate(DILS):
            heads = list(range(gi * hpg, (gi + 1) * hpg))
            if r == 1:
                douts = [d_refs[gi][:, hh * DH:(hh + 1) * DH].astype(f32) for hh in range(hpg)]
            else:
                for hh in range(hpg):
                    for j in range(r):
                        scr[hh, pl.ds(j, tm // r, stride=r), :] = d_refs[gi][:, j * ow + hh * DH:j * ow + (hh + 1) * DH].astype(f32)
                douts = [scr[hh] for hh in range(hpg)]
            xs = [x_ref[:, h * DH:(h + 1) * DH].astype(f32) for h in heads]
            rs = [lax.rsqrt(_head_mean(xv * xv) + EPS) for xv in xs]
            xhs = [xv * rv for xv, rv in zip(xs, rs)]
            dns = [_rope_t(d, c, t, swap) for d in douts]
            for dn, xh in zip(dns, xhs):
                dg = dg + jnp.sum(dn * xh, axis=0, keepdims=True)
            dxns = [dn * gv for dn in dns]
            dxs = [rv * (dxn - xh * _head_mean(dxn * xh)) for rv, dxn, xh in zip(rs, dxns, xhs)]
            for h, dx in zip(heads, dxs):
                o_ref[:, h * DH:(h + 1) * DH] = dx.astype(o_ref.dtype)
        for gi, e_ref in enumerate(e_refs):
            o_ref[:, width + gi * ow:width + (gi + 1) * ow] = _from_view(e_ref, vscr, DILS[gi], ow, tm).astype(o_ref.dtype)
        dg_ref[...] += dg

    views = [pl.BlockSpec((tm // r, r * ow), lambda i: (i, 0)) for r in DILS]
    tab = pl.BlockSpec((tm, DH), lambda i: (i, 0))
    return pl.pallas_call(
        body, name=name, grid=(S // tm,),
        in_specs=views + [pl.BlockSpec((tm, width), lambda i: (i, 0)), _acc_spec(DH), tab, tab, tab] + (views if extra else []),
        out_specs=[pl.BlockSpec((tm, wout), lambda i: (i, 0)), _acc_spec(DH)],
        out_shape=[jax.ShapeDtypeStruct((S, wout), bf16), jax.ShapeDtypeStruct((1, DH), f32)],
        scratch_shapes=[pltpu.VMEM((hpg, tm, DH), f32), pltpu.VMEM((ow // DH, tm, DH), f32)],
        compiler_params=_cp((ARB,)))(*dparts, raw, g, *tabs, *extra)


def _dot_nt(a, b):
    return lax.dot_general(a, b, (((1,), (1,)), ((), ())), preferred_element_type=f32)


def _dot_tn(a, b):
    return lax.dot_general(a, b, (((0,), (0,)), ((), ())), preferred_element_type=f32)


def _band_masks():
    qi = lax.broadcasted_iota(jnp.int32, (BLK, BLK), 0)
    ki = lax.broadcasted_iota(jnp.int32, (BLK, BLK), 1)
    return ki <= qi, ki >= qi


def _attn_fwd(qv, kview, vview, vbase, r, name):
    sr = qv.shape[0]
    ow = qv.shape[1] // r
    hpg = ow // DH
    nb = sr // BLK
    scale = 1.0 / math.sqrt(DH)

    def body(q_ref, kc_ref, kp_ref, vc_ref, vp_ref, o_ref, l_ref):
        n = pl.program_id(1)
        m_cur, m_prev = _band_masks()
        m_prev = m_prev & (n > 0)
        hs = [slice(h * DH, (h + 1) * DH) for h in range(hpg)]
        s_c = [jnp.where(m_cur, _dot_nt(q_ref[:, s], kc_ref[:, s]) * scale, NEG) for s in hs]
        s_p = [jnp.where(m_prev, _dot_nt(q_ref[:, s], kp_ref[:, s]) * scale, NEG) for s in hs]
        mx = [jnp.maximum(jnp.max(a, axis=-1, keepdims=True), jnp.max(b, axis=-1, keepdims=True)) for a, b in zip(s_c, s_p)]
        p_c = [jnp.exp(a - m) for a, m in zip(s_c, mx)]
        p_p = [jnp.exp(a - m) for a, m in zip(s_p, mx)]
        den = [jnp.sum(a, axis=-1, keepdims=True) + jnp.sum(b, axis=-1, keepdims=True) for a, b in zip(p_c, p_p)]
        for h, s in enumerate(hs):
            o = jnp.dot(p_c[h].astype(bf16), vc_ref[:, s], preferred_element_type=f32)
            o = o + jnp.dot(p_p[h].astype(bf16), vp_ref[:, s], preferred_element_type=f32)
            o_ref[:, s] = (o / den[h]).astype(o_ref.dtype)
            l_ref[:, s] = jnp.broadcast_to(mx[h] + jnp.log(den[h]), (BLK, DH))

    cur = lambda j, n: (n, j)
    prev = lambda j, n: (jnp.maximum(n - 1, 0), j)
    vcur = lambda j, n: (n, vbase + j)
    vprev = lambda j, n: (jnp.maximum(n - 1, 0), vbase + j)
    blk = lambda f: pl.BlockSpec((BLK, ow), f)
    return pl.pallas_call(
        body, name=name, grid=(r, nb), in_specs=[blk(cur), blk(cur), blk(prev), blk(vcur), blk(vprev)],
        out_specs=[blk(cur), blk(cur)],
        out_shape=[jax.ShapeDtypeStruct((sr, r * ow), bf16), jax.ShapeDtypeStruct((sr, r * ow), f32)],
        compiler_params=_cp((PAR, PAR)))(qv, kview, kview, vview, vview)


def _attn_bwd_q(qv, kview, vview, vbase, do_g, lse, corr, r, name, comms=()):
    sr = qv.shape[0]
    ow = qv.shape[1] // r
    hpg = ow // DH
    nb = sr // BLK
    scale = 1.0 / math.sqrt(DH)

    def body(q_ref, kc_ref, kp_ref, vc_ref, vp_ref, do_ref, l_ref, c_ref, dq_ref):
        n = pl.program_id(1)
        m_cur, m_prev = _band_masks()
        m_prev = m_prev & (n > 0)
        hs = [slice(h * DH, (h + 1) * DH) for h in range(hpg)]
        ls = [slice(h * DH, h * DH + BLK) for h in range(hpg)]
        sides = ((kc_ref, vc_ref, m_cur), (kp_ref, vp_ref, m_prev))
        sc = [[jnp.where(msk, _dot_nt(q_ref[:, s], k_ref[:, s]) * scale, NEG) for s in hs] for k_ref, _, msk in sides]
        dp = [[_dot_nt(do_ref[:, s], v_ref[:, s]) for s in hs] for _, v_ref, _ in sides]
        ds = [[(jnp.exp(sc[i][h] - l_ref[:, ls[h]]) * (dp[i][h] + c_ref[:, ls[h]])).astype(bf16) for h in range(hpg)]
              for i in range(2)]
        for h, s in enumerate(hs):
            dq = jnp.dot(ds[0][h], kc_ref[:, s], preferred_element_type=f32)
            dq = dq + jnp.dot(ds[1][h], kp_ref[:, s], preferred_element_type=f32)
            dq_ref[:, s] = (dq * scale).astype(dq_ref.dtype)

    cur = lambda j, n: (n, j)
    prev = lambda j, n: (jnp.maximum(n - 1, 0), j)
    vcur = lambda j, n: (n, vbase + j)
    vprev = lambda j, n: (jnp.maximum(n - 1, 0), vbase + j)
    blk = lambda f: pl.BlockSpec((BLK, ow), f)
    return _pcall(
        body, name=name, grid=(r, nb),
        in_specs=[blk(cur), blk(cur), blk(prev), blk(vcur), blk(vprev), blk(cur), blk(cur), blk(cur)],
        out_specs=blk(cur), out_shape=jax.ShapeDtypeStruct((sr, r * ow), bf16), sem=(PAR, PAR), comms=comms,
        args=[qv, kview, kview, vview, vview, do_g, lse, corr])


def _attn_bwd_kv(qv, kview, vview, vbase, do_g, lse, corr, r, name):
    sr = qv.shape[0]
    ow = qv.shape[1] // r
    hpg = ow // DH
    nb = sr // BLK
    scale = 1.0 / math.sqrt(DH)

    def body(k_ref, v_ref, qc_ref, qn_ref, doc_ref, don_ref, lc_ref, ln_ref, cc_ref, cn_ref, dk_ref, dv_ref):
        n = pl.program_id(1)
        m_cur, m_prev = _band_masks()
        m_next = m_prev & (n < nb - 1)
        hs = [slice(h * DH, (h + 1) * DH) for h in range(hpg)]
        ls = [slice(h * DH, h * DH + BLK) for h in range(hpg)]
        sides = ((qc_ref, doc_ref, lc_ref, cc_ref, m_cur), (qn_ref, don_ref, ln_ref, cn_ref, m_next))
        sc = [[jnp.where(msk, _dot_nt(q_ref[:, s], k_ref[:, s]) * scale, NEG) for s in hs] for q_ref, _, _, _, msk in sides]
        dp = [[_dot_nt(do_ref[:, s], v_ref[:, s]) for s in hs] for _, do_ref, _, _, _ in sides]
        p = [[jnp.exp(sc[i][h] - sides[i][2][:, ls[h]]) for h in range(hpg)] for i in range(2)]
        ds = [[(p[i][h] * (dp[i][h] + sides[i][3][:, ls[h]])).astype(bf16) for h in range(hpg)] for i in range(2)]
        for h, s in enumerate(hs):
            dv = _dot_tn(p[0][h].astype(bf16), doc_ref[:, s]) + _dot_tn(p[1][h].astype(bf16), don_ref[:, s])
            dk = _dot_tn(ds[0][h], qc_ref[:, s]) + _dot_tn(ds[1][h], qn_ref[:, s])
            dk_ref[:, s] = (dk * scale).astype(dk_ref.dtype)
            dv_ref[:, s] = dv.astype(dv_ref.dtype)

    cur = lambda j, n: (n, j)
    nxt = lambda j, n: (jnp.minimum(n + 1, nb - 1), j)
    vcur = lambda j, n: (n, vbase + j)
    blk = lambda f: pl.BlockSpec((BLK, ow), f)
    shp = jax.ShapeDtypeStruct((sr, r * ow), bf16)
    return pl.pallas_call(
        body, name=name, grid=(r, nb),
        in_specs=[blk(cur), blk(vcur), blk(cur), blk(nxt), blk(cur), blk(nxt), blk(cur), blk(nxt), blk(cur), blk(nxt)],
        out_specs=[blk(cur), blk(cur)], out_shape=[shp, shp],
        compiler_params=_cp((PAR, PAR)))(kview, vview, qv, qv, do_g, do_g, lse, lse, corr, corr)


def _mix_weights(l_refs):
    ls = [l[...] for l in l_refs]
    mx = functools.reduce(jnp.maximum, ls)
    es = [jnp.exp(l - mx) for l in ls]
    den = functools.reduce(lambda a, b: a + b, es)
    return [e / den for e in es]


def _from_view(ref, scr, r, ow, tm):
    if r == 1:
        return ref[...].astype(f32)
    for c in range(ow // DH):
        for j in range(r):
            scr[c, pl.ds(j, tm // r, stride=r), :] = ref[:, j * ow + c * DH:j * ow + (c + 1) * DH].astype(f32)
    return jnp.concatenate([scr[c] for c in range(ow // DH)], axis=1)


def _to_view(val, ref, scr, r, ow, tm):
    if r == 1:
        ref[...] = val.astype(ref.dtype)
        return
    for c in range(ow // DH):
        scr[c] = val[:, c * DH:(c + 1) * DH]
        for j in range(r):
            ref[:, j * ow + c * DH:j * ow + (c + 1) * DH] = scr[c, pl.ds(j, tm // r, stride=r), :].astype(ref.dtype)


def _view_specs(tm, ow):
    return [pl.BlockSpec((tm // r, r * ow), lambda i: (i, 0)) for r in DILS]


def _combine_fwd(os_, lses, name):
    ow = os_[0].shape[1] // DILS[0]
    S = os_[0].shape[0] * DILS[0]
    tm = _tile(S, 256, 16 * max(DILS))

    def body(*refs):
        o_refs, l_refs, out_ref = refs[:NG], refs[NG:2 * NG], refs[2 * NG]
        scr = refs[2 * NG + 1:]
        ov = [_from_view(o_refs[gi], scr[2 * gi], DILS[gi], ow, tm) for gi in range(NG)]
        lv = [_from_view(l_refs[gi], scr[2 * gi + 1], DILS[gi], ow, tm) for gi in range(NG)]
        al = _mix_weights(lv)
        acc = al[0] * ov[0]
        for gi in range(1, NG):
            acc = acc + al[gi] * ov[gi]
        out_ref[...] = acc.astype(out_ref.dtype)

    views = _view_specs(tm, ow)
    return pl.pallas_call(body, name=name, grid=(S // tm,), in_specs=views + views,
                          out_specs=pl.BlockSpec((tm, ow), lambda i: (i, 0)), out_shape=jax.ShapeDtypeStruct((S, ow), bf16),
                          scratch_shapes=[pltpu.VMEM((ow // DH, tm, DH), f32)] * (2 * NG),
                          compiler_params=_cp((PAR,), VMEM_BIG))(*os_, *lses)


def _combine_bwd(do, os_, lses, name, comms=()):
    S, ow = do.shape
    hpg = ow // DH
    tm = _tile(S, 256, 16 * max(DILS))

    def body(*refs):
        do_ref = refs[0]
        o_refs, l_refs = refs[1:1 + NG], refs[1 + NG:1 + 2 * NG]
        dog_refs, c_refs = refs[1 + 2 * NG:1 + 3 * NG], refs[1 + 3 * NG:1 + 4 * NG]
        scr = refs[1 + 4 * NG:]
        ov = [_from_view(o_refs[gi], scr[2 * gi], DILS[gi], ow, tm) for gi in range(NG)]
        lv = [_from_view(l_refs[gi], scr[2 * gi + 1], DILS[gi], ow, tm) for gi in range(NG)]
        al = _mix_weights(lv)
        dov = do_ref[...]
        o = al[0] * ov[0]
        for gi in range(1, NG):
            o = o + al[gi] * ov[gi]
        prod = dov * o
        t = jnp.concatenate(
            [jnp.broadcast_to(jnp.sum(prod[:, h * DH:(h + 1) * DH], axis=-1, keepdims=True), (tm, DH)) for h in range(hpg)],
            axis=1)
        for gi in range(NG):
            _to_view(al[gi] * dov, dog_refs[gi], scr[2 * NG], DILS[gi], ow, tm)
            _to_view(-(al[gi] * t), c_refs[gi], scr[2 * NG], DILS[gi], ow, tm)

    views = _view_specs(tm, ow)
    vshape = lambda dt: [jax.ShapeDtypeStruct((S // r, r * ow), dt) for r in DILS]
    return _pcall(
        body, name=name, grid=(S // tm,), in_specs=[pl.BlockSpec((tm, ow), lambda i: (i, 0))] + views + views,
        out_specs=views + views, out_shape=vshape(bf16) + vshape(f32),
        args=[do, *os_, *lses], scratch_shapes=[pltpu.VMEM((ow // DH, tm, DH), f32)] * (2 * NG + 1), sem=(PAR,), vmem=VMEM_BIG,
        comms=comms)


def _pad_rows(w, rows):
    return jnp.concatenate([w, jnp.zeros((rows - w.shape[0], w.shape[1]), w.dtype)], axis=0)


def kernel(x, c, positions, mod_w, mod_b, norm_mix_g, norm_ffn_g, conv_pw1_w, conv_pw1_b, conv_dw_w, conv_dw_b, conv_ln_g, conv_ln_b, conv_pw2_w, conv_pw2_b, kv_mod_w, kv_mod_b, kv_norm_g, w_kv, k_norm_g, w_q, q_norm_g, w_o, ffn_up_w, ffn_dw_w, ffn_dw_b, ffn_down_w, loss_target, m_mod_w, m_mod_b, m_norm_mix_g, m_norm_ffn_g, m_conv_pw1_w, m_conv_pw1_b, m_conv_dw_w, m_conv_dw_b, m_conv_ln_g, m_conv_ln_b, m_conv_pw2_w, m_conv_pw2_b, m_kv_mod_w, m_kv_mod_b, m_kv_norm_g, m_w_kv, m_k_norm_g, m_w_q, m_q_norm_g, m_w_o, m_ffn_up_w, m_ffn_dw_w, m_ffn_dw_b, m_ffn_down_w, v_mod_w, v_mod_b, v_norm_mix_g, v_norm_ffn_g, v_conv_pw1_w, v_conv_pw1_b, v_conv_dw_w, v_conv_dw_b, v_conv_ln_g, v_conv_ln_b, v_conv_pw2_w, v_conv_pw2_b, v_kv_mod_w, v_kv_mod_b, v_kv_norm_g, v_w_kv, v_k_norm_g, v_w_q, v_q_norm_g, v_w_o, v_ffn_up_w, v_ffn_dw_w, v_ffn_dw_b, v_ffn_down_w):
    S, Dm = x.shape[1], x.shape[2]
    F = ffn_dw_b.shape[1]
    QW = NG * HPG * DH
    OW = HPG * DH
    mx, my, mc = _me()
    me = 4 * mx + 2 * my + mc
    core = jnp.reshape(mc, (1,)).astype(jnp.int32)
    chip = jnp.reshape(2 * mx + my, (1,)).astype(jnp.int32)
    x0 = x.reshape(S, Dm)
    target = loss_target.reshape(S, Dm)

    c_all = _ag_small(c, "ag_c").reshape(NDEV, Dm)
    n_mod = mod_w.shape[2]
    n_kvm = kv_mod_w.shape[1]
    b0 = lax.dynamic_slice(mod_b, (0, me * n_mod), (1, n_mod))
    b1 = lax.dynamic_slice(mod_b, (1, me * n_mod), (1, n_mod))
    bk = lax.dynamic_slice(kv_mod_b.reshape(1, -1), (0, me * n_kvm), (1, n_kvm))
    m_part = jnp.concatenate([_modproj(c_all, mod_w[0], b0, "modproj0"), _modproj(c_all, mod_w[1], b1, "modproj1"),
                              _modproj(c_all, kv_mod_w, bk, "modproj_kv")], axis=1)
    m_all = _ag_small(m_part, "ag_mod")
    m_mine = lax.dynamic_index_in_dim(m_all, me, axis=1, keepdims=False)
    mod0 = m_mine[:, :n_mod].reshape(6, Dm)
    mod1 = m_mine[:, n_mod:2 * n_mod].reshape(6, Dm)
    modkv = m_mine[:, 2 * n_mod:].reshape(2, Dm)
    row = lambda a, i: a[i:i + 1]

    as3 = lambda w: w if w.ndim == 3 else w[None]
    sh16 = lambda w: as3(w).astype(bf16)
    ag_pw1 = _comm_allgather(sh16(conv_pw1_w), 2)
    ag_pw2 = _comm_allgather(sh16(conv_pw2_w), 1)
    ag_up = [_comm_allgather(sh16(ffn_up_w[l]), 2) for l in range(2)]
    ag_down = [_comm_allgather(sh16(ffn_down_w[l]), 1) for l in range(2)]
    ag_kv = _comm_allgather(sh16(w_kv), 2)
    ag_q = _comm_allgather(sh16(w_q), 2)
    ag_o = _comm_allgather(sh16(w_o), 2)

    sp_flat = jnp.concatenate([conv_pw1_b.reshape(-1), conv_dw_b.reshape(-1), conv_ln_g.reshape(-1), conv_ln_b.reshape(-1),
                               conv_pw2_b.reshape(-1), conv_dw_w.reshape(-1), ffn_dw_w.reshape(-1)])
    sp_rows = -(-sp_flat.shape[0] // 1024) * 8
    sp_flat = jnp.concatenate([sp_flat, jnp.zeros((sp_rows * 128 - sp_flat.shape[0],), f32)]).reshape(sp_rows, 128)
    n1, nd = conv_pw1_b.shape[1], conv_dw_b.shape[1]
    nfw = ffn_dw_w.shape[2]
    sp = _ag_small(sp_flat, "ag_small_params").reshape(NDEV, -1)
    off = 0
    pw1_b = sp[:, off:off + n1].reshape(1, -1); off += n1
    dw_b = sp[:, off:off + nd].reshape(1, -1); off += nd
    ln_g = sp[:, off:off + nd].reshape(1, -1); off += nd
    ln_b = sp[:, off:off + nd].reshape(1, -1); off += nd
    pw2_b = sp[:, off:off + nd].reshape(1, -1); off += nd
    dw_w = jnp.transpose(sp[:, off:off + CONV_K * nd].reshape(NDEV, CONV_K, nd), (1, 0, 2)).reshape(CONV_K, -1); off += CONV_K * nd
    fdw_w = jnp.transpose(sp[:, off:off + 2 * FFN_K * nfw].reshape(NDEV, 2, FFN_K, nfw), (1, 2, 0, 3)).reshape(2, FFN_K, -1)
    dw_w32 = _pad_rows(dw_w, 32)

    tabs = _rope_tables(positions.reshape(S, 1), "rope_tables")

    def with_comms(res, comms):
        return res if comms else (res, [])

    def rs_d2d(dwb):
        return [_comm_rs_sibling(dwb)]

    def rs_add(dwb, couts, tag):
        return _chip_partial(dwb, couts[0][0], core, f"rs_add_{tag}")

    def rs_ici(part):
        return [_comm_rs_chips(part)]

    def ffn_forward(xin, l, modv, w_up, w_down, up_comms, down_comms):
        h2 = _mod_fwd(xin, row(norm_ffn_g, l), row(modv, 3), row(modv, 4), f"ffn{l}_mod")
        fw8 = _pad_rows(fdw_w[l], 8)
        (u2, act), c_up = with_comms(_up_gate(h2, w_up, fw8, row(ffn_dw_b, l), f"ffn{l}_up", comms=up_comms), up_comms)
        if w_down is None:
            w_down, c_up = c_up[0][0], c_up[1:]
        (xout, f), c_down = with_comms(
            _mm_nn(act, w_down, 0, name=f"ffn{l}_down", res=xin, gate=row(modv, 5), tk=F, tn=512, comms=down_comms), down_comms)
        return xout, (h2, u2, act, f, fw8, w_up, w_down), c_up, c_down

    def ffn_backward(dx, xin, l, modv, saved, dact_comms):
        h2, u2, act, f, fw8, w_up, w_down = saved
        df, dgate, _ = _gate_bwd(dx, f, row(modv, 5), f"ffn{l}_gate_bwd")
        dact, c_dact = with_comms(_mm_nt(df, w_down, 0, name=f"ffn{l}_dact", out_dtype=bf16, tko=512, tn=Dm, comms=dact_comms), dact_comms)
        d_down = _mm_tn(act, df, name=f"ffn{l}_ddown", col_sharded=False)
        (du2, d_fw, d_fb), c1 = _ffn_gate_bwd(u2, dact, fw8, row(ffn_dw_b, l), f"ffn{l}_gatebwd", comms=rs_d2d(d_down))
        part_down = rs_add(d_down, c1, f"down{l}")
        dh2, c2 = _mm_nt(du2, w_up, 0, name=f"ffn{l}_dh", out_dtype=f32, tko=1024, tn=F // 2, comms=rs_ici(part_down))
        d_up = _mm_tn(h2, du2, name=f"ffn{l}_dup", col_sharded=True)
        (dxin, dsh, dsc, dg), c3 = _mod_bwd(dh2, xin, dx, row(norm_ffn_g, l), row(modv, 4), f"ffn{l}_mod_bwd", comms=rs_d2d(d_up))
        part_up = rs_add(d_up, c3, f"up{l}")
        grads = dict(d_fw=d_fw[:FFN_K], d_fb=d_fb, dsh=dsh, dsc=dsc, dgate=dgate, dg=dg,
                     down=(part_down, c2[0][0]), part_up=part_up)
        return dxin, grads, c_dact

    h0, c = _mod_fwd(x0, row(norm_mix_g, 0), row(mod0, 0), row(mod0, 1), "l0_mod", comms=[ag_pw1])
    W_pw1 = c[0][0]
    u0, c = _mm_nn(h0, W_pw1, 0, name="l0_pw1", bias=pw1_b, comms=[ag_pw2])
    W_pw2 = c[0][0]
    (s0, cv0), c = _conv_fwd(u0, dw_w32, dw_b, ln_g, ln_b, "l0_conv", comms=[ag_up[0]])
    W_up0 = c[0][0]
    (x1, f0), c = _mm_nn(s0, W_pw2, 0, name="l0_pw2", bias=pw2_b, res=x0, gate=row(mod0, 2), comms=[ag_q, ag_o])
    W_q, W_o = c[0][0], c[1][0]
    x2, ffn0_saved, c_up, c_down = ffn_forward(x1, 0, mod0, W_up0, None, [ag_down[0], ag_kv, ag_down[1]], [ag_up[1]])
    W_kv, W_down1, W_up1 = c_up[0][0], c_up[1][0], c_down[0][0]

    hkv = _mod_fwd(x2, kv_norm_g.reshape(1, -1), row(modkv, 0), row(modkv, 1), "kv_mod")
    kvraw = _mm_nn(hkv, W_kv, 0, name="kv_proj")
    kg = k_norm_g.reshape(1, -1)
    k_gv, v_dil = _qk_fwd(kvraw, kg, tabs, QW, True, "k_norm_rope")
    dilated = [gi for gi, r in enumerate(DILS) if r > 1]
    v_of = {gi: (kvraw, NG + gi) for gi, r in enumerate(DILS) if r == 1}
    v_of.update({gi: (v_dil[i], 0) for i, gi in enumerate(dilated)})
    h1 = _mod_fwd(x2, row(norm_mix_g, 1), row(mod1, 0), row(mod1, 1), "l1_mod")
    qraw = _mm_nn(h1, W_q, 0, name="q_proj")
    qg = q_norm_g.reshape(1, -1)
    q_gv, _ = _qk_fwd(qraw, qg, tabs, QW, False, "q_norm_rope")
    o_gs, lses = [], []
    for gi, r in enumerate(DILS):
        o_g, lse_g = _attn_fwd(q_gv[gi], k_gv[gi], *v_of[gi], r, f"attn_fwd{gi}")
        o_gs.append(o_g)
        lses.append(lse_g)
    o_mix = _combine_fwd(o_gs, lses, "attn_mix")
    x3, f1 = _mm_nn(o_mix, W_o, 0, name="o_proj", res=x2, gate=row(mod1, 2))
    x4, ffn1_saved, _, _ = ffn_forward(x3, 1, mod1, W_up1, W_down1, (), ())

    dx4, loss_blk = _loss_grad(x4, target, "loss")
    loss = lax.psum(loss_blk[0, 0], ("x", "y", "c"))

    red = {}
    dx3, gf1, _ = ffn_backward(dx4, x3, 1, mod1, ffn1_saved, ())
    dy1, dgate_m1, _ = _gate_bwd(dx3, f1, row(mod1, 2), "l1_gate_bwd")
    do = _mm_nt(dy1, W_o, 0, name="o_proj_dx", out_dtype=f32, tko=1024, tn=Dm)
    d_wo = _mm_tn(o_mix, dy1, name="o_proj_dw", col_sharded=True)
    outs, c = _combine_bwd(do, o_gs, lses, "attn_mix_bwd", comms=rs_d2d(d_wo))
    part_wo = rs_add(d_wo, c, "wo")
    do_gs, corrs = outs[:NG], outs[NG:]
    dq_gs, dk_gs, dv_gs = [], [], []
    for gi, r in enumerate(DILS):
        cm = rs_ici(part_wo) if gi == 0 else ()
        dq_g, c = with_comms(_attn_bwd_q(q_gv[gi], k_gv[gi], *v_of[gi], do_gs[gi], lses[gi], corrs[gi], r, f"attn_bwd_q{gi}",
                                         comms=cm), cm)
        if gi == 0:
            red["w_o"] = (part_wo, c[0][0])
        dq_gs.append(dq_g)
        dk_g, dv_g = _attn_bwd_kv(q_gv[gi], k_gv[gi], *v_of[gi], do_gs[gi], lses[gi], corrs[gi], r, f"attn_bwd_kv{gi}")
        dk_gs.append(dk_g)
        dv_gs.append(dv_g)
    dqraw, d_qg = _qk_bwd(dq_gs, qraw, qg, tabs, QW, (), "q_norm_rope_bwd")
    dkvraw, d_kg = _qk_bwd(dk_gs, kvraw, kg, tabs, QW, tuple(dv_gs), "k_norm_rope_bwd")
    dh1 = _mm_nt(dqraw, W_q, 0, name="q_proj_dx", out_dtype=f32, tko=1024, tn=QW)
    d_wq = _mm_tn(h1, dqraw, name="q_proj_dw", col_sharded=True)
    dhkv, c = _mm_nt(dkvraw, W_kv, 0, name="kv_proj_dx", out_dtype=f32, tko=512, tn=2 * QW, comms=rs_d2d(d_wq))
    part_wq = rs_add(d_wq, c, "wq")
    d_wkv, c = _mm_tn(hkv, dkvraw, name="kv_proj_dw", col_sharded=True, comms=rs_ici(gf1["part_up"]))
    red["ffn_up_w1"] = (gf1["part_up"], c[0][0])
    (dx2a, dsh_m1, dsc_m1, dg_mix1), c = _mod_bwd(dh1, x2, dx3, row(norm_mix_g, 1), row(mod1, 1), "l1_mod_bwd",
                                                  comms=rs_ici(part_wq))
    red["w_q"] = (part_wq, c[0][0])
    (dx2, dsh_kv, dsc_kv, dg_kvn), c = _mod_bwd(dhkv, x2, dx2a, kv_norm_g.reshape(1, -1), row(modkv, 1), "kv_mod_bwd",
                                                comms=rs_d2d(d_wkv))
    part_wkv = rs_add(d_wkv, c, "wkv")

    dx1, gf0, c = ffn_backward(dx2, x1, 0, mod0, ffn0_saved, rs_ici(part_wkv))
    red["w_kv"] = (part_wkv, c[0][0])
    dy0, dgate_m0, d_pw2b = _gate_bwd(dx1, f0, row(mod0, 2), "l0_gate_bwd")
    ds0 = _mm_nt(dy0, W_pw2, 0, name="l0_pw2_dx", out_dtype=bf16, tko=1024, tn=Dm)
    d_pw2 = _mm_tn(s0, dy0, name="l0_pw2_dw", col_sharded=False)
    (dcv, d_lng, d_lnb, d_dwb, d_dww), c = _conv_bwd1(u0, cv0, ds0, dw_w32, dw_b, ln_g, ln_b, "l0_conv_bwd1",
                                                      comms=rs_ici(gf0["part_up"]))
    red["ffn_up_w0"] = (gf0["part_up"], c[0][0])
    (du0, d_pw1b), c = _conv_bwd2(dcv, u0, dw_w32, "l0_conv_bwd2", comms=rs_d2d(d_pw2))
    part_pw2 = rs_add(d_pw2, c, "pw2")
    d_pw1 = _mm_tn(h0, du0, name="l0_pw1_dw", col_sharded=True)
    dh0, c = _mm_nt(du0, W_pw1, 0, name="l0_pw1_dx", out_dtype=f32, tko=1024, tn=2 * Dm, comms=rs_ici(part_pw2) + rs_d2d(d_pw1))
    red["conv_pw2_w"] = (part_pw2, c[0][0])
    part_pw1 = rs_add(d_pw1, c[1:], "pw1")
    (grad_x, dsh_m0, dsc_m0, dg_mix0), c = _mod_bwd(dh0, x0, dx1, row(norm_mix_g, 0), row(mod0, 1), "l0_mod_bwd",
                                                    comms=rs_ici(part_pw1))
    red["conv_pw1_w"] = (part_pw1, c[0][0])
    red["ffn_down_w0"], red["ffn_down_w1"] = gf0["down"], gf1["down"]

    dm0 = [dsh_m0, dsc_m0, dgate_m0, gf0["dsh"], gf0["dsc"], gf0["dgate"]]
    dm1 = [dsh_m1, dsc_m1, dgate_m1, gf1["dsh"], gf1["dsc"], gf1["dgate"]]
    pieces = dm0 + dm1 + [dsh_kv, dsc_kv,
                          dg_mix0, dg_mix1, gf0["dg"], gf1["dg"], dg_kvn, d_kg, d_qg, gf0["d_fb"], gf1["d_fb"],
                          d_pw1b, d_dww[:CONV_K], d_dwb, d_lng, d_lnb, d_pw2b, gf0["d_fw"], gf1["d_fw"]]
    flat = jnp.concatenate([p.reshape(-1) for p in pieces])
    n_flat = flat.shape[0]
    n_rows = -(-n_flat // 1024) * 8
    flat = jnp.concatenate([flat, jnp.zeros((n_rows * 128 - n_flat,), f32)]).reshape(n_rows, 128)
    g_all = _ag_small(flat, "ag_small_grads")
    g_sum = _sum8(g_all, "sum_small_grads").reshape(-1)
    n_dm = 2 * 6 * Dm + 2 * Dm
    dm_all = g_all.reshape(NDEV, -1)[:, :n_dm]

    take_pos = [0]

    def take(shape):
        n = int(np.prod(shape))
        out = g_sum[take_pos[0]:take_pos[0] + n].reshape(shape)
        take_pos[0] += n
        return out

    g_mod_b = take((2, 6 * Dm))
    g_kv_mod_b = take((2 * Dm,))
    g_norm_mix0, g_norm_mix1 = take((Dm,)), take((Dm,))
    g_norm_ffn0, g_norm_ffn1 = take((Dm,)), take((Dm,))
    g_kv_norm = take((Dm,))
    g_k_norm = take((DH,))
    g_q_norm = take((1, DH))
    g_ffn_dw_b = take((2, F))
    shard = lambda full, n, axis: lax.dynamic_slice_in_dim(full, me * n, n, axis)
    g_pw1_b = shard(take((1, 2 * Dm)), n1, 1)
    g_dw_w = shard(take((1, CONV_K, Dm)), nd, 2)
    g_dw_b = shard(take((1, Dm)), nd, 1)
    g_ln_g = shard(take((1, Dm)), nd, 1)
    g_ln_b = shard(take((1, Dm)), nd, 1)
    g_pw2_b = shard(take((1, Dm)), nd, 1)
    g_ffn_dw_w = shard(jnp.stack([take((FFN_K, F)), take((FFN_K, F))]), nfw, 2)
    g_norm_mix = jnp.stack([g_norm_mix0, g_norm_mix1])
    g_norm_ffn = jnp.stack([g_norm_ffn0, g_norm_ffn1])

    small = [("mod_b", mod_b, m_mod_b, v_mod_b, g_mod_b), ("norm_mix_g", norm_mix_g, m_norm_mix_g, v_norm_mix_g, g_norm_mix),
             ("norm_ffn_g", norm_ffn_g, m_norm_ffn_g, v_norm_ffn_g, g_norm_ffn),
             ("conv_pw1_b", conv_pw1_b, m_conv_pw1_b, v_conv_pw1_b, g_pw1_b),
             ("conv_dw_w", conv_dw_w, m_conv_dw_w, v_conv_dw_w, g_dw_w), ("conv_dw_b", conv_dw_b, m_conv_dw_b, v_conv_dw_b, g_dw_b),
             ("conv_ln_g", conv_ln_g, m_conv_ln_g, v_conv_ln_g, g_ln_g), ("conv_ln_b", conv_ln_b, m_conv_ln_b, v_conv_ln_b, g_ln_b),
             ("conv_pw2_b", conv_pw2_b, m_conv_pw2_b, v_conv_pw2_b, g_pw2_b),
             ("kv_mod_b", kv_mod_b, m_kv_mod_b, v_kv_mod_b, g_kv_mod_b), ("kv_norm_g", kv_norm_g, m_kv_norm_g, v_kv_norm_g, g_kv_norm),
             ("k_norm_g", k_norm_g, m_k_norm_g, v_k_norm_g, g_k_norm), ("q_norm_g", q_norm_g, m_q_norm_g, v_q_norm_g, g_q_norm),
             ("ffn_dw_w", ffn_dw_w, m_ffn_dw_w, v_ffn_dw_w, g_ffn_dw_w), ("ffn_dw_b", ffn_dw_b, m_ffn_dw_b, v_ffn_dw_b, g_ffn_dw_b)]
    n_small = sum(int(np.prod(s[1].shape)) for s in small)
    rows_small = -(-n_small // 1024) * 8

    def pack(idx):
        fl = jnp.concatenate([s[idx].reshape(-1) for s in small])
        return jnp.concatenate([fl, jnp.ones((rows_small * 128 - n_small,), f32)]).reshape(rows_small, 128)

    sd, sm, sv = _adamw_plain(pack(1), pack(2), pack(3), pack(4), "adamw_small")
    res = {}
    pos = 0
    for name, w, _, _, g in small:
        n = int(np.prod(w.shape))
        cut = lambda a: a.reshape(-1)[pos:pos + n].reshape(w.shape)
        res[name] = (g.reshape(w.shape), cut(sd), cut(sm), cut(sv))
        pos += n

    c_all_t = jnp.transpose(c_all)

    def mod_update(w2d, m2d, v2d, dm_cols, tag):
        g = _modgrad(c_all_t, dm_cols, f"modgrad_{tag}")
        d, m2, v2 = _adamw_plain(w2d, m2d, v2d, g, f"adamw_{tag}")
        return g, d, m2, v2

    mw = []
    for l in range(2):
        cols = lax.dynamic_slice_in_dim(dm_all[:, l * 6 * Dm:(l + 1) * 6 * Dm], me * n_mod, n_mod, 1)
        mw.append(mod_update(mod_w[l], m_mod_w[l], v_mod_w[l], cols, f"mod_w{l}"))
    res["mod_w"] = tuple(jnp.stack([mw[0][i], mw[1][i]]) for i in range(4))
    cols = lax.dynamic_slice_in_dim(dm_all[:, 12 * Dm:], me * n_kvm, n_kvm, 1)
    res["kv_mod_w"] = mod_update(kv_mod_w, m_kv_mod_w, v_kv_mod_w, cols, "kv_mod_w")

    def mine(part):
        return lax.dynamic_index_in_dim(part, chip[0], 0, keepdims=False)

    def big(key, w, m, v, l, prev, tag, comms=()):
        part, r2 = red[key]
        return _adamw_reduced(as3(w), as3(m), as3(v), mine(part), r2, l, prev, f"adamw_{tag}", comms=comms)

    up1 = big("ffn_up_w1", ffn_up_w, m_ffn_up_w, v_ffn_up_w, 1, None, "up1")
    res["ffn_up_w"] = tuple(big("ffn_up_w0", ffn_up_w, m_ffn_up_w, v_ffn_up_w, 0, up1, "up0"))
    down1 = big("ffn_down_w1", ffn_down_w, m_ffn_down_w, v_ffn_down_w, 1, None, "down1")
    res["ffn_down_w"] = tuple(big("ffn_down_w0", ffn_down_w, m_ffn_down_w, v_ffn_down_w, 0, down1, "down0"))
    for key, w, m, v in (("conv_pw1_w", conv_pw1_w, m_conv_pw1_w, v_conv_pw1_w), ("conv_pw2_w", conv_pw2_w, m_conv_pw2_w, v_conv_pw2_w),
                         ("w_kv", w_kv, m_w_kv, v_w_kv), ("w_q", w_q, m_w_q, v_w_q), ("w_o", w_o, m_w_o, v_w_o)):
        res[key] = tuple(o.reshape(w.shape) for o in big(key, w, m, v, 0, None, key))

    order = ["mod_w", "mod_b", "norm_mix_g", "norm_ffn_g", "conv_pw1_w", "conv_pw1_b", "conv_dw_w", "conv_dw_b", "conv_ln_g",
             "conv_ln_b", "conv_pw2_w", "conv_pw2_b", "kv_mod_w", "kv_mod_b", "kv_norm_g", "w_kv", "k_norm_g", "w_q", "q_norm_g",
             "w_o", "ffn_up_w", "ffn_dw_w", "ffn_dw_b", "ffn_down_w"]
    out = [loss, grad_x.reshape(x.shape)]
    for i in range(4):
        out += [res[n][i] for n in order]
    return tuple(out)
```

```python
import functools
import math

import numpy as np
import jax
import jax.numpy as jnp
from jax import lax
from jax.experimental import pallas as pl
from jax.experimental.pallas import tpu as pltpu

f32 = jnp.float32
bf16 = jnp.bfloat16

D = 2048
SEQ = 8192
FF = 5632
CONV_K = 31
FFN_K = 3
HPG = 8
DH = 128
NG = 3
DILS = (1, 4, 16)
BLK = 128
ROT = 32
THETA = 500000.0
EPS = 1e-6
NEG = -1e30
NDEV = 8
HALO = 32
FHALO = 16
FFN_TM = 1024

LR, B1, B2, AEPS, WD, STEP = 0.001, 0.9, 0.999, 1e-08, 0.01, 10

VMEM_BIG = 56 * 1024 * 1024

ARB = "arbitrary"
PAR = "parallel"
MESH = pl.DeviceIdType.MESH


def _cp(sem, vmem=None):
    return pltpu.CompilerParams(dimension_semantics=sem, vmem_limit_bytes=vmem)


def _tile(n, pref, mult=128):
    if n <= pref:
        return n
    t = (pref // mult) * mult
    while t >= mult:
        if n % t == 0:
            return t
        t -= mult
    return n


def _sigmoid(x):
    return 1.0 / (1.0 + jnp.exp(-x))


def _me():
    return lax.axis_index("x"), lax.axis_index("y"), lax.axis_index("c")


class _Comm:
    def __init__(self, arrays, out_shapes, sems, start, finish, mid=None):
        self.arrays, self.out_shapes, self.sems, self.start, self.finish = arrays, out_shapes, sems, start, finish
        self.mid = mid


def _pcall(body, *, name, grid, in_specs, out_specs, out_shape, args, scratch_shapes=(), sem=None, vmem=None, comms=(),
           aliases=None):
    aliases = aliases or {}
    if not comms:
        return pl.pallas_call(body, name=name, grid=grid, in_specs=in_specs, out_specs=out_specs, out_shape=out_shape,
                              scratch_shapes=list(scratch_shapes), input_output_aliases=aliases,
                              compiler_params=_cp(sem, vmem))(*args)
    single = not isinstance(out_shape, (list, tuple))
    outs_shape = [out_shape] if single else list(out_shape)
    outs_spec = [out_specs] if single else list(out_specs)
    n_in, n_out, n_scr = len(args), len(outs_shape), len(scratch_shapes)
    c_in = [a for cm in comms for a in cm.arrays]
    c_out = [s for cm in comms for s in cm.out_shapes]
    c_scr = [s for cm in comms for s in cm.sems]
    total = int(np.prod(grid))
    late = total - 1 - max(1, total // 8) if total >= 8 else None

    def split(refs, counts):
        out, pos = [], 0
        for n in counts:
            out.append(refs[pos:pos + n])
            pos += n
        return out

    def wrapped(*refs):
        ins, cins, outs, couts, scr, cscr = split(refs, [n_in, len(c_in), n_out, len(c_out), n_scr, len(c_scr)])
        ids = [pl.program_id(a) for a in range(len(grid))]
        first = functools.reduce(jnp.logical_and, [i == 0 for i in ids])
        last = functools.reduce(jnp.logical_and, [i == g - 1 for i, g in zip(ids, grid)])
        per_in = split(cins, [len(cm.arrays) for cm in comms])
        per_out = split(couts, [len(cm.out_shapes) for cm in comms])
        per_sem = split(cscr, [len(cm.sems) for cm in comms])

        @pl.when(first)
        def _():
            for cm, a, b, s in zip(comms, per_in, per_out, per_sem):
                cm.start(a, b, s)

        if late is not None:
            step = functools.reduce(lambda acc, ig: acc * ig[1] + ig[0], zip(ids, grid), 0)

            @pl.when(step == late)
            def _():
                for cm, a, b, s in zip(comms, per_in, per_out, per_sem):
                    if cm.mid is not None:
                        cm.mid(a, b, s)

        body(*ins, *outs, *scr)

        @pl.when(last)
        def _():
            for cm, a, b, s in zip(comms, per_in, per_out, per_sem):
                if late is None and cm.mid is not None:
                    cm.mid(a, b, s)
                cm.finish(a, b, s)

    hbm = pl.BlockSpec(memory_space=pl.ANY)
    res = pl.pallas_call(
        wrapped, name=name, grid=grid, in_specs=list(in_specs) + [hbm] * len(c_in),
        out_specs=outs_spec + [hbm] * len(c_out), out_shape=outs_shape + c_out,
        scratch_shapes=list(scratch_shapes) + c_scr, input_output_aliases=aliases,
        compiler_params=_cp((ARB,) * len(grid), vmem))(*args, *c_in)
    main = res[0] if single else list(res[:n_out])
    return main, split(list(res[n_out:]), [len(cm.out_shapes) for cm in comms])


def _comm_allgather(w, axis):
    n = w.shape[axis]
    out_shape = list(w.shape)
    out_shape[axis] = NDEV * n

    def parts(ins, outs, sems):
        x_ref, out_ref = ins[0], outs[0]
        send_sems, recv_sems, local_sem = sems
        mx, my, mc = _me()
        chips = [(1 - mx, my), (mx, 1 - my), (1 - mx, 1 - my)]

        def blk(px, py, pc):
            start = pl.multiple_of((4 * px + 2 * py + pc) * n, n)
            if axis == 1:
                return out_ref.at[:, pl.ds(start, n), :]
            return out_ref.at[:, :, pl.ds(start, n)]

        def copy(k, block, to, src=None):
            return pltpu.make_async_remote_copy(
                src_ref=blk(*block) if src is None else src, dst_ref=blk(*block),
                send_sem=send_sems.at[k], recv_sem=recv_sems.at[k], device_id=to, device_id_type=MESH)

        me, sibling = (mx, my, mc), (mx, my, 1 - mc)
        mine = pltpu.make_async_copy(x_ref, blk(*me), local_sem)
        first = [copy(0, me, sibling, src=x_ref)] + [copy(1 + j, me, (*chip, mc), src=x_ref) for j, chip in enumerate(chips)]
        passed = [copy(4 + j, (*chip, mc), sibling) for j, chip in enumerate(chips)]
        return me, sibling, chips, mc, copy, mine, first, passed

    def start(ins, outs, sems):
        *_, mine, first, _ = parts(ins, outs, sems)
        mine.start()
        for cp in first:
            cp.start()

    def mid(ins, outs, sems):
        me, sibling, chips, mc, copy, mine, first, passed = parts(ins, outs, sems)
        for j, chip in enumerate(chips):
            copy(1 + j, (*chip, mc), me).wait_recv()
            passed[j].start()

    def finish(ins, outs, sems):
        me, sibling, chips, mc, copy, mine, first, passed = parts(ins, outs, sems)
        copy(0, sibling, me).wait_recv()
        for j, chip in enumerate(chips):
            copy(4 + j, (*chip, 1 - mc), me).wait_recv()
        for cp in first + passed:
            cp.wait_send()
        mine.wait()

    return _Comm([w], [jax.ShapeDtypeStruct(tuple(out_shape), w.dtype)],
                 [pltpu.SemaphoreType.DMA((7,)), pltpu.SemaphoreType.DMA((7,)), pltpu.SemaphoreType.DMA], start, finish, mid)


def _comm_rs_sibling(dwb):
    def copies(ins, outs, sems):
        mx, my, mc = _me()
        return [pltpu.make_async_remote_copy(
            src_ref=ins[0].at[2 * p + (1 - mc)], dst_ref=outs[0].at[p], send_sem=sems[0].at[p], recv_sem=sems[1].at[p],
            device_id=(mx, my, 1 - mc), device_id_type=MESH) for p in range(4)]

    def start(ins, outs, sems):
        for cp in copies(ins, outs, sems):
            cp.start()

    def finish(ins, outs, sems):
        cps = copies(ins, outs, sems)
        for cp in cps:
            cp.wait_recv()
        for cp in cps:
            cp.wait_send()

    return _Comm([dwb], [jax.ShapeDtypeStruct((4,) + dwb.shape[1:], dwb.dtype)],
                 [pltpu.SemaphoreType.DMA((4,)), pltpu.SemaphoreType.DMA((4,))], start, finish)


def _comm_rs_chips(part):
    def copies(ins, outs, sems):
        mx, my, mc = _me()
        chips = [(1 - mx, my), (mx, 1 - my), (1 - mx, 1 - my)]
        return [pltpu.make_async_remote_copy(
            src_ref=ins[0].at[2 * px + py], dst_ref=outs[0].at[k], send_sem=sems[0].at[k], recv_sem=sems[1].at[k],
            device_id=(px, py, mc), device_id_type=MESH) for k, (px, py) in enumerate(chips)]

    def start(ins, outs, sems):
        for cp in copies(ins, outs, sems):
            cp.start()

    def finish(ins, outs, sems):
        cps = copies(ins, outs, sems)
        for cp in cps:
            cp.wait_recv()
        for cp in cps:
            cp.wait_send()

    return _Comm([part], [jax.ShapeDtypeStruct((3,) + part.shape[1:], part.dtype)],
                 [pltpu.SemaphoreType.DMA((3,)), pltpu.SemaphoreType.DMA((3,))], start, finish)


def _ag_small(x, name):
    r, c = x.shape

    def body(x_ref, out_ref, send_sems, recv_sems):
        mx, my, mc = _me()
        mine = 4 * mx + 2 * my + mc
        out_ref[mine] = x_ref[...]
        copies = []
        for k in range(1, NDEV):
            px = 1 - mx if (k >> 2) & 1 else mx
            py = 1 - my if (k >> 1) & 1 else my
            pc = 1 - mc if k & 1 else mc
            cp = pltpu.make_async_remote_copy(
                src_ref=x_ref, dst_ref=out_ref.at[mine], send_sem=send_sems.at[k - 1], recv_sem=recv_sems.at[k - 1],
                device_id=(px, py, pc), device_id_type=MESH)
            cp.start()
            copies.append((cp, 4 * px + 2 * py + pc))
        for k, (cp, peer) in enumerate(copies):
            pltpu.make_async_remote_copy(
                src_ref=x_ref, dst_ref=out_ref.at[peer], send_sem=send_sems.at[k], recv_sem=recv_sems.at[k],
                device_id=(mx, my, mc), device_id_type=MESH).wait_recv()
        for cp, _ in copies:
            cp.wait_send()

    return pl.pallas_call(
        body, name=name,
        out_shape=jax.ShapeDtypeStruct((NDEV, r, c), x.dtype),
        in_specs=[pl.BlockSpec(memory_space=pltpu.VMEM)],
        out_specs=pl.BlockSpec(memory_space=pltpu.VMEM),
        scratch_shapes=[pltpu.SemaphoreType.DMA((NDEV - 1,)), pltpu.SemaphoreType.DMA((NDEV - 1,))],
    )(x)


def _ag_big(w, axis, name):
    n = w.shape[axis]
    out_shape = list(w.shape)
    out_shape[axis] = NDEV * n

    def body(x_ref, out_ref, send_sems, recv_sems, local_sem):
        mx, my, mc = _me()
        me, sibling = (mx, my, mc), (mx, my, 1 - mc)
        chips = [(1 - mx, my), (mx, 1 - my), (1 - mx, 1 - my)]

        def blk(px, py, pc):
            start = pl.multiple_of((4 * px + 2 * py + pc) * n, n)
            if axis == 1:
                return out_ref.at[:, pl.ds(start, n), :]
            return out_ref.at[:, :, pl.ds(start, n)]

        def copy(k, block, to, src=None):
            return pltpu.make_async_remote_copy(
                src_ref=blk(*block) if src is None else src, dst_ref=blk(*block),
                send_sem=send_sems.at[k], recv_sem=recv_sems.at[k], device_id=to, device_id_type=MESH)

        mine = pltpu.make_async_copy(x_ref, blk(*me), local_sem)
        mine.start()
        first = [copy(0, me, sibling, src=x_ref)]
        first += [copy(1 + j, me, (*chip, mc), src=x_ref) for j, chip in enumerate(chips)]
        for cp in first:
            cp.start()
        passed = [copy(4 + j, (*chip, mc), sibling) for j, chip in enumerate(chips)]
        for j, chip in enumerate(chips):
            copy(1 + j, (*chip, mc), me).wait_recv()
            passed[j].start()
        copy(0, sibling, me).wait_recv()
        for j, chip in enumerate(chips):
            copy(4 + j, (*chip, 1 - mc), me).wait_recv()
        for cp in first + passed:
            cp.wait_send()
        mine.wait()

    return pl.pallas_call(
        body, name=name,
        out_shape=jax.ShapeDtypeStruct(tuple(out_shape), w.dtype),
        in_specs=[pl.BlockSpec(memory_space=pl.ANY)],
        out_specs=pl.BlockSpec(memory_space=pl.ANY),
        scratch_shapes=[pltpu.SemaphoreType.DMA((7,)), pltpu.SemaphoreType.DMA((7,)), pltpu.SemaphoreType.DMA],
    )(w)


def _chip_partial(dwb, r1, core, name):
    _, A, B = dwb.shape
    ta = _tile(A, 512, 16)

    def body(c_ref, a_ref, b_ref, o_ref):
        o_ref[...] = (a_ref[...].astype(f32) + b_ref[...].astype(f32)).astype(o_ref.dtype)

    grid_spec = pltpu.PrefetchScalarGridSpec(
        num_scalar_prefetch=1, grid=(4, A // ta),
        in_specs=[pl.BlockSpec((None, ta, B), lambda p, i, c: (2 * p + c[0], i, 0)),
                  pl.BlockSpec((None, ta, B), lambda p, i, c: (p, i, 0))],
        out_specs=pl.BlockSpec((None, ta, B), lambda p, i, c: (p, i, 0)))
    return pl.pallas_call(body, name=name, grid_spec=grid_spec,
                          out_shape=jax.ShapeDtypeStruct((4, A, B), dwb.dtype),
                          compiler_params=_cp((PAR, PAR)))(core, dwb, r1)


def _adam_math(w, g, m, v):
    m2 = B1 * m + (1.0 - B1) * g
    v2 = B2 * v + (1.0 - B2) * (g * g)
    m_hat = m2 / (1.0 - B1 ** STEP)
    v_hat = v2 / (1.0 - B2 ** STEP)
    delta = -LR * (m_hat / (jnp.sqrt(v_hat) + AEPS) + WD * w)
    return delta, m2, v2


def _adamw_reduced(w, m, v, mine, r2, l, prev, name, comms=()):
    L, A, B = w.shape
    ta = _tile(A, 256, 8)

    def body(w_ref, m_ref, v_ref, p_ref, r_ref, *rest):
        g_out, d_out, m_out, v_out = rest[-4:]
        g = ((p_ref[...].astype(f32) + r_ref[0].astype(f32)) + r_ref[1].astype(f32)) + r_ref[2].astype(f32)
        d, m2, v2 = _adam_math(w_ref[...], g, m_ref[...], v_ref[...])
        g_out[...] = g
        d_out[...] = d
        m_out[...] = m2
        v_out[...] = v2

    wspec = pl.BlockSpec((None, ta, B), lambda i: (l, i, 0))
    in_specs = [wspec, wspec, wspec, pl.BlockSpec((ta, B), lambda i: (i, 0)), pl.BlockSpec((3, ta, B), lambda i: (0, i, 0))]
    args = [w, m, v, mine, r2]
    aliases = {}
    if prev is not None:
        in_specs += [pl.BlockSpec(memory_space=pl.ANY)] * 4
        args += list(prev)
        aliases = {5 + i: i for i in range(4)}
    shp = jax.ShapeDtypeStruct((L, A, B), f32)
    return _pcall(body, name=name, grid=(A // ta,), in_specs=in_specs, out_specs=[wspec] * 4, out_shape=[shp] * 4,
                  args=args, sem=(PAR,), comms=comms, aliases=aliases)


def _adamw_plain(w, m, v, g, name):
    A, B = w.shape
    ta = _tile(A, 256, 8)

    def body(w_ref, m_ref, v_ref, g_ref, d_out, m_out, v_out):
        d, m2, v2 = _adam_math(w_ref[...], g_ref[...], m_ref[...], v_ref[...])
        d_out[...] = d
        m_out[...] = m2
        v_out[...] = v2

    spec = pl.BlockSpec((ta, B), lambda i: (i, 0))
    shp = jax.ShapeDtypeStruct((A, B), f32)
    return pl.pallas_call(body, name=name, grid=(A // ta,), in_specs=[spec] * 4, out_specs=[spec] * 3,
                          out_shape=[shp, shp, shp], compiler_params=_cp((PAR,)))(w, m, v, g)


def _sum8(g, name):
    _, R, C = g.shape

    def body(g_ref, o_ref):
        acc = g_ref[0]
        for j in range(1, NDEV):
            acc = acc + g_ref[j]
        o_ref[...] = acc

    return pl.pallas_call(body, name=name, out_shape=jax.ShapeDtypeStruct((R, C), f32))(g)


def _modproj(c_all, w, bias, name):
    K, N = w.shape
    tn = _tile(N, 512)

    def body(c_ref, w_ref, b_ref, o_ref):
        cc = c_ref[...]
        sc = (cc * _sigmoid(cc)).astype(bf16)
        o_ref[...] = jnp.dot(sc, w_ref[...].astype(bf16), preferred_element_type=f32) + b_ref[...]

    return pl.pallas_call(
        body, name=name, grid=(N // tn,),
        in_specs=[pl.BlockSpec((NDEV, K), lambda j: (0, 0)), pl.BlockSpec((K, tn), lambda j: (0, j)),
                  pl.BlockSpec((1, tn), lambda j: (0, j))],
        out_specs=pl.BlockSpec((NDEV, tn), lambda j: (0, j)),
        out_shape=jax.ShapeDtypeStruct((NDEV, N), f32), compiler_params=_cp((PAR,)))(c_all, w, bias)


def _modgrad(c_all_t, dm, name):
    K = c_all_t.shape[0]
    N = dm.shape[1]
    tn = _tile(N, 512)

    def body(c_ref, d_ref, o_ref):
        cc = c_ref[...]
        sc = cc * _sigmoid(cc)
        dmv = d_ref[...]
        acc = sc[:, 0:1] * dmv[0:1, :]
        for b in range(1, NDEV):
            acc = acc + sc[:, b:b + 1] * dmv[b:b + 1, :]
        o_ref[...] = acc

    return pl.pallas_call(
        body, name=name, grid=(N // tn,),
        in_specs=[pl.BlockSpec((K, NDEV), lambda j: (0, 0)), pl.BlockSpec((NDEV, tn), lambda j: (0, j))],
        out_specs=pl.BlockSpec((K, tn), lambda j: (0, j)),
        out_shape=jax.ShapeDtypeStruct((K, N), f32), compiler_params=_cp((PAR,)))(c_all_t, dm)


def _mm_nn(a, w, l, *, name, out_dtype=bf16, bias=None, res=None, gate=None, tm=1024, tn=1024, tk=2048, comms=()):
    M, K = a.shape
    N = w.shape[2]
    tm, tn, tk = _tile(M, tm, 8), _tile(N, tn), _tile(K, tk)
    nk = K // tk
    epi = res is not None

    def body(*refs):
        it = iter(refs)
        a_ref, w_ref = next(it), next(it)
        b_ref = next(it) if bias is not None else None
        r_ref = next(it) if epi else None
        g_ref = next(it) if epi else None
        o_ref = next(it)
        f_ref = next(it) if epi else None

        def finish(y):
            if b_ref is not None:
                y = y + b_ref[...]
            if epi:
                f_ref[...] = y.astype(f_ref.dtype)
                o_ref[...] = r_ref[...] + g_ref[...] * y
            else:
                o_ref[...] = y.astype(o_ref.dtype)

        if nk == 1:
            finish(jnp.dot(a_ref[...], w_ref[...], preferred_element_type=f32))
            return
        acc = next(it)
        k = pl.program_id(2)

        @pl.when(k == 0)
        def _():
            acc[...] = jnp.zeros_like(acc)

        acc[...] += jnp.dot(a_ref[...], w_ref[...], preferred_element_type=f32)

        @pl.when(k == nk - 1)
        def _():
            finish(acc[...])

    in_specs = [pl.BlockSpec((tm, tk), lambda i, j, k: (i, k)), pl.BlockSpec((None, tk, tn), lambda i, j, k: (l, k, j))]
    args = [a, w]
    if bias is not None:
        in_specs.append(pl.BlockSpec((1, tn), lambda i, j, k: (0, j)))
        args.append(bias)
    ospec = pl.BlockSpec((tm, tn), lambda i, j, k: (i, j))
    if epi:
        in_specs += [ospec, pl.BlockSpec((1, tn), lambda i, j, k: (0, j))]
        args += [res, gate]
        out_shape = [jax.ShapeDtypeStruct((M, N), f32), jax.ShapeDtypeStruct((M, N), bf16)]
        out_specs = [ospec, ospec]
    else:
        out_shape = jax.ShapeDtypeStruct((M, N), out_dtype)
        out_specs = ospec
    return _pcall(body, name=name, grid=(M // tm, N // tn, nk), in_specs=in_specs, out_specs=out_specs, out_shape=out_shape,
                  args=args, scratch_shapes=[pltpu.VMEM((tm, tn), f32)] if nk > 1 else [], sem=(PAR, PAR, ARB), vmem=VMEM_BIG,
                  comms=comms)


def _mm_nt(a, w, l, *, name, out_dtype, tm=1024, tko=2048, tn=1024, comms=()):
    planes = a.ndim == 3
    M = a.shape[-2]
    K, N = w.shape[1], w.shape[2]
    npl = a.shape[-1]
    tm, tko = _tile(M, tm, 8), _tile(K, tko)
    tn = _tile(npl, tn)
    nn = N // tn
    per_plane = npl // tn

    def body(a_ref, w_ref, o_ref, *scratch):
        if nn == 1:
            o_ref[...] = _dot_nt(a_ref[...], w_ref[...]).astype(o_ref.dtype)
            return
        acc = scratch[0]
        k = pl.program_id(2)

        @pl.when(k == 0)
        def _():
            acc[...] = jnp.zeros_like(acc)

        acc[...] += _dot_nt(a_ref[...], w_ref[...])

        @pl.when(k == nn - 1)
        def _():
            o_ref[...] = acc[...].astype(o_ref.dtype)

    if planes:
        a_spec = pl.BlockSpec((None, tm, tn), lambda i, j, k: (k // per_plane, i, k % per_plane))
    else:
        a_spec = pl.BlockSpec((tm, tn), lambda i, j, k: (i, k))
    return _pcall(body, name=name, grid=(M // tm, K // tko, nn),
                  in_specs=[a_spec, pl.BlockSpec((None, tko, tn), lambda i, j, k: (l, j, k))],
                  out_specs=pl.BlockSpec((tm, tko), lambda i, j, k: (i, j)),
                  out_shape=jax.ShapeDtypeStruct((M, K), out_dtype), args=[a, w],
                  scratch_shapes=[pltpu.VMEM((tm, tko), f32)] if nn > 1 else [], sem=(PAR, PAR, ARB), vmem=VMEM_BIG,
                  comms=comms)


def _mm_tn(a, b, *, name, col_sharded, comms=()):
    planes = b.ndim == 3
    S, K = a.shape
    N = b.shape[-1] * (2 if planes else 1)
    if col_sharded:
        tn, tk, ts = N // NDEV, _tile(K, 1024), _tile(S, 2048, 16)
    else:
        tn, tk, ts = N, _tile(K, 1408), _tile(S, 1024, 16)
    ns_steps = S // ts
    per_plane = (b.shape[-1] // tn) if planes else 0

    def body(a_ref, b_ref, o_ref, acc):
        s = pl.program_id(2)

        @pl.when(s == 0)
        def _():
            acc[...] = jnp.zeros_like(acc)

        acc[...] += lax.dot_general(a_ref[...], b_ref[...], (((0,), (0,)), ((), ())), preferred_element_type=f32)

        @pl.when(s == ns_steps - 1)
        def _():
            o_ref[...] = acc[...].astype(o_ref.dtype)

    if planes:
        b_spec = pl.BlockSpec((None, ts, tn), lambda k, n, s: (n // per_plane, s, n % per_plane))
    else:
        b_spec = pl.BlockSpec((ts, tn), lambda k, n, s: (s, n))
    if col_sharded:
        out_shape = jax.ShapeDtypeStruct((NDEV, K, tn), bf16)
        out_spec = pl.BlockSpec((None, tk, tn), lambda k, n, s: (n, k, 0))
    else:
        out_shape = jax.ShapeDtypeStruct((K, N), bf16)
        out_spec = pl.BlockSpec((tk, tn), lambda k, n, s: (k, n))
    res = _pcall(body, name=name, grid=(K // tk, N // tn, ns_steps),
                 in_specs=[pl.BlockSpec((ts, tk), lambda k, n, s: (s, k)), b_spec],
                 out_specs=out_spec, out_shape=out_shape, args=[a, b],
                 scratch_shapes=[pltpu.VMEM((tk, tn), f32)], sem=(PAR, PAR, ARB), vmem=VMEM_BIG, comms=comms)
    out, couts = res if comms else (res, None)
    if not col_sharded:
        out = out.reshape(NDEV, K // NDEV, N)
    return (out, couts) if comms else out


def _acc_spec(w, rows=1):
    return pl.BlockSpec((rows, w), lambda i: (0, 0))


def _mod_fwd(x, g, sh, sc, name, comms=()):
    S, W = x.shape
    tm = _tile(S, 256, 8)

    def body(x_ref, g_ref, sh_ref, sc_ref, h_ref):
        xv = x_ref[...]
        r = lax.rsqrt(jnp.mean(xv * xv, axis=-1, keepdims=True) + EPS)
        h_ref[...] = ((xv * r) * g_ref[...] * (1.0 + sc_ref[...]) + sh_ref[...]).astype(h_ref.dtype)

    row = pl.BlockSpec((tm, W), lambda i: (i, 0))
    return _pcall(body, name=name, grid=(S // tm,), in_specs=[row, _acc_spec(W), _acc_spec(W), _acc_spec(W)],
                  out_specs=row, out_shape=jax.ShapeDtypeStruct((S, W), bf16), args=[x, g, sh, sc], sem=(PAR,), comms=comms)


def _mod_bwd(dh, x, dx_in, g, sc, name, comms=()):
    S, W = x.shape
    tm = _tile(S, 256, 8)
    nt = S // tm

    def body(dh_ref, x_ref, dxi_ref, g_ref, sc_ref, dx_ref, dsh_ref, dsc_ref, dg_ref):
        i = pl.program_id(0)

        @pl.when(i == 0)
        def _():
            dsh_ref[...] = jnp.zeros_like(dsh_ref)
            dsc_ref[...] = jnp.zeros_like(dsc_ref)

        xv = x_ref[...]
        dh = dh_ref[...].astype(f32)
        r = lax.rsqrt(jnp.mean(xv * xv, axis=-1, keepdims=True) + EPS)
        n = xv * r
        dn = dh * (g_ref[...] * (1.0 + sc_ref[...]))
        dx = r * (dn - n * jnp.mean(dn * n, axis=-1, keepdims=True))
        dx_ref[...] = dxi_ref[...] + dx
        dsh_ref[...] += jnp.sum(dh, axis=0, keepdims=True)
        dsc_ref[...] += jnp.sum(dh * n, axis=0, keepdims=True)

        @pl.when(i == nt - 1)
        def _():
            a2 = dsc_ref[...]
            dg_ref[...] = a2 * (1.0 + sc_ref[...])
            dsc_ref[...] = a2 * g_ref[...]

    row = pl.BlockSpec((tm, W), lambda i: (i, 0))
    vec = jax.ShapeDtypeStruct((1, W), f32)
    return _pcall(body, name=name, grid=(nt,), in_specs=[row, row, row, _acc_spec(W), _acc_spec(W)],
                  out_specs=[row, _acc_spec(W), _acc_spec(W), _acc_spec(W)],
                  out_shape=[jax.ShapeDtypeStruct((S, W), f32), vec, vec, vec], args=[dh, x, dx_in, g, sc], sem=(ARB,),
                  comms=comms)


def _gate_bwd(dx, f, gate, name):
    S, W = dx.shape
    tm = _tile(S, 256, 16)

    def body(dx_ref, f_ref, g_ref, df_ref, dg_ref, sdf_ref):
        i = pl.program_id(0)

        @pl.when(i == 0)
        def _():
            dg_ref[...] = jnp.zeros_like(dg_ref)
            sdf_ref[...] = jnp.zeros_like(sdf_ref)

        d = dx_ref[...]
        df = g_ref[...] * d
        df_ref[...] = df.astype(df_ref.dtype)
        dg_ref[...] += jnp.sum(d * f_ref[...].astype(f32), axis=0, keepdims=True)
        sdf_ref[...] += jnp.sum(df, axis=0, keepdims=True)

    row = pl.BlockSpec((tm, W), lambda i: (i, 0))
    vec = jax.ShapeDtypeStruct((1, W), f32)
    return pl.pallas_call(
        body, name=name, grid=(S // tm,), in_specs=[row, row, _acc_spec(W)], out_specs=[row, _acc_spec(W), _acc_spec(W)],
        out_shape=[jax.ShapeDtypeStruct((S, W), bf16), vec, vec], compiler_params=_cp((ARB,)))(dx, f, gate)


def _loss_grad(y, target, name):
    S, W = y.shape
    tm = _tile(S, 256, 8)

    def body(y_ref, t_ref, dy_ref, l_ref):
        i = pl.program_id(0)

        @pl.when(i == 0)
        def _():
            l_ref[...] = jnp.zeros_like(l_ref)

        e = y_ref[...] - t_ref[...]
        dy_ref[...] = e * (1.0 / W)
        l_ref[...] += 0.5 * jnp.sum(jnp.mean(e * e, axis=-1, keepdims=True))

    row = pl.BlockSpec((tm, W), lambda i: (i, 0))
    return pl.pallas_call(
        body, name=name, grid=(S // tm,), in_specs=[row, row], out_specs=[row, pl.BlockSpec((8, 128), lambda i: (0, 0))],
        out_shape=[jax.ShapeDtypeStruct((S, W), f32), jax.ShapeDtypeStruct((8, 128), f32)],
        compiler_params=_cp((ARB,)))(y, target)


def _tap_groups(offsets):
    groups = {}
    for k, o in enumerate(offsets):
        groups.setdefault(o % 8, []).append((k, o - o % 8))
    return sorted(groups.items())


def _tap_sum(buf, w, offsets, tm):
    out = None
    for b, taps in _tap_groups(offsets):
        n = tm + 8 if b else tm
        y = None
        for k, base in taps:
            term = w[k:k + 1, :] * buf[pl.ds(base, n), :]
            y = term if y is None else y + term
        part = y[b:b + tm] if b else y
        out = part if out is None else out + part
    return out


def _tap_wgrad(d, buf, dsh, acc_ref, offsets, tm):
    for b, taps in _tap_groups(offsets):
        if b:
            dsh[pl.ds(0, 8), :] = jnp.zeros((8, dsh.shape[1]), f32)
            dsh[pl.ds(tm, 8), :] = jnp.zeros((8, dsh.shape[1]), f32)
            dsh[pl.ds(b, tm), :] = d
            dd, n = dsh[...], tm + 8
        else:
            dd, n = d, tm
        for k, base in taps:
            acc_ref[pl.ds(k, 1), :] += jnp.sum(dd * buf[pl.ds(base, n), :], axis=0, keepdims=True)


_CONV_OFFSETS = [HALO - (CONV_K - 1) + k for k in range(CONV_K)]
_CONV_OFFSETS_T = [CONV_K - 1 - k for k in range(CONV_K)]


def _conv_core(u_ref, uh_ref, w_ref, b_ref, lg_ref, lb_ref, gbuf, tm, first, cv_ref=None):
    C = u_ref.shape[1] // 2
    u = u_ref[...].astype(f32)
    uh = uh_ref[...].astype(f32)
    gbuf[pl.ds(HALO, tm), :] = u[:, :C] * _sigmoid(u[:, C:])
    halo = uh[:, :C] * _sigmoid(uh[:, C:])
    gbuf[pl.ds(0, HALO), :] = jnp.where(first, 0.0, halo)
    cv = _tap_sum(gbuf, w_ref[...], _CONV_OFFSETS, tm) + b_ref[...] if cv_ref is None else cv_ref[...]
    mu = jnp.mean(cv, axis=-1, keepdims=True)
    xc = cv - mu
    rstd = lax.rsqrt(jnp.mean(xc * xc, axis=-1, keepdims=True) + EPS)
    z = xc * rstd
    ln = z * lg_ref[...] + lb_ref[...]
    return cv, z, rstd, ln


def _halo_prev(tm, hb, w):
    return pl.BlockSpec((hb, w), lambda i: (jnp.maximum(i * (tm // hb) - 1, 0), 0))


def _conv_fwd(u, w, b, lg, lb, name, comms=()):
    S, C2 = u.shape
    C = C2 // 2
    tm = _tile(S, 256, HALO)

    def body(u_ref, uh_ref, w_ref, b_ref, lg_ref, lb_ref, s_ref, cv_ref, gbuf):
        first = pl.program_id(0) == 0
        cv, _, _, ln = _conv_core(u_ref, uh_ref, w_ref, b_ref, lg_ref, lb_ref, gbuf, tm, first)
        s_ref[...] = (ln * _sigmoid(ln)).astype(s_ref.dtype)
        cv_ref[...] = cv

    return _pcall(body, name=name, grid=(S // tm,),
                  in_specs=[pl.BlockSpec((tm, C2), lambda i: (i, 0)), _halo_prev(tm, HALO, C2), _acc_spec(C, 32),
                            _acc_spec(C), _acc_spec(C), _acc_spec(C)],
                  out_specs=[pl.BlockSpec((tm, C), lambda i: (i, 0))] * 2,
                  out_shape=[jax.ShapeDtypeStruct((S, C), bf16), jax.ShapeDtypeStruct((S, C), f32)],
                  args=[u, u, w, b, lg, lb], scratch_shapes=[pltpu.VMEM((tm + HALO, C), f32)], sem=(PAR,), vmem=VMEM_BIG,
                  comms=comms)


def _conv_bwd1(u, cv, ds, w, b, lg, lb, name, comms=()):
    S, C2 = u.shape
    C = C2 // 2
    tm = _tile(S, 256, HALO)

    def body(u_ref, uh_ref, cv_ref, ds_ref, w_ref, b_ref, lg_ref, lb_ref, dcv_ref, dlg_ref, dlb_ref, ddb_ref, ddw_ref, gbuf, dsh):
        i = pl.program_id(0)

        @pl.when(i == 0)
        def _():
            dlg_ref[...] = jnp.zeros_like(dlg_ref)
            dlb_ref[...] = jnp.zeros_like(dlb_ref)
            ddb_ref[...] = jnp.zeros_like(ddb_ref)
            ddw_ref[...] = jnp.zeros_like(ddw_ref)

        _, z, rstd, ln = _conv_core(u_ref, uh_ref, w_ref, b_ref, lg_ref, lb_ref, gbuf, tm, i == 0, cv_ref)
        sg = _sigmoid(ln)
        dln = ds_ref[...].astype(f32) * (sg * (1.0 + ln * (1.0 - sg)))
        dlg_ref[...] += jnp.sum(dln * z, axis=0, keepdims=True)
        dlb_ref[...] += jnp.sum(dln, axis=0, keepdims=True)
        dz = dln * lg_ref[...]
        dcv = rstd * (dz - jnp.mean(dz, axis=-1, keepdims=True) - z * jnp.mean(dz * z, axis=-1, keepdims=True))
        dcv_ref[...] = dcv
        ddb_ref[...] += jnp.sum(dcv, axis=0, keepdims=True)
        _tap_wgrad(dcv, gbuf, dsh, ddw_ref, _CONV_OFFSETS, tm)

    vec = jax.ShapeDtypeStruct((1, C), f32)
    return _pcall(
        body, name=name, grid=(S // tm,),
        in_specs=[pl.BlockSpec((tm, C2), lambda i: (i, 0)), _halo_prev(tm, HALO, C2), pl.BlockSpec((tm, C), lambda i: (i, 0)),
                  pl.BlockSpec((tm, C), lambda i: (i, 0)), _acc_spec(C, 32), _acc_spec(C), _acc_spec(C), _acc_spec(C)],
        out_specs=[pl.BlockSpec((tm, C), lambda i: (i, 0)), _acc_spec(C), _acc_spec(C), _acc_spec(C), _acc_spec(C, 32)],
        out_shape=[jax.ShapeDtypeStruct((S, C), f32), vec, vec, vec, jax.ShapeDtypeStruct((32, C), f32)],
        args=[u, u, cv, ds, w, b, lg, lb], scratch_shapes=[pltpu.VMEM((tm + HALO, C), f32), pltpu.VMEM((tm + 8, C), f32)],
        sem=(ARB,), vmem=VMEM_BIG, comms=comms)


def _conv_bwd2(dcv, u, w, name, comms=()):
    S, C2 = u.shape
    C = C2 // 2
    tm = _tile(S, 256, HALO)
    nt = S // tm
    nhb = S // HALO

    def body(d_ref, dn_ref, u_ref, w_ref, du_ref, db_ref, dbuf):
        i = pl.program_id(0)

        @pl.when(i == 0)
        def _():
            db_ref[...] = jnp.zeros_like(db_ref)

        dbuf[pl.ds(0, tm), :] = d_ref[...]
        dbuf[pl.ds(tm, HALO), :] = jnp.where(i == nt - 1, 0.0, dn_ref[...])
        dglu = _tap_sum(dbuf, w_ref[...], _CONV_OFFSETS_T, tm)
        u = u_ref[...].astype(f32)
        a, gt = u[:, :C], u[:, C:]
        sg = _sigmoid(gt)
        da = dglu * sg
        dgt = dglu * a * sg * (1.0 - sg)
        du_ref[:, :C] = da.astype(du_ref.dtype)
        du_ref[:, C:] = dgt.astype(du_ref.dtype)
        db_ref[:, :C] += jnp.sum(da, axis=0, keepdims=True)
        db_ref[:, C:] += jnp.sum(dgt, axis=0, keepdims=True)

    return _pcall(
        body, name=name, grid=(nt,),
        in_specs=[pl.BlockSpec((tm, C), lambda i: (i, 0)),
                  pl.BlockSpec((HALO, C), lambda i: (jnp.minimum((i + 1) * (tm // HALO), nhb - 1), 0)),
                  pl.BlockSpec((tm, C2), lambda i: (i, 0)), _acc_spec(C, 32)],
        out_specs=[pl.BlockSpec((tm, C2), lambda i: (i, 0)), _acc_spec(C2)],
        out_shape=[jax.ShapeDtypeStruct((S, C2), bf16), jax.ShapeDtypeStruct((1, C2), f32)],
        args=[dcv, dcv, u, w], scratch_shapes=[pltpu.VMEM((tm + HALO, C), f32)], sem=(ARB,), vmem=VMEM_BIG, comms=comms)


def _up_gate(h, w_up, w, b, name, comms=()):
    S, K = h.shape
    F = w.shape[1]
    tm = _tile(S, FFN_TM, 16)
    tn = _tile(F, 512)
    nf = F // tn
    ch = _tile(tn, 256)

    def body(h_ref, wg_ref, wv_ref, w_ref, b_ref, u_ref, a_ref, tail):
        i, j = pl.program_id(0), pl.program_id(1)

        @pl.when(i == 0)
        def _():
            tail[j] = jnp.zeros((8, tn), f32)

        hv = h_ref[...]
        for c in range(tn // ch):
            cs = slice(c * ch, (c + 1) * ch)
            g16 = jnp.dot(hv, wg_ref[:, cs], preferred_element_type=f32).astype(bf16)
            v16 = jnp.dot(hv, wv_ref[:, cs], preferred_element_type=f32).astype(bf16)
            u_ref[0, :, cs] = g16
            u_ref[1, :, cs] = v16
            g = g16.astype(f32)
            ext = jnp.concatenate([tail[j, :, cs], g], axis=0)
            gc = b_ref[:, cs] + w_ref[0:1, cs] * ext[6:6 + tm] + w_ref[1:2, cs] * ext[7:7 + tm] + w_ref[2:3, cs] * g
            a_ref[:, cs] = (gc * _sigmoid(gc) * v16.astype(f32)).astype(a_ref.dtype)
            tail[j, :, cs] = g[tm - 8:tm]

    return _pcall(
        body, name=name, grid=(S // tm, nf),
        in_specs=[pl.BlockSpec((tm, K), lambda i, j: (i, 0)),
                  pl.BlockSpec((None, K, tn), lambda i, j: (0, 0, j)), pl.BlockSpec((None, K, tn), lambda i, j: (0, 0, nf + j)),
                  pl.BlockSpec((8, tn), lambda i, j: (0, j)), pl.BlockSpec((1, tn), lambda i, j: (0, j))],
        out_specs=[pl.BlockSpec((2, tm, tn), lambda i, j: (0, i, j)), pl.BlockSpec((tm, tn), lambda i, j: (i, j))],
        out_shape=[jax.ShapeDtypeStruct((2, S, F), bf16), jax.ShapeDtypeStruct((S, F), bf16)],
        args=[h, w_up, w_up, w, b], scratch_shapes=[pltpu.VMEM((nf, 8, tn), f32)], sem=(ARB, ARB), vmem=VMEM_BIG, comms=comms)


def _dact_gate_bwd(df, w_down, u2, w, b, name, comms=()):
    S, D_ = df.shape
    F = w.shape[1]
    tm = _tile(S, FFN_TM, 16)
    tn = _tile(F, 512)
    nf, nt = F // tn, S // tm
    ch = _tile(tn, 256)
    hb = tm // FHALO

    def body(df_ref, wd_ref, g_ref, gp_ref, v_ref, w_ref, b_ref, du_ref, dw_ref, db_ref, head):
        ii = pl.program_id(1)
        first_tile = ii == nt - 1

        @pl.when(ii == 0)
        def _():
            dw_ref[...] = jnp.zeros_like(dw_ref)
            db_ref[...] = jnp.zeros_like(db_ref)
            head[...] = jnp.zeros_like(head)

        dfv = df_ref[...]
        for c in range(tn // ch):
            cs = slice(c * ch, (c + 1) * ch)
            dact = _dot_nt(dfv, wd_ref[cs, :])
            hist = jnp.where(first_tile, 0.0, gp_ref[:, cs].astype(f32))
            g = jnp.concatenate([hist, g_ref[:, cs].astype(f32)], axis=0)
            taps = [g[FHALO - 2 + k:FHALO - 2 + k + tm] for k in range(FFN_K)]
            gc = b_ref[:, cs] + w_ref[0:1, cs] * taps[0] + w_ref[1:2, cs] * taps[1] + w_ref[2:3, cs] * taps[2]
            sg = _sigmoid(gc)
            dgc = dact * v_ref[:, cs].astype(f32) * (sg * (1.0 + gc * (1.0 - sg)))
            du_ref[1, :, cs] = (dact * (gc * sg)).astype(du_ref.dtype)
            ext = jnp.concatenate([dgc, head[:, cs]], axis=0)
            dgt = w_ref[0:1, cs] * ext[2:2 + tm] + w_ref[1:2, cs] * ext[1:1 + tm] + w_ref[2:3, cs] * dgc
            du_ref[0, :, cs] = dgt.astype(du_ref.dtype)
            db_ref[:, cs] += jnp.sum(dgc, axis=0, keepdims=True)
            for k in range(FFN_K):
                dw_ref[pl.ds(k, 1), cs] += jnp.sum(dgc * taps[k], axis=0, keepdims=True)
            head[:, cs] = dgc[0:8]

    rev = lambda ii: nt - 1 - ii
    return _pcall(
        body, name=name, grid=(nf, nt), comms=comms, sem=(ARB, ARB), vmem=VMEM_BIG,
        args=[df, w_down, u2, u2, u2, w, b],
        in_specs=[pl.BlockSpec((tm, D_), lambda j, ii: (rev(ii), 0)),
                  pl.BlockSpec((None, tn, D_), lambda j, ii: (0, j, 0)),
                  pl.BlockSpec((None, tm, tn), lambda j, ii: (0, rev(ii), j)),
                  pl.BlockSpec((None, FHALO, tn), lambda j, ii: (0, jnp.maximum(rev(ii) * hb - 1, 0), j)),
                  pl.BlockSpec((None, tm, tn), lambda j, ii: (1, rev(ii), j)),
                  pl.BlockSpec((8, tn), lambda j, ii: (0, j)), pl.BlockSpec((1, tn), lambda j, ii: (0, j))],
        out_specs=[pl.BlockSpec((2, tm, tn), lambda j, ii: (0, rev(ii), j)), pl.BlockSpec((8, tn), lambda j, ii: (0, j)),
                   pl.BlockSpec((1, tn), lambda j, ii: (0, j))],
        out_shape=[jax.ShapeDtypeStruct((2, S, F), bf16), jax.ShapeDtypeStruct((8, F), f32), jax.ShapeDtypeStruct((1, F), f32)],
        scratch_shapes=[pltpu.VMEM((8, tn), f32)])


def _ffn_gate_bwd(u2, dact, w, b, name, comms=()):
    _, S, F = u2.shape
    cw = _tile(F, 1408)
    ncw = F // cw
    tm = _tile(S, 256, FHALO)
    nt = S // tm
    nhb = S // FHALO
    R = tm + 2 * FHALO

    def body(g_ref, gp_ref, gn_ref, v_ref, vn_ref, d_ref, dn_ref, w_ref, b_ref, du_ref, dw_ref, db_ref, gbuf, dbuf):
        i = pl.program_id(1)
        first, last = i == 0, i == nt - 1

        @pl.when(i == 0)
        def _():
            dw_ref[...] = jnp.zeros_like(dw_ref)
            db_ref[...] = jnp.zeros_like(db_ref)

        gbuf[pl.ds(0, FHALO), :] = jnp.where(first, 0.0, gp_ref[...].astype(f32))
        gbuf[pl.ds(FHALO, tm), :] = g_ref[...].astype(f32)
        gbuf[pl.ds(FHALO + tm, FHALO), :] = gn_ref[...].astype(f32)
        w = w_ref[...]
        n_ext = tm + FHALO
        gc = jnp.zeros((n_ext, cw), f32) + b_ref[...]
        for k in range(FFN_K):
            gc = gc + w[k:k + 1, :] * gbuf[pl.ds(FHALO - (FFN_K - 1) + k, n_ext), :]
        sg = _sigmoid(gc)
        val = jnp.concatenate([v_ref[...].astype(f32), vn_ref[...].astype(f32)], axis=0)
        dact_ext = jnp.concatenate([d_ref[...].astype(f32), jnp.where(last, 0.0, dn_ref[...].astype(f32))], axis=0)
        dgc = dact_ext * val * (sg * (1.0 + gc * (1.0 - sg)))
        dbuf[...] = dgc
        dval = dact_ext[:tm] * (gc[:tm] * sg[:tm])
        dgt = jnp.zeros((tm, cw), f32)
        for k in range(FFN_K):
            dgt = dgt + w[k:k + 1, :] * dbuf[pl.ds(FFN_K - 1 - k, tm), :]
        du_ref[0] = dgt.astype(du_ref.dtype)
        du_ref[1] = dval.astype(du_ref.dtype)
        dgc_t = dgc[:tm]
        db_ref[...] += jnp.sum(dgc_t, axis=0, keepdims=True)
        for k in range(FFN_K):
            dw_ref[pl.ds(k, 1), :] += jnp.sum(dgc_t * gbuf[pl.ds(FHALO - (FFN_K - 1) + k, tm), :], axis=0, keepdims=True)

    hb = tm // FHALO
    prev = lambda j, i: (jnp.maximum(i * hb - 1, 0), j)
    nxt = lambda j, i: (jnp.minimum((i + 1) * hb, nhb - 1), j)
    plane = lambda p, f: (lambda j, i: (p,) + f(j, i))
    return _pcall(
        body, name=name, grid=(ncw, nt), comms=comms, sem=(PAR, ARB), vmem=VMEM_BIG,
        args=[u2, u2, u2, u2, u2, dact, dact, w, b],
        in_specs=[pl.BlockSpec((None, tm, cw), lambda j, i: (0, i, j)), pl.BlockSpec((None, FHALO, cw), plane(0, prev)),
                  pl.BlockSpec((None, FHALO, cw), plane(0, nxt)),
                  pl.BlockSpec((None, tm, cw), lambda j, i: (1, i, j)), pl.BlockSpec((None, FHALO, cw), plane(1, nxt)),
                  pl.BlockSpec((tm, cw), lambda j, i: (i, j)), pl.BlockSpec((FHALO, cw), nxt),
                  pl.BlockSpec((8, cw), lambda j, i: (0, j)), pl.BlockSpec((1, cw), lambda j, i: (0, j))],
        out_specs=[pl.BlockSpec((2, tm, cw), lambda j, i: (0, i, j)), pl.BlockSpec((8, cw), lambda j, i: (0, j)),
                   pl.BlockSpec((1, cw), lambda j, i: (0, j))],
        out_shape=[jax.ShapeDtypeStruct((2, S, F), bf16), jax.ShapeDtypeStruct((8, F), f32), jax.ShapeDtypeStruct((1, F), f32)],
        scratch_shapes=[pltpu.VMEM((R, cw), f32), pltpu.VMEM((tm + FHALO, cw), f32)])


def _rope_tables(pos_col, name):
    S = pos_col.shape[0]
    tm = _tile(S, 512, 8)
    half = ROT // 2
    inv = THETA ** (-np.arange(0, ROT, 2, dtype=np.float32) / ROT)
    lane_freq = np.zeros((1, DH), np.float32)
    lane_freq[0, :half] = inv
    lane_freq[0, half:ROT] = inv
    lane_freq = jnp.asarray(lane_freq)

    def body(p_ref, fr_ref, c_ref, sa_ref, sb_ref):
        ang = p_ref[...].astype(f32) * fr_ref[...]
        lane = lax.broadcasted_iota(jnp.int32, (tm, DH), 1)
        cs, sn = jnp.cos(ang), jnp.sin(ang)
        c_ref[...] = jnp.where(lane < ROT, cs, 1.0)
        sa_ref[...] = jnp.where(lane < half, -sn, 0.0)
        sb_ref[...] = jnp.where((lane >= half) & (lane < ROT), sn, 0.0)

    row = pl.BlockSpec((tm, DH), lambda i: (i, 0))
    shp = jax.ShapeDtypeStruct((S, DH), f32)
    return pl.pallas_call(body, name=name, grid=(S // tm,),
                          in_specs=[pl.BlockSpec((tm, 1), lambda i: (i, 0)), pl.BlockSpec((1, DH), lambda i: (0, 0))],
                          out_specs=[row, row, row], out_shape=[shp, shp, shp], compiler_params=_cp((PAR,)))(pos_col, lane_freq)


def _swap_matrix():
    k = lax.broadcasted_iota(jnp.int32, (DH, DH), 0)
    i = lax.broadcasted_iota(jnp.int32, (DH, DH), 1)
    half = ROT // 2
    hit = ((i < half) & (k == i + half)) | ((i >= half) & (i < ROT) & (k == i - half))
    return jnp.where(hit, 1.0, 0.0).astype(bf16)


def _head_mean(x):
    return jnp.dot(x.astype(bf16), jnp.ones((DH, DH), bf16), preferred_element_type=f32) * (1.0 / DH)


def _rope(n, c, t, swap):
    return n * c + jnp.dot(n.astype(bf16), swap, preferred_element_type=f32) * t


def _rope_t(d, c, t, swap):
    return d * c + jnp.dot((d * t).astype(bf16), swap, preferred_element_type=f32)


def _qk_fwd(raw, g, tabs, width, with_values, name):
    S = raw.shape[0]
    nh = width // DH
    ow = width // NG
    hpg = ow // DH
    tm = _tile(S, 256, 16 * max(DILS))
    vgroups = [gi for gi, r in enumerate(DILS) if r > 1] if with_values else []

    def body(x_ref, g_ref, c_ref, sa_ref, sb_ref, *rest):
        o_refs = rest[:NG]
        v_refs = rest[NG:NG + len(vgroups)]
        scr, vscr = rest[NG + len(vgroups):]
        c, t, swap = c_ref[...], sa_ref[...] + sb_ref[...], _swap_matrix()
        for gi, r in enumerate(DILS):
            heads = range(gi * hpg, (gi + 1) * hpg)
            xs = [x_ref[:, h * DH:(h + 1) * DH].astype(f32) for h in heads]
            rs = [lax.rsqrt(_head_mean(xv * xv) + EPS) for xv in xs]
            ys = [_rope(xv * rv * g_ref[...], c, t, swap) for xv, rv in zip(xs, rs)]
            for hh, y in enumerate(ys):
                if r == 1:
                    o_refs[gi][:, hh * DH:(hh + 1) * DH] = y.astype(bf16)
                else:
                    scr[hh] = y
            if r > 1:
                for hh in range(hpg):
                    for j in range(r):
                        o_refs[gi][:, j * ow + hh * DH:j * ow + (hh + 1) * DH] = scr[hh, pl.ds(j, tm // r, stride=r), :].astype(bf16)
        for vi, gi in enumerate(vgroups):
            _to_view(x_ref[:, width + gi * ow:width + (gi + 1) * ow].astype(f32), v_refs[vi], vscr, DILS[gi], ow, tm)

    win = raw.shape[1] if with_values else width
    row = pl.BlockSpec((tm, win), lambda i: (i, 0))
    tab = pl.BlockSpec((tm, DH), lambda i: (i, 0))
    view = lambda r: pl.BlockSpec((tm // r, r * ow), lambda i: (i, 0))
    vshape = lambda r: jax.ShapeDtypeStruct((S // r, r * ow), bf16)
    outs = pl.pallas_call(
        body, name=name, grid=(S // tm,), in_specs=[row, _acc_spec(DH), tab, tab, tab],
        out_specs=[view(r) for r in DILS] + [view(DILS[gi]) for gi in vgroups],
        out_shape=[vshape(r) for r in DILS] + [vshape(DILS[gi]) for gi in vgroups],
        scratch_shapes=[pltpu.VMEM((hpg, tm, DH), f32), pltpu.VMEM((ow // DH, tm, DH), f32)],
        compiler_params=_cp((PAR,)))(raw, g, *tabs)
    return outs[:NG], outs[NG:]


def _qk_bwd(dparts, raw, g, tabs, width, extra, name):
    S = raw.shape[0]
    nh = width // DH
    ow = width // NG
    hpg = ow // DH
    tm = _tile(S, 256, 16 * max(DILS))
    wout = width + len(extra) * ow

    def body(*refs):
        d_refs = refs[:NG]
        x_ref, g_ref, c_ref, sa_ref, sb_ref = refs[NG:NG + 5]
        e_refs = refs[NG + 5:NG + 5 + len(extra)]
        o_ref, dg_ref, scr, vscr = refs[NG + 5 + len(extra):]
        i = pl.program_id(0)

        @pl.when(i == 0)
        def _():
            dg_ref[...] = jnp.zeros_like(dg_ref)

        c, t, swap = c_ref[...], sa_ref[...] + sb_ref[...], _swap_matrix()
        gv = g_ref[...]
        dg = jnp.zeros((1, DH), f32)
        for gi, r in enumerate(DILS):
            heads = list(range(gi * hpg, (gi + 1) * hpg))
            if r == 1:
                douts = [d_refs[gi][:, hh * DH:(hh + 1) * DH].astype(f32) for hh in range(hpg)]
            else:
                for hh in range(hpg):
                    for j in range(r):
                        scr[hh, pl.ds(j, tm // r, stride=r), :] = d_refs[gi][:, j * ow + hh * DH:j * ow + (hh + 1) * DH].astype(f32)
                douts = [scr[hh] for hh in range(hpg)]
            xs = [x_ref[:, h * DH:(h + 1) * DH].astype(f32) for h in heads]
            rs = [lax.rsqrt(_head_mean(xv * xv) + EPS) for xv in xs]
            xhs = [xv * rv for xv, rv in zip(xs, rs)]
            dns = [_rope_t(d, c, t, swap) for d in douts]
            for dn, xh in zip(dns, xhs):
                dg = dg + jnp.sum(dn * xh, axis=0, keepdims=True)
            dxns = [dn * gv for dn in dns]
            dxs = [rv * (dxn - xh * _head_mean(dxn * xh)) for rv, dxn, xh in zip(rs, dxns, xhs)]
            for h, dx in zip(heads, dxs):
                o_ref[:, h * DH:(h + 1) * DH] = dx.astype(o_ref.dtype)
        for gi, e_ref in enumerate(e_refs):
            o_ref[:, width + gi * ow:width + (gi + 1) * ow] = _from_view(e_ref, vscr, DILS[gi], ow, tm).astype(o_ref.dtype)
        dg_ref[...] += dg

    views = [pl.BlockSpec((tm // r, r * ow), lambda i: (i, 0)) for r in DILS]
    tab = pl.BlockSpec((tm, DH), lambda i: (i, 0))
    return pl.pallas_call(
        body, name=name, grid=(S // tm,),
        in_specs=views + [pl.BlockSpec((tm, width), lambda i: (i, 0)), _acc_spec(DH), tab, tab, tab] + (views if extra else []),
        out_specs=[pl.BlockSpec((tm, wout), lambda i: (i, 0)), _acc_spec(DH)],
        out_shape=[jax.ShapeDtypeStruct((S, wout), bf16), jax.ShapeDtypeStruct((1, DH), f32)],
        scratch_shapes=[pltpu.VMEM((hpg, tm, DH), f32), pltpu.VMEM((ow // DH, tm, DH), f32)],
        compiler_params=_cp((ARB,)))(*dparts, raw, g, *tabs, *extra)


def _dot_nt(a, b):
    return lax.dot_general(a, b, (((1,), (1,)), ((), ())), preferred_element_type=f32)


def _dot_tn(a, b):
    return lax.dot_general(a, b, (((0,), (0,)), ((), ())), preferred_element_type=f32)


def _band_masks():
    qi = lax.broadcasted_iota(jnp.int32, (BLK, BLK), 0)
    ki = lax.broadcasted_iota(jnp.int32, (BLK, BLK), 1)
    return ki <= qi, ki >= qi


def _attn_fwd(qv, kview, vview, vbase, r, name):
    sr = qv.shape[0]
    ow = qv.shape[1] // r
    hpg = ow // DH
    nb = sr // BLK
    scale = 1.0 / math.sqrt(DH)

    def body(q_ref, kc_ref, kp_ref, vc_ref, vp_ref, o_ref, l_ref):
        n = pl.program_id(1)
        m_cur, m_prev = _band_masks()
        m_prev = m_prev & (n > 0)
        hs = [slice(h * DH, (h + 1) * DH) for h in range(hpg)]
        s_c = [jnp.where(m_cur, _dot_nt(q_ref[:, s], kc_ref[:, s]) * scale, NEG) for s in hs]
        s_p = [jnp.where(m_prev, _dot_nt(q_ref[:, s], kp_ref[:, s]) * scale, NEG) for s in hs]
        mx = [jnp.maximum(jnp.max(a, axis=-1, keepdims=True), jnp.max(b, axis=-1, keepdims=True)) for a, b in zip(s_c, s_p)]
        p_c = [jnp.exp(a - m) for a, m in zip(s_c, mx)]
        p_p = [jnp.exp(a - m) for a, m in zip(s_p, mx)]
        den = [jnp.sum(a, axis=-1, keepdims=True) + jnp.sum(b, axis=-1, keepdims=True) for a, b in zip(p_c, p_p)]
        for h, s in enumerate(hs):
            o = jnp.dot(p_c[h].astype(bf16), vc_ref[:, s], preferred_element_type=f32)
            o = o + jnp.dot(p_p[h].astype(bf16), vp_ref[:, s], preferred_element_type=f32)
            o_ref[:, s] = (o / den[h]).astype(o_ref.dtype)
            l_ref[:, s] = jnp.broadcast_to(mx[h] + jnp.log(den[h]), (BLK, DH))

    cur = lambda j, n: (n, j)
    prev = lambda j, n: (jnp.maximum(n - 1, 0), j)
    vcur = lambda j, n: (n, vbase + j)
    vprev = lambda j, n: (jnp.maximum(n - 1, 0), vbase + j)
    blk = lambda f: pl.BlockSpec((BLK, ow), f)
    return pl.pallas_call(
        body, name=name, grid=(r, nb), in_specs=[blk(cur), blk(cur), blk(prev), blk(vcur), blk(vprev)],
        out_specs=[blk(cur), blk(cur)],
        out_shape=[jax.ShapeDtypeStruct((sr, r * ow), bf16), jax.ShapeDtypeStruct((sr, r * ow), f32)],
        compiler_params=_cp((PAR, PAR)))(qv, kview, kview, vview, vview)


def _attn_bwd_q(qv, kview, vview, vbase, do_g, lse, corr, r, name, comms=()):
    sr = qv.shape[0]
    ow = qv.shape[1] // r
    hpg = ow // DH
    nb = sr // BLK
    scale = 1.0 / math.sqrt(DH)

    def body(q_ref, kc_ref, kp_ref, vc_ref, vp_ref, do_ref, l_ref, c_ref, dq_ref):
        n = pl.program_id(1)
        m_cur, m_prev = _band_masks()
        m_prev = m_prev & (n > 0)
        hs = [slice(h * DH, (h + 1) * DH) for h in range(hpg)]
        ls = [slice(h * DH, h * DH + BLK) for h in range(hpg)]
        sides = ((kc_ref, vc_ref, m_cur), (kp_ref, vp_ref, m_prev))
        sc = [[jnp.where(msk, _dot_nt(q_ref[:, s], k_ref[:, s]) * scale, NEG) for s in hs] for k_ref, _, msk in sides]
        dp = [[_dot_nt(do_ref[:, s], v_ref[:, s]) for s in hs] for _, v_ref, _ in sides]
        ds = [[(jnp.exp(sc[i][h] - l_ref[:, ls[h]]) * (dp[i][h] + c_ref[:, ls[h]])).astype(bf16) for h in range(hpg)]
              for i in range(2)]
        for h, s in enumerate(hs):
            dq = jnp.dot(ds[0][h], kc_ref[:, s], preferred_element_type=f32)
            dq = dq + jnp.dot(ds[1][h], kp_ref[:, s], preferred_element_type=f32)
            dq_ref[:, s] = (dq * scale).astype(dq_ref.dtype)

    cur = lambda j, n: (n, j)
    prev = lambda j, n: (jnp.maximum(n - 1, 0), j)
    vcur = lambda j, n: (n, vbase + j)
    vprev = lambda j, n: (jnp.maximum(n - 1, 0), vbase + j)
    blk = lambda f: pl.BlockSpec((BLK, ow), f)
    return _pcall(
        body, name=name, grid=(r, nb),
        in_specs=[blk(cur), blk(cur), blk(prev), blk(vcur), blk(vprev), blk(cur), blk(cur), blk(cur)],
        out_specs=blk(cur), out_shape=jax.ShapeDtypeStruct((sr, r * ow), bf16), sem=(PAR, PAR), comms=comms,
        args=[qv, kview, kview, vview, vview, do_g, lse, corr])


def _attn_bwd_kv(qv, kview, vview, vbase, do_g, lse, corr, r, name):
    sr = qv.shape[0]
    ow = qv.shape[1] // r
    hpg = ow // DH
    nb = sr // BLK
    scale = 1.0 / math.sqrt(DH)

    def body(k_ref, v_ref, qc_ref, qn_ref, doc_ref, don_ref, lc_ref, ln_ref, cc_ref, cn_ref, dk_ref, dv_ref):
        n = pl.program_id(1)
        m_cur, m_prev = _band_masks()
        m_next = m_prev & (n < nb - 1)
        hs = [slice(h * DH, (h + 1) * DH) for h in range(hpg)]
        ls = [slice(h * DH, h * DH + BLK) for h in range(hpg)]
        sides = ((qc_ref, doc_ref, lc_ref, cc_ref, m_cur), (qn_ref, don_ref, ln_ref, cn_ref, m_next))
        sc = [[jnp.where(msk, _dot_nt(q_ref[:, s], k_ref[:, s]) * scale, NEG) for s in hs] for q_ref, _, _, _, msk in sides]
        dp = [[_dot_nt(do_ref[:, s], v_ref[:, s]) for s in hs] for _, do_ref, _, _, _ in sides]
        p = [[jnp.exp(sc[i][h] - sides[i][2][:, ls[h]]) for h in range(hpg)] for i in range(2)]
        ds = [[(p[i][h] * (dp[i][h] + sides[i][3][:, ls[h]])).astype(bf16) for h in range(hpg)] for i in range(2)]
        for h, s in enumerate(hs):
            dv = _dot_tn(p[0][h].astype(bf16), doc_ref[:, s]) + _dot_tn(p[1][h].astype(bf16), don_ref[:, s])
            dk = _dot_tn(ds[0][h], qc_ref[:, s]) + _dot_tn(ds[1][h], qn_ref[:, s])
            dk_ref[:, s] = (dk * scale).astype(dk_ref.dtype)
            dv_ref[:, s] = dv.astype(dv_ref.dtype)

    cur = lambda j, n: (n, j)
    nxt = lambda j, n: (jnp.minimum(n + 1, nb - 1), j)
    vcur = lambda j, n: (n, vbase + j)
    blk = lambda f: pl.BlockSpec((BLK, ow), f)
    shp = jax.ShapeDtypeStruct((sr, r * ow), bf16)
    return pl.pallas_call(
        body, name=name, grid=(r, nb),
        in_specs=[blk(cur), blk(vcur), blk(cur), blk(nxt), blk(cur), blk(nxt), blk(cur), blk(nxt), blk(cur), blk(nxt)],
        out_specs=[blk(cur), blk(cur)], out_shape=[shp, shp],
        compiler_params=_cp((PAR, PAR)))(kview, vview, qv, qv, do_g, do_g, lse, lse, corr, corr)


def _mix_weights(l_refs):
    ls = [l[...] for l in l_refs]
    mx = functools.reduce(jnp.maximum, ls)
    es = [jnp.exp(l - mx) for l in ls]
    den = functools.reduce(lambda a, b: a + b, es)
    return [e / den for e in es]


def _from_view(ref, scr, r, ow, tm):
    if r == 1:
        return ref[...].astype(f32)
    for c in range(ow // DH):
        for j in range(r):
            scr[c, pl.ds(j, tm // r, stride=r), :] = ref[:, j * ow + c * DH:j * ow + (c + 1) * DH].astype(f32)
    return jnp.concatenate([scr[c] for c in range(ow // DH)], axis=1)


def _to_view(val, ref, scr, r, ow, tm):
    if r == 1:
        ref[...] = val.astype(ref.dtype)
        return
    for c in range(ow // DH):
        scr[c] = val[:, c * DH:(c + 1) * DH]
        for j in range(r):
            ref[:, j * ow + c * DH:j * ow + (c + 1) * DH] = scr[c, pl.ds(j, tm // r, stride=r), :].astype(ref.dtype)


def _view_specs(tm, ow):
    return [pl.BlockSpec((tm // r, r * ow), lambda i: (i, 0)) for r in DILS]


def _combine_fwd(os_, lses, name):
    ow = os_[0].shape[1] // DILS[0]
    S = os_[0].shape[0] * DILS[0]
    tm = _tile(S, 256, 16 * max(DILS))

    def body(*refs):
        o_refs, l_refs, out_ref = refs[:NG], refs[NG:2 * NG], refs[2 * NG]
        scr = refs[2 * NG + 1:]
        ov = [_from_view(o_refs[gi], scr[2 * gi], DILS[gi], ow, tm) for gi in range(NG)]
        lv = [_from_view(l_refs[gi], scr[2 * gi + 1], DILS[gi], ow, tm) for gi in range(NG)]
        al = _mix_weights(lv)
        acc = al[0] * ov[0]
        for gi in range(1, NG):
            acc = acc + al[gi] * ov[gi]
        out_ref[...] = acc.astype(out_ref.dtype)

    views = _view_specs(tm, ow)
    return pl.pallas_call(body, name=name, grid=(S // tm,), in_specs=views + views,
                          out_specs=pl.BlockSpec((tm, ow), lambda i: (i, 0)), out_shape=jax.ShapeDtypeStruct((S, ow), bf16),
                          scratch_shapes=[pltpu.VMEM((ow // DH, tm, DH), f32)] * (2 * NG),
                          compiler_params=_cp((PAR,), VMEM_BIG))(*os_, *lses)


def _combine_bwd(do, os_, lses, name, comms=()):
    S, ow = do.shape
    hpg = ow // DH
    tm = _tile(S, 256, 16 * max(DILS))

    def body(*refs):
        do_ref = refs[0]
        o_refs, l_refs = refs[1:1 + NG], refs[1 + NG:1 + 2 * NG]
        dog_refs, c_refs = refs[1 + 2 * NG:1 + 3 * NG], refs[1 + 3 * NG:1 + 4 * NG]
        scr = refs[1 + 4 * NG:]
        ov = [_from_view(o_refs[gi], scr[2 * gi], DILS[gi], ow, tm) for gi in range(NG)]
        lv = [_from_view(l_refs[gi], scr[2 * gi + 1], DILS[gi], ow, tm) for gi in range(NG)]
        al = _mix_weights(lv)
        dov = do_ref[...]
        o = al[0] * ov[0]
        for gi in range(1, NG):
            o = o + al[gi] * ov[gi]
        prod = dov * o
        t = jnp.concatenate(
            [jnp.broadcast_to(jnp.sum(prod[:, h * DH:(h + 1) * DH], axis=-1, keepdims=True), (tm, DH)) for h in range(hpg)],
            axis=1)
        for gi in range(NG):
            _to_view(al[gi] * dov, dog_refs[gi], scr[2 * NG], DILS[gi], ow, tm)
            _to_view(-(al[gi] * t), c_refs[gi], scr[2 * NG], DILS[gi], ow, tm)

    views = _view_specs(tm, ow)
    vshape = lambda dt: [jax.ShapeDtypeStruct((S // r, r * ow), dt) for r in DILS]
    return _pcall(
        body, name=name, grid=(S // tm,), in_specs=[pl.BlockSpec((tm, ow), lambda i: (i, 0))] + views + views,
        out_specs=views + views, out_shape=vshape(bf16) + vshape(f32),
        args=[do, *os_, *lses], scratch_shapes=[pltpu.VMEM((ow // DH, tm, DH), f32)] * (2 * NG + 1), sem=(PAR,), vmem=VMEM_BIG,
        comms=comms)


def _pad_rows(w, rows):
    return jnp.concatenate([w, jnp.zeros((rows - w.shape[0], w.shape[1]), w.dtype)], axis=0)


def kernel(x, c, positions, mod_w, mod_b, norm_mix_g, norm_ffn_g, conv_pw1_w, conv_pw1_b, conv_dw_w, conv_dw_b, conv_ln_g, conv_ln_b, conv_pw2_w, conv_pw2_b, kv_mod_w, kv_mod_b, kv_norm_g, w_kv, k_norm_g, w_q, q_norm_g, w_o, ffn_up_w, ffn_dw_w, ffn_dw_b, ffn_down_w, loss_target, m_mod_w, m_mod_b, m_norm_mix_g, m_norm_ffn_g, m_conv_pw1_w, m_conv_pw1_b, m_conv_dw_w, m_conv_dw_b, m_conv_ln_g, m_conv_ln_b, m_conv_pw2_w, m_conv_pw2_b, m_kv_mod_w, m_kv_mod_b, m_kv_norm_g, m_w_kv, m_k_norm_g, m_w_q, m_q_norm_g, m_w_o, m_ffn_up_w, m_ffn_dw_w, m_ffn_dw_b, m_ffn_down_w, v_mod_w, v_mod_b, v_norm_mix_g, v_norm_ffn_g, v_conv_pw1_w, v_conv_pw1_b, v_conv_dw_w, v_conv_dw_b, v_conv_ln_g, v_conv_ln_b, v_conv_pw2_w, v_conv_pw2_b, v_kv_mod_w, v_kv_mod_b, v_kv_norm_g, v_w_kv, v_k_norm_g, v_w_q, v_q_norm_g, v_w_o, v_ffn_up_w, v_ffn_dw_w, v_ffn_dw_b, v_ffn_down_w):
    S, Dm = x.shape[1], x.shape[2]
    F = ffn_dw_b.shape[1]
    QW = NG * HPG * DH
    OW = HPG * DH
    mx, my, mc = _me()
    me = 4 * mx + 2 * my + mc
    core = jnp.reshape(mc, (1,)).astype(jnp.int32)
    chip = jnp.reshape(2 * mx + my, (1,)).astype(jnp.int32)
    x0 = x.reshape(S, Dm)
    target = loss_target.reshape(S, Dm)

    c_all = _ag_small(c, "ag_c").reshape(NDEV, Dm)
    n_mod = mod_w.shape[2]
    n_kvm = kv_mod_w.shape[1]
    b0 = lax.dynamic_slice(mod_b, (0, me * n_mod), (1, n_mod))
    b1 = lax.dynamic_slice(mod_b, (1, me * n_mod), (1, n_mod))
    bk = lax.dynamic_slice(kv_mod_b.reshape(1, -1), (0, me * n_kvm), (1, n_kvm))
    m_part = jnp.concatenate([_modproj(c_all, mod_w[0], b0, "modproj0"), _modproj(c_all, mod_w[1], b1, "modproj1"),
                              _modproj(c_all, kv_mod_w, bk, "modproj_kv")], axis=1)
    m_all = _ag_small(m_part, "ag_mod")
    m_mine = lax.dynamic_index_in_dim(m_all, me, axis=1, keepdims=False)
    mod0 = m_mine[:, :n_mod].reshape(6, Dm)
    mod1 = m_mine[:, n_mod:2 * n_mod].reshape(6, Dm)
    modkv = m_mine[:, 2 * n_mod:].reshape(2, Dm)
    row = lambda a, i: a[i:i + 1]

    as3 = lambda w: w if w.ndim == 3 else w[None]
    sh16 = lambda w: as3(w).astype(bf16)
    ag_pw1 = _comm_allgather(sh16(conv_pw1_w), 2)
    ag_pw2 = _comm_allgather(sh16(conv_pw2_w), 1)
    ag_up = [_comm_allgather(sh16(ffn_up_w[l]), 2) for l in range(2)]
    ag_down = [_comm_allgather(sh16(ffn_down_w[l]), 1) for l in range(2)]
    ag_kv = _comm_allgather(sh16(w_kv), 2)
    ag_q = _comm_allgather(sh16(w_q), 2)
    ag_o = _comm_allgather(sh16(w_o), 2)

    sp_flat = jnp.concatenate([conv_pw1_b.reshape(-1), conv_dw_b.reshape(-1), conv_ln_g.reshape(-1), conv_ln_b.reshape(-1),
                               conv_pw2_b.reshape(-1), conv_dw_w.reshape(-1), ffn_dw_w.reshape(-1)])
    sp_rows = -(-sp_flat.shape[0] // 1024) * 8
    sp_flat = jnp.concatenate([sp_flat, jnp.zeros((sp_rows * 128 - sp_flat.shape[0],), f32)]).reshape(sp_rows, 128)
    n1, nd = conv_pw1_b.shape[1], conv_dw_b.shape[1]
    nfw = ffn_dw_w.shape[2]
    sp = _ag_small(sp_flat, "ag_small_params").reshape(NDEV, -1)
    off = 0
    pw1_b = sp[:, off:off + n1].reshape(1, -1); off += n1
    dw_b = sp[:, off:off + nd].reshape(1, -1); off += nd
    ln_g = sp[:, off:off + nd].reshape(1, -1); off += nd
    ln_b = sp[:, off:off + nd].reshape(1, -1); off += nd
    pw2_b = sp[:, off:off + nd].reshape(1, -1); off += nd
    dw_w = jnp.transpose(sp[:, off:off + CONV_K * nd].reshape(NDEV, CONV_K, nd), (1, 0, 2)).reshape(CONV_K, -1); off += CONV_K * nd
    fdw_w = jnp.transpose(sp[:, off:off + 2 * FFN_K * nfw].reshape(NDEV, 2, FFN_K, nfw), (1, 2, 0, 3)).reshape(2, FFN_K, -1)
    dw_w32 = _pad_rows(dw_w, 32)

    tabs = _rope_tables(positions.reshape(S, 1), "rope_tables")

    def with_comms(res, comms):
        return res if comms else (res, [])

    def rs_d2d(dwb):
        return [_comm_rs_sibling(dwb)]

    def rs_add(dwb, couts, tag):
        return _chip_partial(dwb, couts[0][0], core, f"rs_add_{tag}")

    def rs_ici(part):
        return [_comm_rs_chips(part)]

    def ffn_forward(xin, l, modv, w_up, w_down, up_comms, down_comms):
        h2 = _mod_fwd(xin, row(norm_ffn_g, l), row(modv, 3), row(modv, 4), f"ffn{l}_mod")
        fw8 = _pad_rows(fdw_w[l], 8)
        (u2, act), c_up = with_comms(_up_gate(h2, w_up, fw8, row(ffn_dw_b, l), f"ffn{l}_up", comms=up_comms), up_comms)
        if w_down is None:
            w_down, c_up = c_up[0][0], c_up[1:]
        (xout, f), c_down = with_comms(
            _mm_nn(act, w_down, 0, name=f"ffn{l}_down", res=xin, gate=row(modv, 5), tk=F, tn=512, comms=down_comms), down_comms)
        return xout, (h2, u2, act, f, fw8, w_up, w_down), c_up, c_down

    def ffn_backward(dx, xin, l, modv, saved, dact_comms):
        h2, u2, act, f, fw8, w_up, w_down = saved
        df, dgate, _ = _gate_bwd(dx, f, row(modv, 5), f"ffn{l}_gate_bwd")
        d_down = _mm_tn(act, df, name=f"ffn{l}_ddown", col_sharded=False)
        (du2, d_fw, d_fb), c1 = _dact_gate_bwd(df, w_down, u2, fw8, row(ffn_dw_b, l), f"ffn{l}_gatebwd",
                                               comms=rs_d2d(d_down) + list(dact_comms))
        part_down, c_dact = rs_add(d_down, c1, f"down{l}"), c1[1:]
        dh2, c2 = _mm_nt(du2, w_up, 0, name=f"ffn{l}_dh", out_dtype=f32, tko=1024, tn=F // 2, comms=rs_ici(part_down))
        d_up = _mm_tn(h2, du2, name=f"ffn{l}_dup", col_sharded=True)
        (dxin, dsh, dsc, dg), c3 = _mod_bwd(dh2, xin, dx, row(norm_ffn_g, l), row(modv, 4), f"ffn{l}_mod_bwd", comms=rs_d2d(d_up))
        part_up = rs_add(d_up, c3, f"up{l}")
        grads = dict(d_fw=d_fw[:FFN_K], d_fb=d_fb, dsh=dsh, dsc=dsc, dgate=dgate, dg=dg,
                     down=(part_down, c2[0][0]), part_up=part_up)
        return dxin, grads, c_dact

    h0, c = _mod_fwd(x0, row(norm_mix_g, 0), row(mod0, 0), row(mod0, 1), "l0_mod", comms=[ag_pw1])
    W_pw1 = c[0][0]
    u0, c = _mm_nn(h0, W_pw1, 0, name="l0_pw1", bias=pw1_b, comms=[ag_pw2])
    W_pw2 = c[0][0]
    (s0, cv0), c = _conv_fwd(u0, dw_w32, dw_b, ln_g, ln_b, "l0_conv", comms=[ag_up[0]])
    W_up0 = c[0][0]
    (x1, f0), c = _mm_nn(s0, W_pw2, 0, name="l0_pw2", bias=pw2_b, res=x0, gate=row(mod0, 2), comms=[ag_q, ag_o])
    W_q, W_o = c[0][0], c[1][0]
    x2, ffn0_saved, c_up, _ = ffn_forward(x1, 0, mod0, W_up0, None, [ag_down[0], ag_kv, ag_up[1]], ())
    W_kv, W_up1 = c_up[0][0], c_up[1][0]

    hkv = _mod_fwd(x2, kv_norm_g.reshape(1, -1), row(modkv, 0), row(modkv, 1), "kv_mod")
    kvraw, c = _mm_nn(hkv, W_kv, 0, name="kv_proj", comms=[ag_down[1]])
    W_down1 = c[0][0]
    kg = k_norm_g.reshape(1, -1)
    k_gv, v_dil = _qk_fwd(kvraw, kg, tabs, QW, True, "k_norm_rope")
    dilated = [gi for gi, r in enumerate(DILS) if r > 1]
    v_of = {gi: (kvraw, NG + gi) for gi, r in enumerate(DILS) if r == 1}
    v_of.update({gi: (v_dil[i], 0) for i, gi in enumerate(dilated)})
    h1 = _mod_fwd(x2, row(norm_mix_g, 1), row(mod1, 0), row(mod1, 1), "l1_mod")
    qraw = _mm_nn(h1, W_q, 0, name="q_proj")
    qg = q_norm_g.reshape(1, -1)
    q_gv, _ = _qk_fwd(qraw, qg, tabs, QW, False, "q_norm_rope")
    o_gs, lses = [], []
    for gi, r in enumerate(DILS):
        o_g, lse_g = _attn_fwd(q_gv[gi], k_gv[gi], *v_of[gi], r, f"attn_fwd{gi}")
        o_gs.append(o_g)
        lses.append(lse_g)
    o_mix = _combine_fwd(o_gs, lses, "attn_mix")
    x3, f1 = _mm_nn(o_mix, W_o, 0, name="o_proj", res=x2, gate=row(mod1, 2))
    x4, ffn1_saved, _, _ = ffn_forward(x3, 1, mod1, W_up1, W_down1, (), ())

    dx4, loss_blk = _loss_grad(x4, target, "loss")
    loss = lax.psum(loss_blk[0, 0], ("x", "y", "c"))

    red = {}
    dx3, gf1, _ = ffn_backward(dx4, x3, 1, mod1, ffn1_saved, ())
    dy1, dgate_m1, _ = _gate_bwd(dx3, f1, row(mod1, 2), "l1_gate_bwd")
    do = _mm_nt(dy1, W_o, 0, name="o_proj_dx", out_dtype=f32, tko=1024, tn=Dm)
    d_wo = _mm_tn(o_mix, dy1, name="o_proj_dw", col_sharded=True)
    outs, c = _combine_bwd(do, o_gs, lses, "attn_mix_bwd", comms=rs_d2d(d_wo))
    part_wo = rs_add(d_wo, c, "wo")
    do_gs, corrs = outs[:NG], outs[NG:]
    dq_gs, dk_gs, dv_gs = [], [], []
    for gi, r in enumerate(DILS):
        cm = rs_ici(part_wo) if gi == 0 else ()
        dq_g, c = with_comms(_attn_bwd_q(q_gv[gi], k_gv[gi], *v_of[gi], do_gs[gi], lses[gi], corrs[gi], r, f"attn_bwd_q{gi}",
                                         comms=cm), cm)
        if gi == 0:
            red["w_o"] = (part_wo, c[0][0])
        dq_gs.append(dq_g)
        dk_g, dv_g = _attn_bwd_kv(q_gv[gi], k_gv[gi], *v_of[gi], do_gs[gi], lses[gi], corrs[gi], r, f"attn_bwd_kv{gi}")
        dk_gs.append(dk_g)
        dv_gs.append(dv_g)
    dqraw, d_qg = _qk_bwd(dq_gs, qraw, qg, tabs, QW, (), "q_norm_rope_bwd")
    dkvraw, d_kg = _qk_bwd(dk_gs, kvraw, kg, tabs, QW, tuple(dv_gs), "k_norm_rope_bwd")
    dh1 = _mm_nt(dqraw, W_q, 0, name="q_proj_dx", out_dtype=f32, tko=1024, tn=QW)
    d_wq = _mm_tn(h1, dqraw, name="q_proj_dw", col_sharded=True)
    dhkv, c = _mm_nt(dkvraw, W_kv, 0, name="kv_proj_dx", out_dtype=f32, tko=512, tn=2 * QW, comms=rs_d2d(d_wq))
    part_wq = rs_add(d_wq, c, "wq")
    d_wkv, c = _mm_tn(hkv, dkvraw, name="kv_proj_dw", col_sharded=True, comms=rs_ici(gf1["part_up"]))
    red["ffn_up_w1"] = (gf1["part_up"], c[0][0])
    (dx2a, dsh_m1, dsc_m1, dg_mix1), c = _mod_bwd(dh1, x2, dx3, row(norm_mix_g, 1), row(mod1, 1), "l1_mod_bwd",
                                                  comms=rs_ici(part_wq))
    red["w_q"] = (part_wq, c[0][0])
    (dx2, dsh_kv, dsc_kv, dg_kvn), c = _mod_bwd(dhkv, x2, dx2a, kv_norm_g.reshape(1, -1), row(modkv, 1), "kv_mod_bwd",
                                                comms=rs_d2d(d_wkv))
    part_wkv = rs_add(d_wkv, c, "wkv")

    dx1, gf0, c = ffn_backward(dx2, x1, 0, mod0, ffn0_saved, rs_ici(part_wkv))
    red["w_kv"] = (part_wkv, c[0][0])
    dy0, dgate_m0, d_pw2b = _gate_bwd(dx1, f0, row(mod0, 2), "l0_gate_bwd")
    ds0 = _mm_nt(dy0, W_pw2, 0, name="l0_pw2_dx", out_dtype=bf16, tko=1024, tn=Dm)
    d_pw2 = _mm_tn(s0, dy0, name="l0_pw2_dw", col_sharded=False)
    (dcv, d_lng, d_lnb, d_dwb, d_dww), c = _conv_bwd1(u0, cv0, ds0, dw_w32, dw_b, ln_g, ln_b, "l0_conv_bwd1",
                                                      comms=rs_ici(gf0["part_up"]))
    red["ffn_up_w0"] = (gf0["part_up"], c[0][0])
    (du0, d_pw1b), c = _conv_bwd2(dcv, u0, dw_w32, "l0_conv_bwd2", comms=rs_d2d(d_pw2))
    part_pw2 = rs_add(d_pw2, c, "pw2")
    d_pw1 = _mm_tn(h0, du0, name="l0_pw1_dw", col_sharded=True)
    dh0, c = _mm_nt(du0, W_pw1, 0, name="l0_pw1_dx", out_dtype=f32, tko=1024, tn=2 * Dm, comms=rs_ici(part_pw2) + rs_d2d(d_pw1))
    red["conv_pw2_w"] = (part_pw2, c[0][0])
    part_pw1 = rs_add(d_pw1, c[1:], "pw1")
    (grad_x, dsh_m0, dsc_m0, dg_mix0), c = _mod_bwd(dh0, x0, dx1, row(norm_mix_g, 0), row(mod0, 1), "l0_mod_bwd",
                                                    comms=rs_ici(part_pw1))
    red["conv_pw1_w"] = (part_pw1, c[0][0])
    red["ffn_down_w0"], red["ffn_down_w1"] = gf0["down"], gf1["down"]

    dm0 = [dsh_m0, dsc_m0, dgate_m0, gf0["dsh"], gf0["dsc"], gf0["dgate"]]
    dm1 = [dsh_m1, dsc_m1, dgate_m1, gf1["dsh"], gf1["dsc"], gf1["dgate"]]
    pieces = dm0 + dm1 + [dsh_kv, dsc_kv,
                          dg_mix0, dg_mix1, gf0["dg"], gf1["dg"], dg_kvn, d_kg, d_qg, gf0["d_fb"], gf1["d_fb"],
                          d_pw1b, d_dww[:CONV_K], d_dwb, d_lng, d_lnb, d_pw2b, gf0["d_fw"], gf1["d_fw"]]
    flat = jnp.concatenate([p.reshape(-1) for p in pieces])
    n_flat = flat.shape[0]
    n_rows = -(-n_flat // 1024) * 8
    flat = jnp.concatenate([flat, jnp.zeros((n_rows * 128 - n_flat,), f32)]).reshape(n_rows, 128)
    g_all = _ag_small(flat, "ag_small_grads")
    g_sum = _sum8(g_all, "sum_small_grads").reshape(-1)
    n_dm = 2 * 6 * Dm + 2 * Dm
    dm_all = g_all.reshape(NDEV, -1)[:, :n_dm]

    take_pos = [0]

    def take(shape):
        n = int(np.prod(shape))
        out = g_sum[take_pos[0]:take_pos[0] + n].reshape(shape)
        take_pos[0] += n
        return out

    g_mod_b = take((2, 6 * Dm))
    g_kv_mod_b = take((2 * Dm,))
    g_norm_mix0, g_norm_mix1 = take((Dm,)), take((Dm,))
    g_norm_ffn0, g_norm_ffn1 = take((Dm,)), take((Dm,))
    g_kv_norm = take((Dm,))
    g_k_norm = take((DH,))
    g_q_norm = take((1, DH))
    g_ffn_dw_b = take((2, F))
    shard = lambda full, n, axis: lax.dynamic_slice_in_dim(full, me * n, n, axis)
    g_pw1_b = shard(take((1, 2 * Dm)), n1, 1)
    g_dw_w = shard(take((1, CONV_K, Dm)), nd, 2)
    g_dw_b = shard(take((1, Dm)), nd, 1)
    g_ln_g = shard(take((1, Dm)), nd, 1)
    g_ln_b = shard(take((1, Dm)), nd, 1)
    g_pw2_b = shard(take((1, Dm)), nd, 1)
    g_ffn_dw_w = shard(jnp.stack([take((FFN_K, F)), take((FFN_K, F))]), nfw, 2)
    g_norm_mix = jnp.stack([g_norm_mix0, g_norm_mix1])
    g_norm_ffn = jnp.stack([g_norm_ffn0, g_norm_ffn1])

    small = [("mod_b", mod_b, m_mod_b, v_mod_b, g_mod_b), ("norm_mix_g", norm_mix_g, m_norm_mix_g, v_norm_mix_g, g_norm_mix),
             ("norm_ffn_g", norm_ffn_g, m_norm_ffn_g, v_norm_ffn_g, g_norm_ffn),
             ("conv_pw1_b", conv_pw1_b, m_conv_pw1_b, v_conv_pw1_b, g_pw1_b),
             ("conv_dw_w", conv_dw_w, m_conv_dw_w, v_conv_dw_w, g_dw_w), ("conv_dw_b", conv_dw_b, m_conv_dw_b, v_conv_dw_b, g_dw_b),
             ("conv_ln_g", conv_ln_g, m_conv_ln_g, v_conv_ln_g, g_ln_g), ("conv_ln_b", conv_ln_b, m_conv_ln_b, v_conv_ln_b, g_ln_b),
             ("conv_pw2_b", conv_pw2_b, m_conv_pw2_b, v_conv_pw2_b, g_pw2_b),
             ("kv_mod_b", kv_mod_b, m_kv_mod_b, v_kv_mod_b, g_kv_mod_b), ("kv_norm_g", kv_norm_g, m_kv_norm_g, v_kv_norm_g, g_kv_norm),
             ("k_norm_g", k_norm_g, m_k_norm_g, v_k_norm_g, g_k_norm), ("q_norm_g", q_norm_g, m_q_norm_g, v_q_norm_g, g_q_norm),
             ("ffn_dw_w", ffn_dw_w, m_ffn_dw_w, v_ffn_dw_w, g_ffn_dw_w), ("ffn_dw_b", ffn_dw_b, m_ffn_dw_b, v_ffn_dw_b, g_ffn_dw_b)]
    n_small = sum(int(np.prod(s[1].shape)) for s in small)
    rows_small = -(-n_small // 1024) * 8

    def pack(idx):
        fl = jnp.concatenate([s[idx].reshape(-1) for s in small])
        return jnp.concatenate([fl, jnp.ones((rows_small * 128 - n_small,), f32)]).reshape(rows_small, 128)

    sd, sm, sv = _adamw_plain(pack(1), pack(2), pack(3), pack(4), "adamw_small")
    res = {}
    pos = 0
    for name, w, _, _, g in small:
        n = int(np.prod(w.shape))
        cut = lambda a: a.reshape(-1)[pos:pos + n].reshape(w.shape)
        res[name] = (g.reshape(w.shape), cut(sd), cut(sm), cut(sv))
        pos += n

    c_all_t = jnp.transpose(c_all)

    def mod_update(w2d, m2d, v2d, dm_cols, tag):
        g = _modgrad(c_all_t, dm_cols, f"modgrad_{tag}")
        d, m2, v2 = _adamw_plain(w2d, m2d, v2d, g, f"adamw_{tag}")
        return g, d, m2, v2

    mw = []
    for l in range(2):
        cols = lax.dynamic_slice_in_dim(dm_all[:, l * 6 * Dm:(l + 1) * 6 * Dm], me * n_mod, n_mod, 1)
        mw.append(mod_update(mod_w[l], m_mod_w[l], v_mod_w[l], cols, f"mod_w{l}"))
    res["mod_w"] = tuple(jnp.stack([mw[0][i], mw[1][i]]) for i in range(4))
    cols = lax.dynamic_slice_in_dim(dm_all[:, 12 * Dm:], me * n_kvm, n_kvm, 1)
    res["kv_mod_w"] = mod_update(kv_mod_w, m_kv_mod_w, v_kv_mod_w, cols, "kv_mod_w")

    def mine(part):
        return lax.dynamic_index_in_dim(part, chip[0], 0, keepdims=False)

    def big(key, w, m, v, l, prev, tag, comms=()):
        part, r2 = red[key]
        return _adamw_reduced(as3(w), as3(m), as3(v), mine(part), r2, l, prev, f"adamw_{tag}", comms=comms)

    up1 = big("ffn_up_w1", ffn_up_w, m_ffn_up_w, v_ffn_up_w, 1, None, "up1")
    res["ffn_up_w"] = tuple(big("ffn_up_w0", ffn_up_w, m_ffn_up_w, v_ffn_up_w, 0, up1, "up0"))
    down1 = big("ffn_down_w1", ffn_down_w, m_ffn_down_w, v_ffn_down_w, 1, None, "down1")
    res["ffn_down_w"] = tuple(big("ffn_down_w0", ffn_down_w, m_ffn_down_w, v_ffn_down_w, 0, down1, "down0"))
    for key, w, m, v in (("conv_pw1_w", conv_pw1_w, m_conv_pw1_w, v_conv_pw1_w), ("conv_pw2_w", conv_pw2_w, m_conv_pw2_w, v_conv_pw2_w),
                         ("w_kv", w_kv, m_w_kv, v_w_kv), ("w_q", w_q, m_w_q, v_w_q), ("w_o", w_o, m_w_o, v_w_o)):
        res[key] = tuple(o.reshape(w.shape) for o in big(key, w, m, v, 0, None, key))

    order = ["mod_w", "mod_b", "norm_mix_g", "norm_ffn_g", "conv_pw1_w", "conv_pw1_b", "conv_dw_w", "conv_dw_b", "conv_ln_g",
             "conv_ln_b", "conv_pw2_w", "conv_pw2_b", "kv_mod_w", "kv_mod_b", "kv_norm_g", "w_kv", "k_norm_g", "w_q", "q_norm_g",
             "w_o", "ffn_up_w", "ffn_dw_w", "ffn_dw_b", "ffn_down_w"]
    out = [loss, grad_x.reshape(x.shape)]
    for i in range(4):
        out += [res[n][i] for n in order]
    return tuple(out)
```

```python
import functools
import math

import numpy as np
import jax
import jax.numpy as jnp
from jax import lax
from jax.experimental import pallas as pl
from jax.experimental.pallas import tpu as pltpu

f32 = jnp.float32
bf16 = jnp.bfloat16

D = 2048
SEQ = 8192
FF = 5632
CONV_K = 31
FFN_K = 3
HPG = 8
DH = 128
NG = 3
DILS = (1, 4, 16)
BLK = 128
ROT = 32
THETA = 500000.0
EPS = 1e-6
NEG = -1e30
NDEV = 8
HALO = 32
FHALO = 16
FFN_TM = 1024

LR, B1, B2, AEPS, WD, STEP = 0.001, 0.9, 0.999, 1e-08, 0.01, 10

VMEM_BIG = 56 * 1024 * 1024

ARB = "arbitrary"
PAR = "parallel"
MESH = pl.DeviceIdType.MESH


def _cp(sem, vmem=None):
    return pltpu.CompilerParams(dimension_semantics=sem, vmem_limit_bytes=vmem)


def _tile(n, pref, mult=128):
    if n <= pref:
        return n
    t = (pref // mult) * mult
    while t >= mult:
        if n % t == 0:
            return t
        t -= mult
    return n


def _sigmoid(x):
    return 1.0 / (1.0 + jnp.exp(-x))


def _me():
    return lax.axis_index("x"), lax.axis_index("y"), lax.axis_index("c")


class _Comm:
    def __init__(self, arrays, out_shapes, sems, start, finish, mid=None):
        self.arrays, self.out_shapes, self.sems, self.start, self.finish = arrays, out_shapes, sems, start, finish
        self.mid = mid


def _pcall(body, *, name, grid, in_specs, out_specs, out_shape, args, scratch_shapes=(), sem=None, vmem=None, comms=(),
           aliases=None):
    aliases = aliases or {}
    if not comms:
        return pl.pallas_call(body, name=name, grid=grid, in_specs=in_specs, out_specs=out_specs, out_shape=out_shape,
                              scratch_shapes=list(scratch_shapes), input_output_aliases=aliases,
                              compiler_params=_cp(sem, vmem))(*args)
    single = not isinstance(out_shape, (list, tuple))
    outs_shape = [out_shape] if single else list(out_shape)
    outs_spec = [out_specs] if single else list(out_specs)
    n_in, n_out, n_scr = len(args), len(outs_shape), len(scratch_shapes)
    c_in = [a for cm in comms for a in cm.arrays]
    c_out = [s for cm in comms for s in cm.out_shapes]
    c_scr = [s for cm in comms for s in cm.sems]
    total = int(np.prod(grid))
    late = total - 1 - max(1, total // 8) if total >= 8 else None

    def split(refs, counts):
        out, pos = [], 0
        for n in counts:
            out.append(refs[pos:pos + n])
            pos += n
        return out

    def wrapped(*refs):
        ins, cins, outs, couts, scr, cscr = split(refs, [n_in, len(c_in), n_out, len(c_out), n_scr, len(c_scr)])
        ids = [pl.program_id(a) for a in range(len(grid))]
        first = functools.reduce(jnp.logical_and, [i == 0 for i in ids])
        last = functools.reduce(jnp.logical_and, [i == g - 1 for i, g in zip(ids, grid)])
        per_in = split(cins, [len(cm.arrays) for cm in comms])
        per_out = split(couts, [len(cm.out_shapes) for cm in comms])
        per_sem = split(cscr, [len(cm.sems) for cm in comms])

        @pl.when(first)
        def _():
            for cm, a, b, s in zip(comms, per_in, per_out, per_sem):
                cm.start(a, b, s)

        if late is not None:
            step = functools.reduce(lambda acc, ig: acc * ig[1] + ig[0], zip(ids, grid), 0)

            @pl.when(step == late)
            def _():
                for cm, a, b, s in zip(comms, per_in, per_out, per_sem):
                    if cm.mid is not None:
                        cm.mid(a, b, s)

        body(*ins, *outs, *scr)

        @pl.when(last)
        def _():
            for cm, a, b, s in zip(comms, per_in, per_out, per_sem):
                if late is None and cm.mid is not None:
                    cm.mid(a, b, s)
                cm.finish(a, b, s)

    hbm = pl.BlockSpec(memory_space=pl.ANY)
    res = pl.pallas_call(
        wrapped, name=name, grid=grid, in_specs=list(in_specs) + [hbm] * len(c_in),
        out_specs=outs_spec + [hbm] * len(c_out), out_shape=outs_shape + c_out,
        scratch_shapes=list(scratch_shapes) + c_scr, input_output_aliases=aliases,
        compiler_params=_cp((ARB,) * len(grid), vmem))(*args, *c_in)
    main = res[0] if single else list(res[:n_out])
    return main, split(list(res[n_out:]), [len(cm.out_shapes) for cm in comms])


def _comm_allgather(w, axis):
    n = w.shape[axis]
    out_shape = list(w.shape)
    out_shape[axis] = NDEV * n

    def parts(ins, outs, sems):
        x_ref, out_ref = ins[0], outs[0]
        send_sems, recv_sems, local_sem = sems
        mx, my, mc = _me()
        chips = [(1 - mx, my), (mx, 1 - my), (1 - mx, 1 - my)]

        def blk(px, py, pc):
            start = pl.multiple_of((4 * px + 2 * py + pc) * n, n)
            if axis == 1:
                return out_ref.at[:, pl.ds(start, n), :]
            return out_ref.at[:, :, pl.ds(start, n)]

        def copy(k, block, to, src=None):
            return pltpu.make_async_remote_copy(
                src_ref=blk(*block) if src is None else src, dst_ref=blk(*block),
                send_sem=send_sems.at[k], recv_sem=recv_sems.at[k], device_id=to, device_id_type=MESH)

        me, sibling = (mx, my, mc), (mx, my, 1 - mc)
        mine = pltpu.make_async_copy(x_ref, blk(*me), local_sem)
        first = [copy(0, me, sibling, src=x_ref)] + [copy(1 + j, me, (*chip, mc), src=x_ref) for j, chip in enumerate(chips)]
        passed = [copy(4 + j, (*chip, mc), sibling) for j, chip in enumerate(chips)]
        return me, sibling, chips, mc, copy, mine, first, passed

    def start(ins, outs, sems):
        *_, mine, first, _ = parts(ins, outs, sems)
        mine.start()
        for cp in first:
            cp.start()

    def mid(ins, outs, sems):
        me, sibling, chips, mc, copy, mine, first, passed = parts(ins, outs, sems)
        for j, chip in enumerate(chips):
            copy(1 + j, (*chip, mc), me).wait_recv()
            passed[j].start()

    def finish(ins, outs, sems):
        me, sibling, chips, mc, copy, mine, first, passed = parts(ins, outs, sems)
        copy(0, sibling, me).wait_recv()
        for j, chip in enumerate(chips):
            copy(4 + j, (*chip, 1 - mc), me).wait_recv()
        for cp in first + passed:
            cp.wait_send()
        mine.wait()

    return _Comm([w], [jax.ShapeDtypeStruct(tuple(out_shape), w.dtype)],
                 [pltpu.SemaphoreType.DMA((7,)), pltpu.SemaphoreType.DMA((7,)), pltpu.SemaphoreType.DMA], start, finish, mid)


def _comm_rs_sibling(dwb):
    def copies(ins, outs, sems):
        mx, my, mc = _me()
        return [pltpu.make_async_remote_copy(
            src_ref=ins[0].at[2 * p + (1 - mc)], dst_ref=outs[0].at[p], send_sem=sems[0].at[p], recv_sem=sems[1].at[p],
            device_id=(mx, my, 1 - mc), device_id_type=MESH) for p in range(4)]

    def start(ins, outs, sems):
        for cp in copies(ins, outs, sems):
            cp.start()

    def finish(ins, outs, sems):
        cps = copies(ins, outs, sems)
        for cp in cps:
            cp.wait_recv()
        for cp in cps:
            cp.wait_send()

    return _Comm([dwb], [jax.ShapeDtypeStruct((4,) + dwb.shape[1:], dwb.dtype)],
                 [pltpu.SemaphoreType.DMA((4,)), pltpu.SemaphoreType.DMA((4,))], start, finish)


def _comm_rs_chips(part):
    def copies(ins, outs, sems):
        mx, my, mc = _me()
        chips = [(1 - mx, my), (mx, 1 - my), (1 - mx, 1 - my)]
        return [pltpu.make_async_remote_copy(
            src_ref=ins[0].at[2 * px + py], dst_ref=outs[0].at[k], send_sem=sems[0].at[k], recv_sem=sems[1].at[k],
            device_id=(px, py, mc), device_id_type=MESH) for k, (px, py) in enumerate(chips)]

    def start(ins, outs, sems):
        for cp in copies(ins, outs, sems):
            cp.start()

    def finish(ins, outs, sems):
        cps = copies(ins, outs, sems)
        for cp in cps:
            cp.wait_recv()
        for cp in cps:
            cp.wait_send()

    return _Comm([part], [jax.ShapeDtypeStruct((3,) + part.shape[1:], part.dtype)],
                 [pltpu.SemaphoreType.DMA((3,)), pltpu.SemaphoreType.DMA((3,))], start, finish)


def _ag_small(x, name):
    r, c = x.shape

    def body(x_ref, out_ref, send_sems, recv_sems):
        mx, my, mc = _me()
        mine = 4 * mx + 2 * my + mc
        out_ref[mine] = x_ref[...]
        copies = []
        for k in range(1, NDEV):
            px = 1 - mx if (k >> 2) & 1 else mx
            py = 1 - my if (k >> 1) & 1 else my
            pc = 1 - mc if k & 1 else mc
            cp = pltpu.make_async_remote_copy(
                src_ref=x_ref, dst_ref=out_ref.at[mine], send_sem=send_sems.at[k - 1], recv_sem=recv_sems.at[k - 1],
                device_id=(px, py, pc), device_id_type=MESH)
            cp.start()
            copies.append((cp, 4 * px + 2 * py + pc))
        for k, (cp, peer) in enumerate(copies):
            pltpu.make_async_remote_copy(
                src_ref=x_ref, dst_ref=out_ref.at[peer], send_sem=send_sems.at[k], recv_sem=recv_sems.at[k],
                device_id=(mx, my, mc), device_id_type=MESH).wait_recv()
        for cp, _ in copies:
            cp.wait_send()

    return pl.pallas_call(
        body, name=name,
        out_shape=jax.ShapeDtypeStruct((NDEV, r, c), x.dtype),
        in_specs=[pl.BlockSpec(memory_space=pltpu.VMEM)],
        out_specs=pl.BlockSpec(memory_space=pltpu.VMEM),
        scratch_shapes=[pltpu.SemaphoreType.DMA((NDEV - 1,)), pltpu.SemaphoreType.DMA((NDEV - 1,))],
    )(x)


def _ag_big(w, axis, name):
    n = w.shape[axis]
    out_shape = list(w.shape)
    out_shape[axis] = NDEV * n

    def body(x_ref, out_ref, send_sems, recv_sems, local_sem):
        mx, my, mc = _me()
        me, sibling = (mx, my, mc), (mx, my, 1 - mc)
        chips = [(1 - mx, my), (mx, 1 - my), (1 - mx, 1 - my)]

        def blk(px, py, pc):
            start = pl.multiple_of((4 * px + 2 * py + pc) * n, n)
            if axis == 1:
                return out_ref.at[:, pl.ds(start, n), :]
            return out_ref.at[:, :, pl.ds(start, n)]

        def copy(k, block, to, src=None):
            return pltpu.make_async_remote_copy(
                src_ref=blk(*block) if src is None else src, dst_ref=blk(*block),
                send_sem=send_sems.at[k], recv_sem=recv_sems.at[k], device_id=to, device_id_type=MESH)

        mine = pltpu.make_async_copy(x_ref, blk(*me), local_sem)
        mine.start()
        first = [copy(0, me, sibling, src=x_ref)]
        first += [copy(1 + j, me, (*chip, mc), src=x_ref) for j, chip in enumerate(chips)]
        for cp in first:
            cp.start()
        passed = [copy(4 + j, (*chip, mc), sibling) for j, chip in enumerate(chips)]
        for j, chip in enumerate(chips):
            copy(1 + j, (*chip, mc), me).wait_recv()
            passed[j].start()
        copy(0, sibling, me).wait_recv()
        for j, chip in enumerate(chips):
            copy(4 + j, (*chip, 1 - mc), me).wait_recv()
        for cp in first + passed:
            cp.wait_send()
        mine.wait()

    return pl.pallas_call(
        body, name=name,
        out_shape=jax.ShapeDtypeStruct(tuple(out_shape), w.dtype),
        in_specs=[pl.BlockSpec(memory_space=pl.ANY)],
        out_specs=pl.BlockSpec(memory_space=pl.ANY),
        scratch_shapes=[pltpu.SemaphoreType.DMA((7,)), pltpu.SemaphoreType.DMA((7,)), pltpu.SemaphoreType.DMA],
    )(w)


def _chip_partial(dwb, r1, core, name):
    _, A, B = dwb.shape
    ta = _tile(A, 512, 16)

    def body(c_ref, a_ref, b_ref, o_ref):
        o_ref[...] = (a_ref[...].astype(f32) + b_ref[...].astype(f32)).astype(o_ref.dtype)

    grid_spec = pltpu.PrefetchScalarGridSpec(
        num_scalar_prefetch=1, grid=(4, A // ta),
        in_specs=[pl.BlockSpec((None, ta, B), lambda p, i, c: (2 * p + c[0], i, 0)),
                  pl.BlockSpec((None, ta, B), lambda p, i, c: (p, i, 0))],
        out_specs=pl.BlockSpec((None, ta, B), lambda p, i, c: (p, i, 0)))
    return pl.pallas_call(body, name=name, grid_spec=grid_spec,
                          out_shape=jax.ShapeDtypeStruct((4, A, B), dwb.dtype),
                          compiler_params=_cp((PAR, PAR)))(core, dwb, r1)


def _adam_math(w, g, m, v):
    m2 = B1 * m + (1.0 - B1) * g
    v2 = B2 * v + (1.0 - B2) * (g * g)
    m_hat = m2 / (1.0 - B1 ** STEP)
    v_hat = v2 / (1.0 - B2 ** STEP)
    delta = -LR * (m_hat / (jnp.sqrt(v_hat) + AEPS) + WD * w)
    return delta, m2, v2


def _adamw_reduced(w, m, v, mine, r2, l, prev, name, comms=()):
    L, A, B = w.shape
    ta = _tile(A, 256, 8)

    def body(w_ref, m_ref, v_ref, p_ref, r_ref, *rest):
        g_out, d_out, m_out, v_out = rest[-4:]
        g = ((p_ref[...].astype(f32) + r_ref[0].astype(f32)) + r_ref[1].astype(f32)) + r_ref[2].astype(f32)
        d, m2, v2 = _adam_math(w_ref[...], g, m_ref[...], v_ref[...])
        g_out[...] = g
        d_out[...] = d
        m_out[...] = m2
        v_out[...] = v2

    wspec = pl.BlockSpec((None, ta, B), lambda i: (l, i, 0))
    in_specs = [wspec, wspec, wspec, pl.BlockSpec((ta, B), lambda i: (i, 0)), pl.BlockSpec((3, ta, B), lambda i: (0, i, 0))]
    args = [w, m, v, mine, r2]
    aliases = {}
    if prev is not None:
        in_specs += [pl.BlockSpec(memory_space=pl.ANY)] * 4
        args += list(prev)
        aliases = {5 + i: i for i in range(4)}
    shp = jax.ShapeDtypeStruct((L, A, B), f32)
    return _pcall(body, name=name, grid=(A // ta,), in_specs=in_specs, out_specs=[wspec] * 4, out_shape=[shp] * 4,
                  args=args, sem=(PAR,), comms=comms, aliases=aliases)


def _adamw_plain(w, m, v, g, name):
    A, B = w.shape
    ta = _tile(A, 256, 8)

    def body(w_ref, m_ref, v_ref, g_ref, d_out, m_out, v_out):
        d, m2, v2 = _adam_math(w_ref[...], g_ref[...], m_ref[...], v_ref[...])
        d_out[...] = d
        m_out[...] = m2
        v_out[...] = v2

    spec = pl.BlockSpec((ta, B), lambda i: (i, 0))
    shp = jax.ShapeDtypeStruct((A, B), f32)
    return pl.pallas_call(body, name=name, grid=(A // ta,), in_specs=[spec] * 4, out_specs=[spec] * 3,
                          out_shape=[shp, shp, shp], compiler_params=_cp((PAR,)))(w, m, v, g)


def _sum8(g, name):
    _, R, C = g.shape

    def body(g_ref, o_ref):
        acc = g_ref[0]
        for j in range(1, NDEV):
            acc = acc + g_ref[j]
        o_ref[...] = acc

    return pl.pallas_call(body, name=name, out_shape=jax.ShapeDtypeStruct((R, C), f32))(g)


def _modproj(c_all, w, bias, name):
    K, N = w.shape
    tn = _tile(N, 512)

    def body(c_ref, w_ref, b_ref, o_ref):
        cc = c_ref[...]
        sc = (cc * _sigmoid(cc)).astype(bf16)
        o_ref[...] = jnp.dot(sc, w_ref[...].astype(bf16), preferred_element_type=f32) + b_ref[...]

    return pl.pallas_call(
        body, name=name, grid=(N // tn,),
        in_specs=[pl.BlockSpec((NDEV, K), lambda j: (0, 0)), pl.BlockSpec((K, tn), lambda j: (0, j)),
                  pl.BlockSpec((1, tn), lambda j: (0, j))],
        out_specs=pl.BlockSpec((NDEV, tn), lambda j: (0, j)),
        out_shape=jax.ShapeDtypeStruct((NDEV, N), f32), compiler_params=_cp((PAR,)))(c_all, w, bias)


def _modgrad(c_all_t, dm, name):
    K = c_all_t.shape[0]
    N = dm.shape[1]
    tn = _tile(N, 512)

    def body(c_ref, d_ref, o_ref):
        cc = c_ref[...]
        sc = cc * _sigmoid(cc)
        dmv = d_ref[...]
        acc = sc[:, 0:1] * dmv[0:1, :]
        for b in range(1, NDEV):
            acc = acc + sc[:, b:b + 1] * dmv[b:b + 1, :]
        o_ref[...] = acc

    return pl.pallas_call(
        body, name=name, grid=(N // tn,),
        in_specs=[pl.BlockSpec((K, NDEV), lambda j: (0, 0)), pl.BlockSpec((NDEV, tn), lambda j: (0, j))],
        out_specs=pl.BlockSpec((K, tn), lambda j: (0, j)),
        out_shape=jax.ShapeDtypeStruct((K, N), f32), compiler_params=_cp((PAR,)))(c_all_t, dm)


def _mm_nn(a, w, l, *, name, out_dtype=bf16, bias=None, res=None, gate=None, tm=1024, tn=1024, tk=2048, comms=()):
    M, K = a.shape
    N = w.shape[2]
    tm, tn, tk = _tile(M, tm, 8), _tile(N, tn), _tile(K, tk)
    nk = K // tk
    epi = res is not None

    def body(*refs):
        it = iter(refs)
        a_ref, w_ref = next(it), next(it)
        b_ref = next(it) if bias is not None else None
        r_ref = next(it) if epi else None
        g_ref = next(it) if epi else None
        o_ref = next(it)
        f_ref = next(it) if epi else None

        def finish(y):
            if b_ref is not None:
                y = y + b_ref[...]
            if epi:
                f_ref[...] = y.astype(f_ref.dtype)
                o_ref[...] = r_ref[...] + g_ref[...] * y
            else:
                o_ref[...] = y.astype(o_ref.dtype)

        if nk == 1:
            finish(jnp.dot(a_ref[...], w_ref[...], preferred_element_type=f32))
            return
        acc = next(it)
        k = pl.program_id(2)

        @pl.when(k == 0)
        def _():
            acc[...] = jnp.zeros_like(acc)

        acc[...] += jnp.dot(a_ref[...], w_ref[...], preferred_element_type=f32)

        @pl.when(k == nk - 1)
        def _():
            finish(acc[...])

    in_specs = [pl.BlockSpec((tm, tk), lambda i, j, k: (i, k)), pl.BlockSpec((None, tk, tn), lambda i, j, k: (l, k, j))]
    args = [a, w]
    if bias is not None:
        in_specs.append(pl.BlockSpec((1, tn), lambda i, j, k: (0, j)))
        args.append(bias)
    ospec = pl.BlockSpec((tm, tn), lambda i, j, k: (i, j))
    if epi:
        in_specs += [ospec, pl.BlockSpec((1, tn), lambda i, j, k: (0, j))]
        args += [res, gate]
        out_shape = [jax.ShapeDtypeStruct((M, N), f32), jax.ShapeDtypeStruct((M, N), bf16)]
        out_specs = [ospec, ospec]
    else:
        out_shape = jax.ShapeDtypeStruct((M, N), out_dtype)
        out_specs = ospec
    return _pcall(body, name=name, grid=(M // tm, N // tn, nk), in_specs=in_specs, out_specs=out_specs, out_shape=out_shape,
                  args=args, scratch_shapes=[pltpu.VMEM((tm, tn), f32)] if nk > 1 else [], sem=(PAR, PAR, ARB), vmem=VMEM_BIG,
                  comms=comms)


def _mm_nt(a, w, l, *, name, out_dtype, tm=1024, tko=2048, tn=1024, comms=()):
    planes = a.ndim == 3
    M = a.shape[-2]
    K, N = w.shape[1], w.shape[2]
    npl = a.shape[-1]
    tm, tko = _tile(M, tm, 8), _tile(K, tko)
    tn = _tile(npl, tn)
    nn = N // tn
    per_plane = npl // tn

    def body(a_ref, w_ref, o_ref, *scratch):
        if nn == 1:
            o_ref[...] = _dot_nt(a_ref[...], w_ref[...]).astype(o_ref.dtype)
            return
        acc = scratch[0]
        k = pl.program_id(2)

        @pl.when(k == 0)
        def _():
            acc[...] = jnp.zeros_like(acc)

        acc[...] += _dot_nt(a_ref[...], w_ref[...])

        @pl.when(k == nn - 1)
        def _():
            o_ref[...] = acc[...].astype(o_ref.dtype)

    if planes:
        a_spec = pl.BlockSpec((None, tm, tn), lambda i, j, k: (k // per_plane, i, k % per_plane))
    else:
        a_spec = pl.BlockSpec((tm, tn), lambda i, j, k: (i, k))
    return _pcall(body, name=name, grid=(M // tm, K // tko, nn),
                  in_specs=[a_spec, pl.BlockSpec((None, tko, tn), lambda i, j, k: (l, j, k))],
                  out_specs=pl.BlockSpec((tm, tko), lambda i, j, k: (i, j)),
                  out_shape=jax.ShapeDtypeStruct((M, K), out_dtype), args=[a, w],
                  scratch_shapes=[pltpu.VMEM((tm, tko), f32)] if nn > 1 else [], sem=(PAR, PAR, ARB), vmem=VMEM_BIG,
                  comms=comms)


def _mm_tn(a, b, *, name, col_sharded, comms=()):
    planes = b.ndim == 3
    S, K = a.shape
    N = b.shape[-1] * (2 if planes else 1)
    if col_sharded:
        tn, tk, ts = N // NDEV, _tile(K, 1024), _tile(S, 2048, 16)
    else:
        tn, tk, ts = N, _tile(K, 1408), _tile(S, 1024, 16)
    ns_steps = S // ts
    per_plane = (b.shape[-1] // tn) if planes else 0

    def body(a_ref, b_ref, o_ref, acc):
        s = pl.program_id(2)

        @pl.when(s == 0)
        def _():
            acc[...] = jnp.zeros_like(acc)

        acc[...] += lax.dot_general(a_ref[...], b_ref[...], (((0,), (0,)), ((), ())), preferred_element_type=f32)

        @pl.when(s == ns_steps - 1)
        def _():
            o_ref[...] = acc[...].astype(o_ref.dtype)

    if planes:
        b_spec = pl.BlockSpec((None, ts, tn), lambda k, n, s: (n // per_plane, s, n % per_plane))
    else:
        b_spec = pl.BlockSpec((ts, tn), lambda k, n, s: (s, n))
    if col_sharded:
        out_shape = jax.ShapeDtypeStruct((NDEV, K, tn), bf16)
        out_spec = pl.BlockSpec((None, tk, tn), lambda k, n, s: (n, k, 0))
    else:
        out_shape = jax.ShapeDtypeStruct((K, N), bf16)
        out_spec = pl.BlockSpec((tk, tn), lambda k, n, s: (k, n))
    res = _pcall(body, name=name, grid=(K // tk, N // tn, ns_steps),
                 in_specs=[pl.BlockSpec((ts, tk), lambda k, n, s: (s, k)), b_spec],
                 out_specs=out_spec, out_shape=out_shape, args=[a, b],
                 scratch_shapes=[pltpu.VMEM((tk, tn), f32)], sem=(PAR, PAR, ARB), vmem=VMEM_BIG, comms=comms)
    out, couts = res if comms else (res, None)
    if not col_sharded:
        out = out.reshape(NDEV, K // NDEV, N)
    return (out, couts) if comms else out


def _acc_spec(w, rows=1):
    return pl.BlockSpec((rows, w), lambda i: (0, 0))


def _mod_fwd(x, g, sh, sc, name, comms=()):
    S, W = x.shape
    tm = _tile(S, 256, 8)

    def body(x_ref, g_ref, sh_ref, sc_ref, h_ref):
        xv = x_ref[...]
        r = lax.rsqrt(jnp.mean(xv * xv, axis=-1, keepdims=True) + EPS)
        h_ref[...] = ((xv * r) * g_ref[...] * (1.0 + sc_ref[...]) + sh_ref[...]).astype(h_ref.dtype)

    row = pl.BlockSpec((tm, W), lambda i: (i, 0))
    return _pcall(body, name=name, grid=(S // tm,), in_specs=[row, _acc_spec(W), _acc_spec(W), _acc_spec(W)],
                  out_specs=row, out_shape=jax.ShapeDtypeStruct((S, W), bf16), args=[x, g, sh, sc], sem=(PAR,), comms=comms)


def _gate_tail(d, f_ref, gate_ref, df_ref, dgate_ref, sdf_ref, first):
    @pl.when(first)
    def _():
        dgate_ref[...] = jnp.zeros_like(dgate_ref)
        sdf_ref[...] = jnp.zeros_like(sdf_ref)

    df = gate_ref[...] * d
    df_ref[...] = df.astype(df_ref.dtype)
    dgate_ref[...] += jnp.sum(d * f_ref[...].astype(f32), axis=0, keepdims=True)
    sdf_ref[...] += jnp.sum(df, axis=0, keepdims=True)


def _mod_bwd(dh, x, dx_in, g, sc, name, comms=(), gate_next=None):
    S, W = x.shape
    tm = _tile(S, 256, 16)
    nt = S // tm
    fused = gate_next is not None

    def body(dh_ref, x_ref, dxi_ref, g_ref, sc_ref, *rest):
        if fused:
            f_ref, gate_ref, dx_ref, dsh_ref, dsc_ref, dg_ref, df_ref, dgate_ref, sdf_ref = rest
        else:
            dx_ref, dsh_ref, dsc_ref, dg_ref = rest
        i = pl.program_id(0)

        @pl.when(i == 0)
        def _():
            dsh_ref[...] = jnp.zeros_like(dsh_ref)
            dsc_ref[...] = jnp.zeros_like(dsc_ref)

        xv = x_ref[...]
        dh = dh_ref[...].astype(f32)
        r = lax.rsqrt(jnp.mean(xv * xv, axis=-1, keepdims=True) + EPS)
        n = xv * r
        dn = dh * (g_ref[...] * (1.0 + sc_ref[...]))
        dx = dxi_ref[...] + r * (dn - n * jnp.mean(dn * n, axis=-1, keepdims=True))
        dx_ref[...] = dx
        dsh_ref[...] += jnp.sum(dh, axis=0, keepdims=True)
        dsc_ref[...] += jnp.sum(dh * n, axis=0, keepdims=True)
        if fused:
            _gate_tail(dx, f_ref, gate_ref, df_ref, dgate_ref, sdf_ref, i == 0)

        @pl.when(i == nt - 1)
        def _():
            a2 = dsc_ref[...]
            dg_ref[...] = a2 * (1.0 + sc_ref[...])
            dsc_ref[...] = a2 * g_ref[...]

    row = pl.BlockSpec((tm, W), lambda i: (i, 0))
    vec = jax.ShapeDtypeStruct((1, W), f32)
    in_specs, args = [row, row, row, _acc_spec(W), _acc_spec(W)], [dh, x, dx_in, g, sc]
    out_specs = [row, _acc_spec(W), _acc_spec(W), _acc_spec(W)]
    out_shape = [jax.ShapeDtypeStruct((S, W), f32), vec, vec, vec]
    if fused:
        in_specs, args = in_specs + [row, _acc_spec(W)], args + list(gate_next)
        out_specs = out_specs + [row, _acc_spec(W), _acc_spec(W)]
        out_shape = out_shape + [jax.ShapeDtypeStruct((S, W), bf16), vec, vec]
    return _pcall(body, name=name, grid=(nt,), in_specs=in_specs, out_specs=out_specs, out_shape=out_shape, args=args,
                  sem=(ARB,), comms=comms)


def _loss_grad(y, target, f, gate, name):
    S, W = y.shape
    tm = _tile(S, 256, 16)

    def body(y_ref, t_ref, f_ref, gate_ref, dy_ref, l_ref, df_ref, dgate_ref, sdf_ref):
        i = pl.program_id(0)

        @pl.when(i == 0)
        def _():
            l_ref[...] = jnp.zeros_like(l_ref)

        e = y_ref[...] - t_ref[...]
        dy = e * (1.0 / W)
        dy_ref[...] = dy
        l_ref[...] += 0.5 * jnp.sum(jnp.mean(e * e, axis=-1, keepdims=True))
        _gate_tail(dy, f_ref, gate_ref, df_ref, dgate_ref, sdf_ref, i == 0)

    row = pl.BlockSpec((tm, W), lambda i: (i, 0))
    vec = jax.ShapeDtypeStruct((1, W), f32)
    return pl.pallas_call(
        body, name=name, grid=(S // tm,), in_specs=[row, row, row, _acc_spec(W)],
        out_specs=[row, pl.BlockSpec((8, 128), lambda i: (0, 0)), row, _acc_spec(W), _acc_spec(W)],
        out_shape=[jax.ShapeDtypeStruct((S, W), f32), jax.ShapeDtypeStruct((8, 128), f32), jax.ShapeDtypeStruct((S, W), bf16), vec, vec],
        compiler_params=_cp((ARB,)))(y, target, f, gate)


def _tap_groups(offsets):
    groups = {}
    for k, o in enumerate(offsets):
        groups.setdefault(o % 8, []).append((k, o - o % 8))
    return sorted(groups.items())


def _tap_sum(buf, w, offsets, tm):
    out = None
    for b, taps in _tap_groups(offsets):
        n = tm + 8 if b else tm
        y = None
        for k, base in taps:
            term = w[k:k + 1, :] * buf[pl.ds(base, n), :]
            y = term if y is None else y + term
        part = y[b:b + tm] if b else y
        out = part if out is None else out + part
    return out


def _tap_wgrad(d, buf, dsh, acc_ref, offsets, tm):
    for b, taps in _tap_groups(offsets):
        if b:
            dsh[pl.ds(0, 8), :] = jnp.zeros((8, dsh.shape[1]), f32)
            dsh[pl.ds(tm, 8), :] = jnp.zeros((8, dsh.shape[1]), f32)
            dsh[pl.ds(b, tm), :] = d
            dd, n = dsh[...], tm + 8
        else:
            dd, n = d, tm
        for k, base in taps:
            acc_ref[pl.ds(k, 1), :] += jnp.sum(dd * buf[pl.ds(base, n), :], axis=0, keepdims=True)


_CONV_OFFSETS = [HALO - (CONV_K - 1) + k for k in range(CONV_K)]
_CONV_OFFSETS_T = [CONV_K - 1 - k for k in range(CONV_K)]


def _conv_core(u_ref, uh_ref, w_ref, b_ref, lg_ref, lb_ref, gbuf, tm, first, cv_ref=None):
    C = u_ref.shape[1] // 2
    u = u_ref[...].astype(f32)
    uh = uh_ref[...].astype(f32)
    gbuf[pl.ds(HALO, tm), :] = u[:, :C] * _sigmoid(u[:, C:])
    halo = uh[:, :C] * _sigmoid(uh[:, C:])
    gbuf[pl.ds(0, HALO), :] = jnp.where(first, 0.0, halo)
    cv = _tap_sum(gbuf, w_ref[...], _CONV_OFFSETS, tm) + b_ref[...] if cv_ref is None else cv_ref[...]
    mu = jnp.mean(cv, axis=-1, keepdims=True)
    xc = cv - mu
    rstd = lax.rsqrt(jnp.mean(xc * xc, axis=-1, keepdims=True) + EPS)
    z = xc * rstd
    ln = z * lg_ref[...] + lb_ref[...]
    return cv, z, rstd, ln


def _halo_prev(tm, hb, w):
    return pl.BlockSpec((hb, w), lambda i: (jnp.maximum(i * (tm // hb) - 1, 0), 0))


def _conv_fwd(u, w, b, lg, lb, name, comms=()):
    S, C2 = u.shape
    C = C2 // 2
    tm = _tile(S, 256, HALO)

    def body(u_ref, uh_ref, w_ref, b_ref, lg_ref, lb_ref, s_ref, cv_ref, gbuf):
        first = pl.program_id(0) == 0
        cv, _, _, ln = _conv_core(u_ref, uh_ref, w_ref, b_ref, lg_ref, lb_ref, gbuf, tm, first)
        s_ref[...] = (ln * _sigmoid(ln)).astype(s_ref.dtype)
        cv_ref[...] = cv

    return _pcall(body, name=name, grid=(S // tm,),
                  in_specs=[pl.BlockSpec((tm, C2), lambda i: (i, 0)), _halo_prev(tm, HALO, C2), _acc_spec(C, 32),
                            _acc_spec(C), _acc_spec(C), _acc_spec(C)],
                  out_specs=[pl.BlockSpec((tm, C), lambda i: (i, 0))] * 2,
                  out_shape=[jax.ShapeDtypeStruct((S, C), bf16), jax.ShapeDtypeStruct((S, C), f32)],
                  args=[u, u, w, b, lg, lb], scratch_shapes=[pltpu.VMEM((tm + HALO, C), f32)], sem=(PAR,), vmem=VMEM_BIG,
                  comms=comms)


def _conv_bwd1(u, cv, ds, w, b, lg, lb, name, comms=()):
    S, C2 = u.shape
    C = C2 // 2
    tm = _tile(S, 256, HALO)

    def body(u_ref, uh_ref, cv_ref, ds_ref, w_ref, b_ref, lg_ref, lb_ref, dcv_ref, dlg_ref, dlb_ref, ddb_ref, ddw_ref, gbuf, dsh):
        i = pl.program_id(0)

        @pl.when(i == 0)
        def _():
            dlg_ref[...] = jnp.zeros_like(dlg_ref)
            dlb_ref[...] = jnp.zeros_like(dlb_ref)
            ddb_ref[...] = jnp.zeros_like(ddb_ref)
            ddw_ref[...] = jnp.zeros_like(ddw_ref)

        _, z, rstd, ln = _conv_core(u_ref, uh_ref, w_ref, b_ref, lg_ref, lb_ref, gbuf, tm, i == 0, cv_ref)
        sg = _sigmoid(ln)
        dln = ds_ref[...].astype(f32) * (sg * (1.0 + ln * (1.0 - sg)))
        dlg_ref[...] += jnp.sum(dln * z, axis=0, keepdims=True)
        dlb_ref[...] += jnp.sum(dln, axis=0, keepdims=True)
        dz = dln * lg_ref[...]
        dcv = rstd * (dz - jnp.mean(dz, axis=-1, keepdims=True) - z * jnp.mean(dz * z, axis=-1, keepdims=True))
        dcv_ref[...] = dcv
        ddb_ref[...] += jnp.sum(dcv, axis=0, keepdims=True)
        _tap_wgrad(dcv, gbuf, dsh, ddw_ref, _CONV_OFFSETS, tm)

    vec = jax.ShapeDtypeStruct((1, C), f32)
    return _pcall(
        body, name=name, grid=(S // tm,),
        in_specs=[pl.BlockSpec((tm, C2), lambda i: (i, 0)), _halo_prev(tm, HALO, C2), pl.BlockSpec((tm, C), lambda i: (i, 0)),
                  pl.BlockSpec((tm, C), lambda i: (i, 0)), _acc_spec(C, 32), _acc_spec(C), _acc_spec(C), _acc_spec(C)],
        out_specs=[pl.BlockSpec((tm, C), lambda i: (i, 0)), _acc_spec(C), _acc_spec(C), _acc_spec(C), _acc_spec(C, 32)],
        out_shape=[jax.ShapeDtypeStruct((S, C), f32), vec, vec, vec, jax.ShapeDtypeStruct((32, C), f32)],
        args=[u, u, cv, ds, w, b, lg, lb], scratch_shapes=[pltpu.VMEM((tm + HALO, C), f32), pltpu.VMEM((tm + 8, C), f32)],
        sem=(ARB,), vmem=VMEM_BIG, comms=comms)


def _conv_bwd2(dcv, u, w, name, comms=()):
    S, C2 = u.shape
    C = C2 // 2
    tm = _tile(S, 256, HALO)
    nt = S // tm
    nhb = S // HALO

    def body(d_ref, dn_ref, u_ref, w_ref, du_ref, db_ref, dbuf):
        i = pl.program_id(0)

        @pl.when(i == 0)
        def _():
            db_ref[...] = jnp.zeros_like(db_ref)

        dbuf[pl.ds(0, tm), :] = d_ref[...]
        dbuf[pl.ds(tm, HALO), :] = jnp.where(i == nt - 1, 0.0, dn_ref[...])
        dglu = _tap_sum(dbuf, w_ref[...], _CONV_OFFSETS_T, tm)
        u = u_ref[...].astype(f32)
        a, gt = u[:, :C], u[:, C:]
        sg = _sigmoid(gt)
        da = dglu * sg
        dgt = dglu * a * sg * (1.0 - sg)
        du_ref[:, :C] = da.astype(du_ref.dtype)
        du_ref[:, C:] = dgt.astype(du_ref.dtype)
        db_ref[:, :C] += jnp.sum(da, axis=0, keepdims=True)
        db_ref[:, C:] += jnp.sum(dgt, axis=0, keepdims=True)

    return _pcall(
        body, name=name, grid=(nt,),
        in_specs=[pl.BlockSpec((tm, C), lambda i: (i, 0)),
                  pl.BlockSpec((HALO, C), lambda i: (jnp.minimum((i + 1) * (tm // HALO), nhb - 1), 0)),
                  pl.BlockSpec((tm, C2), lambda i: (i, 0)), _acc_spec(C, 32)],
        out_specs=[pl.BlockSpec((tm, C2), lambda i: (i, 0)), _acc_spec(C2)],
        out_shape=[jax.ShapeDtypeStruct((S, C2), bf16), jax.ShapeDtypeStruct((1, C2), f32)],
        args=[dcv, dcv, u, w], scratch_shapes=[pltpu.VMEM((tm + HALO, C), f32)], sem=(ARB,), vmem=VMEM_BIG, comms=comms)


def _up_gate(h, w_up, w, b, name, comms=()):
    S, K = h.shape
    F = w.shape[1]
    tm = _tile(S, FFN_TM, 16)
    tn = _tile(F, 512)
    nf = F // tn
    ch = _tile(tn, 256)

    def body(h_ref, wg_ref, wv_ref, w_ref, b_ref, u_ref, a_ref, tail):
        i, j = pl.program_id(0), pl.program_id(1)

        @pl.when(i == 0)
        def _():
            tail[j] = jnp.zeros((8, tn), f32)

        hv = h_ref[...]
        for c in range(tn // ch):
            cs = slice(c * ch, (c + 1) * ch)
            g16 = jnp.dot(hv, wg_ref[:, cs], preferred_element_type=f32).astype(bf16)
            v16 = jnp.dot(hv, wv_ref[:, cs], preferred_element_type=f32).astype(bf16)
            u_ref[0, :, cs] = g16
            u_ref[1, :, cs] = v16
            g = g16.astype(f32)
            ext = jnp.concatenate([tail[j, :, cs], g], axis=0)
            gc = b_ref[:, cs] + w_ref[0:1, cs] * ext[6:6 + tm] + w_ref[1:2, cs] * ext[7:7 + tm] + w_ref[2:3, cs] * g
            a_ref[:, cs] = (gc * _sigmoid(gc) * v16.astype(f32)).astype(a_ref.dtype)
            tail[j, :, cs] = g[tm - 8:tm]

    return _pcall(
        body, name=name, grid=(S // tm, nf),
        in_specs=[pl.BlockSpec((tm, K), lambda i, j: (i, 0)),
                  pl.BlockSpec((None, K, tn), lambda i, j: (0, 0, j)), pl.BlockSpec((None, K, tn), lambda i, j: (0, 0, nf + j)),
                  pl.BlockSpec((8, tn), lambda i, j: (0, j)), pl.BlockSpec((1, tn), lambda i, j: (0, j))],
        out_specs=[pl.BlockSpec((2, tm, tn), lambda i, j: (0, i, j)), pl.BlockSpec((tm, tn), lambda i, j: (i, j))],
        out_shape=[jax.ShapeDtypeStruct((2, S, F), bf16), jax.ShapeDtypeStruct((S, F), bf16)],
        args=[h, w_up, w_up, w, b], scratch_shapes=[pltpu.VMEM((nf, 8, tn), f32)], sem=(ARB, ARB), vmem=VMEM_BIG, comms=comms)


def _dact_gate_bwd(df, w_down, u2, w, b, name, comms=()):
    S, D_ = df.shape
    F = w.shape[1]
    tm = _tile(S, FFN_TM, 16)
    tn = _tile(F, 512)
    nf, nt = F // tn, S // tm
    ch = _tile(tn, 256)
    hb = tm // FHALO

    def body(df_ref, wd_ref, g_ref, gp_ref, v_ref, w_ref, b_ref, du_ref, dw_ref, db_ref, head):
        ii = pl.program_id(1)
        first_tile = ii == nt - 1

        @pl.when(ii == 0)
        def _():
            dw_ref[...] = jnp.zeros_like(dw_ref)
            db_ref[...] = jnp.zeros_like(db_ref)
            head[...] = jnp.zeros_like(head)

        dfv = df_ref[...]
        for c in range(tn // ch):
            cs = slice(c * ch, (c + 1) * ch)
            dact = _dot_nt(dfv, wd_ref[cs, :])
            hist = jnp.where(first_tile, 0.0, gp_ref[:, cs].astype(f32))
            g = jnp.concatenate([hist, g_ref[:, cs].astype(f32)], axis=0)
            taps = [g[FHALO - 2 + k:FHALO - 2 + k + tm] for k in range(FFN_K)]
            gc = b_ref[:, cs] + w_ref[0:1, cs] * taps[0] + w_ref[1:2, cs] * taps[1] + w_ref[2:3, cs] * taps[2]
            sg = _sigmoid(gc)
            dgc = dact * v_ref[:, cs].astype(f32) * (sg * (1.0 + gc * (1.0 - sg)))
            du_ref[1, :, cs] = (dact * (gc * sg)).astype(du_ref.dtype)
            ext = jnp.concatenate([dgc, head[:, cs]], axis=0)
            dgt = w_ref[0:1, cs] * ext[2:2 + tm] + w_ref[1:2, cs] * ext[1:1 + tm] + w_ref[2:3, cs] * dgc
            du_ref[0, :, cs] = dgt.astype(du_ref.dtype)
            db_ref[:, cs] += jnp.sum(dgc, axis=0, keepdims=True)
            for k in range(FFN_K):
                dw_ref[pl.ds(k, 1), cs] += jnp.sum(dgc * taps[k], axis=0, keepdims=True)
            head[:, cs] = dgc[0:8]

    rev = lambda ii: nt - 1 - ii
    return _pcall(
        body, name=name, grid=(nf, nt), comms=comms, sem=(ARB, ARB), vmem=VMEM_BIG,
        args=[df, w_down, u2, u2, u2, w, b],
        in_specs=[pl.BlockSpec((tm, D_), lambda j, ii: (rev(ii), 0)),
                  pl.BlockSpec((None, tn, D_), lambda j, ii: (0, j, 0)),
                  pl.BlockSpec((None, tm, tn), lambda j, ii: (0, rev(ii), j)),
                  pl.BlockSpec((None, FHALO, tn), lambda j, ii: (0, jnp.maximum(rev(ii) * hb - 1, 0), j)),
                  pl.BlockSpec((None, tm, tn), lambda j, ii: (1, rev(ii), j)),
                  pl.BlockSpec((8, tn), lambda j, ii: (0, j)), pl.BlockSpec((1, tn), lambda j, ii: (0, j))],
        out_specs=[pl.BlockSpec((2, tm, tn), lambda j, ii: (0, rev(ii), j)), pl.BlockSpec((8, tn), lambda j, ii: (0, j)),
                   pl.BlockSpec((1, tn), lambda j, ii: (0, j))],
        out_shape=[jax.ShapeDtypeStruct((2, S, F), bf16), jax.ShapeDtypeStruct((8, F), f32), jax.ShapeDtypeStruct((1, F), f32)],
        scratch_shapes=[pltpu.VMEM((8, tn), f32)])


def _ffn_gate_bwd(u2, dact, w, b, name, comms=()):
    _, S, F = u2.shape
    cw = _tile(F, 1408)
    ncw = F // cw
    tm = _tile(S, 256, FHALO)
    nt = S // tm
    nhb = S // FHALO
    R = tm + 2 * FHALO

    def body(g_ref, gp_ref, gn_ref, v_ref, vn_ref, d_ref, dn_ref, w_ref, b_ref, du_ref, dw_ref, db_ref, gbuf, dbuf):
        i = pl.program_id(1)
        first, last = i == 0, i == nt - 1

        @pl.when(i == 0)
        def _():
            dw_ref[...] = jnp.zeros_like(dw_ref)
            db_ref[...] = jnp.zeros_like(db_ref)

        gbuf[pl.ds(0, FHALO), :] = jnp.where(first, 0.0, gp_ref[...].astype(f32))
        gbuf[pl.ds(FHALO, tm), :] = g_ref[...].astype(f32)
        gbuf[pl.ds(FHALO + tm, FHALO), :] = gn_ref[...].astype(f32)
        w = w_ref[...]
        n_ext = tm + FHALO
        gc = jnp.zeros((n_ext, cw), f32) + b_ref[...]
        for k in range(FFN_K):
            gc = gc + w[k:k + 1, :] * gbuf[pl.ds(FHALO - (FFN_K - 1) + k, n_ext), :]
        sg = _sigmoid(gc)
        val = jnp.concatenate([v_ref[...].astype(f32), vn_ref[...].astype(f32)], axis=0)
        dact_ext = jnp.concatenate([d_ref[...].astype(f32), jnp.where(last, 0.0, dn_ref[...].astype(f32))], axis=0)
        dgc = dact_ext * val * (sg * (1.0 + gc * (1.0 - sg)))
        dbuf[...] = dgc
        dval = dact_ext[:tm] * (gc[:tm] * sg[:tm])
        dgt = jnp.zeros((tm, cw), f32)
        for k in range(FFN_K):
            dgt = dgt + w[k:k + 1, :] * dbuf[pl.ds(FFN_K - 1 - k, tm), :]
        du_ref[0] = dgt.astype(du_ref.dtype)
        du_ref[1] = dval.astype(du_ref.dtype)
        dgc_t = dgc[:tm]
        db_ref[...] += jnp.sum(dgc_t, axis=0, keepdims=True)
        for k in range(FFN_K):
            dw_ref[pl.ds(k, 1), :] += jnp.sum(dgc_t * gbuf[pl.ds(FHALO - (FFN_K - 1) + k, tm), :], axis=0, keepdims=True)

    hb = tm // FHALO
    prev = lambda j, i: (jnp.maximum(i * hb - 1, 0), j)
    nxt = lambda j, i: (jnp.minimum((i + 1) * hb, nhb - 1), j)
    plane = lambda p, f: (lambda j, i: (p,) + f(j, i))
    return _pcall(
        body, name=name, grid=(ncw, nt), comms=comms, sem=(PAR, ARB), vmem=VMEM_BIG,
        args=[u2, u2, u2, u2, u2, dact, dact, w, b],
        in_specs=[pl.BlockSpec((None, tm, cw), lambda j, i: (0, i, j)), pl.BlockSpec((None, FHALO, cw), plane(0, prev)),
                  pl.BlockSpec((None, FHALO, cw), plane(0, nxt)),
                  pl.BlockSpec((None, tm, cw), lambda j, i: (1, i, j)), pl.BlockSpec((None, FHALO, cw), plane(1, nxt)),
                  pl.BlockSpec((tm, cw), lambda j, i: (i, j)), pl.BlockSpec((FHALO, cw), nxt),
                  pl.BlockSpec((8, cw), lambda j, i: (0, j)), pl.BlockSpec((1, cw), lambda j, i: (0, j))],
        out_specs=[pl.BlockSpec((2, tm, cw), lambda j, i: (0, i, j)), pl.BlockSpec((8, cw), lambda j, i: (0, j)),
                   pl.BlockSpec((1, cw), lambda j, i: (0, j))],
        out_shape=[jax.ShapeDtypeStruct((2, S, F), bf16), jax.ShapeDtypeStruct((8, F), f32), jax.ShapeDtypeStruct((1, F), f32)],
        scratch_shapes=[pltpu.VMEM((R, cw), f32), pltpu.VMEM((tm + FHALO, cw), f32)])


def _rope_tables(pos_col, name):
    S = pos_col.shape[0]
    tm = _tile(S, 512, 8)
    half = ROT // 2
    inv = THETA ** (-np.arange(0, ROT, 2, dtype=np.float32) / ROT)
    lane_freq = np.zeros((1, DH), np.float32)
    lane_freq[0, :half] = inv
    lane_freq[0, half:ROT] = inv
    lane_freq = jnp.asarray(lane_freq)

    def body(p_ref, fr_ref, c_ref, sa_ref, sb_ref):
        ang = p_ref[...].astype(f32) * fr_ref[...]
        lane = lax.broadcasted_iota(jnp.int32, (tm, DH), 1)
        cs, sn = jnp.cos(ang), jnp.sin(ang)
        c_ref[...] = jnp.where(lane < ROT, cs, 1.0)
        sa_ref[...] = jnp.where(lane < half, -sn, 0.0)
        sb_ref[...] = jnp.where((lane >= half) & (lane < ROT), sn, 0.0)

    row = pl.BlockSpec((tm, DH), lambda i: (i, 0))
    shp = jax.ShapeDtypeStruct((S, DH), f32)
    return pl.pallas_call(body, name=name, grid=(S // tm,),
                          in_specs=[pl.BlockSpec((tm, 1), lambda i: (i, 0)), pl.BlockSpec((1, DH), lambda i: (0, 0))],
                          out_specs=[row, row, row], out_shape=[shp, shp, shp], compiler_params=_cp((PAR,)))(pos_col, lane_freq)


def _swap_matrix():
    k = lax.broadcasted_iota(jnp.int32, (DH, DH), 0)
    i = lax.broadcasted_iota(jnp.int32, (DH, DH), 1)
    half = ROT // 2
    hit = ((i < half) & (k == i + half)) | ((i >= half) & (i < ROT) & (k == i - half))
    return jnp.where(hit, 1.0, 0.0).astype(bf16)


def _head_mean(x):
    return jnp.dot(x.astype(bf16), jnp.ones((DH, DH), bf16), preferred_element_type=f32) * (1.0 / DH)


def _rope(n, c, t, swap):
    return n * c + jnp.dot(n.astype(bf16), swap, preferred_element_type=f32) * t


def _rope_t(d, c, t, swap):
    return d * c + jnp.dot((d * t).astype(bf16), swap, preferred_element_type=f32)


def _qk_fwd(raw, g, tabs, width, with_values, name):
    S = raw.shape[0]
    nh = width // DH
    ow = width // NG
    hpg = ow // DH
    tm = _tile(S, 256, 16 * max(DILS))
    vgroups = [gi for gi, r in enumerate(DILS) if r > 1] if with_values else []

    def body(x_ref, g_ref, c_ref, sa_ref, sb_ref, *rest):
        o_refs = rest[:NG]
        v_refs = rest[NG:NG + len(vgroups)]
        scr, vscr = rest[NG + len(vgroups):]
        c, t, swap = c_ref[...], sa_ref[...] + sb_ref[...], _swap_matrix()
        for gi, r in enumerate(DILS):
            heads = range(gi * hpg, (gi + 1) * hpg)
            xs = [x_ref[:, h * DH:(h + 1) * DH].astype(f32) for h in heads]
            rs = [lax.rsqrt(_head_mean(xv * xv) + EPS) for xv in xs]
            ys = [_rope(xv * rv * g_ref[...], c, t, swap) for xv, rv in zip(xs, rs)]
            for hh, y in enumerate(ys):
                if r == 1:
                    o_refs[gi][:, hh * DH:(hh + 1) * DH] = y.astype(bf16)
                else:
                    scr[hh] = y
            if r > 1:
                for hh in range(hpg):
                    for j in range(r):
                        o_refs[gi][:, j * ow + hh * DH:j * ow + (hh + 1) * DH] = scr[hh, pl.ds(j, tm // r, stride=r), :].astype(bf16)
        for vi, gi in enumerate(vgroups):
            _to_view(x_ref[:, width + gi * ow:width + (gi + 1) * ow].astype(f32), v_refs[vi], vscr, DILS[gi], ow, tm)

    win = raw.shape[1] if with_values else width
    row = pl.BlockSpec((tm, win), lambda i: (i, 0))
    tab = pl.BlockSpec((tm, DH), lambda i: (i, 0))
    view = lambda r: pl.BlockSpec((tm // r, r * ow), lambda i: (i, 0))
    vshape = lambda r: jax.ShapeDtypeStruct((S // r, r * ow), bf16)
    outs = pl.pallas_call(
        body, name=name, grid=(S // tm,), in_specs=[row, _acc_spec(DH), tab, tab, tab],
        out_specs=[view(r) for r in DILS] + [view(DILS[gi]) for gi in vgroups],
        out_shape=[vshape(r) for r in DILS] + [vshape(DILS[gi]) for gi in vgroups],
        scratch_shapes=[pltpu.VMEM((hpg, tm, DH), f32), pltpu.VMEM((ow // DH, tm, DH), f32)],
        compiler_params=_cp((PAR,)))(raw, g, *tabs)
    return outs[:NG], outs[NG:]


def _qk_bwd(dparts, raw, g, tabs, width, extra, name):
    S = raw.shape[0]
    nh = width // DH
    ow = width // NG
    hpg = ow // DH
    tm = _tile(S, 256, 16 * max(DILS))
    wout = width + len(extra) * ow

    def body(*refs):
        d_refs = refs[:NG]
        x_ref, g_ref, c_ref, sa_ref, sb_ref = refs[NG:NG + 5]
        e_refs = refs[NG + 5:NG + 5 + len(extra)]
        o_ref, dg_ref, scr, vscr = refs[NG + 5 + len(extra):]
        i = pl.program_id(0)

        @pl.when(i == 0)
        def _():
            dg_ref[...] = jnp.zeros_like(dg_ref)

        c, t, swap = c_ref[...], sa_ref[...] + sb_ref[...], _swap_matrix()
        gv = g_ref[...]
        dg = jnp.zeros((1, DH), f32)
        for gi, r in enumerate(DILS):
            heads = list(range(gi * hpg, (gi + 1) * hpg))
            if r == 1:
                douts = [d_refs[gi][:, hh * DH:(hh + 1) * DH].astype(f32) for hh in range(hpg)]
            else:
                for hh in range(hpg):
                    for j in range(r):
                        scr[hh, pl.ds(j, tm // r, stride=r), :] = d_refs[gi][:, j * ow + hh * DH:j * ow + (hh + 1) * DH].astype(f32)
                douts = [scr[hh] for hh in range(hpg)]
            xs = [x_ref[:, h * DH:(h + 1) * DH].astype(f32) for h in heads]
            rs = [lax.rsqrt(_head_mean(xv * xv) + EPS) for xv in xs]
            xhs = [xv * rv for xv, rv in zip(xs, rs)]
            dns = [_rope_t(d, c, t, swap) for d in douts]
            for dn, xh in zip(dns, xhs):
                dg = dg + jnp.sum(dn * xh, axis=0, keepdims=True)
            dxns = [dn * gv for dn in dns]
            dxs = [rv * (dxn - xh * _head_mean(dxn * xh)) for rv, dxn, xh in zip(rs, dxns, xhs)]
            for h, dx in zip(heads, dxs):
                o_ref[:, h * DH:(h + 1) * DH] = dx.astype(o_ref.dtype)
        for gi, e_ref in enumerate(e_refs):
            o_ref[:, width + gi * ow:width + (gi + 1) * ow] = _from_view(e_ref, vscr, DILS[gi], ow, tm).astype(o_ref.dtype)
        dg_ref[...] += dg

    views = [pl.BlockSpec((tm // r, r * ow), lambda i: (i, 0)) for r in DILS]
    tab = pl.BlockSpec((tm, DH), lambda i: (i, 0))
    return pl.pallas_call(
        body, name=name, grid=(S // tm,),
        in_specs=views + [pl.BlockSpec((tm, width), lambda i: (i, 0)), _acc_spec(DH), tab, tab, tab] + (views if extra else []),
        out_specs=[pl.BlockSpec((tm, wout), lambda i: (i, 0)), _acc_spec(DH)],
        out_shape=[jax.ShapeDtypeStruct((S, wout), bf16), jax.ShapeDtypeStruct((1, DH), f32)],
        scratch_shapes=[pltpu.VMEM((hpg, tm, DH), f32), pltpu.VMEM((ow // DH, tm, DH), f32)],
        compiler_params=_cp((ARB,)))(*dparts, raw, g, *tabs, *extra)


def _dot_nt(a, b):
    return lax.dot_general(a, b, (((1,), (1,)), ((), ())), preferred_element_type=f32)


def _dot_tn(a, b):
    return lax.dot_general(a, b, (((0,), (0,)), ((), ())), preferred_element_type=f32)


def _band_masks():
    qi = lax.broadcasted_iota(jnp.int32, (BLK, BLK), 0)
    ki = lax.broadcasted_iota(jnp.int32, (BLK, BLK), 1)
    return ki <= qi, ki >= qi


def _attn_fwd(qv, kview, vview, vbase, r, name):
    sr = qv.shape[0]
    ow = qv.shape[1] // r
    hpg = ow // DH
    nb = sr // BLK
    scale = 1.0 / math.sqrt(DH)

    def body(q_ref, kc_ref, kp_ref, vc_ref, vp_ref, o_ref, l_ref):
        n = pl.program_id(1)
        m_cur, m_prev = _band_masks()
        m_prev = m_prev & (n > 0)
        hs = [slice(h * DH, (h + 1) * DH) for h in range(hpg)]
        s_c = [jnp.where(m_cur, _dot_nt(q_ref[:, s], kc_ref[:, s]) * scale, NEG) for s in hs]
        s_p = [jnp.where(m_prev, _dot_nt(q_ref[:, s], kp_ref[:, s]) * scale, NEG) for s in hs]
        mx = [jnp.maximum(jnp.max(a, axis=-1, keepdims=True), jnp.max(b, axis=-1, keepdims=True)) for a, b in zip(s_c, s_p)]
        p_c = [jnp.exp(a - m) for a, m in zip(s_c, mx)]
        p_p = [jnp.exp(a - m) for a, m in zip(s_p, mx)]
        den = [jnp.sum(a, axis=-1, keepdims=True) + jnp.sum(b, axis=-1, keepdims=True) for a, b in zip(p_c, p_p)]
        for h, s in enumerate(hs):
            o = jnp.dot(p_c[h].astype(bf16), vc_ref[:, s], preferred_element_type=f32)
            o = o + jnp.dot(p_p[h].astype(bf16), vp_ref[:, s], preferred_element_type=f32)
            o_ref[:, s] = (o / den[h]).astype(o_ref.dtype)
            l_ref[:, s] = jnp.broadcast_to(mx[h] + jnp.log(den[h]), (BLK, DH))

    cur = lambda j, n: (n, j)
    prev = lambda j, n: (jnp.maximum(n - 1, 0), j)
    vcur = lambda j, n: (n, vbase + j)
    vprev = lambda j, n: (jnp.maximum(n - 1, 0), vbase + j)
    blk = lambda f: pl.BlockSpec((BLK, ow), f)
    return pl.pallas_call(
        body, name=name, grid=(r, nb), in_specs=[blk(cur), blk(cur), blk(prev), blk(vcur), blk(vprev)],
        out_specs=[blk(cur), blk(cur)],
        out_shape=[jax.ShapeDtypeStruct((sr, r * ow), bf16), jax.ShapeDtypeStruct((sr, r * ow), f32)],
        compiler_params=_cp((PAR, PAR)))(qv, kview, kview, vview, vview)


def _attn_bwd_q(qv, kview, vview, vbase, do_g, lse, corr, r, name, comms=()):
    sr = qv.shape[0]
    ow = qv.shape[1] // r
    hpg = ow // DH
    nb = sr // BLK
    scale = 1.0 / math.sqrt(DH)

    def body(q_ref, kc_ref, kp_ref, vc_ref, vp_ref, do_ref, l_ref, c_ref, dq_ref):
        n = pl.program_id(1)
        m_cur, m_prev = _band_masks()
        m_prev = m_prev & (n > 0)
        hs = [slice(h * DH, (h + 1) * DH) for h in range(hpg)]
        ls = [slice(h * DH, h * DH + BLK) for h in range(hpg)]
        sides = ((kc_ref, vc_ref, m_cur), (kp_ref, vp_ref, m_prev))
        sc = [[jnp.where(msk, _dot_nt(q_ref[:, s], k_ref[:, s]) * scale, NEG) for s in hs] for k_ref, _, msk in sides]
        dp = [[_dot_nt(do_ref[:, s], v_ref[:, s]) for s in hs] for _, v_ref, _ in sides]
        ds = [[(jnp.exp(sc[i][h] - l_ref[:, ls[h]]) * (dp[i][h] + c_ref[:, ls[h]])).astype(bf16) for h in range(hpg)]
              for i in range(2)]
        for h, s in enumerate(hs):
            dq = jnp.dot(ds[0][h], kc_ref[:, s], preferred_element_type=f32)
            dq = dq + jnp.dot(ds[1][h], kp_ref[:, s], preferred_element_type=f32)
            dq_ref[:, s] = (dq * scale).astype(dq_ref.dtype)

    cur = lambda j, n: (n, j)
    prev = lambda j, n: (jnp.maximum(n - 1, 0), j)
    vcur = lambda j, n: (n, vbase + j)
    vprev = lambda j, n: (jnp.maximum(n - 1, 0), vbase + j)
    blk = lambda f: pl.BlockSpec((BLK, ow), f)
    return _pcall(
        body, name=name, grid=(r, nb),
        in_specs=[blk(cur), blk(cur), blk(prev), blk(vcur), blk(vprev), blk(cur), blk(cur), blk(cur)],
        out_specs=blk(cur), out_shape=jax.ShapeDtypeStruct((sr, r * ow), bf16), sem=(PAR, PAR), comms=comms,
        args=[qv, kview, kview, vview, vview, do_g, lse, corr])


def _attn_bwd_kv(qv, kview, vview, vbase, do_g, lse, corr, r, name):
    sr = qv.shape[0]
    ow = qv.shape[1] // r
    hpg = ow // DH
    nb = sr // BLK
    scale = 1.0 / math.sqrt(DH)

    def body(k_ref, v_ref, qc_ref, qn_ref, doc_ref, don_ref, lc_ref, ln_ref, cc_ref, cn_ref, dk_ref, dv_ref):
        n = pl.program_id(1)
        m_cur, m_prev = _band_masks()
        m_next = m_prev & (n < nb - 1)
        hs = [slice(h * DH, (h + 1) * DH) for h in range(hpg)]
        ls = [slice(h * DH, h * DH + BLK) for h in range(hpg)]
        sides = ((qc_ref, doc_ref, lc_ref, cc_ref, m_cur), (qn_ref, don_ref, ln_ref, cn_ref, m_next))
        sc = [[jnp.where(msk, _dot_nt(q_ref[:, s], k_ref[:, s]) * scale, NEG) for s in hs] for q_ref, _, _, _, msk in sides]
        dp = [[_dot_nt(do_ref[:, s], v_ref[:, s]) for s in hs] for _, do_ref, _, _, _ in sides]
        p = [[jnp.exp(sc[i][h] - sides[i][2][:, ls[h]]) for h in range(hpg)] for i in range(2)]
        ds = [[(p[i][h] * (dp[i][h] + sides[i][3][:, ls[h]])).astype(bf16) for h in range(hpg)] for i in range(2)]
        for h, s in enumerate(hs):
            dv = _dot_tn(p[0][h].astype(bf16), doc_ref[:, s]) + _dot_tn(p[1][h].astype(bf16), don_ref[:, s])
            dk = _dot_tn(ds[0][h], qc_ref[:, s]) + _dot_tn(ds[1][h], qn_ref[:, s])
            dk_ref[:, s] = (dk * scale).astype(dk_ref.dtype)
            dv_ref[:, s] = dv.astype(dv_ref.dtype)

    cur = lambda j, n: (n, j)
    nxt = lambda j, n: (jnp.minimum(n + 1, nb - 1), j)
    vcur = lambda j, n: (n, vbase + j)
    blk = lambda f: pl.BlockSpec((BLK, ow), f)
    shp = jax.ShapeDtypeStruct((sr, r * ow), bf16)
    return pl.pallas_call(
        body, name=name, grid=(r, nb),
        in_specs=[blk(cur), blk(vcur), blk(cur), blk(nxt), blk(cur), blk(nxt), blk(cur), blk(nxt), blk(cur), blk(nxt)],
        out_specs=[blk(cur), blk(cur)], out_shape=[shp, shp],
        compiler_params=_cp((PAR, PAR)))(kview, vview, qv, qv, do_g, do_g, lse, lse, corr, corr)


def _mix_weights(l_refs):
    ls = [l[...] for l in l_refs]
    mx = functools.reduce(jnp.maximum, ls)
    es = [jnp.exp(l - mx) for l in ls]
    den = functools.reduce(lambda a, b: a + b, es)
    return [e / den for e in es]


def _from_view(ref, scr, r, ow, tm):
    if r == 1:
        return ref[...].astype(f32)
    for c in range(ow // DH):
        for j in range(r):
            scr[c, pl.ds(j, tm // r, stride=r), :] = ref[:, j * ow + c * DH:j * ow + (c + 1) * DH].astype(f32)
    return jnp.concatenate([scr[c] for c in range(ow // DH)], axis=1)


def _to_view(val, ref, scr, r, ow, tm):
    if r == 1:
        ref[...] = val.astype(ref.dtype)
        return
    for c in range(ow // DH):
        scr[c] = val[:, c * DH:(c + 1) * DH]
        for j in range(r):
            ref[:, j * ow + c * DH:j * ow + (c + 1) * DH] = scr[c, pl.ds(j, tm // r, stride=r), :].astype(ref.dtype)


def _view_specs(tm, ow):
    return [pl.BlockSpec((tm // r, r * ow), lambda i: (i, 0)) for r in DILS]


def _combine_fwd(os_, lses, name):
    ow = os_[0].shape[1] // DILS[0]
    S = os_[0].shape[0] * DILS[0]
    tm = _tile(S, 256, 16 * max(DILS))

    def body(*refs):
        o_refs, l_refs, out_ref = refs[:NG], refs[NG:2 * NG], refs[2 * NG]
        scr = refs[2 * NG + 1:]
        ov = [_from_view(o_refs[gi], scr[2 * gi], DILS[gi], ow, tm) for gi in range(NG)]
        lv = [_from_view(l_refs[gi], scr[2 * gi + 1], DILS[gi], ow, tm) for gi in range(NG)]
        al = _mix_weights(lv)
        acc = al[0] * ov[0]
        for gi in range(1, NG):
            acc = acc + al[gi] * ov[gi]
        out_ref[...] = acc.astype(out_ref.dtype)

    views = _view_specs(tm, ow)
    return pl.pallas_call(body, name=name, grid=(S // tm,), in_specs=views + views,
                          out_specs=pl.BlockSpec((tm, ow), lambda i: (i, 0)), out_shape=jax.ShapeDtypeStruct((S, ow), bf16),
                          scratch_shapes=[pltpu.VMEM((ow // DH, tm, DH), f32)] * (2 * NG),
                          compiler_params=_cp((PAR,), VMEM_BIG))(*os_, *lses)


def _combine_bwd(do, os_, lses, name, comms=()):
    S, ow = do.shape
    hpg = ow // DH
    tm = _tile(S, 256, 16 * max(DILS))

    def body(*refs):
        do_ref = refs[0]
        o_refs, l_refs = refs[1:1 + NG], refs[1 + NG:1 + 2 * NG]
        dog_refs, c_refs = refs[1 + 2 * NG:1 + 3 * NG], refs[1 + 3 * NG:1 + 4 * NG]
        scr = refs[1 + 4 * NG:]
        ov = [_from_view(o_refs[gi], scr[2 * gi], DILS[gi], ow, tm) for gi in range(NG)]
        lv = [_from_view(l_refs[gi], scr[2 * gi + 1], DILS[gi], ow, tm) for gi in range(NG)]
        al = _mix_weights(lv)
        dov = do_ref[...]
        o = al[0] * ov[0]
        for gi in range(1, NG):
            o = o + al[gi] * ov[gi]
        prod = dov * o
        t = jnp.concatenate(
            [jnp.broadcast_to(jnp.sum(prod[:, h * DH:(h + 1) * DH], axis=-1, keepdims=True), (tm, DH)) for h in range(hpg)],
            axis=1)
        for gi in range(NG):
            _to_view(al[gi] * dov, dog_refs[gi], scr[2 * NG], DILS[gi], ow, tm)
            _to_view(-(al[gi] * t), c_refs[gi], scr[2 * NG], DILS[gi], ow, tm)

    views = _view_specs(tm, ow)
    vshape = lambda dt: [jax.ShapeDtypeStruct((S // r, r * ow), dt) for r in DILS]
    return _pcall(
        body, name=name, grid=(S // tm,), in_specs=[pl.BlockSpec((tm, ow), lambda i: (i, 0))] + views + views,
        out_specs=views + views, out_shape=vshape(bf16) + vshape(f32),
        args=[do, *os_, *lses], scratch_shapes=[pltpu.VMEM((ow // DH, tm, DH), f32)] * (2 * NG + 1), sem=(PAR,), vmem=VMEM_BIG,
        comms=comms)


def _pad_rows(w, rows):
    return jnp.concatenate([w, jnp.zeros((rows - w.shape[0], w.shape[1]), w.dtype)], axis=0)


def kernel(x, c, positions, mod_w, mod_b, norm_mix_g, norm_ffn_g, conv_pw1_w, conv_pw1_b, conv_dw_w, conv_dw_b, conv_ln_g, conv_ln_b, conv_pw2_w, conv_pw2_b, kv_mod_w, kv_mod_b, kv_norm_g, w_kv, k_norm_g, w_q, q_norm_g, w_o, ffn_up_w, ffn_dw_w, ffn_dw_b, ffn_down_w, loss_target, m_mod_w, m_mod_b, m_norm_mix_g, m_norm_ffn_g, m_conv_pw1_w, m_conv_pw1_b, m_conv_dw_w, m_conv_dw_b, m_conv_ln_g, m_conv_ln_b, m_conv_pw2_w, m_conv_pw2_b, m_kv_mod_w, m_kv_mod_b, m_kv_norm_g, m_w_kv, m_k_norm_g, m_w_q, m_q_norm_g, m_w_o, m_ffn_up_w, m_ffn_dw_w, m_ffn_dw_b, m_ffn_down_w, v_mod_w, v_mod_b, v_norm_mix_g, v_norm_ffn_g, v_conv_pw1_w, v_conv_pw1_b, v_conv_dw_w, v_conv_dw_b, v_conv_ln_g, v_conv_ln_b, v_conv_pw2_w, v_conv_pw2_b, v_kv_mod_w, v_kv_mod_b, v_kv_norm_g, v_w_kv, v_k_norm_g, v_w_q, v_q_norm_g, v_w_o, v_ffn_up_w, v_ffn_dw_w, v_ffn_dw_b, v_ffn_down_w):
    S, Dm = x.shape[1], x.shape[2]
    F = ffn_dw_b.shape[1]
    QW = NG * HPG * DH
    OW = HPG * DH
    mx, my, mc = _me()
    me = 4 * mx + 2 * my + mc
    core = jnp.reshape(mc, (1,)).astype(jnp.int32)
    chip = jnp.reshape(2 * mx + my, (1,)).astype(jnp.int32)
    x0 = x.reshape(S, Dm)
    target = loss_target.reshape(S, Dm)

    c_all = _ag_small(c, "ag_c").reshape(NDEV, Dm)
    n_mod = mod_w.shape[2]
    n_kvm = kv_mod_w.shape[1]
    b0 = lax.dynamic_slice(mod_b, (0, me * n_mod), (1, n_mod))
    b1 = lax.dynamic_slice(mod_b, (1, me * n_mod), (1, n_mod))
    bk = lax.dynamic_slice(kv_mod_b.reshape(1, -1), (0, me * n_kvm), (1, n_kvm))
    m_part = jnp.concatenate([_modproj(c_all, mod_w[0], b0, "modproj0"), _modproj(c_all, mod_w[1], b1, "modproj1"),
                              _modproj(c_all, kv_mod_w, bk, "modproj_kv")], axis=1)
    m_all = _ag_small(m_part, "ag_mod")
    m_mine = lax.dynamic_index_in_dim(m_all, me, axis=1, keepdims=False)
    mod0 = m_mine[:, :n_mod].reshape(6, Dm)
    mod1 = m_mine[:, n_mod:2 * n_mod].reshape(6, Dm)
    modkv = m_mine[:, 2 * n_mod:].reshape(2, Dm)
    row = lambda a, i: a[i:i + 1]

    as3 = lambda w: w if w.ndim == 3 else w[None]
    sh16 = lambda w: as3(w).astype(bf16)
    ag_pw1 = _comm_allgather(sh16(conv_pw1_w), 2)
    ag_pw2 = _comm_allgather(sh16(conv_pw2_w), 1)
    ag_up = [_comm_allgather(sh16(ffn_up_w[l]), 2) for l in range(2)]
    ag_down = [_comm_allgather(sh16(ffn_down_w[l]), 1) for l in range(2)]
    ag_kv = _comm_allgather(sh16(w_kv), 2)
    ag_q = _comm_allgather(sh16(w_q), 2)
    ag_o = _comm_allgather(sh16(w_o), 2)

    sp_flat = jnp.concatenate([conv_pw1_b.reshape(-1), conv_dw_b.reshape(-1), conv_ln_g.reshape(-1), conv_ln_b.reshape(-1),
                               conv_pw2_b.reshape(-1), conv_dw_w.reshape(-1), ffn_dw_w.reshape(-1)])
    sp_rows = -(-sp_flat.shape[0] // 1024) * 8
    sp_flat = jnp.concatenate([sp_flat, jnp.zeros((sp_rows * 128 - sp_flat.shape[0],), f32)]).reshape(sp_rows, 128)
    n1, nd = conv_pw1_b.shape[1], conv_dw_b.shape[1]
    nfw = ffn_dw_w.shape[2]
    sp = _ag_small(sp_flat, "ag_small_params").reshape(NDEV, -1)
    off = 0
    pw1_b = sp[:, off:off + n1].reshape(1, -1); off += n1
    dw_b = sp[:, off:off + nd].reshape(1, -1); off += nd
    ln_g = sp[:, off:off + nd].reshape(1, -1); off += nd
    ln_b = sp[:, off:off + nd].reshape(1, -1); off += nd
    pw2_b = sp[:, off:off + nd].reshape(1, -1); off += nd
    dw_w = jnp.transpose(sp[:, off:off + CONV_K * nd].reshape(NDEV, CONV_K, nd), (1, 0, 2)).reshape(CONV_K, -1); off += CONV_K * nd
    fdw_w = jnp.transpose(sp[:, off:off + 2 * FFN_K * nfw].reshape(NDEV, 2, FFN_K, nfw), (1, 2, 0, 3)).reshape(2, FFN_K, -1)
    dw_w32 = _pad_rows(dw_w, 32)

    tabs = _rope_tables(positions.reshape(S, 1), "rope_tables")

    def with_comms(res, comms):
        return res if comms else (res, [])

    def rs_d2d(dwb):
        return [_comm_rs_sibling(dwb)]

    def rs_add(dwb, couts, tag):
        return _chip_partial(dwb, couts[0][0], core, f"rs_add_{tag}")

    def rs_ici(part):
        return [_comm_rs_chips(part)]

    def ffn_forward(xin, l, modv, w_up, w_down, up_comms, down_comms):
        h2 = _mod_fwd(xin, row(norm_ffn_g, l), row(modv, 3), row(modv, 4), f"ffn{l}_mod")
        fw8 = _pad_rows(fdw_w[l], 8)
        (u2, act), c_up = with_comms(_up_gate(h2, w_up, fw8, row(ffn_dw_b, l), f"ffn{l}_up", comms=up_comms), up_comms)
        if w_down is None:
            w_down, c_up = c_up[0][0], c_up[1:]
        (xout, f), c_down = with_comms(
            _mm_nn(act, w_down, 0, name=f"ffn{l}_down", res=xin, gate=row(modv, 5), tk=F, tn=512, comms=down_comms), down_comms)
        return xout, (h2, u2, act, f, fw8, w_up, w_down), c_up, c_down

    def ffn_backward(dx, df, dgate, xin, l, modv, saved, dact_comms, gate_next):
        h2, u2, act, f, fw8, w_up, w_down = saved
        d_down = _mm_tn(act, df, name=f"ffn{l}_ddown", col_sharded=False)
        (du2, d_fw, d_fb), c1 = _dact_gate_bwd(df, w_down, u2, fw8, row(ffn_dw_b, l), f"ffn{l}_gatebwd",
                                               comms=rs_d2d(d_down) + list(dact_comms))
        part_down, c_dact = rs_add(d_down, c1, f"down{l}"), c1[1:]
        dh2, c2 = _mm_nt(du2, w_up, 0, name=f"ffn{l}_dh", out_dtype=f32, tko=1024, tn=F // 2, comms=rs_ici(part_down))
        d_up = _mm_tn(h2, du2, name=f"ffn{l}_dup", col_sharded=True)
        (dxin, dsh, dsc, dg, *below), c3 = _mod_bwd(dh2, xin, dx, row(norm_ffn_g, l), row(modv, 4), f"ffn{l}_mod_bwd",
                                                    comms=rs_d2d(d_up), gate_next=gate_next)
        part_up = rs_add(d_up, c3, f"up{l}")
        grads = dict(d_fw=d_fw[:FFN_K], d_fb=d_fb, dsh=dsh, dsc=dsc, dgate=dgate, dg=dg,
                     down=(part_down, c2[0][0]), part_up=part_up)
        return dxin, grads, c_dact, below

    h0, c = _mod_fwd(x0, row(norm_mix_g, 0), row(mod0, 0), row(mod0, 1), "l0_mod", comms=[ag_pw1])
    W_pw1 = c[0][0]
    u0, c = _mm_nn(h0, W_pw1, 0, name="l0_pw1", bias=pw1_b, comms=[ag_pw2, ag_q])
    W_pw2, W_q = c[0][0], c[1][0]
    (s0, cv0), c = _conv_fwd(u0, dw_w32, dw_b, ln_g, ln_b, "l0_conv", comms=[ag_up[0], ag_o])
    W_up0, W_o = c[0][0], c[1][0]
    x1, f0 = _mm_nn(s0, W_pw2, 0, name="l0_pw2", bias=pw2_b, res=x0, gate=row(mod0, 2))
    x2, ffn0_saved, c_up, c_down = ffn_forward(x1, 0, mod0, W_up0, None, [ag_down[0], ag_up[1]], [ag_kv])
    W_up1, W_kv = c_up[0][0], c_down[0][0]

    hkv = _mod_fwd(x2, kv_norm_g.reshape(1, -1), row(modkv, 0), row(modkv, 1), "kv_mod")
    kvraw, c = _mm_nn(hkv, W_kv, 0, name="kv_proj", comms=[ag_down[1]])
    W_down1 = c[0][0]
    kg = k_norm_g.reshape(1, -1)
    k_gv, v_dil = _qk_fwd(kvraw, kg, tabs, QW, True, "k_norm_rope")
    dilated = [gi for gi, r in enumerate(DILS) if r > 1]
    v_of = {gi: (kvraw, NG + gi) for gi, r in enumerate(DILS) if r == 1}
    v_of.update({gi: (v_dil[i], 0) for i, gi in enumerate(dilated)})
    h1 = _mod_fwd(x2, row(norm_mix_g, 1), row(mod1, 0), row(mod1, 1), "l1_mod")
    qraw = _mm_nn(h1, W_q, 0, name="q_proj")
    qg = q_norm_g.reshape(1, -1)
    q_gv, _ = _qk_fwd(qraw, qg, tabs, QW, False, "q_norm_rope")
    o_gs, lses = [], []
    for gi, r in enumerate(DILS):
        o_g, lse_g = _attn_fwd(q_gv[gi], k_gv[gi], *v_of[gi], r, f"attn_fwd{gi}")
        o_gs.append(o_g)
        lses.append(lse_g)
    o_mix = _combine_fwd(o_gs, lses, "attn_mix")
    x3, f1 = _mm_nn(o_mix, W_o, 0, name="o_proj", res=x2, gate=row(mod1, 2))
    x4, ffn1_saved, _, _ = ffn_forward(x3, 1, mod1, W_up1, W_down1, (), ())

    dx4, loss_blk, df_f1, dgate_f1, _ = _loss_grad(x4, target, ffn1_saved[3], row(mod1, 5), "loss")
    loss = lax.psum(loss_blk[0, 0], ("x", "y", "c"))

    red = {}
    dx3, gf1, _, (dy1, dgate_m1, _) = ffn_backward(dx4, df_f1, dgate_f1, x3, 1, mod1, ffn1_saved, (), (f1, row(mod1, 2)))
    do = _mm_nt(dy1, W_o, 0, name="o_proj_dx", out_dtype=f32, tko=1024, tn=Dm)
    d_wo = _mm_tn(o_mix, dy1, name="o_proj_dw", col_sharded=True)
    outs, c = _combine_bwd(do, o_gs, lses, "attn_mix_bwd", comms=rs_d2d(d_wo))
    part_wo = rs_add(d_wo, c, "wo")
    do_gs, corrs = outs[:NG], outs[NG:]
    dq_gs, dk_gs, dv_gs = [], [], []
    for gi, r in enumerate(DILS):
        cm = rs_ici(part_wo) if gi == 0 else ()
        dq_g, c = with_comms(_attn_bwd_q(q_gv[gi], k_gv[gi], *v_of[gi], do_gs[gi], lses[gi], corrs[gi], r, f"attn_bwd_q{gi}",
                                         comms=cm), cm)
        if gi == 0:
            red["w_o"] = (part_wo, c[0][0])
        dq_gs.append(dq_g)
        dk_g, dv_g = _attn_bwd_kv(q_gv[gi], k_gv[gi], *v_of[gi], do_gs[gi], lses[gi], corrs[gi], r, f"attn_bwd_kv{gi}")
        dk_gs.append(dk_g)
        dv_gs.append(dv_g)
    dqraw, d_qg = _qk_bwd(dq_gs, qraw, qg, tabs, QW, (), "q_norm_rope_bwd")
    dkvraw, d_kg = _qk_bwd(dk_gs, kvraw, kg, tabs, QW, tuple(dv_gs), "k_norm_rope_bwd")
    dh1 = _mm_nt(dqraw, W_q, 0, name="q_proj_dx", out_dtype=f32, tko=1024, tn=QW)
    d_wq = _mm_tn(h1, dqraw, name="q_proj_dw", col_sharded=True)
    dhkv, c = _mm_nt(dkvraw, W_kv, 0, name="kv_proj_dx", out_dtype=f32, tko=512, tn=2 * QW, comms=rs_d2d(d_wq))
    part_wq = rs_add(d_wq, c, "wq")
    d_wkv, c = _mm_tn(hkv, dkvraw, name="kv_proj_dw", col_sharded=True, comms=rs_ici(gf1["part_up"]))
    red["ffn_up_w1"] = (gf1["part_up"], c[0][0])
    (dx2a, dsh_m1, dsc_m1, dg_mix1), c = _mod_bwd(dh1, x2, dx3, row(norm_mix_g, 1), row(mod1, 1), "l1_mod_bwd",
                                                  comms=rs_ici(part_wq))
    red["w_q"] = (part_wq, c[0][0])
    (dx2, dsh_kv, dsc_kv, dg_kvn, df_f0, dgate_f0, _), c = _mod_bwd(
        dhkv, x2, dx2a, kv_norm_g.reshape(1, -1), row(modkv, 1), "kv_mod_bwd", comms=rs_d2d(d_wkv),
        gate_next=(ffn0_saved[3], row(mod0, 5)))
    part_wkv = rs_add(d_wkv, c, "wkv")

    dx1, gf0, c, (dy0, dgate_m0, d_pw2b) = ffn_backward(dx2, df_f0, dgate_f0, x1, 0, mod0, ffn0_saved, rs_ici(part_wkv),
                                                        (f0, row(mod0, 2)))
    red["w_kv"] = (part_wkv, c[0][0])
    ds0 = _mm_nt(dy0, W_pw2, 0, name="l0_pw2_dx", out_dtype=bf16, tko=1024, tn=Dm)
    d_pw2 = _mm_tn(s0, dy0, name="l0_pw2_dw", col_sharded=False)
    (dcv, d_lng, d_lnb, d_dwb, d_dww), c = _conv_bwd1(u0, cv0, ds0, dw_w32, dw_b, ln_g, ln_b, "l0_conv_bwd1",
                                                      comms=rs_ici(gf0["part_up"]))
    red["ffn_up_w0"] = (gf0["part_up"], c[0][0])
    (du0, d_pw1b), c = _conv_bwd2(dcv, u0, dw_w32, "l0_conv_bwd2", comms=rs_d2d(d_pw2))
    part_pw2 = rs_add(d_pw2, c, "pw2")
    d_pw1 = _mm_tn(h0, du0, name="l0_pw1_dw", col_sharded=True)
    dh0, c = _mm_nt(du0, W_pw1, 0, name="l0_pw1_dx", out_dtype=f32, tko=1024, tn=2 * Dm, comms=rs_ici(part_pw2) + rs_d2d(d_pw1))
    red["conv_pw2_w"] = (part_pw2, c[0][0])
    part_pw1 = rs_add(d_pw1, c[1:], "pw1")
    (grad_x, dsh_m0, dsc_m0, dg_mix0), c = _mod_bwd(dh0, x0, dx1, row(norm_mix_g, 0), row(mod0, 1), "l0_mod_bwd",
                                                    comms=rs_ici(part_pw1))
    red["conv_pw1_w"] = (part_pw1, c[0][0])
    red["ffn_down_w0"], red["ffn_down_w1"] = gf0["down"], gf1["down"]

    dm0 = [dsh_m0, dsc_m0, dgate_m0, gf0["dsh"], gf0["dsc"], gf0["dgate"]]
    dm1 = [dsh_m1, dsc_m1, dgate_m1, gf1["dsh"], gf1["dsc"], gf1["dgate"]]
    pieces = dm0 + dm1 + [dsh_kv, dsc_kv,
                          dg_mix0, dg_mix1, gf0["dg"], gf1["dg"], dg_kvn, d_kg, d_qg, gf0["d_fb"], gf1["d_fb"],
                          d_pw1b, d_dww[:CONV_K], d_dwb, d_lng, d_lnb, d_pw2b, gf0["d_fw"], gf1["d_fw"]]
    flat = jnp.concatenate([p.reshape(-1) for p in pieces])
    n_flat = flat.shape[0]
    n_rows = -(-n_flat // 1024) * 8
    flat = jnp.concatenate([flat, jnp.zeros((n_rows * 128 - n_flat,), f32)]).reshape(n_rows, 128)
    g_all = _ag_small(flat, "ag_small_grads")
    g_sum = _sum8(g_all, "sum_small_grads").reshape(-1)
    n_dm = 2 * 6 * Dm + 2 * Dm
    dm_all = g_all.reshape(NDEV, -1)[:, :n_dm]

    take_pos = [0]

    def take(shape):
        n = int(np.prod(shape))
        out = g_sum[take_pos[0]:take_pos[0] + n].reshape(shape)
        take_pos[0] += n
        return out

    g_mod_b = take((2, 6 * Dm))
    g_kv_mod_b = take((2 * Dm,))
    g_norm_mix0, g_norm_mix1 = take((Dm,)), take((Dm,))
    g_norm_ffn0, g_norm_ffn1 = take((Dm,)), take((Dm,))
    g_kv_norm = take((Dm,))
    g_k_norm = take((DH,))
    g_q_norm = take((1, DH))
    g_ffn_dw_b = take((2, F))
    shard = lambda full, n, axis: lax.dynamic_slice_in_dim(full, me * n, n, axis)
    g_pw1_b = shard(take((1, 2 * Dm)), n1, 1)
    g_dw_w = shard(take((1, CONV_K, Dm)), nd, 2)
    g_dw_b = shard(take((1, Dm)), nd, 1)
    g_ln_g = shard(take((1, Dm)), nd, 1)
    g_ln_b = shard(take((1, Dm)), nd, 1)
    g_pw2_b = shard(take((1, Dm)), nd, 1)
    g_ffn_dw_w = shard(jnp.stack([take((FFN_K, F)), take((FFN_K, F))]), nfw, 2)
    g_norm_mix = jnp.stack([g_norm_mix0, g_norm_mix1])
    g_norm_ffn = jnp.stack([g_norm_ffn0, g_norm_ffn1])

    small = [("mod_b", mod_b, m_mod_b, v_mod_b, g_mod_b), ("norm_mix_g", norm_mix_g, m_norm_mix_g, v_norm_mix_g, g_norm_mix),
             ("norm_ffn_g", norm_ffn_g, m_norm_ffn_g, v_norm_ffn_g, g_norm_ffn),
             ("conv_pw1_b", conv_pw1_b, m_conv_pw1_b, v_conv_pw1_b, g_pw1_b),
             ("conv_dw_w", conv_dw_w, m_conv_dw_w, v_conv_dw_w, g_dw_w), ("conv_dw_b", conv_dw_b, m_conv_dw_b, v_conv_dw_b, g_dw_b),
             ("conv_ln_g", conv_ln_g, m_conv_ln_g, v_conv_ln_g, g_ln_g), ("conv_ln_b", conv_ln_b, m_conv_ln_b, v_conv_ln_b, g_ln_b),
             ("conv_pw2_b", conv_pw2_b, m_conv_pw2_b, v_conv_pw2_b, g_pw2_b),
             ("kv_mod_b", kv_mod_b, m_kv_mod_b, v_kv_mod_b, g_kv_mod_b), ("kv_norm_g", kv_norm_g, m_kv_norm_g, v_kv_norm_g, g_kv_norm),
             ("k_norm_g", k_norm_g, m_k_norm_g, v_k_norm_g, g_k_norm), ("q_norm_g", q_norm_g, m_q_norm_g, v_q_norm_g, g_q_norm),
             ("ffn_dw_w", ffn_dw_w, m_ffn_dw_w, v_ffn_dw_w, g_ffn_dw_w), ("ffn_dw_b", ffn_dw_b, m_ffn_dw_b, v_ffn_dw_b, g_ffn_dw_b)]
    n_small = sum(int(np.prod(s[1].shape)) for s in small)
    rows_small = -(-n_small // 1024) * 8

    def pack(idx):
        fl = jnp.concatenate([s[idx].reshape(-1) for s in small])
        return jnp.concatenate([fl, jnp.ones((rows_small * 128 - n_small,), f32)]).reshape(rows_small, 128)

    sd, sm, sv = _adamw_plain(pack(1), pack(2), pack(3), pack(4), "adamw_small")
    res = {}
    pos = 0
    for name, w, _, _, g in small:
        n = int(np.prod(w.shape))
        cut = lambda a: a.reshape(-1)[pos:pos + n].reshape(w.shape)
        res[name] = (g.reshape(w.shape), cut(sd), cut(sm), cut(sv))
        pos += n

    c_all_t = jnp.transpose(c_all)

    def mod_update(w2d, m2d, v2d, dm_cols, tag):
        g = _modgrad(c_all_t, dm_cols, f"modgrad_{tag}")
        d, m2, v2 = _adamw_plain(w2d, m2d, v2d, g, f"adamw_{tag}")
        return g, d, m2, v2

    mw = []
    for l in range(2):
        cols = lax.dynamic_slice_in_dim(dm_all[:, l * 6 * Dm:(l + 1) * 6 * Dm], me * n_mod, n_mod, 1)
        mw.append(mod_update(mod_w[l], m_mod_w[l], v_mod_w[l], cols, f"mod_w{l}"))
    res["mod_w"] = tuple(jnp.stack([mw[0][i], mw[1][i]]) for i in range(4))
    cols = lax.dynamic_slice_in_dim(dm_all[:, 12 * Dm:], me * n_kvm, n_kvm, 1)
    res["kv_mod_w"] = mod_update(kv_mod_w, m_kv_mod_w, v_kv_mod_w, cols, "kv_mod_w")

    def mine(part):
        return lax.dynamic_index_in_dim(part, chip[0], 0, keepdims=False)

    def big(key, w, m, v, l, prev, tag, comms=()):
        part, r2 = red[key]
        return _adamw_reduced(as3(w), as3(m), as3(v), mine(part), r2, l, prev, f"adamw_{tag}", comms=comms)

    up1 = big("ffn_up_w1", ffn_up_w, m_ffn_up_w, v_ffn_up_w, 1, None, "up1")
    res["ffn_up_w"] = tuple(big("ffn_up_w0", ffn_up_w, m_ffn_up_w, v_ffn_up_w, 0, up1, "up0"))
    down1 = big("ffn_down_w1", ffn_down_w, m_ffn_down_w, v_ffn_down_w, 1, None, "down1")
    res["ffn_down_w"] = tuple(big("ffn_down_w0", ffn_down_w, m_ffn_down_w, v_ffn_down_w, 0, down1, "down0"))
    for key, w, m, v in (("conv_pw1_w", conv_pw1_w, m_conv_pw1_w, v_conv_pw1_w), ("conv_pw2_w", conv_pw2_w, m_conv_pw2_w, v_conv_pw2_w),
                         ("w_kv", w_kv, m_w_kv, v_w_kv), ("w_q", w_q, m_w_q, v_w_q), ("w_o", w_o, m_w_o, v_w_o)):
        res[key] = tuple(o.reshape(w.shape) for o in big(key, w, m, v, 0, None, key))

    order = ["mod_w", "mod_b", "norm_mix_g", "norm_ffn_g", "conv_pw1_w", "conv_pw1_b", "conv_dw_w", "conv_dw_b", "conv_ln_g",
             "conv_ln_b", "conv_pw2_w", "conv_pw2_b", "kv_mod_w", "kv_mod_b", "kv_norm_g", "w_kv", "k_norm_g", "w_q", "q_norm_g",
             "w_o", "ffn_up_w", "ffn_dw_w", "ffn_dw_b", "ffn_down_w"]
    out = [loss, grad_x.reshape(x.shape)]
    for i in range(4):
        out += [res[n][i] for n in order]
    return tuple(out)
```

```python
import functools
import math

import numpy as np
import jax
import jax.numpy as jnp
from jax import lax
from jax.experimental import pallas as pl
from jax.experimental.pallas import tpu as pltpu

f32 = jnp.float32
bf16 = jnp.bfloat16

D = 2048
SEQ = 8192
FF = 5632
CONV_K = 31
FFN_K = 3
HPG = 8
DH = 128
NG = 3
DILS = (1, 4, 16)
BLK = 128
ROT = 32
THETA = 500000.0
EPS = 1e-6
NEG = -1e30
NDEV = 8
HALO = 32
FHALO = 16
FFN_TM = 1024

LR, B1, B2, AEPS, WD, STEP = 0.001, 0.9, 0.999, 1e-08, 0.01, 10

VMEM_BIG = 56 * 1024 * 1024

ARB = "arbitrary"
PAR = "parallel"
MESH = pl.DeviceIdType.MESH


def _cp(sem, vmem=None):
    return pltpu.CompilerParams(dimension_semantics=sem, vmem_limit_bytes=vmem)


def _tile(n, pref, mult=128):
    if n <= pref:
        return n
    t = (pref // mult) * mult
    while t >= mult:
        if n % t == 0:
            return t
        t -= mult
    return n


def _sigmoid(x):
    return 1.0 / (1.0 + jnp.exp(-x))


def _me():
    return lax.axis_index("x"), lax.axis_index("y"), lax.axis_index("c")


class _Comm:
    def __init__(self, arrays, out_shapes, sems, start, finish, mid=None):
        self.arrays, self.out_shapes, self.sems, self.start, self.finish = arrays, out_shapes, sems, start, finish
        self.mid = mid


def _pcall(body, *, name, grid, in_specs, out_specs, out_shape, args, scratch_shapes=(), sem=None, vmem=None, comms=(),
           aliases=None):
    aliases = aliases or {}
    if not comms:
        return pl.pallas_call(body, name=name, grid=grid, in_specs=in_specs, out_specs=out_specs, out_shape=out_shape,
                              scratch_shapes=list(scratch_shapes), input_output_aliases=aliases,
                              compiler_params=_cp(sem, vmem))(*args)
    single = not isinstance(out_shape, (list, tuple))
    outs_shape = [out_shape] if single else list(out_shape)
    outs_spec = [out_specs] if single else list(out_specs)
    n_in, n_out, n_scr = len(args), len(outs_shape), len(scratch_shapes)
    c_in = [a for cm in comms for a in cm.arrays]
    c_out = [s for cm in comms for s in cm.out_shapes]
    c_scr = [s for cm in comms for s in cm.sems]
    total = int(np.prod(grid))
    late = total - 1 - max(1, total // 8) if total >= 8 else None

    def split(refs, counts):
        out, pos = [], 0
        for n in counts:
            out.append(refs[pos:pos + n])
            pos += n
        return out

    def wrapped(*refs):
        ins, cins, outs, couts, scr, cscr = split(refs, [n_in, len(c_in), n_out, len(c_out), n_scr, len(c_scr)])
        ids = [pl.program_id(a) for a in range(len(grid))]
        first = functools.reduce(jnp.logical_and, [i == 0 for i in ids])
        last = functools.reduce(jnp.logical_and, [i == g - 1 for i, g in zip(ids, grid)])
        per_in = split(cins, [len(cm.arrays) for cm in comms])
        per_out = split(couts, [len(cm.out_shapes) for cm in comms])
        per_sem = split(cscr, [len(cm.sems) for cm in comms])

        @pl.when(first)
        def _():
            for cm, a, b, s in zip(comms, per_in, per_out, per_sem):
                cm.start(a, b, s)

        if late is not None:
            step = functools.reduce(lambda acc, ig: acc * ig[1] + ig[0], zip(ids, grid), 0)

            @pl.when(step == late)
            def _():
                for cm, a, b, s in zip(comms, per_in, per_out, per_sem):
                    if cm.mid is not None:
                        cm.mid(a, b, s)

        body(*ins, *outs, *scr)

        @pl.when(last)
        def _():
            for cm, a, b, s in zip(comms, per_in, per_out, per_sem):
                if late is None and cm.mid is not None:
                    cm.mid(a, b, s)
                cm.finish(a, b, s)

    hbm = pl.BlockSpec(memory_space=pl.ANY)
    res = pl.pallas_call(
        wrapped, name=name, grid=grid, in_specs=list(in_specs) + [hbm] * len(c_in),
        out_specs=outs_spec + [hbm] * len(c_out), out_shape=outs_shape + c_out,
        scratch_shapes=list(scratch_shapes) + c_scr, input_output_aliases=aliases,
        compiler_params=_cp((ARB,) * len(grid), vmem))(*args, *c_in)
    main = res[0] if single else list(res[:n_out])
    return main, split(list(res[n_out:]), [len(cm.out_shapes) for cm in comms])


def _comm_allgather(w, axis):
    n = w.shape[axis]
    out_shape = list(w.shape)
    out_shape[axis] = NDEV * n

    def parts(ins, outs, sems):
        x_ref, out_ref = ins[0], outs[0]
        send_sems, recv_sems, local_sem = sems
        mx, my, mc = _me()
        chips = [(1 - mx, my), (mx, 1 - my), (1 - mx, 1 - my)]

        def blk(px, py, pc):
            start = pl.multiple_of((4 * px + 2 * py + pc) * n, n)
            if axis == 1:
                return out_ref.at[:, pl.ds(start, n), :]
            return out_ref.at[:, :, pl.ds(start, n)]

        def copy(k, block, to, src=None):
            return pltpu.make_async_remote_copy(
                src_ref=blk(*block) if src is None else src, dst_ref=blk(*block),
                send_sem=send_sems.at[k], recv_sem=recv_sems.at[k], device_id=to, device_id_type=MESH)

        me, sibling = (mx, my, mc), (mx, my, 1 - mc)
        mine = pltpu.make_async_copy(x_ref, blk(*me), local_sem)
        first = [copy(0, me, sibling, src=x_ref)] + [copy(1 + j, me, (*chip, mc), src=x_ref) for j, chip in enumerate(chips)]
        passed = [copy(4 + j, (*chip, mc), sibling) for j, chip in enumerate(chips)]
        return me, sibling, chips, mc, copy, mine, first, passed

    def start(ins, outs, sems):
        *_, mine, first, _ = parts(ins, outs, sems)
        mine.start()
        for cp in first:
            cp.start()

    def mid(ins, outs, sems):
        me, sibling, chips, mc, copy, mine, first, passed = parts(ins, outs, sems)
        for j, chip in enumerate(chips):
            copy(1 + j, (*chip, mc), me).wait_recv()
            passed[j].start()

    def finish(ins, outs, sems):
        me, sibling, chips, mc, copy, mine, first, passed = parts(ins, outs, sems)
        copy(0, sibling, me).wait_recv()
        for j, chip in enumerate(chips):
            copy(4 + j, (*chip, 1 - mc), me).wait_recv()
        for cp in first + passed:
            cp.wait_send()
        mine.wait()

    return _Comm([w], [jax.ShapeDtypeStruct(tuple(out_shape), w.dtype)],
                 [pltpu.SemaphoreType.DMA((7,)), pltpu.SemaphoreType.DMA((7,)), pltpu.SemaphoreType.DMA], start, finish, mid)


def _comm_rs_sibling(dwb):
    def copies(ins, outs, sems):
        mx, my, mc = _me()
        return [pltpu.make_async_remote_copy(
            src_ref=ins[0].at[2 * p + (1 - mc)], dst_ref=outs[0].at[p], send_sem=sems[0].at[p], recv_sem=sems[1].at[p],
            device_id=(mx, my, 1 - mc), device_id_type=MESH) for p in range(4)]

    def start(ins, outs, sems):
        for cp in copies(ins, outs, sems):
            cp.start()

    def finish(ins, outs, sems):
        cps = copies(ins, outs, sems)
        for cp in cps:
            cp.wait_recv()
        for cp in cps:
            cp.wait_send()

    return _Comm([dwb], [jax.ShapeDtypeStruct((4,) + dwb.shape[1:], dwb.dtype)],
                 [pltpu.SemaphoreType.DMA((4,)), pltpu.SemaphoreType.DMA((4,))], start, finish)


def _comm_rs_chips(part):
    def copies(ins, outs, sems):
        mx, my, mc = _me()
        chips = [(1 - mx, my), (mx, 1 - my), (1 - mx, 1 - my)]
        return [pltpu.make_async_remote_copy(
            src_ref=ins[0].at[2 * px + py], dst_ref=outs[0].at[k], send_sem=sems[0].at[k], recv_sem=sems[1].at[k],
            device_id=(px, py, mc), device_id_type=MESH) for k, (px, py) in enumerate(chips)]

    def start(ins, outs, sems):
        for cp in copies(ins, outs, sems):
            cp.start()

    def finish(ins, outs, sems):
        cps = copies(ins, outs, sems)
        for cp in cps:
            cp.wait_recv()
        for cp in cps:
            cp.wait_send()

    return _Comm([part], [jax.ShapeDtypeStruct((3,) + part.shape[1:], part.dtype)],
                 [pltpu.SemaphoreType.DMA((3,)), pltpu.SemaphoreType.DMA((3,))], start, finish)


def _ag_small(x, name):
    r, c = x.shape

    def body(x_ref, out_ref, send_sems, recv_sems):
        mx, my, mc = _me()
        mine = 4 * mx + 2 * my + mc
        out_ref[mine] = x_ref[...]
        copies = []
        for k in range(1, NDEV):
            px = 1 - mx if (k >> 2) & 1 else mx
            py = 1 - my if (k >> 1) & 1 else my
            pc = 1 - mc if k & 1 else mc
            cp = pltpu.make_async_remote_copy(
                src_ref=x_ref, dst_ref=out_ref.at[mine], send_sem=send_sems.at[k - 1], recv_sem=recv_sems.at[k - 1],
                device_id=(px, py, pc), device_id_type=MESH)
            cp.start()
            copies.append((cp, 4 * px + 2 * py + pc))
        for k, (cp, peer) in enumerate(copies):
            pltpu.make_async_remote_copy(
                src_ref=x_ref, dst_ref=out_ref.at[peer], send_sem=send_sems.at[k], recv_sem=recv_sems.at[k],
                device_id=(mx, my, mc), device_id_type=MESH).wait_recv()
        for cp, _ in copies:
            cp.wait_send()

    return pl.pallas_call(
        body, name=name,
        out_shape=jax.ShapeDtypeStruct((NDEV, r, c), x.dtype),
        in_specs=[pl.BlockSpec(memory_space=pltpu.VMEM)],
        out_specs=pl.BlockSpec(memory_space=pltpu.VMEM),
        scratch_shapes=[pltpu.SemaphoreType.DMA((NDEV - 1,)), pltpu.SemaphoreType.DMA((NDEV - 1,))],
    )(x)


def _chip_partial(dwb, r1, core, name):
    _, A, B = dwb.shape
    ta = _tile(A, 512, 16)

    def body(c_ref, a_ref, b_ref, o_ref):
        o_ref[...] = (a_ref[...].astype(f32) + b_ref[...].astype(f32)).astype(o_ref.dtype)

    grid_spec = pltpu.PrefetchScalarGridSpec(
        num_scalar_prefetch=1, grid=(4, A // ta),
        in_specs=[pl.BlockSpec((None, ta, B), lambda p, i, c: (2 * p + c[0], i, 0)),
                  pl.BlockSpec((None, ta, B), lambda p, i, c: (p, i, 0))],
        out_specs=pl.BlockSpec((None, ta, B), lambda p, i, c: (p, i, 0)))
    return pl.pallas_call(body, name=name, grid_spec=grid_spec,
                          out_shape=jax.ShapeDtypeStruct((4, A, B), dwb.dtype),
                          compiler_params=_cp((PAR, PAR)))(core, dwb, r1)


def _adam_math(w, g, m, v):
    m2 = B1 * m + (1.0 - B1) * g
    v2 = B2 * v + (1.0 - B2) * (g * g)
    m_hat = m2 / (1.0 - B1 ** STEP)
    v_hat = v2 / (1.0 - B2 ** STEP)
    delta = -LR * (m_hat / (jnp.sqrt(v_hat) + AEPS) + WD * w)
    return delta, m2, v2


def _adamw_reduced(w, m, v, mine, r2, l, prev, name, comms=()):
    L, A, B = w.shape
    ta = _tile(A, 256, 8)
    summed = r2 is None

    def body(w_ref, m_ref, v_ref, p_ref, *rest):
        g_out, d_out, m_out, v_out = rest[-4:]
        g = p_ref[...].astype(f32)
        if not summed:
            r_ref = rest[0]
            g = ((g + r_ref[0].astype(f32)) + r_ref[1].astype(f32)) + r_ref[2].astype(f32)
        d, m2, v2 = _adam_math(w_ref[...], g, m_ref[...], v_ref[...])
        g_out[...] = g
        d_out[...] = d
        m_out[...] = m2
        v_out[...] = v2

    wspec = pl.BlockSpec((None, ta, B), lambda i: (l, i, 0))
    in_specs = [wspec, wspec, wspec, pl.BlockSpec((ta, B), lambda i: (i, 0))]
    args = [w, m, v, mine]
    if not summed:
        in_specs.append(pl.BlockSpec((3, ta, B), lambda i: (0, i, 0)))
        args.append(r2)
    aliases = {}
    if prev is not None:
        aliases = {len(args) + i: i for i in range(4)}
        in_specs += [pl.BlockSpec(memory_space=pl.ANY)] * 4
        args += list(prev)
    shp = jax.ShapeDtypeStruct((L, A, B), f32)
    return _pcall(body, name=name, grid=(A // ta,), in_specs=in_specs, out_specs=[wspec] * 4, out_shape=[shp] * 4,
                  args=args, sem=(PAR,), comms=comms, aliases=aliases)


def _adamw_plain(w, m, v, g, name):
    A, B = w.shape
    ta = _tile(A, 256, 8)

    def body(w_ref, m_ref, v_ref, g_ref, d_out, m_out, v_out):
        d, m2, v2 = _adam_math(w_ref[...], g_ref[...], m_ref[...], v_ref[...])
        d_out[...] = d
        m_out[...] = m2
        v_out[...] = v2

    spec = pl.BlockSpec((ta, B), lambda i: (i, 0))
    shp = jax.ShapeDtypeStruct((A, B), f32)
    return pl.pallas_call(body, name=name, grid=(A // ta,), in_specs=[spec] * 4, out_specs=[spec] * 3,
                          out_shape=[shp, shp, shp], compiler_params=_cp((PAR,)))(w, m, v, g)


def _sum8(g, name):
    _, R, C = g.shape

    def body(g_ref, o_ref):
        acc = g_ref[0]
        for j in range(1, NDEV):
            acc = acc + g_ref[j]
        o_ref[...] = acc

    return pl.pallas_call(body, name=name, out_shape=jax.ShapeDtypeStruct((R, C), f32))(g)


def _modproj(c_all, w, l, bias, name):
    _, K, N = w.shape
    tn = _tile(N, 512)

    def body(c_ref, w_ref, b_ref, o_ref):
        cc = c_ref[...]
        sc = (cc * _sigmoid(cc)).astype(bf16)
        o_ref[...] = jnp.dot(sc, w_ref[...].astype(bf16), preferred_element_type=f32) + b_ref[...]

    return pl.pallas_call(
        body, name=name, grid=(N // tn,),
        in_specs=[pl.BlockSpec((NDEV, K), lambda j: (0, 0)), pl.BlockSpec((None, K, tn), lambda j: (l, 0, j)),
                  pl.BlockSpec((1, tn), lambda j: (0, j))],
        out_specs=pl.BlockSpec((NDEV, tn), lambda j: (0, j)),
        out_shape=jax.ShapeDtypeStruct((NDEV, N), f32), compiler_params=_cp((PAR,)))(c_all, w, bias)


def _modgrad(c_all_t, dm, name):
    K = c_all_t.shape[0]
    N = dm.shape[1]
    tn = _tile(N, 512)

    def body(c_ref, d_ref, o_ref):
        cc = c_ref[...]
        sc = cc * _sigmoid(cc)
        dmv = d_ref[...]
        acc = sc[:, 0:1] * dmv[0:1, :]
        for b in range(1, NDEV):
            acc = acc + sc[:, b:b + 1] * dmv[b:b + 1, :]
        o_ref[...] = acc

    return pl.pallas_call(
        body, name=name, grid=(N // tn,),
        in_specs=[pl.BlockSpec((K, NDEV), lambda j: (0, 0)), pl.BlockSpec((NDEV, tn), lambda j: (0, j))],
        out_specs=pl.BlockSpec((K, tn), lambda j: (0, j)),
        out_shape=jax.ShapeDtypeStruct((K, N), f32), compiler_params=_cp((PAR,)))(c_all_t, dm)


def _mm_nn(a, w, l, *, name, out_dtype=bf16, bias=None, res=None, gate=None, tm=1024, tn=1024, tk=2048, comms=()):
    M, K = a.shape
    N = w.shape[2]
    tm, tn, tk = _tile(M, tm, 8), _tile(N, tn), _tile(K, tk)
    nk = K // tk
    epi = res is not None

    def body(*refs):
        it = iter(refs)
        a_ref, w_ref = next(it), next(it)
        b_ref = next(it) if bias is not None else None
        r_ref = next(it) if epi else None
        g_ref = next(it) if epi else None
        o_ref = next(it)
        f_ref = next(it) if epi else None

        def finish(y):
            if b_ref is not None:
                y = y + b_ref[...]
            if epi:
                f_ref[...] = y.astype(f_ref.dtype)
                o_ref[...] = r_ref[...] + g_ref[...] * y
            else:
                o_ref[...] = y.astype(o_ref.dtype)

        if nk == 1:
            finish(jnp.dot(a_ref[...], w_ref[...], preferred_element_type=f32))
            return
        acc = next(it)
        k = pl.program_id(2)

        @pl.when(k == 0)
        def _():
            acc[...] = jnp.zeros_like(acc)

        acc[...] += jnp.dot(a_ref[...], w_ref[...], preferred_element_type=f32)

        @pl.when(k == nk - 1)
        def _():
            finish(acc[...])

    in_specs = [pl.BlockSpec((tm, tk), lambda i, j, k: (i, k)), pl.BlockSpec((None, tk, tn), lambda i, j, k: (l, k, j))]
    args = [a, w]
    if bias is not None:
        in_specs.append(pl.BlockSpec((1, tn), lambda i, j, k: (0, j)))
        args.append(bias)
    ospec = pl.BlockSpec((tm, tn), lambda i, j, k: (i, j))
    if epi:
        in_specs += [ospec, pl.BlockSpec((1, tn), lambda i, j, k: (0, j))]
        args += [res, gate]
        out_shape = [jax.ShapeDtypeStruct((M, N), f32), jax.ShapeDtypeStruct((M, N), bf16)]
        out_specs = [ospec, ospec]
    else:
        out_shape = jax.ShapeDtypeStruct((M, N), out_dtype)
        out_specs = ospec
    return _pcall(body, name=name, grid=(M // tm, N // tn, nk), in_specs=in_specs, out_specs=out_specs, out_shape=out_shape,
                  args=args, scratch_shapes=[pltpu.VMEM((tm, tn), f32)] if nk > 1 else [], sem=(PAR, PAR, ARB), vmem=VMEM_BIG,
                  comms=comms)


def _mm_nt(a, w, l, *, name, out_dtype, tm=1024, tko=2048, tn=1024, comms=()):
    planes = a.ndim == 3
    M = a.shape[-2]
    K, N = w.shape[1], w.shape[2]
    npl = a.shape[-1]
    tm, tko = _tile(M, tm, 8), _tile(K, tko)
    tn = _tile(npl, tn)
    nn = N // tn
    per_plane = npl // tn

    def body(a_ref, w_ref, o_ref, *scratch):
        if nn == 1:
            o_ref[...] = _dot_nt(a_ref[...], w_ref[...]).astype(o_ref.dtype)
            return
        acc = scratch[0]
        k = pl.program_id(2)

        @pl.when(k == 0)
        def _():
            acc[...] = jnp.zeros_like(acc)

        acc[...] += _dot_nt(a_ref[...], w_ref[...])

        @pl.when(k == nn - 1)
        def _():
            o_ref[...] = acc[...].astype(o_ref.dtype)

    if planes:
        a_spec = pl.BlockSpec((None, tm, tn), lambda i, j, k: (k // per_plane, i, k % per_plane))
    else:
        a_spec = pl.BlockSpec((tm, tn), lambda i, j, k: (i, k))
    return _pcall(body, name=name, grid=(M // tm, K // tko, nn),
                  in_specs=[a_spec, pl.BlockSpec((None, tko, tn), lambda i, j, k: (l, j, k))],
                  out_specs=pl.BlockSpec((tm, tko), lambda i, j, k: (i, j)),
                  out_shape=jax.ShapeDtypeStruct((M, K), out_dtype), args=[a, w],
                  scratch_shapes=[pltpu.VMEM((tm, tko), f32)] if nn > 1 else [], sem=(PAR, PAR, ARB), vmem=VMEM_BIG,
                  comms=comms)


def _mm_tn(a, b, *, name, col_sharded, comms=()):
    planes = b.ndim == 3
    S, K = a.shape
    N = b.shape[-1] * (2 if planes else 1)
    if col_sharded:
        tn, tk, ts = N // NDEV, _tile(K, 1024), _tile(S, 2048, 16)
    else:
        tn, tk, ts = N, _tile(K, 1408), _tile(S, 1024, 16)
    ns_steps = S // ts
    per_plane = (b.shape[-1] // tn) if planes else 0

    def body(a_ref, b_ref, o_ref, acc):
        s = pl.program_id(2)

        @pl.when(s == 0)
        def _():
            acc[...] = jnp.zeros_like(acc)

        acc[...] += lax.dot_general(a_ref[...], b_ref[...], (((0,), (0,)), ((), ())), preferred_element_type=f32)

        @pl.when(s == ns_steps - 1)
        def _():
            o_ref[...] = acc[...].astype(o_ref.dtype)

    if planes:
        b_spec = pl.BlockSpec((None, ts, tn), lambda k, n, s: (n // per_plane, s, n % per_plane))
    else:
        b_spec = pl.BlockSpec((ts, tn), lambda k, n, s: (s, n))
    if col_sharded:
        out_shape = jax.ShapeDtypeStruct((NDEV, K, tn), bf16)
        out_spec = pl.BlockSpec((None, tk, tn), lambda k, n, s: (n, k, 0))
    else:
        out_shape = jax.ShapeDtypeStruct((K, N), bf16)
        out_spec = pl.BlockSpec((tk, tn), lambda k, n, s: (k, n))
    res = _pcall(body, name=name, grid=(K // tk, N // tn, ns_steps),
                 in_specs=[pl.BlockSpec((ts, tk), lambda k, n, s: (s, k)), b_spec],
                 out_specs=out_spec, out_shape=out_shape, args=[a, b],
                 scratch_shapes=[pltpu.VMEM((tk, tn), f32)], sem=(PAR, PAR, ARB), vmem=VMEM_BIG, comms=comms)
    out, couts = res if comms else (res, None)
    if not col_sharded:
        out = out.reshape(NDEV, K // NDEV, N)
    return (out, couts) if comms else out


def _acc_spec(w, rows=1):
    return pl.BlockSpec((rows, w), lambda i: (0, 0))


def _mod_fwd(x, g, sh, sc, name, comms=()):
    S, W = x.shape
    tm = _tile(S, 256, 8)

    def body(x_ref, g_ref, sh_ref, sc_ref, h_ref):
        xv = x_ref[...]
        r = lax.rsqrt(jnp.mean(xv * xv, axis=-1, keepdims=True) + EPS)
        h_ref[...] = ((xv * r) * g_ref[...] * (1.0 + sc_ref[...]) + sh_ref[...]).astype(h_ref.dtype)

    row = pl.BlockSpec((tm, W), lambda i: (i, 0))
    return _pcall(body, name=name, grid=(S // tm,), in_specs=[row, _acc_spec(W), _acc_spec(W), _acc_spec(W)],
                  out_specs=row, out_shape=jax.ShapeDtypeStruct((S, W), bf16), args=[x, g, sh, sc], sem=(PAR,), comms=comms)


def _gate_tail(d, f_ref, gate_ref, df_ref, dgate_ref, sdf_ref, first):
    @pl.when(first)
    def _():
        dgate_ref[...] = jnp.zeros_like(dgate_ref)
        sdf_ref[...] = jnp.zeros_like(sdf_ref)

    df = gate_ref[...] * d
    df_ref[...] = df.astype(df_ref.dtype)
    dgate_ref[...] += jnp.sum(d * f_ref[...].astype(f32), axis=0, keepdims=True)
    sdf_ref[...] += jnp.sum(df, axis=0, keepdims=True)


def _mod_bwd(dh, x, dx_in, g, sc, name, comms=(), gate_next=None):
    S, W = x.shape
    tm = _tile(S, 256, 16)
    nt = S // tm
    fused = gate_next is not None

    def body(dh_ref, x_ref, dxi_ref, g_ref, sc_ref, *rest):
        if fused:
            f_ref, gate_ref, dx_ref, dsh_ref, dsc_ref, dg_ref, df_ref, dgate_ref, sdf_ref = rest
        else:
            dx_ref, dsh_ref, dsc_ref, dg_ref = rest
        i = pl.program_id(0)

        @pl.when(i == 0)
        def _():
            dsh_ref[...] = jnp.zeros_like(dsh_ref)
            dsc_ref[...] = jnp.zeros_like(dsc_ref)

        xv = x_ref[...]
        dh = dh_ref[...].astype(f32)
        r = lax.rsqrt(jnp.mean(xv * xv, axis=-1, keepdims=True) + EPS)
        n = xv * r
        dn = dh * (g_ref[...] * (1.0 + sc_ref[...]))
        dx = dxi_ref[...] + r * (dn - n * jnp.mean(dn * n, axis=-1, keepdims=True))
        dx_ref[...] = dx
        dsh_ref[...] += jnp.sum(dh, axis=0, keepdims=True)
        dsc_ref[...] += jnp.sum(dh * n, axis=0, keepdims=True)
        if fused:
            _gate_tail(dx, f_ref, gate_ref, df_ref, dgate_ref, sdf_ref, i == 0)

        @pl.when(i == nt - 1)
        def _():
            a2 = dsc_ref[...]
            dg_ref[...] = a2 * (1.0 + sc_ref[...])
            dsc_ref[...] = a2 * g_ref[...]

    row = pl.BlockSpec((tm, W), lambda i: (i, 0))
    vec = jax.ShapeDtypeStruct((1, W), f32)
    in_specs, args = [row, row, row, _acc_spec(W), _acc_spec(W)], [dh, x, dx_in, g, sc]
    out_specs = [row, _acc_spec(W), _acc_spec(W), _acc_spec(W)]
    out_shape = [jax.ShapeDtypeStruct((S, W), f32), vec, vec, vec]
    if fused:
        in_specs, args = in_specs + [row, _acc_spec(W)], args + list(gate_next)
        out_specs = out_specs + [row, _acc_spec(W), _acc_spec(W)]
        out_shape = out_shape + [jax.ShapeDtypeStruct((S, W), bf16), vec, vec]
    return _pcall(body, name=name, grid=(nt,), in_specs=in_specs, out_specs=out_specs, out_shape=out_shape, args=args,
                  sem=(ARB,), comms=comms)


def _loss_grad(y, target, f, gate, name):
    S, W = y.shape
    tm = _tile(S, 256, 16)

    def body(y_ref, t_ref, f_ref, gate_ref, dy_ref, l_ref, df_ref, dgate_ref, sdf_ref):
        i = pl.program_id(0)

        @pl.when(i == 0)
        def _():
            l_ref[...] = jnp.zeros_like(l_ref)

        e = y_ref[...] - t_ref[...]
        dy = e * (1.0 / W)
        dy_ref[...] = dy
        l_ref[...] += 0.5 * jnp.sum(jnp.mean(e * e, axis=-1, keepdims=True))
        _gate_tail(dy, f_ref, gate_ref, df_ref, dgate_ref, sdf_ref, i == 0)

    row = pl.BlockSpec((tm, W), lambda i: (i, 0))
    vec = jax.ShapeDtypeStruct((1, W), f32)
    return pl.pallas_call(
        body, name=name, grid=(S // tm,), in_specs=[row, row, row, _acc_spec(W)],
        out_specs=[row, pl.BlockSpec((8, 128), lambda i: (0, 0)), row, _acc_spec(W), _acc_spec(W)],
        out_shape=[jax.ShapeDtypeStruct((S, W), f32), jax.ShapeDtypeStruct((8, 128), f32), jax.ShapeDtypeStruct((S, W), bf16), vec, vec],
        compiler_params=_cp((ARB,)))(y, target, f, gate)


def _tap_groups(offsets):
    groups = {}
    for k, o in enumerate(offsets):
        groups.setdefault(o % 8, []).append((k, o - o % 8))
    return sorted(groups.items())


def _tap_sum(buf, w, offsets, tm):
    out = None
    for b, taps in _tap_groups(offsets):
        n = tm + 8 if b else tm
        y = None
        for k, base in taps:
            term = w[k:k + 1, :] * buf[pl.ds(base, n), :]
            y = term if y is None else y + term
        part = y[b:b + tm] if b else y
        out = part if out is None else out + part
    return out


def _tap_wgrad(d, buf, dsh, acc_ref, offsets, tm):
    for b, taps in _tap_groups(offsets):
        if b:
            dsh[pl.ds(0, 8), :] = jnp.zeros((8, dsh.shape[1]), f32)
            dsh[pl.ds(tm, 8), :] = jnp.zeros((8, dsh.shape[1]), f32)
            dsh[pl.ds(b, tm), :] = d
            dd, n = dsh[...], tm + 8
        else:
            dd, n = d, tm
        for k, base in taps:
            acc_ref[pl.ds(k, 1), :] += jnp.sum(dd * buf[pl.ds(base, n), :], axis=0, keepdims=True)


_CONV_OFFSETS = [HALO - (CONV_K - 1) + k for k in range(CONV_K)]
_CONV_OFFSETS_T = [CONV_K - 1 - k for k in range(CONV_K)]


def _conv_core(u_ref, uh_ref, w_ref, b_ref, lg_ref, lb_ref, gbuf, tm, first, cv_ref=None):
    C = u_ref.shape[1] // 2
    u = u_ref[...].astype(f32)
    uh = uh_ref[...].astype(f32)
    gbuf[pl.ds(HALO, tm), :] = u[:, :C] * _sigmoid(u[:, C:])
    halo = uh[:, :C] * _sigmoid(uh[:, C:])
    gbuf[pl.ds(0, HALO), :] = jnp.where(first, 0.0, halo)
    cv = _tap_sum(gbuf, w_ref[...], _CONV_OFFSETS, tm) + b_ref[...] if cv_ref is None else cv_ref[...]
    mu = jnp.mean(cv, axis=-1, keepdims=True)
    xc = cv - mu
    rstd = lax.rsqrt(jnp.mean(xc * xc, axis=-1, keepdims=True) + EPS)
    z = xc * rstd
    ln = z * lg_ref[...] + lb_ref[...]
    return cv, z, rstd, ln


def _halo_prev(tm, hb, w):
    return pl.BlockSpec((hb, w), lambda i: (jnp.maximum(i * (tm // hb) - 1, 0), 0))


def _conv_fwd(u, w, b, lg, lb, name, comms=()):
    S, C2 = u.shape
    C = C2 // 2
    tm = _tile(S, 256, HALO)

    def body(u_ref, uh_ref, w_ref, b_ref, lg_ref, lb_ref, s_ref, cv_ref, gbuf):
        first = pl.program_id(0) == 0
        cv, _, _, ln = _conv_core(u_ref, uh_ref, w_ref, b_ref, lg_ref, lb_ref, gbuf, tm, first)
        s_ref[...] = (ln * _sigmoid(ln)).astype(s_ref.dtype)
        cv_ref[...] = cv

    return _pcall(body, name=name, grid=(S // tm,),
                  in_specs=[pl.BlockSpec((tm, C2), lambda i: (i, 0)), _halo_prev(tm, HALO, C2), _acc_spec(C, 32),
                            _acc_spec(C), _acc_spec(C), _acc_spec(C)],
                  out_specs=[pl.BlockSpec((tm, C), lambda i: (i, 0))] * 2,
                  out_shape=[jax.ShapeDtypeStruct((S, C), bf16), jax.ShapeDtypeStruct((S, C), f32)],
                  args=[u, u, w, b, lg, lb], scratch_shapes=[pltpu.VMEM((tm + HALO, C), f32)], sem=(PAR,), vmem=VMEM_BIG,
                  comms=comms)


def _conv_bwd1(u, cv, ds, w, b, lg, lb, name, comms=()):
    S, C2 = u.shape
    C = C2 // 2
    tm = _tile(S, 256, HALO)

    def body(u_ref, uh_ref, cv_ref, ds_ref, w_ref, b_ref, lg_ref, lb_ref, dcv_ref, dlg_ref, dlb_ref, ddb_ref, ddw_ref, gbuf, dsh):
        i = pl.program_id(0)

        @pl.when(i == 0)
        def _():
            dlg_ref[...] = jnp.zeros_like(dlg_ref)
            dlb_ref[...] = jnp.zeros_like(dlb_ref)
            ddb_ref[...] = jnp.zeros_like(ddb_ref)
            ddw_ref[...] = jnp.zeros_like(ddw_ref)

        _, z, rstd, ln = _conv_core(u_ref, uh_ref, w_ref, b_ref, lg_ref, lb_ref, gbuf, tm, i == 0, cv_ref)
        sg = _sigmoid(ln)
        dln = ds_ref[...].astype(f32) * (sg * (1.0 + ln * (1.0 - sg)))
        dlg_ref[...] += jnp.sum(dln * z, axis=0, keepdims=True)
        dlb_ref[...] += jnp.sum(dln, axis=0, keepdims=True)
        dz = dln * lg_ref[...]
        dcv = rstd * (dz - jnp.mean(dz, axis=-1, keepdims=True) - z * jnp.mean(dz * z, axis=-1, keepdims=True))
        dcv_ref[...] = dcv
        ddb_ref[...] += jnp.sum(dcv, axis=0, keepdims=True)
        _tap_wgrad(dcv, gbuf, dsh, ddw_ref, _CONV_OFFSETS, tm)

    vec = jax.ShapeDtypeStruct((1, C), f32)
    return _pcall(
        body, name=name, grid=(S // tm,),
        in_specs=[pl.BlockSpec((tm, C2), lambda i: (i, 0)), _halo_prev(tm, HALO, C2), pl.BlockSpec((tm, C), lambda i: (i, 0)),
                  pl.BlockSpec((tm, C), lambda i: (i, 0)), _acc_spec(C, 32), _acc_spec(C), _acc_spec(C), _acc_spec(C)],
        out_specs=[pl.BlockSpec((tm, C), lambda i: (i, 0)), _acc_spec(C), _acc_spec(C), _acc_spec(C), _acc_spec(C, 32)],
        out_shape=[jax.ShapeDtypeStruct((S, C), f32), vec, vec, vec, jax.ShapeDtypeStruct((32, C), f32)],
        args=[u, u, cv, ds, w, b, lg, lb], scratch_shapes=[pltpu.VMEM((tm + HALO, C), f32), pltpu.VMEM((tm + 8, C), f32)],
        sem=(ARB,), vmem=VMEM_BIG, comms=comms)


def _conv_bwd2(dcv, u, w, name, comms=()):
    S, C2 = u.shape
    C = C2 // 2
    tm = _tile(S, 256, HALO)
    nt = S // tm
    nhb = S // HALO

    def body(d_ref, dn_ref, u_ref, w_ref, du_ref, db_ref, dbuf):
        i = pl.program_id(0)

        @pl.when(i == 0)
        def _():
            db_ref[...] = jnp.zeros_like(db_ref)

        dbuf[pl.ds(0, tm), :] = d_ref[...]
        dbuf[pl.ds(tm, HALO), :] = jnp.where(i == nt - 1, 0.0, dn_ref[...])
        dglu = _tap_sum(dbuf, w_ref[...], _CONV_OFFSETS_T, tm)
        u = u_ref[...].astype(f32)
        a, gt = u[:, :C], u[:, C:]
        sg = _sigmoid(gt)
        da = dglu * sg
        dgt = dglu * a * sg * (1.0 - sg)
        du_ref[:, :C] = da.astype(du_ref.dtype)
        du_ref[:, C:] = dgt.astype(du_ref.dtype)
        db_ref[:, :C] += jnp.sum(da, axis=0, keepdims=True)
        db_ref[:, C:] += jnp.sum(dgt, axis=0, keepdims=True)

    return _pcall(
        body, name=name, grid=(nt,),
        in_specs=[pl.BlockSpec((tm, C), lambda i: (i, 0)),
                  pl.BlockSpec((HALO, C), lambda i: (jnp.minimum((i + 1) * (tm // HALO), nhb - 1), 0)),
                  pl.BlockSpec((tm, C2), lambda i: (i, 0)), _acc_spec(C, 32)],
        out_specs=[pl.BlockSpec((tm, C2), lambda i: (i, 0)), _acc_spec(C2)],
        out_shape=[jax.ShapeDtypeStruct((S, C2), bf16), jax.ShapeDtypeStruct((1, C2), f32)],
        args=[dcv, dcv, u, w], scratch_shapes=[pltpu.VMEM((tm + HALO, C), f32)], sem=(ARB,), vmem=VMEM_BIG, comms=comms)


def _up_gate(h, w_up, w, b, name, comms=()):
    S, K = h.shape
    F = w.shape[1]
    tm = _tile(S, FFN_TM, 16)
    tn = _tile(F, 512)
    nf = F // tn
    ch = _tile(tn, 256)

    def body(h_ref, wg_ref, wv_ref, w_ref, b_ref, u_ref, a_ref, tail):
        i, j = pl.program_id(0), pl.program_id(1)

        @pl.when(i == 0)
        def _():
            tail[j] = jnp.zeros((8, tn), f32)

        hv = h_ref[...]
        for c in range(tn // ch):
            cs = slice(c * ch, (c + 1) * ch)
            g16 = jnp.dot(hv, wg_ref[:, cs], preferred_element_type=f32).astype(bf16)
            v16 = jnp.dot(hv, wv_ref[:, cs], preferred_element_type=f32).astype(bf16)
            u_ref[0, :, cs] = g16
            u_ref[1, :, cs] = v16
            g = g16.astype(f32)
            ext = jnp.concatenate([tail[j, :, cs], g], axis=0)
            gc = b_ref[:, cs] + w_ref[0:1, cs] * ext[6:6 + tm] + w_ref[1:2, cs] * ext[7:7 + tm] + w_ref[2:3, cs] * g
            a_ref[:, cs] = (gc * _sigmoid(gc) * v16.astype(f32)).astype(a_ref.dtype)
            tail[j, :, cs] = g[tm - 8:tm]

    return _pcall(
        body, name=name, grid=(S // tm, nf),
        in_specs=[pl.BlockSpec((tm, K), lambda i, j: (i, 0)),
                  pl.BlockSpec((None, K, tn), lambda i, j: (0, 0, j)), pl.BlockSpec((None, K, tn), lambda i, j: (0, 0, nf + j)),
                  pl.BlockSpec((8, tn), lambda i, j: (0, j)), pl.BlockSpec((1, tn), lambda i, j: (0, j))],
        out_specs=[pl.BlockSpec((2, tm, tn), lambda i, j: (0, i, j)), pl.BlockSpec((tm, tn), lambda i, j: (i, j))],
        out_shape=[jax.ShapeDtypeStruct((2, S, F), bf16), jax.ShapeDtypeStruct((S, F), bf16)],
        args=[h, w_up, w_up, w, b], scratch_shapes=[pltpu.VMEM((nf, 8, tn), f32)], sem=(ARB, ARB), vmem=VMEM_BIG, comms=comms)


def _dact_gate_bwd(df, w_down, u2, w, b, name, comms=()):
    S, D_ = df.shape
    F = w.shape[1]
    tm = _tile(S, FFN_TM, 16)
    tn = _tile(F, 512)
    nf, nt = F // tn, S // tm
    ch = _tile(tn, 256)
    hb = tm // FHALO

    def body(df_ref, wd_ref, g_ref, gp_ref, v_ref, w_ref, b_ref, du_ref, dw_ref, db_ref, head):
        ii = pl.program_id(1)
        first_tile = ii == nt - 1

        @pl.when(ii == 0)
        def _():
            dw_ref[...] = jnp.zeros_like(dw_ref)
            db_ref[...] = jnp.zeros_like(db_ref)
            head[...] = jnp.zeros_like(head)

        dfv = df_ref[...]
        for c in range(tn // ch):
            cs = slice(c * ch, (c + 1) * ch)
            dact = _dot_nt(dfv, wd_ref[cs, :])
            hist = jnp.where(first_tile, 0.0, gp_ref[:, cs].astype(f32))
            g = jnp.concatenate([hist, g_ref[:, cs].astype(f32)], axis=0)
            taps = [g[FHALO - 2 + k:FHALO - 2 + k + tm] for k in range(FFN_K)]
            gc = b_ref[:, cs] + w_ref[0:1, cs] * taps[0] + w_ref[1:2, cs] * taps[1] + w_ref[2:3, cs] * taps[2]
            sg = _sigmoid(gc)
            dgc = dact * v_ref[:, cs].astype(f32) * (sg * (1.0 + gc * (1.0 - sg)))
            du_ref[1, :, cs] = (dact * (gc * sg)).astype(du_ref.dtype)
            ext = jnp.concatenate([dgc, head[:, cs]], axis=0)
            dgt = w_ref[0:1, cs] * ext[2:2 + tm] + w_ref[1:2, cs] * ext[1:1 + tm] + w_ref[2:3, cs] * dgc
            du_ref[0, :, cs] = dgt.astype(du_ref.dtype)
            db_ref[:, cs] += jnp.sum(dgc, axis=0, keepdims=True)
            for k in range(FFN_K):
                dw_ref[pl.ds(k, 1), cs] += jnp.sum(dgc * taps[k], axis=0, keepdims=True)
            head[:, cs] = dgc[0:8]

    rev = lambda ii: nt - 1 - ii
    return _pcall(
        body, name=name, grid=(nf, nt), comms=comms, sem=(ARB, ARB), vmem=VMEM_BIG,
        args=[df, w_down, u2, u2, u2, w, b],
        in_specs=[pl.BlockSpec((tm, D_), lambda j, ii: (rev(ii), 0)),
                  pl.BlockSpec((None, tn, D_), lambda j, ii: (0, j, 0)),
                  pl.BlockSpec((None, tm, tn), lambda j, ii: (0, rev(ii), j)),
                  pl.BlockSpec((None, FHALO, tn), lambda j, ii: (0, jnp.maximum(rev(ii) * hb - 1, 0), j)),
                  pl.BlockSpec((None, tm, tn), lambda j, ii: (1, rev(ii), j)),
                  pl.BlockSpec((8, tn), lambda j, ii: (0, j)), pl.BlockSpec((1, tn), lambda j, ii: (0, j))],
        out_specs=[pl.BlockSpec((2, tm, tn), lambda j, ii: (0, rev(ii), j)), pl.BlockSpec((8, tn), lambda j, ii: (0, j)),
                   pl.BlockSpec((1, tn), lambda j, ii: (0, j))],
        out_shape=[jax.ShapeDtypeStruct((2, S, F), bf16), jax.ShapeDtypeStruct((8, F), f32), jax.ShapeDtypeStruct((1, F), f32)],
        scratch_shapes=[pltpu.VMEM((8, tn), f32)])


def _rope_tables(pos_col, name):
    S = pos_col.shape[0]
    tm = _tile(S, 512, 8)
    half = ROT // 2
    inv = THETA ** (-np.arange(0, ROT, 2, dtype=np.float32) / ROT)
    lane_freq = np.zeros((1, DH), np.float32)
    lane_freq[0, :half] = inv
    lane_freq[0, half:ROT] = inv
    lane_freq = jnp.asarray(lane_freq)

    def body(p_ref, fr_ref, c_ref, sa_ref, sb_ref):
        ang = p_ref[...].astype(f32) * fr_ref[...]
        lane = lax.broadcasted_iota(jnp.int32, (tm, DH), 1)
        cs, sn = jnp.cos(ang), jnp.sin(ang)
        c_ref[...] = jnp.where(lane < ROT, cs, 1.0)
        sa_ref[...] = jnp.where(lane < half, -sn, 0.0)
        sb_ref[...] = jnp.where((lane >= half) & (lane < ROT), sn, 0.0)

    row = pl.BlockSpec((tm, DH), lambda i: (i, 0))
    shp = jax.ShapeDtypeStruct((S, DH), f32)
    return pl.pallas_call(body, name=name, grid=(S // tm,),
                          in_specs=[pl.BlockSpec((tm, 1), lambda i: (i, 0)), pl.BlockSpec((1, DH), lambda i: (0, 0))],
                          out_specs=[row, row, row], out_shape=[shp, shp, shp], compiler_params=_cp((PAR,)))(pos_col, lane_freq)


def _swap_matrix():
    k = lax.broadcasted_iota(jnp.int32, (DH, DH), 0)
    i = lax.broadcasted_iota(jnp.int32, (DH, DH), 1)
    half = ROT // 2
    hit = ((i < half) & (k == i + half)) | ((i >= half) & (i < ROT) & (k == i - half))
    return jnp.where(hit, 1.0, 0.0).astype(bf16)


def _head_mean(x):
    return jnp.dot(x.astype(bf16), jnp.ones((DH, DH), bf16), preferred_element_type=f32) * (1.0 / DH)


def _rope(n, c, t, swap):
    return n * c + jnp.dot(n.astype(bf16), swap, preferred_element_type=f32) * t


def _rope_t(d, c, t, swap):
    return d * c + jnp.dot((d * t).astype(bf16), swap, preferred_element_type=f32)


def _qk_fwd(raw, g, tabs, width, with_values, name):
    S = raw.shape[0]
    nh = width // DH
    ow = width // NG
    hpg = ow // DH
    tm = _tile(S, 256, 16 * max(DILS))
    vgroups = [gi for gi, r in enumerate(DILS) if r > 1] if with_values else []

    def body(x_ref, g_ref, c_ref, sa_ref, sb_ref, *rest):
        o_refs = rest[:NG]
        v_refs = rest[NG:NG + len(vgroups)]
        scr, vscr = rest[NG + len(vgroups):]
        c, t, swap = c_ref[...], sa_ref[...] + sb_ref[...], _swap_matrix()
        for gi, r in enumerate(DILS):
            heads = range(gi * hpg, (gi + 1) * hpg)
            xs = [x_ref[:, h * DH:(h + 1) * DH].astype(f32) for h in heads]
            rs = [lax.rsqrt(_head_mean(xv * xv) + EPS) for xv in xs]
            ys = [_rope(xv * rv * g_ref[...], c, t, swap) for xv, rv in zip(xs, rs)]
            for hh, y in enumerate(ys):
                if r == 1:
                    o_refs[gi][:, hh * DH:(hh + 1) * DH] = y.astype(bf16)
                else:
                    scr[hh] = y
            if r > 1:
                for hh in range(hpg):
                    for j in range(r):
                        o_refs[gi][:, j * ow + hh * DH:j * ow + (hh + 1) * DH] = scr[hh, pl.ds(j, tm // r, stride=r), :].astype(bf16)
        for vi, gi in enumerate(vgroups):
            _to_view(x_ref[:, width + gi * ow:width + (gi + 1) * ow].astype(f32), v_refs[vi], vscr, DILS[gi], ow, tm)

    win = raw.shape[1] if with_values else width
    row = pl.BlockSpec((tm, win), lambda i: (i, 0))
    tab = pl.BlockSpec((tm, DH), lambda i: (i, 0))
    view = lambda r: pl.BlockSpec((tm // r, r * ow), lambda i: (i, 0))
    vshape = lambda r: jax.ShapeDtypeStruct((S // r, r * ow), bf16)
    outs = pl.pallas_call(
        body, name=name, grid=(S // tm,), in_specs=[row, _acc_spec(DH), tab, tab, tab],
        out_specs=[view(r) for r in DILS] + [view(DILS[gi]) for gi in vgroups],
        out_shape=[vshape(r) for r in DILS] + [vshape(DILS[gi]) for gi in vgroups],
        scratch_shapes=[pltpu.VMEM((hpg, tm, DH), f32), pltpu.VMEM((ow // DH, tm, DH), f32)],
        compiler_params=_cp((PAR,)))(raw, g, *tabs)
    return outs[:NG], outs[NG:]


def _qk_bwd(dparts, raw, g, tabs, width, extra, name):
    S = raw.shape[0]
    nh = width // DH
    ow = width // NG
    hpg = ow // DH
    tm = _tile(S, 256, 16 * max(DILS))
    wout = width + len(extra) * ow

    def body(*refs):
        d_refs = refs[:NG]
        x_ref, g_ref, c_ref, sa_ref, sb_ref = refs[NG:NG + 5]
        e_refs = refs[NG + 5:NG + 5 + len(extra)]
        o_ref, dg_ref, scr, vscr = refs[NG + 5 + len(extra):]
        i = pl.program_id(0)

        @pl.when(i == 0)
        def _():
            dg_ref[...] = jnp.zeros_like(dg_ref)

        c, t, swap = c_ref[...], sa_ref[...] + sb_ref[...], _swap_matrix()
        gv = g_ref[...]
        dg = jnp.zeros((1, DH), f32)
        for gi, r in enumerate(DILS):
            heads = list(range(gi * hpg, (gi + 1) * hpg))
            if r == 1:
                douts = [d_refs[gi][:, hh * DH:(hh + 1) * DH].astype(f32) for hh in range(hpg)]
            else:
                for hh in range(hpg):
                    for j in range(r):
                        scr[hh, pl.ds(j, tm // r, stride=r), :] = d_refs[gi][:, j * ow + hh * DH:j * ow + (hh + 1) * DH].astype(f32)
                douts = [scr[hh] for hh in range(hpg)]
            xs = [x_ref[:, h * DH:(h + 1) * DH].astype(f32) for h in heads]
            rs = [lax.rsqrt(_head_mean(xv * xv) + EPS) for xv in xs]
            xhs = [xv * rv for xv, rv in zip(xs, rs)]
            dns = [_rope_t(d, c, t, swap) for d in douts]
            for dn, xh in zip(dns, xhs):
                dg = dg + jnp.sum(dn * xh, axis=0, keepdims=True)
            dxns = [dn * gv for dn in dns]
            dxs = [rv * (dxn - xh * _head_mean(dxn * xh)) for rv, dxn, xh in zip(rs, dxns, xhs)]
            for h, dx in zip(heads, dxs):
                o_ref[:, h * DH:(h + 1) * DH] = dx.astype(o_ref.dtype)
        for gi, e_ref in enumerate(e_refs):
            o_ref[:, width + gi * ow:width + (gi + 1) * ow] = _from_view(e_ref, vscr, DILS[gi], ow, tm).astype(o_ref.dtype)
        dg_ref[...] += dg

    views = [pl.BlockSpec((tm // r, r * ow), lambda i: (i, 0)) for r in DILS]
    tab = pl.BlockSpec((tm, DH), lambda i: (i, 0))
    return pl.pallas_call(
        body, name=name, grid=(S // tm,),
        in_specs=views + [pl.BlockSpec((tm, width), lambda i: (i, 0)), _acc_spec(DH), tab, tab, tab] + (views if extra else []),
        out_specs=[pl.BlockSpec((tm, wout), lambda i: (i, 0)), _acc_spec(DH)],
        out_shape=[jax.ShapeDtypeStruct((S, wout), bf16), jax.ShapeDtypeStruct((1, DH), f32)],
        scratch_shapes=[pltpu.VMEM((hpg, tm, DH), f32), pltpu.VMEM((ow // DH, tm, DH), f32)],
        compiler_params=_cp((ARB,)))(*dparts, raw, g, *tabs, *extra)


def _dot_nt(a, b):
    return lax.dot_general(a, b, (((1,), (1,)), ((), ())), preferred_element_type=f32)


def _dot_tn(a, b):
    return lax.dot_general(a, b, (((0,), (0,)), ((), ())), preferred_element_type=f32)


def _band_masks():
    qi = lax.broadcasted_iota(jnp.int32, (BLK, BLK), 0)
    ki = lax.broadcasted_iota(jnp.int32, (BLK, BLK), 1)
    return ki <= qi, ki >= qi


def _attn_fwd(qv, kview, vview, vbase, r, name):
    sr = qv.shape[0]
    ow = qv.shape[1] // r
    hpg = ow // DH
    nb = sr // BLK
    scale = 1.0 / math.sqrt(DH)

    def body(q_ref, kc_ref, kp_ref, vc_ref, vp_ref, o_ref, l_ref):
        n = pl.program_id(1)
        m_cur, m_prev = _band_masks()
        m_prev = m_prev & (n > 0)
        hs = [slice(h * DH, (h + 1) * DH) for h in range(hpg)]
        s_c = [jnp.where(m_cur, _dot_nt(q_ref[:, s], kc_ref[:, s]) * scale, NEG) for s in hs]
        s_p = [jnp.where(m_prev, _dot_nt(q_ref[:, s], kp_ref[:, s]) * scale, NEG) for s in hs]
        mx = [jnp.maximum(jnp.max(a, axis=-1, keepdims=True), jnp.max(b, axis=-1, keepdims=True)) for a, b in zip(s_c, s_p)]
        p_c = [jnp.exp(a - m) for a, m in zip(s_c, mx)]
        p_p = [jnp.exp(a - m) for a, m in zip(s_p, mx)]
        den = [jnp.sum(a, axis=-1, keepdims=True) + jnp.sum(b, axis=-1, keepdims=True) for a, b in zip(p_c, p_p)]
        for h, s in enumerate(hs):
            o = jnp.dot(p_c[h].astype(bf16), vc_ref[:, s], preferred_element_type=f32)
            o = o + jnp.dot(p_p[h].astype(bf16), vp_ref[:, s], preferred_element_type=f32)
            o_ref[:, s] = (o / den[h]).astype(o_ref.dtype)
            l_ref[:, s] = jnp.broadcast_to(mx[h] + jnp.log(den[h]), (BLK, DH))

    cur = lambda j, n: (n, j)
    prev = lambda j, n: (jnp.maximum(n - 1, 0), j)
    vcur = lambda j, n: (n, vbase + j)
    vprev = lambda j, n: (jnp.maximum(n - 1, 0), vbase + j)
    blk = lambda f: pl.BlockSpec((BLK, ow), f)
    return pl.pallas_call(
        body, name=name, grid=(r, nb), in_specs=[blk(cur), blk(cur), blk(prev), blk(vcur), blk(vprev)],
        out_specs=[blk(cur), blk(cur)],
        out_shape=[jax.ShapeDtypeStruct((sr, r * ow), bf16), jax.ShapeDtypeStruct((sr, r * ow), f32)],
        compiler_params=_cp((PAR, PAR)))(qv, kview, kview, vview, vview)


def _attn_bwd_q(qv, kview, vview, vbase, do_g, lse, corr, r, name, comms=()):
    sr = qv.shape[0]
    ow = qv.shape[1] // r
    hpg = ow // DH
    nb = sr // BLK
    scale = 1.0 / math.sqrt(DH)

    def body(q_ref, kc_ref, kp_ref, vc_ref, vp_ref, do_ref, l_ref, c_ref, dq_ref):
        n = pl.program_id(1)
        m_cur, m_prev = _band_masks()
        m_prev = m_prev & (n > 0)
        hs = [slice(h * DH, (h + 1) * DH) for h in range(hpg)]
        ls = [slice(h * DH, h * DH + BLK) for h in range(hpg)]
        sides = ((kc_ref, vc_ref, m_cur), (kp_ref, vp_ref, m_prev))
        sc = [[jnp.where(msk, _dot_nt(q_ref[:, s], k_ref[:, s]) * scale, NEG) for s in hs] for k_ref, _, msk in sides]
        dp = [[_dot_nt(do_ref[:, s], v_ref[:, s]) for s in hs] for _, v_ref, _ in sides]
        ds = [[(jnp.exp(sc[i][h] - l_ref[:, ls[h]]) * (dp[i][h] + c_ref[:, ls[h]])).astype(bf16) for h in range(hpg)]
              for i in range(2)]
        for h, s in enumerate(hs):
            dq = jnp.dot(ds[0][h], kc_ref[:, s], preferred_element_type=f32)
            dq = dq + jnp.dot(ds[1][h], kp_ref[:, s], preferred_element_type=f32)
            dq_ref[:, s] = (dq * scale).astype(dq_ref.dtype)

    cur = lambda j, n: (n, j)
    prev = lambda j, n: (jnp.maximum(n - 1, 0), j)
    vcur = lambda j, n: (n, vbase + j)
    vprev = lambda j, n: (jnp.maximum(n - 1, 0), vbase + j)
    blk = lambda f: pl.BlockSpec((BLK, ow), f)
    return _pcall(
        body, name=name, grid=(r, nb),
        in_specs=[blk(cur), blk(cur), blk(prev), blk(vcur), blk(vprev), blk(cur), blk(cur), blk(cur)],
        out_specs=blk(cur), out_shape=jax.ShapeDtypeStruct((sr, r * ow), bf16), sem=(PAR, PAR), comms=comms,
        args=[qv, kview, kview, vview, vview, do_g, lse, corr])


def _attn_bwd_kv(qv, kview, vview, vbase, do_g, lse, corr, r, name):
    sr = qv.shape[0]
    ow = qv.shape[1] // r
    hpg = ow // DH
    nb = sr // BLK
    scale = 1.0 / math.sqrt(DH)

    def body(k_ref, v_ref, qc_ref, qn_ref, doc_ref, don_ref, lc_ref, ln_ref, cc_ref, cn_ref, dk_ref, dv_ref):
        n = pl.program_id(1)
        m_cur, m_prev = _band_masks()
        m_next = m_prev & (n < nb - 1)
        hs = [slice(h * DH, (h + 1) * DH) for h in range(hpg)]
        ls = [slice(h * DH, h * DH + BLK) for h in range(hpg)]
        sides = ((qc_ref, doc_ref, lc_ref, cc_ref, m_cur), (qn_ref, don_ref, ln_ref, cn_ref, m_next))
        sc = [[jnp.where(msk, _dot_nt(q_ref[:, s], k_ref[:, s]) * scale, NEG) for s in hs] for q_ref, _, _, _, msk in sides]
        dp = [[_dot_nt(do_ref[:, s], v_ref[:, s]) for s in hs] for _, do_ref, _, _, _ in sides]
        p = [[jnp.exp(sc[i][h] - sides[i][2][:, ls[h]]) for h in range(hpg)] for i in range(2)]
        ds = [[(p[i][h] * (dp[i][h] + sides[i][3][:, ls[h]])).astype(bf16) for h in range(hpg)] for i in range(2)]
        for h, s in enumerate(hs):
            dv = _dot_tn(p[0][h].astype(bf16), doc_ref[:, s]) + _dot_tn(p[1][h].astype(bf16), don_ref[:, s])
            dk = _dot_tn(ds[0][h], qc_ref[:, s]) + _dot_tn(ds[1][h], qn_ref[:, s])
            dk_ref[:, s] = (dk * scale).astype(dk_ref.dtype)
            dv_ref[:, s] = dv.astype(dv_ref.dtype)

    cur = lambda j, n: (n, j)
    nxt = lambda j, n: (jnp.minimum(n + 1, nb - 1), j)
    vcur = lambda j, n: (n, vbase + j)
    blk = lambda f: pl.BlockSpec((BLK, ow), f)
    shp = jax.ShapeDtypeStruct((sr, r * ow), bf16)
    return pl.pallas_call(
        body, name=name, grid=(r, nb),
        in_specs=[blk(cur), blk(vcur), blk(cur), blk(nxt), blk(cur), blk(nxt), blk(cur), blk(nxt), blk(cur), blk(nxt)],
        out_specs=[blk(cur), blk(cur)], out_shape=[shp, shp],
        compiler_params=_cp((PAR, PAR)))(kview, vview, qv, qv, do_g, do_g, lse, lse, corr, corr)


def _mix_weights(l_refs):
    ls = [l[...] for l in l_refs]
    mx = functools.reduce(jnp.maximum, ls)
    es = [jnp.exp(l - mx) for l in ls]
    den = functools.reduce(lambda a, b: a + b, es)
    return [e / den for e in es]


def _from_view(ref, scr, r, ow, tm):
    if r == 1:
        return ref[...].astype(f32)
    for c in range(ow // DH):
        for j in range(r):
            scr[c, pl.ds(j, tm // r, stride=r), :] = ref[:, j * ow + c * DH:j * ow + (c + 1) * DH].astype(f32)
    return jnp.concatenate([scr[c] for c in range(ow // DH)], axis=1)


def _to_view(val, ref, scr, r, ow, tm):
    if r == 1:
        ref[...] = val.astype(ref.dtype)
        return
    for c in range(ow // DH):
        scr[c] = val[:, c * DH:(c + 1) * DH]
        for j in range(r):
            ref[:, j * ow + c * DH:j * ow + (c + 1) * DH] = scr[c, pl.ds(j, tm // r, stride=r), :].astype(ref.dtype)


def _view_specs(tm, ow):
    return [pl.BlockSpec((tm // r, r * ow), lambda i: (i, 0)) for r in DILS]


def _combine_fwd(os_, lses, name):
    ow = os_[0].shape[1] // DILS[0]
    S = os_[0].shape[0] * DILS[0]
    tm = _tile(S, 256, 16 * max(DILS))

    def body(*refs):
        o_refs, l_refs, out_ref = refs[:NG], refs[NG:2 * NG], refs[2 * NG]
        scr = refs[2 * NG + 1:]
        ov = [_from_view(o_refs[gi], scr[2 * gi], DILS[gi], ow, tm) for gi in range(NG)]
        lv = [_from_view(l_refs[gi], scr[2 * gi + 1], DILS[gi], ow, tm) for gi in range(NG)]
        al = _mix_weights(lv)
        acc = al[0] * ov[0]
        for gi in range(1, NG):
            acc = acc + al[gi] * ov[gi]
        out_ref[...] = acc.astype(out_ref.dtype)

    views = _view_specs(tm, ow)
    return pl.pallas_call(body, name=name, grid=(S // tm,), in_specs=views + views,
                          out_specs=pl.BlockSpec((tm, ow), lambda i: (i, 0)), out_shape=jax.ShapeDtypeStruct((S, ow), bf16),
                          scratch_shapes=[pltpu.VMEM((ow // DH, tm, DH), f32)] * (2 * NG),
                          compiler_params=_cp((PAR,), VMEM_BIG))(*os_, *lses)


def _combine_bwd(do, os_, lses, name, comms=()):
    S, ow = do.shape
    hpg = ow // DH
    tm = _tile(S, 256, 16 * max(DILS))

    def body(*refs):
        do_ref = refs[0]
        o_refs, l_refs = refs[1:1 + NG], refs[1 + NG:1 + 2 * NG]
        dog_refs, c_refs = refs[1 + 2 * NG:1 + 3 * NG], refs[1 + 3 * NG:1 + 4 * NG]
        scr = refs[1 + 4 * NG:]
        ov = [_from_view(o_refs[gi], scr[2 * gi], DILS[gi], ow, tm) for gi in range(NG)]
        lv = [_from_view(l_refs[gi], scr[2 * gi + 1], DILS[gi], ow, tm) for gi in range(NG)]
        al = _mix_weights(lv)
        dov = do_ref[...]
        o = al[0] * ov[0]
        for gi in range(1, NG):
            o = o + al[gi] * ov[gi]
        prod = dov * o
        t = jnp.concatenate(
            [jnp.broadcast_to(jnp.sum(prod[:, h * DH:(h + 1) * DH], axis=-1, keepdims=True), (tm, DH)) for h in range(hpg)],
            axis=1)
        for gi in range(NG):
            _to_view(al[gi] * dov, dog_refs[gi], scr[2 * NG], DILS[gi], ow, tm)
            _to_view(-(al[gi] * t), c_refs[gi], scr[2 * NG], DILS[gi], ow, tm)

    views = _view_specs(tm, ow)
    vshape = lambda dt: [jax.ShapeDtypeStruct((S // r, r * ow), dt) for r in DILS]
    return _pcall(
        body, name=name, grid=(S // tm,), in_specs=[pl.BlockSpec((tm, ow), lambda i: (i, 0))] + views + views,
        out_specs=views + views, out_shape=vshape(bf16) + vshape(f32),
        args=[do, *os_, *lses], scratch_shapes=[pltpu.VMEM((ow // DH, tm, DH), f32)] * (2 * NG + 1), sem=(PAR,), vmem=VMEM_BIG,
        comms=comms)


def _pad_rows(w, rows):
    return jnp.concatenate([w, jnp.zeros((rows - w.shape[0], w.shape[1]), w.dtype)], axis=0)


def kernel(x, c, positions, mod_w, mod_b, norm_mix_g, norm_ffn_g, conv_pw1_w, conv_pw1_b, conv_dw_w, conv_dw_b, conv_ln_g, conv_ln_b, conv_pw2_w, conv_pw2_b, kv_mod_w, kv_mod_b, kv_norm_g, w_kv, k_norm_g, w_q, q_norm_g, w_o, ffn_up_w, ffn_dw_w, ffn_dw_b, ffn_down_w, loss_target, m_mod_w, m_mod_b, m_norm_mix_g, m_norm_ffn_g, m_conv_pw1_w, m_conv_pw1_b, m_conv_dw_w, m_conv_dw_b, m_conv_ln_g, m_conv_ln_b, m_conv_pw2_w, m_conv_pw2_b, m_kv_mod_w, m_kv_mod_b, m_kv_norm_g, m_w_kv, m_k_norm_g, m_w_q, m_q_norm_g, m_w_o, m_ffn_up_w, m_ffn_dw_w, m_ffn_dw_b, m_ffn_down_w, v_mod_w, v_mod_b, v_norm_mix_g, v_norm_ffn_g, v_conv_pw1_w, v_conv_pw1_b, v_conv_dw_w, v_conv_dw_b, v_conv_ln_g, v_conv_ln_b, v_conv_pw2_w, v_conv_pw2_b, v_kv_mod_w, v_kv_mod_b, v_kv_norm_g, v_w_kv, v_k_norm_g, v_w_q, v_q_norm_g, v_w_o, v_ffn_up_w, v_ffn_dw_w, v_ffn_dw_b, v_ffn_down_w):
    S, Dm = x.shape[1], x.shape[2]
    F = ffn_dw_b.shape[1]
    QW = NG * HPG * DH
    OW = HPG * DH
    mx, my, mc = _me()
    me = 4 * mx + 2 * my + mc
    core = jnp.reshape(mc, (1,)).astype(jnp.int32)
    chip = jnp.reshape(2 * mx + my, (1,)).astype(jnp.int32)
    x0 = x.reshape(S, Dm)
    target = loss_target.reshape(S, Dm)

    c_all = _ag_small(c, "ag_c").reshape(NDEV, Dm)
    n_mod = mod_w.shape[2]
    n_kvm = kv_mod_w.shape[1]
    b0 = lax.dynamic_slice(mod_b, (0, me * n_mod), (1, n_mod))
    b1 = lax.dynamic_slice(mod_b, (1, me * n_mod), (1, n_mod))
    bk = lax.dynamic_slice(kv_mod_b.reshape(1, -1), (0, me * n_kvm), (1, n_kvm))
    m_part = jnp.concatenate([_modproj(c_all, mod_w, 0, b0, "modproj0"), _modproj(c_all, mod_w, 1, b1, "modproj1"),
                              _modproj(c_all, kv_mod_w[None], 0, bk, "modproj_kv")], axis=1)
    m_all = _ag_small(m_part, "ag_mod")
    m_mine = lax.dynamic_index_in_dim(m_all, me, axis=1, keepdims=False)
    mod0 = m_mine[:, :n_mod].reshape(6, Dm)
    mod1 = m_mine[:, n_mod:2 * n_mod].reshape(6, Dm)
    modkv = m_mine[:, 2 * n_mod:].reshape(2, Dm)
    row = lambda a, i: a[i:i + 1]

    as3 = lambda w: w if w.ndim == 3 else w[None]
    sh16 = lambda w: as3(w).astype(bf16)
    ag_pw1 = _comm_allgather(sh16(conv_pw1_w), 2)
    ag_pw2 = _comm_allgather(sh16(conv_pw2_w), 1)
    ag_up = [_comm_allgather(sh16(ffn_up_w[l]), 2) for l in range(2)]
    ag_down = [_comm_allgather(sh16(ffn_down_w[l]), 1) for l in range(2)]
    ag_kv = _comm_allgather(sh16(w_kv), 2)
    ag_q = _comm_allgather(sh16(w_q), 2)
    ag_o = _comm_allgather(sh16(w_o), 2)

    sp_flat = jnp.concatenate([conv_pw1_b.reshape(-1), conv_dw_b.reshape(-1), conv_ln_g.reshape(-1), conv_ln_b.reshape(-1),
                               conv_pw2_b.reshape(-1), conv_dw_w.reshape(-1), ffn_dw_w.reshape(-1)])
    sp_rows = -(-sp_flat.shape[0] // 1024) * 8
    sp_flat = jnp.concatenate([sp_flat, jnp.zeros((sp_rows * 128 - sp_flat.shape[0],), f32)]).reshape(sp_rows, 128)
    n1, nd = conv_pw1_b.shape[1], conv_dw_b.shape[1]
    nfw = ffn_dw_w.shape[2]
    sp = _ag_small(sp_flat, "ag_small_params").reshape(NDEV, -1)
    off = 0
    pw1_b = sp[:, off:off + n1].reshape(1, -1); off += n1
    dw_b = sp[:, off:off + nd].reshape(1, -1); off += nd
    ln_g = sp[:, off:off + nd].reshape(1, -1); off += nd
    ln_b = sp[:, off:off + nd].reshape(1, -1); off += nd
    pw2_b = sp[:, off:off + nd].reshape(1, -1); off += nd
    dw_w = jnp.transpose(sp[:, off:off + CONV_K * nd].reshape(NDEV, CONV_K, nd), (1, 0, 2)).reshape(CONV_K, -1); off += CONV_K * nd
    fdw_w = jnp.transpose(sp[:, off:off + 2 * FFN_K * nfw].reshape(NDEV, 2, FFN_K, nfw), (1, 2, 0, 3)).reshape(2, FFN_K, -1)
    dw_w32 = _pad_rows(dw_w, 32)

    tabs = _rope_tables(positions.reshape(S, 1), "rope_tables")

    def with_comms(res, comms):
        return res if comms else (res, [])

    def rs_d2d(dwb):
        return [_comm_rs_sibling(dwb)]

    def rs_add(dwb, couts, tag):
        return _chip_partial(dwb, couts[0][0], core, f"rs_add_{tag}")

    def rs_ici(part):
        return [_comm_rs_chips(part)]

    def ffn_forward(xin, l, modv, w_up, w_down, up_comms, down_comms):
        h2 = _mod_fwd(xin, row(norm_ffn_g, l), row(modv, 3), row(modv, 4), f"ffn{l}_mod")
        fw8 = _pad_rows(fdw_w[l], 8)
        (u2, act), c_up = with_comms(_up_gate(h2, w_up, fw8, row(ffn_dw_b, l), f"ffn{l}_up", comms=up_comms), up_comms)
        if w_down is None:
            w_down, c_up = c_up[0][0], c_up[1:]
        (xout, f), c_down = with_comms(
            _mm_nn(act, w_down, 0, name=f"ffn{l}_down", res=xin, gate=row(modv, 5), tk=F, tn=512, comms=down_comms), down_comms)
        return xout, (h2, u2, act, f, fw8, w_up, w_down), c_up, c_down

    def ffn_backward(dx, df, dgate, xin, l, modv, saved, dact_comms, gate_next):
        h2, u2, act, f, fw8, w_up, w_down = saved
        d_down = _mm_tn(act, df, name=f"ffn{l}_ddown", col_sharded=False)
        (du2, d_fw, d_fb), c1 = _dact_gate_bwd(df, w_down, u2, fw8, row(ffn_dw_b, l), f"ffn{l}_gatebwd",
                                               comms=rs_d2d(d_down) + list(dact_comms))
        part_down, c_dact = rs_add(d_down, c1, f"down{l}"), c1[1:]
        dh2, c2 = _mm_nt(du2, w_up, 0, name=f"ffn{l}_dh", out_dtype=f32, tko=1024, tn=F // 2, comms=rs_ici(part_down))
        d_up = _mm_tn(h2, du2, name=f"ffn{l}_dup", col_sharded=True)
        (dxin, dsh, dsc, dg, *below), c3 = _mod_bwd(dh2, xin, dx, row(norm_ffn_g, l), row(modv, 4), f"ffn{l}_mod_bwd",
                                                    comms=rs_d2d(d_up), gate_next=gate_next)
        part_up = rs_add(d_up, c3, f"up{l}")
        grads = dict(d_fw=d_fw[:FFN_K], d_fb=d_fb, dsh=dsh, dsc=dsc, dgate=dgate, dg=dg,
                     down=(part_down, c2[0][0]), part_up=part_up)
        return dxin, grads, c_dact, below

    h0, c = _mod_fwd(x0, row(norm_mix_g, 0), row(mod0, 0), row(mod0, 1), "l0_mod", comms=[ag_pw1])
    W_pw1 = c[0][0]
    u0, c = _mm_nn(h0, W_pw1, 0, name="l0_pw1", bias=pw1_b, comms=[ag_pw2, ag_q])
    W_pw2, W_q = c[0][0], c[1][0]
    (s0, cv0), c = _conv_fwd(u0, dw_w32, dw_b, ln_g, ln_b, "l0_conv", comms=[ag_up[0], ag_o])
    W_up0, W_o = c[0][0], c[1][0]
    x1, f0 = _mm_nn(s0, W_pw2, 0, name="l0_pw2", bias=pw2_b, res=x0, gate=row(mod0, 2))
    x2, ffn0_saved, c_up, c_down = ffn_forward(x1, 0, mod0, W_up0, None, [ag_down[0], ag_up[1]], [ag_kv])
    W_up1, W_kv = c_up[0][0], c_down[0][0]

    hkv = _mod_fwd(x2, kv_norm_g.reshape(1, -1), row(modkv, 0), row(modkv, 1), "kv_mod")
    kvraw, c = _mm_nn(hkv, W_kv, 0, name="kv_proj", comms=[ag_down[1]])
    W_down1 = c[0][0]
    kg = k_norm_g.reshape(1, -1)
    k_gv, v_dil = _qk_fwd(kvraw, kg, tabs, QW, True, "k_norm_rope")
    dilated = [gi for gi, r in enumerate(DILS) if r > 1]
    v_of = {gi: (kvraw, NG + gi) for gi, r in enumerate(DILS) if r == 1}
    v_of.update({gi: (v_dil[i], 0) for i, gi in enumerate(dilated)})
    h1 = _mod_fwd(x2, row(norm_mix_g, 1), row(mod1, 0), row(mod1, 1), "l1_mod")
    qraw = _mm_nn(h1, W_q, 0, name="q_proj")
    qg = q_norm_g.reshape(1, -1)
    q_gv, _ = _qk_fwd(qraw, qg, tabs, QW, False, "q_norm_rope")
    o_gs, lses = [], []
    for gi, r in enumerate(DILS):
        o_g, lse_g = _attn_fwd(q_gv[gi], k_gv[gi], *v_of[gi], r, f"attn_fwd{gi}")
        o_gs.append(o_g)
        lses.append(lse_g)
    o_mix = _combine_fwd(o_gs, lses, "attn_mix")
    x3, f1 = _mm_nn(o_mix, W_o, 0, name="o_proj", res=x2, gate=row(mod1, 2))
    x4, ffn1_saved, _, _ = ffn_forward(x3, 1, mod1, W_up1, W_down1, (), ())

    dx4, loss_blk, df_f1, dgate_f1, _ = _loss_grad(x4, target, ffn1_saved[3], row(mod1, 5), "loss")
    loss = lax.psum(loss_blk[0, 0], ("x", "y", "c"))

    red = {}
    dx3, gf1, _, (dy1, dgate_m1, _) = ffn_backward(dx4, df_f1, dgate_f1, x3, 1, mod1, ffn1_saved, (), (f1, row(mod1, 2)))
    do = _mm_nt(dy1, W_o, 0, name="o_proj_dx", out_dtype=f32, tko=1024, tn=Dm)
    d_wo = _mm_tn(o_mix, dy1, name="o_proj_dw", col_sharded=True)
    outs, c = _combine_bwd(do, o_gs, lses, "attn_mix_bwd", comms=rs_d2d(d_wo))
    part_wo = rs_add(d_wo, c, "wo")
    do_gs, corrs = outs[:NG], outs[NG:]
    dq_gs, dk_gs, dv_gs = [], [], []
    for gi, r in enumerate(DILS):
        cm = rs_ici(part_wo) if gi == 0 else ()
        dq_g, c = with_comms(_attn_bwd_q(q_gv[gi], k_gv[gi], *v_of[gi], do_gs[gi], lses[gi], corrs[gi], r, f"attn_bwd_q{gi}",
                                         comms=cm), cm)
        if gi == 0:
            red["w_o"] = (part_wo, c[0][0])
        dq_gs.append(dq_g)
        dk_g, dv_g = _attn_bwd_kv(q_gv[gi], k_gv[gi], *v_of[gi], do_gs[gi], lses[gi], corrs[gi], r, f"attn_bwd_kv{gi}")
        dk_gs.append(dk_g)
        dv_gs.append(dv_g)
    dqraw, d_qg = _qk_bwd(dq_gs, qraw, qg, tabs, QW, (), "q_norm_rope_bwd")
    dkvraw, d_kg = _qk_bwd(dk_gs, kvraw, kg, tabs, QW, tuple(dv_gs), "k_norm_rope_bwd")
    dh1 = _mm_nt(dqraw, W_q, 0, name="q_proj_dx", out_dtype=f32, tko=1024, tn=QW)
    d_wq = _mm_tn(h1, dqraw, name="q_proj_dw", col_sharded=True)
    dhkv, c = _mm_nt(dkvraw, W_kv, 0, name="kv_proj_dx", out_dtype=f32, tko=512, tn=2 * QW, comms=rs_d2d(d_wq))
    part_wq = rs_add(d_wq, c, "wq")
    d_wkv, c = _mm_tn(hkv, dkvraw, name="kv_proj_dw", col_sharded=True, comms=rs_ici(gf1["part_up"]))
    red["ffn_up_w1"] = (gf1["part_up"], c[0][0])
    (dx2a, dsh_m1, dsc_m1, dg_mix1), c = _mod_bwd(dh1, x2, dx3, row(norm_mix_g, 1), row(mod1, 1), "l1_mod_bwd",
                                                  comms=rs_ici(part_wq))
    red["w_q"] = (part_wq, c[0][0])
    (dx2, dsh_kv, dsc_kv, dg_kvn, df_f0, dgate_f0, _), c = _mod_bwd(
        dhkv, x2, dx2a, kv_norm_g.reshape(1, -1), row(modkv, 1), "kv_mod_bwd", comms=rs_d2d(d_wkv),
        gate_next=(ffn0_saved[3], row(mod0, 5)))
    part_wkv = rs_add(d_wkv, c, "wkv")

    dx1, gf0, c, (dy0, dgate_m0, d_pw2b) = ffn_backward(dx2, df_f0, dgate_f0, x1, 0, mod0, ffn0_saved, rs_ici(part_wkv),
                                                        (f0, row(mod0, 2)))
    red["w_kv"] = (part_wkv, c[0][0])
    ds0 = _mm_nt(dy0, W_pw2, 0, name="l0_pw2_dx", out_dtype=bf16, tko=1024, tn=Dm)
    d_pw2 = _mm_tn(s0, dy0, name="l0_pw2_dw", col_sharded=False)
    (dcv, d_lng, d_lnb, d_dwb, d_dww), c = _conv_bwd1(u0, cv0, ds0, dw_w32, dw_b, ln_g, ln_b, "l0_conv_bwd1",
                                                      comms=rs_ici(gf0["part_up"]))
    red["ffn_up_w0"] = (gf0["part_up"], c[0][0])
    (du0, d_pw1b), c = _conv_bwd2(dcv, u0, dw_w32, "l0_conv_bwd2", comms=rs_d2d(d_pw2))
    part_pw2 = rs_add(d_pw2, c, "pw2")
    d_pw1 = _mm_tn(h0, du0, name="l0_pw1_dw", col_sharded=True)
    dh0, c = _mm_nt(du0, W_pw1, 0, name="l0_pw1_dx", out_dtype=f32, tko=1024, tn=2 * Dm, comms=rs_ici(part_pw2) + rs_d2d(d_pw1))
    red["conv_pw2_w"] = (part_pw2, c[0][0])
    part_pw1 = rs_add(d_pw1, c[1:], "pw1")
    (grad_x, dsh_m0, dsc_m0, dg_mix0), c = _mod_bwd(dh0, x0, dx1, row(norm_mix_g, 0), row(mod0, 1), "l0_mod_bwd",
                                                    comms=rs_ici(part_pw1))
    red["conv_pw1_w"] = (part_pw1, c[0][0])
    red["ffn_down_w0"], red["ffn_down_w1"] = gf0["down"], gf1["down"]

    dm0 = [dsh_m0, dsc_m0, dgate_m0, gf0["dsh"], gf0["dsc"], gf0["dgate"]]
    dm1 = [dsh_m1, dsc_m1, dgate_m1, gf1["dsh"], gf1["dsc"], gf1["dgate"]]
    pieces = dm0 + dm1 + [dsh_kv, dsc_kv,
                          dg_mix0, dg_mix1, gf0["dg"], gf1["dg"], dg_kvn, d_kg, d_qg, gf0["d_fb"], gf1["d_fb"],
                          d_pw1b, d_dww[:CONV_K], d_dwb, d_lng, d_lnb, d_pw2b, gf0["d_fw"], gf1["d_fw"]]
    flat = jnp.concatenate([p.reshape(-1) for p in pieces])
    n_flat = flat.shape[0]
    n_rows = -(-n_flat // 1024) * 8
    flat = jnp.concatenate([flat, jnp.zeros((n_rows * 128 - n_flat,), f32)]).reshape(n_rows, 128)
    g_all = _ag_small(flat, "ag_small_grads")
    g_sum = _sum8(g_all, "sum_small_grads").reshape(-1)
    n_dm = 2 * 6 * Dm + 2 * Dm
    dm_all = g_all.reshape(NDEV, -1)[:, :n_dm]

    take_pos = [0]

    def take(shape):
        n = int(np.prod(shape))
        out = g_sum[take_pos[0]:take_pos[0] + n].reshape(shape)
        take_pos[0] += n
        return out

    g_mod_b = take((2, 6 * Dm))
    g_kv_mod_b = take((2 * Dm,))
    g_norm_mix0, g_norm_mix1 = take((Dm,)), take((Dm,))
    g_norm_ffn0, g_norm_ffn1 = take((Dm,)), take((Dm,))
    g_kv_norm = take((Dm,))
    g_k_norm = take((DH,))
    g_q_norm = take((1, DH))
    g_ffn_dw_b = take((2, F))
    shard = lambda full, n, axis: lax.dynamic_slice_in_dim(full, me * n, n, axis)
    g_pw1_b = shard(take((1, 2 * Dm)), n1, 1)
    g_dw_w = shard(take((1, CONV_K, Dm)), nd, 2)
    g_dw_b = shard(take((1, Dm)), nd, 1)
    g_ln_g = shard(take((1, Dm)), nd, 1)
    g_ln_b = shard(take((1, Dm)), nd, 1)
    g_pw2_b = shard(take((1, Dm)), nd, 1)
    g_ffn_dw_w = shard(jnp.stack([take((FFN_K, F)), take((FFN_K, F))]), nfw, 2)
    g_norm_mix = jnp.stack([g_norm_mix0, g_norm_mix1])
    g_norm_ffn = jnp.stack([g_norm_ffn0, g_norm_ffn1])

    small = [("mod_b", mod_b, m_mod_b, v_mod_b, g_mod_b), ("norm_mix_g", norm_mix_g, m_norm_mix_g, v_norm_mix_g, g_norm_mix),
             ("norm_ffn_g", norm_ffn_g, m_norm_ffn_g, v_norm_ffn_g, g_norm_ffn),
             ("conv_pw1_b", conv_pw1_b, m_conv_pw1_b, v_conv_pw1_b, g_pw1_b),
             ("conv_dw_w", conv_dw_w, m_conv_dw_w, v_conv_dw_w, g_dw_w), ("conv_dw_b", conv_dw_b, m_conv_dw_b, v_conv_dw_b, g_dw_b),
             ("conv_ln_g", conv_ln_g, m_conv_ln_g, v_conv_ln_g, g_ln_g), ("conv_ln_b", conv_ln_b, m_conv_ln_b, v_conv_ln_b, g_ln_b),
             ("conv_pw2_b", conv_pw2_b, m_conv_pw2_b, v_conv_pw2_b, g_pw2_b),
             ("kv_mod_b", kv_mod_b, m_kv_mod_b, v_kv_mod_b, g_kv_mod_b), ("kv_norm_g", kv_norm_g, m_kv_norm_g, v_kv_norm_g, g_kv_norm),
             ("k_norm_g", k_norm_g, m_k_norm_g, v_k_norm_g, g_k_norm), ("q_norm_g", q_norm_g, m_q_norm_g, v_q_norm_g, g_q_norm),
             ("ffn_dw_w", ffn_dw_w, m_ffn_dw_w, v_ffn_dw_w, g_ffn_dw_w), ("ffn_dw_b", ffn_dw_b, m_ffn_dw_b, v_ffn_dw_b, g_ffn_dw_b)]
    n_small = sum(int(np.prod(s[1].shape)) for s in small)
    rows_small = -(-n_small // 1024) * 8

    def pack(idx):
        fl = jnp.concatenate([s[idx].reshape(-1) for s in small])
        return jnp.concatenate([fl, jnp.ones((rows_small * 128 - n_small,), f32)]).reshape(rows_small, 128)

    sd, sm, sv = _adamw_plain(pack(1), pack(2), pack(3), pack(4), "adamw_small")
    res = {}
    pos = 0
    for name, w, _, _, g in small:
        n = int(np.prod(w.shape))
        cut = lambda a: a.reshape(-1)[pos:pos + n].reshape(w.shape)
        res[name] = (g.reshape(w.shape), cut(sd), cut(sm), cut(sv))
        pos += n

    c_all_t = jnp.transpose(c_all)

    def mod_update(w, m, v, l, prev, dm_cols, tag):
        g = _modgrad(c_all_t, dm_cols, f"modgrad_{tag}")
        return _adamw_reduced(w, m, v, g, None, l, prev, f"adamw_{tag}")

    prev = None
    for l in (1, 0):
        cols = lax.dynamic_slice_in_dim(dm_all[:, l * 6 * Dm:(l + 1) * 6 * Dm], me * n_mod, n_mod, 1)
        prev = mod_update(mod_w, m_mod_w, v_mod_w, l, prev, cols, f"mod_w{l}")
    res["mod_w"] = tuple(prev)
    cols = lax.dynamic_slice_in_dim(dm_all[:, 12 * Dm:], me * n_kvm, n_kvm, 1)
    res["kv_mod_w"] = tuple(o.reshape(kv_mod_w.shape)
                            for o in mod_update(kv_mod_w[None], m_kv_mod_w[None], v_kv_mod_w[None], 0, None, cols, "kv_mod_w"))

    def mine(part):
        return lax.dynamic_index_in_dim(part, chip[0], 0, keepdims=False)

    def big(key, w, m, v, l, prev, tag, comms=()):
        part, r2 = red[key]
        return _adamw_reduced(as3(w), as3(m), as3(v), mine(part), r2, l, prev, f"adamw_{tag}", comms=comms)

    up1 = big("ffn_up_w1", ffn_up_w, m_ffn_up_w, v_ffn_up_w, 1, None, "up1")
    res["ffn_up_w"] = tuple(big("ffn_up_w0", ffn_up_w, m_ffn_up_w, v_ffn_up_w, 0, up1, "up0"))
    down1 = big("ffn_down_w1", ffn_down_w, m_ffn_down_w, v_ffn_down_w, 1, None, "down1")
    res["ffn_down_w"] = tuple(big("ffn_down_w0", ffn_down_w, m_ffn_down_w, v_ffn_down_w, 0, down1, "down0"))
    for key, w, m, v in (("conv_pw1_w", conv_pw1_w, m_conv_pw1_w, v_conv_pw1_w), ("conv_pw2_w", conv_pw2_w, m_conv_pw2_w, v_conv_pw2_w),
                         ("w_kv", w_kv, m_w_kv, v_w_kv), ("w_q", w_q, m_w_q, v_w_q), ("w_o", w_o, m_w_o, v_w_o)):
        res[key] = tuple(o.reshape(w.shape) for o in big(key, w, m, v, 0, None, key))

    order = ["mod_w", "mod_b", "norm_mix_g", "norm_ffn_g", "conv_pw1_w", "conv_pw1_b", "conv_dw_w", "conv_dw_b", "conv_ln_g",
             "conv_ln_b", "conv_pw2_w", "conv_pw2_b", "kv_mod_w", "kv_mod_b", "kv_norm_g", "w_kv", "k_norm_g", "w_q", "q_norm_g",
             "w_o", "ffn_up_w", "ffn_dw_w", "ffn_dw_b", "ffn_down_w"]
    out = [loss, grad_x.reshape(x.shape)]
    for i in range(4):
        out += [res[n][i] for n in order]
    return tuple(out)
```

```python
import functools
import math

import numpy as np
import jax
import jax.numpy as jnp
from jax import lax
from jax.experimental import pallas as pl
from jax.experimental.pallas import tpu as pltpu

f32 = jnp.float32
bf16 = jnp.bfloat16

D = 2048
SEQ = 8192
FF = 5632
CONV_K = 31
FFN_K = 3
HPG = 8
DH = 128
NG = 3
DILS = (1, 4, 16)
BLK = 128
ROT = 32
THETA = 500000.0
EPS = 1e-6
NEG = -1e30
NDEV = 8
HALO = 32
FHALO = 16
FFN_TM = 1024

LR, B1, B2, AEPS, WD, STEP = 0.001, 0.9, 0.999, 1e-08, 0.01, 10

VMEM_BIG = 56 * 1024 * 1024

ARB = "arbitrary"
PAR = "parallel"
MESH = pl.DeviceIdType.MESH


def _cp(sem, vmem=None):
    return pltpu.CompilerParams(dimension_semantics=sem, vmem_limit_bytes=vmem)


def _tile(n, pref, mult=128):
    if n <= pref:
        return n
    t = (pref // mult) * mult
    while t >= mult:
        if n % t == 0:
            return t
        t -= mult
    return n


def _sigmoid(x):
    return 1.0 / (1.0 + jnp.exp(-x))


def _me():
    return lax.axis_index("x"), lax.axis_index("y"), lax.axis_index("c")


class _Comm:
    def __init__(self, arrays, out_shapes, sems, start, finish, mid=None):
        self.arrays, self.out_shapes, self.sems, self.start, self.finish = arrays, out_shapes, sems, start, finish
        self.mid = mid


def _pcall(body, *, name, grid, in_specs, out_specs, out_shape, args, scratch_shapes=(), sem=None, vmem=None, comms=(),
           aliases=None):
    aliases = aliases or {}
    if not comms:
        return pl.pallas_call(body, name=name, grid=grid, in_specs=in_specs, out_specs=out_specs, out_shape=out_shape,
                              scratch_shapes=list(scratch_shapes), input_output_aliases=aliases,
                              compiler_params=_cp(sem, vmem))(*args)
    single = not isinstance(out_shape, (list, tuple))
    outs_shape = [out_shape] if single else list(out_shape)
    outs_spec = [out_specs] if single else list(out_specs)
    n_in, n_out, n_scr = len(args), len(outs_shape), len(scratch_shapes)
    c_in = [a for cm in comms for a in cm.arrays]
    c_out = [s for cm in comms for s in cm.out_shapes]
    c_scr = [s for cm in comms for s in cm.sems]
    total = int(np.prod(grid))
    late = total - 1 - max(1, total // 8) if total >= 8 else None

    def split(refs, counts):
        out, pos = [], 0
        for n in counts:
            out.append(refs[pos:pos + n])
            pos += n
        return out

    def wrapped(*refs):
        ins, cins, outs, couts, scr, cscr = split(refs, [n_in, len(c_in), n_out, len(c_out), n_scr, len(c_scr)])
        ids = [pl.program_id(a) for a in range(len(grid))]
        first = functools.reduce(jnp.logical_and, [i == 0 for i in ids])
        last = functools.reduce(jnp.logical_and, [i == g - 1 for i, g in zip(ids, grid)])
        per_in = split(cins, [len(cm.arrays) for cm in comms])
        per_out = split(couts, [len(cm.out_shapes) for cm in comms])
        per_sem = split(cscr, [len(cm.sems) for cm in comms])

        @pl.when(first)
        def _():
            for cm, a, b, s in zip(comms, per_in, per_out, per_sem):
                cm.start(a, b, s)

        if late is not None:
            step = functools.reduce(lambda acc, ig: acc * ig[1] + ig[0], zip(ids, grid), 0)

            @pl.when(step == late)
            def _():
                for cm, a, b, s in zip(comms, per_in, per_out, per_sem):
                    if cm.mid is not None:
                        cm.mid(a, b, s)

        body(*ins, *outs, *scr)

        @pl.when(last)
        def _():
            for cm, a, b, s in zip(comms, per_in, per_out, per_sem):
                if late is None and cm.mid is not None:
                    cm.mid(a, b, s)
                cm.finish(a, b, s)

    hbm = pl.BlockSpec(memory_space=pl.ANY)
    res = pl.pallas_call(
        wrapped, name=name, grid=grid, in_specs=list(in_specs) + [hbm] * len(c_in),
        out_specs=outs_spec + [hbm] * len(c_out), out_shape=outs_shape + c_out,
        scratch_shapes=list(scratch_shapes) + c_scr, input_output_aliases=aliases,
        compiler_params=_cp((ARB,) * len(grid), vmem))(*args, *c_in)
    main = res[0] if single else list(res[:n_out])
    return main, split(list(res[n_out:]), [len(cm.out_shapes) for cm in comms])


def _comm_allgather(w, axis):
    n = w.shape[axis]
    out_shape = list(w.shape)
    out_shape[axis] = NDEV * n

    def parts(ins, outs, sems):
        x_ref, out_ref = ins[0], outs[0]
        send_sems, recv_sems, local_sem = sems
        mx, my, mc = _me()
        chips = [(1 - mx, my), (mx, 1 - my), (1 - mx, 1 - my)]

        def blk(px, py, pc):
            start = pl.multiple_of((4 * px + 2 * py + pc) * n, n)
            if axis == 1:
                return out_ref.at[:, pl.ds(start, n), :]
            return out_ref.at[:, :, pl.ds(start, n)]

        def copy(k, block, to, src=None):
            return pltpu.make_async_remote_copy(
                src_ref=blk(*block) if src is None else src, dst_ref=blk(*block),
                send_sem=send_sems.at[k], recv_sem=recv_sems.at[k], device_id=to, device_id_type=MESH)

        me, sibling = (mx, my, mc), (mx, my, 1 - mc)
        mine = pltpu.make_async_copy(x_ref, blk(*me), local_sem)
        first = [copy(0, me, sibling, src=x_ref)] + [copy(1 + j, me, (*chip, mc), src=x_ref) for j, chip in enumerate(chips)]
        passed = [copy(4 + j, (*chip, mc), sibling) for j, chip in enumerate(chips)]
        return me, sibling, chips, mc, copy, mine, first, passed

    def start(ins, outs, sems):
        *_, mine, first, _ = parts(ins, outs, sems)
        mine.start()
        for cp in first:
            cp.start()

    def mid(ins, outs, sems):
        me, sibling, chips, mc, copy, mine, first, passed = parts(ins, outs, sems)
        for j, chip in enumerate(chips):
            copy(1 + j, (*chip, mc), me).wait_recv()
            passed[j].start()

    def finish(ins, outs, sems):
        me, sibling, chips, mc, copy, mine, first, passed = parts(ins, outs, sems)
        copy(0, sibling, me).wait_recv()
        for j, chip in enumerate(chips):
            copy(4 + j, (*chip, 1 - mc), me).wait_recv()
        for cp in first + passed:
            cp.wait_send()
        mine.wait()

    return _Comm([w], [jax.ShapeDtypeStruct(tuple(out_shape), w.dtype)],
                 [pltpu.SemaphoreType.DMA((7,)), pltpu.SemaphoreType.DMA((7,)), pltpu.SemaphoreType.DMA], start, finish, mid)


def _comm_rs_sibling(dwb):
    def copies(ins, outs, sems):
        mx, my, mc = _me()
        return [pltpu.make_async_remote_copy(
            src_ref=ins[0].at[2 * p + (1 - mc)], dst_ref=outs[0].at[p], send_sem=sems[0].at[p], recv_sem=sems[1].at[p],
            device_id=(mx, my, 1 - mc), device_id_type=MESH) for p in range(4)]

    def start(ins, outs, sems):
        for cp in copies(ins, outs, sems):
            cp.start()

    def finish(ins, outs, sems):
        cps = copies(ins, outs, sems)
        for cp in cps:
            cp.wait_recv()
        for cp in cps:
            cp.wait_send()

    return _Comm([dwb], [jax.ShapeDtypeStruct((4,) + dwb.shape[1:], dwb.dtype)],
                 [pltpu.SemaphoreType.DMA((4,)), pltpu.SemaphoreType.DMA((4,))], start, finish)


def _comm_rs_chips(part):
    def copies(ins, outs, sems):
        mx, my, mc = _me()
        chips = [(1 - mx, my), (mx, 1 - my), (1 - mx, 1 - my)]
        return [pltpu.make_async_remote_copy(
            src_ref=ins[0].at[2 * px + py], dst_ref=outs[0].at[k], send_sem=sems[0].at[k], recv_sem=sems[1].at[k],
            device_id=(px, py, mc), device_id_type=MESH) for k, (px, py) in enumerate(chips)]

    def start(ins, outs, sems):
        for cp in copies(ins, outs, sems):
            cp.start()

    def finish(ins, outs, sems):
        cps = copies(ins, outs, sems)
        for cp in cps:
            cp.wait_recv()
        for cp in cps:
            cp.wait_send()

    return _Comm([part], [jax.ShapeDtypeStruct((3,) + part.shape[1:], part.dtype)],
                 [pltpu.SemaphoreType.DMA((3,)), pltpu.SemaphoreType.DMA((3,))], start, finish)


def _ag_small(x, name):
    r, c = x.shape

    def body(x_ref, out_ref, send_sems, recv_sems):
        mx, my, mc = _me()
        mine = 4 * mx + 2 * my + mc
        out_ref[mine] = x_ref[...]
        copies = []
        for k in range(1, NDEV):
            px = 1 - mx if (k >> 2) & 1 else mx
            py = 1 - my if (k >> 1) & 1 else my
            pc = 1 - mc if k & 1 else mc
            cp = pltpu.make_async_remote_copy(
                src_ref=x_ref, dst_ref=out_ref.at[mine], send_sem=send_sems.at[k - 1], recv_sem=recv_sems.at[k - 1],
                device_id=(px, py, pc), device_id_type=MESH)
            cp.start()
            copies.append((cp, 4 * px + 2 * py + pc))
        for k, (cp, peer) in enumerate(copies):
            pltpu.make_async_remote_copy(
                src_ref=x_ref, dst_ref=out_ref.at[peer], send_sem=send_sems.at[k], recv_sem=recv_sems.at[k],
                device_id=(mx, my, mc), device_id_type=MESH).wait_recv()
        for cp, _ in copies:
            cp.wait_send()

    return pl.pallas_call(
        body, name=name,
        out_shape=jax.ShapeDtypeStruct((NDEV, r, c), x.dtype),
        in_specs=[pl.BlockSpec(memory_space=pltpu.VMEM)],
        out_specs=pl.BlockSpec(memory_space=pltpu.VMEM),
        scratch_shapes=[pltpu.SemaphoreType.DMA((NDEV - 1,)), pltpu.SemaphoreType.DMA((NDEV - 1,))],
    )(x)


def _chip_partial(dwb, r1, core, name):
    _, A, B = dwb.shape
    ta = _tile(A, 512, 16)

    def body(c_ref, a_ref, b_ref, o_ref):
        o_ref[...] = (a_ref[...].astype(f32) + b_ref[...].astype(f32)).astype(o_ref.dtype)

    grid_spec = pltpu.PrefetchScalarGridSpec(
        num_scalar_prefetch=1, grid=(4, A // ta),
        in_specs=[pl.BlockSpec((None, ta, B), lambda p, i, c: (2 * p + c[0], i, 0)),
                  pl.BlockSpec((None, ta, B), lambda p, i, c: (p, i, 0))],
        out_specs=pl.BlockSpec((None, ta, B), lambda p, i, c: (p, i, 0)))
    return pl.pallas_call(body, name=name, grid_spec=grid_spec,
                          out_shape=jax.ShapeDtypeStruct((4, A, B), dwb.dtype),
                          compiler_params=_cp((PAR, PAR)))(core, dwb, r1)


def _adam_math(w, g, m, v):
    m2 = B1 * m + (1.0 - B1) * g
    v2 = B2 * v + (1.0 - B2) * (g * g)
    m_hat = m2 / (1.0 - B1 ** STEP)
    v_hat = v2 / (1.0 - B2 ** STEP)
    delta = -LR * (m_hat / (jnp.sqrt(v_hat) + AEPS) + WD * w)
    return delta, m2, v2


def _adamw_reduced(w, m, v, mine, r2, l, prev, name, comms=()):
    L, A, B = w.shape
    ta = _tile(A, 256, 8)
    summed = r2 is None

    def body(w_ref, m_ref, v_ref, p_ref, *rest):
        g_out, d_out, m_out, v_out = rest[-4:]
        g = p_ref[...].astype(f32)
        if not summed:
            r_ref = rest[0]
            g = ((g + r_ref[0].astype(f32)) + r_ref[1].astype(f32)) + r_ref[2].astype(f32)
        d, m2, v2 = _adam_math(w_ref[...], g, m_ref[...], v_ref[...])
        g_out[...] = g
        d_out[...] = d
        m_out[...] = m2
        v_out[...] = v2

    wspec = pl.BlockSpec((None, ta, B), lambda i: (l, i, 0))
    in_specs = [wspec, wspec, wspec, pl.BlockSpec((ta, B), lambda i: (i, 0))]
    args = [w, m, v, mine]
    if not summed:
        in_specs.append(pl.BlockSpec((3, ta, B), lambda i: (0, i, 0)))
        args.append(r2)
    aliases = {}
    if prev is not None:
        aliases = {len(args) + i: i for i in range(4)}
        in_specs += [pl.BlockSpec(memory_space=pl.ANY)] * 4
        args += list(prev)
    shp = jax.ShapeDtypeStruct((L, A, B), f32)
    return _pcall(body, name=name, grid=(A // ta,), in_specs=in_specs, out_specs=[wspec] * 4, out_shape=[shp] * 4,
                  args=args, sem=(PAR,), comms=comms, aliases=aliases)


def _adamw_plain(w, m, v, g, name):
    A, B = w.shape
    ta = _tile(A, 256, 8)

    def body(w_ref, m_ref, v_ref, g_ref, d_out, m_out, v_out):
        d, m2, v2 = _adam_math(w_ref[...], g_ref[...], m_ref[...], v_ref[...])
        d_out[...] = d
        m_out[...] = m2
        v_out[...] = v2

    spec = pl.BlockSpec((ta, B), lambda i: (i, 0))
    shp = jax.ShapeDtypeStruct((A, B), f32)
    return pl.pallas_call(body, name=name, grid=(A // ta,), in_specs=[spec] * 4, out_specs=[spec] * 3,
                          out_shape=[shp, shp, shp], compiler_params=_cp((PAR,)))(w, m, v, g)


def _sum8(g, name):
    _, R, C = g.shape

    def body(g_ref, o_ref):
        acc = g_ref[0]
        for j in range(1, NDEV):
            acc = acc + g_ref[j]
        o_ref[...] = acc

    return pl.pallas_call(body, name=name, out_shape=jax.ShapeDtypeStruct((R, C), f32))(g)


def _modproj(c_all, w, l, bias, name):
    _, K, N = w.shape
    tn = _tile(N, 512)

    def body(c_ref, w_ref, b_ref, o_ref):
        cc = c_ref[...]
        sc = (cc * _sigmoid(cc)).astype(bf16)
        o_ref[...] = jnp.dot(sc, w_ref[...].astype(bf16), preferred_element_type=f32) + b_ref[...]

    return pl.pallas_call(
        body, name=name, grid=(N // tn,),
        in_specs=[pl.BlockSpec((NDEV, K), lambda j: (0, 0)), pl.BlockSpec((None, K, tn), lambda j: (l, 0, j)),
                  pl.BlockSpec((1, tn), lambda j: (0, j))],
        out_specs=pl.BlockSpec((NDEV, tn), lambda j: (0, j)),
        out_shape=jax.ShapeDtypeStruct((NDEV, N), f32), compiler_params=_cp((PAR,)))(c_all, w, bias)


def _modgrad(c_all_t, dm, name):
    K = c_all_t.shape[0]
    N = dm.shape[1]
    tn = _tile(N, 512)

    def body(c_ref, d_ref, o_ref):
        cc = c_ref[...]
        sc = cc * _sigmoid(cc)
        dmv = d_ref[...]
        acc = sc[:, 0:1] * dmv[0:1, :]
        for b in range(1, NDEV):
            acc = acc + sc[:, b:b + 1] * dmv[b:b + 1, :]
        o_ref[...] = acc

    return pl.pallas_call(
        body, name=name, grid=(N // tn,),
        in_specs=[pl.BlockSpec((K, NDEV), lambda j: (0, 0)), pl.BlockSpec((NDEV, tn), lambda j: (0, j))],
        out_specs=pl.BlockSpec((K, tn), lambda j: (0, j)),
        out_shape=jax.ShapeDtypeStruct((K, N), f32), compiler_params=_cp((PAR,)))(c_all_t, dm)


def _mm_nn(a, w, l, *, name, out_dtype=bf16, bias=None, res=None, gate=None, tm=1024, tn=1024, tk=2048, comms=()):
    M, K = a.shape
    N = w.shape[2]
    tm, tn, tk = _tile(M, tm, 8), _tile(N, tn), _tile(K, tk)
    nk = K // tk
    epi = res is not None

    def body(*refs):
        it = iter(refs)
        a_ref, w_ref = next(it), next(it)
        b_ref = next(it) if bias is not None else None
        r_ref = next(it) if epi else None
        g_ref = next(it) if epi else None
        o_ref = next(it)
        f_ref = next(it) if epi else None

        def finish(y):
            if b_ref is not None:
                y = y + b_ref[...]
            if epi:
                f_ref[...] = y.astype(f_ref.dtype)
                o_ref[...] = r_ref[...] + g_ref[...] * y
            else:
                o_ref[...] = y.astype(o_ref.dtype)

        if nk == 1:
            finish(jnp.dot(a_ref[...], w_ref[...], preferred_element_type=f32))
            return
        acc = next(it)
        k = pl.program_id(2)

        @pl.when(k == 0)
        def _():
            acc[...] = jnp.zeros_like(acc)

        acc[...] += jnp.dot(a_ref[...], w_ref[...], preferred_element_type=f32)

        @pl.when(k == nk - 1)
        def _():
            finish(acc[...])

    in_specs = [pl.BlockSpec((tm, tk), lambda i, j, k: (i, k)), pl.BlockSpec((None, tk, tn), lambda i, j, k: (l, k, j))]
    args = [a, w]
    if bias is not None:
        in_specs.append(pl.BlockSpec((1, tn), lambda i, j, k: (0, j)))
        args.append(bias)
    ospec = pl.BlockSpec((tm, tn), lambda i, j, k: (i, j))
    if epi:
        in_specs += [ospec, pl.BlockSpec((1, tn), lambda i, j, k: (0, j))]
        args += [res, gate]
        out_shape = [jax.ShapeDtypeStruct((M, N), f32), jax.ShapeDtypeStruct((M, N), bf16)]
        out_specs = [ospec, ospec]
    else:
        out_shape = jax.ShapeDtypeStruct((M, N), out_dtype)
        out_specs = ospec
    return _pcall(body, name=name, grid=(M // tm, N // tn, nk), in_specs=in_specs, out_specs=out_specs, out_shape=out_shape,
                  args=args, scratch_shapes=[pltpu.VMEM((tm, tn), f32)] if nk > 1 else [], sem=(PAR, PAR, ARB), vmem=VMEM_BIG,
                  comms=comms)


def _mm_nt(a, w, l, *, name, out_dtype, tm=1024, tko=2048, tn=1024, comms=()):
    planes = a.ndim == 3
    M = a.shape[-2]
    K, N = w.shape[1], w.shape[2]
    npl = a.shape[-1]
    tm, tko = _tile(M, tm, 8), _tile(K, tko)
    tn = _tile(npl, tn)
    nn = N // tn
    per_plane = npl // tn

    def body(a_ref, w_ref, o_ref, *scratch):
        if nn == 1:
            o_ref[...] = _dot_nt(a_ref[...], w_ref[...]).astype(o_ref.dtype)
            return
        acc = scratch[0]
        k = pl.program_id(2)

        @pl.when(k == 0)
        def _():
            acc[...] = jnp.zeros_like(acc)

        acc[...] += _dot_nt(a_ref[...], w_ref[...])

        @pl.when(k == nn - 1)
        def _():
            o_ref[...] = acc[...].astype(o_ref.dtype)

    if planes:
        a_spec = pl.BlockSpec((None, tm, tn), lambda i, j, k: (k // per_plane, i, k % per_plane))
    else:
        a_spec = pl.BlockSpec((tm, tn), lambda i, j, k: (i, k))
    return _pcall(body, name=name, grid=(M // tm, K // tko, nn),
                  in_specs=[a_spec, pl.BlockSpec((None, tko, tn), lambda i, j, k: (l, j, k))],
                  out_specs=pl.BlockSpec((tm, tko), lambda i, j, k: (i, j)),
                  out_shape=jax.ShapeDtypeStruct((M, K), out_dtype), args=[a, w],
                  scratch_shapes=[pltpu.VMEM((tm, tko), f32)] if nn > 1 else [], sem=(PAR, PAR, ARB), vmem=VMEM_BIG,
                  comms=comms)


def _mm_tn(a, b, *, name, col_sharded, comms=()):
    planes = b.ndim == 3
    S, K = a.shape
    N = b.shape[-1] * (2 if planes else 1)
    if col_sharded:
        tn, tk, ts = N // NDEV, _tile(K, 1024), _tile(S, 2048, 16)
    else:
        tn, tk, ts = N, _tile(K, 1408), _tile(S, 1024, 16)
    ns_steps = S // ts
    per_plane = (b.shape[-1] // tn) if planes else 0

    def body(a_ref, b_ref, o_ref, acc):
        s = pl.program_id(2)

        @pl.when(s == 0)
        def _():
            acc[...] = jnp.zeros_like(acc)

        acc[...] += lax.dot_general(a_ref[...], b_ref[...], (((0,), (0,)), ((), ())), preferred_element_type=f32)

        @pl.when(s == ns_steps - 1)
        def _():
            o_ref[...] = acc[...].astype(o_ref.dtype)

    if planes:
        b_spec = pl.BlockSpec((None, ts, tn), lambda k, n, s: (n // per_plane, s, n % per_plane))
    else:
        b_spec = pl.BlockSpec((ts, tn), lambda k, n, s: (s, n))
    if col_sharded:
        out_shape = jax.ShapeDtypeStruct((NDEV, K, tn), bf16)
        out_spec = pl.BlockSpec((None, tk, tn), lambda k, n, s: (n, k, 0))
    else:
        out_shape = jax.ShapeDtypeStruct((K, N), bf16)
        out_spec = pl.BlockSpec((tk, tn), lambda k, n, s: (k, n))
    res = _pcall(body, name=name, grid=(K // tk, N // tn, ns_steps),
                 in_specs=[pl.BlockSpec((ts, tk), lambda k, n, s: (s, k)), b_spec],
                 out_specs=out_spec, out_shape=out_shape, args=[a, b],
                 scratch_shapes=[pltpu.VMEM((tk, tn), f32)], sem=(PAR, PAR, ARB), vmem=VMEM_BIG, comms=comms)
    out, couts = res if comms else (res, None)
    if not col_sharded:
        out = out.reshape(NDEV, K // NDEV, N)
    return (out, couts) if comms else out


def _acc_spec(w, rows=1):
    return pl.BlockSpec((rows, w), lambda i: (0, 0))


def _mod_fwd(x, g, sh, sc, name, comms=()):
    S, W = x.shape
    tm = _tile(S, 256, 8)

    def body(x_ref, g_ref, sh_ref, sc_ref, h_ref):
        xv = x_ref[...]
        r = lax.rsqrt(jnp.mean(xv * xv, axis=-1, keepdims=True) + EPS)
        h_ref[...] = ((xv * r) * g_ref[...] * (1.0 + sc_ref[...]) + sh_ref[...]).astype(h_ref.dtype)

    row = pl.BlockSpec((tm, W), lambda i: (i, 0))
    return _pcall(body, name=name, grid=(S // tm,), in_specs=[row, _acc_spec(W), _acc_spec(W), _acc_spec(W)],
                  out_specs=row, out_shape=jax.ShapeDtypeStruct((S, W), bf16), args=[x, g, sh, sc], sem=(PAR,), comms=comms)


def _gate_tail(d, f_ref, gate_ref, df_ref, dgate_ref, sdf_ref, first):
    @pl.when(first)
    def _():
        dgate_ref[...] = jnp.zeros_like(dgate_ref)
        sdf_ref[...] = jnp.zeros_like(sdf_ref)

    df = gate_ref[...] * d
    df_ref[...] = df.astype(df_ref.dtype)
    dgate_ref[...] += jnp.sum(d * f_ref[...].astype(f32), axis=0, keepdims=True)
    sdf_ref[...] += jnp.sum(df, axis=0, keepdims=True)


def _mod_bwd(dh, x, dx_in, g, sc, name, comms=(), gate_next=None):
    S, W = x.shape
    tm = _tile(S, 256, 16)
    nt = S // tm
    fused = gate_next is not None

    def body(dh_ref, x_ref, dxi_ref, g_ref, sc_ref, *rest):
        if fused:
            f_ref, gate_ref, dx_ref, dsh_ref, dsc_ref, dg_ref, df_ref, dgate_ref, sdf_ref = rest
        else:
            dx_ref, dsh_ref, dsc_ref, dg_ref = rest
        i = pl.program_id(0)

        @pl.when(i == 0)
        def _():
            dsh_ref[...] = jnp.zeros_like(dsh_ref)
            dsc_ref[...] = jnp.zeros_like(dsc_ref)

        xv = x_ref[...]
        dh = dh_ref[...].astype(f32)
        r = lax.rsqrt(jnp.mean(xv * xv, axis=-1, keepdims=True) + EPS)
        n = xv * r
        dn = dh * (g_ref[...] * (1.0 + sc_ref[...]))
        dx = dxi_ref[...] + r * (dn - n * jnp.mean(dn * n, axis=-1, keepdims=True))
        dx_ref[...] = dx
        dsh_ref[...] += jnp.sum(dh, axis=0, keepdims=True)
        dsc_ref[...] += jnp.sum(dh * n, axis=0, keepdims=True)
        if fused:
            _gate_tail(dx, f_ref, gate_ref, df_ref, dgate_ref, sdf_ref, i == 0)

        @pl.when(i == nt - 1)
        def _():
            a2 = dsc_ref[...]
            dg_ref[...] = a2 * (1.0 + sc_ref[...])
            dsc_ref[...] = a2 * g_ref[...]

    row = pl.BlockSpec((tm, W), lambda i: (i, 0))
    vec = jax.ShapeDtypeStruct((1, W), f32)
    in_specs, args = [row, row, row, _acc_spec(W), _acc_spec(W)], [dh, x, dx_in, g, sc]
    out_specs = [row, _acc_spec(W), _acc_spec(W), _acc_spec(W)]
    out_shape = [jax.ShapeDtypeStruct((S, W), f32), vec, vec, vec]
    if fused:
        in_specs, args = in_specs + [row, _acc_spec(W)], args + list(gate_next)
        out_specs = out_specs + [row, _acc_spec(W), _acc_spec(W)]
        out_shape = out_shape + [jax.ShapeDtypeStruct((S, W), bf16), vec, vec]
    return _pcall(body, name=name, grid=(nt,), in_specs=in_specs, out_specs=out_specs, out_shape=out_shape, args=args,
                  sem=(ARB,), comms=comms)


def _loss_grad(y, target, f, gate, name):
    S, W = y.shape
    tm = _tile(S, 256, 16)

    def body(y_ref, t_ref, f_ref, gate_ref, dy_ref, l_ref, df_ref, dgate_ref, sdf_ref):
        i = pl.program_id(0)

        @pl.when(i == 0)
        def _():
            l_ref[...] = jnp.zeros_like(l_ref)

        e = y_ref[...] - t_ref[...]
        dy = e * (1.0 / W)
        dy_ref[...] = dy
        l_ref[...] += 0.5 * jnp.sum(jnp.mean(e * e, axis=-1, keepdims=True))
        _gate_tail(dy, f_ref, gate_ref, df_ref, dgate_ref, sdf_ref, i == 0)

    row = pl.BlockSpec((tm, W), lambda i: (i, 0))
    vec = jax.ShapeDtypeStruct((1, W), f32)
    return pl.pallas_call(
        body, name=name, grid=(S // tm,), in_specs=[row, row, row, _acc_spec(W)],
        out_specs=[row, pl.BlockSpec((8, 128), lambda i: (0, 0)), row, _acc_spec(W), _acc_spec(W)],
        out_shape=[jax.ShapeDtypeStruct((S, W), f32), jax.ShapeDtypeStruct((8, 128), f32), jax.ShapeDtypeStruct((S, W), bf16), vec, vec],
        compiler_params=_cp((ARB,)))(y, target, f, gate)


def _tap_groups(offsets):
    groups = {}
    for k, o in enumerate(offsets):
        groups.setdefault(o % 8, []).append((k, o - o % 8))
    return sorted(groups.items())


def _tap_sum(buf, w, offsets, tm):
    out = None
    for b, taps in _tap_groups(offsets):
        n = tm + 8 if b else tm
        y = None
        for k, base in taps:
            term = w[k:k + 1, :] * buf[pl.ds(base, n), :]
            y = term if y is None else y + term
        part = y[b:b + tm] if b else y
        out = part if out is None else out + part
    return out


def _tap_wgrad(d, buf, dsh, acc_ref, offsets, tm):
    for b, taps in _tap_groups(offsets):
        if b:
            dsh[pl.ds(0, 8), :] = jnp.zeros((8, dsh.shape[1]), f32)
            dsh[pl.ds(tm, 8), :] = jnp.zeros((8, dsh.shape[1]), f32)
            dsh[pl.ds(b, tm), :] = d
            dd, n = dsh[...], tm + 8
        else:
            dd, n = d, tm
        for k, base in taps:
            acc_ref[pl.ds(k, 1), :] += jnp.sum(dd * buf[pl.ds(base, n), :], axis=0, keepdims=True)


_CONV_OFFSETS = [HALO - (CONV_K - 1) + k for k in range(CONV_K)]
_CONV_OFFSETS_T = [CONV_K - 1 - k for k in range(CONV_K)]


def _conv_core(u_ref, uh_ref, w_ref, b_ref, lg_ref, lb_ref, gbuf, tm, first, cv_ref=None):
    C = u_ref.shape[1] // 2
    u = u_ref[...].astype(f32)
    uh = uh_ref[...].astype(f32)
    gbuf[pl.ds(HALO, tm), :] = u[:, :C] * _sigmoid(u[:, C:])
    halo = uh[:, :C] * _sigmoid(uh[:, C:])
    gbuf[pl.ds(0, HALO), :] = jnp.where(first, 0.0, halo)
    cv = _tap_sum(gbuf, w_ref[...], _CONV_OFFSETS, tm) + b_ref[...] if cv_ref is None else cv_ref[...]
    mu = jnp.mean(cv, axis=-1, keepdims=True)
    xc = cv - mu
    rstd = lax.rsqrt(jnp.mean(xc * xc, axis=-1, keepdims=True) + EPS)
    z = xc * rstd
    ln = z * lg_ref[...] + lb_ref[...]
    return cv, z, rstd, ln


def _halo_prev(tm, hb, w):
    return pl.BlockSpec((hb, w), lambda i: (jnp.maximum(i * (tm // hb) - 1, 0), 0))


def _conv_fwd(u, w, b, lg, lb, name, comms=()):
    S, C2 = u.shape
    C = C2 // 2
    tm = _tile(S, 256, HALO)

    def body(u_ref, uh_ref, w_ref, b_ref, lg_ref, lb_ref, s_ref, cv_ref, gbuf):
        first = pl.program_id(0) == 0
        cv, _, _, ln = _conv_core(u_ref, uh_ref, w_ref, b_ref, lg_ref, lb_ref, gbuf, tm, first)
        s_ref[...] = (ln * _sigmoid(ln)).astype(s_ref.dtype)
        cv_ref[...] = cv

    return _pcall(body, name=name, grid=(S // tm,),
                  in_specs=[pl.BlockSpec((tm, C2), lambda i: (i, 0)), _halo_prev(tm, HALO, C2), _acc_spec(C, 32),
                            _acc_spec(C), _acc_spec(C), _acc_spec(C)],
                  out_specs=[pl.BlockSpec((tm, C), lambda i: (i, 0))] * 2,
                  out_shape=[jax.ShapeDtypeStruct((S, C), bf16), jax.ShapeDtypeStruct((S, C), f32)],
                  args=[u, u, w, b, lg, lb], scratch_shapes=[pltpu.VMEM((tm + HALO, C), f32)], sem=(PAR,), vmem=VMEM_BIG,
                  comms=comms)


def _conv_bwd1(u, cv, ds, w, b, lg, lb, name, comms=()):
    S, C2 = u.shape
    C = C2 // 2
    tm = _tile(S, 256, HALO)

    def body(u_ref, uh_ref, cv_ref, ds_ref, w_ref, b_ref, lg_ref, lb_ref, dcv_ref, dlg_ref, dlb_ref, ddb_ref, ddw_ref, gbuf, dsh):
        i = pl.program_id(0)

        @pl.when(i == 0)
        def _():
            dlg_ref[...] = jnp.zeros_like(dlg_ref)
            dlb_ref[...] = jnp.zeros_like(dlb_ref)
            ddb_ref[...] = jnp.zeros_like(ddb_ref)
            ddw_ref[...] = jnp.zeros_like(ddw_ref)

        _, z, rstd, ln = _conv_core(u_ref, uh_ref, w_ref, b_ref, lg_ref, lb_ref, gbuf, tm, i == 0, cv_ref)
        sg = _sigmoid(ln)
        dln = ds_ref[...].astype(f32) * (sg * (1.0 + ln * (1.0 - sg)))
        dlg_ref[...] += jnp.sum(dln * z, axis=0, keepdims=True)
        dlb_ref[...] += jnp.sum(dln, axis=0, keepdims=True)
        dz = dln * lg_ref[...]
        dcv = rstd * (dz - jnp.mean(dz, axis=-1, keepdims=True) - z * jnp.mean(dz * z, axis=-1, keepdims=True))
        dcv_ref[...] = dcv
        ddb_ref[...] += jnp.sum(dcv, axis=0, keepdims=True)
        _tap_wgrad(dcv, gbuf, dsh, ddw_ref, _CONV_OFFSETS, tm)

    vec = jax.ShapeDtypeStruct((1, C), f32)
    return _pcall(
        body, name=name, grid=(S // tm,),
        in_specs=[pl.BlockSpec((tm, C2), lambda i: (i, 0)), _halo_prev(tm, HALO, C2), pl.BlockSpec((tm, C), lambda i: (i, 0)),
                  pl.BlockSpec((tm, C), lambda i: (i, 0)), _acc_spec(C, 32), _acc_spec(C), _acc_spec(C), _acc_spec(C)],
        out_specs=[pl.BlockSpec((tm, C), lambda i: (i, 0)), _acc_spec(C), _acc_spec(C), _acc_spec(C), _acc_spec(C, 32)],
        out_shape=[jax.ShapeDtypeStruct((S, C), f32), vec, vec, vec, jax.ShapeDtypeStruct((32, C), f32)],
        args=[u, u, cv, ds, w, b, lg, lb], scratch_shapes=[pltpu.VMEM((tm + HALO, C), f32), pltpu.VMEM((tm + 8, C), f32)],
        sem=(ARB,), vmem=VMEM_BIG, comms=comms)


def _conv_bwd2(dcv, u, w, name, comms=()):
    S, C2 = u.shape
    C = C2 // 2
    tm = _tile(S, 256, HALO)
    nt = S // tm
    nhb = S // HALO

    def body(d_ref, dn_ref, u_ref, w_ref, du_ref, db_ref, dbuf):
        i = pl.program_id(0)

        @pl.when(i == 0)
        def _():
            db_ref[...] = jnp.zeros_like(db_ref)

        dbuf[pl.ds(0, tm), :] = d_ref[...]
        dbuf[pl.ds(tm, HALO), :] = jnp.where(i == nt - 1, 0.0, dn_ref[...])
        dglu = _tap_sum(dbuf, w_ref[...], _CONV_OFFSETS_T, tm)
        u = u_ref[...].astype(f32)
        a, gt = u[:, :C], u[:, C:]
        sg = _sigmoid(gt)
        da = dglu * sg
        dgt = dglu * a * sg * (1.0 - sg)
        du_ref[:, :C] = da.astype(du_ref.dtype)
        du_ref[:, C:] = dgt.astype(du_ref.dtype)
        db_ref[:, :C] += jnp.sum(da, axis=0, keepdims=True)
        db_ref[:, C:] += jnp.sum(dgt, axis=0, keepdims=True)

    return _pcall(
        body, name=name, grid=(nt,),
        in_specs=[pl.BlockSpec((tm, C), lambda i: (i, 0)),
                  pl.BlockSpec((HALO, C), lambda i: (jnp.minimum((i + 1) * (tm // HALO), nhb - 1), 0)),
                  pl.BlockSpec((tm, C2), lambda i: (i, 0)), _acc_spec(C, 32)],
        out_specs=[pl.BlockSpec((tm, C2), lambda i: (i, 0)), _acc_spec(C2)],
        out_shape=[jax.ShapeDtypeStruct((S, C2), bf16), jax.ShapeDtypeStruct((1, C2), f32)],
        args=[dcv, dcv, u, w], scratch_shapes=[pltpu.VMEM((tm + HALO, C), f32)], sem=(ARB,), vmem=VMEM_BIG, comms=comms)


def _up_gate(h, w_up, w, b, name, comms=()):
    S, K = h.shape
    F = w.shape[1]
    tm = _tile(S, FFN_TM, 16)
    tn = _tile(F, 512)
    nf = F // tn
    ch = _tile(tn, 512)

    def body(h_ref, wg_ref, wv_ref, w_ref, b_ref, u_ref, a_ref, tail):
        i, j = pl.program_id(0), pl.program_id(1)

        @pl.when(i == 0)
        def _():
            tail[j] = jnp.zeros((8, tn), f32)

        hv = h_ref[...]
        for c in range(tn // ch):
            cs = slice(c * ch, (c + 1) * ch)
            g16 = jnp.dot(hv, wg_ref[:, cs], preferred_element_type=f32).astype(bf16)
            v16 = jnp.dot(hv, wv_ref[:, cs], preferred_element_type=f32).astype(bf16)
            u_ref[0, :, cs] = g16
            u_ref[1, :, cs] = v16
            g = g16.astype(f32)
            ext = jnp.concatenate([tail[j, :, cs], g], axis=0)
            gc = b_ref[:, cs] + w_ref[0:1, cs] * ext[6:6 + tm] + w_ref[1:2, cs] * ext[7:7 + tm] + w_ref[2:3, cs] * g
            a_ref[:, cs] = (gc * _sigmoid(gc) * v16.astype(f32)).astype(a_ref.dtype)
            tail[j, :, cs] = g[tm - 8:tm]

    return _pcall(
        body, name=name, grid=(S // tm, nf),
        in_specs=[pl.BlockSpec((tm, K), lambda i, j: (i, 0)),
                  pl.BlockSpec((None, K, tn), lambda i, j: (0, 0, j)), pl.BlockSpec((None, K, tn), lambda i, j: (0, 0, nf + j)),
                  pl.BlockSpec((8, tn), lambda i, j: (0, j)), pl.BlockSpec((1, tn), lambda i, j: (0, j))],
        out_specs=[pl.BlockSpec((2, tm, tn), lambda i, j: (0, i, j)), pl.BlockSpec((tm, tn), lambda i, j: (i, j))],
        out_shape=[jax.ShapeDtypeStruct((2, S, F), bf16), jax.ShapeDtypeStruct((S, F), bf16)],
        args=[h, w_up, w_up, w, b], scratch_shapes=[pltpu.VMEM((nf, 8, tn), f32)], sem=(ARB, ARB), vmem=VMEM_BIG, comms=comms)


def _dact_gate_bwd(df, w_down, u2, w, b, name, comms=()):
    S, D_ = df.shape
    F = w.shape[1]
    tm = _tile(S, FFN_TM, 16)
    tn = _tile(F, 512)
    nf, nt = F // tn, S // tm
    ch = _tile(tn, 256)
    hb = tm // FHALO

    def body(df_ref, wd_ref, g_ref, gp_ref, v_ref, w_ref, b_ref, du_ref, dw_ref, db_ref, head):
        ii = pl.program_id(1)
        first_tile = ii == nt - 1

        @pl.when(ii == 0)
        def _():
            dw_ref[...] = jnp.zeros_like(dw_ref)
            db_ref[...] = jnp.zeros_like(db_ref)
            head[...] = jnp.zeros_like(head)

        dfv = df_ref[...]
        for c in range(tn // ch):
            cs = slice(c * ch, (c + 1) * ch)
            dact = _dot_nt(dfv, wd_ref[cs, :])
            hist = jnp.where(first_tile, 0.0, gp_ref[:, cs].astype(f32))
            g = jnp.concatenate([hist, g_ref[:, cs].astype(f32)], axis=0)
            taps = [g[FHALO - 2 + k:FHALO - 2 + k + tm] for k in range(FFN_K)]
            gc = b_ref[:, cs] + w_ref[0:1, cs] * taps[0] + w_ref[1:2, cs] * taps[1] + w_ref[2:3, cs] * taps[2]
            sg = _sigmoid(gc)
            dgc = dact * v_ref[:, cs].astype(f32) * (sg * (1.0 + gc * (1.0 - sg)))
            du_ref[1, :, cs] = (dact * (gc * sg)).astype(du_ref.dtype)
            ext = jnp.concatenate([dgc, head[:, cs]], axis=0)
            dgt = w_ref[0:1, cs] * ext[2:2 + tm] + w_ref[1:2, cs] * ext[1:1 + tm] + w_ref[2:3, cs] * dgc
            du_ref[0, :, cs] = dgt.astype(du_ref.dtype)
            db_ref[:, cs] += jnp.sum(dgc, axis=0, keepdims=True)
            for k in range(FFN_K):
                dw_ref[pl.ds(k, 1), cs] += jnp.sum(dgc * taps[k], axis=0, keepdims=True)
            head[:, cs] = dgc[0:8]

    rev = lambda ii: nt - 1 - ii
    return _pcall(
        body, name=name, grid=(nf, nt), comms=comms, sem=(ARB, ARB), vmem=VMEM_BIG,
        args=[df, w_down, u2, u2, u2, w, b],
        in_specs=[pl.BlockSpec((tm, D_), lambda j, ii: (rev(ii), 0)),
                  pl.BlockSpec((None, tn, D_), lambda j, ii: (0, j, 0)),
                  pl.BlockSpec((None, tm, tn), lambda j, ii: (0, rev(ii), j)),
                  pl.BlockSpec((None, FHALO, tn), lambda j, ii: (0, jnp.maximum(rev(ii) * hb - 1, 0), j)),
                  pl.BlockSpec((None, tm, tn), lambda j, ii: (1, rev(ii), j)),
                  pl.BlockSpec((8, tn), lambda j, ii: (0, j)), pl.BlockSpec((1, tn), lambda j, ii: (0, j))],
        out_specs=[pl.BlockSpec((2, tm, tn), lambda j, ii: (0, rev(ii), j)), pl.BlockSpec((8, tn), lambda j, ii: (0, j)),
                   pl.BlockSpec((1, tn), lambda j, ii: (0, j))],
        out_shape=[jax.ShapeDtypeStruct((2, S, F), bf16), jax.ShapeDtypeStruct((8, F), f32), jax.ShapeDtypeStruct((1, F), f32)],
        scratch_shapes=[pltpu.VMEM((8, tn), f32)])


def _rope_tables(pos_col, name):
    S = pos_col.shape[0]
    tm = _tile(S, 512, 8)
    half = ROT // 2
    inv = THETA ** (-np.arange(0, ROT, 2, dtype=np.float32) / ROT)
    lane_freq = np.zeros((1, DH), np.float32)
    lane_freq[0, :half] = inv
    lane_freq[0, half:ROT] = inv
    lane_freq = jnp.asarray(lane_freq)

    def body(p_ref, fr_ref, c_ref, sa_ref, sb_ref):
        ang = p_ref[...].astype(f32) * fr_ref[...]
        lane = lax.broadcasted_iota(jnp.int32, (tm, DH), 1)
        cs, sn = jnp.cos(ang), jnp.sin(ang)
        c_ref[...] = jnp.where(lane < ROT, cs, 1.0)
        sa_ref[...] = jnp.where(lane < half, -sn, 0.0)
        sb_ref[...] = jnp.where((lane >= half) & (lane < ROT), sn, 0.0)

    row = pl.BlockSpec((tm, DH), lambda i: (i, 0))
    shp = jax.ShapeDtypeStruct((S, DH), f32)
    return pl.pallas_call(body, name=name, grid=(S // tm,),
                          in_specs=[pl.BlockSpec((tm, 1), lambda i: (i, 0)), pl.BlockSpec((1, DH), lambda i: (0, 0))],
                          out_specs=[row, row, row], out_shape=[shp, shp, shp], compiler_params=_cp((PAR,)))(pos_col, lane_freq)


def _swap_matrix():
    k = lax.broadcasted_iota(jnp.int32, (DH, DH), 0)
    i = lax.broadcasted_iota(jnp.int32, (DH, DH), 1)
    half = ROT // 2
    hit = ((i < half) & (k == i + half)) | ((i >= half) & (i < ROT) & (k == i - half))
    return jnp.where(hit, 1.0, 0.0).astype(bf16)


def _head_mean(x):
    return jnp.dot(x.astype(bf16), jnp.ones((DH, DH), bf16), preferred_element_type=f32) * (1.0 / DH)


def _rope(n, c, t, swap):
    return n * c + jnp.dot(n.astype(bf16), swap, preferred_element_type=f32) * t


def _rope_t(d, c, t, swap):
    return d * c + jnp.dot((d * t).astype(bf16), swap, preferred_element_type=f32)


def _qk_fwd(raw, g, tabs, width, with_values, name):
    S = raw.shape[0]
    nh = width // DH
    ow = width // NG
    hpg = ow // DH
    tm = _tile(S, 256, 16 * max(DILS))
    vgroups = [gi for gi, r in enumerate(DILS) if r > 1] if with_values else []

    def body(x_ref, g_ref, c_ref, sa_ref, sb_ref, *rest):
        o_refs = rest[:NG]
        v_refs = rest[NG:NG + len(vgroups)]
        scr, vscr = rest[NG + len(vgroups):]
        c, t, swap = c_ref[...], sa_ref[...] + sb_ref[...], _swap_matrix()
        for gi, r in enumerate(DILS):
            heads = range(gi * hpg, (gi + 1) * hpg)
            xs = [x_ref[:, h * DH:(h + 1) * DH].astype(f32) for h in heads]
            rs = [lax.rsqrt(_head_mean(xv * xv) + EPS) for xv in xs]
            ys = [_rope(xv * rv * g_ref[...], c, t, swap) for xv, rv in zip(xs, rs)]
            for hh, y in enumerate(ys):
                if r == 1:
                    o_refs[gi][:, hh * DH:(hh + 1) * DH] = y.astype(bf16)
                else:
                    scr[hh] = y
            if r > 1:
                for hh in range(hpg):
                    for j in range(r):
                        o_refs[gi][:, j * ow + hh * DH:j * ow + (hh + 1) * DH] = scr[hh, pl.ds(j, tm // r, stride=r), :].astype(bf16)
        for vi, gi in enumerate(vgroups):
            _to_view(x_ref[:, width + gi * ow:width + (gi + 1) * ow].astype(f32), v_refs[vi], vscr, DILS[gi], ow, tm)

    win = raw.shape[1] if with_values else width
    row = pl.BlockSpec((tm, win), lambda i: (i, 0))
    tab = pl.BlockSpec((tm, DH), lambda i: (i, 0))
    view = lambda r: pl.BlockSpec((tm // r, r * ow), lambda i: (i, 0))
    vshape = lambda r: jax.ShapeDtypeStruct((S // r, r * ow), bf16)
    outs = pl.pallas_call(
        body, name=name, grid=(S // tm,), in_specs=[row, _acc_spec(DH), tab, tab, tab],
        out_specs=[view(r) for r in DILS] + [view(DILS[gi]) for gi in vgroups],
        out_shape=[vshape(r) for r in DILS] + [vshape(DILS[gi]) for gi in vgroups],
        scratch_shapes=[pltpu.VMEM((hpg, tm, DH), f32), pltpu.VMEM((ow // DH, tm, DH), f32)],
        compiler_params=_cp((PAR,)))(raw, g, *tabs)
    return outs[:NG], outs[NG:]


def _qk_bwd(dparts, raw, g, tabs, width, extra, name):
    S = raw.shape[0]
    nh = width // DH
    ow = width // NG
    hpg = ow // DH
    tm = _tile(S, 256, 16 * max(DILS))
    wout = width + len(extra) * ow

    def body(*refs):
        d_refs = refs[:NG]
        x_ref, g_ref, c_ref, sa_ref, sb_ref = refs[NG:NG + 5]
        e_refs = refs[NG + 5:NG + 5 + len(extra)]
        o_ref, dg_ref, scr, vscr = refs[NG + 5 + len(extra):]
        i = pl.program_id(0)

        @pl.when(i == 0)
        def _():
            dg_ref[...] = jnp.zeros_like(dg_ref)

        c, t, swap = c_ref[...], sa_ref[...] + sb_ref[...], _swap_matrix()
        gv = g_ref[...]
        dg = jnp.zeros((1, DH), f32)
        for gi, r in enumerate(DILS):
            heads = list(range(gi * hpg, (gi + 1) * hpg))
            if r == 1:
                douts = [d_refs[gi][:, hh * DH:(hh + 1) * DH].astype(f32) for hh in range(hpg)]
            else:
                for hh in range(hpg):
                    for j in range(r):
                        scr[hh, pl.ds(j, tm // r, stride=r), :] = d_refs[gi][:, j * ow + hh * DH:j * ow + (hh + 1) * DH].astype(f32)
                douts = [scr[hh] for hh in range(hpg)]
            xs = [x_ref[:, h * DH:(h + 1) * DH].astype(f32) for h in heads]
            rs = [lax.rsqrt(_head_mean(xv * xv) + EPS) for xv in xs]
            xhs = [xv * rv for xv, rv in zip(xs, rs)]
            dns = [_rope_t(d, c, t, swap) for d in douts]
            for dn, xh in zip(dns, xhs):
                dg = dg + jnp.sum(dn * xh, axis=0, keepdims=True)
            dxns = [dn * gv for dn in dns]
            dxs = [rv * (dxn - xh * _head_mean(dxn * xh)) for rv, dxn, xh in zip(rs, dxns, xhs)]
            for h, dx in zip(heads, dxs):
                o_ref[:, h * DH:(h + 1) * DH] = dx.astype(o_ref.dtype)
        for gi, e_ref in enumerate(e_refs):
            o_ref[:, width + gi * ow:width + (gi + 1) * ow] = _from_view(e_ref, vscr, DILS[gi], ow, tm).astype(o_ref.dtype)
        dg_ref[...] += dg

    views = [pl.BlockSpec((tm // r, r * ow), lambda i: (i, 0)) for r in DILS]
    tab = pl.BlockSpec((tm, DH), lambda i: (i, 0))
    return pl.pallas_call(
        body, name=name, grid=(S // tm,),
        in_specs=views + [pl.BlockSpec((tm, width), lambda i: (i, 0)), _acc_spec(DH), tab, tab, tab] + (views if extra else []),
        out_specs=[pl.BlockSpec((tm, wout), lambda i: (i, 0)), _acc_spec(DH)],
        out_shape=[jax.ShapeDtypeStruct((S, wout), bf16), jax.ShapeDtypeStruct((1, DH), f32)],
        scratch_shapes=[pltpu.VMEM((hpg, tm, DH), f32), pltpu.VMEM((ow // DH, tm, DH), f32)],
        compiler_params=_cp((ARB,)))(*dparts, raw, g, *tabs, *extra)


def _dot_nt(a, b):
    return lax.dot_general(a, b, (((1,), (1,)), ((), ())), preferred_element_type=f32)


def _dot_tn(a, b):
    return lax.dot_general(a, b, (((0,), (0,)), ((), ())), preferred_element_type=f32)


def _band_masks():
    qi = lax.broadcasted_iota(jnp.int32, (BLK, BLK), 0)
    ki = lax.broadcasted_iota(jnp.int32, (BLK, BLK), 1)
    return ki <= qi, ki >= qi


def _attn_fwd(qv, kview, vview, vbase, r, name):
    sr = qv.shape[0]
    ow = qv.shape[1] // r
    hpg = ow // DH
    nb = sr // BLK
    scale = 1.0 / math.sqrt(DH)

    def body(q_ref, kc_ref, kp_ref, vc_ref, vp_ref, o_ref, l_ref):
        n = pl.program_id(1)
        m_cur, m_prev = _band_masks()
        m_prev = m_prev & (n > 0)
        hs = [slice(h * DH, (h + 1) * DH) for h in range(hpg)]
        s_c = [jnp.where(m_cur, _dot_nt(q_ref[:, s], kc_ref[:, s]) * scale, NEG) for s in hs]
        s_p = [jnp.where(m_prev, _dot_nt(q_ref[:, s], kp_ref[:, s]) * scale, NEG) for s in hs]
        mx = [jnp.maximum(jnp.max(a, axis=-1, keepdims=True), jnp.max(b, axis=-1, keepdims=True)) for a, b in zip(s_c, s_p)]
        p_c = [jnp.exp(a - m) for a, m in zip(s_c, mx)]
        p_p = [jnp.exp(a - m) for a, m in zip(s_p, mx)]
        den = [jnp.sum(a, axis=-1, keepdims=True) + jnp.sum(b, axis=-1, keepdims=True) for a, b in zip(p_c, p_p)]
        for h, s in enumerate(hs):
            o = jnp.dot(p_c[h].astype(bf16), vc_ref[:, s], preferred_element_type=f32)
            o = o + jnp.dot(p_p[h].astype(bf16), vp_ref[:, s], preferred_element_type=f32)
            o_ref[:, s] = (o / den[h]).astype(o_ref.dtype)
            l_ref[:, s] = jnp.broadcast_to(mx[h] + jnp.log(den[h]), (BLK, DH))

    cur = lambda j, n: (n, j)
    prev = lambda j, n: (jnp.maximum(n - 1, 0), j)
    vcur = lambda j, n: (n, vbase + j)
    vprev = lambda j, n: (jnp.maximum(n - 1, 0), vbase + j)
    blk = lambda f: pl.BlockSpec((BLK, ow), f)
    return pl.pallas_call(
        body, name=name, grid=(r, nb), in_specs=[blk(cur), blk(cur), blk(prev), blk(vcur), blk(vprev)],
        out_specs=[blk(cur), blk(cur)],
        out_shape=[jax.ShapeDtypeStruct((sr, r * ow), bf16), jax.ShapeDtypeStruct((sr, r * ow), f32)],
        compiler_params=_cp((PAR, PAR)))(qv, kview, kview, vview, vview)


def _attn_bwd_q(qv, kview, vview, vbase, do_g, lse, corr, r, name, comms=()):
    sr = qv.shape[0]
    ow = qv.shape[1] // r
    hpg = ow // DH
    nb = sr // BLK
    scale = 1.0 / math.sqrt(DH)

    def body(q_ref, kc_ref, kp_ref, vc_ref, vp_ref, do_ref, l_ref, c_ref, dq_ref):
        n = pl.program_id(1)
        m_cur, m_prev = _band_masks()
        m_prev = m_prev & (n > 0)
        hs = [slice(h * DH, (h + 1) * DH) for h in range(hpg)]
        ls = [slice(h * DH, h * DH + BLK) for h in range(hpg)]
        sides = ((kc_ref, vc_ref, m_cur), (kp_ref, vp_ref, m_prev))
        sc = [[jnp.where(msk, _dot_nt(q_ref[:, s], k_ref[:, s]) * scale, NEG) for s in hs] for k_ref, _, msk in sides]
        dp = [[_dot_nt(do_ref[:, s], v_ref[:, s]) for s in hs] for _, v_ref, _ in sides]
        ds = [[(jnp.exp(sc[i][h] - l_ref[:, ls[h]]) * (dp[i][h] + c_ref[:, ls[h]])).astype(bf16) for h in range(hpg)]
              for i in range(2)]
        for h, s in enumerate(hs):
            dq = jnp.dot(ds[0][h], kc_ref[:, s], preferred_element_type=f32)
            dq = dq + jnp.dot(ds[1][h], kp_ref[:, s], preferred_element_type=f32)
            dq_ref[:, s] = (dq * scale).astype(dq_ref.dtype)

    cur = lambda j, n: (n, j)
    prev = lambda j, n: (jnp.maximum(n - 1, 0), j)
    vcur = lambda j, n: (n, vbase + j)
    vprev = lambda j, n: (jnp.maximum(n - 1, 0), vbase + j)
    blk = lambda f: pl.BlockSpec((BLK, ow), f)
    return _pcall(
        body, name=name, grid=(r, nb),
        in_specs=[blk(cur), blk(cur), blk(prev), blk(vcur), blk(vprev), blk(cur), blk(cur), blk(cur)],
        out_specs=blk(cur), out_shape=jax.ShapeDtypeStruct((sr, r * ow), bf16), sem=(PAR, PAR), comms=comms,
        args=[qv, kview, kview, vview, vview, do_g, lse, corr])


def _attn_bwd_kv(qv, kview, vview, vbase, do_g, lse, corr, r, name):
    sr = qv.shape[0]
    ow = qv.shape[1] // r
    hpg = ow // DH
    nb = sr // BLK
    scale = 1.0 / math.sqrt(DH)

    def body(k_ref, v_ref, qc_ref, qn_ref, doc_ref, don_ref, lc_ref, ln_ref, cc_ref, cn_ref, dk_ref, dv_ref):
        n = pl.program_id(1)
        m_cur, m_prev = _band_masks()
        m_next = m_prev & (n < nb - 1)
        hs = [slice(h * DH, (h + 1) * DH) for h in range(hpg)]
        ls = [slice(h * DH, h * DH + BLK) for h in range(hpg)]
        sides = ((qc_ref, doc_ref, lc_ref, cc_ref, m_cur), (qn_ref, don_ref, ln_ref, cn_ref, m_next))
        sc = [[jnp.where(msk, _dot_nt(q_ref[:, s], k_ref[:, s]) * scale, NEG) for s in hs] for q_ref, _, _, _, msk in sides]
        dp = [[_dot_nt(do_ref[:, s], v_ref[:, s]) for s in hs] for _, do_ref, _, _, _ in sides]
        p = [[jnp.exp(sc[i][h] - sides[i][2][:, ls[h]]) for h in range(hpg)] for i in range(2)]
        ds = [[(p[i][h] * (dp[i][h] + sides[i][3][:, ls[h]])).astype(bf16) for h in range(hpg)] for i in range(2)]
        for h, s in enumerate(hs):
            dv = _dot_tn(p[0][h].astype(bf16), doc_ref[:, s]) + _dot_tn(p[1][h].astype(bf16), don_ref[:, s])
            dk = _dot_tn(ds[0][h], qc_ref[:, s]) + _dot_tn(ds[1][h], qn_ref[:, s])
            dk_ref[:, s] = (dk * scale).astype(dk_ref.dtype)
            dv_ref[:, s] = dv.astype(dv_ref.dtype)

    cur = lambda j, n: (n, j)
    nxt = lambda j, n: (jnp.minimum(n + 1, nb - 1), j)
    vcur = lambda j, n: (n, vbase + j)
    blk = lambda f: pl.BlockSpec((BLK, ow), f)
    shp = jax.ShapeDtypeStruct((sr, r * ow), bf16)
    return pl.pallas_call(
        body, name=name, grid=(r, nb),
        in_specs=[blk(cur), blk(vcur), blk(cur), blk(nxt), blk(cur), blk(nxt), blk(cur), blk(nxt), blk(cur), blk(nxt)],
        out_specs=[blk(cur), blk(cur)], out_shape=[shp, shp],
        compiler_params=_cp((PAR, PAR)))(kview, vview, qv, qv, do_g, do_g, lse, lse, corr, corr)


def _mix_weights(l_refs):
    ls = [l[...] for l in l_refs]
    mx = functools.reduce(jnp.maximum, ls)
    es = [jnp.exp(l - mx) for l in ls]
    den = functools.reduce(lambda a, b: a + b, es)
    return [e / den for e in es]


def _from_view(ref, scr, r, ow, tm):
    if r == 1:
        return ref[...].astype(f32)
    for c in range(ow // DH):
        for j in range(r):
            scr[c, pl.ds(j, tm // r, stride=r), :] = ref[:, j * ow + c * DH:j * ow + (c + 1) * DH].astype(f32)
    return jnp.concatenate([scr[c] for c in range(ow // DH)], axis=1)


def _to_view(val, ref, scr, r, ow, tm):
    if r == 1:
        ref[...] = val.astype(ref.dtype)
        return
    for c in range(ow // DH):
        scr[c] = val[:, c * DH:(c + 1) * DH]
        for j in range(r):
            ref[:, j * ow + c * DH:j * ow + (c + 1) * DH] = scr[c, pl.ds(j, tm // r, stride=r), :].astype(ref.dtype)


def _view_specs(tm, ow):
    return [pl.BlockSpec((tm // r, r * ow), lambda i: (i, 0)) for r in DILS]


def _combine_fwd(os_, lses, name):
    ow = os_[0].shape[1] // DILS[0]
    S = os_[0].shape[0] * DILS[0]
    tm = _tile(S, 256, 16 * max(DILS))

    def body(*refs):
        o_refs, l_refs, out_ref = refs[:NG], refs[NG:2 * NG], refs[2 * NG]
        scr = refs[2 * NG + 1:]
        ov = [_from_view(o_refs[gi], scr[2 * gi], DILS[gi], ow, tm) for gi in range(NG)]
        lv = [_from_view(l_refs[gi], scr[2 * gi + 1], DILS[gi], ow, tm) for gi in range(NG)]
        al = _mix_weights(lv)
        acc = al[0] * ov[0]
        for gi in range(1, NG):
            acc = acc + al[gi] * ov[gi]
        out_ref[...] = acc.astype(out_ref.dtype)

    views = _view_specs(tm, ow)
    return pl.pallas_call(body, name=name, grid=(S // tm,), in_specs=views + views,
                          out_specs=pl.BlockSpec((tm, ow), lambda i: (i, 0)), out_shape=jax.ShapeDtypeStruct((S, ow), bf16),
                          scratch_shapes=[pltpu.VMEM((ow // DH, tm, DH), f32)] * (2 * NG),
                          compiler_params=_cp((PAR,), VMEM_BIG))(*os_, *lses)


def _combine_bwd(do, os_, lses, name, comms=()):
    S, ow = do.shape
    hpg = ow // DH
    tm = _tile(S, 256, 16 * max(DILS))

    def body(*refs):
        do_ref = refs[0]
        o_refs, l_refs = refs[1:1 + NG], refs[1 + NG:1 + 2 * NG]
        dog_refs, c_refs = refs[1 + 2 * NG:1 + 3 * NG], refs[1 + 3 * NG:1 + 4 * NG]
        scr = refs[1 + 4 * NG:]
        ov = [_from_view(o_refs[gi], scr[2 * gi], DILS[gi], ow, tm) for gi in range(NG)]
        lv = [_from_view(l_refs[gi], scr[2 * gi + 1], DILS[gi], ow, tm) for gi in range(NG)]
        al = _mix_weights(lv)
        dov = do_ref[...]
        o = al[0] * ov[0]
        for gi in range(1, NG):
            o = o + al[gi] * ov[gi]
        prod = dov * o
        t = jnp.concatenate(
            [jnp.broadcast_to(jnp.sum(prod[:, h * DH:(h + 1) * DH], axis=-1, keepdims=True), (tm, DH)) for h in range(hpg)],
            axis=1)
        for gi in range(NG):
            _to_view(al[gi] * dov, dog_refs[gi], scr[2 * NG], DILS[gi], ow, tm)
            _to_view(-(al[gi] * t), c_refs[gi], scr[2 * NG], DILS[gi], ow, tm)

    views = _view_specs(tm, ow)
    vshape = lambda dt: [jax.ShapeDtypeStruct((S // r, r * ow), dt) for r in DILS]
    return _pcall(
        body, name=name, grid=(S // tm,), in_specs=[pl.BlockSpec((tm, ow), lambda i: (i, 0))] + views + views,
        out_specs=views + views, out_shape=vshape(bf16) + vshape(f32),
        args=[do, *os_, *lses], scratch_shapes=[pltpu.VMEM((ow // DH, tm, DH), f32)] * (2 * NG + 1), sem=(PAR,), vmem=VMEM_BIG,
        comms=comms)


def _pad_rows(w, rows):
    return jnp.concatenate([w, jnp.zeros((rows - w.shape[0], w.shape[1]), w.dtype)], axis=0)


def kernel(x, c, positions, mod_w, mod_b, norm_mix_g, norm_ffn_g, conv_pw1_w, conv_pw1_b, conv_dw_w, conv_dw_b, conv_ln_g, conv_ln_b, conv_pw2_w, conv_pw2_b, kv_mod_w, kv_mod_b, kv_norm_g, w_kv, k_norm_g, w_q, q_norm_g, w_o, ffn_up_w, ffn_dw_w, ffn_dw_b, ffn_down_w, loss_target, m_mod_w, m_mod_b, m_norm_mix_g, m_norm_ffn_g, m_conv_pw1_w, m_conv_pw1_b, m_conv_dw_w, m_conv_dw_b, m_conv_ln_g, m_conv_ln_b, m_conv_pw2_w, m_conv_pw2_b, m_kv_mod_w, m_kv_mod_b, m_kv_norm_g, m_w_kv, m_k_norm_g, m_w_q, m_q_norm_g, m_w_o, m_ffn_up_w, m_ffn_dw_w, m_ffn_dw_b, m_ffn_down_w, v_mod_w, v_mod_b, v_norm_mix_g, v_norm_ffn_g, v_conv_pw1_w, v_conv_pw1_b, v_conv_dw_w, v_conv_dw_b, v_conv_ln_g, v_conv_ln_b, v_conv_pw2_w, v_conv_pw2_b, v_kv_mod_w, v_kv_mod_b, v_kv_norm_g, v_w_kv, v_k_norm_g, v_w_q, v_q_norm_g, v_w_o, v_ffn_up_w, v_ffn_dw_w, v_ffn_dw_b, v_ffn_down_w):
    S, Dm = x.shape[1], x.shape[2]
    F = ffn_dw_b.shape[1]
    QW = NG * HPG * DH
    OW = HPG * DH
    mx, my, mc = _me()
    me = 4 * mx + 2 * my + mc
    core = jnp.reshape(mc, (1,)).astype(jnp.int32)
    chip = jnp.reshape(2 * mx + my, (1,)).astype(jnp.int32)
    x0 = x.reshape(S, Dm)
    target = loss_target.reshape(S, Dm)

    c_all = _ag_small(c, "ag_c").reshape(NDEV, Dm)
    n_mod = mod_w.shape[2]
    n_kvm = kv_mod_w.shape[1]
    b0 = lax.dynamic_slice(mod_b, (0, me * n_mod), (1, n_mod))
    b1 = lax.dynamic_slice(mod_b, (1, me * n_mod), (1, n_mod))
    bk = lax.dynamic_slice(kv_mod_b.reshape(1, -1), (0, me * n_kvm), (1, n_kvm))
    m_part = jnp.concatenate([_modproj(c_all, mod_w, 0, b0, "modproj0"), _modproj(c_all, mod_w, 1, b1, "modproj1"),
                              _modproj(c_all, kv_mod_w[None], 0, bk, "modproj_kv")], axis=1)
    m_all = _ag_small(m_part, "ag_mod")
    m_mine = lax.dynamic_index_in_dim(m_all, me, axis=1, keepdims=False)
    mod0 = m_mine[:, :n_mod].reshape(6, Dm)
    mod1 = m_mine[:, n_mod:2 * n_mod].reshape(6, Dm)
    modkv = m_mine[:, 2 * n_mod:].reshape(2, Dm)
    row = lambda a, i: a[i:i + 1]

    as3 = lambda w: w if w.ndim == 3 else w[None]
    sh16 = lambda w: as3(w).astype(bf16)
    ag_pw1 = _comm_allgather(sh16(conv_pw1_w), 2)
    ag_pw2 = _comm_allgather(sh16(conv_pw2_w), 1)
    ag_up = [_comm_allgather(sh16(ffn_up_w[l]), 2) for l in range(2)]
    ag_down = [_comm_allgather(sh16(ffn_down_w[l]), 1) for l in range(2)]
    ag_kv = _comm_allgather(sh16(w_kv), 2)
    ag_q = _comm_allgather(sh16(w_q), 2)
    ag_o = _comm_allgather(sh16(w_o), 2)

    sp_flat = jnp.concatenate([conv_pw1_b.reshape(-1), conv_dw_b.reshape(-1), conv_ln_g.reshape(-1), conv_ln_b.reshape(-1),
                               conv_pw2_b.reshape(-1), conv_dw_w.reshape(-1), ffn_dw_w.reshape(-1)])
    sp_rows = -(-sp_flat.shape[0] // 1024) * 8
    sp_flat = jnp.concatenate([sp_flat, jnp.zeros((sp_rows * 128 - sp_flat.shape[0],), f32)]).reshape(sp_rows, 128)
    n1, nd = conv_pw1_b.shape[1], conv_dw_b.shape[1]
    nfw = ffn_dw_w.shape[2]
    sp = _ag_small(sp_flat, "ag_small_params").reshape(NDEV, -1)
    off = 0
    pw1_b = sp[:, off:off + n1].reshape(1, -1); off += n1
    dw_b = sp[:, off:off + nd].reshape(1, -1); off += nd
    ln_g = sp[:, off:off + nd].reshape(1, -1); off += nd
    ln_b = sp[:, off:off + nd].reshape(1, -1); off += nd
    pw2_b = sp[:, off:off + nd].reshape(1, -1); off += nd
    dw_w = jnp.transpose(sp[:, off:off + CONV_K * nd].reshape(NDEV, CONV_K, nd), (1, 0, 2)).reshape(CONV_K, -1); off += CONV_K * nd
    fdw_w = jnp.transpose(sp[:, off:off + 2 * FFN_K * nfw].reshape(NDEV, 2, FFN_K, nfw), (1, 2, 0, 3)).reshape(2, FFN_K, -1)
    dw_w32 = _pad_rows(dw_w, 32)

    tabs = _rope_tables(positions.reshape(S, 1), "rope_tables")

    def with_comms(res, comms):
        return res if comms else (res, [])

    def rs_d2d(dwb):
        return [_comm_rs_sibling(dwb)]

    def rs_add(dwb, couts, tag):
        return _chip_partial(dwb, couts[0][0], core, f"rs_add_{tag}")

    def rs_ici(part):
        return [_comm_rs_chips(part)]

    def ffn_forward(xin, l, modv, w_up, w_down, up_comms, down_comms):
        h2 = _mod_fwd(xin, row(norm_ffn_g, l), row(modv, 3), row(modv, 4), f"ffn{l}_mod")
        fw8 = _pad_rows(fdw_w[l], 8)
        (u2, act), c_up = with_comms(_up_gate(h2, w_up, fw8, row(ffn_dw_b, l), f"ffn{l}_up", comms=up_comms), up_comms)
        if w_down is None:
            w_down, c_up = c_up[0][0], c_up[1:]
        (xout, f), c_down = with_comms(
            _mm_nn(act, w_down, 0, name=f"ffn{l}_down", res=xin, gate=row(modv, 5), tk=F, tn=512, comms=down_comms), down_comms)
        return xout, (h2, u2, act, f, fw8, w_up, w_down), c_up, c_down

    def ffn_backward(dx, df, dgate, xin, l, modv, saved, dact_comms, gate_next):
        h2, u2, act, f, fw8, w_up, w_down = saved
        d_down = _mm_tn(act, df, name=f"ffn{l}_ddown", col_sharded=False)
        (du2, d_fw, d_fb), c1 = _dact_gate_bwd(df, w_down, u2, fw8, row(ffn_dw_b, l), f"ffn{l}_gatebwd",
                                               comms=rs_d2d(d_down) + list(dact_comms))
        part_down, c_dact = rs_add(d_down, c1, f"down{l}"), c1[1:]
        dh2, c2 = _mm_nt(du2, w_up, 0, name=f"ffn{l}_dh", out_dtype=f32, tko=1024, tn=F // 2, comms=rs_ici(part_down))
        d_up = _mm_tn(h2, du2, name=f"ffn{l}_dup", col_sharded=True)
        (dxin, dsh, dsc, dg, *below), c3 = _mod_bwd(dh2, xin, dx, row(norm_ffn_g, l), row(modv, 4), f"ffn{l}_mod_bwd",
                                                    comms=rs_d2d(d_up), gate_next=gate_next)
        part_up = rs_add(d_up, c3, f"up{l}")
        grads = dict(d_fw=d_fw[:FFN_K], d_fb=d_fb, dsh=dsh, dsc=dsc, dgate=dgate, dg=dg,
                     down=(part_down, c2[0][0]), part_up=part_up)
        return dxin, grads, c_dact, below

    h0, c = _mod_fwd(x0, row(norm_mix_g, 0), row(mod0, 0), row(mod0, 1), "l0_mod", comms=[ag_pw1])
    W_pw1 = c[0][0]
    u0, c = _mm_nn(h0, W_pw1, 0, name="l0_pw1", bias=pw1_b, comms=[ag_pw2, ag_q])
    W_pw2, W_q = c[0][0], c[1][0]
    (s0, cv0), c = _conv_fwd(u0, dw_w32, dw_b, ln_g, ln_b, "l0_conv", comms=[ag_up[0], ag_o])
    W_up0, W_o = c[0][0], c[1][0]
    x1, f0 = _mm_nn(s0, W_pw2, 0, name="l0_pw2", bias=pw2_b, res=x0, gate=row(mod0, 2))
    x2, ffn0_saved, c_up, c_down = ffn_forward(x1, 0, mod0, W_up0, None, [ag_down[0], ag_up[1]], [ag_kv])
    W_up1, W_kv = c_up[0][0], c_down[0][0]

    hkv = _mod_fwd(x2, kv_norm_g.reshape(1, -1), row(modkv, 0), row(modkv, 1), "kv_mod")
    kvraw, c = _mm_nn(hkv, W_kv, 0, name="kv_proj", comms=[ag_down[1]])
    W_down1 = c[0][0]
    kg = k_norm_g.reshape(1, -1)
    k_gv, v_dil = _qk_fwd(kvraw, kg, tabs, QW, True, "k_norm_rope")
    dilated = [gi for gi, r in enumerate(DILS) if r > 1]
    v_of = {gi: (kvraw, NG + gi) for gi, r in enumerate(DILS) if r == 1}
    v_of.update({gi: (v_dil[i], 0) for i, gi in enumerate(dilated)})
    h1 = _mod_fwd(x2, row(norm_mix_g, 1), row(mod1, 0), row(mod1, 1), "l1_mod")
    qraw = _mm_nn(h1, W_q, 0, name="q_proj")
    qg = q_norm_g.reshape(1, -1)
    q_gv, _ = _qk_fwd(qraw, qg, tabs, QW, False, "q_norm_rope")
    o_gs, lses = [], []
    for gi, r in enumerate(DILS):
        o_g, lse_g = _attn_fwd(q_gv[gi], k_gv[gi], *v_of[gi], r, f"attn_fwd{gi}")
        o_gs.append(o_g)
        lses.append(lse_g)
    o_mix = _combine_fwd(o_gs, lses, "attn_mix")
    x3, f1 = _mm_nn(o_mix, W_o, 0, name="o_proj", res=x2, gate=row(mod1, 2))
    x4, ffn1_saved, _, _ = ffn_forward(x3, 1, mod1, W_up1, W_down1, (), ())

    dx4, loss_blk, df_f1, dgate_f1, _ = _loss_grad(x4, target, ffn1_saved[3], row(mod1, 5), "loss")
    loss = lax.psum(loss_blk[0, 0], ("x", "y", "c"))

    red = {}
    dx3, gf1, _, (dy1, dgate_m1, _) = ffn_backward(dx4, df_f1, dgate_f1, x3, 1, mod1, ffn1_saved, (), (f1, row(mod1, 2)))
    do = _mm_nt(dy1, W_o, 0, name="o_proj_dx", out_dtype=f32, tko=1024, tn=Dm)
    d_wo = _mm_tn(o_mix, dy1, name="o_proj_dw", col_sharded=True)
    outs, c = _combine_bwd(do, o_gs, lses, "attn_mix_bwd", comms=rs_d2d(d_wo))
    part_wo = rs_add(d_wo, c, "wo")
    do_gs, corrs = outs[:NG], outs[NG:]
    dq_gs, dk_gs, dv_gs = [], [], []
    for gi, r in enumerate(DILS):
        cm = rs_ici(part_wo) if gi == 0 else ()
        dq_g, c = with_comms(_attn_bwd_q(q_gv[gi], k_gv[gi], *v_of[gi], do_gs[gi], lses[gi], corrs[gi], r, f"attn_bwd_q{gi}",
                                         comms=cm), cm)
        if gi == 0:
            red["w_o"] = (part_wo, c[0][0])
        dq_gs.append(dq_g)
        dk_g, dv_g = _attn_bwd_kv(q_gv[gi], k_gv[gi], *v_of[gi], do_gs[gi], lses[gi], corrs[gi], r, f"attn_bwd_kv{gi}")
        dk_gs.append(dk_g)
        dv_gs.append(dv_g)
    dqraw, d_qg = _qk_bwd(dq_gs, qraw, qg, tabs, QW, (), "q_norm_rope_bwd")
    dkvraw, d_kg = _qk_bwd(dk_gs, kvraw, kg, tabs, QW, tuple(dv_gs), "k_norm_rope_bwd")
    dh1 = _mm_nt(dqraw, W_q, 0, name="q_proj_dx", out_dtype=f32, tko=1024, tn=QW)
    d_wq = _mm_tn(h1, dqraw, name="q_proj_dw", col_sharded=True)
    dhkv, c = _mm_nt(dkvraw, W_kv, 0, name="kv_proj_dx", out_dtype=f32, tko=512, tn=2 * QW, comms=rs_d2d(d_wq))
    part_wq = rs_add(d_wq, c, "wq")
    d_wkv, c = _mm_tn(hkv, dkvraw, name="kv_proj_dw", col_sharded=True, comms=rs_ici(gf1["part_up"]))
    red["ffn_up_w1"] = (gf1["part_up"], c[0][0])
    (dx2a, dsh_m1, dsc_m1, dg_mix1), c = _mod_bwd(dh1, x2, dx3, row(norm_mix_g, 1), row(mod1, 1), "l1_mod_bwd",
                                                  comms=rs_ici(part_wq))
    red["w_q"] = (part_wq, c[0][0])
    (dx2, dsh_kv, dsc_kv, dg_kvn, df_f0, dgate_f0, _), c = _mod_bwd(
        dhkv, x2, dx2a, kv_norm_g.reshape(1, -1), row(modkv, 1), "kv_mod_bwd", comms=rs_d2d(d_wkv),
        gate_next=(ffn0_saved[3], row(mod0, 5)))
    part_wkv = rs_add(d_wkv, c, "wkv")

    dx1, gf0, c, (dy0, dgate_m0, d_pw2b) = ffn_backward(dx2, df_f0, dgate_f0, x1, 0, mod0, ffn0_saved, rs_ici(part_wkv),
                                                        (f0, row(mod0, 2)))
    red["w_kv"] = (part_wkv, c[0][0])
    ds0 = _mm_nt(dy0, W_pw2, 0, name="l0_pw2_dx", out_dtype=bf16, tko=1024, tn=Dm)
    d_pw2 = _mm_tn(s0, dy0, name="l0_pw2_dw", col_sharded=False)
    (dcv, d_lng, d_lnb, d_dwb, d_dww), c = _conv_bwd1(u0, cv0, ds0, dw_w32, dw_b, ln_g, ln_b, "l0_conv_bwd1",
                                                      comms=rs_ici(gf0["part_up"]))
    red["ffn_up_w0"] = (gf0["part_up"], c[0][0])
    (du0, d_pw1b), c = _conv_bwd2(dcv, u0, dw_w32, "l0_conv_bwd2", comms=rs_d2d(d_pw2))
    part_pw2 = rs_add(d_pw2, c, "pw2")
    d_pw1 = _mm_tn(h0, du0, name="l0_pw1_dw", col_sharded=True)
    dh0, c = _mm_nt(du0, W_pw1, 0, name="l0_pw1_dx", out_dtype=f32, tko=1024, tn=2 * Dm, comms=rs_ici(part_pw2) + rs_d2d(d_pw1))
    red["conv_pw2_w"] = (part_pw2, c[0][0])
    part_pw1 = rs_add(d_pw1, c[1:], "pw1")
    (grad_x, dsh_m0, dsc_m0, dg_mix0), c = _mod_bwd(dh0, x0, dx1, row(norm_mix_g, 0), row(mod0, 1), "l0_mod_bwd",
                                                    comms=rs_ici(part_pw1))
    red["conv_pw1_w"] = (part_pw1, c[0][0])
    red["ffn_down_w0"], red["ffn_down_w1"] = gf0["down"], gf1["down"]

    dm0 = [dsh_m0, dsc_m0, dgate_m0, gf0["dsh"], gf0["dsc"], gf0["dgate"]]
    dm1 = [dsh_m1, dsc_m1, dgate_m1, gf1["dsh"], gf1["dsc"], gf1["dgate"]]
    pieces = dm0 + dm1 + [dsh_kv, dsc_kv,
                          dg_mix0, dg_mix1, gf0["dg"], gf1["dg"], dg_kvn, d_kg, d_qg, gf0["d_fb"], gf1["d_fb"],
                          d_pw1b, d_dww[:CONV_K], d_dwb, d_lng, d_lnb, d_pw2b, gf0["d_fw"], gf1["d_fw"]]
    flat = jnp.concatenate([p.reshape(-1) for p in pieces])
    n_flat = flat.shape[0]
    n_rows = -(-n_flat // 1024) * 8
    flat = jnp.concatenate([flat, jnp.zeros((n_rows * 128 - n_flat,), f32)]).reshape(n_rows, 128)
    g_all = _ag_small(flat, "ag_small_grads")
    g_sum = _sum8(g_all, "sum_small_grads").reshape(-1)
    n_dm = 2 * 6 * Dm + 2 * Dm
    dm_all = g_all.reshape(NDEV, -1)[:, :n_dm]

    take_pos = [0]

    def take(shape):
        n = int(np.prod(shape))
        out = g_sum[take_pos[0]:take_pos[0] + n].reshape(shape)
        take_pos[0] += n
        return out

    g_mod_b = take((2, 6 * Dm))
    g_kv_mod_b = take((2 * Dm,))
    g_norm_mix0, g_norm_mix1 = take((Dm,)), take((Dm,))
    g_norm_ffn0, g_norm_ffn1 = take((Dm,)), take((Dm,))
    g_kv_norm = take((Dm,))
    g_k_norm = take((DH,))
    g_q_norm = take((1, DH))
    g_ffn_dw_b = take((2, F))
    shard = lambda full, n, axis: lax.dynamic_slice_in_dim(full, me * n, n, axis)
    g_pw1_b = shard(take((1, 2 * Dm)), n1, 1)
    g_dw_w = shard(take((1, CONV_K, Dm)), nd, 2)
    g_dw_b = shard(take((1, Dm)), nd, 1)
    g_ln_g = shard(take((1, Dm)), nd, 1)
    g_ln_b = shard(take((1, Dm)), nd, 1)
    g_pw2_b = shard(take((1, Dm)), nd, 1)
    g_ffn_dw_w = shard(jnp.stack([take((FFN_K, F)), take((FFN_K, F))]), nfw, 2)
    g_norm_mix = jnp.stack([g_norm_mix0, g_norm_mix1])
    g_norm_ffn = jnp.stack([g_norm_ffn0, g_norm_ffn1])

    small = [("mod_b", mod_b, m_mod_b, v_mod_b, g_mod_b), ("norm_mix_g", norm_mix_g, m_norm_mix_g, v_norm_mix_g, g_norm_mix),
             ("norm_ffn_g", norm_ffn_g, m_norm_ffn_g, v_norm_ffn_g, g_norm_ffn),
             ("conv_pw1_b", conv_pw1_b, m_conv_pw1_b, v_conv_pw1_b, g_pw1_b),
             ("conv_dw_w", conv_dw_w, m_conv_dw_w, v_conv_dw_w, g_dw_w), ("conv_dw_b", conv_dw_b, m_conv_dw_b, v_conv_dw_b, g_dw_b),
             ("conv_ln_g", conv_ln_g, m_conv_ln_g, v_conv_ln_g, g_ln_g), ("conv_ln_b", conv_ln_b, m_conv_ln_b, v_conv_ln_b, g_ln_b),
             ("conv_pw2_b", conv_pw2_b, m_conv_pw2_b, v_conv_pw2_b, g_pw2_b),
             ("kv_mod_b", kv_mod_b, m_kv_mod_b, v_kv_mod_b, g_kv_mod_b), ("kv_norm_g", kv_norm_g, m_kv_norm_g, v_kv_norm_g, g_kv_norm),
             ("k_norm_g", k_norm_g, m_k_norm_g, v_k_norm_g, g_k_norm), ("q_norm_g", q_norm_g, m_q_norm_g, v_q_norm_g, g_q_norm),
             ("ffn_dw_w", ffn_dw_w, m_ffn_dw_w, v_ffn_dw_w, g_ffn_dw_w), ("ffn_dw_b", ffn_dw_b, m_ffn_dw_b, v_ffn_dw_b, g_ffn_dw_b)]
    n_small = sum(int(np.prod(s[1].shape)) for s in small)
    rows_small = -(-n_small // 1024) * 8

    def pack(idx):
        fl = jnp.concatenate([s[idx].reshape(-1) for s in small])
        return jnp.concatenate([fl, jnp.ones((rows_small * 128 - n_small,), f32)]).reshape(rows_small, 128)

    sd, sm, sv = _adamw_plain(pack(1), pack(2), pack(3), pack(4), "adamw_small")
    res = {}
    pos = 0
    for name, w, _, _, g in small:
        n = int(np.prod(w.shape))
        cut = lambda a: a.reshape(-1)[pos:pos + n].reshape(w.shape)
        res[name] = (g.reshape(w.shape), cut(sd), cut(sm), cut(sv))
        pos += n

    c_all_t = jnp.transpose(c_all)

    def mod_update(w, m, v, l, prev, dm_cols, tag):
        g = _modgrad(c_all_t, dm_cols, f"modgrad_{tag}")
        return _adamw_reduced(w, m, v, g, None, l, prev, f"adamw_{tag}")

    prev = None
    for l in (1, 0):
        cols = lax.dynamic_slice_in_dim(dm_all[:, l * 6 * Dm:(l + 1) * 6 * Dm], me * n_mod, n_mod, 1)
        prev = mod_update(mod_w, m_mod_w, v_mod_w, l, prev, cols, f"mod_w{l}")
    res["mod_w"] = tuple(prev)
    cols = lax.dynamic_slice_in_dim(dm_all[:, 12 * Dm:], me * n_kvm, n_kvm, 1)
    res["kv_mod_w"] = tuple(o.reshape(kv_mod_w.shape)
                            for o in mod_update(kv_mod_w[None], m_kv_mod_w[None], v_kv_mod_w[None], 0, None, cols, "kv_mod_w"))

    def mine(part):
        return lax.dynamic_index_in_dim(part, chip[0], 0, keepdims=False)

    def big(key, w, m, v, l, prev, tag, comms=()):
        part, r2 = red[key]
        return _adamw_reduced(as3(w), as3(m), as3(v), mine(part), r2, l, prev, f"adamw_{tag}", comms=comms)

    up1 = big("ffn_up_w1", ffn_up_w, m_ffn_up_w, v_ffn_up_w, 1, None, "up1")
    res["ffn_up_w"] = tuple(big("ffn_up_w0", ffn_up_w, m_ffn_up_w, v_ffn_up_w, 0, up1, "up0"))
    down1 = big("ffn_down_w1", ffn_down_w, m_ffn_down_w, v_ffn_down_w, 1, None, "down1")
    res["ffn_down_w"] = tuple(big("ffn_down_w0", ffn_down_w, m_ffn_down_w, v_ffn_down_w, 0, down1, "down0"))
    for key, w, m, v in (("conv_pw1_w", conv_pw1_w, m_conv_pw1_w, v_conv_pw1_w), ("conv_pw2_w", conv_pw2_w, m_conv_pw2_w, v_conv_pw2_w),
                         ("w_kv", w_kv, m_w_kv, v_w_kv), ("w_q", w_q, m_w_q, v_w_q), ("w_o", w_o, m_w_o, v_w_o)):
        res[key] = tuple(o.reshape(w.shape) for o in big(key, w, m, v, 0, None, key))

    order = ["mod_w", "mod_b", "norm_mix_g", "norm_ffn_g", "conv_pw1_w", "conv_pw1_b", "conv_dw_w", "conv_dw_b", "conv_ln_g",
             "conv_ln_b", "conv_pw2_w", "conv_pw2_b", "kv_mod_w", "kv_mod_b", "kv_norm_g", "w_kv", "k_norm_g", "w_q", "q_norm_g",
             "w_o", "ffn_up_w", "ffn_dw_w", "ffn_dw_b", "ffn_down_w"]
    out = [loss, grad_x.reshape(x.shape)]
    for i in range(4):
        out += [res[n][i] for n in order]
    return tuple(out)
```
